```python
import jax, jax.numpy as jnp
from jax import lax
import numpy as np

D_MODEL = 2048
BATCH = 16
SEQ = 2048
DEPTH = 1

CHUNK = 64
D_MIX = D_MODEL
D_CONV = D_MIX // 2
CONV_K = 31
D_SSD = D_MIX - D_CONV
SSD_HEAD_DIM = 64
SSD_HEADS = D_SSD // SSD_HEAD_DIM
SSD_GROUPS = 4
SSD_STATE = 128
SSD_CONV_K = 4
D_XBC = D_SSD + 2 * SSD_GROUPS * SSD_STATE
D_IN = 2 * D_CONV + D_SSD + D_XBC + SSD_HEADS
D_FF = -(-(8 * D_MODEL) // (3 * 256)) * 256
EPS = 1e-6

kernel_name = "hybrid_conformer_conv_mamba2_ssd_block"


def rmsnorm(x, g):
    xf = x.astype(jnp.float32)
    y = xf * lax.rsqrt(jnp.mean(xf * xf, axis=-1, keepdims=True) + EPS)
    return (y * g.astype(jnp.float32)).astype(x.dtype)


def layernorm(x, g, b):
    xf = x.astype(jnp.float32)
    mu = jnp.mean(xf, axis=-1, keepdims=True)
    var = jnp.mean(jnp.square(xf - mu), axis=-1, keepdims=True)
    y = (xf - mu) * lax.rsqrt(var + EPS)
    return (y * g.astype(jnp.float32) + b.astype(jnp.float32)).astype(x.dtype)


def causal_depthwise_conv(u, w, b):
    k, c = w.shape
    out = lax.conv_general_dilated(
        u, w[:, None, :].astype(u.dtype), window_strides=(1,), padding=[(k - 1, 0)],
        dimension_numbers=("NWC", "WIO", "NWC"), feature_group_count=c)
    return out + b.astype(u.dtype)


def segsum(a):
    cs = jnp.cumsum(a, axis=-1)
    diff = cs[..., :, None] - cs[..., None, :]
    l = a.shape[-1]
    mask = jnp.tril(jnp.ones((l, l), dtype=bool))
    return jnp.where(mask, diff, -jnp.inf)


def ssd_chunked(xh, dt, a_head, bm, cm):
    bsz, s, h, p = xh.shape
    g, n = bm.shape[2], bm.shape[3]
    r = h // g
    nc = s // CHUNK
    xd = (xh * dt[..., None]).reshape(bsz, nc, CHUNK, g, r, p)
    a = (dt * a_head).reshape(bsz, nc, CHUNK, g, r).transpose(0, 3, 4, 1, 2)
    bc = bm.reshape(bsz, nc, CHUNK, g, n)
    cc = cm.reshape(bsz, nc, CHUNK, g, n)
    a_cs = jnp.cumsum(a, axis=-1)
    decay_mat = jnp.exp(segsum(a))
    scores = jnp.einsum("bclgn,bcsgn->bgcls", cc, bc)
    y_diag = jnp.einsum("bgrcls,bcsgrp->bclgrp", scores[:, :, None] * decay_mat, xd)
    decay_states = jnp.exp(a_cs[..., -1:] - a_cs)
    states = jnp.einsum("bclgn,bgrcl,bclgrp->bcgrpn", bc, decay_states, xd)
    chunk_decay = jnp.exp(a_cs[..., -1])

    def step(hstate, inp):
        st, dec = inp
        return hstate * dec[..., None, None] + st, hstate

    h0 = jnp.zeros((bsz, g, r, p, n), dtype=xd.dtype)
    _, prev = lax.scan(step, h0, (states.transpose(1, 0, 2, 3, 4, 5),
                                  chunk_decay.transpose(3, 0, 1, 2)))
    prev = prev.transpose(1, 0, 2, 3, 4, 5)
    y_off = jnp.einsum("bclgn,bcgrpn,bgrcl->bclgrp", cc, prev, jnp.exp(a_cs))
    return (y_diag + y_off).reshape(bsz, s, h, p)


def _fwd_setup_inputs(seed: int = 0) -> dict:
    key = jax.random.key(seed)
    ks = jax.random.split(key, 24)
    f32 = jnp.float32

    def nrm(k, shape, scale):
        return jax.random.normal(k, shape, f32) * scale

    def gain(k, shape):
        return 1.0 + 0.05 * jax.random.normal(k, shape, f32)

    L = DEPTH
    x = jax.random.normal(ks[0], (BATCH, SEQ, D_MODEL), f32)
    norm_mix_pre = gain(ks[1], (L, D_MODEL))
    w_in = nrm(ks[2], (L, D_MODEL, D_IN), D_MODEL ** -0.5)
    conv_dw_w = nrm(ks[3], (L, CONV_K, D_CONV), CONV_K ** -0.5)
    conv_dw_b = nrm(ks[4], (L, D_CONV), 0.02)
    conv_ln_g = gain(ks[5], (L, D_CONV))
    conv_ln_b = nrm(ks[6], (L, D_CONV), 0.02)
    ssd_conv_w = nrm(ks[7], (L, SSD_CONV_K, D_XBC), SSD_CONV_K ** -0.5)
    ssd_conv_b = nrm(ks[8], (L, D_XBC), 0.02)
    dt0 = jnp.exp(jax.random.uniform(ks[9], (L, SSD_HEADS), f32,
                                     minval=np.log(1e-3), maxval=np.log(1e-1)))
    ssd_dt_bias = dt0 + jnp.log(-jnp.expm1(-dt0))
    ssd_a_log = jnp.log(jax.random.uniform(ks[10], (L, SSD_HEADS), f32, minval=1.0, maxval=16.0))
    ssd_d = gain(ks[11], (L, SSD_HEADS))
    ssd_norm_w = gain(ks[12], (L, D_SSD))
    w_out = nrm(ks[13], (L, D_MIX, D_MODEL), D_MIX ** -0.5)
    norm_mix_post = gain(ks[14], (L, D_MODEL))
    norm_ffn_pre = gain(ks[15], (L, D_MODEL))
    w_gate = nrm(ks[16], (L, D_MODEL, D_FF), D_MODEL ** -0.5)
    w_up = nrm(ks[17], (L, D_MODEL, D_FF), D_MODEL ** -0.5)
    w_down = nrm(ks[18], (L, D_FF, D_MODEL), D_FF ** -0.5)
    norm_ffn_post = gain(ks[19], (L, D_MODEL))
    return {"x": x, "norm_mix_pre": norm_mix_pre, "w_in": w_in,
            "conv_dw_w": conv_dw_w, "conv_dw_b": conv_dw_b,
            "conv_ln_g": conv_ln_g, "conv_ln_b": conv_ln_b,
            "ssd_conv_w": ssd_conv_w, "ssd_conv_b": ssd_conv_b,
            "ssd_dt_bias": ssd_dt_bias, "ssd_a_log": ssd_a_log, "ssd_d": ssd_d,
            "ssd_norm_w": ssd_norm_w, "w_out": w_out, "norm_mix_post": norm_mix_post,
            "norm_ffn_pre": norm_ffn_pre, "w_gate": w_gate, "w_up": w_up,
            "w_down": w_down, "norm_ffn_post": norm_ffn_post}


def _fwd_reference(x, norm_mix_pre, w_in, conv_dw_w, conv_dw_b, conv_ln_g, conv_ln_b,
              ssd_conv_w, ssd_conv_b, ssd_dt_bias, ssd_a_log, ssd_d, ssd_norm_w,
              w_out, norm_mix_post, norm_ffn_pre, w_gate, w_up, w_down, norm_ffn_post):
    bsz, s, _ = x.shape
    f32 = jnp.float32
    for l in range(DEPTH):
        h = rmsnorm(x, norm_mix_pre[l])
        proj = h @ w_in[l]
        c_a, c_g, z, xbc, dt_raw = jnp.split(
            proj, [D_CONV, 2 * D_CONV, 2 * D_CONV + D_SSD, 2 * D_CONV + D_SSD + D_XBC], axis=-1)

        u = c_a * jax.nn.sigmoid(c_g)
        u = causal_depthwise_conv(u, conv_dw_w[l], conv_dw_b[l])
        u = jax.nn.silu(layernorm(u, conv_ln_g[l], conv_ln_b[l]))

        xbc = jax.nn.silu(causal_depthwise_conv(xbc, ssd_conv_w[l], ssd_conv_b[l]))
        xs, bm, cm = jnp.split(xbc, [D_SSD, D_SSD + SSD_GROUPS * SSD_STATE], axis=-1)
        xh = xs.astype(f32).reshape(bsz, s, SSD_HEADS, SSD_HEAD_DIM)
        bm = bm.astype(f32).reshape(bsz, s, SSD_GROUPS, SSD_STATE)
        cm = cm.astype(f32).reshape(bsz, s, SSD_GROUPS, SSD_STATE)
        dt = jax.nn.softplus(dt_raw.astype(f32) + ssd_dt_bias[l].astype(f32))
        a_head = -jnp.exp(ssd_a_log[l].astype(f32))
        y = ssd_chunked(xh, dt, a_head, bm, cm)
        y = y + xh * ssd_d[l].astype(f32)[:, None]
        y = y.reshape(bsz, s, D_SSD) * jax.nn.silu(z.astype(f32))
        yg = y.reshape(bsz, s, SSD_GROUPS, D_SSD // SSD_GROUPS)
        yg = yg * lax.rsqrt(jnp.mean(yg * yg, axis=-1, keepdims=True) + EPS)
        y = (yg.reshape(bsz, s, D_SSD) * ssd_norm_w[l].astype(f32)).astype(x.dtype)

        mix = jnp.concatenate([u, y], axis=-1) @ w_out[l]
        x = x + rmsnorm(mix, norm_mix_post[l])

        h = rmsnorm(x, norm_ffn_pre[l])
        f = (jax.nn.silu(h @ w_gate[l]) * (h @ w_up[l])) @ w_down[l]
        x = x + rmsnorm(f, norm_ffn_post[l])
    return x


import jax as _jax
import jax.numpy as _jnp

TWIN_FORMAT = 'train_step'
FWD_PARAMS = ['x', 'norm_mix_pre', 'w_in', 'conv_dw_w', 'conv_dw_b', 'conv_ln_g', 'conv_ln_b', 'ssd_conv_w', 'ssd_conv_b', 'ssd_dt_bias', 'ssd_a_log', 'ssd_d', 'ssd_norm_w', 'w_out', 'norm_mix_post', 'norm_ffn_pre', 'w_gate', 'w_up', 'w_down', 'norm_ffn_post']
TWIN_WEIGHTS = ['norm_mix_pre', 'w_in', 'conv_dw_w', 'conv_dw_b', 'conv_ln_g', 'conv_ln_b', 'ssd_conv_w', 'ssd_conv_b', 'ssd_dt_bias', 'ssd_a_log', 'ssd_d', 'ssd_norm_w', 'w_out', 'norm_mix_post', 'norm_ffn_pre', 'w_gate', 'w_up', 'w_down', 'norm_ffn_post']
TWIN_DIFF_INPUT = 'x'
TWIN_INPUTS = ['x', 'norm_mix_pre', 'w_in', 'conv_dw_w', 'conv_dw_b', 'conv_ln_g', 'conv_ln_b', 'ssd_conv_w', 'ssd_conv_b', 'ssd_dt_bias', 'ssd_a_log', 'ssd_d', 'ssd_norm_w', 'w_out', 'norm_mix_post', 'norm_ffn_pre', 'w_gate', 'w_up', 'w_down', 'norm_ffn_post', 'loss_target', 'm_norm_mix_pre', 'm_w_in', 'm_conv_dw_w', 'm_conv_dw_b', 'm_conv_ln_g', 'm_conv_ln_b', 'm_ssd_conv_w', 'm_ssd_conv_b', 'm_ssd_dt_bias', 'm_ssd_a_log', 'm_ssd_d', 'm_ssd_norm_w', 'm_w_out', 'm_norm_mix_post', 'm_norm_ffn_pre', 'm_w_gate', 'm_w_up', 'm_w_down', 'm_norm_ffn_post', 'v_norm_mix_pre', 'v_w_in', 'v_conv_dw_w', 'v_conv_dw_b', 'v_conv_ln_g', 'v_conv_ln_b', 'v_ssd_conv_w', 'v_ssd_conv_b', 'v_ssd_dt_bias', 'v_ssd_a_log', 'v_ssd_d', 'v_ssd_norm_w', 'v_w_out', 'v_norm_mix_post', 'v_norm_ffn_pre', 'v_w_gate', 'v_w_up', 'v_w_down', 'v_norm_ffn_post']
TWIN_OUTPUTS = ['loss', 'grad_x', 'grad_norm_mix_pre', 'grad_w_in', 'grad_conv_dw_w', 'grad_conv_dw_b', 'grad_conv_ln_g', 'grad_conv_ln_b', 'grad_ssd_conv_w', 'grad_ssd_conv_b', 'grad_ssd_dt_bias', 'grad_ssd_a_log', 'grad_ssd_d', 'grad_ssd_norm_w', 'grad_w_out', 'grad_norm_mix_post', 'grad_norm_ffn_pre', 'grad_w_gate', 'grad_w_up', 'grad_w_down', 'grad_norm_ffn_post', 'delta_norm_mix_pre', 'delta_w_in', 'delta_conv_dw_w', 'delta_conv_dw_b', 'delta_conv_ln_g', 'delta_conv_ln_b', 'delta_ssd_conv_w', 'delta_ssd_conv_b', 'delta_ssd_dt_bias', 'delta_ssd_a_log', 'delta_ssd_d', 'delta_ssd_norm_w', 'delta_w_out', 'delta_norm_mix_post', 'delta_norm_ffn_pre', 'delta_w_gate', 'delta_w_up', 'delta_w_down', 'delta_norm_ffn_post', 'new_m_norm_mix_pre', 'new_m_w_in', 'new_m_conv_dw_w', 'new_m_conv_dw_b', 'new_m_conv_ln_g', 'new_m_conv_ln_b', 'new_m_ssd_conv_w', 'new_m_ssd_conv_b', 'new_m_ssd_dt_bias', 'new_m_ssd_a_log', 'new_m_ssd_d', 'new_m_ssd_norm_w', 'new_m_w_out', 'new_m_norm_mix_post', 'new_m_norm_ffn_pre', 'new_m_w_gate', 'new_m_w_up', 'new_m_w_down', 'new_m_norm_ffn_post', 'new_v_norm_mix_pre', 'new_v_w_in', 'new_v_conv_dw_w', 'new_v_conv_dw_b', 'new_v_conv_ln_g', 'new_v_conv_ln_b', 'new_v_ssd_conv_w', 'new_v_ssd_conv_b', 'new_v_ssd_dt_bias', 'new_v_ssd_a_log', 'new_v_ssd_d', 'new_v_ssd_norm_w', 'new_v_w_out', 'new_v_norm_mix_post', 'new_v_norm_ffn_pre', 'new_v_w_gate', 'new_v_w_up', 'new_v_w_down', 'new_v_norm_ffn_post']
TWIN_LEAF_KINDS = {'loss': 'loss', 'grad_x': 'grad_x', 'grad_norm_mix_pre': 'grad_w', 'grad_w_in': 'grad_w', 'grad_conv_dw_w': 'grad_w', 'grad_conv_dw_b': 'grad_w', 'grad_conv_ln_g': 'grad_w', 'grad_conv_ln_b': 'grad_w', 'grad_ssd_conv_w': 'grad_w', 'grad_ssd_conv_b': 'grad_w', 'grad_ssd_dt_bias': 'grad_w', 'grad_ssd_a_log': 'grad_w', 'grad_ssd_d': 'grad_w', 'grad_ssd_norm_w': 'grad_w', 'grad_w_out': 'grad_w', 'grad_norm_mix_post': 'grad_w', 'grad_norm_ffn_pre': 'grad_w', 'grad_w_gate': 'grad_w', 'grad_w_up': 'grad_w', 'grad_w_down': 'grad_w', 'grad_norm_ffn_post': 'grad_w', 'delta_norm_mix_pre': 'delta_w', 'delta_w_in': 'delta_w', 'delta_conv_dw_w': 'delta_w', 'delta_conv_dw_b': 'delta_w', 'delta_conv_ln_g': 'delta_w', 'delta_conv_ln_b': 'delta_w', 'delta_ssd_conv_w': 'delta_w', 'delta_ssd_conv_b': 'delta_w', 'delta_ssd_dt_bias': 'delta_w', 'delta_ssd_a_log': 'delta_w', 'delta_ssd_d': 'delta_w', 'delta_ssd_norm_w': 'delta_w', 'delta_w_out': 'delta_w', 'delta_norm_mix_post': 'delta_w', 'delta_norm_ffn_pre': 'delta_w', 'delta_w_gate': 'delta_w', 'delta_w_up': 'delta_w', 'delta_w_down': 'delta_w', 'delta_norm_ffn_post': 'delta_w', 'new_m_norm_mix_pre': 'new_m', 'new_m_w_in': 'new_m', 'new_m_conv_dw_w': 'new_m', 'new_m_conv_dw_b': 'new_m', 'new_m_conv_ln_g': 'new_m', 'new_m_conv_ln_b': 'new_m', 'new_m_ssd_conv_w': 'new_m', 'new_m_ssd_conv_b': 'new_m', 'new_m_ssd_dt_bias': 'new_m', 'new_m_ssd_a_log': 'new_m', 'new_m_ssd_d': 'new_m', 'new_m_ssd_norm_w': 'new_m', 'new_m_w_out': 'new_m', 'new_m_norm_mix_post': 'new_m', 'new_m_norm_ffn_pre': 'new_m', 'new_m_w_gate': 'new_m', 'new_m_w_up': 'new_m', 'new_m_w_down': 'new_m', 'new_m_norm_ffn_post': 'new_m', 'new_v_norm_mix_pre': 'new_v', 'new_v_w_in': 'new_v', 'new_v_conv_dw_w': 'new_v', 'new_v_conv_dw_b': 'new_v', 'new_v_conv_ln_g': 'new_v', 'new_v_conv_ln_b': 'new_v', 'new_v_ssd_conv_w': 'new_v', 'new_v_ssd_conv_b': 'new_v', 'new_v_ssd_dt_bias': 'new_v', 'new_v_ssd_a_log': 'new_v', 'new_v_ssd_d': 'new_v', 'new_v_ssd_norm_w': 'new_v', 'new_v_w_out': 'new_v', 'new_v_norm_mix_post': 'new_v', 'new_v_norm_ffn_pre': 'new_v', 'new_v_w_gate': 'new_v', 'new_v_w_up': 'new_v', 'new_v_w_down': 'new_v', 'new_v_norm_ffn_post': 'new_v'}


def _forward(args):
    return _fwd_reference(*[args[k] for k in FWD_PARAMS])


def _output_shape():
    out = _jax.eval_shape(lambda: _forward(_fwd_setup_inputs(0)))
    return out.shape, out.dtype

N_MICROBATCH = 1
ADAM_LR = 0.001
ADAM_B1 = 0.9
ADAM_B2 = 0.999
ADAM_EPS = 1e-08
ADAM_WD = 0.01
ADAM_STEP = 10
PER_EXAMPLE_BATCH_AXIS = {'x': 0, 'loss_target': 0}
SHARED_INPUTS = []
_WEIGHT_DTYPES = {'norm_mix_pre': _jnp.float32, 'w_in': _jnp.float32, 'conv_dw_w': _jnp.float32, 'conv_dw_b': _jnp.float32, 'conv_ln_g': _jnp.float32, 'conv_ln_b': _jnp.float32, 'ssd_conv_w': _jnp.float32, 'ssd_conv_b': _jnp.float32, 'ssd_dt_bias': _jnp.float32, 'ssd_a_log': _jnp.float32, 'ssd_d': _jnp.float32, 'ssd_norm_w': _jnp.float32, 'w_out': _jnp.float32, 'norm_mix_post': _jnp.float32, 'norm_ffn_pre': _jnp.float32, 'w_gate': _jnp.float32, 'w_up': _jnp.float32, 'w_down': _jnp.float32, 'norm_ffn_post': _jnp.float32}
MOMENT_SCALE = {'norm_mix_pre': 3.056253e-01, 'w_in': 2.015926e-01, 'conv_dw_w': 1.791116e-01, 'conv_dw_b': 1.330630e+00, 'conv_ln_g': 5.152353e-01, 'conv_ln_b': 7.920282e-01, 'ssd_conv_w': 2.414129e-01, 'ssd_conv_b': 5.070352e-01, 'ssd_dt_bias': 6.088881e-01, 'ssd_a_log': 2.814476e+00, 'ssd_d': 2.293229e+00, 'ssd_norm_w': 4.393263e-01, 'w_out': 3.511160e-01, 'norm_mix_post': 1.607824e+01, 'norm_ffn_pre': 3.274024e-01, 'w_gate': 1.124506e-01, 'w_up': 1.581292e-01, 'w_down': 2.615234e-01, 'norm_ffn_post': 1.604224e+01}


def _to_microbatches(a, axis):
    t = _jnp.moveaxis(a, axis, 0)
    t = t.reshape((N_MICROBATCH, t.shape[0] // N_MICROBATCH) + t.shape[1:])
    return _jnp.moveaxis(t, 1, axis + 1)


def setup_inputs(seed: int = 0) -> dict:
    inp = _fwd_setup_inputs(seed)
    key = _jax.random.fold_in(_jax.random.key(seed), 7919)
    shape, _ = _output_shape()
    out = dict(inp)
    out["loss_target"] = _jax.random.normal(_jax.random.fold_in(key, 0), shape, _jnp.float32)
    for i, name in enumerate(TWIN_WEIGHTS):
        w = inp[name].astype(_jnp.float32)
        if MOMENT_SCALE is None:
            s = _jnp.sqrt(_jnp.mean(_jnp.square(w)) + 1e-30)
        else:
            s = MOMENT_SCALE[name]
        km, kv = _jax.random.split(_jax.random.fold_in(key, i + 1))
        out[name] = w
        out["m_" + name] = s * _jax.random.normal(km, w.shape, _jnp.float32)
        out["v_" + name] = (s * s) * _jax.random.uniform(kv, w.shape, _jnp.float32, 0.5, 1.5)
    if N_MICROBATCH > 1:
        for name, axis in PER_EXAMPLE_BATCH_AXIS.items():
            out[name] = _to_microbatches(out[name], axis)
    return {'x': out['x'], 'norm_mix_pre': out['norm_mix_pre'], 'w_in': out['w_in'], 'conv_dw_w': out['conv_dw_w'], 'conv_dw_b': out['conv_dw_b'], 'conv_ln_g': out['conv_ln_g'], 'conv_ln_b': out['conv_ln_b'], 'ssd_conv_w': out['ssd_conv_w'], 'ssd_conv_b': out['ssd_conv_b'], 'ssd_dt_bias': out['ssd_dt_bias'], 'ssd_a_log': out['ssd_a_log'], 'ssd_d': out['ssd_d'], 'ssd_norm_w': out['ssd_norm_w'], 'w_out': out['w_out'], 'norm_mix_post': out['norm_mix_post'], 'norm_ffn_pre': out['norm_ffn_pre'], 'w_gate': out['w_gate'], 'w_up': out['w_up'], 'w_down': out['w_down'], 'norm_ffn_post': out['norm_ffn_post'], 'loss_target': out['loss_target'], 'm_norm_mix_pre': out['m_norm_mix_pre'], 'm_w_in': out['m_w_in'], 'm_conv_dw_w': out['m_conv_dw_w'], 'm_conv_dw_b': out['m_conv_dw_b'], 'm_conv_ln_g': out['m_conv_ln_g'], 'm_conv_ln_b': out['m_conv_ln_b'], 'm_ssd_conv_w': out['m_ssd_conv_w'], 'm_ssd_conv_b': out['m_ssd_conv_b'], 'm_ssd_dt_bias': out['m_ssd_dt_bias'], 'm_ssd_a_log': out['m_ssd_a_log'], 'm_ssd_d': out['m_ssd_d'], 'm_ssd_norm_w': out['m_ssd_norm_w'], 'm_w_out': out['m_w_out'], 'm_norm_mix_post': out['m_norm_mix_post'], 'm_norm_ffn_pre': out['m_norm_ffn_pre'], 'm_w_gate': out['m_w_gate'], 'm_w_up': out['m_w_up'], 'm_w_down': out['m_w_down'], 'm_norm_ffn_post': out['m_norm_ffn_post'], 'v_norm_mix_pre': out['v_norm_mix_pre'], 'v_w_in': out['v_w_in'], 'v_conv_dw_w': out['v_conv_dw_w'], 'v_conv_dw_b': out['v_conv_dw_b'], 'v_conv_ln_g': out['v_conv_ln_g'], 'v_conv_ln_b': out['v_conv_ln_b'], 'v_ssd_conv_w': out['v_ssd_conv_w'], 'v_ssd_conv_b': out['v_ssd_conv_b'], 'v_ssd_dt_bias': out['v_ssd_dt_bias'], 'v_ssd_a_log': out['v_ssd_a_log'], 'v_ssd_d': out['v_ssd_d'], 'v_ssd_norm_w': out['v_ssd_norm_w'], 'v_w_out': out['v_w_out'], 'v_norm_mix_post': out['v_norm_mix_post'], 'v_norm_ffn_pre': out['v_norm_ffn_pre'], 'v_w_gate': out['v_w_gate'], 'v_w_up': out['v_w_up'], 'v_w_down': out['v_w_down'], 'v_norm_ffn_post': out['v_norm_ffn_post']}


def _loss(weights, diff, rest, loss_target):
    with _jax.named_scope("forward"):
        args = {**rest, TWIN_DIFF_INPUT: diff, **{k: w.astype(_WEIGHT_DTYPES[k]) for k, w in weights.items()}}
        y = _forward(args)
    with _jax.named_scope("loss_head"):
        err = _jnp.square(y.astype(_jnp.float32) - loss_target)
        return 0.5 * _jnp.sum(_jnp.mean(err, axis=-1)) if err.ndim else 0.5 * err


def _adamw(w, g, m, v):
    m = ADAM_B1 * m + (1.0 - ADAM_B1) * g
    v = ADAM_B2 * v + (1.0 - ADAM_B2) * _jnp.square(g)
    m_hat = m / (1.0 - ADAM_B1 ** ADAM_STEP)
    v_hat = v / (1.0 - ADAM_B2 ** ADAM_STEP)
    delta = -ADAM_LR * (m_hat / (_jnp.sqrt(v_hat) + ADAM_EPS) + ADAM_WD * w)
    return delta, m, v


def reference(x, norm_mix_pre, w_in, conv_dw_w, conv_dw_b, conv_ln_g, conv_ln_b, ssd_conv_w, ssd_conv_b, ssd_dt_bias, ssd_a_log, ssd_d, ssd_norm_w, w_out, norm_mix_post, norm_ffn_pre, w_gate, w_up, w_down, norm_ffn_post, loss_target, m_norm_mix_pre, m_w_in, m_conv_dw_w, m_conv_dw_b, m_conv_ln_g, m_conv_ln_b, m_ssd_conv_w, m_ssd_conv_b, m_ssd_dt_bias, m_ssd_a_log, m_ssd_d, m_ssd_norm_w, m_w_out, m_norm_mix_post, m_norm_ffn_pre, m_w_gate, m_w_up, m_w_down, m_norm_ffn_post, v_norm_mix_pre, v_w_in, v_conv_dw_w, v_conv_dw_b, v_conv_ln_g, v_conv_ln_b, v_ssd_conv_w, v_ssd_conv_b, v_ssd_dt_bias, v_ssd_a_log, v_ssd_d, v_ssd_norm_w, v_w_out, v_norm_mix_post, v_norm_ffn_pre, v_w_gate, v_w_up, v_w_down, v_norm_ffn_post):
    given = dict(x=x, norm_mix_pre=norm_mix_pre, w_in=w_in, conv_dw_w=conv_dw_w, conv_dw_b=conv_dw_b, conv_ln_g=conv_ln_g, conv_ln_b=conv_ln_b, ssd_conv_w=ssd_conv_w, ssd_conv_b=ssd_conv_b, ssd_dt_bias=ssd_dt_bias, ssd_a_log=ssd_a_log, ssd_d=ssd_d, ssd_norm_w=ssd_norm_w, w_out=w_out, norm_mix_post=norm_mix_post, norm_ffn_pre=norm_ffn_pre, w_gate=w_gate, w_up=w_up, w_down=w_down, norm_ffn_post=norm_ffn_post, loss_target=loss_target, m_norm_mix_pre=m_norm_mix_pre, m_w_in=m_w_in, m_conv_dw_w=m_conv_dw_w, m_conv_dw_b=m_conv_dw_b, m_conv_ln_g=m_conv_ln_g, m_conv_ln_b=m_conv_ln_b, m_ssd_conv_w=m_ssd_conv_w, m_ssd_conv_b=m_ssd_conv_b, m_ssd_dt_bias=m_ssd_dt_bias, m_ssd_a_log=m_ssd_a_log, m_ssd_d=m_ssd_d, m_ssd_norm_w=m_ssd_norm_w, m_w_out=m_w_out, m_norm_mix_post=m_norm_mix_post, m_norm_ffn_pre=m_norm_ffn_pre, m_w_gate=m_w_gate, m_w_up=m_w_up, m_w_down=m_w_down, m_norm_ffn_post=m_norm_ffn_post, v_norm_mix_pre=v_norm_mix_pre, v_w_in=v_w_in, v_conv_dw_w=v_conv_dw_w, v_conv_dw_b=v_conv_dw_b, v_conv_ln_g=v_conv_ln_g, v_conv_ln_b=v_conv_ln_b, v_ssd_conv_w=v_ssd_conv_w, v_ssd_conv_b=v_ssd_conv_b, v_ssd_dt_bias=v_ssd_dt_bias, v_ssd_a_log=v_ssd_a_log, v_ssd_d=v_ssd_d, v_ssd_norm_w=v_ssd_norm_w, v_w_out=v_w_out, v_norm_mix_post=v_norm_mix_post, v_norm_ffn_pre=v_norm_ffn_pre, v_w_gate=v_w_gate, v_w_up=v_w_up, v_w_down=v_w_down, v_norm_ffn_post=v_norm_ffn_post)
    weights = {n: given[n] for n in TWIN_WEIGHTS}
    shared = {n: given[n] for n in SHARED_INPUTS}
    per_example = {n: given[n] for n in ['x']}
    grad_fn = _jax.value_and_grad(_loss, argnums=(0, 1))

    def one_microbatch(ex, loss_target):
        ex = dict(ex)
        diff = ex.pop(TWIN_DIFF_INPUT)
        return grad_fn(weights, diff, {**shared, **ex}, loss_target)

    if N_MICROBATCH == 1:
        loss, (grad_w, grad_x) = one_microbatch(per_example, given["loss_target"])
    else:
        def body(carry, xs):
            loss_sum, grad_sum = carry
            l_k, (gw_k, gx_k) = one_microbatch(xs[0], xs[1])
            with _jax.named_scope("update"):
                return (loss_sum + l_k, _jax.tree.map(_jnp.add, grad_sum, gw_k)), gx_k

        init = (_jnp.zeros((), _jnp.float32), _jax.tree.map(_jnp.zeros_like, weights))
        (loss, grad_w), grad_x = _jax.lax.scan(body, init, (per_example, given["loss_target"]))
    with _jax.named_scope("update"):
        delta_w, new_m, new_v = {}, {}, {}
        for n in TWIN_WEIGHTS:
            delta_w[n], new_m[n], new_v[n] = _adamw(weights[n], grad_w[n], given["m_" + n], given["v_" + n])
    return (loss, grad_x, *[grad_w[n] for n in TWIN_WEIGHTS], *[delta_w[n] for n in TWIN_WEIGHTS],
            *[new_m[n] for n in TWIN_WEIGHTS], *[new_v[n] for n in TWIN_WEIGHTS])
```

```python
import functools

import jax
import jax.numpy as jnp
from jax import lax
from jax.experimental import pallas as pl
from jax.experimental.pallas import tpu as pltpu

F32 = jnp.float32
BF16 = jnp.bfloat16
EPS = 1e-6

D_MODEL = 2048
D_CONV = 1024
D_SSD = 1024
D_XBC = 2048
HEADS = 16
HEAD_DIM = 64
GROUPS = 4
STATE = 128
CONV_K = 31
SSD_CONV_K = 4
D_FF = 5632
D_MAIN = 2 * D_CONV + D_SSD + D_XBC
D_IN = D_MAIN + HEADS
N_CHIPS = 4
LANES = 128
CHUNK = 128
PAIRS = HEADS // 2

ADAM_LR = 0.001
ADAM_B1 = 0.9
ADAM_B2 = 0.999
ADAM_EPS = 1e-08
ADAM_WD = 0.01
ADAM_STEP = 10

MESH_AXES = ("x", "y", "c")
VMEM_LIMIT = 56 * 1024 * 1024


def _sig(v):
    return 1.0 / (1.0 + jnp.exp(-v))


def _cparams(sem, vmem=VMEM_LIMIT):
    return pltpu.CompilerParams(dimension_semantics=sem, vmem_limit_bytes=vmem)


_DIMS = {"nn": ((1,), (0,)), "nt": ((1,), (1,)), "tn": ((0,), (0,))}


def _matmul(pairs, *, mode, out_dtype, tm, tn, tk, name, slot_out=False):
    a0, b0 = pairs[0]
    if mode == "nn":
        (m, k), n = a0.shape, b0.shape[1]
    elif mode == "nt":
        (m, k), n = a0.shape, b0.shape[0]
    else:
        (k, m), n = a0.shape, b0.shape[1]
    tm, tn, tk = min(tm, m), min(tn, n), min(tk, k)
    assert m % tm == 0 and n % tn == 0 and k % tk == 0, (name, m, n, k, tm, tn, tk)
    nk = k // tk
    npairs = len(pairs)
    dims = (_DIMS[mode], ((), ()))

    def body(*refs):
        ins, o_ref = refs[: 2 * npairs], refs[2 * npairs]
        part = None
        for p in range(npairs):
            d = lax.dot_general(ins[2 * p][...], ins[2 * p + 1][...], dims, preferred_element_type=F32)
            part = d if part is None else part + d
        if nk == 1:
            o_ref[...] = part.astype(out_dtype)
            return
        acc = refs[2 * npairs + 1]
        kk = pl.program_id(2)

        @pl.when(kk == 0)
        def _():
            acc[...] = part

        @pl.when(kk > 0)
        def _():
            acc[...] += part

        @pl.when(kk == nk - 1)
        def _():
            o_ref[...] = acc[...].astype(out_dtype)

    if mode == "nn":
        a_spec = pl.BlockSpec((tm, tk), lambda i, j, kk: (i, kk))
        b_spec = pl.BlockSpec((tk, tn), lambda i, j, kk: (kk, j))
    elif mode == "nt":
        a_spec = pl.BlockSpec((tm, tk), lambda i, j, kk: (i, kk))
        b_spec = pl.BlockSpec((tn, tk), lambda i, j, kk: (j, kk))
    else:
        a_spec = pl.BlockSpec((tk, tm), lambda i, j, kk: (kk, i))
        b_spec = pl.BlockSpec((tk, tn), lambda i, j, kk: (kk, j))
    if slot_out:
        out_shape = jax.ShapeDtypeStruct((n // tn, m, tn), out_dtype)
        out_spec = pl.BlockSpec((None, tm, tn), lambda i, j, kk: (j, i, 0))
    else:
        out_shape = jax.ShapeDtypeStruct((m, n), out_dtype)
        out_spec = pl.BlockSpec((tm, tn), lambda i, j, kk: (i, j))
    flat = [t for ab in pairs for t in ab]
    return pl.pallas_call(
        body,
        grid=(m // tm, n // tn, nk),
        in_specs=[a_spec, b_spec] * npairs,
        out_specs=out_spec,
        out_shape=out_shape,
        scratch_shapes=[] if nk == 1 else [pltpu.VMEM((tm, tn), F32)],
        compiler_params=_cparams(("parallel", "parallel", "arbitrary")),
        name=name,
    )(*flat)


def _ffn_up(h2, wg, wu, *, tm, tn):
    t, k = h2.shape
    n = wg.shape[1]
    tm = min(tm, t)
    assert t % tm == 0 and n % tn == 0, (t, n, tm, tn)

    def body(h_ref, wg_ref, wu_ref, g_ref, u_ref, a_ref):
        hv = h_ref[...]
        g = jnp.dot(hv, wg_ref[...], preferred_element_type=F32)
        u = jnp.dot(hv, wu_ref[...], preferred_element_type=F32)
        g_ref[...] = g.astype(BF16)
        u_ref[...] = u.astype(BF16)
        a_ref[...] = (g * _sig(g) * u).astype(BF16)

    o = jax.ShapeDtypeStruct((t, n), BF16)
    ospec = pl.BlockSpec((tm, tn), lambda i, j: (i, j))
    return pl.pallas_call(
        body,
        grid=(t // tm, n // tn),
        in_specs=[pl.BlockSpec((tm, k), lambda i, j: (i, 0)), pl.BlockSpec((k, tn), lambda i, j: (0, j)),
                  pl.BlockSpec((k, tn), lambda i, j: (0, j))],
        out_specs=[ospec, ospec, ospec],
        out_shape=[o, o, o],
        compiler_params=_cparams(("parallel", "parallel")),
        name="ffn_up",
    )(h2, wg, wu)


def _ffn_bwd_act(df, wd, gt, up, *, tm, tn):
    t, k = df.shape
    n = wd.shape[0]
    tm = min(tm, t)
    assert t % tm == 0 and n % tn == 0, (t, n, tm, tn)

    def body(df_ref, wd_ref, g_ref, u_ref, dg_ref, du_ref):
        da = lax.dot_general(df_ref[...], wd_ref[...], (_DIMS["nt"], ((), ())), preferred_element_type=F32)
        g = g_ref[...].astype(F32)
        u = u_ref[...].astype(F32)
        s = _sig(g)
        dg_ref[...] = (da * u * s * (1.0 + g * (1.0 - s))).astype(BF16)
        du_ref[...] = (da * g * s).astype(BF16)

    o = jax.ShapeDtypeStruct((t, n), BF16)
    blk = pl.BlockSpec((tm, tn), lambda i, j: (i, j))
    return pl.pallas_call(
        body,
        grid=(t // tm, n // tn),
        in_specs=[pl.BlockSpec((tm, k), lambda i, j: (i, 0)), pl.BlockSpec((tn, k), lambda i, j: (j, 0)), blk, blk],
        out_specs=[blk, blk],
        out_shape=[o, o],
        compiler_params=_cparams(("parallel", "parallel")),
        name="ffn_bwd_act",
    )(df, wd, gt, up)


ROW_TILE = 256


def _rms_fwd(xv, g, *, res=None, out_dtype, name):
    t, d = xv.shape
    has_res = res is not None

    def body(*refs):
        x_ref, g_ref = refs[0], refs[1]
        o_ref = refs[-1]
        v = x_ref[...]
        r = lax.rsqrt(jnp.mean(v * v, axis=-1, keepdims=True) + EPS)
        y = v * r * g_ref[...]
        if has_res:
            y = refs[2][...] + y
        o_ref[...] = y.astype(out_dtype)

    row = pl.BlockSpec((ROW_TILE, d), lambda i: (i, 0))
    vec = pl.BlockSpec((1, d), lambda i: (0, 0))
    return pl.pallas_call(
        body,
        grid=(t // ROW_TILE,),
        in_specs=[row, vec] + ([row] if has_res else []),
        out_specs=row,
        out_shape=jax.ShapeDtypeStruct((t, d), out_dtype),
        compiler_params=_cparams(("parallel",)),
        name=name,
    )(*([xv, g] + ([res] if has_res else [])))


def _rms_bwd(dys, xv, g, *, addend=None, out_dtype, name):
    t, d = xv.shape
    ndy = len(dys)
    has_add = addend is not None

    def body(*refs):
        dy_refs = refs[:ndy]
        x_ref, g_ref = refs[ndy], refs[ndy + 1]
        dx_ref, dg_ref = refs[-2], refs[-1]
        dy = dy_refs[0][...].astype(F32)
        for rr in dy_refs[1:]:
            dy = dy + rr[...].astype(F32)
        v = x_ref[...]
        r = lax.rsqrt(jnp.mean(v * v, axis=-1, keepdims=True) + EPS)
        xh = v * r
        gdy = dy * g_ref[...]
        dx = r * (gdy - xh * jnp.mean(gdy * xh, axis=-1, keepdims=True))
        if has_add:
            dx = dx + refs[ndy + 2][...]
        dx_ref[...] = dx.astype(out_dtype)

        @pl.when(pl.program_id(0) == 0)
        def _():
            dg_ref[...] = jnp.zeros_like(dg_ref)

        dg_ref[...] += jnp.sum(dy * xh, axis=0, keepdims=True)

    row = pl.BlockSpec((ROW_TILE, d), lambda i: (i, 0))
    vec = pl.BlockSpec((1, d), lambda i: (0, 0))
    return pl.pallas_call(
        body,
        grid=(t // ROW_TILE,),
        in_specs=[row] * ndy + [row, vec] + ([row] if has_add else []),
        out_specs=[row, vec],
        out_shape=[jax.ShapeDtypeStruct((t, d), out_dtype), jax.ShapeDtypeStruct((1, d), F32)],
        compiler_params=_cparams(("arbitrary",)),
        name=name,
    )(*(list(dys) + [xv, g] + ([addend] if has_add else [])))


def _ffn_post_loss(f, x1, tgt, g):
    t, d = f.shape

    def body(f_ref, x1_ref, t_ref, g_ref, dx2_ref, df_ref, loss_ref, dg_ref):
        v = f_ref[...]
        gv = g_ref[...]
        r = lax.rsqrt(jnp.mean(v * v, axis=-1, keepdims=True) + EPS)
        fh = v * r
        e = x1_ref[...] + fh * gv - t_ref[...]
        dx2 = e * (1.0 / d)
        dx2_ref[...] = dx2
        gdy = dx2 * gv
        df_ref[...] = (r * (gdy - fh * jnp.mean(gdy * fh, axis=-1, keepdims=True))).astype(BF16)

        @pl.when(pl.program_id(0) == 0)
        def _():
            dg_ref[...] = jnp.zeros_like(dg_ref)
            loss_ref[...] = jnp.zeros_like(loss_ref)

        dg_ref[...] += jnp.sum(dx2 * fh, axis=0, keepdims=True)
        per_tok = jnp.mean(e * e, axis=-1, keepdims=True)
        loss_ref[...] += 0.5 * jnp.sum(per_tok, axis=0, keepdims=True)

    row = pl.BlockSpec((ROW_TILE, d), lambda i: (i, 0))
    vec = pl.BlockSpec((1, d), lambda i: (0, 0))
    return pl.pallas_call(
        body,
        grid=(t // ROW_TILE,),
        in_specs=[row, row, row, vec],
        out_specs=[row, row, pl.BlockSpec((1, 1), lambda i: (0, 0)), vec],
        out_shape=[jax.ShapeDtypeStruct((t, d), F32), jax.ShapeDtypeStruct((t, d), BF16),
                   jax.ShapeDtypeStruct((1, 1), F32), jax.ShapeDtypeStruct((1, d), F32)],
        compiler_params=_cparams(("arbitrary",)),
        name="ffn_post_loss",
    )(f, x1, tgt, g)


CONV_ROWS = 256
TAP_ROWS = 64
HALO31 = 32
HALO4 = 8


def _sum8(v):
    return jnp.sum(v.reshape(v.shape[0] // 8, 8, v.shape[1]), axis=0)


def _conv_branch_fwd(proj, cw, cb, lg, lb, *, nb, seq):
    ts, c, halo = CONV_ROWS, D_CONV, HALO31
    ns = seq // ts
    base = halo - CONV_K + 1

    def body(ca_ref, cg_ref, w_ref, b_ref, lg_ref, lb_ref, u1_ref, u_ref, ubuf):
        i = pl.program_id(1)

        @pl.when(i == 0)
        def _():
            ubuf[0:halo, :] = jnp.zeros((halo, c), F32)

        @pl.when(i > 0)
        def _():
            ubuf[0:halo, :] = ubuf[ts:ts + halo, :]

        ubuf[halo:halo + ts, :] = ca_ref[...] * _sig(cg_ref[...])

        def lane_tile(j, carry):
            ln = pl.ds(pl.multiple_of(j * LANES, LANES), LANES)
            for r in range(ts // TAP_ROWS):
                acc = jnp.broadcast_to(b_ref[:, ln], (TAP_ROWS, LANES))
                for k in range(CONV_K):
                    acc = acc + w_ref[pl.ds(k, 1), ln] * ubuf[pl.ds(r * TAP_ROWS + base + k, TAP_ROWS), ln]
                u1_ref[pl.ds(r * TAP_ROWS, TAP_ROWS), ln] = acc
            return carry

        lax.fori_loop(0, c // LANES, lane_tile, 0)
        v = u1_ref[...]
        mu = jnp.mean(v, axis=-1, keepdims=True)
        dv = v - mu
        xh = dv * lax.rsqrt(jnp.mean(dv * dv, axis=-1, keepdims=True) + EPS)
        u2 = xh * lg_ref[...] + lb_ref[...]
        u_ref[...] = (u2 * _sig(u2)).astype(BF16)

    t = nb * seq
    row = lambda col: pl.BlockSpec((ts, c), lambda b, i: (b * ns + i, col))
    vec = pl.BlockSpec((1, c), lambda b, i: (0, 0))
    return pl.pallas_call(
        body,
        grid=(nb, ns),
        in_specs=[row(0), row(1), pl.BlockSpec((32, c), lambda b, i: (0, 0)), vec, vec, vec],
        out_specs=[row(0), row(0)],
        out_shape=[jax.ShapeDtypeStruct((t, c), F32), jax.ShapeDtypeStruct((t, c), BF16)],
        scratch_shapes=[pltpu.VMEM((halo + ts, c), F32)],
        compiler_params=_cparams(("parallel", "arbitrary")),
        name="conv_branch_fwd",
    )(proj, proj, cw, cb, lg, lb)


def _conv_branch_bwd(duy, u1, proj, cw, lg, lb, *, nb, seq):
    ts, c, halo = CONV_ROWS, D_CONV, HALO31
    ns = seq // ts
    base = halo - CONV_K + 1
    hb = ts // halo

    def body(du_ref, u1_ref, ca_ref, cg_ref, cah_ref, cgh_ref, w_ref, lg_ref, lb_ref,
             dcacg_ref, dw_ref, db_ref, dlg_ref, dlb_ref,
             ubuf, dbuf, du0buf, dwacc, dbacc, dlgacc, dlbacc):
        b, i = pl.program_id(0), pl.program_id(1)
        rc = ns - 1 - i

        @pl.when(jnp.logical_and(b == 0, i == 0))
        def _():
            dwacc[...] = jnp.zeros_like(dwacc)
            dbacc[...] = jnp.zeros_like(dbacc)
            dlgacc[...] = jnp.zeros_like(dlgacc)
            dlbacc[...] = jnp.zeros_like(dlbacc)

        @pl.when(i == 0)
        def _():
            dbuf[ts:ts + halo, :] = jnp.zeros((halo, c), F32)

        @pl.when(i > 0)
        def _():
            dbuf[ts:ts + halo, :] = dbuf[0:halo, :]

        v = u1_ref[...]
        mu = jnp.mean(v, axis=-1, keepdims=True)
        dv = v - mu
        rstd = lax.rsqrt(jnp.mean(dv * dv, axis=-1, keepdims=True) + EPS)
        xh = dv * rstd
        lgv = lg_ref[...]
        u2 = xh * lgv + lb_ref[...]
        s2 = _sig(u2)
        du2 = du_ref[...] * (s2 * (1.0 + u2 * (1.0 - s2)))
        dlgacc[...] += jnp.sum(du2 * xh, axis=0, keepdims=True)
        dlbacc[...] += jnp.sum(du2, axis=0, keepdims=True)
        gd = du2 * lgv
        du1 = rstd * (gd - jnp.mean(gd, axis=-1, keepdims=True) - xh * jnp.mean(gd * xh, axis=-1, keepdims=True))
        dbacc[...] += jnp.sum(du1, axis=0, keepdims=True)
        dbuf[0:ts, :] = du1

        @pl.when(rc == 0)
        def _():
            ubuf[0:halo, :] = jnp.zeros((halo, c), F32)

        @pl.when(rc > 0)
        def _():
            ubuf[0:halo, :] = cah_ref[...] * _sig(cgh_ref[...])

        cav = ca_ref[...]
        sg = _sig(cg_ref[...])
        ubuf[halo:halo + ts, :] = cav * sg

        def lane_tile(j, carry):
            ln = pl.ds(pl.multiple_of(j * LANES, LANES), LANES)
            for r in range(ts // TAP_ROWS):
                r0 = r * TAP_ROWS
                d1 = dbuf[pl.ds(r0, TAP_ROWS), ln]
                acc = jnp.zeros((TAP_ROWS, LANES), F32)
                for k in range(CONV_K):
                    acc = acc + w_ref[pl.ds(k, 1), ln] * dbuf[pl.ds(r0 + CONV_K - 1 - k, TAP_ROWS), ln]
                    dwacc[pl.ds(k * 8, 8), ln] += _sum8(d1 * ubuf[pl.ds(r0 + base + k, TAP_ROWS), ln])
                du0buf[pl.ds(r0, TAP_ROWS), ln] = acc
            return carry

        lax.fori_loop(0, c // LANES, lane_tile, 0)
        du0 = du0buf[...]
        dcacg_ref[:, 0:c] = (du0 * sg).astype(BF16)
        dcacg_ref[:, c:2 * c] = (du0 * cav * sg * (1.0 - sg)).astype(BF16)

        @pl.when(jnp.logical_and(b == nb - 1, i == ns - 1))
        def _():
            for k in range(CONV_K):
                dw_ref[pl.ds(k, 1), :] = jnp.sum(dwacc[pl.ds(k * 8, 8), :], axis=0, keepdims=True)
            dw_ref[pl.ds(CONV_K, 1), :] = jnp.zeros((1, c), F32)
            db_ref[...] = dbacc[...]
            dlg_ref[...] = dlgacc[...]
            dlb_ref[...] = dlbacc[...]

    t = nb * seq
    rowblk = lambda b, i: b * ns + (ns - 1 - i)
    row = lambda col: pl.BlockSpec((ts, c), lambda b, i: (rowblk(b, i), col))
    hrow = lambda col: pl.BlockSpec((halo, c), lambda b, i: (jnp.maximum(rowblk(b, i) * hb - 1, 0), col))
    vec = pl.BlockSpec((1, c), lambda b, i: (0, 0))
    wspec = pl.BlockSpec((32, c), lambda b, i: (0, 0))
    return pl.pallas_call(
        body,
        grid=(nb, ns),
        in_specs=[row(0), row(0), row(0), row(1), hrow(0), hrow(1), wspec, vec, vec],
        out_specs=[pl.BlockSpec((ts, 2 * c), lambda b, i: (rowblk(b, i), 0)), wspec, vec, vec, vec],
        out_shape=[jax.ShapeDtypeStruct((t, 2 * c), BF16), jax.ShapeDtypeStruct((32, c), F32),
                   jax.ShapeDtypeStruct((1, c), F32), jax.ShapeDtypeStruct((1, c), F32), jax.ShapeDtypeStruct((1, c), F32)],
        scratch_shapes=[pltpu.VMEM((halo + ts, c), F32), pltpu.VMEM((ts + halo, c), F32), pltpu.VMEM((ts, c), F32),
                        pltpu.VMEM((CONV_K * 8, c), F32), pltpu.VMEM((1, c), F32), pltpu.VMEM((1, c), F32),
                        pltpu.VMEM((1, c), F32)],
        compiler_params=_cparams(("arbitrary", "arbitrary")),
        name="conv_branch_bwd",
    )(duy, u1, proj, proj, proj, proj, cw, lg, lb)


XBC_COL0 = (2 * D_CONV + D_SSD) // 1024


def _ssd_pre_fwd(proj, sw, sb, *, nb, seq):
    ts, c, halo = CONV_ROWS, 1024, HALO4
    ns = seq // ts
    base = halo - SSD_CONV_K + 1

    def body(x_ref, w_ref, b_ref, o_ref, xbuf):
        i = pl.program_id(2)

        @pl.when(i == 0)
        def _():
            xbuf[0:halo, :] = jnp.zeros((halo, c), F32)

        @pl.when(i > 0)
        def _():
            xbuf[0:halo, :] = xbuf[ts:ts + halo, :]

        xbuf[halo:halo + ts, :] = x_ref[...]

        def lane_tile(j, carry):
            ln = pl.ds(pl.multiple_of(j * LANES, LANES), LANES)
            for r in range(ts // TAP_ROWS):
                acc = jnp.broadcast_to(b_ref[:, ln], (TAP_ROWS, LANES))
                for k in range(SSD_CONV_K):
                    acc = acc + w_ref[pl.ds(k, 1), ln] * xbuf[pl.ds(r * TAP_ROWS + base + k, TAP_ROWS), ln]
                o_ref[pl.ds(r * TAP_ROWS, TAP_ROWS), ln] = acc * _sig(acc)
            return carry

        lax.fori_loop(0, c // LANES, lane_tile, 0)

    t = nb * seq
    return pl.pallas_call(
        body,
        grid=(2, nb, ns),
        in_specs=[pl.BlockSpec((ts, c), lambda j, b, i: (b * ns + i, XBC_COL0 + j)),
                  pl.BlockSpec((8, c), lambda j, b, i: (0, j)), pl.BlockSpec((1, c), lambda j, b, i: (0, j))],
        out_specs=pl.BlockSpec((ts, c), lambda j, b, i: (b * ns + i, j)),
        out_shape=jax.ShapeDtypeStruct((t, D_XBC), F32),
        scratch_shapes=[pltpu.VMEM((halo + ts, c), F32)],
        compiler_params=_cparams(("parallel", "parallel", "arbitrary")),
        name="ssd_pre_fwd",
    )(proj, sw, sb)


def _ssd_pre_bwd(dxs, proj, sw, sb, *, nb, seq):
    ts, c, halo = CONV_ROWS, 1024, HALO4
    ns = seq // ts
    base = halo - SSD_CONV_K + 1
    hb = ts // halo

    def body(d_ref, x_ref, xh_ref, w_ref, b_ref, dx_ref, dw_ref, db_ref, xbuf, dbuf, dwacc, dbacc):
        b, i = pl.program_id(1), pl.program_id(2)
        rc = ns - 1 - i

        @pl.when(jnp.logical_and(b == 0, i == 0))
        def _():
            dwacc[...] = jnp.zeros_like(dwacc)
            dbacc[...] = jnp.zeros_like(dbacc)

        @pl.when(i == 0)
        def _():
            dbuf[ts:ts + halo, :] = jnp.zeros((halo, c), F32)

        @pl.when(i > 0)
        def _():
            dbuf[ts:ts + halo, :] = dbuf[0:halo, :]

        @pl.when(rc == 0)
        def _():
            xbuf[0:halo, :] = jnp.zeros((halo, c), F32)

        @pl.when(rc > 0)
        def _():
            xbuf[0:halo, :] = xh_ref[...]

        xbuf[halo:halo + ts, :] = x_ref[...]

        def pre_tile(j, carry):
            ln = pl.ds(pl.multiple_of(j * LANES, LANES), LANES)
            for r in range(ts // TAP_ROWS):
                r0 = r * TAP_ROWS
                acc = jnp.broadcast_to(b_ref[:, ln], (TAP_ROWS, LANES))
                for k in range(SSD_CONV_K):
                    acc = acc + w_ref[pl.ds(k, 1), ln] * xbuf[pl.ds(r0 + base + k, TAP_ROWS), ln]
                s = _sig(acc)
                dc = d_ref[pl.ds(r0, TAP_ROWS), ln] * (s * (1.0 + acc * (1.0 - s)))
                dbuf[pl.ds(r0, TAP_ROWS), ln] = dc
                dbacc[:, ln] += _sum8(dc)
            return carry

        lax.fori_loop(0, c // LANES, pre_tile, 0)

        def lane_tile(j, carry):
            ln = pl.ds(pl.multiple_of(j * LANES, LANES), LANES)
            for r in range(ts // TAP_ROWS):
                r0 = r * TAP_ROWS
                d1 = dbuf[pl.ds(r0, TAP_ROWS), ln]
                acc = jnp.zeros((TAP_ROWS, LANES), F32)
                for k in range(SSD_CONV_K):
                    acc = acc + w_ref[pl.ds(k, 1), ln] * dbuf[pl.ds(r0 + SSD_CONV_K - 1 - k, TAP_ROWS), ln]
                    dwacc[pl.ds(k * 8, 8), ln] += _sum8(d1 * xbuf[pl.ds(r0 + base + k, TAP_ROWS), ln])
                dx_ref[pl.ds(r0, TAP_ROWS), ln] = acc.astype(BF16)
            return carry

        lax.fori_loop(0, c // LANES, lane_tile, 0)

        @pl.when(jnp.logical_and(b == nb - 1, i == ns - 1))
        def _():
            for k in range(SSD_CONV_K):
                dw_ref[pl.ds(k, 1), :] = jnp.sum(dwacc[pl.ds(k * 8, 8), :], axis=0, keepdims=True)
            dw_ref[pl.ds(SSD_CONV_K, 8 - SSD_CONV_K), :] = jnp.zeros((8 - SSD_CONV_K, c), F32)
            db_ref[...] = jnp.sum(dbacc[...], axis=0, keepdims=True)

    t = nb * seq
    rowblk = lambda b, i: b * ns + (ns - 1 - i)
    return pl.pallas_call(
        body,
        grid=(2, nb, ns),
        in_specs=[pl.BlockSpec((ts, c), lambda j, b, i: (rowblk(b, i), j)),
                  pl.BlockSpec((ts, c), lambda j, b, i: (rowblk(b, i), XBC_COL0 + j)),
                  pl.BlockSpec((halo, c), lambda j, b, i: (jnp.maximum(rowblk(b, i) * hb - 1, 0), XBC_COL0 + j)),
                  pl.BlockSpec((8, c), lambda j, b, i: (0, j)), pl.BlockSpec((1, c), lambda j, b, i: (0, j))],
        out_specs=[pl.BlockSpec((ts, c), lambda j, b, i: (rowblk(b, i), j)),
                   pl.BlockSpec((8, c), lambda j, b, i: (0, j)), pl.BlockSpec((1, c), lambda j, b, i: (0, j))],
        out_shape=[jax.ShapeDtypeStruct((t, D_XBC), BF16), jax.ShapeDtypeStruct((8, D_XBC), F32),
                   jax.ShapeDtypeStruct((1, D_XBC), F32)],
        scratch_shapes=[pltpu.VMEM((halo + ts, c), F32), pltpu.VMEM((ts + halo, c), F32),
                        pltpu.VMEM((SSD_CONV_K * 8, c), F32), pltpu.VMEM((8, c), F32)],
        compiler_params=_cparams(("arbitrary", "arbitrary", "arbitrary")),
        name="ssd_pre_bwd",
    )(dxs, proj, proj, sw, sb)


Z_COL = (2 * D_CONV) // 1024
GROUP_W = D_SSD // GROUPS


def _softplus(v):
    return jnp.maximum(v, 0.0) + jnp.log(1.0 + jnp.exp(-jnp.abs(v)))


def _dot(a, b):
    return jnp.dot(a, b, preferred_element_type=F32)


def _dot_nt(a, b):
    return lax.dot_general(a, b, (_DIMS["nt"], ((), ())), preferred_element_type=F32)


def _dot_tn(a, b):
    return lax.dot_general(a, b, (_DIMS["tn"], ((), ())), preferred_element_type=F32)


def _dot_exact(a, b):
    return jnp.dot(a, b, precision=lax.Precision.HIGHEST, preferred_element_type=F32)


def _chunk_decays(dtr_ref, bias_ref, alog_ref):
    q = CHUNK
    ii = lax.broadcasted_iota(jnp.int32, (q, q), 0)
    jj = lax.broadcasted_iota(jnp.int32, (q, q), 1)
    tri = jj <= ii
    dt = _softplus(dtr_ref[...] + bias_ref[...])
    a_head = -jnp.exp(alog_ref[...])
    cs = _dot_exact(tri.astype(F32), dt * a_head)
    return tri, dt, a_head, cs, cs.T


def _ssd_fwd(xs_all, proj, dtr, dt_bias, a_log, d_lanes, norm_w, *, nb, seq):
    q = CHUNK
    nc = seq // q
    t = nb * seq

    def body(xs_ref, bm_ref, cm_ref, z_ref, dtr_ref, bias_ref, alog_ref, dl_ref, nw_ref,
             y_ref, ys_ref, st_ref, state):
        @pl.when(pl.program_id(1) == 0)
        def _():
            state[...] = jnp.zeros_like(state)

        tri, dt, _, cs, cst = _chunk_decays(dtr_ref, bias_ref, alog_ref)
        first = lax.broadcasted_iota(jnp.int32, (1, LANES), 1) < HEAD_DIM
        for g in range(GROUPS):
            gl = slice(g * STATE, (g + 1) * STATE)
            bb = bm_ref[:, gl].astype(BF16)
            cb = cm_ref[:, gl].astype(BF16)
            scores = _dot_nt(cb, bb)
            for p in range(2):
                pr = 2 * g + p
                h0 = 2 * pr
                sl = slice(pr * LANES, (pr + 1) * LANES)
                xv = xs_ref[:, sl]
                dtp = jnp.where(first, dt[:, h0:h0 + 1], dt[:, h0 + 1:h0 + 2])
                csp = jnp.where(first, cs[:, h0:h0 + 1], cs[:, h0 + 1:h0 + 2])
                xd = xv * dtp
                yv = None
                for hh, keep in ((h0, first), (h0 + 1, jnp.logical_not(first))):
                    decay = jnp.where(tri, jnp.exp(cs[:, hh:hh + 1] - cst[hh:hh + 1, :]), 0.0)
                    part = _dot((scores * decay).astype(BF16), jnp.where(keep, xd, 0.0).astype(BF16))
                    yv = part if yv is None else yv + part
                hp = state[pr]
                st_ref[0, pr] = hp
                yv = yv + jnp.exp(csp) * _dot(cb, hp.astype(BF16))
                last = csp[q - 1:q, :]
                state[pr] = jnp.exp(last) * hp + _dot_tn(bb, (xd * jnp.exp(last - csp)).astype(BF16))
                ys_ref[:, sl] = yv + dl_ref[:, sl] * xv
        zv = z_ref[...]
        gated = ys_ref[...] * (zv * _sig(zv))
        for g in range(GROUPS):
            gl = slice(g * GROUP_W, (g + 1) * GROUP_W)
            v = gated[:, gl]
            r = lax.rsqrt(jnp.mean(v * v, axis=-1, keepdims=True) + EPS)
            y_ref[:, gl] = (v * r * nw_ref[:, gl]).astype(BF16)

    blk = lambda w, col: pl.BlockSpec((q, w), lambda b, c: (b * nc + c, col))
    vec = lambda w: pl.BlockSpec((1, w), lambda b, c: (0, 0))
    return pl.pallas_call(
        body,
        grid=(nb, nc),
        in_specs=[blk(D_SSD, 0), blk(GROUPS * STATE, 2), blk(GROUPS * STATE, 3), blk(D_SSD, Z_COL), blk(LANES, 0),
                  vec(LANES), vec(LANES), vec(D_SSD), vec(D_SSD)],
        out_specs=[blk(D_SSD, 0), blk(D_SSD, 0),
                   pl.BlockSpec((1, PAIRS, STATE, LANES), lambda b, c: (b * nc + c, 0, 0, 0))],
        out_shape=[jax.ShapeDtypeStruct((t, D_SSD), BF16), jax.ShapeDtypeStruct((t, D_SSD), F32),
                   jax.ShapeDtypeStruct((nb * nc, PAIRS, STATE, LANES), F32)],
        scratch_shapes=[pltpu.VMEM((PAIRS, STATE, LANES), F32)],
        compiler_params=_cparams(("parallel", "arbitrary")),
        name="ssd_fwd",
    )(xs_all, xs_all, xs_all, proj, dtr, dt_bias, a_log, d_lanes, norm_w)


def _ssd_bwd(duy, proj, ys, xs_all, dtr, states, dt_bias, a_log, d_lanes, norm_w, *, nb, seq):
    q = CHUNK
    nc = seq // q
    t = nb * seq

    def body(dy_ref, z_ref, ys_ref, xs_ref, bm_ref, cm_ref, dtr_ref, st_ref, bias_ref, alog_ref, dl_ref, nw_ref,
             dz_ref, dx_ref, ddtr_ref, small_ref,
             dstate, dys_buf, dcsl, ddtl, dcst, dnw_acc, dd_acc, dbias_acc, da_acc):
        b, c = pl.program_id(0), pl.program_id(1)

        @pl.when(jnp.logical_and(b == 0, c == 0))
        def _():
            dnw_acc[...] = jnp.zeros_like(dnw_acc)
            dd_acc[...] = jnp.zeros_like(dd_acc)
            dbias_acc[...] = jnp.zeros_like(dbias_acc)
            da_acc[...] = jnp.zeros_like(da_acc)
            dcst[...] = jnp.zeros_like(dcst)

        @pl.when(c == 0)
        def _():
            dstate[...] = jnp.zeros_like(dstate)

        zv = z_ref[...]
        sz = _sig(zv)
        silz = zv * sz
        ysv = ys_ref[...]
        gated = ysv * silz
        dyv = dy_ref[...]
        nwv = nw_ref[...]
        for g in range(GROUPS):
            gl = slice(g * GROUP_W, (g + 1) * GROUP_W)
            v = gated[:, gl]
            r = lax.rsqrt(jnp.mean(v * v, axis=-1, keepdims=True) + EPS)
            yn = v * r
            dyn = dyv[:, gl] * nwv[:, gl]
            dnw_acc[:, gl] += jnp.sum(dyv[:, gl] * yn, axis=0, keepdims=True)
            dys_buf[:, gl] = r * (dyn - yn * jnp.mean(dyn * yn, axis=-1, keepdims=True))
        dgated = dys_buf[...]
        dz_ref[...] = (dgated * ysv * (sz * (1.0 + zv * (1.0 - sz)))).astype(BF16)
        dys_all = dgated * silz
        dys_buf[...] = dys_all
        dd_acc[...] += jnp.sum(dys_all * xs_ref[...], axis=0, keepdims=True)

        tri, dt, a_head, cs, cst = _chunk_decays(dtr_ref, bias_ref, alog_ref)
        lane = lax.broadcasted_iota(jnp.int32, (1, LANES), 1)
        first = lane < HEAD_DIM
        dcs_h = jnp.zeros((q, LANES), F32)
        for g in range(GROUPS):
            gl = slice(g * STATE, (g + 1) * STATE)
            bb = bm_ref[:, gl].astype(BF16)
            cb = cm_ref[:, gl].astype(BF16)
            scores = _dot_nt(cb, bb)
            dscores = jnp.zeros((q, q), F32)
            dbg = jnp.zeros((q, STATE), F32)
            dcg = jnp.zeros((q, STATE), F32)
            for p in range(2):
                pr = 2 * g + p
                h0 = 2 * pr
                sl = slice(pr * LANES, (pr + 1) * LANES)
                xv = xs_ref[:, sl]
                dyp = dys_buf[:, sl]
                dtp = jnp.where(first, dt[:, h0:h0 + 1], dt[:, h0 + 1:h0 + 2])
                csp = jnp.where(first, cs[:, h0:h0 + 1], cs[:, h0 + 1:h0 + 2])
                xd = xv * dtp
                xdb = xd.astype(BF16)
                hp = st_ref[0, pr]
                dhn = dstate[pr]
                hpb = hp.astype(BF16)
                dhnb = dhn.astype(BF16)
                lam = jnp.exp(csp)
                last = csp[q - 1:q, :]
                gam = jnp.exp(last)
                w = jnp.exp(last - csp)
                dxd = jnp.zeros((q, LANES), F32)
                for hh, keep in ((h0, first), (h0 + 1, jnp.logical_not(first))):
                    decay = jnp.where(tri, jnp.exp(cs[:, hh:hh + 1] - cst[hh:hh + 1, :]), 0.0)
                    m = scores * decay
                    dym = jnp.where(keep, dyp, 0.0).astype(BF16)
                    dm = _dot_nt(dym, xdb)
                    dxd = dxd + _dot_tn(m.astype(BF16), dym)
                    e = dm * m
                    dcs_h = dcs_h + jnp.where(lane == hh, jnp.sum(e, axis=1, keepdims=True), 0.0)
                    dcst[hh:hh + 1, :] = jnp.sum(e, axis=0, keepdims=True)
                    dscores = dscores + dm * decay
                yoff = lam * _dot(cb, hpb)
                ldy = (lam * dyp).astype(BF16)
                dcg = dcg + _dot_nt(ldy, hpb)
                dstate[pr] = gam * dhn + _dot_tn(cb, ldy)
                bdh = _dot(bb, dhnb)
                dxd = dxd + w * bdh
                xdw = xd * w
                dbg = dbg + _dot_nt(xdw.astype(BF16), dhnb)
                wd = xdw * bdh
                dcsl[:, sl] = dyp * yoff - wd
                dcsl[q - 1:q, sl] += (jnp.sum(wd, axis=0, keepdims=True)
                                      + gam * jnp.sum(dhn * hp, axis=0, keepdims=True))
                dx_ref[:, sl] = dxd * dtp + dyp * dl_ref[:, sl]
                ddtl[:, sl] = dxd * xv
            dsb = dscores.astype(BF16)
            dx_ref[:, D_SSD + g * STATE:D_SSD + (g + 1) * STATE] = dbg + _dot_tn(dsb, cb)
            dx_ref[:, D_SSD + (GROUPS + g) * STATE:D_SSD + (GROUPS + g + 1) * STATE] = dcg + _dot(dsb, bb)

        li = lax.broadcasted_iota(jnp.int32, (D_SSD, LANES), 0)
        hi = lax.broadcasted_iota(jnp.int32, (D_SSD, LANES), 1)
        sel = (li // HEAD_DIM == hi).astype(F32)
        dcs_h = dcs_h + _dot_exact(dcsl[...], sel) - dcst[...].T
        ddt = _dot_exact(ddtl[...], sel)
        upper = lax.broadcasted_iota(jnp.int32, (q, q), 1) >= lax.broadcasted_iota(jnp.int32, (q, q), 0)
        da = _dot_exact(upper.astype(F32), dcs_h)
        ddt = ddt + da * a_head
        da_acc[...] += jnp.sum(da * dt, axis=0, keepdims=True)
        ddtr = ddt * _sig(dtr_ref[...] + bias_ref[...])
        ddtr_ref[...] = ddtr
        dbias_acc[...] += jnp.sum(ddtr, axis=0, keepdims=True)

        @pl.when(jnp.logical_and(b == nb - 1, c == nc - 1))
        def _():
            small_ref[...] = jnp.zeros_like(small_ref)
            small_ref[0:1, :] = dnw_acc[...]
            small_ref[1:2, 0:LANES] = _dot_exact(jnp.broadcast_to(dd_acc[...], (8, D_SSD)), sel)[0:1, :]
            small_ref[2:3, 0:LANES] = dbias_acc[...]
            small_ref[3:4, 0:LANES] = da_acc[...] * a_head

    rowblk = lambda b, c: b * nc + (nc - 1 - c)
    blk = lambda w, col: pl.BlockSpec((q, w), lambda b, c: (rowblk(b, c), col))
    vec = lambda w: pl.BlockSpec((1, w), lambda b, c: (0, 0))
    return pl.pallas_call(
        body,
        grid=(nb, nc),
        in_specs=[blk(D_SSD, 1), blk(D_SSD, Z_COL), blk(D_SSD, 0), blk(D_SSD, 0), blk(GROUPS * STATE, 2),
                  blk(GROUPS * STATE, 3), blk(LANES, 0),
                  pl.BlockSpec((1, PAIRS, STATE, LANES), lambda b, c: (rowblk(b, c), 0, 0, 0)),
                  vec(LANES), vec(LANES), vec(D_SSD), vec(D_SSD)],
        out_specs=[blk(D_SSD, 0), blk(D_XBC, 0), blk(LANES, 0), pl.BlockSpec((8, D_SSD), lambda b, c: (0, 0))],
        out_shape=[jax.ShapeDtypeStruct((t, D_SSD), BF16), jax.ShapeDtypeStruct((t, D_XBC), F32),
                   jax.ShapeDtypeStruct((t, LANES), F32), jax.ShapeDtypeStruct((8, D_SSD), F32)],
        scratch_shapes=[pltpu.VMEM((PAIRS, STATE, LANES), F32), pltpu.VMEM((q, D_SSD), F32),
                        pltpu.VMEM((q, D_SSD), F32), pltpu.VMEM((q, D_SSD), F32), pltpu.VMEM((LANES, q), F32),
                        pltpu.VMEM((1, D_SSD), F32), pltpu.VMEM((1, D_SSD), F32), pltpu.VMEM((1, LANES), F32),
                        pltpu.VMEM((1, LANES), F32)],
        compiler_params=_cparams(("arbitrary", "arbitrary")),
        name="ssd_bwd",
    )(duy, proj, ys, xs_all, xs_all, xs_all, dtr, states, dt_bias, a_log, d_lanes, norm_w)


HBM_SPEC = pl.BlockSpec(memory_space=pltpu.HBM)
MESH_ID = pl.DeviceIdType.MESH


def _coords():
    return lax.axis_index("x"), lax.axis_index("y"), lax.axis_index("c")


def _chip_peer(xi, yi, ci, d):
    return (jnp.bitwise_xor(xi, d >> 1), jnp.bitwise_xor(yi, d & 1), ci)


def _remote(src, dst, send_sem, recv_sem, peer):
    return pltpu.make_async_remote_copy(src_ref=src, dst_ref=dst, send_sem=send_sem, recv_sem=recv_sem,
                                        device_id=peer, device_id_type=MESH_ID)


def _gather_weights(big, small):
    nbig, nsm = len(big), len(small)
    ntot = nbig + nsm

    def body(*refs):
        ins, outs = refs[:ntot], refs[ntot:2 * ntot]
        send_sems, recv_sems, loc_sems = refs[2 * ntot:]
        xi, yi, ci = _coords()
        me = 2 * xi + yi
        sibling = (xi, yi, 1 - ci)
        local = [pltpu.make_async_copy(ins[t], outs[t].at[me], loc_sems.at[t]) for t in range(ntot)]
        for cp in local:
            cp.start()
        sends = []
        for t in range(nbig):
            hr = big[t].shape[0] // 2
            mine = pl.ds(ci * hr, hr)
            for d in (1, 2, 3):
                k = 6 * t + d - 1
                cp = _remote(ins[t].at[mine, :], outs[t].at[me, mine, :], send_sems.at[k], recv_sems.at[k],
                             _chip_peer(xi, yi, ci, d))
                cp.start()
                sends.append(cp)
        for s in range(nsm):
            t = nbig + s
            for d in (1, 2, 3):
                k = 6 * nbig + 3 * s + d - 1
                cp = _remote(ins[t], outs[t].at[me], send_sems.at[k], recv_sems.at[k], _chip_peer(xi, yi, ci, d))
                cp.start()
                sends.append(cp)
        for t in range(nbig):
            hr = big[t].shape[0] // 2
            mine = pl.ds(ci * hr, hr)
            for d in (1, 2, 3):
                k = 6 * t + d - 1
                src_chip = jnp.bitwise_xor(me, d)
                landed = outs[t].at[src_chip, mine, :]
                _remote(landed, landed, send_sems.at[k], recv_sems.at[k], sibling).wait_recv()
                cp = _remote(landed, landed, send_sems.at[k + 3], recv_sems.at[k + 3], sibling)
                cp.start()
                sends.append(cp)
        for t in range(nbig):
            hr = big[t].shape[0] // 2
            theirs = pl.ds((1 - ci) * hr, hr)
            for d in (1, 2, 3):
                k = 6 * t + 3 + d - 1
                landed = outs[t].at[jnp.bitwise_xor(me, d), theirs, :]
                _remote(landed, landed, send_sems.at[k], recv_sems.at[k], sibling).wait_recv()
        for s in range(nsm):
            t = nbig + s
            for d in (1, 2, 3):
                k = 6 * nbig + 3 * s + d - 1
                landed = outs[t].at[jnp.bitwise_xor(me, d)]
                _remote(landed, landed, send_sems.at[k], recv_sems.at[k], sibling).wait_recv()
        for cp in sends:
            cp.wait_send()
        for cp in local:
            cp.wait()

    nsem = 6 * nbig + 3 * nsm
    allin = list(big) + list(small)
    return pl.pallas_call(
        body,
        in_specs=[HBM_SPEC] * ntot,
        out_specs=[HBM_SPEC] * ntot,
        out_shape=[jax.ShapeDtypeStruct((N_CHIPS,) + a.shape, a.dtype) for a in allin],
        scratch_shapes=[pltpu.SemaphoreType.DMA((nsem,)), pltpu.SemaphoreType.DMA((nsem,)),
                        pltpu.SemaphoreType.DMA((ntot,))],
        name="gather_weights",
    )(*allin)


def _swap_other_halves(gs):
    n = len(gs)

    def body(*refs):
        ins, lands = refs[:n], refs[n:2 * n]
        send_sems, recv_sems = refs[2 * n:]
        xi, yi, ci = _coords()
        sibling = (xi, yi, 1 - ci)
        cps = []
        for t in range(n):
            hr = gs[t].shape[1] // 2
            cp = _remote(ins[t].at[:, pl.ds((1 - ci) * hr, hr), :], lands[t], send_sems.at[t], recv_sems.at[t], sibling)
            cp.start()
            cps.append(cp)
        for cp in cps:
            cp.wait_recv()
        for cp in cps:
            cp.wait_send()

    return pl.pallas_call(
        body,
        in_specs=[HBM_SPEC] * n,
        out_specs=[HBM_SPEC] * n,
        out_shape=[jax.ShapeDtypeStruct((g.shape[0], g.shape[1] // 2, g.shape[2]), g.dtype) for g in gs],
        scratch_shapes=[pltpu.SemaphoreType.DMA((n,)), pltpu.SemaphoreType.DMA((n,))],
        name="swap_other_halves",
    )(*gs)


def _row_tile(rows, cap=512, mult=16):
    best = mult
    for cand in range(mult, min(rows, cap) + 1, mult):
        if rows % cand == 0:
            best = cand
    assert rows % best == 0, rows
    return best


def _add_core_halves(g, land, where):
    nslot, rows, cols = g.shape
    hr = rows // 2
    tr = _row_tile(hr)
    nr = hr // tr

    def body(where_ref, g_ref, l_ref, f_ref, b_ref):
        s = g_ref[...] + l_ref[...]
        f_ref[...] = s
        b_ref[...] = s.astype(BF16)

    blk = pl.BlockSpec((None, tr, cols), lambda s, i, w: (s, i, 0))
    return pl.pallas_call(
        body,
        grid_spec=pltpu.PrefetchScalarGridSpec(
            num_scalar_prefetch=1,
            grid=(nslot, nr),
            in_specs=[pl.BlockSpec((None, tr, cols), lambda s, i, w: (s, w[0] * nr + i, 0)), blk],
            out_specs=[blk, blk],
        ),
        out_shape=[jax.ShapeDtypeStruct((nslot, hr, cols), F32), jax.ShapeDtypeStruct((nslot, hr, cols), BF16)],
        compiler_params=_cparams(("parallel", "parallel")),
        name="add_core_halves",
    )(where, g, land)


def _send_to_owners(ps):
    n = len(ps)

    def body(*refs):
        ins, lands = refs[:n], refs[n:2 * n]
        send_sems, recv_sems = refs[2 * n:]
        xi, yi, ci = _coords()
        me = 2 * xi + yi
        cps = []
        for t in range(n):
            for d in (1, 2, 3):
                k = 3 * t + d - 1
                cp = _remote(ins[t].at[jnp.bitwise_xor(me, d)], lands[t].at[d - 1], send_sems.at[k], recv_sems.at[k],
                             _chip_peer(xi, yi, ci, d))
                cp.start()
                cps.append(cp)
        for cp in cps:
            cp.wait_recv()
        for cp in cps:
            cp.wait_send()

    return pl.pallas_call(
        body,
        in_specs=[HBM_SPEC] * n,
        out_specs=[HBM_SPEC] * n,
        out_shape=[jax.ShapeDtypeStruct((3,) + p.shape[1:], p.dtype) for p in ps],
        scratch_shapes=[pltpu.SemaphoreType.DMA((3 * n,)), pltpu.SemaphoreType.DMA((3 * n,))],
        name="send_to_owners",
    )(*ps)


def _add_chip_sums(pf, land, where):
    _, hr, cols = pf.shape
    tr = _row_tile(hr)

    def body(where_ref, p_ref, l_ref, o_ref):
        acc = p_ref[...]
        for d in range(3):
            acc = acc + l_ref[d].astype(F32)
        o_ref[...] = acc

    return pl.pallas_call(
        body,
        grid_spec=pltpu.PrefetchScalarGridSpec(
            num_scalar_prefetch=1,
            grid=(hr // tr,),
            in_specs=[pl.BlockSpec((None, tr, cols), lambda i, w: (w[1], i, 0)),
                      pl.BlockSpec((3, tr, cols), lambda i, w: (0, i, 0))],
            out_specs=pl.BlockSpec((tr, cols), lambda i, w: (i, 0)),
        ),
        out_shape=jax.ShapeDtypeStruct((hr, cols), F32),
        compiler_params=_cparams(("parallel",)),
        name="add_chip_sums",
    )(where, pf, land)


def _join_core_halves(rs):
    n = len(rs)

    def body(*refs):
        ins, outs = refs[:n], refs[n:2 * n]
        send_sems, recv_sems, loc_sems = refs[2 * n:]
        xi, yi, ci = _coords()
        sibling = (xi, yi, 1 - ci)
        cps, local = [], []
        for t in range(n):
            hr = rs[t].shape[0]
            mine = outs[t].at[pl.ds(ci * hr, hr), :]
            lc = pltpu.make_async_copy(ins[t], mine, loc_sems.at[t])
            lc.start()
            local.append(lc)
            cp = _remote(ins[t], mine, send_sems.at[t], recv_sems.at[t], sibling)
            cp.start()
            cps.append(cp)
        for t in range(n):
            hr = rs[t].shape[0]
            theirs = outs[t].at[pl.ds((1 - ci) * hr, hr), :]
            _remote(theirs, theirs, send_sems.at[t], recv_sems.at[t], sibling).wait_recv()
        for cp in cps:
            cp.wait_send()
        for lc in local:
            lc.wait()

    return pl.pallas_call(
        body,
        in_specs=[HBM_SPEC] * n,
        out_specs=[HBM_SPEC] * n,
        out_shape=[jax.ShapeDtypeStruct((2 * r.shape[0], r.shape[1]), r.dtype) for r in rs],
        scratch_shapes=[pltpu.SemaphoreType.DMA((n,)), pltpu.SemaphoreType.DMA((n,)), pltpu.SemaphoreType.DMA((n,))],
        name="join_core_halves",
    )(*rs)


N_DEV = 8


def _all_reduce_small(part):
    r, w = part.shape

    def body(p_ref, o_ref, gath, send_sems, recv_sems):
        xi, yi, ci = _coords()
        me = 4 * xi + 2 * yi + ci
        gath[me] = p_ref[...]
        cps = []
        for d in range(1, N_DEV):
            peer = (jnp.bitwise_xor(xi, d >> 2), jnp.bitwise_xor(yi, (d >> 1) & 1), jnp.bitwise_xor(ci, d & 1))
            cp = _remote(p_ref, gath.at[me], send_sems.at[d - 1], recv_sems.at[d - 1], peer)
            cp.start()
            cps.append(cp)
        for d in range(1, N_DEV):
            src = gath.at[jnp.bitwise_xor(me, d)]
            _remote(src, src, send_sems.at[d - 1], recv_sems.at[d - 1], (xi, yi, ci)).wait_recv()
        acc = gath[0]
        for k in range(1, N_DEV):
            acc = acc + gath[k]
        o_ref[...] = acc
        for cp in cps:
            cp.wait_send()

    vm = pl.BlockSpec(memory_space=pltpu.VMEM)
    return pl.pallas_call(
        body,
        in_specs=[vm],
        out_specs=vm,
        out_shape=jax.ShapeDtypeStruct((r, w), F32),
        scratch_shapes=[pltpu.VMEM((N_DEV, r, w), F32), pltpu.SemaphoreType.DMA((N_DEV - 1,)),
                        pltpu.SemaphoreType.DMA((N_DEV - 1,))],
        name="all_reduce_small",
    )(part)


def _adamw(w, g, m, v, *, name):
    rows, cols = w.shape
    tr = _row_tile(rows, cap=256, mult=8)

    def body(w_ref, g_ref, m_ref, v_ref, d_ref, nm_ref, nv_ref):
        gv = g_ref[...]
        mn = ADAM_B1 * m_ref[...] + (1.0 - ADAM_B1) * gv
        vn = ADAM_B2 * v_ref[...] + (1.0 - ADAM_B2) * (gv * gv)
        m_hat = mn / (1.0 - ADAM_B1 ** ADAM_STEP)
        v_hat = vn / (1.0 - ADAM_B2 ** ADAM_STEP)
        d_ref[...] = -ADAM_LR * (m_hat / (jnp.sqrt(v_hat) + ADAM_EPS) + ADAM_WD * w_ref[...])
        nm_ref[...] = mn
        nv_ref[...] = vn

    blk = pl.BlockSpec((tr, cols), lambda i: (i, 0))
    o = jax.ShapeDtypeStruct((rows, cols), F32)
    return pl.pallas_call(
        body,
        grid=(rows // tr,),
        in_specs=[blk] * 4,
        out_specs=[blk] * 3,
        out_shape=[o, o, o],
        compiler_params=_cparams(("parallel",)),
        name=name,
    )(w, g, m, v)


def _pack(arrs):
    flat = jnp.concatenate([a.reshape(-1) for a in arrs])
    pad = (-flat.shape[0]) % (8 * LANES)
    return jnp.pad(flat, (0, pad)).reshape(-1, LANES)


def _unpack(packed, shapes):
    flat = packed.reshape(-1)
    out, off = [], 0
    for s in shapes:
        n = 1
        for dim in s:
            n *= dim
        out.append(flat[off:off + n].reshape(s))
        off += n
    return out


def _pad_rows(a, rows):
    return jnp.pad(a, ((0, rows - a.shape[0]), (0, 0)))


def _pad_lanes(a, lanes=LANES):
    return jnp.pad(a, ((0, 0), (0, lanes - a.shape[1])))


def _local_grads(x2d, tgt2d, prm, *, nb, seq):
    g_pre, g_post, g_fpre, g_fpost = prm["norm_mix_pre"], prm["norm_mix_post"], prm["norm_ffn_pre"], prm["norm_ffn_post"]
    w_main, w_dt, w_out, w_gate, w_up, w_down = (prm[k] for k in ("w_main", "w_dt", "w_out", "w_gate", "w_up", "w_down"))
    cw, sw = prm["conv_dw_w"], prm["ssd_conv_w"]
    dt_bias, a_log = _pad_lanes(prm["ssd_dt_bias"]), _pad_lanes(prm["ssd_a_log"])
    d_lanes = jnp.repeat(prm["ssd_d"], HEAD_DIM, axis=1)

    h = _rms_fwd(x2d, g_pre, out_dtype=BF16, name="rms_mix_pre")
    proj = _matmul([(h, w_main)], mode="nn", out_dtype=F32, tm=1024, tn=1024, tk=2048, name="mm_proj")
    dtr = _matmul([(h, w_dt)], mode="nn", out_dtype=F32, tm=1024, tn=128, tk=2048, name="mm_dt")
    u1, u = _conv_branch_fwd(proj, cw, prm["conv_dw_b"], prm["conv_ln_g"], prm["conv_ln_b"], nb=nb, seq=seq)
    xs_all = _ssd_pre_fwd(proj, sw, prm["ssd_conv_b"], nb=nb, seq=seq)
    y, ys, states = _ssd_fwd(xs_all, proj, dtr, dt_bias, a_log, d_lanes, prm["ssd_norm_w"], nb=nb, seq=seq)
    mix = _matmul([(u, w_out[:D_CONV]), (y, w_out[D_CONV:])], mode="nn", out_dtype=F32, tm=1024, tn=1024, tk=1024,
                  name="mm_mix")
    x1 = _rms_fwd(mix, g_post, res=x2d, out_dtype=F32, name="rms_mix_post")
    h2 = _rms_fwd(x1, g_fpre, out_dtype=BF16, name="rms_ffn_pre")
    gt, up, act = _ffn_up(h2, w_gate, w_up, tm=1024, tn=512)
    f = _matmul([(act, w_down)], mode="nn", out_dtype=F32, tm=1024, tn=1024, tk=1408, name="mm_down")
    dx2, df, loss, dg_fpost = _ffn_post_loss(f, x1, tgt2d, g_fpost)

    dgt, dup = _ffn_bwd_act(df, w_down, gt, up, tm=1024, tn=512)
    dw_down = _matmul([(act, df)], mode="tn", out_dtype=F32, tm=1408, tn=1024, tk=1024, name="mm_dw_down")
    dh2 = _matmul([(dgt, w_gate), (dup, w_up)], mode="nt", out_dtype=F32, tm=1024, tn=1024, tk=1408, name="mm_dh2")
    dw_gate = _matmul([(h2, dgt)], mode="tn", out_dtype=F32, tm=1024, tn=1408, tk=1024, name="mm_dw_gate", slot_out=True)
    dw_up = _matmul([(h2, dup)], mode="tn", out_dtype=F32, tm=1024, tn=1408, tk=1024, name="mm_dw_up", slot_out=True)
    dx1, dg_fpre = _rms_bwd([dh2], x1, g_fpre, addend=dx2, out_dtype=F32, name="rms_ffn_pre_bwd")
    dmix, dg_post = _rms_bwd([dx1], mix, g_post, out_dtype=BF16, name="rms_mix_post_bwd")
    duy = _matmul([(dmix, w_out)], mode="nt", out_dtype=F32, tm=1024, tn=1024, tk=2048, name="mm_duy")
    dw_out_u = _matmul([(u, dmix)], mode="tn", out_dtype=F32, tm=1024, tn=1024, tk=1024, name="mm_dw_out_u")
    dw_out_y = _matmul([(y, dmix)], mode="tn", out_dtype=F32, tm=1024, tn=1024, tk=1024, name="mm_dw_out_y")
    dcacg, dcw, dcb, dlg, dlb = _conv_branch_bwd(duy, u1, proj, cw, prm["conv_ln_g"], prm["conv_ln_b"], nb=nb, seq=seq)
    dz, dxs, ddtr, ssd_small = _ssd_bwd(duy, proj, ys, xs_all, dtr, states, dt_bias, a_log, d_lanes,
                                        prm["ssd_norm_w"], nb=nb, seq=seq)
    dxbc, dsw, dsb = _ssd_pre_bwd(dxs, proj, sw, prm["ssd_conv_b"], nb=nb, seq=seq)
    dproj = jnp.concatenate([dcacg, dz, dxbc], axis=1)
    ddtr_b = ddtr.astype(BF16)
    dh_main = _matmul([(dproj, w_main)], mode="nt", out_dtype=F32, tm=1024, tn=1024, tk=1024, name="mm_dh_main")
    dh_dt = _matmul([(ddtr_b, w_dt)], mode="nt", out_dtype=F32, tm=1024, tn=1024, tk=128, name="mm_dh_dt")
    dx, dg_pre = _rms_bwd([dh_main, dh_dt], x2d, g_pre, addend=dx1, out_dtype=F32, name="rms_mix_pre_bwd")
    dw_main = _matmul([(h, dproj)], mode="tn", out_dtype=F32, tm=1024, tn=1024, tk=1024, name="mm_dw_main")
    dw_dt = _matmul([(h, ddtr_b)], mode="tn", out_dtype=F32, tm=1024, tn=128, tk=1024, name="mm_dw_dt")

    grads = {
        "norm_mix_pre": dg_pre,
        "w_in": jnp.concatenate([dw_main, dw_dt[:, :HEADS]], axis=1),
        "conv_dw_w": dcw[:CONV_K], "conv_dw_b": dcb, "conv_ln_g": dlg, "conv_ln_b": dlb,
        "ssd_conv_w": dsw[:SSD_CONV_K], "ssd_conv_b": dsb,
        "ssd_dt_bias": ssd_small[2:3, :HEADS], "ssd_a_log": ssd_small[3:4, :HEADS], "ssd_d": ssd_small[1:2, :HEADS],
        "ssd_norm_w": ssd_small[0:1],
        "w_out": jnp.concatenate([dw_out_u, dw_out_y], axis=0),
        "norm_mix_post": dg_post, "norm_ffn_pre": dg_fpre,
        "w_gate": dw_gate, "w_up": dw_up,
        "w_down": dw_down, "norm_ffn_post": dg_fpost,
    }
    return loss, dx, grads


BIG = ("w_in", "w_out", "w_gate", "w_up", "w_down")
COL_SHARDED = ("w_in", "w_gate", "w_up")
SMALL = ("norm_mix_pre", "conv_dw_w", "conv_dw_b", "conv_ln_g", "conv_ln_b", "ssd_conv_w", "ssd_conv_b", "ssd_dt_bias",
         "ssd_a_log", "ssd_d", "ssd_norm_w", "norm_mix_post", "norm_ffn_pre", "norm_ffn_post")
WEIGHTS = ("norm_mix_pre", "w_in", "conv_dw_w", "conv_dw_b", "conv_ln_g", "conv_ln_b", "ssd_conv_w", "ssd_conv_b",
           "ssd_dt_bias", "ssd_a_log", "ssd_d", "ssd_norm_w", "w_out", "norm_mix_post", "norm_ffn_pre", "w_gate", "w_up",
           "w_down", "norm_ffn_post")


def _slots_by_cols(a):
    rows, cols = a.shape
    return a.reshape(rows, N_CHIPS, cols // N_CHIPS).transpose(1, 0, 2)


def _cols_from_slots(a):
    n, rows, w = a.shape
    return a.transpose(1, 0, 2).reshape(rows, n * w)


def kernel(x, norm_mix_pre, w_in, conv_dw_w, conv_dw_b, conv_ln_g, conv_ln_b, ssd_conv_w, ssd_conv_b, ssd_dt_bias, ssd_a_log, ssd_d, ssd_norm_w, w_out, norm_mix_post, norm_ffn_pre, w_gate, w_up, w_down, norm_ffn_post, loss_target, m_norm_mix_pre, m_w_in, m_conv_dw_w, m_conv_dw_b, m_conv_ln_g, m_conv_ln_b, m_ssd_conv_w, m_ssd_conv_b, m_ssd_dt_bias, m_ssd_a_log, m_ssd_d, m_ssd_norm_w, m_w_out, m_norm_mix_post, m_norm_ffn_pre, m_w_gate, m_w_up, m_w_down, m_norm_ffn_post, v_norm_mix_pre, v_w_in, v_conv_dw_w, v_conv_dw_b, v_conv_ln_g, v_conv_ln_b, v_ssd_conv_w, v_ssd_conv_b, v_ssd_dt_bias, v_ssd_a_log, v_ssd_d, v_ssd_norm_w, v_w_out, v_norm_mix_post, v_norm_ffn_pre, v_w_gate, v_w_up, v_w_down, v_norm_ffn_post):
    args = dict(locals())
    two_d = lambda a: a.reshape(a.shape[-2:])
    wts = {n: two_d(args[n]) for n in WEIGHTS}
    ms = {n: two_d(args["m_" + n]) for n in WEIGHTS}
    vs = {n: two_d(args["v_" + n]) for n in WEIGHTS}
    nb, seq, d = x.shape
    t = nb * seq
    xi, yi, ci = _coords()
    chip = 2 * xi + yi
    where = jnp.stack([ci, chip]).astype(jnp.int32)

    gathered = _gather_weights(
        [wts[n].astype(BF16) for n in BIG],
        [_pad_rows(wts["conv_dw_w"], 32), _pad_rows(wts["ssd_conv_w"], 8)])
    g_in, g_out, g_gate, g_up, g_down, g_cw, g_sw = gathered
    w_in_full = _cols_from_slots(g_in)
    prm = {n: wts[n] for n in SMALL}
    prm.update(
        w_main=w_in_full[:, :D_MAIN], w_dt=_pad_lanes(w_in_full[:, D_MAIN:]),
        w_out=g_out.reshape(D_MODEL, D_MODEL), w_gate=_cols_from_slots(g_gate), w_up=_cols_from_slots(g_up),
        w_down=g_down.reshape(D_FF, D_MODEL), conv_dw_w=_cols_from_slots(g_cw), ssd_conv_w=_cols_from_slots(g_sw))

    loss, dx, grads = _local_grads(x.reshape(t, d), loss_target.reshape(t, d), prm, nb=nb, seq=seq)
    loss = lax.psum(loss[0, 0], MESH_AXES)

    slots = [_slots_by_cols(grads["w_in"]), grads["w_out"].reshape(N_CHIPS, D_MODEL // N_CHIPS, D_MODEL),
             grads["w_gate"], grads["w_up"], grads["w_down"].reshape(N_CHIPS, D_FF // N_CHIPS, D_MODEL)]
    lands = _swap_other_halves(slots)
    sums = [_add_core_halves(g, l, where) for g, l in zip(slots, lands)]
    recv = _send_to_owners([s[1] for s in sums])
    halves = [_add_chip_sums(s[0], r, where) for s, r in zip(sums, recv)]
    big_grads = dict(zip(BIG, _join_core_halves(halves)))

    small_shapes = [grads[n].shape for n in SMALL]
    small_sum = _unpack(_all_reduce_small(_pack([grads[n] for n in SMALL])), small_shapes)
    small_grads = dict(zip(SMALL, small_sum))
    cwid, swid = D_CONV // N_CHIPS, D_XBC // N_CHIPS
    small_grads["conv_dw_w"] = lax.dynamic_slice(small_grads["conv_dw_w"], (0, chip * cwid), (CONV_K, cwid))
    small_grads["ssd_conv_w"] = lax.dynamic_slice(small_grads["ssd_conv_w"], (0, chip * swid), (SSD_CONV_K, swid))

    out_g, out_d, out_m, out_v = {}, {}, {}, {}
    for n in BIG:
        out_g[n] = big_grads[n]
        out_d[n], out_m[n], out_v[n] = _adamw(wts[n], big_grads[n], ms[n], vs[n], name="adamw_" + n)
    shard_shapes = [wts[n].shape for n in SMALL]
    pd, pm, pv = _adamw(_pack([wts[n] for n in SMALL]), _pack([small_grads[n] for n in SMALL]),
                        _pack([ms[n] for n in SMALL]), _pack([vs[n] for n in SMALL]), name="adamw_small")
    for n, dd, mm, vv in zip(SMALL, _unpack(pd, shard_shapes), _unpack(pm, shard_shapes), _unpack(pv, shard_shapes)):
        out_g[n], out_d[n], out_m[n], out_v[n] = small_grads[n], dd, mm, vv

    outs = [o[n].reshape(args[n].shape) for o in (out_g, out_d, out_m, out_v) for n in WEIGHTS]
    return (loss, dx.reshape(nb, seq, d), *outs)
```

```python
import functools

import jax
import jax.numpy as jnp
from jax import lax
from jax.experimental import pallas as pl
from jax.experimental.pallas import tpu as pltpu

F32 = jnp.float32
BF16 = jnp.bfloat16
EPS = 1e-6

D_MODEL = 2048
D_CONV = 1024
D_SSD = 1024
D_XBC = 2048
HEADS = 16
HEAD_DIM = 64
GROUPS = 4
STATE = 128
CONV_K = 31
SSD_CONV_K = 4
D_FF = 5632
D_MAIN = 2 * D_CONV + D_SSD + D_XBC
D_IN = D_MAIN + HEADS
N_CHIPS = 4
LANES = 128
CHUNK = 128
PAIRS = HEADS // 2

ADAM_LR = 0.001
ADAM_B1 = 0.9
ADAM_B2 = 0.999
ADAM_EPS = 1e-08
ADAM_WD = 0.01
ADAM_STEP = 10

MESH_AXES = ("x", "y", "c")
VMEM_LIMIT = 56 * 1024 * 1024


def _sig(v):
    return 1.0 / (1.0 + jnp.exp(-v))


def _cparams(sem, vmem=VMEM_LIMIT):
    return pltpu.CompilerParams(dimension_semantics=sem, vmem_limit_bytes=vmem)


_DIMS = {"nn": ((1,), (0,)), "nt": ((1,), (1,)), "tn": ((0,), (0,))}


def _matmul(pairs, *, mode, out_dtype, tm, tn, tk, name, slot_out=False):
    a0, b0 = pairs[0]
    if mode == "nn":
        (m, k), n = a0.shape, b0.shape[1]
    elif mode == "nt":
        (m, k), n = a0.shape, b0.shape[0]
    else:
        (k, m), n = a0.shape, b0.shape[1]
    tm, tn, tk = min(tm, m), min(tn, n), min(tk, k)
    assert m % tm == 0 and n % tn == 0 and k % tk == 0, (name, m, n, k, tm, tn, tk)
    nk = k // tk
    npairs = len(pairs)
    dims = (_DIMS[mode], ((), ()))

    def body(*refs):
        ins, o_ref = refs[: 2 * npairs], refs[2 * npairs]
        part = None
        for p in range(npairs):
            d = lax.dot_general(ins[2 * p][...], ins[2 * p + 1][...], dims, preferred_element_type=F32)
            part = d if part is None else part + d
        if nk == 1:
            o_ref[...] = part.astype(out_dtype)
            return
        acc = refs[2 * npairs + 1]
        kk = pl.program_id(2)

        @pl.when(kk == 0)
        def _():
            acc[...] = part

        @pl.when(kk > 0)
        def _():
            acc[...] += part

        @pl.when(kk == nk - 1)
        def _():
            o_ref[...] = acc[...].astype(out_dtype)

    if mode == "nn":
        a_spec = pl.BlockSpec((tm, tk), lambda i, j, kk: (i, kk))
        b_spec = pl.BlockSpec((tk, tn), lambda i, j, kk: (kk, j))
    elif mode == "nt":
        a_spec = pl.BlockSpec((tm, tk), lambda i, j, kk: (i, kk))
        b_spec = pl.BlockSpec((tn, tk), lambda i, j, kk: (j, kk))
    else:
        a_spec = pl.BlockSpec((tk, tm), lambda i, j, kk: (kk, i))
        b_spec = pl.BlockSpec((tk, tn), lambda i, j, kk: (kk, j))
    if slot_out:
        out_shape = jax.ShapeDtypeStruct((n // tn, m, tn), out_dtype)
        out_spec = pl.BlockSpec((None, tm, tn), lambda i, j, kk: (j, i, 0))
    else:
        out_shape = jax.ShapeDtypeStruct((m, n), out_dtype)
        out_spec = pl.BlockSpec((tm, tn), lambda i, j, kk: (i, j))
    flat = [t for ab in pairs for t in ab]
    return pl.pallas_call(
        body,
        grid=(m // tm, n // tn, nk),
        in_specs=[a_spec, b_spec] * npairs,
        out_specs=out_spec,
        out_shape=out_shape,
        scratch_shapes=[] if nk == 1 else [pltpu.VMEM((tm, tn), F32)],
        compiler_params=_cparams(("parallel", "parallel", "arbitrary")),
        name=name,
    )(*flat)


def _ffn_up(h2, wg, wu, *, tm, tn):
    t, k = h2.shape
    n = wg.shape[1]
    tm = min(tm, t)
    assert t % tm == 0 and n % tn == 0, (t, n, tm, tn)

    def body(h_ref, wg_ref, wu_ref, g_ref, u_ref, a_ref):
        hv = h_ref[...]
        g = jnp.dot(hv, wg_ref[...], preferred_element_type=F32)
        u = jnp.dot(hv, wu_ref[...], preferred_element_type=F32)
        g_ref[...] = g.astype(BF16)
        u_ref[...] = u.astype(BF16)
        a_ref[...] = (g * _sig(g) * u).astype(BF16)

    o = jax.ShapeDtypeStruct((t, n), BF16)
    ospec = pl.BlockSpec((tm, tn), lambda i, j: (i, j))
    return pl.pallas_call(
        body,
        grid=(t // tm, n // tn),
        in_specs=[pl.BlockSpec((tm, k), lambda i, j: (i, 0)), pl.BlockSpec((k, tn), lambda i, j: (0, j)),
                  pl.BlockSpec((k, tn), lambda i, j: (0, j))],
        out_specs=[ospec, ospec, ospec],
        out_shape=[o, o, o],
        compiler_params=_cparams(("parallel", "parallel")),
        name="ffn_up",
    )(h2, wg, wu)


def _ffn_bwd_act(df, wd, gt, up, *, tm, tn):
    t, k = df.shape
    n = wd.shape[0]
    tm = min(tm, t)
    assert t % tm == 0 and n % tn == 0, (t, n, tm, tn)

    def body(df_ref, wd_ref, g_ref, u_ref, dg_ref, du_ref):
        da = lax.dot_general(df_ref[...], wd_ref[...], (_DIMS["nt"], ((), ())), preferred_element_type=F32)
        g = g_ref[...].astype(F32)
        u = u_ref[...].astype(F32)
        s = _sig(g)
        dg_ref[...] = (da * u * s * (1.0 + g * (1.0 - s))).astype(BF16)
        du_ref[...] = (da * g * s).astype(BF16)

    o = jax.ShapeDtypeStruct((t, n), BF16)
    blk = pl.BlockSpec((tm, tn), lambda i, j: (i, j))
    return pl.pallas_call(
        body,
        grid=(t // tm, n // tn),
        in_specs=[pl.BlockSpec((tm, k), lambda i, j: (i, 0)), pl.BlockSpec((tn, k), lambda i, j: (j, 0)), blk, blk],
        out_specs=[blk, blk],
        out_shape=[o, o],
        compiler_params=_cparams(("parallel", "parallel")),
        name="ffn_bwd_act",
    )(df, wd, gt, up)


ROW_TILE = 256


def _rms_fwd(xv, g, *, res=None, out_dtype, name):
    t, d = xv.shape
    has_res = res is not None

    def body(*refs):
        x_ref, g_ref = refs[0], refs[1]
        o_ref = refs[-1]
        v = x_ref[...]
        r = lax.rsqrt(jnp.mean(v * v, axis=-1, keepdims=True) + EPS)
        y = v * r * g_ref[...]
        if has_res:
            y = refs[2][...] + y
        o_ref[...] = y.astype(out_dtype)

    row = pl.BlockSpec((ROW_TILE, d), lambda i: (i, 0))
    vec = pl.BlockSpec((1, d), lambda i: (0, 0))
    return pl.pallas_call(
        body,
        grid=(t // ROW_TILE,),
        in_specs=[row, vec] + ([row] if has_res else []),
        out_specs=row,
        out_shape=jax.ShapeDtypeStruct((t, d), out_dtype),
        compiler_params=_cparams(("parallel",)),
        name=name,
    )(*([xv, g] + ([res] if has_res else [])))


def _rms_bwd(dys, xv, g, *, addend=None, out_dtype, name):
    t, d = xv.shape
    ndy = len(dys)
    has_add = addend is not None

    def body(*refs):
        dy_refs = refs[:ndy]
        x_ref, g_ref = refs[ndy], refs[ndy + 1]
        dx_ref, dg_ref = refs[-2], refs[-1]
        dy = dy_refs[0][...].astype(F32)
        for rr in dy_refs[1:]:
            dy = dy + rr[...].astype(F32)
        v = x_ref[...]
        r = lax.rsqrt(jnp.mean(v * v, axis=-1, keepdims=True) + EPS)
        xh = v * r
        gdy = dy * g_ref[...]
        dx = r * (gdy - xh * jnp.mean(gdy * xh, axis=-1, keepdims=True))
        if has_add:
            dx = dx + refs[ndy + 2][...]
        dx_ref[...] = dx.astype(out_dtype)

        @pl.when(pl.program_id(0) == 0)
        def _():
            dg_ref[...] = jnp.zeros_like(dg_ref)

        dg_ref[...] += jnp.sum(dy * xh, axis=0, keepdims=True)

    row = pl.BlockSpec((ROW_TILE, d), lambda i: (i, 0))
    vec = pl.BlockSpec((1, d), lambda i: (0, 0))
    return pl.pallas_call(
        body,
        grid=(t // ROW_TILE,),
        in_specs=[row] * ndy + [row, vec] + ([row] if has_add else []),
        out_specs=[row, vec],
        out_shape=[jax.ShapeDtypeStruct((t, d), out_dtype), jax.ShapeDtypeStruct((1, d), F32)],
        compiler_params=_cparams(("arbitrary",)),
        name=name,
    )(*(list(dys) + [xv, g] + ([addend] if has_add else [])))


def _ffn_post_loss(f, x1, tgt, g):
    t, d = f.shape

    def body(f_ref, x1_ref, t_ref, g_ref, dx2_ref, df_ref, loss_ref, dg_ref):
        v = f_ref[...]
        gv = g_ref[...]
        r = lax.rsqrt(jnp.mean(v * v, axis=-1, keepdims=True) + EPS)
        fh = v * r
        e = x1_ref[...] + fh * gv - t_ref[...]
        dx2 = e * (1.0 / d)
        dx2_ref[...] = dx2
        gdy = dx2 * gv
        df_ref[...] = (r * (gdy - fh * jnp.mean(gdy * fh, axis=-1, keepdims=True))).astype(BF16)

        @pl.when(pl.program_id(0) == 0)
        def _():
            dg_ref[...] = jnp.zeros_like(dg_ref)
            loss_ref[...] = jnp.zeros_like(loss_ref)

        dg_ref[...] += jnp.sum(dx2 * fh, axis=0, keepdims=True)
        per_tok = jnp.mean(e * e, axis=-1, keepdims=True)
        loss_ref[...] += 0.5 * jnp.sum(per_tok, axis=0, keepdims=True)

    row = pl.BlockSpec((ROW_TILE, d), lambda i: (i, 0))
    vec = pl.BlockSpec((1, d), lambda i: (0, 0))
    return pl.pallas_call(
        body,
        grid=(t // ROW_TILE,),
        in_specs=[row, row, row, vec],
        out_specs=[row, row, pl.BlockSpec((1, 1), lambda i: (0, 0)), vec],
        out_shape=[jax.ShapeDtypeStruct((t, d), F32), jax.ShapeDtypeStruct((t, d), BF16),
                   jax.ShapeDtypeStruct((1, 1), F32), jax.ShapeDtypeStruct((1, d), F32)],
        compiler_params=_cparams(("arbitrary",)),
        name="ffn_post_loss",
    )(f, x1, tgt, g)


CONV_ROWS = 256
TAP_ROWS = 64
HALO31 = 32
HALO4 = 8


def _sum8(v):
    return jnp.sum(v.reshape(v.shape[0] // 8, 8, v.shape[1]), axis=0)


def _conv_branch_fwd(proj, cw, cb, lg, lb, *, nb, seq):
    ts, c, halo = CONV_ROWS, D_CONV, HALO31
    ns = seq // ts
    base = halo - CONV_K + 1

    def body(ca_ref, cg_ref, w_ref, b_ref, lg_ref, lb_ref, u1_ref, u_ref, ubuf):
        i = pl.program_id(1)

        @pl.when(i == 0)
        def _():
            ubuf[0:halo, :] = jnp.zeros((halo, c), F32)

        @pl.when(i > 0)
        def _():
            ubuf[0:halo, :] = ubuf[ts:ts + halo, :]

        ubuf[halo:halo + ts, :] = ca_ref[...] * _sig(cg_ref[...])

        def lane_tile(j, carry):
            ln = pl.ds(pl.multiple_of(j * LANES, LANES), LANES)
            for r in range(ts // TAP_ROWS):
                acc = jnp.broadcast_to(b_ref[:, ln], (TAP_ROWS, LANES))
                for k in range(CONV_K):
                    acc = acc + w_ref[pl.ds(k, 1), ln] * ubuf[pl.ds(r * TAP_ROWS + base + k, TAP_ROWS), ln]
                u1_ref[pl.ds(r * TAP_ROWS, TAP_ROWS), ln] = acc
            return carry

        lax.fori_loop(0, c // LANES, lane_tile, 0)
        v = u1_ref[...]
        mu = jnp.mean(v, axis=-1, keepdims=True)
        dv = v - mu
        xh = dv * lax.rsqrt(jnp.mean(dv * dv, axis=-1, keepdims=True) + EPS)
        u2 = xh * lg_ref[...] + lb_ref[...]
        u_ref[...] = (u2 * _sig(u2)).astype(BF16)

    t = nb * seq
    row = lambda col: pl.BlockSpec((ts, c), lambda b, i: (b * ns + i, col))
    vec = pl.BlockSpec((1, c), lambda b, i: (0, 0))
    return pl.pallas_call(
        body,
        grid=(nb, ns),
        in_specs=[row(0), row(1), pl.BlockSpec((32, c), lambda b, i: (0, 0)), vec, vec, vec],
        out_specs=[row(0), row(0)],
        out_shape=[jax.ShapeDtypeStruct((t, c), F32), jax.ShapeDtypeStruct((t, c), BF16)],
        scratch_shapes=[pltpu.VMEM((halo + ts, c), F32)],
        compiler_params=_cparams(("parallel", "arbitrary")),
        name="conv_branch_fwd",
    )(proj, proj, cw, cb, lg, lb)


def _conv_branch_bwd(duy, u1, proj, cw, lg, lb, *, nb, seq):
    ts, c, halo = CONV_ROWS, D_CONV, HALO31
    ns = seq // ts
    base = halo - CONV_K + 1
    hb = ts // halo

    def body(du_ref, u1_ref, ca_ref, cg_ref, cah_ref, cgh_ref, w_ref, lg_ref, lb_ref,
             dcacg_ref, dw_ref, db_ref, dlg_ref, dlb_ref,
             ubuf, dbuf, du0buf, dwacc, dbacc, dlgacc, dlbacc):
        b, i = pl.program_id(0), pl.program_id(1)
        rc = ns - 1 - i

        @pl.when(jnp.logical_and(b == 0, i == 0))
        def _():
            dwacc[...] = jnp.zeros_like(dwacc)
            dbacc[...] = jnp.zeros_like(dbacc)
            dlgacc[...] = jnp.zeros_like(dlgacc)
            dlbacc[...] = jnp.zeros_like(dlbacc)

        @pl.when(i == 0)
        def _():
            dbuf[ts:ts + halo, :] = jnp.zeros((halo, c), F32)

        @pl.when(i > 0)
        def _():
            dbuf[ts:ts + halo, :] = dbuf[0:halo, :]

        v = u1_ref[...]
        mu = jnp.mean(v, axis=-1, keepdims=True)
        dv = v - mu
        rstd = lax.rsqrt(jnp.mean(dv * dv, axis=-1, keepdims=True) + EPS)
        xh = dv * rstd
        lgv = lg_ref[...]
        u2 = xh * lgv + lb_ref[...]
        s2 = _sig(u2)
        du2 = du_ref[...] * (s2 * (1.0 + u2 * (1.0 - s2)))
        dlgacc[...] += jnp.sum(du2 * xh, axis=0, keepdims=True)
        dlbacc[...] += jnp.sum(du2, axis=0, keepdims=True)
        gd = du2 * lgv
        du1 = rstd * (gd - jnp.mean(gd, axis=-1, keepdims=True) - xh * jnp.mean(gd * xh, axis=-1, keepdims=True))
        dbacc[...] += jnp.sum(du1, axis=0, keepdims=True)
        dbuf[0:ts, :] = du1

        @pl.when(rc == 0)
        def _():
            ubuf[0:halo, :] = jnp.zeros((halo, c), F32)

        @pl.when(rc > 0)
        def _():
            ubuf[0:halo, :] = cah_ref[...] * _sig(cgh_ref[...])

        cav = ca_ref[...]
        sg = _sig(cg_ref[...])
        ubuf[halo:halo + ts, :] = cav * sg

        def lane_tile(j, carry):
            ln = pl.ds(pl.multiple_of(j * LANES, LANES), LANES)
            for r in range(ts // TAP_ROWS):
                r0 = r * TAP_ROWS
                d1 = dbuf[pl.ds(r0, TAP_ROWS), ln]
                acc = jnp.zeros((TAP_ROWS, LANES), F32)
                for k in range(CONV_K):
                    acc = acc + w_ref[pl.ds(k, 1), ln] * dbuf[pl.ds(r0 + CONV_K - 1 - k, TAP_ROWS), ln]
                    dwacc[pl.ds(k * 8, 8), ln] += _sum8(d1 * ubuf[pl.ds(r0 + base + k, TAP_ROWS), ln])
                du0buf[pl.ds(r0, TAP_ROWS), ln] = acc
            return carry

        lax.fori_loop(0, c // LANES, lane_tile, 0)
        du0 = du0buf[...]
        dcacg_ref[:, 0:c] = (du0 * sg).astype(BF16)
        dcacg_ref[:, c:2 * c] = (du0 * cav * sg * (1.0 - sg)).astype(BF16)

        @pl.when(jnp.logical_and(b == nb - 1, i == ns - 1))
        def _():
            for k in range(CONV_K):
                dw_ref[pl.ds(k, 1), :] = jnp.sum(dwacc[pl.ds(k * 8, 8), :], axis=0, keepdims=True)
            dw_ref[pl.ds(CONV_K, 1), :] = jnp.zeros((1, c), F32)
            db_ref[...] = dbacc[...]
            dlg_ref[...] = dlgacc[...]
            dlb_ref[...] = dlbacc[...]

    t = nb * seq
    rowblk = lambda b, i: b * ns + (ns - 1 - i)
    row = lambda col: pl.BlockSpec((ts, c), lambda b, i: (rowblk(b, i), col))
    hrow = lambda col: pl.BlockSpec((halo, c), lambda b, i: (jnp.maximum(rowblk(b, i) * hb - 1, 0), col))
    vec = pl.BlockSpec((1, c), lambda b, i: (0, 0))
    wspec = pl.BlockSpec((32, c), lambda b, i: (0, 0))
    return pl.pallas_call(
        body,
        grid=(nb, ns),
        in_specs=[row(0), row(0), row(0), row(1), hrow(0), hrow(1), wspec, vec, vec],
        out_specs=[pl.BlockSpec((ts, 2 * c), lambda b, i: (rowblk(b, i), 0)), wspec, vec, vec, vec],
        out_shape=[jax.ShapeDtypeStruct((t, 2 * c), BF16), jax.ShapeDtypeStruct((32, c), F32),
                   jax.ShapeDtypeStruct((1, c), F32), jax.ShapeDtypeStruct((1, c), F32), jax.ShapeDtypeStruct((1, c), F32)],
        scratch_shapes=[pltpu.VMEM((halo + ts, c), F32), pltpu.VMEM((ts + halo, c), F32), pltpu.VMEM((ts, c), F32),
                        pltpu.VMEM((CONV_K * 8, c), F32), pltpu.VMEM((1, c), F32), pltpu.VMEM((1, c), F32),
                        pltpu.VMEM((1, c), F32)],
        compiler_params=_cparams(("arbitrary", "arbitrary")),
        name="conv_branch_bwd",
    )(duy, u1, proj, proj, proj, proj, cw, lg, lb)


XBC_COL0 = (2 * D_CONV + D_SSD) // 1024


def _ssd_pre_fwd(proj, sw, sb, *, nb, seq):
    ts, c, halo = CONV_ROWS, 1024, HALO4
    ns = seq // ts
    base = halo - SSD_CONV_K + 1

    def body(x_ref, w_ref, b_ref, o_ref, xbuf):
        i = pl.program_id(2)

        @pl.when(i == 0)
        def _():
            xbuf[0:halo, :] = jnp.zeros((halo, c), F32)

        @pl.when(i > 0)
        def _():
            xbuf[0:halo, :] = xbuf[ts:ts + halo, :]

        xbuf[halo:halo + ts, :] = x_ref[...]

        def lane_tile(j, carry):
            ln = pl.ds(pl.multiple_of(j * LANES, LANES), LANES)
            for r in range(ts // TAP_ROWS):
                acc = jnp.broadcast_to(b_ref[:, ln], (TAP_ROWS, LANES))
                for k in range(SSD_CONV_K):
                    acc = acc + w_ref[pl.ds(k, 1), ln] * xbuf[pl.ds(r * TAP_ROWS + base + k, TAP_ROWS), ln]
                o_ref[pl.ds(r * TAP_ROWS, TAP_ROWS), ln] = acc * _sig(acc)
            return carry

        lax.fori_loop(0, c // LANES, lane_tile, 0)

    t = nb * seq
    return pl.pallas_call(
        body,
        grid=(2, nb, ns),
        in_specs=[pl.BlockSpec((ts, c), lambda j, b, i: (b * ns + i, XBC_COL0 + j)),
                  pl.BlockSpec((8, c), lambda j, b, i: (0, j)), pl.BlockSpec((1, c), lambda j, b, i: (0, j))],
        out_specs=pl.BlockSpec((ts, c), lambda j, b, i: (b * ns + i, j)),
        out_shape=jax.ShapeDtypeStruct((t, D_XBC), F32),
        scratch_shapes=[pltpu.VMEM((halo + ts, c), F32)],
        compiler_params=_cparams(("parallel", "parallel", "arbitrary")),
        name="ssd_pre_fwd",
    )(proj, sw, sb)


def _ssd_pre_bwd(dxs, proj, sw, sb, *, nb, seq):
    ts, c, halo = CONV_ROWS, 1024, HALO4
    ns = seq // ts
    base = halo - SSD_CONV_K + 1
    hb = ts // halo

    def body(d_ref, x_ref, xh_ref, w_ref, b_ref, dx_ref, dw_ref, db_ref, xbuf, dbuf, dwacc, dbacc):
        b, i = pl.program_id(1), pl.program_id(2)
        rc = ns - 1 - i

        @pl.when(jnp.logical_and(b == 0, i == 0))
        def _():
            dwacc[...] = jnp.zeros_like(dwacc)
            dbacc[...] = jnp.zeros_like(dbacc)

        @pl.when(i == 0)
        def _():
            dbuf[ts:ts + halo, :] = jnp.zeros((halo, c), F32)

        @pl.when(i > 0)
        def _():
            dbuf[ts:ts + halo, :] = dbuf[0:halo, :]

        @pl.when(rc == 0)
        def _():
            xbuf[0:halo, :] = jnp.zeros((halo, c), F32)

        @pl.when(rc > 0)
        def _():
            xbuf[0:halo, :] = xh_ref[...]

        xbuf[halo:halo + ts, :] = x_ref[...]

        def pre_tile(j, carry):
            ln = pl.ds(pl.multiple_of(j * LANES, LANES), LANES)
            for r in range(ts // TAP_ROWS):
                r0 = r * TAP_ROWS
                acc = jnp.broadcast_to(b_ref[:, ln], (TAP_ROWS, LANES))
                for k in range(SSD_CONV_K):
                    acc = acc + w_ref[pl.ds(k, 1), ln] * xbuf[pl.ds(r0 + base + k, TAP_ROWS), ln]
                s = _sig(acc)
                dc = d_ref[pl.ds(r0, TAP_ROWS), ln] * (s * (1.0 + acc * (1.0 - s)))
                dbuf[pl.ds(r0, TAP_ROWS), ln] = dc
                dbacc[:, ln] += _sum8(dc)
            return carry

        lax.fori_loop(0, c // LANES, pre_tile, 0)

        def lane_tile(j, carry):
            ln = pl.ds(pl.multiple_of(j * LANES, LANES), LANES)
            for r in range(ts // TAP_ROWS):
                r0 = r * TAP_ROWS
                d1 = dbuf[pl.ds(r0, TAP_ROWS), ln]
                acc = jnp.zeros((TAP_ROWS, LANES), F32)
                for k in range(SSD_CONV_K):
                    acc = acc + w_ref[pl.ds(k, 1), ln] * dbuf[pl.ds(r0 + SSD_CONV_K - 1 - k, TAP_ROWS), ln]
                    dwacc[pl.ds(k * 8, 8), ln] += _sum8(d1 * xbuf[pl.ds(r0 + base + k, TAP_ROWS), ln])
                dx_ref[pl.ds(r0, TAP_ROWS), ln] = acc.astype(BF16)
            return carry

        lax.fori_loop(0, c // LANES, lane_tile, 0)

        @pl.when(jnp.logical_and(b == nb - 1, i == ns - 1))
        def _():
            for k in range(SSD_CONV_K):
                dw_ref[pl.ds(k, 1), :] = jnp.sum(dwacc[pl.ds(k * 8, 8), :], axis=0, keepdims=True)
            dw_ref[pl.ds(SSD_CONV_K, 8 - SSD_CONV_K), :] = jnp.zeros((8 - SSD_CONV_K, c), F32)
            db_ref[...] = jnp.sum(dbacc[...], axis=0, keepdims=True)

    t = nb * seq
    rowblk = lambda b, i: b * ns + (ns - 1 - i)
    return pl.pallas_call(
        body,
        grid=(2, nb, ns),
        in_specs=[pl.BlockSpec((ts, c), lambda j, b, i: (rowblk(b, i), j)),
                  pl.BlockSpec((ts, c), lambda j, b, i: (rowblk(b, i), XBC_COL0 + j)),
                  pl.BlockSpec((halo, c), lambda j, b, i: (jnp.maximum(rowblk(b, i) * hb - 1, 0), XBC_COL0 + j)),
                  pl.BlockSpec((8, c), lambda j, b, i: (0, j)), pl.BlockSpec((1, c), lambda j, b, i: (0, j))],
        out_specs=[pl.BlockSpec((ts, c), lambda j, b, i: (rowblk(b, i), j)),
                   pl.BlockSpec((8, c), lambda j, b, i: (0, j)), pl.BlockSpec((1, c), lambda j, b, i: (0, j))],
        out_shape=[jax.ShapeDtypeStruct((t, D_XBC), BF16), jax.ShapeDtypeStruct((8, D_XBC), F32),
                   jax.ShapeDtypeStruct((1, D_XBC), F32)],
        scratch_shapes=[pltpu.VMEM((halo + ts, c), F32), pltpu.VMEM((ts + halo, c), F32),
                        pltpu.VMEM((SSD_CONV_K * 8, c), F32), pltpu.VMEM((8, c), F32)],
        compiler_params=_cparams(("arbitrary", "arbitrary", "arbitrary")),
        name="ssd_pre_bwd",
    )(dxs, proj, proj, sw, sb)


Z_COL = (2 * D_CONV) // 1024
GROUP_W = D_SSD // GROUPS


def _softplus(v):
    return jnp.maximum(v, 0.0) + jnp.log(1.0 + jnp.exp(-jnp.abs(v)))


def _dot(a, b):
    return jnp.dot(a, b, preferred_element_type=F32)


def _dot_nt(a, b):
    return lax.dot_general(a, b, (_DIMS["nt"], ((), ())), preferred_element_type=F32)


def _dot_tn(a, b):
    return lax.dot_general(a, b, (_DIMS["tn"], ((), ())), preferred_element_type=F32)


def _dot_exact(a, b):
    return jnp.dot(a, b, precision=lax.Precision.HIGHEST, preferred_element_type=F32)


def _chunk_decays(dtr_ref, bias_ref, alog_ref):
    q = CHUNK
    ii = lax.broadcasted_iota(jnp.int32, (q, q), 0)
    jj = lax.broadcasted_iota(jnp.int32, (q, q), 1)
    tri = jj <= ii
    dt = _softplus(dtr_ref[...] + bias_ref[...])
    a_head = -jnp.exp(alog_ref[...])
    cs = _dot_exact(tri.astype(F32), dt * a_head)
    return tri, dt, a_head, cs, cs.T


def _ssd_fwd(xs_all, proj, dtr, dt_bias, a_log, d_lanes, norm_w, *, nb, seq):
    q = CHUNK
    nc = seq // q
    t = nb * seq

    def body(xs_ref, bm_ref, cm_ref, z_ref, dtr_ref, bias_ref, alog_ref, dl_ref, nw_ref,
             y_ref, ys_ref, st_ref, state):
        @pl.when(pl.program_id(1) == 0)
        def _():
            state[...] = jnp.zeros_like(state)

        tri, dt, _, cs, cst = _chunk_decays(dtr_ref, bias_ref, alog_ref)
        first = lax.broadcasted_iota(jnp.int32, (1, LANES), 1) < HEAD_DIM
        for g in range(GROUPS):
            gl = slice(g * STATE, (g + 1) * STATE)
            bb = bm_ref[:, gl].astype(BF16)
            cb = cm_ref[:, gl].astype(BF16)
            scores = _dot_nt(cb, bb)
            for p in range(2):
                pr = 2 * g + p
                h0 = 2 * pr
                sl = slice(pr * LANES, (pr + 1) * LANES)
                xv = xs_ref[:, sl]
                dtp = jnp.where(first, dt[:, h0:h0 + 1], dt[:, h0 + 1:h0 + 2])
                csp = jnp.where(first, cs[:, h0:h0 + 1], cs[:, h0 + 1:h0 + 2])
                xd = xv * dtp
                yv = None
                for hh, keep in ((h0, first), (h0 + 1, jnp.logical_not(first))):
                    decay = jnp.where(tri, jnp.exp(cs[:, hh:hh + 1] - cst[hh:hh + 1, :]), 0.0)
                    part = _dot((scores * decay).astype(BF16), jnp.where(keep, xd, 0.0).astype(BF16))
                    yv = part if yv is None else yv + part
                hp = state[pr]
                st_ref[0, pr] = hp
                yv = yv + jnp.exp(csp) * _dot(cb, hp.astype(BF16))
                last = csp[q - 1:q, :]
                state[pr] = jnp.exp(last) * hp + _dot_tn(bb, (xd * jnp.exp(last - csp)).astype(BF16))
                ys_ref[:, sl] = yv + dl_ref[:, sl] * xv
        zv = z_ref[...]
        gated = ys_ref[...] * (zv * _sig(zv))
        for g in range(GROUPS):
            gl = slice(g * GROUP_W, (g + 1) * GROUP_W)
            v = gated[:, gl]
            r = lax.rsqrt(jnp.mean(v * v, axis=-1, keepdims=True) + EPS)
            y_ref[:, gl] = (v * r * nw_ref[:, gl]).astype(BF16)

    blk = lambda w, col: pl.BlockSpec((q, w), lambda b, c: (b * nc + c, col))
    vec = lambda w: pl.BlockSpec((1, w), lambda b, c: (0, 0))
    return pl.pallas_call(
        body,
        grid=(nb, nc),
        in_specs=[blk(D_SSD, 0), blk(GROUPS * STATE, 2), blk(GROUPS * STATE, 3), blk(D_SSD, Z_COL), blk(LANES, 0),
                  vec(LANES), vec(LANES), vec(D_SSD), vec(D_SSD)],
        out_specs=[blk(D_SSD, 0), blk(D_SSD, 0),
                   pl.BlockSpec((1, PAIRS, STATE, LANES), lambda b, c: (b * nc + c, 0, 0, 0))],
        out_shape=[jax.ShapeDtypeStruct((t, D_SSD), BF16), jax.ShapeDtypeStruct((t, D_SSD), F32),
                   jax.ShapeDtypeStruct((nb * nc, PAIRS, STATE, LANES), F32)],
        scratch_shapes=[pltpu.VMEM((PAIRS, STATE, LANES), F32)],
        compiler_params=_cparams(("parallel", "arbitrary")),
        name="ssd_fwd",
    )(xs_all, xs_all, xs_all, proj, dtr, dt_bias, a_log, d_lanes, norm_w)


def _ssd_bwd(duy, proj, ys, xs_all, dtr, states, dt_bias, a_log, d_lanes, norm_w, *, nb, seq):
    q = CHUNK
    nc = seq // q
    t = nb * seq

    def body(dy_ref, z_ref, ys_ref, xs_ref, bm_ref, cm_ref, dtr_ref, st_ref, bias_ref, alog_ref, dl_ref, nw_ref,
             dz_ref, dx_ref, ddtr_ref, small_ref,
             dstate, dys_buf, dcsl, ddtl, dcst, dnw_acc, dd_acc, dbias_acc, da_acc):
        b, c = pl.program_id(0), pl.program_id(1)

        @pl.when(jnp.logical_and(b == 0, c == 0))
        def _():
            dnw_acc[...] = jnp.zeros_like(dnw_acc)
            dd_acc[...] = jnp.zeros_like(dd_acc)
            dbias_acc[...] = jnp.zeros_like(dbias_acc)
            da_acc[...] = jnp.zeros_like(da_acc)
            dcst[...] = jnp.zeros_like(dcst)

        @pl.when(c == 0)
        def _():
            dstate[...] = jnp.zeros_like(dstate)

        zv = z_ref[...]
        sz = _sig(zv)
        silz = zv * sz
        ysv = ys_ref[...]
        gated = ysv * silz
        dyv = dy_ref[...]
        nwv = nw_ref[...]
        for g in range(GROUPS):
            gl = slice(g * GROUP_W, (g + 1) * GROUP_W)
            v = gated[:, gl]
            r = lax.rsqrt(jnp.mean(v * v, axis=-1, keepdims=True) + EPS)
            yn = v * r
            dyn = dyv[:, gl] * nwv[:, gl]
            dnw_acc[:, gl] += jnp.sum(dyv[:, gl] * yn, axis=0, keepdims=True)
            dys_buf[:, gl] = r * (dyn - yn * jnp.mean(dyn * yn, axis=-1, keepdims=True))
        dgated = dys_buf[...]
        dz_ref[...] = (dgated * ysv * (sz * (1.0 + zv * (1.0 - sz)))).astype(BF16)
        dys_all = dgated * silz
        dys_buf[...] = dys_all
        dd_acc[...] += jnp.sum(dys_all * xs_ref[...], axis=0, keepdims=True)

        tri, dt, a_head, cs, cst = _chunk_decays(dtr_ref, bias_ref, alog_ref)
        lane = lax.broadcasted_iota(jnp.int32, (1, LANES), 1)
        first = lane < HEAD_DIM
        dcs_h = jnp.zeros((q, LANES), F32)
        for g in range(GROUPS):
            gl = slice(g * STATE, (g + 1) * STATE)
            bb = bm_ref[:, gl].astype(BF16)
            cb = cm_ref[:, gl].astype(BF16)
            scores = _dot_nt(cb, bb)
            dscores = jnp.zeros((q, q), F32)
            dbg = jnp.zeros((q, STATE), F32)
            dcg = jnp.zeros((q, STATE), F32)
            for p in range(2):
                pr = 2 * g + p
                h0 = 2 * pr
                sl = slice(pr * LANES, (pr + 1) * LANES)
                xv = xs_ref[:, sl]
                dyp = dys_buf[:, sl]
                dtp = jnp.where(first, dt[:, h0:h0 + 1], dt[:, h0 + 1:h0 + 2])
                csp = jnp.where(first, cs[:, h0:h0 + 1], cs[:, h0 + 1:h0 + 2])
                xd = xv * dtp
                xdb = xd.astype(BF16)
                hp = st_ref[0, pr]
                dhn = dstate[pr]
                hpb = hp.astype(BF16)
                dhnb = dhn.astype(BF16)
                lam = jnp.exp(csp)
                last = csp[q - 1:q, :]
                gam = jnp.exp(last)
                w = jnp.exp(last - csp)
                dxd = jnp.zeros((q, LANES), F32)
                for hh, keep in ((h0, first), (h0 + 1, jnp.logical_not(first))):
                    decay = jnp.where(tri, jnp.exp(cs[:, hh:hh + 1] - cst[hh:hh + 1, :]), 0.0)
                    m = scores * decay
                    dym = jnp.where(keep, dyp, 0.0).astype(BF16)
                    dm = _dot_nt(dym, xdb)
                    dxd = dxd + _dot_tn(m.astype(BF16), dym)
                    e = dm * m
                    dcs_h = dcs_h + jnp.where(lane == hh, jnp.sum(e, axis=1, keepdims=True), 0.0)
                    dcst[hh:hh + 1, :] = jnp.sum(e, axis=0, keepdims=True)
                    dscores = dscores + dm * decay
                yoff = lam * _dot(cb, hpb)
                ldy = (lam * dyp).astype(BF16)
                dcg = dcg + _dot_nt(ldy, hpb)
                dstate[pr] = gam * dhn + _dot_tn(cb, ldy)
                bdh = _dot(bb, dhnb)
                dxd = dxd + w * bdh
                xdw = xd * w
                dbg = dbg + _dot_nt(xdw.astype(BF16), dhnb)
                wd = xdw * bdh
                dcsl[:, sl] = dyp * yoff - wd
                dcsl[q - 1:q, sl] += (jnp.sum(wd, axis=0, keepdims=True)
                                      + gam * jnp.sum(dhn * hp, axis=0, keepdims=True))
                dx_ref[:, sl] = dxd * dtp + dyp * dl_ref[:, sl]
                ddtl[:, sl] = dxd * xv
            dsb = dscores.astype(BF16)
            dx_ref[:, D_SSD + g * STATE:D_SSD + (g + 1) * STATE] = dbg + _dot_tn(dsb, cb)
            dx_ref[:, D_SSD + (GROUPS + g) * STATE:D_SSD + (GROUPS + g + 1) * STATE] = dcg + _dot(dsb, bb)

        li = lax.broadcasted_iota(jnp.int32, (D_SSD, LANES), 0)
        hi = lax.broadcasted_iota(jnp.int32, (D_SSD, LANES), 1)
        sel = (li // HEAD_DIM == hi).astype(F32)
        dcs_h = dcs_h + _dot_exact(dcsl[...], sel) - dcst[...].T
        ddt = _dot_exact(ddtl[...], sel)
        upper = lax.broadcasted_iota(jnp.int32, (q, q), 1) >= lax.broadcasted_iota(jnp.int32, (q, q), 0)
        da = _dot_exact(upper.astype(F32), dcs_h)
        ddt = ddt + da * a_head
        da_acc[...] += jnp.sum(da * dt, axis=0, keepdims=True)
        ddtr = ddt * _sig(dtr_ref[...] + bias_ref[...])
        ddtr_ref[...] = ddtr
        dbias_acc[...] += jnp.sum(ddtr, axis=0, keepdims=True)

        @pl.when(jnp.logical_and(b == nb - 1, c == nc - 1))
        def _():
            small_ref[...] = jnp.zeros_like(small_ref)
            small_ref[0:1, :] = dnw_acc[...]
            small_ref[1:2, 0:LANES] = _dot_exact(jnp.broadcast_to(dd_acc[...], (8, D_SSD)), sel)[0:1, :]
            small_ref[2:3, 0:LANES] = dbias_acc[...]
            small_ref[3:4, 0:LANES] = da_acc[...] * a_head

    rowblk = lambda b, c: b * nc + (nc - 1 - c)
    blk = lambda w, col: pl.BlockSpec((q, w), lambda b, c: (rowblk(b, c), col))
    vec = lambda w: pl.BlockSpec((1, w), lambda b, c: (0, 0))
    return pl.pallas_call(
        body,
        grid=(nb, nc),
        in_specs=[blk(D_SSD, 1), blk(D_SSD, Z_COL), blk(D_SSD, 0), blk(D_SSD, 0), blk(GROUPS * STATE, 2),
                  blk(GROUPS * STATE, 3), blk(LANES, 0),
                  pl.BlockSpec((1, PAIRS, STATE, LANES), lambda b, c: (rowblk(b, c), 0, 0, 0)),
                  vec(LANES), vec(LANES), vec(D_SSD), vec(D_SSD)],
        out_specs=[blk(D_SSD, 0), blk(D_XBC, 0), blk(LANES, 0), pl.BlockSpec((8, D_SSD), lambda b, c: (0, 0))],
        out_shape=[jax.ShapeDtypeStruct((t, D_SSD), BF16), jax.ShapeDtypeStruct((t, D_XBC), F32),
                   jax.ShapeDtypeStruct((t, LANES), F32), jax.ShapeDtypeStruct((8, D_SSD), F32)],
        scratch_shapes=[pltpu.VMEM((PAIRS, STATE, LANES), F32), pltpu.VMEM((q, D_SSD), F32),
                        pltpu.VMEM((q, D_SSD), F32), pltpu.VMEM((q, D_SSD), F32), pltpu.VMEM((LANES, q), F32),
                        pltpu.VMEM((1, D_SSD), F32), pltpu.VMEM((1, D_SSD), F32), pltpu.VMEM((1, LANES), F32),
                        pltpu.VMEM((1, LANES), F32)],
        compiler_params=_cparams(("arbitrary", "arbitrary")),
        name="ssd_bwd",
    )(duy, proj, ys, xs_all, xs_all, xs_all, dtr, states, dt_bias, a_log, d_lanes, norm_w)


HBM_SPEC = pl.BlockSpec(memory_space=pltpu.HBM)
MESH_ID = pl.DeviceIdType.MESH


def _coords():
    return lax.axis_index("x"), lax.axis_index("y"), lax.axis_index("c")


def _chip_peer(xi, yi, ci, d):
    return (jnp.bitwise_xor(xi, d >> 1), jnp.bitwise_xor(yi, d & 1), ci)


def _remote(src, dst, send_sem, recv_sem, peer):
    return pltpu.make_async_remote_copy(src_ref=src, dst_ref=dst, send_sem=send_sem, recv_sem=recv_sem,
                                        device_id=peer, device_id_type=MESH_ID)


def _gather_weights(big, small):
    nbig, nsm = len(big), len(small)
    ntot = nbig + nsm

    def body(*refs):
        ins, outs = refs[:ntot], refs[ntot:2 * ntot]
        send_sems, recv_sems = refs[2 * ntot:]
        xi, yi, ci = _coords()
        me = 2 * xi + yi
        sibling = (xi, yi, 1 - ci)
        sends = []
        for t in range(nbig):
            hr = big[t].shape[0] // 2
            mine = pl.ds(ci * hr, hr)
            for d in (1, 2, 3):
                k = 6 * t + d - 1
                cp = _remote(ins[t].at[mine, :], outs[t].at[me, mine, :], send_sems.at[k], recv_sems.at[k],
                             _chip_peer(xi, yi, ci, d))
                cp.start()
                sends.append(cp)
        for s in range(nsm):
            t = nbig + s
            for d in (1, 2, 3):
                k = 6 * nbig + 3 * s + d - 1
                cp = _remote(ins[t], outs[t].at[me], send_sems.at[k], recv_sems.at[k], _chip_peer(xi, yi, ci, d))
                cp.start()
                sends.append(cp)
        for t in range(nbig):
            hr = big[t].shape[0] // 2
            mine = pl.ds(ci * hr, hr)
            for d in (1, 2, 3):
                k = 6 * t + d - 1
                src_chip = jnp.bitwise_xor(me, d)
                landed = outs[t].at[src_chip, mine, :]
                _remote(landed, landed, send_sems.at[k], recv_sems.at[k], sibling).wait_recv()
                cp = _remote(landed, landed, send_sems.at[k + 3], recv_sems.at[k + 3], sibling)
                cp.start()
                sends.append(cp)
        for t in range(nbig):
            hr = big[t].shape[0] // 2
            theirs = pl.ds((1 - ci) * hr, hr)
            for d in (1, 2, 3):
                k = 6 * t + 3 + d - 1
                landed = outs[t].at[jnp.bitwise_xor(me, d), theirs, :]
                _remote(landed, landed, send_sems.at[k], recv_sems.at[k], sibling).wait_recv()
        for s in range(nsm):
            t = nbig + s
            for d in (1, 2, 3):
                k = 6 * nbig + 3 * s + d - 1
                landed = outs[t].at[jnp.bitwise_xor(me, d)]
                _remote(landed, landed, send_sems.at[k], recv_sems.at[k], sibling).wait_recv()
        for cp in sends:
            cp.wait_send()

    nsem = 6 * nbig + 3 * nsm
    allin = list(big) + list(small)
    return pl.pallas_call(
        body,
        in_specs=[HBM_SPEC] * ntot,
        out_specs=[HBM_SPEC] * ntot,
        out_shape=[jax.ShapeDtypeStruct((N_CHIPS,) + a.shape, a.dtype) for a in allin],
        scratch_shapes=[pltpu.SemaphoreType.DMA((nsem,)), pltpu.SemaphoreType.DMA((nsem,))],
        name="gather_weights",
    )(*allin)


def _swap_other_halves(gs):
    n = len(gs)

    def body(*refs):
        ins, lands = refs[:n], refs[n:2 * n]
        send_sems, recv_sems = refs[2 * n:]
        xi, yi, ci = _coords()
        sibling = (xi, yi, 1 - ci)
        cps = []
        for t in range(n):
            hr = gs[t].shape[1] // 2
            cp = _remote(ins[t].at[:, pl.ds((1 - ci) * hr, hr), :], lands[t], send_sems.at[t], recv_sems.at[t], sibling)
            cp.start()
            cps.append(cp)
        for cp in cps:
            cp.wait_recv()
        for cp in cps:
            cp.wait_send()

    return pl.pallas_call(
        body,
        in_specs=[HBM_SPEC] * n,
        out_specs=[HBM_SPEC] * n,
        out_shape=[jax.ShapeDtypeStruct((g.shape[0], g.shape[1] // 2, g.shape[2]), g.dtype) for g in gs],
        scratch_shapes=[pltpu.SemaphoreType.DMA((n,)), pltpu.SemaphoreType.DMA((n,))],
        name="swap_other_halves",
    )(*gs)


def _row_tile(rows, cap=512, mult=16):
    best = mult
    for cand in range(mult, min(rows, cap) + 1, mult):
        if rows % cand == 0:
            best = cand
    assert rows % best == 0, rows
    return best


def _add_core_halves(g, land, where):
    nslot, rows, cols = g.shape
    hr = rows // 2
    tr = _row_tile(hr)
    nr = hr // tr

    def body(where_ref, g_ref, l_ref, f_ref, b_ref):
        s = g_ref[...] + l_ref[...]
        f_ref[...] = s
        b_ref[...] = s.astype(BF16)

    blk = pl.BlockSpec((None, tr, cols), lambda s, i, w: (s, i, 0))
    return pl.pallas_call(
        body,
        grid_spec=pltpu.PrefetchScalarGridSpec(
            num_scalar_prefetch=1,
            grid=(nslot, nr),
            in_specs=[pl.BlockSpec((None, tr, cols), lambda s, i, w: (s, w[0] * nr + i, 0)), blk],
            out_specs=[blk, blk],
        ),
        out_shape=[jax.ShapeDtypeStruct((nslot, hr, cols), F32), jax.ShapeDtypeStruct((nslot, hr, cols), BF16)],
        compiler_params=_cparams(("parallel", "parallel")),
        name="add_core_halves",
    )(where, g, land)


def _send_to_owners(ps):
    n = len(ps)

    def body(*refs):
        ins, lands = refs[:n], refs[n:2 * n]
        send_sems, recv_sems = refs[2 * n:]
        xi, yi, ci = _coords()
        me = 2 * xi + yi
        cps = []
        for t in range(n):
            for d in (1, 2, 3):
                k = 3 * t + d - 1
                cp = _remote(ins[t].at[jnp.bitwise_xor(me, d)], lands[t].at[d - 1], send_sems.at[k], recv_sems.at[k],
                             _chip_peer(xi, yi, ci, d))
                cp.start()
                cps.append(cp)
        for cp in cps:
            cp.wait_recv()
        for cp in cps:
            cp.wait_send()

    return pl.pallas_call(
        body,
        in_specs=[HBM_SPEC] * n,
        out_specs=[HBM_SPEC] * n,
        out_shape=[jax.ShapeDtypeStruct((3,) + p.shape[1:], p.dtype) for p in ps],
        scratch_shapes=[pltpu.SemaphoreType.DMA((3 * n,)), pltpu.SemaphoreType.DMA((3 * n,))],
        name="send_to_owners",
    )(*ps)


def _add_chip_sums(pf, land, where):
    _, hr, cols = pf.shape
    tr = _row_tile(hr)

    def body(where_ref, p_ref, l_ref, o_ref):
        acc = p_ref[...]
        for d in range(3):
            acc = acc + l_ref[d].astype(F32)
        o_ref[...] = acc

    return pl.pallas_call(
        body,
        grid_spec=pltpu.PrefetchScalarGridSpec(
            num_scalar_prefetch=1,
            grid=(hr // tr,),
            in_specs=[pl.BlockSpec((None, tr, cols), lambda i, w: (w[1], i, 0)),
                      pl.BlockSpec((3, tr, cols), lambda i, w: (0, i, 0))],
            out_specs=pl.BlockSpec((tr, cols), lambda i, w: (i, 0)),
        ),
        out_shape=jax.ShapeDtypeStruct((hr, cols), F32),
        compiler_params=_cparams(("parallel",)),
        name="add_chip_sums",
    )(where, pf, land)


def _swap_reduced_halves(rs):
    n = len(rs)

    def body(*refs):
        ins, outs = refs[:n], refs[n:2 * n]
        send_sems, recv_sems = refs[2 * n:]
        xi, yi, ci = _coords()
        sibling = (xi, yi, 1 - ci)
        cps = [_remote(ins[t], outs[t], send_sems.at[t], recv_sems.at[t], sibling) for t in range(n)]
        for cp in cps:
            cp.start()
        for cp in cps:
            cp.wait_recv()
        for cp in cps:
            cp.wait_send()

    return pl.pallas_call(
        body,
        in_specs=[HBM_SPEC] * n,
        out_specs=[HBM_SPEC] * n,
        out_shape=[jax.ShapeDtypeStruct(r.shape, r.dtype) for r in rs],
        scratch_shapes=[pltpu.SemaphoreType.DMA((n,)), pltpu.SemaphoreType.DMA((n,))],
        name="swap_reduced_halves",
    )(*rs)


N_DEV = 8


def _all_reduce_small(part):
    r, w = part.shape

    def body(p_ref, o_ref, gath, send_sems, recv_sems):
        xi, yi, ci = _coords()
        me = 4 * xi + 2 * yi + ci
        gath[me] = p_ref[...]
        cps = []
        for d in range(1, N_DEV):
            peer = (jnp.bitwise_xor(xi, d >> 2), jnp.bitwise_xor(yi, (d >> 1) & 1), jnp.bitwise_xor(ci, d & 1))
            cp = _remote(p_ref, gath.at[me], send_sems.at[d - 1], recv_sems.at[d - 1], peer)
            cp.start()
            cps.append(cp)
        for d in range(1, N_DEV):
            src = gath.at[jnp.bitwise_xor(me, d)]
            _remote(src, src, send_sems.at[d - 1], recv_sems.at[d - 1], (xi, yi, ci)).wait_recv()
        acc = gath[0]
        for k in range(1, N_DEV):
            acc = acc + gath[k]
        o_ref[...] = acc
        for cp in cps:
            cp.wait_send()

    vm = pl.BlockSpec(memory_space=pltpu.VMEM)
    return pl.pallas_call(
        body,
        in_specs=[vm],
        out_specs=vm,
        out_shape=jax.ShapeDtypeStruct((r, w), F32),
        scratch_shapes=[pltpu.VMEM((N_DEV, r, w), F32), pltpu.SemaphoreType.DMA((N_DEV - 1,)),
                        pltpu.SemaphoreType.DMA((N_DEV - 1,))],
        name="all_reduce_small",
    )(part)


def _adamw_math(wv, gv, mv, vv):
    mn = ADAM_B1 * mv + (1.0 - ADAM_B1) * gv
    vn = ADAM_B2 * vv + (1.0 - ADAM_B2) * (gv * gv)
    m_hat = mn / (1.0 - ADAM_B1 ** ADAM_STEP)
    v_hat = vn / (1.0 - ADAM_B2 ** ADAM_STEP)
    return -ADAM_LR * (m_hat / (jnp.sqrt(v_hat) + ADAM_EPS) + ADAM_WD * wv), mn, vn


def _adamw_halves(w, g_mine, g_other, m, v, where, *, name):
    rows, cols = w.shape
    hr = rows // 2
    tr = _row_tile(hr, cap=256, mult=8)
    nr = hr // tr

    def body(where_ref, w_ref, gm_ref, go_ref, m_ref, v_ref, g_ref, d_ref, nm_ref, nv_ref):
        is_mine = pl.program_id(0) // nr == where_ref[0]
        gv = jnp.where(is_mine, gm_ref[...], go_ref[...])
        g_ref[...] = gv
        d_ref[...], nm_ref[...], nv_ref[...] = _adamw_math(w_ref[...], gv, m_ref[...], v_ref[...])

    def half_idx(i, holder):
        half = i // nr
        return jnp.where(half == holder, i % nr, jnp.where(half < holder, 0, nr - 1))

    blk = pl.BlockSpec((tr, cols), lambda i, wh: (i, 0))
    o = jax.ShapeDtypeStruct((rows, cols), F32)
    return pl.pallas_call(
        body,
        grid_spec=pltpu.PrefetchScalarGridSpec(
            num_scalar_prefetch=1,
            grid=(2 * nr,),
            in_specs=[blk, pl.BlockSpec((tr, cols), lambda i, wh: (half_idx(i, wh[0]), 0)),
                      pl.BlockSpec((tr, cols), lambda i, wh: (half_idx(i, 1 - wh[0]), 0)), blk, blk],
            out_specs=[blk] * 4,
        ),
        out_shape=[o, o, o, o],
        compiler_params=_cparams(("arbitrary",)),
        name=name,
    )(where, w, g_mine, g_other, m, v)


def _adamw(w, g, m, v, *, name):
    rows, cols = w.shape
    tr = _row_tile(rows, cap=256, mult=8)

    def body(w_ref, g_ref, m_ref, v_ref, d_ref, nm_ref, nv_ref):
        d_ref[...], nm_ref[...], nv_ref[...] = _adamw_math(w_ref[...], g_ref[...], m_ref[...], v_ref[...])

    blk = pl.BlockSpec((tr, cols), lambda i: (i, 0))
    o = jax.ShapeDtypeStruct((rows, cols), F32)
    return pl.pallas_call(
        body,
        grid=(rows // tr,),
        in_specs=[blk] * 4,
        out_specs=[blk] * 3,
        out_shape=[o, o, o],
        compiler_params=_cparams(("parallel",)),
        name=name,
    )(w, g, m, v)


def _pack(arrs):
    flat = jnp.concatenate([a.reshape(-1) for a in arrs])
    pad = (-flat.shape[0]) % (8 * LANES)
    return jnp.pad(flat, (0, pad)).reshape(-1, LANES)


def _unpack(packed, shapes):
    flat = packed.reshape(-1)
    out, off = [], 0
    for s in shapes:
        n = 1
        for dim in s:
            n *= dim
        out.append(flat[off:off + n].reshape(s))
        off += n
    return out


def _pad_rows(a, rows):
    return jnp.pad(a, ((0, rows - a.shape[0]), (0, 0)))


def _pad_lanes(a, lanes=LANES):
    return jnp.pad(a, ((0, 0), (0, lanes - a.shape[1])))


def _local_grads(x2d, tgt2d, prm, *, nb, seq):
    g_pre, g_post, g_fpre, g_fpost = prm["norm_mix_pre"], prm["norm_mix_post"], prm["norm_ffn_pre"], prm["norm_ffn_post"]
    w_main, w_dt, w_out, w_gate, w_up, w_down = (prm[k] for k in ("w_main", "w_dt", "w_out", "w_gate", "w_up", "w_down"))
    cw, sw = prm["conv_dw_w"], prm["ssd_conv_w"]
    dt_bias, a_log = _pad_lanes(prm["ssd_dt_bias"]), _pad_lanes(prm["ssd_a_log"])
    d_lanes = jnp.repeat(prm["ssd_d"], HEAD_DIM, axis=1)

    h = _rms_fwd(x2d, g_pre, out_dtype=BF16, name="rms_mix_pre")
    proj = _matmul([(h, w_main)], mode="nn", out_dtype=F32, tm=1024, tn=1024, tk=2048, name="mm_proj")
    dtr = _matmul([(h, w_dt)], mode="nn", out_dtype=F32, tm=1024, tn=128, tk=2048, name="mm_dt")
    u1, u = _conv_branch_fwd(proj, cw, prm["conv_dw_b"], prm["conv_ln_g"], prm["conv_ln_b"], nb=nb, seq=seq)
    xs_all = _ssd_pre_fwd(proj, sw, prm["ssd_conv_b"], nb=nb, seq=seq)
    y, ys, states = _ssd_fwd(xs_all, proj, dtr, dt_bias, a_log, d_lanes, prm["ssd_norm_w"], nb=nb, seq=seq)
    mix = _matmul([(u, w_out[:D_CONV]), (y, w_out[D_CONV:])], mode="nn", out_dtype=F32, tm=1024, tn=1024, tk=1024,
                  name="mm_mix")
    x1 = _rms_fwd(mix, g_post, res=x2d, out_dtype=F32, name="rms_mix_post")
    h2 = _rms_fwd(x1, g_fpre, out_dtype=BF16, name="rms_ffn_pre")
    gt, up, act = _ffn_up(h2, w_gate, w_up, tm=1024, tn=512)
    f = _matmul([(act, w_down)], mode="nn", out_dtype=F32, tm=1024, tn=1024, tk=1408, name="mm_down")
    dx2, df, loss, dg_fpost = _ffn_post_loss(f, x1, tgt2d, g_fpost)

    dgt, dup = _ffn_bwd_act(df, w_down, gt, up, tm=1024, tn=512)
    dw_down = _matmul([(act, df)], mode="tn", out_dtype=F32, tm=1408, tn=1024, tk=1024, name="mm_dw_down")
    dh2 = _matmul([(dgt, w_gate), (dup, w_up)], mode="nt", out_dtype=F32, tm=1024, tn=1024, tk=1408, name="mm_dh2")
    dw_gate = _matmul([(h2, dgt)], mode="tn", out_dtype=F32, tm=1024, tn=1408, tk=1024, name="mm_dw_gate", slot_out=True)
    dw_up = _matmul([(h2, dup)], mode="tn", out_dtype=F32, tm=1024, tn=1408, tk=1024, name="mm_dw_up", slot_out=True)
    dx1, dg_fpre = _rms_bwd([dh2], x1, g_fpre, addend=dx2, out_dtype=F32, name="rms_ffn_pre_bwd")
    dmix, dg_post = _rms_bwd([dx1], mix, g_post, out_dtype=BF16, name="rms_mix_post_bwd")
    duy = _matmul([(dmix, w_out)], mode="nt", out_dtype=F32, tm=1024, tn=1024, tk=2048, name="mm_duy")
    dw_out_u = _matmul([(u, dmix)], mode="tn", out_dtype=F32, tm=1024, tn=1024, tk=1024, name="mm_dw_out_u")
    dw_out_y = _matmul([(y, dmix)], mode="tn", out_dtype=F32, tm=1024, tn=1024, tk=1024, name="mm_dw_out_y")
    dcacg, dcw, dcb, dlg, dlb = _conv_branch_bwd(duy, u1, proj, cw, prm["conv_ln_g"], prm["conv_ln_b"], nb=nb, seq=seq)
    dz, dxs, ddtr, ssd_small = _ssd_bwd(duy, proj, ys, xs_all, dtr, states, dt_bias, a_log, d_lanes,
                                        prm["ssd_norm_w"], nb=nb, seq=seq)
    dxbc, dsw, dsb = _ssd_pre_bwd(dxs, proj, sw, prm["ssd_conv_b"], nb=nb, seq=seq)
    dproj = jnp.concatenate([dcacg, dz, dxbc], axis=1)
    ddtr_b = ddtr.astype(BF16)
    dh_main = _matmul([(dproj, w_main)], mode="nt", out_dtype=F32, tm=1024, tn=1024, tk=1024, name="mm_dh_main")
    dh_dt = _matmul([(ddtr_b, w_dt)], mode="nt", out_dtype=F32, tm=1024, tn=1024, tk=128, name="mm_dh_dt")
    dx, dg_pre = _rms_bwd([dh_main, dh_dt], x2d, g_pre, addend=dx1, out_dtype=F32, name="rms_mix_pre_bwd")
    dw_main = _matmul([(h, dproj)], mode="tn", out_dtype=F32, tm=1024, tn=1024, tk=1024, name="mm_dw_main")
    dw_dt = _matmul([(h, ddtr_b)], mode="tn", out_dtype=F32, tm=1024, tn=128, tk=1024, name="mm_dw_dt")

    grads = {
        "norm_mix_pre": dg_pre,
        "w_in": jnp.concatenate([dw_main, dw_dt[:, :HEADS]], axis=1),
        "conv_dw_w": dcw[:CONV_K], "conv_dw_b": dcb, "conv_ln_g": dlg, "conv_ln_b": dlb,
        "ssd_conv_w": dsw[:SSD_CONV_K], "ssd_conv_b": dsb,
        "ssd_dt_bias": ssd_small[2:3, :HEADS], "ssd_a_log": ssd_small[3:4, :HEADS], "ssd_d": ssd_small[1:2, :HEADS],
        "ssd_norm_w": ssd_small[0:1],
        "w_out": jnp.concatenate([dw_out_u, dw_out_y], axis=0),
        "norm_mix_post": dg_post, "norm_ffn_pre": dg_fpre,
        "w_gate": dw_gate, "w_up": dw_up,
        "w_down": dw_down, "norm_ffn_post": dg_fpost,
    }
    return loss, dx, grads


BIG = ("w_in", "w_out", "w_gate", "w_up", "w_down")
COL_SHARDED = ("w_in", "w_gate", "w_up")
SMALL = ("norm_mix_pre", "conv_dw_w", "conv_dw_b", "conv_ln_g", "conv_ln_b", "ssd_conv_w", "ssd_conv_b", "ssd_dt_bias",
         "ssd_a_log", "ssd_d", "ssd_norm_w", "norm_mix_post", "norm_ffn_pre", "norm_ffn_post")
WEIGHTS = ("norm_mix_pre", "w_in", "conv_dw_w", "conv_dw_b", "conv_ln_g", "conv_ln_b", "ssd_conv_w", "ssd_conv_b",
           "ssd_dt_bias", "ssd_a_log", "ssd_d", "ssd_norm_w", "w_out", "norm_mix_post", "norm_ffn_pre", "w_gate", "w_up",
           "w_down", "norm_ffn_post")


def _slots_by_cols(a):
    rows, cols = a.shape
    return a.reshape(rows, N_CHIPS, cols // N_CHIPS).transpose(1, 0, 2)


def _cols_from_slots(a):
    n, rows, w = a.shape
    return a.transpose(1, 0, 2).reshape(rows, n * w)


def kernel(x, norm_mix_pre, w_in, conv_dw_w, conv_dw_b, conv_ln_g, conv_ln_b, ssd_conv_w, ssd_conv_b, ssd_dt_bias, ssd_a_log, ssd_d, ssd_norm_w, w_out, norm_mix_post, norm_ffn_pre, w_gate, w_up, w_down, norm_ffn_post, loss_target, m_norm_mix_pre, m_w_in, m_conv_dw_w, m_conv_dw_b, m_conv_ln_g, m_conv_ln_b, m_ssd_conv_w, m_ssd_conv_b, m_ssd_dt_bias, m_ssd_a_log, m_ssd_d, m_ssd_norm_w, m_w_out, m_norm_mix_post, m_norm_ffn_pre, m_w_gate, m_w_up, m_w_down, m_norm_ffn_post, v_norm_mix_pre, v_w_in, v_conv_dw_w, v_conv_dw_b, v_conv_ln_g, v_conv_ln_b, v_ssd_conv_w, v_ssd_conv_b, v_ssd_dt_bias, v_ssd_a_log, v_ssd_d, v_ssd_norm_w, v_w_out, v_norm_mix_post, v_norm_ffn_pre, v_w_gate, v_w_up, v_w_down, v_norm_ffn_post):
    args = dict(locals())
    two_d = lambda a: a.reshape(a.shape[-2:])
    wts = {n: two_d(args[n]) for n in WEIGHTS}
    ms = {n: two_d(args["m_" + n]) for n in WEIGHTS}
    vs = {n: two_d(args["v_" + n]) for n in WEIGHTS}
    nb, seq, d = x.shape
    t = nb * seq
    xi, yi, ci = _coords()
    chip = 2 * xi + yi
    where = jnp.stack([ci, chip]).astype(jnp.int32)

    shards = [wts[n].astype(BF16) for n in BIG] + [_pad_rows(wts["conv_dw_w"], 32), _pad_rows(wts["ssd_conv_w"], 8)]
    gathered = _gather_weights(shards[:len(BIG)], shards[len(BIG):])
    gathered = [lax.dynamic_update_slice(g, s[None], (chip, 0, 0)) for g, s in zip(gathered, shards)]
    g_in, g_out, g_gate, g_up, g_down, g_cw, g_sw = gathered
    w_in_full = _cols_from_slots(g_in)
    prm = {n: wts[n] for n in SMALL}
    prm.update(
        w_main=w_in_full[:, :D_MAIN], w_dt=_pad_lanes(w_in_full[:, D_MAIN:]),
        w_out=g_out.reshape(D_MODEL, D_MODEL), w_gate=_cols_from_slots(g_gate), w_up=_cols_from_slots(g_up),
        w_down=g_down.reshape(D_FF, D_MODEL), conv_dw_w=_cols_from_slots(g_cw), ssd_conv_w=_cols_from_slots(g_sw))

    loss, dx, grads = _local_grads(x.reshape(t, d), loss_target.reshape(t, d), prm, nb=nb, seq=seq)
    loss = lax.psum(loss[0, 0], MESH_AXES)

    slots = [_slots_by_cols(grads["w_in"]), grads["w_out"].reshape(N_CHIPS, D_MODEL // N_CHIPS, D_MODEL),
             grads["w_gate"], grads["w_up"], grads["w_down"].reshape(N_CHIPS, D_FF // N_CHIPS, D_MODEL)]
    lands = _swap_other_halves(slots)
    sums = [_add_core_halves(g, l, where) for g, l in zip(slots, lands)]
    recv = _send_to_owners([s[1] for s in sums])
    halves = [_add_chip_sums(s[0], r, where) for s, r in zip(sums, recv)]
    other_halves = _swap_reduced_halves(halves)

    small_shapes = [grads[n].shape for n in SMALL]
    small_sum = _unpack(_all_reduce_small(_pack([grads[n] for n in SMALL])), small_shapes)
    small_grads = dict(zip(SMALL, small_sum))
    cwid, swid = D_CONV // N_CHIPS, D_XBC // N_CHIPS
    small_grads["conv_dw_w"] = lax.dynamic_slice(small_grads["conv_dw_w"], (0, chip * cwid), (CONV_K, cwid))
    small_grads["ssd_conv_w"] = lax.dynamic_slice(small_grads["ssd_conv_w"], (0, chip * swid), (SSD_CONV_K, swid))

    out_g, out_d, out_m, out_v = {}, {}, {}, {}
    for n, mine, other in zip(BIG, halves, other_halves):
        out_g[n], out_d[n], out_m[n], out_v[n] = _adamw_halves(wts[n], mine, other, ms[n], vs[n], where,
                                                                 name="adamw_" + n)
    shard_shapes = [wts[n].shape for n in SMALL]
    pd, pm, pv = _adamw(_pack([wts[n] for n in SMALL]), _pack([small_grads[n] for n in SMALL]),
                        _pack([ms[n] for n in SMALL]), _pack([vs[n] for n in SMALL]), name="adamw_small")
    for n, dd, mm, vv in zip(SMALL, _unpack(pd, shard_shapes), _unpack(pm, shard_shapes), _unpack(pv, shard_shapes)):
        out_g[n], out_d[n], out_m[n], out_v[n] = small_grads[n], dd, mm, vv

    outs = [o[n].reshape(args[n].shape) for o in (out_g, out_d, out_m, out_v) for n in WEIGHTS]
    return (loss, dx.reshape(nb, seq, d), *outs)
```

```python
import functools

import jax
import jax.numpy as jnp
from jax import lax
from jax.experimental import pallas as pl
from jax.experimental.pallas import tpu as pltpu

F32 = jnp.float32
BF16 = jnp.bfloat16
EPS = 1e-6

D_MODEL = 2048
D_CONV = 1024
D_SSD = 1024
D_XBC = 2048
HEADS = 16
HEAD_DIM = 64
GROUPS = 4
STATE = 128
CONV_K = 31
SSD_CONV_K = 4
D_FF = 5632
D_MAIN = 2 * D_CONV + D_SSD + D_XBC
D_IN = D_MAIN + HEADS
N_CHIPS = 4
LANES = 128
CHUNK = 128
PAIRS = HEADS // 2

ADAM_LR = 0.001
ADAM_B1 = 0.9
ADAM_B2 = 0.999
ADAM_EPS = 1e-08
ADAM_WD = 0.01
ADAM_STEP = 10

MESH_AXES = ("x", "y", "c")
VMEM_LIMIT = 56 * 1024 * 1024


def _sig(v):
    return 1.0 / (1.0 + jnp.exp(-v))


def _cparams(sem, vmem=VMEM_LIMIT):
    return pltpu.CompilerParams(dimension_semantics=sem, vmem_limit_bytes=vmem)


_DIMS = {"nn": ((1,), (0,)), "nt": ((1,), (1,)), "tn": ((0,), (0,))}


def _matmul(pairs, *, mode, out_dtype, tm, tn, tk, name, slot_out=False):
    a0, b0 = pairs[0]
    if mode == "nn":
        (m, k), n = a0.shape, b0.shape[1]
    elif mode == "nt":
        (m, k), n = a0.shape, b0.shape[0]
    else:
        (k, m), n = a0.shape, b0.shape[1]
    tm, tn, tk = min(tm, m), min(tn, n), min(tk, k)
    assert m % tm == 0 and n % tn == 0 and k % tk == 0, (name, m, n, k, tm, tn, tk)
    nk = k // tk
    npairs = len(pairs)
    dims = (_DIMS[mode], ((), ()))

    def body(*refs):
        ins, o_ref = refs[: 2 * npairs], refs[2 * npairs]
        part = None
        for p in range(npairs):
            d = lax.dot_general(ins[2 * p][...], ins[2 * p + 1][...], dims, preferred_element_type=F32)
            part = d if part is None else part + d
        if nk == 1:
            o_ref[...] = part.astype(out_dtype)
            return
        acc = refs[2 * npairs + 1]
        kk = pl.program_id(2)

        @pl.when(kk == 0)
        def _():
            acc[...] = part

        @pl.when(kk > 0)
        def _():
            acc[...] += part

        @pl.when(kk == nk - 1)
        def _():
            o_ref[...] = acc[...].astype(out_dtype)

    if mode == "nn":
        a_spec = pl.BlockSpec((tm, tk), lambda i, j, kk: (i, kk))
        b_spec = pl.BlockSpec((tk, tn), lambda i, j, kk: (kk, j))
    elif mode == "nt":
        a_spec = pl.BlockSpec((tm, tk), lambda i, j, kk: (i, kk))
        b_spec = pl.BlockSpec((tn, tk), lambda i, j, kk: (j, kk))
    else:
        a_spec = pl.BlockSpec((tk, tm), lambda i, j, kk: (kk, i))
        b_spec = pl.BlockSpec((tk, tn), lambda i, j, kk: (kk, j))
    if slot_out:
        out_shape = jax.ShapeDtypeStruct((n // tn, m, tn), out_dtype)
        out_spec = pl.BlockSpec((None, tm, tn), lambda i, j, kk: (j, i, 0))
    else:
        out_shape = jax.ShapeDtypeStruct((m, n), out_dtype)
        out_spec = pl.BlockSpec((tm, tn), lambda i, j, kk: (i, j))
    flat = [t for ab in pairs for t in ab]
    return pl.pallas_call(
        body,
        grid=(m // tm, n // tn, nk),
        in_specs=[a_spec, b_spec] * npairs,
        out_specs=out_spec,
        out_shape=out_shape,
        scratch_shapes=[] if nk == 1 else [pltpu.VMEM((tm, tn), F32)],
        compiler_params=_cparams(("parallel", "parallel", "arbitrary")),
        name=name,
    )(*flat)


def _ffn_up(h2, wg, wu, *, tm, tn):
    t, k = h2.shape
    n = wg.shape[1]
    tm = min(tm, t)
    assert t % tm == 0 and n % tn == 0, (t, n, tm, tn)

    def body(h_ref, wg_ref, wu_ref, g_ref, u_ref, a_ref):
        hv = h_ref[...]
        g = jnp.dot(hv, wg_ref[...], preferred_element_type=F32)
        u = jnp.dot(hv, wu_ref[...], preferred_element_type=F32)
        g_ref[...] = g.astype(BF16)
        u_ref[...] = u.astype(BF16)
        a_ref[...] = (g * _sig(g) * u).astype(BF16)

    o = jax.ShapeDtypeStruct((t, n), BF16)
    ospec = pl.BlockSpec((tm, tn), lambda i, j: (i, j))
    return pl.pallas_call(
        body,
        grid=(t // tm, n // tn),
        in_specs=[pl.BlockSpec((tm, k), lambda i, j: (i, 0)), pl.BlockSpec((k, tn), lambda i, j: (0, j)),
                  pl.BlockSpec((k, tn), lambda i, j: (0, j))],
        out_specs=[ospec, ospec, ospec],
        out_shape=[o, o, o],
        compiler_params=_cparams(("parallel", "parallel")),
        name="ffn_up",
    )(h2, wg, wu)


def _ffn_bwd_act(df, wd, gt, up, *, tm, tn):
    t, k = df.shape
    n = wd.shape[0]
    tm = min(tm, t)
    assert t % tm == 0 and n % tn == 0, (t, n, tm, tn)

    def body(df_ref, wd_ref, g_ref, u_ref, dg_ref, du_ref):
        da = lax.dot_general(df_ref[...], wd_ref[...], (_DIMS["nt"], ((), ())), preferred_element_type=F32)
        g = g_ref[...].astype(F32)
        u = u_ref[...].astype(F32)
        s = _sig(g)
        dg_ref[...] = (da * u * s * (1.0 + g * (1.0 - s))).astype(BF16)
        du_ref[...] = (da * g * s).astype(BF16)

    o = jax.ShapeDtypeStruct((t, n), BF16)
    blk = pl.BlockSpec((tm, tn), lambda i, j: (i, j))
    return pl.pallas_call(
        body,
        grid=(t // tm, n // tn),
        in_specs=[pl.BlockSpec((tm, k), lambda i, j: (i, 0)), pl.BlockSpec((tn, k), lambda i, j: (j, 0)), blk, blk],
        out_specs=[blk, blk],
        out_shape=[o, o],
        compiler_params=_cparams(("parallel", "parallel")),
        name="ffn_bwd_act",
    )(df, wd, gt, up)


ROW_TILE = 256


DEP_SPEC = pl.BlockSpec(memory_space=pl.ANY)


def _rms_fwd(xv, g, *, res=None, dep=None, out_dtype, name):
    t, d = xv.shape
    has_res = res is not None
    deps = [] if dep is None else [dep]

    def body(*refs):
        x_ref, g_ref = refs[0], refs[1]
        o_ref = refs[-1]
        v = x_ref[...]
        r = lax.rsqrt(jnp.mean(v * v, axis=-1, keepdims=True) + EPS)
        y = v * r * g_ref[...]
        if has_res:
            y = refs[2][...] + y
        o_ref[...] = y.astype(out_dtype)

    row = pl.BlockSpec((ROW_TILE, d), lambda i: (i, 0))
    vec = pl.BlockSpec((1, d), lambda i: (0, 0))
    return pl.pallas_call(
        body,
        grid=(t // ROW_TILE,),
        in_specs=[row, vec] + ([row] if has_res else []) + [DEP_SPEC] * len(deps),
        out_specs=row,
        out_shape=jax.ShapeDtypeStruct((t, d), out_dtype),
        compiler_params=_cparams(("parallel",)),
        name=name,
    )(*([xv, g] + ([res] if has_res else []) + deps))


def _rms_bwd(dys, xv, g, *, addend=None, dep=None, out_dtype, name):
    t, d = xv.shape
    ndy = len(dys)
    has_add = addend is not None
    deps = [] if dep is None else [dep]

    def body(*refs):
        dy_refs = refs[:ndy]
        x_ref, g_ref = refs[ndy], refs[ndy + 1]
        dx_ref, dg_ref = refs[-2], refs[-1]
        dy = dy_refs[0][...].astype(F32)
        for rr in dy_refs[1:]:
            dy = dy + rr[...].astype(F32)
        v = x_ref[...]
        r = lax.rsqrt(jnp.mean(v * v, axis=-1, keepdims=True) + EPS)
        xh = v * r
        gdy = dy * g_ref[...]
        dx = r * (gdy - xh * jnp.mean(gdy * xh, axis=-1, keepdims=True))
        if has_add:
            dx = dx + refs[ndy + 2][...]
        dx_ref[...] = dx.astype(out_dtype)

        @pl.when(pl.program_id(0) == 0)
        def _():
            dg_ref[...] = jnp.zeros_like(dg_ref)

        dg_ref[...] += jnp.sum(dy * xh, axis=0, keepdims=True)

    row = pl.BlockSpec((ROW_TILE, d), lambda i: (i, 0))
    vec = pl.BlockSpec((1, d), lambda i: (0, 0))
    return pl.pallas_call(
        body,
        grid=(t // ROW_TILE,),
        in_specs=[row] * ndy + [row, vec] + ([row] if has_add else []) + [DEP_SPEC] * len(deps),
        out_specs=[row, vec],
        out_shape=[jax.ShapeDtypeStruct((t, d), out_dtype), jax.ShapeDtypeStruct((1, d), F32)],
        compiler_params=_cparams(("arbitrary",)),
        name=name,
    )(*(list(dys) + [xv, g] + ([addend] if has_add else []) + deps))


def _ffn_post_loss(f, x1, tgt, g):
    t, d = f.shape

    def body(f_ref, x1_ref, t_ref, g_ref, dx2_ref, df_ref, loss_ref, dg_ref):
        v = f_ref[...]
        gv = g_ref[...]
        r = lax.rsqrt(jnp.mean(v * v, axis=-1, keepdims=True) + EPS)
        fh = v * r
        e = x1_ref[...] + fh * gv - t_ref[...]
        dx2 = e * (1.0 / d)
        dx2_ref[...] = dx2
        gdy = dx2 * gv
        df_ref[...] = (r * (gdy - fh * jnp.mean(gdy * fh, axis=-1, keepdims=True))).astype(BF16)

        @pl.when(pl.program_id(0) == 0)
        def _():
            dg_ref[...] = jnp.zeros_like(dg_ref)
            loss_ref[...] = jnp.zeros_like(loss_ref)

        dg_ref[...] += jnp.sum(dx2 * fh, axis=0, keepdims=True)
        per_tok = jnp.mean(e * e, axis=-1, keepdims=True)
        loss_ref[...] += 0.5 * jnp.sum(per_tok, axis=0, keepdims=True)

    row = pl.BlockSpec((ROW_TILE, d), lambda i: (i, 0))
    vec = pl.BlockSpec((1, d), lambda i: (0, 0))
    return pl.pallas_call(
        body,
        grid=(t // ROW_TILE,),
        in_specs=[row, row, row, vec],
        out_specs=[row, row, pl.BlockSpec((1, 1), lambda i: (0, 0)), vec],
        out_shape=[jax.ShapeDtypeStruct((t, d), F32), jax.ShapeDtypeStruct((t, d), BF16),
                   jax.ShapeDtypeStruct((1, 1), F32), jax.ShapeDtypeStruct((1, d), F32)],
        compiler_params=_cparams(("arbitrary",)),
        name="ffn_post_loss",
    )(f, x1, tgt, g)


CONV_ROWS = 256
TAP_ROWS = 64
HALO31 = 32
HALO4 = 8


def _sum8(v):
    return jnp.sum(v.reshape(v.shape[0] // 8, 8, v.shape[1]), axis=0)


def _conv_branch_fwd(proj, cw, cb, lg, lb, *, nb, seq):
    ts, c, halo = CONV_ROWS, D_CONV, HALO31
    ns = seq // ts
    base = halo - CONV_K + 1

    def body(ca_ref, cg_ref, w_ref, b_ref, lg_ref, lb_ref, u1_ref, u_ref, ubuf):
        i = pl.program_id(1)

        @pl.when(i == 0)
        def _():
            ubuf[0:halo, :] = jnp.zeros((halo, c), F32)

        @pl.when(i > 0)
        def _():
            ubuf[0:halo, :] = ubuf[ts:ts + halo, :]

        ubuf[halo:halo + ts, :] = ca_ref[...] * _sig(cg_ref[...])

        def lane_tile(j, carry):
            ln = pl.ds(pl.multiple_of(j * LANES, LANES), LANES)
            for r in range(ts // TAP_ROWS):
                acc = jnp.broadcast_to(b_ref[:, ln], (TAP_ROWS, LANES))
                for k in range(CONV_K):
                    acc = acc + w_ref[pl.ds(k, 1), ln] * ubuf[pl.ds(r * TAP_ROWS + base + k, TAP_ROWS), ln]
                u1_ref[pl.ds(r * TAP_ROWS, TAP_ROWS), ln] = acc
            return carry

        lax.fori_loop(0, c // LANES, lane_tile, 0)
        v = u1_ref[...]
        mu = jnp.mean(v, axis=-1, keepdims=True)
        dv = v - mu
        xh = dv * lax.rsqrt(jnp.mean(dv * dv, axis=-1, keepdims=True) + EPS)
        u2 = xh * lg_ref[...] + lb_ref[...]
        u_ref[...] = (u2 * _sig(u2)).astype(BF16)

    t = nb * seq
    row = lambda col: pl.BlockSpec((ts, c), lambda b, i: (b * ns + i, col))
    vec = pl.BlockSpec((1, c), lambda b, i: (0, 0))
    return pl.pallas_call(
        body,
        grid=(nb, ns),
        in_specs=[row(0), row(1), pl.BlockSpec((32, c), lambda b, i: (0, 0)), vec, vec, vec],
        out_specs=[row(0), row(0)],
        out_shape=[jax.ShapeDtypeStruct((t, c), F32), jax.ShapeDtypeStruct((t, c), BF16)],
        scratch_shapes=[pltpu.VMEM((halo + ts, c), F32)],
        compiler_params=_cparams(("parallel", "arbitrary")),
        name="conv_branch_fwd",
    )(proj, proj, cw, cb, lg, lb)


def _conv_branch_bwd(duy, u1, proj, cw, lg, lb, *, nb, seq):
    ts, c, halo = CONV_ROWS, D_CONV, HALO31
    ns = seq // ts
    base = halo - CONV_K + 1
    hb = ts // halo

    def body(du_ref, u1_ref, ca_ref, cg_ref, cah_ref, cgh_ref, w_ref, lg_ref, lb_ref,
             dcacg_ref, dw_ref, db_ref, dlg_ref, dlb_ref,
             ubuf, dbuf, du0buf, dwacc, dbacc, dlgacc, dlbacc):
        b, i = pl.program_id(0), pl.program_id(1)
        rc = ns - 1 - i

        @pl.when(jnp.logical_and(b == 0, i == 0))
        def _():
            dwacc[...] = jnp.zeros_like(dwacc)
            dbacc[...] = jnp.zeros_like(dbacc)
            dlgacc[...] = jnp.zeros_like(dlgacc)
            dlbacc[...] = jnp.zeros_like(dlbacc)

        @pl.when(i == 0)
        def _():
            dbuf[ts:ts + halo, :] = jnp.zeros((halo, c), F32)

        @pl.when(i > 0)
        def _():
            dbuf[ts:ts + halo, :] = dbuf[0:halo, :]

        v = u1_ref[...]
        mu = jnp.mean(v, axis=-1, keepdims=True)
        dv = v - mu
        rstd = lax.rsqrt(jnp.mean(dv * dv, axis=-1, keepdims=True) + EPS)
        xh = dv * rstd
        lgv = lg_ref[...]
        u2 = xh * lgv + lb_ref[...]
        s2 = _sig(u2)
        du2 = du_ref[...] * (s2 * (1.0 + u2 * (1.0 - s2)))
        dlgacc[...] += jnp.sum(du2 * xh, axis=0, keepdims=True)
        dlbacc[...] += jnp.sum(du2, axis=0, keepdims=True)
        gd = du2 * lgv
        du1 = rstd * (gd - jnp.mean(gd, axis=-1, keepdims=True) - xh * jnp.mean(gd * xh, axis=-1, keepdims=True))
        dbacc[...] += jnp.sum(du1, axis=0, keepdims=True)
        dbuf[0:ts, :] = du1

        @pl.when(rc == 0)
        def _():
            ubuf[0:halo, :] = jnp.zeros((halo, c), F32)

        @pl.when(rc > 0)
        def _():
            ubuf[0:halo, :] = cah_ref[...] * _sig(cgh_ref[...])

        cav = ca_ref[...]
        sg = _sig(cg_ref[...])
        ubuf[halo:halo + ts, :] = cav * sg

        def lane_tile(j, carry):
            ln = pl.ds(pl.multiple_of(j * LANES, LANES), LANES)
            for r in range(ts // TAP_ROWS):
                r0 = r * TAP_ROWS
                d1 = dbuf[pl.ds(r0, TAP_ROWS), ln]
                acc = jnp.zeros((TAP_ROWS, LANES), F32)
                for k in range(CONV_K):
                    acc = acc + w_ref[pl.ds(k, 1), ln] * dbuf[pl.ds(r0 + CONV_K - 1 - k, TAP_ROWS), ln]
                    dwacc[pl.ds(k * 8, 8), ln] += _sum8(d1 * ubuf[pl.ds(r0 + base + k, TAP_ROWS), ln])
                du0buf[pl.ds(r0, TAP_ROWS), ln] = acc
            return carry

        lax.fori_loop(0, c // LANES, lane_tile, 0)
        du0 = du0buf[...]
        dcacg_ref[:, 0:c] = (du0 * sg).astype(BF16)
        dcacg_ref[:, c:2 * c] = (du0 * cav * sg * (1.0 - sg)).astype(BF16)

        @pl.when(jnp.logical_and(b == nb - 1, i == ns - 1))
        def _():
            for k in range(CONV_K):
                dw_ref[pl.ds(k, 1), :] = jnp.sum(dwacc[pl.ds(k * 8, 8), :], axis=0, keepdims=True)
            dw_ref[pl.ds(CONV_K, 1), :] = jnp.zeros((1, c), F32)
            db_ref[...] = dbacc[...]
            dlg_ref[...] = dlgacc[...]
            dlb_ref[...] = dlbacc[...]

    t = nb * seq
    rowblk = lambda b, i: b * ns + (ns - 1 - i)
    row = lambda col: pl.BlockSpec((ts, c), lambda b, i: (rowblk(b, i), col))
    hrow = lambda col: pl.BlockSpec((halo, c), lambda b, i: (jnp.maximum(rowblk(b, i) * hb - 1, 0), col))
    vec = pl.BlockSpec((1, c), lambda b, i: (0, 0))
    wspec = pl.BlockSpec((32, c), lambda b, i: (0, 0))
    return pl.pallas_call(
        body,
        grid=(nb, ns),
        in_specs=[row(0), row(0), row(0), row(1), hrow(0), hrow(1), wspec, vec, vec],
        out_specs=[pl.BlockSpec((ts, 2 * c), lambda b, i: (rowblk(b, i), 0)), wspec, vec, vec, vec],
        out_shape=[jax.ShapeDtypeStruct((t, 2 * c), BF16), jax.ShapeDtypeStruct((32, c), F32),
                   jax.ShapeDtypeStruct((1, c), F32), jax.ShapeDtypeStruct((1, c), F32), jax.ShapeDtypeStruct((1, c), F32)],
        scratch_shapes=[pltpu.VMEM((halo + ts, c), F32), pltpu.VMEM((ts + halo, c), F32), pltpu.VMEM((ts, c), F32),
                        pltpu.VMEM((CONV_K * 8, c), F32), pltpu.VMEM((1, c), F32), pltpu.VMEM((1, c), F32),
                        pltpu.VMEM((1, c), F32)],
        compiler_params=_cparams(("arbitrary", "arbitrary")),
        name="conv_branch_bwd",
    )(duy, u1, proj, proj, proj, proj, cw, lg, lb)


XBC_COL0 = (2 * D_CONV + D_SSD) // 1024


def _ssd_pre_fwd(proj, sw, sb, *, nb, seq):
    ts, c, halo = CONV_ROWS, 1024, HALO4
    ns = seq // ts
    base = halo - SSD_CONV_K + 1

    def body(x_ref, w_ref, b_ref, o_ref, xbuf):
        i = pl.program_id(2)

        @pl.when(i == 0)
        def _():
            xbuf[0:halo, :] = jnp.zeros((halo, c), F32)

        @pl.when(i > 0)
        def _():
            xbuf[0:halo, :] = xbuf[ts:ts + halo, :]

        xbuf[halo:halo + ts, :] = x_ref[...]

        def lane_tile(j, carry):
            ln = pl.ds(pl.multiple_of(j * LANES, LANES), LANES)
            for r in range(ts // TAP_ROWS):
                acc = jnp.broadcast_to(b_ref[:, ln], (TAP_ROWS, LANES))
                for k in range(SSD_CONV_K):
                    acc = acc + w_ref[pl.ds(k, 1), ln] * xbuf[pl.ds(r * TAP_ROWS + base + k, TAP_ROWS), ln]
                o_ref[pl.ds(r * TAP_ROWS, TAP_ROWS), ln] = acc * _sig(acc)
            return carry

        lax.fori_loop(0, c // LANES, lane_tile, 0)

    t = nb * seq
    return pl.pallas_call(
        body,
        grid=(2, nb, ns),
        in_specs=[pl.BlockSpec((ts, c), lambda j, b, i: (b * ns + i, XBC_COL0 + j)),
                  pl.BlockSpec((8, c), lambda j, b, i: (0, j)), pl.BlockSpec((1, c), lambda j, b, i: (0, j))],
        out_specs=pl.BlockSpec((ts, c), lambda j, b, i: (b * ns + i, j)),
        out_shape=jax.ShapeDtypeStruct((t, D_XBC), F32),
        scratch_shapes=[pltpu.VMEM((halo + ts, c), F32)],
        compiler_params=_cparams(("parallel", "parallel", "arbitrary")),
        name="ssd_pre_fwd",
    )(proj, sw, sb)


def _ssd_pre_bwd(dxs, proj, sw, sb, *, nb, seq):
    ts, c, halo = CONV_ROWS, 1024, HALO4
    ns = seq // ts
    base = halo - SSD_CONV_K + 1
    hb = ts // halo

    def body(d_ref, x_ref, xh_ref, w_ref, b_ref, dx_ref, dw_ref, db_ref, xbuf, dbuf, dwacc, dbacc):
        b, i = pl.program_id(1), pl.program_id(2)
        rc = ns - 1 - i

        @pl.when(jnp.logical_and(b == 0, i == 0))
        def _():
            dwacc[...] = jnp.zeros_like(dwacc)
            dbacc[...] = jnp.zeros_like(dbacc)

        @pl.when(i == 0)
        def _():
            dbuf[ts:ts + halo, :] = jnp.zeros((halo, c), F32)

        @pl.when(i > 0)
        def _():
            dbuf[ts:ts + halo, :] = dbuf[0:halo, :]

        @pl.when(rc == 0)
        def _():
            xbuf[0:halo, :] = jnp.zeros((halo, c), F32)

        @pl.when(rc > 0)
        def _():
            xbuf[0:halo, :] = xh_ref[...]

        xbuf[halo:halo + ts, :] = x_ref[...]

        def pre_tile(j, carry):
            ln = pl.ds(pl.multiple_of(j * LANES, LANES), LANES)
            for r in range(ts // TAP_ROWS):
                r0 = r * TAP_ROWS
                acc = jnp.broadcast_to(b_ref[:, ln], (TAP_ROWS, LANES))
                for k in range(SSD_CONV_K):
                    acc = acc + w_ref[pl.ds(k, 1), ln] * xbuf[pl.ds(r0 + base + k, TAP_ROWS), ln]
                s = _sig(acc)
                dc = d_ref[pl.ds(r0, TAP_ROWS), ln] * (s * (1.0 + acc * (1.0 - s)))
                dbuf[pl.ds(r0, TAP_ROWS), ln] = dc
                dbacc[:, ln] += _sum8(dc)
            return carry

        lax.fori_loop(0, c // LANES, pre_tile, 0)

        def lane_tile(j, carry):
            ln = pl.ds(pl.multiple_of(j * LANES, LANES), LANES)
            for r in range(ts // TAP_ROWS):
                r0 = r * TAP_ROWS
                d1 = dbuf[pl.ds(r0, TAP_ROWS), ln]
                acc = jnp.zeros((TAP_ROWS, LANES), F32)
                for k in range(SSD_CONV_K):
                    acc = acc + w_ref[pl.ds(k, 1), ln] * dbuf[pl.ds(r0 + SSD_CONV_K - 1 - k, TAP_ROWS), ln]
                    dwacc[pl.ds(k * 8, 8), ln] += _sum8(d1 * xbuf[pl.ds(r0 + base + k, TAP_ROWS), ln])
                dx_ref[pl.ds(r0, TAP_ROWS), ln] = acc.astype(BF16)
            return carry

        lax.fori_loop(0, c // LANES, lane_tile, 0)

        @pl.when(jnp.logical_and(b == nb - 1, i == ns - 1))
        def _():
            for k in range(SSD_CONV_K):
                dw_ref[pl.ds(k, 1), :] = jnp.sum(dwacc[pl.ds(k * 8, 8), :], axis=0, keepdims=True)
            dw_ref[pl.ds(SSD_CONV_K, 8 - SSD_CONV_K), :] = jnp.zeros((8 - SSD_CONV_K, c), F32)
            db_ref[...] = jnp.sum(dbacc[...], axis=0, keepdims=True)

    t = nb * seq
    rowblk = lambda b, i: b * ns + (ns - 1 - i)
    return pl.pallas_call(
        body,
        grid=(2, nb, ns),
        in_specs=[pl.BlockSpec((ts, c), lambda j, b, i: (rowblk(b, i), j)),
                  pl.BlockSpec((ts, c), lambda j, b, i: (rowblk(b, i), XBC_COL0 + j)),
                  pl.BlockSpec((halo, c), lambda j, b, i: (jnp.maximum(rowblk(b, i) * hb - 1, 0), XBC_COL0 + j)),
                  pl.BlockSpec((8, c), lambda j, b, i: (0, j)), pl.BlockSpec((1, c), lambda j, b, i: (0, j))],
        out_specs=[pl.BlockSpec((ts, c), lambda j, b, i: (rowblk(b, i), j)),
                   pl.BlockSpec((8, c), lambda j, b, i: (0, j)), pl.BlockSpec((1, c), lambda j, b, i: (0, j))],
        out_shape=[jax.ShapeDtypeStruct((t, D_XBC), BF16), jax.ShapeDtypeStruct((8, D_XBC), F32),
                   jax.ShapeDtypeStruct((1, D_XBC), F32)],
        scratch_shapes=[pltpu.VMEM((halo + ts, c), F32), pltpu.VMEM((ts + halo, c), F32),
                        pltpu.VMEM((SSD_CONV_K * 8, c), F32), pltpu.VMEM((8, c), F32)],
        compiler_params=_cparams(("arbitrary", "arbitrary", "arbitrary")),
        name="ssd_pre_bwd",
    )(dxs, proj, proj, sw, sb)


Z_COL = (2 * D_CONV) // 1024
GROUP_W = D_SSD // GROUPS


def _softplus(v):
    return jnp.maximum(v, 0.0) + jnp.log(1.0 + jnp.exp(-jnp.abs(v)))


def _dot(a, b):
    return jnp.dot(a, b, preferred_element_type=F32)


def _dot_nt(a, b):
    return lax.dot_general(a, b, (_DIMS["nt"], ((), ())), preferred_element_type=F32)


def _dot_tn(a, b):
    return lax.dot_general(a, b, (_DIMS["tn"], ((), ())), preferred_element_type=F32)


def _dot_exact(a, b):
    return jnp.dot(a, b, precision=lax.Precision.HIGHEST, preferred_element_type=F32)


def _chunk_decays(dtr_ref, bias_ref, alog_ref):
    q = CHUNK
    ii = lax.broadcasted_iota(jnp.int32, (q, q), 0)
    jj = lax.broadcasted_iota(jnp.int32, (q, q), 1)
    tri = jj <= ii
    dt = _softplus(dtr_ref[...] + bias_ref[...])
    a_head = -jnp.exp(alog_ref[...])
    cs = _dot_exact(tri.astype(F32), dt * a_head)
    return tri, dt, a_head, cs, cs.T


def _ssd_fwd(xs_all, proj, dtr, dt_bias, a_log, d_lanes, norm_w, *, nb, seq):
    q = CHUNK
    nc = seq // q
    t = nb * seq

    def body(xs_ref, bm_ref, cm_ref, z_ref, dtr_ref, bias_ref, alog_ref, dl_ref, nw_ref,
             y_ref, ys_ref, st_ref, state):
        @pl.when(pl.program_id(1) == 0)
        def _():
            state[...] = jnp.zeros_like(state)

        tri, dt, _, cs, cst = _chunk_decays(dtr_ref, bias_ref, alog_ref)
        first = lax.broadcasted_iota(jnp.int32, (1, LANES), 1) < HEAD_DIM
        for g in range(GROUPS):
            gl = slice(g * STATE, (g + 1) * STATE)
            bb = bm_ref[:, gl].astype(BF16)
            cb = cm_ref[:, gl].astype(BF16)
            scores = _dot_nt(cb, bb)
            for p in range(2):
                pr = 2 * g + p
                h0 = 2 * pr
                sl = slice(pr * LANES, (pr + 1) * LANES)
                xv = xs_ref[:, sl]
                dtp = jnp.where(first, dt[:, h0:h0 + 1], dt[:, h0 + 1:h0 + 2])
                csp = jnp.where(first, cs[:, h0:h0 + 1], cs[:, h0 + 1:h0 + 2])
                xd = xv * dtp
                yv = None
                for hh, keep in ((h0, first), (h0 + 1, jnp.logical_not(first))):
                    decay = jnp.where(tri, jnp.exp(cs[:, hh:hh + 1] - cst[hh:hh + 1, :]), 0.0)
                    part = _dot((scores * decay).astype(BF16), jnp.where(keep, xd, 0.0).astype(BF16))
                    yv = part if yv is None else yv + part
                hp = state[pr]
                st_ref[0, pr] = hp
                yv = yv + jnp.exp(csp) * _dot(cb, hp.astype(BF16))
                last = csp[q - 1:q, :]
                state[pr] = jnp.exp(last) * hp + _dot_tn(bb, (xd * jnp.exp(last - csp)).astype(BF16))
                ys_ref[:, sl] = yv + dl_ref[:, sl] * xv
        zv = z_ref[...]
        gated = ys_ref[...] * (zv * _sig(zv))
        for g in range(GROUPS):
            gl = slice(g * GROUP_W, (g + 1) * GROUP_W)
            v = gated[:, gl]
            r = lax.rsqrt(jnp.mean(v * v, axis=-1, keepdims=True) + EPS)
            y_ref[:, gl] = (v * r * nw_ref[:, gl]).astype(BF16)

    blk = lambda w, col: pl.BlockSpec((q, w), lambda b, c: (b * nc + c, col))
    vec = lambda w: pl.BlockSpec((1, w), lambda b, c: (0, 0))
    return pl.pallas_call(
        body,
        grid=(nb, nc),
        in_specs=[blk(D_SSD, 0), blk(GROUPS * STATE, 2), blk(GROUPS * STATE, 3), blk(D_SSD, Z_COL), blk(LANES, 0),
                  vec(LANES), vec(LANES), vec(D_SSD), vec(D_SSD)],
        out_specs=[blk(D_SSD, 0), blk(D_SSD, 0),
                   pl.BlockSpec((1, PAIRS, STATE, LANES), lambda b, c: (b * nc + c, 0, 0, 0))],
        out_shape=[jax.ShapeDtypeStruct((t, D_SSD), BF16), jax.ShapeDtypeStruct((t, D_SSD), F32),
                   jax.ShapeDtypeStruct((nb * nc, PAIRS, STATE, LANES), F32)],
        scratch_shapes=[pltpu.VMEM((PAIRS, STATE, LANES), F32)],
        compiler_params=_cparams(("parallel", "arbitrary")),
        name="ssd_fwd",
    )(xs_all, xs_all, xs_all, proj, dtr, dt_bias, a_log, d_lanes, norm_w)


def _ssd_bwd(duy, proj, ys, xs_all, dtr, states, dt_bias, a_log, d_lanes, norm_w, *, nb, seq):
    q = CHUNK
    nc = seq // q
    t = nb * seq

    def body(dy_ref, z_ref, ys_ref, xs_ref, bm_ref, cm_ref, dtr_ref, st_ref, bias_ref, alog_ref, dl_ref, nw_ref,
             dz_ref, dx_ref, ddtr_ref, small_ref,
             dstate, dys_buf, dcsl, ddtl, dcst, dnw_acc, dd_acc, dbias_acc, da_acc):
        b, c = pl.program_id(0), pl.program_id(1)

        @pl.when(jnp.logical_and(b == 0, c == 0))
        def _():
            dnw_acc[...] = jnp.zeros_like(dnw_acc)
            dd_acc[...] = jnp.zeros_like(dd_acc)
            dbias_acc[...] = jnp.zeros_like(dbias_acc)
            da_acc[...] = jnp.zeros_like(da_acc)
            dcst[...] = jnp.zeros_like(dcst)

        @pl.when(c == 0)
        def _():
            dstate[...] = jnp.zeros_like(dstate)

        zv = z_ref[...]
        sz = _sig(zv)
        silz = zv * sz
        ysv = ys_ref[...]
        gated = ysv * silz
        dyv = dy_ref[...]
        nwv = nw_ref[...]
        for g in range(GROUPS):
            gl = slice(g * GROUP_W, (g + 1) * GROUP_W)
            v = gated[:, gl]
            r = lax.rsqrt(jnp.mean(v * v, axis=-1, keepdims=True) + EPS)
            yn = v * r
            dyn = dyv[:, gl] * nwv[:, gl]
            dnw_acc[:, gl] += jnp.sum(dyv[:, gl] * yn, axis=0, keepdims=True)
            dys_buf[:, gl] = r * (dyn - yn * jnp.mean(dyn * yn, axis=-1, keepdims=True))
        dgated = dys_buf[...]
        dz_ref[...] = (dgated * ysv * (sz * (1.0 + zv * (1.0 - sz)))).astype(BF16)
        dys_all = dgated * silz
        dys_buf[...] = dys_all
        dd_acc[...] += jnp.sum(dys_all * xs_ref[...], axis=0, keepdims=True)

        tri, dt, a_head, cs, cst = _chunk_decays(dtr_ref, bias_ref, alog_ref)
        lane = lax.broadcasted_iota(jnp.int32, (1, LANES), 1)
        first = lane < HEAD_DIM
        dcs_h = jnp.zeros((q, LANES), F32)
        for g in range(GROUPS):
            gl = slice(g * STATE, (g + 1) * STATE)
            bb = bm_ref[:, gl].astype(BF16)
            cb = cm_ref[:, gl].astype(BF16)
            scores = _dot_nt(cb, bb)
            dscores = jnp.zeros((q, q), F32)
            dbg = jnp.zeros((q, STATE), F32)
            dcg = jnp.zeros((q, STATE), F32)
            for p in range(2):
                pr = 2 * g + p
                h0 = 2 * pr
                sl = slice(pr * LANES, (pr + 1) * LANES)
                xv = xs_ref[:, sl]
                dyp = dys_buf[:, sl]
                dtp = jnp.where(first, dt[:, h0:h0 + 1], dt[:, h0 + 1:h0 + 2])
                csp = jnp.where(first, cs[:, h0:h0 + 1], cs[:, h0 + 1:h0 + 2])
                xd = xv * dtp
                xdb = xd.astype(BF16)
                hp = st_ref[0, pr]
                dhn = dstate[pr]
                hpb = hp.astype(BF16)
                dhnb = dhn.astype(BF16)
                lam = jnp.exp(csp)
                last = csp[q - 1:q, :]
                gam = jnp.exp(last)
                w = jnp.exp(last - csp)
                dxd = jnp.zeros((q, LANES), F32)
                for hh, keep in ((h0, first), (h0 + 1, jnp.logical_not(first))):
                    decay = jnp.where(tri, jnp.exp(cs[:, hh:hh + 1] - cst[hh:hh + 1, :]), 0.0)
                    m = scores * decay
                    dym = jnp.where(keep, dyp, 0.0).astype(BF16)
                    dm = _dot_nt(dym, xdb)
                    dxd = dxd + _dot_tn(m.astype(BF16), dym)
                    e = dm * m
                    dcs_h = dcs_h + jnp.where(lane == hh, jnp.sum(e, axis=1, keepdims=True), 0.0)
                    dcst[hh:hh + 1, :] = jnp.sum(e, axis=0, keepdims=True)
                    dscores = dscores + dm * decay
                yoff = lam * _dot(cb, hpb)
                ldy = (lam * dyp).astype(BF16)
                dcg = dcg + _dot_nt(ldy, hpb)
                dstate[pr] = gam * dhn + _dot_tn(cb, ldy)
                bdh = _dot(bb, dhnb)
                dxd = dxd + w * bdh
                xdw = xd * w
                dbg = dbg + _dot_nt(xdw.astype(BF16), dhnb)
                wd = xdw * bdh
                dcsl[:, sl] = dyp * yoff - wd
                dcsl[q - 1:q, sl] += (jnp.sum(wd, axis=0, keepdims=True)
                                      + gam * jnp.sum(dhn * hp, axis=0, keepdims=True))
                dx_ref[:, sl] = dxd * dtp + dyp * dl_ref[:, sl]
                ddtl[:, sl] = dxd * xv
            dsb = dscores.astype(BF16)
            dx_ref[:, D_SSD + g * STATE:D_SSD + (g + 1) * STATE] = dbg + _dot_tn(dsb, cb)
            dx_ref[:, D_SSD + (GROUPS + g) * STATE:D_SSD + (GROUPS + g + 1) * STATE] = dcg + _dot(dsb, bb)

        li = lax.broadcasted_iota(jnp.int32, (D_SSD, LANES), 0)
        hi = lax.broadcasted_iota(jnp.int32, (D_SSD, LANES), 1)
        sel = (li // HEAD_DIM == hi).astype(F32)
        dcs_h = dcs_h + _dot_exact(dcsl[...], sel) - dcst[...].T
        ddt = _dot_exact(ddtl[...], sel)
        upper = lax.broadcasted_iota(jnp.int32, (q, q), 1) >= lax.broadcasted_iota(jnp.int32, (q, q), 0)
        da = _dot_exact(upper.astype(F32), dcs_h)
        ddt = ddt + da * a_head
        da_acc[...] += jnp.sum(da * dt, axis=0, keepdims=True)
        ddtr = ddt * _sig(dtr_ref[...] + bias_ref[...])
        ddtr_ref[...] = ddtr
        dbias_acc[...] += jnp.sum(ddtr, axis=0, keepdims=True)

        @pl.when(jnp.logical_and(b == nb - 1, c == nc - 1))
        def _():
            small_ref[...] = jnp.zeros_like(small_ref)
            small_ref[0:1, :] = dnw_acc[...]
            small_ref[1:2, 0:LANES] = _dot_exact(jnp.broadcast_to(dd_acc[...], (8, D_SSD)), sel)[0:1, :]
            small_ref[2:3, 0:LANES] = dbias_acc[...]
            small_ref[3:4, 0:LANES] = da_acc[...] * a_head

    rowblk = lambda b, c: b * nc + (nc - 1 - c)
    blk = lambda w, col: pl.BlockSpec((q, w), lambda b, c: (rowblk(b, c), col))
    vec = lambda w: pl.BlockSpec((1, w), lambda b, c: (0, 0))
    return pl.pallas_call(
        body,
        grid=(nb, nc),
        in_specs=[blk(D_SSD, 1), blk(D_SSD, Z_COL), blk(D_SSD, 0), blk(D_SSD, 0), blk(GROUPS * STATE, 2),
                  blk(GROUPS * STATE, 3), blk(LANES, 0),
                  pl.BlockSpec((1, PAIRS, STATE, LANES), lambda b, c: (rowblk(b, c), 0, 0, 0)),
                  vec(LANES), vec(LANES), vec(D_SSD), vec(D_SSD)],
        out_specs=[blk(D_SSD, 0), blk(D_XBC, 0), blk(LANES, 0), pl.BlockSpec((8, D_SSD), lambda b, c: (0, 0))],
        out_shape=[jax.ShapeDtypeStruct((t, D_SSD), BF16), jax.ShapeDtypeStruct((t, D_XBC), F32),
                   jax.ShapeDtypeStruct((t, LANES), F32), jax.ShapeDtypeStruct((8, D_SSD), F32)],
        scratch_shapes=[pltpu.VMEM((PAIRS, STATE, LANES), F32), pltpu.VMEM((q, D_SSD), F32),
                        pltpu.VMEM((q, D_SSD), F32), pltpu.VMEM((q, D_SSD), F32), pltpu.VMEM((LANES, q), F32),
                        pltpu.VMEM((1, D_SSD), F32), pltpu.VMEM((1, D_SSD), F32), pltpu.VMEM((1, LANES), F32),
                        pltpu.VMEM((1, LANES), F32)],
        compiler_params=_cparams(("arbitrary", "arbitrary")),
        name="ssd_bwd",
    )(duy, proj, ys, xs_all, xs_all, xs_all, dtr, states, dt_bias, a_log, d_lanes, norm_w)


HBM_SPEC = pl.BlockSpec(memory_space=pltpu.HBM)
MESH_ID = pl.DeviceIdType.MESH


def _coords():
    return lax.axis_index("x"), lax.axis_index("y"), lax.axis_index("c")


def _chip_peer(xi, yi, ci, d):
    return (jnp.bitwise_xor(xi, d >> 1), jnp.bitwise_xor(yi, d & 1), ci)


def _remote(src, dst, send_sem, recv_sem, peer):
    return pltpu.make_async_remote_copy(src_ref=src, dst_ref=dst, send_sem=send_sem, recv_sem=recv_sem,
                                        device_id=peer, device_id_type=MESH_ID)


SEM_SPEC = pl.BlockSpec(memory_space=pltpu.SEMAPHORE)
ANY_SPEC = pl.BlockSpec(memory_space=pl.ANY)
EFFECT = pltpu.SideEffectType.DATAFLOW_SIDE_EFFECTING
COPIES = 3


def _gather_plan(halved):
    def plan(xi, yi, ci, src, land):
        me = 2 * xi + yi
        out = []
        for d in (1, 2, 3):
            if halved:
                hr = src.shape[0] // 2
                rows = pl.ds(ci * hr, hr)
                out.append((src.at[rows, :], land.at[me, rows, :], _chip_peer(xi, yi, ci, d),
                            land.at[jnp.bitwise_xor(me, d), rows, :]))
            else:
                out.append((src, land.at[me], _chip_peer(xi, yi, ci, d), land.at[jnp.bitwise_xor(me, d)]))
        return out
    return plan


def _owners_plan(xi, yi, ci, src, land):
    me = 2 * xi + yi
    return [(src.at[jnp.bitwise_xor(me, d)], land.at[d - 1], _chip_peer(xi, yi, ci, d), land.at[d - 1])
            for d in (1, 2, 3)]


def _split_start(srcs, lands, plans, *, name):
    n = len(srcs)

    def body(*refs):
        src_refs, land_refs = refs[:n], refs[n:2 * n]
        ssems, rsems = refs[2 * n:3 * n], refs[3 * n:4 * n]
        token = refs[-1]
        xi, yi, ci = _coords()
        for t in range(n):
            for k, (src, dst, peer, _) in enumerate(plans[t](xi, yi, ci, src_refs[t], land_refs[t])):
                _remote(src, dst, ssems[t].at[k], rsems[t].at[k], peer).start()
        token[...] = jnp.zeros_like(token)

    bufs = list(srcs) + list(lands)
    outs = pl.pallas_call(
        body,
        name=name,
        in_specs=[HBM_SPEC] * (2 * n),
        out_specs=[SEM_SPEC] * (2 * n) + [HBM_SPEC] * (2 * n) + [pl.BlockSpec(memory_space=pltpu.VMEM)],
        out_shape=[pltpu.SemaphoreType.DMA((COPIES,))] * (2 * n) + [pltpu.HBM(a.shape, a.dtype) for a in bufs]
        + [jax.ShapeDtypeStruct((8, LANES), F32)],
        input_output_aliases={i: 2 * n + i for i in range(2 * n)},
        compiler_params=pltpu.CompilerParams(has_side_effects=EFFECT),
    )(*[pltpu.with_memory_space_constraint(a, pltpu.HBM) for a in bufs])
    return outs[:n], outs[n:2 * n], outs[2 * n:3 * n], outs[3 * n:4 * n], outs[-1]


def _split_wait(ssems, rsems, srcs, lands, plans, after, *, name):
    n = len(srcs)

    def body(*refs):
        src_refs, land_refs = refs[:n], refs[n:2 * n]
        ss, rs = refs[2 * n:3 * n], refs[3 * n:4 * n]
        xi, yi, ci = _coords()
        for t in range(n):
            for k, (src, _, peer, landed) in enumerate(plans[t](xi, yi, ci, src_refs[t], land_refs[t])):
                cp = _remote(src, landed, ss[t].at[k], rs[t].at[k], peer)
                cp.wait_send()
                cp.wait_recv()

    bufs = list(srcs) + list(lands)
    outs = pl.pallas_call(
        body,
        name=name,
        in_specs=[HBM_SPEC] * (2 * n) + [SEM_SPEC] * (2 * n) + [ANY_SPEC],
        out_specs=[HBM_SPEC] * (2 * n),
        out_shape=[pltpu.HBM(a.shape, a.dtype) for a in bufs],
        input_output_aliases={i: i for i in range(2 * n)},
        compiler_params=pltpu.CompilerParams(has_side_effects=EFFECT),
    )(*bufs, *ssems, *rsems, after)
    return outs[:n], outs[n:]


def _forward_halves(lands, *, name):
    n = len(lands)

    def body(*refs):
        ins, outs = refs[:n], refs[n:2 * n]
        send_sems, recv_sems = refs[2 * n:]
        xi, yi, ci = _coords()
        me = 2 * xi + yi
        sibling = (xi, yi, 1 - ci)
        cps = []
        for t in range(n):
            hr = lands[t].shape[1] // 2
            mine = pl.ds(ci * hr, hr)
            for d in (1, 2, 3):
                slot = jnp.bitwise_xor(me, d)
                k = COPIES * t + d - 1
                cp = _remote(ins[t].at[slot, mine, :], outs[t].at[slot, mine, :], send_sems.at[k], recv_sems.at[k], sibling)
                cp.start()
                cps.append(cp)
        for t in range(n):
            hr = lands[t].shape[1] // 2
            theirs = pl.ds((1 - ci) * hr, hr)
            for d in (1, 2, 3):
                got = outs[t].at[jnp.bitwise_xor(me, d), theirs, :]
                k = COPIES * t + d - 1
                _remote(got, got, send_sems.at[k], recv_sems.at[k], sibling).wait_recv()
        for cp in cps:
            cp.wait_send()

    return pl.pallas_call(
        body,
        name=name,
        in_specs=[HBM_SPEC] * n,
        out_specs=[HBM_SPEC] * n,
        out_shape=[jax.ShapeDtypeStruct(a.shape, a.dtype) for a in lands],
        input_output_aliases={i: i for i in range(n)},
        scratch_shapes=[pltpu.SemaphoreType.DMA((COPIES * n,)), pltpu.SemaphoreType.DMA((COPIES * n,))],
    )(*lands)


def _swap_other_halves(gs, *, name):
    n = len(gs)

    def body(*refs):
        ins, lands = refs[:n], refs[n:2 * n]
        send_sems, recv_sems = refs[2 * n:]
        xi, yi, ci = _coords()
        sibling = (xi, yi, 1 - ci)
        cps = []
        for t in range(n):
            hr = gs[t].shape[1] // 2
            cp = _remote(ins[t].at[:, pl.ds((1 - ci) * hr, hr), :], lands[t], send_sems.at[t], recv_sems.at[t], sibling)
            cp.start()
            cps.append(cp)
        for cp in cps:
            cp.wait_recv()
        for cp in cps:
            cp.wait_send()

    return pl.pallas_call(
        body,
        in_specs=[HBM_SPEC] * n,
        out_specs=[HBM_SPEC] * n,
        out_shape=[jax.ShapeDtypeStruct((g.shape[0], g.shape[1] // 2, g.shape[2]), g.dtype) for g in gs],
        scratch_shapes=[pltpu.SemaphoreType.DMA((n,)), pltpu.SemaphoreType.DMA((n,))],
        name=name,
    )(*gs)


def _row_tile(rows, cap=512, mult=16):
    best = mult
    for cand in range(mult, min(rows, cap) + 1, mult):
        if rows % cand == 0:
            best = cand
    assert rows % best == 0, rows
    return best


def _add_core_halves(g, land, where):
    nslot, rows, cols = g.shape
    hr = rows // 2
    tr = _row_tile(hr)
    nr = hr // tr

    def body(where_ref, g_ref, l_ref, f_ref, b_ref):
        s = g_ref[...] + l_ref[...]
        f_ref[...] = s
        b_ref[...] = s.astype(BF16)

    blk = pl.BlockSpec((None, tr, cols), lambda s, i, w: (s, i, 0))
    return pl.pallas_call(
        body,
        grid_spec=pltpu.PrefetchScalarGridSpec(
            num_scalar_prefetch=1,
            grid=(nslot, nr),
            in_specs=[pl.BlockSpec((None, tr, cols), lambda s, i, w: (s, w[0] * nr + i, 0)), blk],
            out_specs=[blk, blk],
        ),
        out_shape=[jax.ShapeDtypeStruct((nslot, hr, cols), F32), jax.ShapeDtypeStruct((nslot, hr, cols), BF16)],
        compiler_params=_cparams(("parallel", "parallel")),
        name="add_core_halves",
    )(where, g, land)


def _send_to_owners(ps):
    n = len(ps)

    def body(*refs):
        ins, lands = refs[:n], refs[n:2 * n]
        send_sems, recv_sems = refs[2 * n:]
        xi, yi, ci = _coords()
        me = 2 * xi + yi
        cps = []
        for t in range(n):
            for d in (1, 2, 3):
                k = 3 * t + d - 1
                cp = _remote(ins[t].at[jnp.bitwise_xor(me, d)], lands[t].at[d - 1], send_sems.at[k], recv_sems.at[k],
                             _chip_peer(xi, yi, ci, d))
                cp.start()
                cps.append(cp)
        for cp in cps:
            cp.wait_recv()
        for cp in cps:
            cp.wait_send()

    return pl.pallas_call(
        body,
        in_specs=[HBM_SPEC] * n,
        out_specs=[HBM_SPEC] * n,
        out_shape=[jax.ShapeDtypeStruct((3,) + p.shape[1:], p.dtype) for p in ps],
        scratch_shapes=[pltpu.SemaphoreType.DMA((3 * n,)), pltpu.SemaphoreType.DMA((3 * n,))],
        name="send_to_owners",
    )(*ps)


def _add_chip_sums(pf, land, where):
    _, hr, cols = pf.shape
    tr = _row_tile(hr)

    def body(where_ref, p_ref, l_ref, o_ref):
        acc = p_ref[...]
        for d in range(3):
            acc = acc + l_ref[d].astype(F32)
        o_ref[...] = acc

    return pl.pallas_call(
        body,
        grid_spec=pltpu.PrefetchScalarGridSpec(
            num_scalar_prefetch=1,
            grid=(hr // tr,),
            in_specs=[pl.BlockSpec((None, tr, cols), lambda i, w: (w[1], i, 0)),
                      pl.BlockSpec((3, tr, cols), lambda i, w: (0, i, 0))],
            out_specs=pl.BlockSpec((tr, cols), lambda i, w: (i, 0)),
        ),
        out_shape=jax.ShapeDtypeStruct((hr, cols), F32),
        compiler_params=_cparams(("parallel",)),
        name="add_chip_sums",
    )(where, pf, land)


def _swap_reduced_halves(rs):
    n = len(rs)

    def body(*refs):
        ins, outs = refs[:n], refs[n:2 * n]
        send_sems, recv_sems = refs[2 * n:]
        xi, yi, ci = _coords()
        sibling = (xi, yi, 1 - ci)
        cps = [_remote(ins[t], outs[t], send_sems.at[t], recv_sems.at[t], sibling) for t in range(n)]
        for cp in cps:
            cp.start()
        for cp in cps:
            cp.wait_recv()
        for cp in cps:
            cp.wait_send()

    return pl.pallas_call(
        body,
        in_specs=[HBM_SPEC] * n,
        out_specs=[HBM_SPEC] * n,
        out_shape=[jax.ShapeDtypeStruct(r.shape, r.dtype) for r in rs],
        scratch_shapes=[pltpu.SemaphoreType.DMA((n,)), pltpu.SemaphoreType.DMA((n,))],
        name="swap_reduced_halves",
    )(*rs)


N_DEV = 8


def _all_reduce_small(part):
    r, w = part.shape

    def body(p_ref, o_ref, gath, send_sems, recv_sems):
        xi, yi, ci = _coords()
        me = 4 * xi + 2 * yi + ci
        gath[me] = p_ref[...]
        cps = []
        for d in range(1, N_DEV):
            peer = (jnp.bitwise_xor(xi, d >> 2), jnp.bitwise_xor(yi, (d >> 1) & 1), jnp.bitwise_xor(ci, d & 1))
            cp = _remote(p_ref, gath.at[me], send_sems.at[d - 1], recv_sems.at[d - 1], peer)
            cp.start()
            cps.append(cp)
        for d in range(1, N_DEV):
            src = gath.at[jnp.bitwise_xor(me, d)]
            _remote(src, src, send_sems.at[d - 1], recv_sems.at[d - 1], (xi, yi, ci)).wait_recv()
        acc = gath[0]
        for k in range(1, N_DEV):
            acc = acc + gath[k]
        o_ref[...] = acc
        for cp in cps:
            cp.wait_send()

    vm = pl.BlockSpec(memory_space=pltpu.VMEM)
    return pl.pallas_call(
        body,
        in_specs=[vm],
        out_specs=vm,
        out_shape=jax.ShapeDtypeStruct((r, w), F32),
        scratch_shapes=[pltpu.VMEM((N_DEV, r, w), F32), pltpu.SemaphoreType.DMA((N_DEV - 1,)),
                        pltpu.SemaphoreType.DMA((N_DEV - 1,))],
        name="all_reduce_small",
    )(part)


def _adamw_math(wv, gv, mv, vv):
    mn = ADAM_B1 * mv + (1.0 - ADAM_B1) * gv
    vn = ADAM_B2 * vv + (1.0 - ADAM_B2) * (gv * gv)
    m_hat = mn / (1.0 - ADAM_B1 ** ADAM_STEP)
    v_hat = vn / (1.0 - ADAM_B2 ** ADAM_STEP)
    return -ADAM_LR * (m_hat / (jnp.sqrt(v_hat) + ADAM_EPS) + ADAM_WD * wv), mn, vn


def _adamw_halves(w, g_mine, g_other, m, v, where, *, name):
    rows, cols = w.shape
    hr = rows // 2
    tr = _row_tile(hr, cap=256, mult=8)
    nr = hr // tr

    def body(where_ref, w_ref, gm_ref, go_ref, m_ref, v_ref, g_ref, d_ref, nm_ref, nv_ref):
        is_mine = pl.program_id(0) // nr == where_ref[0]
        gv = jnp.where(is_mine, gm_ref[...], go_ref[...])
        g_ref[...] = gv
        d_ref[...], nm_ref[...], nv_ref[...] = _adamw_math(w_ref[...], gv, m_ref[...], v_ref[...])

    def half_idx(i, holder):
        half = i // nr
        return jnp.where(half == holder, i % nr, jnp.where(half < holder, 0, nr - 1))

    blk = pl.BlockSpec((tr, cols), lambda i, wh: (i, 0))
    o = jax.ShapeDtypeStruct((rows, cols), F32)
    return pl.pallas_call(
        body,
        grid_spec=pltpu.PrefetchScalarGridSpec(
            num_scalar_prefetch=1,
            grid=(2 * nr,),
            in_specs=[blk, pl.BlockSpec((tr, cols), lambda i, wh: (half_idx(i, wh[0]), 0)),
                      pl.BlockSpec((tr, cols), lambda i, wh: (half_idx(i, 1 - wh[0]), 0)), blk, blk],
            out_specs=[blk] * 4,
        ),
        out_shape=[o, o, o, o],
        compiler_params=_cparams(("arbitrary",)),
        name=name,
    )(where, w, g_mine, g_other, m, v)


def _adamw(w, g, m, v, *, name):
    rows, cols = w.shape
    tr = _row_tile(rows, cap=256, mult=8)

    def body(w_ref, g_ref, m_ref, v_ref, d_ref, nm_ref, nv_ref):
        d_ref[...], nm_ref[...], nv_ref[...] = _adamw_math(w_ref[...], g_ref[...], m_ref[...], v_ref[...])

    blk = pl.BlockSpec((tr, cols), lambda i: (i, 0))
    o = jax.ShapeDtypeStruct((rows, cols), F32)
    return pl.pallas_call(
        body,
        grid=(rows // tr,),
        in_specs=[blk] * 4,
        out_specs=[blk] * 3,
        out_shape=[o, o, o],
        compiler_params=_cparams(("parallel",)),
        name=name,
    )(w, g, m, v)


def _pack(arrs):
    flat = jnp.concatenate([a.reshape(-1) for a in arrs])
    pad = (-flat.shape[0]) % (8 * LANES)
    return jnp.pad(flat, (0, pad)).reshape(-1, LANES)


def _unpack(packed, shapes):
    flat = packed.reshape(-1)
    out, off = [], 0
    for s in shapes:
        n = 1
        for dim in s:
            n *= dim
        out.append(flat[off:off + n].reshape(s))
        off += n
    return out


def _pad_rows(a, rows):
    return jnp.pad(a, ((0, rows - a.shape[0]), (0, 0)))


def _pad_lanes(a, lanes=LANES):
    return jnp.pad(a, ((0, 0), (0, lanes - a.shape[1])))


def _local_grads(x2d, tgt2d, prm, get_w, on_ffn_grads, *, nb, seq, dep=None):
    g_pre, g_post, g_fpre, g_fpost = prm["norm_mix_pre"], prm["norm_mix_post"], prm["norm_ffn_pre"], prm["norm_ffn_post"]
    dt_bias, a_log = _pad_lanes(prm["ssd_dt_bias"]), _pad_lanes(prm["ssd_a_log"])
    d_lanes = jnp.repeat(prm["ssd_d"], HEAD_DIM, axis=1)

    h = _rms_fwd(x2d, g_pre, dep=dep, out_dtype=BF16, name="rms_mix_pre")
    w_main, w_dt, cw, sw = get_w("in", h)
    proj = _matmul([(h, w_main)], mode="nn", out_dtype=F32, tm=1024, tn=1024, tk=2048, name="mm_proj")
    dtr = _matmul([(h, w_dt)], mode="nn", out_dtype=F32, tm=1024, tn=128, tk=2048, name="mm_dt")
    u1, u = _conv_branch_fwd(proj, cw, prm["conv_dw_b"], prm["conv_ln_g"], prm["conv_ln_b"], nb=nb, seq=seq)
    xs_all = _ssd_pre_fwd(proj, sw, prm["ssd_conv_b"], nb=nb, seq=seq)
    y, ys, states = _ssd_fwd(xs_all, proj, dtr, dt_bias, a_log, d_lanes, prm["ssd_norm_w"], nb=nb, seq=seq)
    w_out = get_w("out", y)
    mix = _matmul([(u, w_out[:D_CONV]), (y, w_out[D_CONV:])], mode="nn", out_dtype=F32, tm=1024, tn=1024, tk=1024,
                  name="mm_mix")
    x1 = _rms_fwd(mix, g_post, res=x2d, out_dtype=F32, name="rms_mix_post")
    h2 = _rms_fwd(x1, g_fpre, out_dtype=BF16, name="rms_ffn_pre")
    w_gate, w_up = get_w("up", h2)
    gt, up, act = _ffn_up(h2, w_gate, w_up, tm=1024, tn=512)
    w_down = get_w("down", act)
    f = _matmul([(act, w_down)], mode="nn", out_dtype=F32, tm=1024, tn=1024, tk=1408, name="mm_down")
    dx2, df, loss, dg_fpost = _ffn_post_loss(f, x1, tgt2d, g_fpost)

    dgt, dup = _ffn_bwd_act(df, w_down, gt, up, tm=1024, tn=512)
    dw_down = _matmul([(act, df)], mode="tn", out_dtype=F32, tm=1408, tn=1024, tk=1024, name="mm_dw_down")
    dh2 = _matmul([(dgt, w_gate), (dup, w_up)], mode="nt", out_dtype=F32, tm=1024, tn=1024, tk=1408, name="mm_dh2")
    dw_gate = _matmul([(h2, dgt)], mode="tn", out_dtype=F32, tm=1024, tn=1408, tk=1024, name="mm_dw_gate", slot_out=True)
    dw_up = _matmul([(h2, dup)], mode="tn", out_dtype=F32, tm=1024, tn=1408, tk=1024, name="mm_dw_up", slot_out=True)
    ffn_dep = on_ffn_grads(dw_down, dw_gate, dw_up)
    dx1, dg_fpre = _rms_bwd([dh2], x1, g_fpre, addend=dx2, dep=ffn_dep, out_dtype=F32, name="rms_ffn_pre_bwd")
    dmix, dg_post = _rms_bwd([dx1], mix, g_post, out_dtype=BF16, name="rms_mix_post_bwd")
    duy = _matmul([(dmix, w_out)], mode="nt", out_dtype=F32, tm=1024, tn=1024, tk=2048, name="mm_duy")
    dw_out_u = _matmul([(u, dmix)], mode="tn", out_dtype=F32, tm=1024, tn=1024, tk=1024, name="mm_dw_out_u")
    dw_out_y = _matmul([(y, dmix)], mode="tn", out_dtype=F32, tm=1024, tn=1024, tk=1024, name="mm_dw_out_y")
    dcacg, dcw, dcb, dlg, dlb = _conv_branch_bwd(duy, u1, proj, cw, prm["conv_ln_g"], prm["conv_ln_b"], nb=nb, seq=seq)
    dz, dxs, ddtr, ssd_small = _ssd_bwd(duy, proj, ys, xs_all, dtr, states, dt_bias, a_log, d_lanes,
                                        prm["ssd_norm_w"], nb=nb, seq=seq)
    dxbc, dsw, dsb = _ssd_pre_bwd(dxs, proj, sw, prm["ssd_conv_b"], nb=nb, seq=seq)
    dproj = jnp.concatenate([dcacg, dz, dxbc], axis=1)
    ddtr_b = ddtr.astype(BF16)
    dh_main = _matmul([(dproj, w_main)], mode="nt", out_dtype=F32, tm=1024, tn=1024, tk=1024, name="mm_dh_main")
    dh_dt = _matmul([(ddtr_b, w_dt)], mode="nt", out_dtype=F32, tm=1024, tn=1024, tk=128, name="mm_dh_dt")
    dx, dg_pre = _rms_bwd([dh_main, dh_dt], x2d, g_pre, addend=dx1, out_dtype=F32, name="rms_mix_pre_bwd")
    dw_main = _matmul([(h, dproj)], mode="tn", out_dtype=F32, tm=1024, tn=1024, tk=1024, name="mm_dw_main")
    dw_dt = _matmul([(h, ddtr_b)], mode="tn", out_dtype=F32, tm=1024, tn=128, tk=1024, name="mm_dw_dt")

    grads = {
        "norm_mix_pre": dg_pre,
        "w_in": jnp.concatenate([dw_main, dw_dt[:, :HEADS]], axis=1),
        "conv_dw_w": dcw[:CONV_K], "conv_dw_b": dcb, "conv_ln_g": dlg, "conv_ln_b": dlb,
        "ssd_conv_w": dsw[:SSD_CONV_K], "ssd_conv_b": dsb,
        "ssd_dt_bias": ssd_small[2:3, :HEADS], "ssd_a_log": ssd_small[3:4, :HEADS], "ssd_d": ssd_small[1:2, :HEADS],
        "ssd_norm_w": ssd_small[0:1],
        "w_out": jnp.concatenate([dw_out_u, dw_out_y], axis=0),
        "norm_mix_post": dg_post, "norm_ffn_pre": dg_fpre,
        "w_gate": dw_gate, "w_up": dw_up,
        "w_down": dw_down, "norm_ffn_post": dg_fpost,
    }
    return loss, dx, grads


BIG = ("w_in", "w_out", "w_gate", "w_up", "w_down")
GATHER_STAGES = {"in": ("w_in", "conv_dw_w", "ssd_conv_w"), "out": ("w_out",), "up": ("w_gate", "w_up"),
                 "down": ("w_down",)}
GATHER_ORDER = tuple(n for st in ("in", "out", "up", "down") for n in GATHER_STAGES[st])
SMALL = ("norm_mix_pre", "conv_dw_w", "conv_dw_b", "conv_ln_g", "conv_ln_b", "ssd_conv_w", "ssd_conv_b", "ssd_dt_bias",
         "ssd_a_log", "ssd_d", "ssd_norm_w", "norm_mix_post", "norm_ffn_pre", "norm_ffn_post")
WEIGHTS = ("norm_mix_pre", "w_in", "conv_dw_w", "conv_dw_b", "conv_ln_g", "conv_ln_b", "ssd_conv_w", "ssd_conv_b",
           "ssd_dt_bias", "ssd_a_log", "ssd_d", "ssd_norm_w", "w_out", "norm_mix_post", "norm_ffn_pre", "w_gate", "w_up",
           "w_down", "norm_ffn_post")


def _slots_by_cols(a):
    rows, cols = a.shape
    return a.reshape(rows, N_CHIPS, cols // N_CHIPS).transpose(1, 0, 2)


def _cols_from_slots(a):
    n, rows, w = a.shape
    return a.transpose(1, 0, 2).reshape(rows, n * w)


def kernel(x, norm_mix_pre, w_in, conv_dw_w, conv_dw_b, conv_ln_g, conv_ln_b, ssd_conv_w, ssd_conv_b, ssd_dt_bias, ssd_a_log, ssd_d, ssd_norm_w, w_out, norm_mix_post, norm_ffn_pre, w_gate, w_up, w_down, norm_ffn_post, loss_target, m_norm_mix_pre, m_w_in, m_conv_dw_w, m_conv_dw_b, m_conv_ln_g, m_conv_ln_b, m_ssd_conv_w, m_ssd_conv_b, m_ssd_dt_bias, m_ssd_a_log, m_ssd_d, m_ssd_norm_w, m_w_out, m_norm_mix_post, m_norm_ffn_pre, m_w_gate, m_w_up, m_w_down, m_norm_ffn_post, v_norm_mix_pre, v_w_in, v_conv_dw_w, v_conv_dw_b, v_conv_ln_g, v_conv_ln_b, v_ssd_conv_w, v_ssd_conv_b, v_ssd_dt_bias, v_ssd_a_log, v_ssd_d, v_ssd_norm_w, v_w_out, v_norm_mix_post, v_norm_ffn_pre, v_w_gate, v_w_up, v_w_down, v_norm_ffn_post):
    args = dict(locals())
    two_d = lambda a: a.reshape(a.shape[-2:])
    wts = {n: two_d(args[n]) for n in WEIGHTS}
    ms = {n: two_d(args["m_" + n]) for n in WEIGHTS}
    vs = {n: two_d(args["v_" + n]) for n in WEIGHTS}
    nb, seq, d = x.shape
    t = nb * seq
    xi, yi, ci = _coords()
    chip = 2 * xi + yi
    where = jnp.stack([ci, chip]).astype(jnp.int32)

    shards = {n: wts[n].astype(BF16) for n in BIG}
    shards.update(conv_dw_w=_pad_rows(wts["conv_dw_w"], 32), ssd_conv_w=_pad_rows(wts["ssd_conv_w"], 8))
    srcs = [shards[n] for n in GATHER_ORDER]
    plans = [_gather_plan(n in BIG) for n in GATHER_ORDER]
    ssems, rsems, srcs, lands, token = _split_start(
        srcs, [lax.empty((N_CHIPS,) + s.shape, s.dtype) for s in srcs], plans, name="gather_start")

    def get_w(stage, after):
        idx = [GATHER_ORDER.index(n) for n in GATHER_STAGES[stage]]
        pick = lambda seq_: [seq_[i] for i in idx]
        own, got = _split_wait(pick(ssems), pick(rsems), pick(srcs), pick(lands), pick(plans), after,
                               name="gather_wait_" + stage)
        got = dict(zip(GATHER_STAGES[stage], got))
        own = dict(zip(GATHER_STAGES[stage], own))
        big = [n for n in GATHER_STAGES[stage] if n in BIG]
        got.update(zip(big, _forward_halves([got[n] for n in big], name="gather_forward_" + stage)))
        full = {n: lax.dynamic_update_slice(got[n], own[n][None], (chip, 0, 0)) for n in got}
        if stage == "in":
            w_in_full = _cols_from_slots(full["w_in"])
            return (w_in_full[:, :D_MAIN], _pad_lanes(w_in_full[:, D_MAIN:]), _cols_from_slots(full["conv_dw_w"]),
                    _cols_from_slots(full["ssd_conv_w"]))
        if stage == "out":
            return full["w_out"].reshape(D_MODEL, D_MODEL)
        if stage == "up":
            return _cols_from_slots(full["w_gate"]), _cols_from_slots(full["w_up"])
        return full["w_down"].reshape(D_FF, D_MODEL)

    ffn = {}

    def on_ffn_grads(dw_down, dw_gate, dw_up):
        slots = [dw_down.reshape(N_CHIPS, D_FF // N_CHIPS, D_MODEL), dw_gate, dw_up]
        sums = [_add_core_halves(g, l, where)
                for g, l in zip(slots, _swap_other_halves(slots, name="swap_other_halves_ffn"))]
        ps = [s[1] for s in sums]
        ffn["sums"] = sums
        ffn["ssems"], ffn["rsems"], ffn["ps"], ffn["lands"], started = _split_start(
            ps, [lax.empty((COPIES,) + p.shape[1:], p.dtype) for p in ps], [_owners_plan] * len(ps),
            name="owners_start_ffn")
        return started

    prm = {n: wts[n] for n in SMALL}
    loss, dx, grads = _local_grads(x.reshape(t, d), loss_target.reshape(t, d), prm, get_w, on_ffn_grads,
                                   nb=nb, seq=seq, dep=token)
    loss = lax.psum(loss[0, 0], MESH_AXES)

    slots = [_slots_by_cols(grads["w_in"]), grads["w_out"].reshape(N_CHIPS, D_MODEL // N_CHIPS, D_MODEL)]
    sums = [_add_core_halves(g, l, where) for g, l in zip(slots, _swap_other_halves(slots, name="swap_other_halves_mix"))]
    _, ffn_recv = _split_wait(ffn["ssems"], ffn["rsems"], ffn["ps"], ffn["lands"], [_owners_plan] * 3, sums[0][1],
                              name="owners_wait_ffn")
    recv = _send_to_owners([s[1] for s in sums])
    halves = dict(zip(("w_in", "w_out"), [_add_chip_sums(s[0], r, where) for s, r in zip(sums, recv)]))
    halves.update(zip(("w_down", "w_gate", "w_up"),
                      [_add_chip_sums(s[0], r, where) for s, r in zip(ffn["sums"], ffn_recv)]))
    halves = [halves[n] for n in BIG]
    other_halves = _swap_reduced_halves(halves)

    small_shapes = [grads[n].shape for n in SMALL]
    small_sum = _unpack(_all_reduce_small(_pack([grads[n] for n in SMALL])), small_shapes)
    small_grads = dict(zip(SMALL, small_sum))
    cwid, swid = D_CONV // N_CHIPS, D_XBC // N_CHIPS
    small_grads["conv_dw_w"] = lax.dynamic_slice(small_grads["conv_dw_w"], (0, chip * cwid), (CONV_K, cwid))
    small_grads["ssd_conv_w"] = lax.dynamic_slice(small_grads["ssd_conv_w"], (0, chip * swid), (SSD_CONV_K, swid))

    out_g, out_d, out_m, out_v = {}, {}, {}, {}
    for n, mine, other in zip(BIG, halves, other_halves):
        out_g[n], out_d[n], out_m[n], out_v[n] = _adamw_halves(wts[n], mine, other, ms[n], vs[n], where,
                                                                 name="adamw_" + n)
    shard_shapes = [wts[n].shape for n in SMALL]
    pd, pm, pv = _adamw(_pack([wts[n] for n in SMALL]), _pack([small_grads[n] for n in SMALL]),
                        _pack([ms[n] for n in SMALL]), _pack([vs[n] for n in SMALL]), name="adamw_small")
    for n, dd, mm, vv in zip(SMALL, _unpack(pd, shard_shapes), _unpack(pm, shard_shapes), _unpack(pv, shard_shapes)):
        out_g[n], out_d[n], out_m[n], out_v[n] = small_grads[n], dd, mm, vv

    outs = [o[n].reshape(args[n].shape) for o in (out_g, out_d, out_m, out_v) for n in WEIGHTS]
    return (loss, dx.reshape(nb, seq, d), *outs)
```

```python
import functools

import jax
import jax.numpy as jnp
from jax import lax
from jax.experimental import pallas as pl
from jax.experimental.pallas import tpu as pltpu

F32 = jnp.float32
BF16 = jnp.bfloat16
EPS = 1e-6

D_MODEL = 2048
D_CONV = 1024
D_SSD = 1024
D_XBC = 2048
HEADS = 16
HEAD_DIM = 64
GROUPS = 4
STATE = 128
CONV_K = 31
SSD_CONV_K = 4
D_FF = 5632
D_MAIN = 2 * D_CONV + D_SSD + D_XBC
D_IN = D_MAIN + HEADS
N_CHIPS = 4
LANES = 128
CHUNK = 128
PAIRS = HEADS // 2

ADAM_LR = 0.001
ADAM_B1 = 0.9
ADAM_B2 = 0.999
ADAM_EPS = 1e-08
ADAM_WD = 0.01
ADAM_STEP = 10

MESH_AXES = ("x", "y", "c")
VMEM_LIMIT = 56 * 1024 * 1024


def _sig(v):
    return 1.0 / (1.0 + jnp.exp(-v))


def _cparams(sem, vmem=VMEM_LIMIT):
    return pltpu.CompilerParams(dimension_semantics=sem, vmem_limit_bytes=vmem)


_DIMS = {"nn": ((1,), (0,)), "nt": ((1,), (1,)), "tn": ((0,), (0,))}


def _matmul(pairs, *, mode, out_dtype, tm, tn, tk, name, slot_out=False, extent=None, out_rows=None, dep=None):
    a0, b0 = pairs[0]
    if mode == "nn":
        (m, k), n = a0.shape, b0.shape[1]
    elif mode == "nt":
        (m, k), n = a0.shape, b0.shape[0]
    else:
        (k, m), n = a0.shape, b0.shape[1]
    if extent is not None:
        m, n, k = extent
    tm, tn, tk = min(tm, m), min(tn, n), min(tk, k)
    assert m % tm == 0 and n % tn == 0 and k % tk == 0, (name, m, n, k, tm, tn, tk)
    nk = k // tk
    npairs = len(pairs)
    deps = [] if dep is None else [dep]
    dims = (_DIMS[mode], ((), ()))

    use_scratch = nk > 1 and out_dtype != F32

    def body(*refs):
        ins, o_ref = refs[: 2 * npairs], refs[2 * npairs + len(deps)]
        dot = lambda p: lax.dot_general(ins[2 * p][...], ins[2 * p + 1][...], dims, preferred_element_type=F32)
        if nk == 1:
            part = dot(0)
            for p in range(1, npairs):
                part = part + dot(p)
            o_ref[...] = part.astype(out_dtype)
            return
        acc = refs[-1] if use_scratch else o_ref
        kk = pl.program_id(2)

        @pl.when(kk == 0)
        def _():
            acc[...] = jnp.zeros_like(acc)

        for p in range(npairs):
            acc[...] += dot(p)

        if use_scratch:
            @pl.when(kk == nk - 1)
            def _():
                o_ref[...] = acc[...].astype(out_dtype)

    if mode == "nn":
        a_spec = pl.BlockSpec((tm, tk), lambda i, j, kk: (i, kk))
        b_spec = pl.BlockSpec((tk, tn), lambda i, j, kk: (kk, j))
    elif mode == "nt":
        a_spec = pl.BlockSpec((tm, tk), lambda i, j, kk: (i, kk))
        b_spec = pl.BlockSpec((tn, tk), lambda i, j, kk: (j, kk))
    else:
        a_spec = pl.BlockSpec((tk, tm), lambda i, j, kk: (kk, i))
        b_spec = pl.BlockSpec((tk, tn), lambda i, j, kk: (kk, j))
    if slot_out:
        out_shape = jax.ShapeDtypeStruct((n // tn, m, tn), out_dtype)
        out_spec = pl.BlockSpec((None, tm, tn), lambda i, j, kk: (j, i, 0))
    else:
        out_shape = jax.ShapeDtypeStruct((m if out_rows is None else out_rows, n), out_dtype)
        out_spec = pl.BlockSpec((tm, tn), lambda i, j, kk: (i, j))
    flat = [t for ab in pairs for t in ab]
    return pl.pallas_call(
        body,
        grid=(m // tm, n // tn, nk),
        in_specs=[a_spec, b_spec] * npairs + [pl.BlockSpec(memory_space=pl.ANY)] * len(deps),
        out_specs=out_spec,
        out_shape=out_shape,
        scratch_shapes=[pltpu.VMEM((tm, tn), F32)] if use_scratch else [],
        compiler_params=_cparams(("parallel", "parallel", "arbitrary")),
        name=name,
    )(*flat, *deps)


SUB_ROWS = 256


def _ffn_up(h2, wg, wu, *, tm, tn):
    t, k = h2.shape
    n = wg.shape[1]
    tm = min(tm, t)
    assert t % tm == 0 and n % tn == 0, (t, n, tm, tn)

    sub = min(SUB_ROWS, tm)

    def body(h_ref, wg_ref, wu_ref, g_ref, u_ref, a_ref):
        for r in range(tm // sub):
            rows = pl.ds(r * sub, sub)
            hv = h_ref[rows, :]
            g = jnp.dot(hv, wg_ref[...], preferred_element_type=F32)
            u = jnp.dot(hv, wu_ref[...], preferred_element_type=F32)
            g_ref[rows, :] = g.astype(BF16)
            u_ref[rows, :] = u.astype(BF16)
            a_ref[rows, :] = (g * _sig(g) * u).astype(BF16)

    o = jax.ShapeDtypeStruct((t, n), BF16)
    ospec = pl.BlockSpec((tm, tn), lambda i, j: (i, j))
    return pl.pallas_call(
        body,
        grid=(t // tm, n // tn),
        in_specs=[pl.BlockSpec((tm, k), lambda i, j: (i, 0)), pl.BlockSpec((k, tn), lambda i, j: (0, j)),
                  pl.BlockSpec((k, tn), lambda i, j: (0, j))],
        out_specs=[ospec, ospec, ospec],
        out_shape=[o, o, o],
        compiler_params=_cparams(("parallel", "parallel")),
        name="ffn_up",
    )(h2, wg, wu)


def _ffn_bwd_act(df, wd, gt, up, *, tm, tn):
    t, k = df.shape
    n = wd.shape[0]
    tm = min(tm, t)
    assert t % tm == 0 and n % tn == 0, (t, n, tm, tn)

    sub = min(SUB_ROWS, tm)

    def body(df_ref, wd_ref, g_ref, u_ref, dg_ref, du_ref):
        for r in range(tm // sub):
            rows = pl.ds(r * sub, sub)
            da = lax.dot_general(df_ref[rows, :], wd_ref[...], (_DIMS["nt"], ((), ())), preferred_element_type=F32)
            g = g_ref[rows, :].astype(F32)
            u = u_ref[rows, :].astype(F32)
            s = _sig(g)
            dg_ref[rows, :] = (da * u * s * (1.0 + g * (1.0 - s))).astype(BF16)
            du_ref[rows, :] = (da * g * s).astype(BF16)

    o = jax.ShapeDtypeStruct((t, n), BF16)
    blk = pl.BlockSpec((tm, tn), lambda i, j: (i, j))
    return pl.pallas_call(
        body,
        grid=(t // tm, n // tn),
        in_specs=[pl.BlockSpec((tm, k), lambda i, j: (i, 0)), pl.BlockSpec((tn, k), lambda i, j: (j, 0)), blk, blk],
        out_specs=[blk, blk],
        out_shape=[o, o],
        compiler_params=_cparams(("parallel", "parallel")),
        name="ffn_bwd_act",
    )(df, wd, gt, up)


def _dw_dt_rows(dw_in_t, ddtr_b, h, *, tk=1024):
    t, d = h.shape
    tk = min(tk, t)
    nk = t // tk

    def body(buf_ref, d_ref, h_ref, o_ref, acc):
        kk = pl.program_id(0)

        @pl.when(kk == 0)
        def _():
            acc[...] = jnp.zeros_like(acc)

        acc[...] += lax.dot_general(d_ref[...], h_ref[...], (_DIMS["tn"], ((), ())), preferred_element_type=F32)

        @pl.when(kk == nk - 1)
        def _():
            o_ref[...] = acc[0:HEADS, :]

    return pl.pallas_call(
        body,
        grid=(nk,),
        in_specs=[DEP_SPEC, pl.BlockSpec((tk, LANES), lambda kk: (kk, 0)), pl.BlockSpec((tk, d), lambda kk: (kk, 0))],
        out_specs=pl.BlockSpec((HEADS, d), lambda kk: (D_MAIN // HEADS, 0)),
        out_shape=jax.ShapeDtypeStruct(dw_in_t.shape, F32),
        input_output_aliases={0: 0},
        scratch_shapes=[pltpu.VMEM((LANES, d), F32)],
        compiler_params=_cparams(("arbitrary",)),
        name="mm_dw_dt",
    )(dw_in_t, ddtr_b, h)


ROW_TILE = 256


DEP_SPEC = pl.BlockSpec(memory_space=pl.ANY)


def _rms_fwd(xv, g, *, res=None, dep=None, out_dtype, name):
    t, d = xv.shape
    has_res = res is not None
    deps = [] if dep is None else [dep]

    def body(*refs):
        x_ref, g_ref = refs[0], refs[1]
        o_ref = refs[-1]
        v = x_ref[...]
        r = lax.rsqrt(jnp.mean(v * v, axis=-1, keepdims=True) + EPS)
        y = v * r * g_ref[...]
        if has_res:
            y = refs[2][...] + y
        o_ref[...] = y.astype(out_dtype)

    row = pl.BlockSpec((ROW_TILE, d), lambda i: (i, 0))
    vec = pl.BlockSpec((1, d), lambda i: (0, 0))
    return pl.pallas_call(
        body,
        grid=(t // ROW_TILE,),
        in_specs=[row, vec] + ([row] if has_res else []) + [DEP_SPEC] * len(deps),
        out_specs=row,
        out_shape=jax.ShapeDtypeStruct((t, d), out_dtype),
        compiler_params=_cparams(("parallel",)),
        name=name,
    )(*([xv, g] + ([res] if has_res else []) + deps))


def _rms_bwd(dys, xv, g, *, addend=None, dep=None, out_dtype, name):
    t, d = xv.shape
    ndy = len(dys)
    has_add = addend is not None
    deps = [] if dep is None else [dep]

    def body(*refs):
        dy_refs = refs[:ndy]
        x_ref, g_ref = refs[ndy], refs[ndy + 1]
        dx_ref, dg_ref = refs[-2], refs[-1]
        dy = dy_refs[0][...].astype(F32)
        for rr in dy_refs[1:]:
            dy = dy + rr[...].astype(F32)
        v = x_ref[...]
        r = lax.rsqrt(jnp.mean(v * v, axis=-1, keepdims=True) + EPS)
        xh = v * r
        gdy = dy * g_ref[...]
        dx = r * (gdy - xh * jnp.mean(gdy * xh, axis=-1, keepdims=True))
        if has_add:
            dx = dx + refs[ndy + 2][...]
        dx_ref[...] = dx.astype(out_dtype)

        @pl.when(pl.program_id(0) == 0)
        def _():
            dg_ref[...] = jnp.zeros_like(dg_ref)

        dg_ref[...] += jnp.sum(dy * xh, axis=0, keepdims=True)

    row = pl.BlockSpec((ROW_TILE, d), lambda i: (i, 0))
    vec = pl.BlockSpec((1, d), lambda i: (0, 0))
    return pl.pallas_call(
        body,
        grid=(t // ROW_TILE,),
        in_specs=[row] * ndy + [row, vec] + ([row] if has_add else []) + [DEP_SPEC] * len(deps),
        out_specs=[row, vec],
        out_shape=[jax.ShapeDtypeStruct((t, d), out_dtype), jax.ShapeDtypeStruct((1, d), F32)],
        compiler_params=_cparams(("arbitrary",)),
        name=name,
    )(*(list(dys) + [xv, g] + ([addend] if has_add else []) + deps))


def _ffn_post_loss(f, x1, tgt, g):
    t, d = f.shape

    def body(f_ref, x1_ref, t_ref, g_ref, dx2_ref, df_ref, loss_ref, dg_ref):
        v = f_ref[...]
        gv = g_ref[...]
        r = lax.rsqrt(jnp.mean(v * v, axis=-1, keepdims=True) + EPS)
        fh = v * r
        e = x1_ref[...] + fh * gv - t_ref[...]
        dx2 = e * (1.0 / d)
        dx2_ref[...] = dx2
        gdy = dx2 * gv
        df_ref[...] = (r * (gdy - fh * jnp.mean(gdy * fh, axis=-1, keepdims=True))).astype(BF16)

        @pl.when(pl.program_id(0) == 0)
        def _():
            dg_ref[...] = jnp.zeros_like(dg_ref)
            loss_ref[...] = jnp.zeros_like(loss_ref)

        dg_ref[...] += jnp.sum(dx2 * fh, axis=0, keepdims=True)
        per_tok = jnp.mean(e * e, axis=-1, keepdims=True)
        loss_ref[...] += 0.5 * jnp.sum(per_tok, axis=0, keepdims=True)

    row = pl.BlockSpec((ROW_TILE, d), lambda i: (i, 0))
    vec = pl.BlockSpec((1, d), lambda i: (0, 0))
    return pl.pallas_call(
        body,
        grid=(t // ROW_TILE,),
        in_specs=[row, row, row, vec],
        out_specs=[row, row, pl.BlockSpec((1, 1), lambda i: (0, 0)), vec],
        out_shape=[jax.ShapeDtypeStruct((t, d), F32), jax.ShapeDtypeStruct((t, d), BF16),
                   jax.ShapeDtypeStruct((1, 1), F32), jax.ShapeDtypeStruct((1, d), F32)],
        compiler_params=_cparams(("arbitrary",)),
        name="ffn_post_loss",
    )(f, x1, tgt, g)


CONV_ROWS = 256
TAP_ROWS = 64
HALO31 = 32
HALO4 = 8


def _sum8(v):
    return jnp.sum(v.reshape(v.shape[0] // 8, 8, v.shape[1]), axis=0)


def _conv_branch_fwd(proj, cw, cb, lg, lb, *, nb, seq):
    ts, c, halo = CONV_ROWS, D_CONV, HALO31
    ns = seq // ts
    base = halo - CONV_K + 1

    def body(ca_ref, cg_ref, w_ref, b_ref, lg_ref, lb_ref, u1_ref, u_ref, ubuf):
        i = pl.program_id(1)

        @pl.when(i == 0)
        def _():
            ubuf[0:halo, :] = jnp.zeros((halo, c), F32)

        @pl.when(i > 0)
        def _():
            ubuf[0:halo, :] = ubuf[ts:ts + halo, :]

        ubuf[halo:halo + ts, :] = ca_ref[...] * _sig(cg_ref[...])

        def lane_tile(j, carry):
            ln = pl.ds(pl.multiple_of(j * LANES, LANES), LANES)
            for r in range(ts // TAP_ROWS):
                acc = jnp.broadcast_to(b_ref[:, ln], (TAP_ROWS, LANES))
                for k in range(CONV_K):
                    acc = acc + w_ref[pl.ds(k, 1), ln] * ubuf[pl.ds(r * TAP_ROWS + base + k, TAP_ROWS), ln]
                u1_ref[pl.ds(r * TAP_ROWS, TAP_ROWS), ln] = acc
            return carry

        lax.fori_loop(0, c // LANES, lane_tile, 0)
        v = u1_ref[...]
        mu = jnp.mean(v, axis=-1, keepdims=True)
        dv = v - mu
        xh = dv * lax.rsqrt(jnp.mean(dv * dv, axis=-1, keepdims=True) + EPS)
        u2 = xh * lg_ref[...] + lb_ref[...]
        u_ref[...] = (u2 * _sig(u2)).astype(BF16)

    t = nb * seq
    row = lambda col: pl.BlockSpec((ts, c), lambda b, i: (b * ns + i, col))
    vec = pl.BlockSpec((1, c), lambda b, i: (0, 0))
    return pl.pallas_call(
        body,
        grid=(nb, ns),
        in_specs=[row(0), row(1), pl.BlockSpec((32, c), lambda b, i: (0, 0)), vec, vec, vec],
        out_specs=[row(0), row(0)],
        out_shape=[jax.ShapeDtypeStruct((t, c), F32), jax.ShapeDtypeStruct((t, c + D_SSD), BF16)],
        scratch_shapes=[pltpu.VMEM((halo + ts, c), F32)],
        compiler_params=_cparams(("parallel", "arbitrary")),
        name="conv_branch_fwd",
    )(proj, proj, cw, cb, lg, lb)


def _conv_branch_bwd(duy, u1, proj, cw, lg, lb, *, nb, seq):
    ts, c, halo = CONV_ROWS, D_CONV, HALO31
    ns = seq // ts
    base = halo - CONV_K + 1
    hb = ts // halo

    def body(du_ref, u1_ref, ca_ref, cg_ref, cah_ref, cgh_ref, w_ref, lg_ref, lb_ref,
             dcacg_ref, dw_ref, db_ref, dlg_ref, dlb_ref,
             ubuf, dbuf, du0buf, dwacc, dbacc, dlgacc, dlbacc):
        b, i = pl.program_id(0), pl.program_id(1)
        rc = ns - 1 - i

        @pl.when(jnp.logical_and(b == 0, i == 0))
        def _():
            dwacc[...] = jnp.zeros_like(dwacc)
            dbacc[...] = jnp.zeros_like(dbacc)
            dlgacc[...] = jnp.zeros_like(dlgacc)
            dlbacc[...] = jnp.zeros_like(dlbacc)

        @pl.when(i == 0)
        def _():
            dbuf[ts:ts + halo, :] = jnp.zeros((halo, c), F32)

        @pl.when(i > 0)
        def _():
            dbuf[ts:ts + halo, :] = dbuf[0:halo, :]

        v = u1_ref[...]
        mu = jnp.mean(v, axis=-1, keepdims=True)
        dv = v - mu
        rstd = lax.rsqrt(jnp.mean(dv * dv, axis=-1, keepdims=True) + EPS)
        xh = dv * rstd
        lgv = lg_ref[...]
        u2 = xh * lgv + lb_ref[...]
        s2 = _sig(u2)
        du2 = du_ref[...] * (s2 * (1.0 + u2 * (1.0 - s2)))
        dlgacc[...] += jnp.sum(du2 * xh, axis=0, keepdims=True)
        dlbacc[...] += jnp.sum(du2, axis=0, keepdims=True)
        gd = du2 * lgv
        du1 = rstd * (gd - jnp.mean(gd, axis=-1, keepdims=True) - xh * jnp.mean(gd * xh, axis=-1, keepdims=True))
        dbacc[...] += jnp.sum(du1, axis=0, keepdims=True)
        dbuf[0:ts, :] = du1

        @pl.when(rc == 0)
        def _():
            ubuf[0:halo, :] = jnp.zeros((halo, c), F32)

        @pl.when(rc > 0)
        def _():
            ubuf[0:halo, :] = cah_ref[...] * _sig(cgh_ref[...])

        cav = ca_ref[...]
        sg = _sig(cg_ref[...])
        ubuf[halo:halo + ts, :] = cav * sg

        def lane_tile(j, carry):
            ln = pl.ds(pl.multiple_of(j * LANES, LANES), LANES)
            for r in range(ts // TAP_ROWS):
                r0 = r * TAP_ROWS
                d1 = dbuf[pl.ds(r0, TAP_ROWS), ln]
                acc = jnp.zeros((TAP_ROWS, LANES), F32)
                for k in range(CONV_K):
                    acc = acc + w_ref[pl.ds(k, 1), ln] * dbuf[pl.ds(r0 + CONV_K - 1 - k, TAP_ROWS), ln]
                    dwacc[pl.ds(k * 8, 8), ln] += _sum8(d1 * ubuf[pl.ds(r0 + base + k, TAP_ROWS), ln])
                du0buf[pl.ds(r0, TAP_ROWS), ln] = acc
            return carry

        lax.fori_loop(0, c // LANES, lane_tile, 0)
        du0 = du0buf[...]
        dcacg_ref[:, 0:c] = (du0 * sg).astype(BF16)
        dcacg_ref[:, c:2 * c] = (du0 * cav * sg * (1.0 - sg)).astype(BF16)

        @pl.when(jnp.logical_and(b == nb - 1, i == ns - 1))
        def _():
            for k in range(CONV_K):
                dw_ref[pl.ds(k, 1), :] = jnp.sum(dwacc[pl.ds(k * 8, 8), :], axis=0, keepdims=True)
            dw_ref[pl.ds(CONV_K, 1), :] = jnp.zeros((1, c), F32)
            db_ref[...] = dbacc[...]
            dlg_ref[...] = dlgacc[...]
            dlb_ref[...] = dlbacc[...]

    t = nb * seq
    rowblk = lambda b, i: b * ns + (ns - 1 - i)
    row = lambda col: pl.BlockSpec((ts, c), lambda b, i: (rowblk(b, i), col))
    hrow = lambda col: pl.BlockSpec((halo, c), lambda b, i: (jnp.maximum(rowblk(b, i) * hb - 1, 0), col))
    vec = pl.BlockSpec((1, c), lambda b, i: (0, 0))
    wspec = pl.BlockSpec((32, c), lambda b, i: (0, 0))
    return pl.pallas_call(
        body,
        grid=(nb, ns),
        in_specs=[row(0), row(0), row(0), row(1), hrow(0), hrow(1), wspec, vec, vec],
        out_specs=[pl.BlockSpec((ts, 2 * c), lambda b, i: (rowblk(b, i), 0)), wspec, vec, vec, vec],
        out_shape=[jax.ShapeDtypeStruct((t, D_MAIN), BF16), jax.ShapeDtypeStruct((32, c), F32),
                   jax.ShapeDtypeStruct((1, c), F32), jax.ShapeDtypeStruct((1, c), F32), jax.ShapeDtypeStruct((1, c), F32)],
        scratch_shapes=[pltpu.VMEM((halo + ts, c), F32), pltpu.VMEM((ts + halo, c), F32), pltpu.VMEM((ts, c), F32),
                        pltpu.VMEM((CONV_K * 8, c), F32), pltpu.VMEM((1, c), F32), pltpu.VMEM((1, c), F32),
                        pltpu.VMEM((1, c), F32)],
        compiler_params=_cparams(("arbitrary", "arbitrary")),
        name="conv_branch_bwd",
    )(duy, u1, proj, proj, proj, proj, cw, lg, lb)


XBC_COL0 = (2 * D_CONV + D_SSD) // 1024


def _ssd_pre_fwd(proj, sw, sb, *, nb, seq):
    ts, c, halo = CONV_ROWS, 1024, HALO4
    ns = seq // ts
    base = halo - SSD_CONV_K + 1

    def body(x_ref, w_ref, b_ref, o_ref, xbuf):
        i = pl.program_id(2)

        @pl.when(i == 0)
        def _():
            xbuf[0:halo, :] = jnp.zeros((halo, c), F32)

        @pl.when(i > 0)
        def _():
            xbuf[0:halo, :] = xbuf[ts:ts + halo, :]

        xbuf[halo:halo + ts, :] = x_ref[...]

        def lane_tile(j, carry):
            ln = pl.ds(pl.multiple_of(j * LANES, LANES), LANES)
            for r in range(ts // TAP_ROWS):
                acc = jnp.broadcast_to(b_ref[:, ln], (TAP_ROWS, LANES))
                for k in range(SSD_CONV_K):
                    acc = acc + w_ref[pl.ds(k, 1), ln] * xbuf[pl.ds(r * TAP_ROWS + base + k, TAP_ROWS), ln]
                o_ref[pl.ds(r * TAP_ROWS, TAP_ROWS), ln] = acc * _sig(acc)
            return carry

        lax.fori_loop(0, c // LANES, lane_tile, 0)

    t = nb * seq
    return pl.pallas_call(
        body,
        grid=(2, nb, ns),
        in_specs=[pl.BlockSpec((ts, c), lambda j, b, i: (b * ns + i, XBC_COL0 + j)),
                  pl.BlockSpec((8, c), lambda j, b, i: (0, j)), pl.BlockSpec((1, c), lambda j, b, i: (0, j))],
        out_specs=pl.BlockSpec((ts, c), lambda j, b, i: (b * ns + i, j)),
        out_shape=jax.ShapeDtypeStruct((t, D_XBC), F32),
        scratch_shapes=[pltpu.VMEM((halo + ts, c), F32)],
        compiler_params=_cparams(("parallel", "parallel", "arbitrary")),
        name="ssd_pre_fwd",
    )(proj, sw, sb)


def _ssd_pre_bwd(dproj, dxs, proj, sw, sb, *, nb, seq):
    ts, c, halo = CONV_ROWS, 1024, HALO4
    ns = seq // ts
    base = halo - SSD_CONV_K + 1
    hb = ts // halo

    def body(dproj_ref, d_ref, x_ref, xh_ref, w_ref, b_ref, dx_ref, dw_ref, db_ref, xbuf, dbuf, dwacc, dbacc):
        b, i = pl.program_id(1), pl.program_id(2)
        rc = ns - 1 - i

        @pl.when(jnp.logical_and(b == 0, i == 0))
        def _():
            dwacc[...] = jnp.zeros_like(dwacc)
            dbacc[...] = jnp.zeros_like(dbacc)

        @pl.when(i == 0)
        def _():
            dbuf[ts:ts + halo, :] = jnp.zeros((halo, c), F32)

        @pl.when(i > 0)
        def _():
            dbuf[ts:ts + halo, :] = dbuf[0:halo, :]

        @pl.when(rc == 0)
        def _():
            xbuf[0:halo, :] = jnp.zeros((halo, c), F32)

        @pl.when(rc > 0)
        def _():
            xbuf[0:halo, :] = xh_ref[...]

        xbuf[halo:halo + ts, :] = x_ref[...]

        def pre_tile(j, carry):
            ln = pl.ds(pl.multiple_of(j * LANES, LANES), LANES)
            for r in range(ts // TAP_ROWS):
                r0 = r * TAP_ROWS
                acc = jnp.broadcast_to(b_ref[:, ln], (TAP_ROWS, LANES))
                for k in range(SSD_CONV_K):
                    acc = acc + w_ref[pl.ds(k, 1), ln] * xbuf[pl.ds(r0 + base + k, TAP_ROWS), ln]
                s = _sig(acc)
                dc = d_ref[pl.ds(r0, TAP_ROWS), ln] * (s * (1.0 + acc * (1.0 - s)))
                dbuf[pl.ds(r0, TAP_ROWS), ln] = dc
                dbacc[:, ln] += _sum8(dc)
            return carry

        lax.fori_loop(0, c // LANES, pre_tile, 0)

        def lane_tile(j, carry):
            ln = pl.ds(pl.multiple_of(j * LANES, LANES), LANES)
            for r in range(ts // TAP_ROWS):
                r0 = r * TAP_ROWS
                d1 = dbuf[pl.ds(r0, TAP_ROWS), ln]
                acc = jnp.zeros((TAP_ROWS, LANES), F32)
                for k in range(SSD_CONV_K):
                    acc = acc + w_ref[pl.ds(k, 1), ln] * dbuf[pl.ds(r0 + SSD_CONV_K - 1 - k, TAP_ROWS), ln]
                    dwacc[pl.ds(k * 8, 8), ln] += _sum8(d1 * xbuf[pl.ds(r0 + base + k, TAP_ROWS), ln])
                dx_ref[pl.ds(r0, TAP_ROWS), ln] = acc.astype(BF16)
            return carry

        lax.fori_loop(0, c // LANES, lane_tile, 0)

        @pl.when(jnp.logical_and(b == nb - 1, i == ns - 1))
        def _():
            for k in range(SSD_CONV_K):
                dw_ref[pl.ds(k, 1), :] = jnp.sum(dwacc[pl.ds(k * 8, 8), :], axis=0, keepdims=True)
            dw_ref[pl.ds(SSD_CONV_K, 8 - SSD_CONV_K), :] = jnp.zeros((8 - SSD_CONV_K, c), F32)
            db_ref[...] = jnp.sum(dbacc[...], axis=0, keepdims=True)

    t = nb * seq
    rowblk = lambda b, i: b * ns + (ns - 1 - i)
    return pl.pallas_call(
        body,
        grid=(2, nb, ns),
        in_specs=[DEP_SPEC, pl.BlockSpec((ts, c), lambda j, b, i: (rowblk(b, i), j)),
                  pl.BlockSpec((ts, c), lambda j, b, i: (rowblk(b, i), XBC_COL0 + j)),
                  pl.BlockSpec((halo, c), lambda j, b, i: (jnp.maximum(rowblk(b, i) * hb - 1, 0), XBC_COL0 + j)),
                  pl.BlockSpec((8, c), lambda j, b, i: (0, j)), pl.BlockSpec((1, c), lambda j, b, i: (0, j))],
        out_specs=[pl.BlockSpec((ts, c), lambda j, b, i: (rowblk(b, i), XBC_COL0 + j)),
                   pl.BlockSpec((8, c), lambda j, b, i: (0, j)), pl.BlockSpec((1, c), lambda j, b, i: (0, j))],
        out_shape=[jax.ShapeDtypeStruct(dproj.shape, BF16), jax.ShapeDtypeStruct((8, D_XBC), F32),
                   jax.ShapeDtypeStruct((1, D_XBC), F32)],
        input_output_aliases={0: 0},
        scratch_shapes=[pltpu.VMEM((halo + ts, c), F32), pltpu.VMEM((ts + halo, c), F32),
                        pltpu.VMEM((SSD_CONV_K * 8, c), F32), pltpu.VMEM((8, c), F32)],
        compiler_params=_cparams(("arbitrary", "arbitrary", "arbitrary")),
        name="ssd_pre_bwd",
    )(dproj, dxs, proj, proj, sw, sb)


Z_COL = (2 * D_CONV) // 1024
GROUP_W = D_SSD // GROUPS


def _softplus(v):
    return jnp.maximum(v, 0.0) + jnp.log(1.0 + jnp.exp(-jnp.abs(v)))


def _dot(a, b):
    return jnp.dot(a, b, preferred_element_type=F32)


def _dot_nt(a, b):
    return lax.dot_general(a, b, (_DIMS["nt"], ((), ())), preferred_element_type=F32)


def _dot_tn(a, b):
    return lax.dot_general(a, b, (_DIMS["tn"], ((), ())), preferred_element_type=F32)


def _dot_exact(a, b):
    return jnp.dot(a, b, precision=lax.Precision.HIGHEST, preferred_element_type=F32)


def _chunk_decays(dtr_ref, bias_ref, alog_ref):
    q = CHUNK
    ii = lax.broadcasted_iota(jnp.int32, (q, q), 0)
    jj = lax.broadcasted_iota(jnp.int32, (q, q), 1)
    tri = jj <= ii
    dt = _softplus(dtr_ref[...] + bias_ref[...])
    a_head = -jnp.exp(alog_ref[...])
    cs = _dot_exact(tri.astype(F32), dt * a_head)
    return tri, dt, a_head, cs, cs.T


def _ssd_fwd(uy, xs_all, proj, dtr, dt_bias, a_log, d_lanes, norm_w, *, nb, seq):
    q = CHUNK
    nc = seq // q
    t = nb * seq

    def body(uy_ref, xs_ref, bm_ref, cm_ref, z_ref, dtr_ref, bias_ref, alog_ref, dl_ref, nw_ref,
             y_ref, ys_ref, st_ref, state):
        @pl.when(pl.program_id(1) == 0)
        def _():
            state[...] = jnp.zeros_like(state)

        tri, dt, _, cs, cst = _chunk_decays(dtr_ref, bias_ref, alog_ref)
        first = lax.broadcasted_iota(jnp.int32, (1, LANES), 1) < HEAD_DIM
        for g in range(GROUPS):
            gl = slice(g * STATE, (g + 1) * STATE)
            bb = bm_ref[:, gl].astype(BF16)
            cb = cm_ref[:, gl].astype(BF16)
            scores = _dot_nt(cb, bb)
            for p in range(2):
                pr = 2 * g + p
                h0 = 2 * pr
                sl = slice(pr * LANES, (pr + 1) * LANES)
                xv = xs_ref[:, sl]
                dtp = jnp.where(first, dt[:, h0:h0 + 1], dt[:, h0 + 1:h0 + 2])
                csp = jnp.where(first, cs[:, h0:h0 + 1], cs[:, h0 + 1:h0 + 2])
                xd = xv * dtp
                yv = None
                for hh, keep in ((h0, first), (h0 + 1, jnp.logical_not(first))):
                    decay = jnp.where(tri, jnp.exp(cs[:, hh:hh + 1] - cst[hh:hh + 1, :]), 0.0)
                    part = _dot((scores * decay).astype(BF16), jnp.where(keep, xd, 0.0).astype(BF16))
                    yv = part if yv is None else yv + part
                hp = state[pr]
                st_ref[0, pr] = hp
                yv = yv + jnp.exp(csp) * _dot(cb, hp.astype(BF16))
                last = csp[q - 1:q, :]
                state[pr] = jnp.exp(last) * hp + _dot_tn(bb, (xd * jnp.exp(last - csp)).astype(BF16))
                ys_ref[:, sl] = yv + dl_ref[:, sl] * xv
        zv = z_ref[...]
        gated = ys_ref[...] * (zv * _sig(zv))
        for g in range(GROUPS):
            gl = slice(g * GROUP_W, (g + 1) * GROUP_W)
            v = gated[:, gl]
            r = lax.rsqrt(jnp.mean(v * v, axis=-1, keepdims=True) + EPS)
            y_ref[:, gl] = (v * r * nw_ref[:, gl]).astype(BF16)

    blk = lambda w, col: pl.BlockSpec((q, w), lambda b, c: (b * nc + c, col))
    vec = lambda w: pl.BlockSpec((1, w), lambda b, c: (0, 0))
    return pl.pallas_call(
        body,
        grid=(nb, nc),
        in_specs=[DEP_SPEC, blk(D_SSD, 0), blk(GROUPS * STATE, 2), blk(GROUPS * STATE, 3), blk(D_SSD, Z_COL),
                  blk(LANES, 0), vec(LANES), vec(LANES), vec(D_SSD), vec(D_SSD)],
        out_specs=[blk(D_SSD, 1), blk(D_SSD, 0),
                   pl.BlockSpec((1, PAIRS, STATE, LANES), lambda b, c: (b * nc + c, 0, 0, 0))],
        out_shape=[jax.ShapeDtypeStruct(uy.shape, BF16), jax.ShapeDtypeStruct((t, D_SSD), F32),
                   jax.ShapeDtypeStruct((nb * nc, PAIRS, STATE, LANES), F32)],
        input_output_aliases={0: 0},
        scratch_shapes=[pltpu.VMEM((PAIRS, STATE, LANES), F32)],
        compiler_params=_cparams(("parallel", "arbitrary")),
        name="ssd_fwd",
    )(uy, xs_all, xs_all, xs_all, proj, dtr, dt_bias, a_log, d_lanes, norm_w)


def _ssd_bwd(dproj, duy, proj, ys, xs_all, dtr, states, dt_bias, a_log, d_lanes, norm_w, *, nb, seq):
    q = CHUNK
    nc = seq // q
    t = nb * seq

    def body(dproj_ref, dy_ref, z_ref, ys_ref, xs_ref, bm_ref, cm_ref, dtr_ref, st_ref, bias_ref, alog_ref, dl_ref,
             nw_ref, dz_ref, dx_ref, ddtr_ref, small_ref,
             dstate, dys_buf, dcsl, ddtl, dcst, dnw_acc, dd_acc, dbias_acc, da_acc):
        b, c = pl.program_id(0), pl.program_id(1)

        @pl.when(jnp.logical_and(b == 0, c == 0))
        def _():
            dnw_acc[...] = jnp.zeros_like(dnw_acc)
            dd_acc[...] = jnp.zeros_like(dd_acc)
            dbias_acc[...] = jnp.zeros_like(dbias_acc)
            da_acc[...] = jnp.zeros_like(da_acc)
            dcst[...] = jnp.zeros_like(dcst)

        @pl.when(c == 0)
        def _():
            dstate[...] = jnp.zeros_like(dstate)

        zv = z_ref[...]
        sz = _sig(zv)
        silz = zv * sz
        ysv = ys_ref[...]
        gated = ysv * silz
        dyv = dy_ref[...]
        nwv = nw_ref[...]
        for g in range(GROUPS):
            gl = slice(g * GROUP_W, (g + 1) * GROUP_W)
            v = gated[:, gl]
            r = lax.rsqrt(jnp.mean(v * v, axis=-1, keepdims=True) + EPS)
            yn = v * r
            dyn = dyv[:, gl] * nwv[:, gl]
            dnw_acc[:, gl] += jnp.sum(dyv[:, gl] * yn, axis=0, keepdims=True)
            dys_buf[:, gl] = r * (dyn - yn * jnp.mean(dyn * yn, axis=-1, keepdims=True))
        dgated = dys_buf[...]
        dz_ref[...] = (dgated * ysv * (sz * (1.0 + zv * (1.0 - sz)))).astype(BF16)
        dys_all = dgated * silz
        dys_buf[...] = dys_all
        dd_acc[...] += jnp.sum(dys_all * xs_ref[...], axis=0, keepdims=True)

        tri, dt, a_head, cs, cst = _chunk_decays(dtr_ref, bias_ref, alog_ref)
        lane = lax.broadcasted_iota(jnp.int32, (1, LANES), 1)
        first = lane < HEAD_DIM
        dcs_h = jnp.zeros((q, LANES), F32)
        for g in range(GROUPS):
            gl = slice(g * STATE, (g + 1) * STATE)
            bb = bm_ref[:, gl].astype(BF16)
            cb = cm_ref[:, gl].astype(BF16)
            scores = _dot_nt(cb, bb)
            dscores = jnp.zeros((q, q), F32)
            dbg = jnp.zeros((q, STATE), F32)
            dcg = jnp.zeros((q, STATE), F32)
            for p in range(2):
                pr = 2 * g + p
                h0 = 2 * pr
                sl = slice(pr * LANES, (pr + 1) * LANES)
                xv = xs_ref[:, sl]
                dyp = dys_buf[:, sl]
                dtp = jnp.where(first, dt[:, h0:h0 + 1], dt[:, h0 + 1:h0 + 2])
                csp = jnp.where(first, cs[:, h0:h0 + 1], cs[:, h0 + 1:h0 + 2])
                xd = xv * dtp
                xdb = xd.astype(BF16)
                hp = st_ref[0, pr]
                dhn = dstate[pr]
                hpb = hp.astype(BF16)
                dhnb = dhn.astype(BF16)
                lam = jnp.exp(csp)
                last = csp[q - 1:q, :]
                gam = jnp.exp(last)
                w = jnp.exp(last - csp)
                dxd = jnp.zeros((q, LANES), F32)
                for hh, keep in ((h0, first), (h0 + 1, jnp.logical_not(first))):
                    decay = jnp.where(tri, jnp.exp(cs[:, hh:hh + 1] - cst[hh:hh + 1, :]), 0.0)
                    m = scores * decay
                    dym = jnp.where(keep, dyp, 0.0).astype(BF16)
                    dm = _dot_nt(dym, xdb)
                    dxd = dxd + _dot_tn(m.astype(BF16), dym)
                    e = dm * m
                    dcs_h = dcs_h + jnp.where(lane == hh, jnp.sum(e, axis=1, keepdims=True), 0.0)
                    dcst[hh:hh + 1, :] = jnp.sum(e, axis=0, keepdims=True)
                    dscores = dscores + dm * decay
                yoff = lam * _dot(cb, hpb)
                ldy = (lam * dyp).astype(BF16)
                dcg = dcg + _dot_nt(ldy, hpb)
                dstate[pr] = gam * dhn + _dot_tn(cb, ldy)
                bdh = _dot(bb, dhnb)
                dxd = dxd + w * bdh
                xdw = xd * w
                dbg = dbg + _dot_nt(xdw.astype(BF16), dhnb)
                wd = xdw * bdh
                dcsl[:, sl] = dyp * yoff - wd
                dcsl[q - 1:q, sl] += (jnp.sum(wd, axis=0, keepdims=True)
                                      + gam * jnp.sum(dhn * hp, axis=0, keepdims=True))
                dx_ref[:, sl] = dxd * dtp + dyp * dl_ref[:, sl]
                ddtl[:, sl] = dxd * xv
            dsb = dscores.astype(BF16)
            dx_ref[:, D_SSD + g * STATE:D_SSD + (g + 1) * STATE] = dbg + _dot_tn(dsb, cb)
            dx_ref[:, D_SSD + (GROUPS + g) * STATE:D_SSD + (GROUPS + g + 1) * STATE] = dcg + _dot(dsb, bb)

        li = lax.broadcasted_iota(jnp.int32, (D_SSD, LANES), 0)
        hi = lax.broadcasted_iota(jnp.int32, (D_SSD, LANES), 1)
        sel = (li // HEAD_DIM == hi).astype(F32)
        dcs_h = dcs_h + _dot_exact(dcsl[...], sel) - dcst[...].T
        ddt = _dot_exact(ddtl[...], sel)
        upper = lax.broadcasted_iota(jnp.int32, (q, q), 1) >= lax.broadcasted_iota(jnp.int32, (q, q), 0)
        da = _dot_exact(upper.astype(F32), dcs_h)
        ddt = ddt + da * a_head
        da_acc[...] += jnp.sum(da * dt, axis=0, keepdims=True)
        ddtr = ddt * _sig(dtr_ref[...] + bias_ref[...])
        ddtr_ref[...] = ddtr
        dbias_acc[...] += jnp.sum(ddtr, axis=0, keepdims=True)

        @pl.when(jnp.logical_and(b == nb - 1, c == nc - 1))
        def _():
            small_ref[...] = jnp.zeros_like(small_ref)
            small_ref[0:1, :] = dnw_acc[...]
            small_ref[1:2, 0:LANES] = _dot_exact(jnp.broadcast_to(dd_acc[...], (8, D_SSD)), sel)[0:1, :]
            small_ref[2:3, 0:LANES] = dbias_acc[...]
            small_ref[3:4, 0:LANES] = da_acc[...] * a_head

    rowblk = lambda b, c: b * nc + (nc - 1 - c)
    blk = lambda w, col: pl.BlockSpec((q, w), lambda b, c: (rowblk(b, c), col))
    vec = lambda w: pl.BlockSpec((1, w), lambda b, c: (0, 0))
    return pl.pallas_call(
        body,
        grid=(nb, nc),
        in_specs=[DEP_SPEC, blk(D_SSD, 1), blk(D_SSD, Z_COL), blk(D_SSD, 0), blk(D_SSD, 0), blk(GROUPS * STATE, 2),
                  blk(GROUPS * STATE, 3), blk(LANES, 0),
                  pl.BlockSpec((1, PAIRS, STATE, LANES), lambda b, c: (rowblk(b, c), 0, 0, 0)),
                  vec(LANES), vec(LANES), vec(D_SSD), vec(D_SSD)],
        out_specs=[blk(D_SSD, Z_COL), blk(D_XBC, 0), blk(LANES, 0), pl.BlockSpec((8, D_SSD), lambda b, c: (0, 0))],
        out_shape=[jax.ShapeDtypeStruct(dproj.shape, BF16), jax.ShapeDtypeStruct((t, D_XBC), F32),
                   jax.ShapeDtypeStruct((t, LANES), F32), jax.ShapeDtypeStruct((8, D_SSD), F32)],
        input_output_aliases={0: 0},
        scratch_shapes=[pltpu.VMEM((PAIRS, STATE, LANES), F32), pltpu.VMEM((q, D_SSD), F32),
                        pltpu.VMEM((q, D_SSD), F32), pltpu.VMEM((q, D_SSD), F32), pltpu.VMEM((LANES, q), F32),
                        pltpu.VMEM((1, D_SSD), F32), pltpu.VMEM((1, D_SSD), F32), pltpu.VMEM((1, LANES), F32),
                        pltpu.VMEM((1, LANES), F32)],
        compiler_params=_cparams(("arbitrary", "arbitrary")),
        name="ssd_bwd",
    )(dproj, duy, proj, ys, xs_all, xs_all, xs_all, dtr, states, dt_bias, a_log, d_lanes, norm_w)


HBM_SPEC = pl.BlockSpec(memory_space=pltpu.HBM)
MESH_ID = pl.DeviceIdType.MESH


def _coords():
    return lax.axis_index("x"), lax.axis_index("y"), lax.axis_index("c")


def _chip_peer(xi, yi, ci, d):
    return (jnp.bitwise_xor(xi, d >> 1), jnp.bitwise_xor(yi, d & 1), ci)


def _remote(src, dst, send_sem, recv_sem, peer):
    return pltpu.make_async_remote_copy(src_ref=src, dst_ref=dst, send_sem=send_sem, recv_sem=recv_sem,
                                        device_id=peer, device_id_type=MESH_ID)


SEM_SPEC = pl.BlockSpec(memory_space=pltpu.SEMAPHORE)
ANY_SPEC = pl.BlockSpec(memory_space=pl.ANY)
EFFECT = pltpu.SideEffectType.DATAFLOW_SIDE_EFFECTING
COPIES = 3


def _half(ref, axis, which, lead=0):
    size = ref.shape[lead + axis] // 2
    part = pl.ds(which * size, size)
    idx = (slice(None),) * lead + ((part, slice(None)) if axis == 0 else (slice(None), part))
    return ref.at[idx]


def _halved_shape(shape, axis):
    lead = len(shape) - 2
    return tuple(d // 2 if i == lead + axis else d for i, d in enumerate(shape))


def _gather_plan(axis):
    def plan(xi, yi, ci, src, land):
        me = 2 * xi + yi
        out = []
        for d in (1, 2, 3):
            there = jnp.bitwise_xor(me, d)
            if axis is None:
                out.append((src, land.at[me], _chip_peer(xi, yi, ci, d), land.at[there]))
            else:
                out.append((_half(src, axis, ci), _half(land.at[me], axis, ci), _chip_peer(xi, yi, ci, d),
                            _half(land.at[there], axis, ci)))
        return out
    return plan


def _owners_plan(xi, yi, ci, src, land):
    me = 2 * xi + yi
    return [(src.at[jnp.bitwise_xor(me, d)], land.at[d - 1], _chip_peer(xi, yi, ci, d), land.at[d - 1])
            for d in (1, 2, 3)]


def _split_start(srcs, lands, plans, *, name):
    n = len(srcs)

    def body(*refs):
        src_refs, land_refs = refs[:n], refs[n:2 * n]
        ssems, rsems = refs[2 * n:3 * n], refs[3 * n:4 * n]
        token = refs[-1]
        xi, yi, ci = _coords()
        for t in range(n):
            for k, (src, dst, peer, _) in enumerate(plans[t](xi, yi, ci, src_refs[t], land_refs[t])):
                _remote(src, dst, ssems[t].at[k], rsems[t].at[k], peer).start()
        token[...] = jnp.zeros_like(token)

    bufs = list(srcs) + list(lands)
    outs = pl.pallas_call(
        body,
        name=name,
        in_specs=[HBM_SPEC] * (2 * n),
        out_specs=[SEM_SPEC] * (2 * n) + [HBM_SPEC] * (2 * n) + [pl.BlockSpec(memory_space=pltpu.VMEM)],
        out_shape=[pltpu.SemaphoreType.DMA((COPIES,))] * (2 * n) + [pltpu.HBM(a.shape, a.dtype) for a in bufs]
        + [jax.ShapeDtypeStruct((8, LANES), F32)],
        input_output_aliases={i: 2 * n + i for i in range(2 * n)},
        compiler_params=pltpu.CompilerParams(has_side_effects=EFFECT),
    )(*[pltpu.with_memory_space_constraint(a, pltpu.HBM) for a in bufs])
    return outs[:n], outs[n:2 * n], outs[2 * n:3 * n], outs[3 * n:4 * n], outs[-1]


def _split_wait(ssems, rsems, srcs, lands, plans, after, *, name):
    n = len(srcs)

    def body(*refs):
        src_refs, land_refs = refs[:n], refs[n:2 * n]
        ss, rs = refs[2 * n:3 * n], refs[3 * n:4 * n]
        xi, yi, ci = _coords()
        for t in range(n):
            for k, (src, _, peer, landed) in enumerate(plans[t](xi, yi, ci, src_refs[t], land_refs[t])):
                cp = _remote(src, landed, ss[t].at[k], rs[t].at[k], peer)
                cp.wait_send()
                cp.wait_recv()

    bufs = list(srcs) + list(lands)
    outs = pl.pallas_call(
        body,
        name=name,
        in_specs=[HBM_SPEC] * (2 * n) + [SEM_SPEC] * (2 * n) + [ANY_SPEC],
        out_specs=[HBM_SPEC] * (2 * n),
        out_shape=[pltpu.HBM(a.shape, a.dtype) for a in bufs],
        input_output_aliases={i: i for i in range(2 * n)},
        compiler_params=pltpu.CompilerParams(has_side_effects=EFFECT),
    )(*bufs, *ssems, *rsems, after)
    return outs[:n], outs[n:]


def _forward_halves(lands, axes, *, name):
    n = len(lands)

    def body(*refs):
        ins, outs = refs[:n], refs[n:2 * n]
        send_sems, recv_sems = refs[2 * n:]
        xi, yi, ci = _coords()
        me = 2 * xi + yi
        sibling = (xi, yi, 1 - ci)
        cps = []
        for t in range(n):
            for d in (1, 2, 3):
                slot = jnp.bitwise_xor(me, d)
                k = COPIES * t + d - 1
                cp = _remote(_half(ins[t].at[slot], axes[t], ci), _half(outs[t].at[slot], axes[t], ci),
                             send_sems.at[k], recv_sems.at[k], sibling)
                cp.start()
                cps.append(cp)
        for t in range(n):
            for d in (1, 2, 3):
                got = _half(outs[t].at[jnp.bitwise_xor(me, d)], axes[t], 1 - ci)
                k = COPIES * t + d - 1
                _remote(got, got, send_sems.at[k], recv_sems.at[k], sibling).wait_recv()
        for cp in cps:
            cp.wait_send()

    return pl.pallas_call(
        body,
        name=name,
        in_specs=[HBM_SPEC] * n,
        out_specs=[HBM_SPEC] * n,
        out_shape=[jax.ShapeDtypeStruct(a.shape, a.dtype) for a in lands],
        input_output_aliases={i: i for i in range(n)},
        scratch_shapes=[pltpu.SemaphoreType.DMA((COPIES * n,)), pltpu.SemaphoreType.DMA((COPIES * n,))],
    )(*lands)


def _swap_other_halves(gs, axes, *, name):
    n = len(gs)

    def body(*refs):
        ins, lands = refs[:n], refs[n:2 * n]
        send_sems, recv_sems = refs[2 * n:]
        xi, yi, ci = _coords()
        sibling = (xi, yi, 1 - ci)
        cps = []
        for t in range(n):
            cp = _remote(_half(ins[t], axes[t], 1 - ci, lead=1), lands[t], send_sems.at[t], recv_sems.at[t], sibling)
            cp.start()
            cps.append(cp)
        for cp in cps:
            cp.wait_recv()
        for cp in cps:
            cp.wait_send()

    return pl.pallas_call(
        body,
        in_specs=[HBM_SPEC] * n,
        out_specs=[HBM_SPEC] * n,
        out_shape=[jax.ShapeDtypeStruct(_halved_shape(g.shape, ax), g.dtype) for g, ax in zip(gs, axes)],
        scratch_shapes=[pltpu.SemaphoreType.DMA((n,)), pltpu.SemaphoreType.DMA((n,))],
        name=name,
    )(*gs)


def _row_tile(rows, cap=512, mult=16):
    best = mult
    for cand in range(mult, min(rows, cap) + 1, mult):
        if rows % cand == 0:
            best = cand
    assert rows % best == 0, rows
    return best


COL_TILE = 256


def _half_tiles(hr, hc, axis, cap=512, mult=16):
    if axis == 0:
        tr = _row_tile(hr, cap, mult)
        n = hr // tr
        return (tr, hc), n, lambda half, i: (half * n + i, 0)
    n = hc // COL_TILE
    return (hr, COL_TILE), n, lambda half, i: (0, half * n + i)


def _add_core_halves(g, land, where, axis):
    nslot, hr, hc = land.shape
    bshape, nr, idx = _half_tiles(hr, hc, axis)

    def body(where_ref, g_ref, l_ref, f_ref, b_ref):
        s = g_ref[...] + l_ref[...]
        f_ref[...] = s
        b_ref[...] = s.astype(BF16)

    blk = pl.BlockSpec((None,) + bshape, lambda s, i, w: (s,) + idx(0, i))
    mine = pl.BlockSpec((None,) + bshape, lambda s, i, w: (s,) + idx(w[0], i))
    return pl.pallas_call(
        body,
        grid_spec=pltpu.PrefetchScalarGridSpec(
            num_scalar_prefetch=1,
            grid=(nslot, nr),
            in_specs=[mine, blk],
            out_specs=[blk, blk],
        ),
        out_shape=[jax.ShapeDtypeStruct(land.shape, F32), jax.ShapeDtypeStruct(land.shape, BF16)],
        compiler_params=_cparams(("parallel", "parallel")),
        name="add_core_halves",
    )(where, g, land)


def _add_chip_sums(pf, land, where, axis):
    _, hr, cols = pf.shape
    bshape, nr, idx = _half_tiles(hr, cols, axis)

    def body(where_ref, p_ref, l_ref, o_ref):
        acc = p_ref[...]
        for d in range(3):
            acc = acc + l_ref[d].astype(F32)
        o_ref[...] = acc

    return pl.pallas_call(
        body,
        grid_spec=pltpu.PrefetchScalarGridSpec(
            num_scalar_prefetch=1,
            grid=(nr,),
            in_specs=[pl.BlockSpec((None,) + bshape, lambda i, w: (w[1],) + idx(0, i)),
                      pl.BlockSpec((3,) + bshape, lambda i, w: (0,) + idx(0, i))],
            out_specs=pl.BlockSpec(bshape, lambda i, w: idx(0, i)),
        ),
        out_shape=jax.ShapeDtypeStruct((hr, cols), F32),
        compiler_params=_cparams(("parallel",)),
        name="add_chip_sums",
    )(where, pf, land)


def _swap_reduced_halves(rs):
    n = len(rs)

    def body(*refs):
        ins, outs = refs[:n], refs[n:2 * n]
        send_sems, recv_sems = refs[2 * n:]
        xi, yi, ci = _coords()
        sibling = (xi, yi, 1 - ci)
        cps = [_remote(ins[t], outs[t], send_sems.at[t], recv_sems.at[t], sibling) for t in range(n)]
        for cp in cps:
            cp.start()
        for cp in cps:
            cp.wait_recv()
        for cp in cps:
            cp.wait_send()

    return pl.pallas_call(
        body,
        in_specs=[HBM_SPEC] * n,
        out_specs=[HBM_SPEC] * n,
        out_shape=[jax.ShapeDtypeStruct(r.shape, r.dtype) for r in rs],
        scratch_shapes=[pltpu.SemaphoreType.DMA((n,)), pltpu.SemaphoreType.DMA((n,))],
        name="swap_reduced_halves",
    )(*rs)


N_DEV = 8


def _all_reduce_small(part):
    r, w = part.shape

    def body(p_ref, o_ref, gath, send_sems, recv_sems):
        xi, yi, ci = _coords()
        me = 4 * xi + 2 * yi + ci
        gath[me] = p_ref[...]
        cps = []
        for d in range(1, N_DEV):
            peer = (jnp.bitwise_xor(xi, d >> 2), jnp.bitwise_xor(yi, (d >> 1) & 1), jnp.bitwise_xor(ci, d & 1))
            cp = _remote(p_ref, gath.at[me], send_sems.at[d - 1], recv_sems.at[d - 1], peer)
            cp.start()
            cps.append(cp)
        for d in range(1, N_DEV):
            src = gath.at[jnp.bitwise_xor(me, d)]
            _remote(src, src, send_sems.at[d - 1], recv_sems.at[d - 1], (xi, yi, ci)).wait_recv()
        acc = gath[0]
        for k in range(1, N_DEV):
            acc = acc + gath[k]
        o_ref[...] = acc
        for cp in cps:
            cp.wait_send()

    vm = pl.BlockSpec(memory_space=pltpu.VMEM)
    return pl.pallas_call(
        body,
        in_specs=[vm],
        out_specs=vm,
        out_shape=jax.ShapeDtypeStruct((r, w), F32),
        scratch_shapes=[pltpu.VMEM((N_DEV, r, w), F32), pltpu.SemaphoreType.DMA((N_DEV - 1,)),
                        pltpu.SemaphoreType.DMA((N_DEV - 1,))],
        name="all_reduce_small",
    )(part)


def _adamw_math(wv, gv, mv, vv):
    mn = ADAM_B1 * mv + (1.0 - ADAM_B1) * gv
    vn = ADAM_B2 * vv + (1.0 - ADAM_B2) * (gv * gv)
    m_hat = mn / (1.0 - ADAM_B1 ** ADAM_STEP)
    v_hat = vn / (1.0 - ADAM_B2 ** ADAM_STEP)
    return -ADAM_LR * (m_hat / (jnp.sqrt(v_hat) + ADAM_EPS) + ADAM_WD * wv), mn, vn


def _adamw_halves(w, g_mine, g_other, m, v, where, axis, *, name):
    rows, cols = w.shape
    hr, hc = g_mine.shape
    bshape, nr, idx = _half_tiles(hr, hc, axis, cap=256, mult=8)

    def body(where_ref, w_ref, gm_ref, go_ref, m_ref, v_ref, g_ref, d_ref, nm_ref, nv_ref):
        is_mine = pl.program_id(0) == where_ref[0]
        gv = jnp.where(is_mine, gm_ref[...], go_ref[...])
        g_ref[...] = gv
        d_ref[...], nm_ref[...], nv_ref[...] = _adamw_math(w_ref[...], gv, m_ref[...], v_ref[...])

    def parked(half, i, holder):
        return idx(0, jnp.where(half == holder, i, jnp.where(half < holder, 0, nr - 1)))

    blk = pl.BlockSpec(bshape, lambda hf, i, wh: idx(hf, i))
    o = jax.ShapeDtypeStruct((rows, cols), F32)
    return pl.pallas_call(
        body,
        grid_spec=pltpu.PrefetchScalarGridSpec(
            num_scalar_prefetch=1,
            grid=(2, nr),
            in_specs=[blk, pl.BlockSpec(bshape, lambda hf, i, wh: parked(hf, i, wh[0])),
                      pl.BlockSpec(bshape, lambda hf, i, wh: parked(hf, i, 1 - wh[0])), blk, blk],
            out_specs=[blk] * 4,
        ),
        out_shape=[o, o, o, o],
        compiler_params=_cparams(("arbitrary", "arbitrary")),
        name=name,
    )(where, w, g_mine, g_other, m, v)


def _adamw(w, g, m, v, *, name):
    rows, cols = w.shape
    tr = _row_tile(rows, cap=256, mult=8)

    def body(w_ref, g_ref, m_ref, v_ref, d_ref, nm_ref, nv_ref):
        d_ref[...], nm_ref[...], nv_ref[...] = _adamw_math(w_ref[...], g_ref[...], m_ref[...], v_ref[...])

    blk = pl.BlockSpec((tr, cols), lambda i: (i, 0))
    o = jax.ShapeDtypeStruct((rows, cols), F32)
    return pl.pallas_call(
        body,
        grid=(rows // tr,),
        in_specs=[blk] * 4,
        out_specs=[blk] * 3,
        out_shape=[o, o, o],
        compiler_params=_cparams(("parallel",)),
        name=name,
    )(w, g, m, v)


def _pack(arrs):
    flat = jnp.concatenate([a.reshape(-1) for a in arrs])
    pad = (-flat.shape[0]) % (8 * LANES)
    return jnp.pad(flat, (0, pad)).reshape(-1, LANES)


def _unpack(packed, shapes):
    flat = packed.reshape(-1)
    out, off = [], 0
    for s in shapes:
        n = 1
        for dim in s:
            n *= dim
        out.append(flat[off:off + n].reshape(s))
        off += n
    return out


def _pad_rows(a, rows):
    return jnp.pad(a, ((0, rows - a.shape[0]), (0, 0)))


def _pad_lanes(a, lanes=LANES):
    return jnp.pad(a, ((0, 0), (0, lanes - a.shape[1])))


def _local_grads(x2d, tgt2d, prm, get_w, on_grads, *, nb, seq, dep=None):
    g_pre, g_post, g_fpre, g_fpost = prm["norm_mix_pre"], prm["norm_mix_post"], prm["norm_ffn_pre"], prm["norm_ffn_post"]
    dt_bias, a_log = _pad_lanes(prm["ssd_dt_bias"]), _pad_lanes(prm["ssd_a_log"])
    d_lanes = jnp.repeat(prm["ssd_d"], HEAD_DIM, axis=1)

    h = _rms_fwd(x2d, g_pre, dep=dep, out_dtype=BF16, name="rms_mix_pre")
    t, d = x2d.shape
    w_in_t, w_dt_t, cw, sw = get_w("in", h)
    proj = _matmul([(h, w_in_t)], mode="nt", out_dtype=F32, tm=1024, tn=1024, tk=2048, name="mm_proj",
                   extent=(t, D_MAIN, d))
    dtr = _matmul([(h, w_dt_t)], mode="nt", out_dtype=F32, tm=1024, tn=128, tk=2048, name="mm_dt")
    u1, uy = _conv_branch_fwd(proj, cw, prm["conv_dw_b"], prm["conv_ln_g"], prm["conv_ln_b"], nb=nb, seq=seq)
    xs_all = _ssd_pre_fwd(proj, sw, prm["ssd_conv_b"], nb=nb, seq=seq)
    uy, ys, states = _ssd_fwd(uy, xs_all, proj, dtr, dt_bias, a_log, d_lanes, prm["ssd_norm_w"], nb=nb, seq=seq)
    w_out = get_w("out", uy)
    mix = _matmul([(uy, w_out)], mode="nn", out_dtype=F32, tm=1024, tn=1024, tk=2048, name="mm_mix")
    x1 = _rms_fwd(mix, g_post, res=x2d, out_dtype=F32, name="rms_mix_post")
    h2 = _rms_fwd(x1, g_fpre, out_dtype=BF16, name="rms_ffn_pre")
    w_gate, w_up = get_w("up", h2)
    gt, up, act = _ffn_up(h2, w_gate, w_up, tm=1024, tn=512)
    w_down = get_w("down", act)
    f = _matmul([(act, w_down)], mode="nn", out_dtype=F32, tm=1024, tn=1024, tk=2816, name="mm_down")
    dx2, df, loss, dg_fpost = _ffn_post_loss(f, x1, tgt2d, g_fpost)

    dgt, dup = _ffn_bwd_act(df, w_down, gt, up, tm=1024, tn=512)
    dw_down = _matmul([(act, df)], mode="tn", out_dtype=F32, tm=1408, tn=1024, tk=2048, name="mm_dw_down")
    dw_gate = _matmul([(h2, dgt)], mode="tn", out_dtype=F32, tm=1024, tn=1408, tk=2048, name="mm_dw_gate", slot_out=True)
    dw_up = _matmul([(h2, dup)], mode="tn", out_dtype=F32, tm=1024, tn=1408, tk=2048, name="mm_dw_up", slot_out=True)
    dep = on_grads("ffn", (dw_down, dw_gate, dw_up))
    dh2 = _matmul([(dgt, w_gate), (dup, w_up)], mode="nt", out_dtype=F32, tm=1024, tn=1024, tk=1408, name="mm_dh2",
                  dep=dep)
    dx1, dg_fpre = _rms_bwd([dh2], x1, g_fpre, addend=dx2, out_dtype=F32, name="rms_ffn_pre_bwd")
    dmix, dg_post = _rms_bwd([dx1], mix, g_post, out_dtype=BF16, name="rms_mix_post_bwd")
    dw_out = _matmul([(uy, dmix)], mode="tn", out_dtype=F32, tm=1024, tn=1024, tk=2048, name="mm_dw_out")
    dep = on_grads("out", (dw_out,))
    duy = _matmul([(dmix, w_out)], mode="nt", out_dtype=F32, tm=1024, tn=1024, tk=2048, name="mm_duy", dep=dep)
    dproj, dcw, dcb, dlg, dlb = _conv_branch_bwd(duy, u1, proj, cw, prm["conv_ln_g"], prm["conv_ln_b"], nb=nb, seq=seq)
    dproj, dxs, ddtr, ssd_small = _ssd_bwd(dproj, duy, proj, ys, xs_all, dtr, states, dt_bias, a_log, d_lanes,
                                           prm["ssd_norm_w"], nb=nb, seq=seq)
    dproj, dsw, dsb = _ssd_pre_bwd(dproj, dxs, proj, sw, prm["ssd_conv_b"], nb=nb, seq=seq)
    ddtr_b = ddtr.astype(BF16)
    dw_in_t = _matmul([(dproj, h)], mode="tn", out_dtype=F32, tm=1024, tn=1024, tk=2048, name="mm_dw_main",
                      extent=(D_MAIN, d, t), out_rows=D_IN)
    dw_in_t = _dw_dt_rows(dw_in_t, ddtr_b, h)
    dep = on_grads("in", (dw_in_t,))
    dh_main = _matmul([(dproj, w_in_t)], mode="nn", out_dtype=F32, tm=1024, tn=1024, tk=2560, name="mm_dh_main",
                      extent=(t, d, D_MAIN), dep=dep)
    dh_dt = _matmul([(ddtr_b, w_dt_t)], mode="nn", out_dtype=F32, tm=1024, tn=1024, tk=128, name="mm_dh_dt")
    dx, dg_pre = _rms_bwd([dh_main, dh_dt], x2d, g_pre, addend=dx1, out_dtype=F32, name="rms_mix_pre_bwd")

    grads = {
        "norm_mix_pre": dg_pre,
        "w_in": dw_in_t,
        "conv_dw_w": dcw[:CONV_K], "conv_dw_b": dcb, "conv_ln_g": dlg, "conv_ln_b": dlb,
        "ssd_conv_w": dsw[:SSD_CONV_K], "ssd_conv_b": dsb,
        "ssd_dt_bias": ssd_small[2:3, :HEADS], "ssd_a_log": ssd_small[3:4, :HEADS], "ssd_d": ssd_small[1:2, :HEADS],
        "ssd_norm_w": ssd_small[0:1],
        "w_out": dw_out,
        "norm_mix_post": dg_post, "norm_ffn_pre": dg_fpre,
        "w_gate": dw_gate, "w_up": dw_up,
        "w_down": dw_down, "norm_ffn_post": dg_fpost,
    }
    return loss, dx, grads


BIG = ("w_in", "w_out", "w_gate", "w_up", "w_down")
HALF_AXIS = {"w_in": 1, "w_out": 0, "w_gate": 0, "w_up": 0, "w_down": 0}
GATHER_STAGES = {"in": ("w_in", "conv_dw_w", "ssd_conv_w"), "out": ("w_out",), "up": ("w_gate", "w_up"),
                 "down": ("w_down",)}
GATHER_ORDER = tuple(n for st in ("in", "out", "up", "down") for n in GATHER_STAGES[st])
SMALL = ("norm_mix_pre", "conv_dw_w", "conv_dw_b", "conv_ln_g", "conv_ln_b", "ssd_conv_w", "ssd_conv_b", "ssd_dt_bias",
         "ssd_a_log", "ssd_d", "ssd_norm_w", "norm_mix_post", "norm_ffn_pre", "norm_ffn_post")
WEIGHTS = ("norm_mix_pre", "w_in", "conv_dw_w", "conv_dw_b", "conv_ln_g", "conv_ln_b", "ssd_conv_w", "ssd_conv_b",
           "ssd_dt_bias", "ssd_a_log", "ssd_d", "ssd_norm_w", "w_out", "norm_mix_post", "norm_ffn_pre", "w_gate", "w_up",
           "w_down", "norm_ffn_post")


def _cols_from_slots(a):
    n, rows, w = a.shape
    return a.transpose(1, 0, 2).reshape(rows, n * w)


def kernel(x, norm_mix_pre, w_in, conv_dw_w, conv_dw_b, conv_ln_g, conv_ln_b, ssd_conv_w, ssd_conv_b, ssd_dt_bias, ssd_a_log, ssd_d, ssd_norm_w, w_out, norm_mix_post, norm_ffn_pre, w_gate, w_up, w_down, norm_ffn_post, loss_target, m_norm_mix_pre, m_w_in, m_conv_dw_w, m_conv_dw_b, m_conv_ln_g, m_conv_ln_b, m_ssd_conv_w, m_ssd_conv_b, m_ssd_dt_bias, m_ssd_a_log, m_ssd_d, m_ssd_norm_w, m_w_out, m_norm_mix_post, m_norm_ffn_pre, m_w_gate, m_w_up, m_w_down, m_norm_ffn_post, v_norm_mix_pre, v_w_in, v_conv_dw_w, v_conv_dw_b, v_conv_ln_g, v_conv_ln_b, v_ssd_conv_w, v_ssd_conv_b, v_ssd_dt_bias, v_ssd_a_log, v_ssd_d, v_ssd_norm_w, v_w_out, v_norm_mix_post, v_norm_ffn_pre, v_w_gate, v_w_up, v_w_down, v_norm_ffn_post):
    args = dict(locals())
    two_d = lambda n, a: jnp.swapaxes(a, 1, 2)[0] if n == "w_in" else a.reshape(a.shape[-2:])
    wts = {n: two_d(n, args[n]) for n in WEIGHTS}
    ms = {n: two_d(n, args["m_" + n]) for n in WEIGHTS}
    vs = {n: two_d(n, args["v_" + n]) for n in WEIGHTS}
    nb, seq, d = x.shape
    t = nb * seq
    xi, yi, ci = _coords()
    chip = 2 * xi + yi
    where = jnp.stack([ci, chip]).astype(jnp.int32)

    shards = {n: wts[n].astype(BF16) for n in BIG}
    shards.update(conv_dw_w=_pad_rows(wts["conv_dw_w"], 32), ssd_conv_w=_pad_rows(wts["ssd_conv_w"], 8))
    srcs = [shards[n] for n in GATHER_ORDER]
    plans = [_gather_plan(HALF_AXIS.get(n)) for n in GATHER_ORDER]
    ssems, rsems, srcs, lands, token = _split_start(
        srcs, [lax.empty((N_CHIPS,) + s.shape, s.dtype) for s in srcs], plans, name="gather_start")

    def get_w(stage, after):
        names = GATHER_STAGES[stage]
        pick = lambda seq_: [seq_[GATHER_ORDER.index(n)] for n in names]
        own, got = _split_wait(pick(ssems), pick(rsems), pick(srcs), pick(lands), pick(plans), after,
                               name="gather_wait_" + stage)
        got, own = dict(zip(names, got)), dict(zip(names, own))
        big = [n for n in names if n in BIG]
        got.update(zip(big, _forward_halves([got[n] for n in big], [HALF_AXIS[n] for n in big],
                                            name="gather_forward_" + stage)))
        full = {n: lax.dynamic_update_slice(got[n], own[n][None], (chip, 0, 0)) for n in got}
        if stage == "in":
            w_in_t = full["w_in"].reshape(D_IN, D_MODEL)
            return (w_in_t, _pad_rows(w_in_t[D_MAIN:], LANES), _cols_from_slots(full["conv_dw_w"]),
                    _cols_from_slots(full["ssd_conv_w"]))
        if stage == "out":
            return full["w_out"].reshape(D_MODEL, D_MODEL)
        if stage == "up":
            return _cols_from_slots(full["w_gate"]), _cols_from_slots(full["w_up"])
        return full["w_down"].reshape(D_FF, D_MODEL)

    reduce_groups = {"ffn": ("w_down", "w_gate", "w_up"), "out": ("w_out",), "in": ("w_in",)}
    in_flight = {}

    def on_grads(stage, gs):
        names = reduce_groups[stage]
        axes = [HALF_AXIS[n] for n in names]
        slots = [g if g.ndim == 3 else g.reshape((N_CHIPS, g.shape[0] // N_CHIPS, g.shape[1])) for g in gs]
        kept = _swap_other_halves(slots, axes, name="swap_other_halves_" + stage)
        sums = [_add_core_halves(g, l, where, ax) for g, l, ax in zip(slots, kept, axes)]
        ps = [s[1] for s in sums]
        ssem, rsem, ps, recv, started = _split_start(
            ps, [lax.empty((COPIES,) + p.shape[1:], p.dtype) for p in ps], [_owners_plan] * len(ps),
            name="owners_start_" + stage)
        in_flight[stage] = (ssem, rsem, ps, recv, [s[0] for s in sums])
        return started

    prm = {n: wts[n] for n in SMALL}
    loss, dx, grads = _local_grads(x.reshape(t, d), loss_target.reshape(t, d), prm, get_w, on_grads,
                                   nb=nb, seq=seq, dep=token)
    loss = lax.psum(loss[0, 0], MESH_AXES)

    halves = {}
    for stage, names in reduce_groups.items():
        ssem, rsem, ps, recv, own_sums = in_flight[stage]
        _, recv = _split_wait(ssem, rsem, ps, recv, [_owners_plan] * len(ps), dx, name="owners_wait_" + stage)
        for n, f32_sum, r in zip(names, own_sums, recv):
            halves[n] = _add_chip_sums(f32_sum, r, where, HALF_AXIS[n])
    halves = [halves[n] for n in BIG]
    other_halves = _swap_reduced_halves(halves)

    small_shapes = [grads[n].shape for n in SMALL]
    small_sum = _unpack(_all_reduce_small(_pack([grads[n] for n in SMALL])), small_shapes)
    small_grads = dict(zip(SMALL, small_sum))
    cwid, swid = D_CONV // N_CHIPS, D_XBC // N_CHIPS
    small_grads["conv_dw_w"] = lax.dynamic_slice(small_grads["conv_dw_w"], (0, chip * cwid), (CONV_K, cwid))
    small_grads["ssd_conv_w"] = lax.dynamic_slice(small_grads["ssd_conv_w"], (0, chip * swid), (SSD_CONV_K, swid))

    out_g, out_d, out_m, out_v = {}, {}, {}, {}
    for n, mine, other in zip(BIG, halves, other_halves):
        out_g[n], out_d[n], out_m[n], out_v[n] = _adamw_halves(wts[n], mine, other, ms[n], vs[n], where,
                                                                 HALF_AXIS[n], name="adamw_" + n)
    shard_shapes = [wts[n].shape for n in SMALL]
    pd, pm, pv = _adamw(_pack([wts[n] for n in SMALL]), _pack([small_grads[n] for n in SMALL]),
                        _pack([ms[n] for n in SMALL]), _pack([vs[n] for n in SMALL]), name="adamw_small")
    for n, dd, mm, vv in zip(SMALL, _unpack(pd, shard_shapes), _unpack(pm, shard_shapes), _unpack(pv, shard_shapes)):
        out_g[n], out_d[n], out_m[n], out_v[n] = small_grads[n], dd, mm, vv

    back = lambda n, a: jnp.swapaxes(a[None], 1, 2) if n == "w_in" else a.reshape(args[n].shape)
    outs = [back(n, o[n]) for o in (out_g, out_d, out_m, out_v) for n in WEIGHTS]
    return (loss, dx.reshape(nb, seq, d), *outs)
```

```python
import functools

import jax
import jax.numpy as jnp
from jax import lax
from jax.experimental import pallas as pl
from jax.experimental.pallas import tpu as pltpu

F32 = jnp.float32
BF16 = jnp.bfloat16
EPS = 1e-6

D_MODEL = 2048
D_CONV = 1024
D_SSD = 1024
D_XBC = 2048
HEADS = 16
HEAD_DIM = 64
GROUPS = 4
STATE = 128
CONV_K = 31
SSD_CONV_K = 4
D_FF = 5632
D_MAIN = 2 * D_CONV + D_SSD + D_XBC
D_IN = D_MAIN + HEADS
N_CHIPS = 4
LANES = 128
CHUNK = 128
PAIRS = HEADS // 2

ADAM_LR = 0.001
ADAM_B1 = 0.9
ADAM_B2 = 0.999
ADAM_EPS = 1e-08
ADAM_WD = 0.01
ADAM_STEP = 10

MESH_AXES = ("x", "y", "c")
VMEM_LIMIT = 56 * 1024 * 1024


def _sig(v):
    return 1.0 / (1.0 + jnp.exp(-v))


def _cparams(sem, vmem=VMEM_LIMIT):
    return pltpu.CompilerParams(dimension_semantics=sem, vmem_limit_bytes=vmem)


_DIMS = {"nn": ((1,), (0,)), "nt": ((1,), (1,)), "tn": ((0,), (0,))}


def _matmul(pairs, *, mode, out_dtype, tm, tn, tk, name, slot_out=False, extent=None, out_rows=None, dep=None,
            also_bf16=False):
    a0, b0 = pairs[0]
    if mode == "nn":
        (m, k), n = a0.shape, b0.shape[1]
    elif mode == "nt":
        (m, k), n = a0.shape, b0.shape[0]
    else:
        (k, m), n = a0.shape, b0.shape[1]
    if extent is not None:
        m, n, k = extent
    tm, tn, tk = min(tm, m), min(tn, n), min(tk, k)
    assert m % tm == 0 and n % tn == 0 and k % tk == 0, (name, m, n, k, tm, tn, tk)
    nk = k // tk
    npairs = len(pairs)
    deps = [] if dep is None else [dep]
    dims = (_DIMS[mode], ((), ()))

    use_scratch = nk > 1 and out_dtype != F32

    def body(*refs):
        ins, o_ref = refs[: 2 * npairs], refs[2 * npairs + len(deps)]
        dot = lambda p: lax.dot_general(ins[2 * p][...], ins[2 * p + 1][...], dims, preferred_element_type=F32)
        if nk == 1:
            part = dot(0)
            for p in range(1, npairs):
                part = part + dot(p)
            o_ref[...] = part.astype(out_dtype)
            if also_bf16:
                refs[2 * npairs + len(deps) + 1][...] = part.astype(BF16)
            return
        acc = refs[-1] if use_scratch else o_ref
        kk = pl.program_id(2)

        @pl.when(kk == 0)
        def _():
            acc[...] = jnp.zeros_like(acc)

        for p in range(npairs):
            acc[...] += dot(p)

        if use_scratch:
            @pl.when(kk == nk - 1)
            def _():
                o_ref[...] = acc[...].astype(out_dtype)

        if also_bf16:
            @pl.when(kk == nk - 1)
            def _():
                refs[2 * npairs + len(deps) + 1][...] = acc[...].astype(BF16)

    if mode == "nn":
        a_spec = pl.BlockSpec((tm, tk), lambda i, j, kk: (i, kk))
        b_spec = pl.BlockSpec((tk, tn), lambda i, j, kk: (kk, j))
    elif mode == "nt":
        a_spec = pl.BlockSpec((tm, tk), lambda i, j, kk: (i, kk))
        b_spec = pl.BlockSpec((tn, tk), lambda i, j, kk: (j, kk))
    else:
        a_spec = pl.BlockSpec((tk, tm), lambda i, j, kk: (kk, i))
        b_spec = pl.BlockSpec((tk, tn), lambda i, j, kk: (kk, j))
    if slot_out:
        out_shape = jax.ShapeDtypeStruct((n // tn, m, tn), out_dtype)
        out_spec = pl.BlockSpec((None, tm, tn), lambda i, j, kk: (j, i, 0))
    else:
        out_shape = jax.ShapeDtypeStruct((m if out_rows is None else out_rows, n), out_dtype)
        out_spec = pl.BlockSpec((tm, tn), lambda i, j, kk: (i, j))
    flat = [t for ab in pairs for t in ab]
    if also_bf16:
        out_spec = [out_spec, out_spec]
        out_shape = [out_shape, jax.ShapeDtypeStruct(out_shape.shape, BF16)]
    return pl.pallas_call(
        body,
        grid=(m // tm, n // tn, nk),
        in_specs=[a_spec, b_spec] * npairs + [pl.BlockSpec(memory_space=pl.ANY)] * len(deps),
        out_specs=out_spec,
        out_shape=out_shape,
        scratch_shapes=[pltpu.VMEM((tm, tn), F32)] if use_scratch else [],
        compiler_params=_cparams(("parallel", "parallel", "arbitrary")),
        name=name,
    )(*flat, *deps)


SUB_ROWS = 256


def _ffn_up(h2, wg, wu, *, tm, tn):
    t, k = h2.shape
    n = wg.shape[1]
    tm = min(tm, t)
    assert t % tm == 0 and n % tn == 0, (t, n, tm, tn)

    sub = min(SUB_ROWS, tm)

    def body(h_ref, wg_ref, wu_ref, g_ref, u_ref, a_ref):
        for r in range(tm // sub):
            rows = pl.ds(r * sub, sub)
            hv = h_ref[rows, :]
            g = jnp.dot(hv, wg_ref[...], preferred_element_type=F32)
            u = jnp.dot(hv, wu_ref[...], preferred_element_type=F32)
            g_ref[rows, :] = g.astype(BF16)
            u_ref[rows, :] = u.astype(BF16)
            a_ref[rows, :] = (g * _sig(g) * u).astype(BF16)

    o = jax.ShapeDtypeStruct((t, n), BF16)
    ospec = pl.BlockSpec((tm, tn), lambda i, j: (i, j))
    return pl.pallas_call(
        body,
        grid=(t // tm, n // tn),
        in_specs=[pl.BlockSpec((tm, k), lambda i, j: (i, 0)), pl.BlockSpec((k, tn), lambda i, j: (0, j)),
                  pl.BlockSpec((k, tn), lambda i, j: (0, j))],
        out_specs=[ospec, ospec, ospec],
        out_shape=[o, o, o],
        compiler_params=_cparams(("parallel", "parallel")),
        name="ffn_up",
    )(h2, wg, wu)


def _ffn_bwd_act(df, wd, gt, up, *, tm, tn):
    t, k = df.shape
    n = wd.shape[0]
    tm = min(tm, t)
    assert t % tm == 0 and n % tn == 0, (t, n, tm, tn)

    sub = min(SUB_ROWS, tm)

    def body(df_ref, wd_ref, g_ref, u_ref, dg_ref, du_ref):
        for r in range(tm // sub):
            rows = pl.ds(r * sub, sub)
            da = lax.dot_general(df_ref[rows, :], wd_ref[...], (_DIMS["nt"], ((), ())), preferred_element_type=F32)
            g = g_ref[rows, :].astype(F32)
            u = u_ref[rows, :].astype(F32)
            s = _sig(g)
            dg_ref[rows, :] = (da * u * s * (1.0 + g * (1.0 - s))).astype(BF16)
            du_ref[rows, :] = (da * g * s).astype(BF16)

    o = jax.ShapeDtypeStruct((t, n), BF16)
    blk = pl.BlockSpec((tm, tn), lambda i, j: (i, j))
    return pl.pallas_call(
        body,
        grid=(t // tm, n // tn),
        in_specs=[pl.BlockSpec((tm, k), lambda i, j: (i, 0)), pl.BlockSpec((tn, k), lambda i, j: (j, 0)), blk, blk],
        out_specs=[blk, blk],
        out_shape=[o, o],
        compiler_params=_cparams(("parallel", "parallel")),
        name="ffn_bwd_act",
    )(df, wd, gt, up)


def _dw_dt_rows(dw_in_t, dw_in_t_b, ddtr_b, h, *, tk=1024):
    t, d = h.shape
    tk = min(tk, t)
    nk = t // tk

    def body(buf_ref, bufb_ref, d_ref, h_ref, o_ref, ob_ref, acc):
        kk = pl.program_id(0)

        @pl.when(kk == 0)
        def _():
            acc[...] = jnp.zeros_like(acc)

        acc[...] += lax.dot_general(d_ref[...], h_ref[...], (_DIMS["tn"], ((), ())), preferred_element_type=F32)

        @pl.when(kk == nk - 1)
        def _():
            o_ref[...] = acc[0:HEADS, :]
            ob_ref[...] = acc[0:HEADS, :].astype(BF16)

    rows = pl.BlockSpec((HEADS, d), lambda kk: (D_MAIN // HEADS, 0))
    return pl.pallas_call(
        body,
        grid=(nk,),
        in_specs=[DEP_SPEC, DEP_SPEC, pl.BlockSpec((tk, LANES), lambda kk: (kk, 0)),
                  pl.BlockSpec((tk, d), lambda kk: (kk, 0))],
        out_specs=[rows, rows],
        out_shape=[jax.ShapeDtypeStruct(dw_in_t.shape, F32), jax.ShapeDtypeStruct(dw_in_t.shape, BF16)],
        input_output_aliases={0: 0, 1: 1},
        scratch_shapes=[pltpu.VMEM((LANES, d), F32)],
        compiler_params=_cparams(("arbitrary",)),
        name="mm_dw_dt",
    )(dw_in_t, dw_in_t_b, ddtr_b, h)


ROW_TILE = 256


DEP_SPEC = pl.BlockSpec(memory_space=pl.ANY)


def _rms_fwd(xv, g, *, res=None, dep=None, out_dtype, name):
    t, d = xv.shape
    has_res = res is not None
    deps = [] if dep is None else [dep]

    def body(*refs):
        x_ref, g_ref = refs[0], refs[1]
        o_ref = refs[-1]
        v = x_ref[...]
        r = lax.rsqrt(jnp.mean(v * v, axis=-1, keepdims=True) + EPS)
        y = v * r * g_ref[...]
        if has_res:
            y = refs[2][...] + y
        o_ref[...] = y.astype(out_dtype)

    row = pl.BlockSpec((ROW_TILE, d), lambda i: (i, 0))
    vec = pl.BlockSpec((1, d), lambda i: (0, 0))
    return pl.pallas_call(
        body,
        grid=(t // ROW_TILE,),
        in_specs=[row, vec] + ([row] if has_res else []) + [DEP_SPEC] * len(deps),
        out_specs=row,
        out_shape=jax.ShapeDtypeStruct((t, d), out_dtype),
        compiler_params=_cparams(("parallel",)),
        name=name,
    )(*([xv, g] + ([res] if has_res else []) + deps))


def _rms_bwd(dys, xv, g, *, addend=None, dep=None, out_dtype, name):
    t, d = xv.shape
    ndy = len(dys)
    has_add = addend is not None
    deps = [] if dep is None else [dep]

    def body(*refs):
        dy_refs = refs[:ndy]
        x_ref, g_ref = refs[ndy], refs[ndy + 1]
        dx_ref, dg_ref = refs[-2], refs[-1]
        dy = dy_refs[0][...].astype(F32)
        for rr in dy_refs[1:]:
            dy = dy + rr[...].astype(F32)
        v = x_ref[...]
        r = lax.rsqrt(jnp.mean(v * v, axis=-1, keepdims=True) + EPS)
        xh = v * r
        gdy = dy * g_ref[...]
        dx = r * (gdy - xh * jnp.mean(gdy * xh, axis=-1, keepdims=True))
        if has_add:
            dx = dx + refs[ndy + 2][...]
        dx_ref[...] = dx.astype(out_dtype)

        @pl.when(pl.program_id(0) == 0)
        def _():
            dg_ref[...] = jnp.zeros_like(dg_ref)

        dg_ref[...] += jnp.sum(dy * xh, axis=0, keepdims=True)

    row = pl.BlockSpec((ROW_TILE, d), lambda i: (i, 0))
    vec = pl.BlockSpec((1, d), lambda i: (0, 0))
    return pl.pallas_call(
        body,
        grid=(t // ROW_TILE,),
        in_specs=[row] * ndy + [row, vec] + ([row] if has_add else []) + [DEP_SPEC] * len(deps),
        out_specs=[row, vec],
        out_shape=[jax.ShapeDtypeStruct((t, d), out_dtype), jax.ShapeDtypeStruct((1, d), F32)],
        compiler_params=_cparams(("arbitrary",)),
        name=name,
    )(*(list(dys) + [xv, g] + ([addend] if has_add else []) + deps))


def _ffn_post_loss(f, x1, tgt, g):
    t, d = f.shape

    def body(f_ref, x1_ref, t_ref, g_ref, dx2_ref, df_ref, loss_ref, dg_ref):
        v = f_ref[...]
        gv = g_ref[...]
        r = lax.rsqrt(jnp.mean(v * v, axis=-1, keepdims=True) + EPS)
        fh = v * r
        e = x1_ref[...] + fh * gv - t_ref[...]
        dx2 = e * (1.0 / d)
        dx2_ref[...] = dx2
        gdy = dx2 * gv
        df_ref[...] = (r * (gdy - fh * jnp.mean(gdy * fh, axis=-1, keepdims=True))).astype(BF16)

        @pl.when(pl.program_id(0) == 0)
        def _():
            dg_ref[...] = jnp.zeros_like(dg_ref)
            loss_ref[...] = jnp.zeros_like(loss_ref)

        dg_ref[...] += jnp.sum(dx2 * fh, axis=0, keepdims=True)
        per_tok = jnp.mean(e * e, axis=-1, keepdims=True)
        loss_ref[...] += 0.5 * jnp.sum(per_tok, axis=0, keepdims=True)

    row = pl.BlockSpec((ROW_TILE, d), lambda i: (i, 0))
    vec = pl.BlockSpec((1, d), lambda i: (0, 0))
    return pl.pallas_call(
        body,
        grid=(t // ROW_TILE,),
        in_specs=[row, row, row, vec],
        out_specs=[row, row, pl.BlockSpec((1, 1), lambda i: (0, 0)), vec],
        out_shape=[jax.ShapeDtypeStruct((t, d), F32), jax.ShapeDtypeStruct((t, d), BF16),
                   jax.ShapeDtypeStruct((1, 1), F32), jax.ShapeDtypeStruct((1, d), F32)],
        compiler_params=_cparams(("arbitrary",)),
        name="ffn_post_loss",
    )(f, x1, tgt, g)


CONV_ROWS = 256
TAP_ROWS = 64
HALO31 = 32
HALO4 = 8


def _sum8(v):
    return jnp.sum(v.reshape(v.shape[0] // 8, 8, v.shape[1]), axis=0)


SUBLANES = 8
PHASE_SPAN = (CONV_K - 1) // SUBLANES * SUBLANES


def _phase_scratch(ts):
    return pltpu.VMEM((SUBLANES, ts + PHASE_SPAN, LANES), F32)


def _phase_copies(ph, buf, ln, base, ts):
    for s in range(SUBLANES):
        n = ts + (CONV_K - 1 - s) // SUBLANES * SUBLANES
        ph[s, 0:n, :] = buf[pl.ds(base + s, n), ln]


def _tap_rows(ph, off, r0):
    s = off % SUBLANES
    return ph[s, pl.ds(r0 + off - s, TAP_ROWS), :]


def _conv_branch_fwd(proj, cw, cb, lg, lb, *, nb, seq):
    ts, c, halo = CONV_ROWS, D_CONV, HALO31
    ns = seq // ts
    base = halo - CONV_K + 1

    def body(ca_ref, cg_ref, w_ref, b_ref, lg_ref, lb_ref, u1_ref, u_ref, ubuf, uph):
        i = pl.program_id(1)

        @pl.when(i == 0)
        def _():
            ubuf[0:halo, :] = jnp.zeros((halo, c), F32)

        @pl.when(i > 0)
        def _():
            ubuf[0:halo, :] = ubuf[ts:ts + halo, :]

        ubuf[halo:halo + ts, :] = ca_ref[...] * _sig(cg_ref[...])

        def lane_tile(j, carry):
            ln = pl.ds(pl.multiple_of(j * LANES, LANES), LANES)
            _phase_copies(uph, ubuf, ln, base, ts)
            for r in range(ts // TAP_ROWS):
                acc = jnp.broadcast_to(b_ref[:, ln], (TAP_ROWS, LANES))
                for k in range(CONV_K):
                    acc = acc + w_ref[pl.ds(k, 1), ln] * _tap_rows(uph, k, r * TAP_ROWS)
                u1_ref[pl.ds(r * TAP_ROWS, TAP_ROWS), ln] = acc
            return carry

        lax.fori_loop(0, c // LANES, lane_tile, 0)
        v = u1_ref[...]
        mu = jnp.mean(v, axis=-1, keepdims=True)
        dv = v - mu
        xh = dv * lax.rsqrt(jnp.mean(dv * dv, axis=-1, keepdims=True) + EPS)
        u2 = xh * lg_ref[...] + lb_ref[...]
        u_ref[...] = (u2 * _sig(u2)).astype(BF16)

    t = nb * seq
    row = lambda col: pl.BlockSpec((ts, c), lambda b, i: (b * ns + i, col))
    vec = pl.BlockSpec((1, c), lambda b, i: (0, 0))
    return pl.pallas_call(
        body,
        grid=(nb, ns),
        in_specs=[row(0), row(1), pl.BlockSpec((32, c), lambda b, i: (0, 0)), vec, vec, vec],
        out_specs=[row(0), row(0)],
        out_shape=[jax.ShapeDtypeStruct((t, c), F32), jax.ShapeDtypeStruct((t, c + D_SSD), BF16)],
        scratch_shapes=[pltpu.VMEM((halo + ts, c), F32), _phase_scratch(ts)],
        compiler_params=_cparams(("parallel", "arbitrary")),
        name="conv_branch_fwd",
    )(proj, proj, cw, cb, lg, lb)


def _conv_branch_bwd(duy, u1, proj, cw, lg, lb, *, nb, seq):
    ts, c, halo = CONV_ROWS, D_CONV, HALO31
    ns = seq // ts
    base = halo - CONV_K + 1
    hb = ts // halo

    def body(du_ref, u1_ref, ca_ref, cg_ref, cah_ref, cgh_ref, w_ref, lg_ref, lb_ref,
             dcacg_ref, dw_ref, db_ref, dlg_ref, dlb_ref,
             ubuf, dbuf, du0buf, dwacc, dbacc, dlgacc, dlbacc, uph, dph):
        b, i = pl.program_id(0), pl.program_id(1)
        rc = ns - 1 - i

        @pl.when(jnp.logical_and(b == 0, i == 0))
        def _():
            dwacc[...] = jnp.zeros_like(dwacc)
            dbacc[...] = jnp.zeros_like(dbacc)
            dlgacc[...] = jnp.zeros_like(dlgacc)
            dlbacc[...] = jnp.zeros_like(dlbacc)

        @pl.when(i == 0)
        def _():
            dbuf[ts:ts + halo, :] = jnp.zeros((halo, c), F32)

        @pl.when(i > 0)
        def _():
            dbuf[ts:ts + halo, :] = dbuf[0:halo, :]

        v = u1_ref[...]
        mu = jnp.mean(v, axis=-1, keepdims=True)
        dv = v - mu
        rstd = lax.rsqrt(jnp.mean(dv * dv, axis=-1, keepdims=True) + EPS)
        xh = dv * rstd
        lgv = lg_ref[...]
        u2 = xh * lgv + lb_ref[...]
        s2 = _sig(u2)
        du2 = du_ref[...] * (s2 * (1.0 + u2 * (1.0 - s2)))
        dlgacc[...] += jnp.sum(du2 * xh, axis=0, keepdims=True)
        dlbacc[...] += jnp.sum(du2, axis=0, keepdims=True)
        gd = du2 * lgv
        du1 = rstd * (gd - jnp.mean(gd, axis=-1, keepdims=True) - xh * jnp.mean(gd * xh, axis=-1, keepdims=True))
        dbacc[...] += jnp.sum(du1, axis=0, keepdims=True)
        dbuf[0:ts, :] = du1

        @pl.when(rc == 0)
        def _():
            ubuf[0:halo, :] = jnp.zeros((halo, c), F32)

        @pl.when(rc > 0)
        def _():
            ubuf[0:halo, :] = cah_ref[...] * _sig(cgh_ref[...])

        cav = ca_ref[...]
        sg = _sig(cg_ref[...])
        ubuf[halo:halo + ts, :] = cav * sg

        def lane_tile(j, carry):
            ln = pl.ds(pl.multiple_of(j * LANES, LANES), LANES)
            _phase_copies(uph, ubuf, ln, base, ts)
            _phase_copies(dph, dbuf, ln, 0, ts)
            for r in range(ts // TAP_ROWS):
                r0 = r * TAP_ROWS
                d1 = dbuf[pl.ds(r0, TAP_ROWS), ln]
                acc = jnp.zeros((TAP_ROWS, LANES), F32)
                for k in range(CONV_K):
                    acc = acc + w_ref[pl.ds(k, 1), ln] * _tap_rows(dph, CONV_K - 1 - k, r0)
                    dwacc[pl.ds(k * 8, 8), ln] += _sum8(d1 * _tap_rows(uph, k, r0))
                du0buf[pl.ds(r0, TAP_ROWS), ln] = acc
            return carry

        lax.fori_loop(0, c // LANES, lane_tile, 0)
        du0 = du0buf[...]
        dcacg_ref[:, 0:c] = (du0 * sg).astype(BF16)
        dcacg_ref[:, c:2 * c] = (du0 * cav * sg * (1.0 - sg)).astype(BF16)

        @pl.when(jnp.logical_and(b == nb - 1, i == ns - 1))
        def _():
            for k in range(CONV_K):
                dw_ref[pl.ds(k, 1), :] = jnp.sum(dwacc[pl.ds(k * 8, 8), :], axis=0, keepdims=True)
            dw_ref[pl.ds(CONV_K, 1), :] = jnp.zeros((1, c), F32)
            db_ref[...] = dbacc[...]
            dlg_ref[...] = dlgacc[...]
            dlb_ref[...] = dlbacc[...]

    t = nb * seq
    rowblk = lambda b, i: b * ns + (ns - 1 - i)
    row = lambda col: pl.BlockSpec((ts, c), lambda b, i: (rowblk(b, i), col))
    hrow = lambda col: pl.BlockSpec((halo, c), lambda b, i: (jnp.maximum(rowblk(b, i) * hb - 1, 0), col))
    vec = pl.BlockSpec((1, c), lambda b, i: (0, 0))
    wspec = pl.BlockSpec((32, c), lambda b, i: (0, 0))
    return pl.pallas_call(
        body,
        grid=(nb, ns),
        in_specs=[row(0), row(0), row(0), row(1), hrow(0), hrow(1), wspec, vec, vec],
        out_specs=[pl.BlockSpec((ts, 2 * c), lambda b, i: (rowblk(b, i), 0)), wspec, vec, vec, vec],
        out_shape=[jax.ShapeDtypeStruct((t, D_MAIN), BF16), jax.ShapeDtypeStruct((32, c), F32),
                   jax.ShapeDtypeStruct((1, c), F32), jax.ShapeDtypeStruct((1, c), F32), jax.ShapeDtypeStruct((1, c), F32)],
        scratch_shapes=[pltpu.VMEM((halo + ts, c), F32), pltpu.VMEM((ts + halo, c), F32), pltpu.VMEM((ts, c), F32),
                        pltpu.VMEM((CONV_K * 8, c), F32), pltpu.VMEM((1, c), F32), pltpu.VMEM((1, c), F32),
                        pltpu.VMEM((1, c), F32), _phase_scratch(ts), _phase_scratch(ts)],
        compiler_params=_cparams(("arbitrary", "arbitrary")),
        name="conv_branch_bwd",
    )(duy, u1, proj, proj, proj, proj, cw, lg, lb)


XBC_COL0 = (2 * D_CONV + D_SSD) // 1024


def _ssd_pre_fwd(proj, sw, sb, *, nb, seq):
    ts, c, halo = CONV_ROWS, 1024, HALO4
    ns = seq // ts
    base = halo - SSD_CONV_K + 1

    def body(x_ref, w_ref, b_ref, o_ref, xbuf):
        i = pl.program_id(2)

        @pl.when(i == 0)
        def _():
            xbuf[0:halo, :] = jnp.zeros((halo, c), F32)

        @pl.when(i > 0)
        def _():
            xbuf[0:halo, :] = xbuf[ts:ts + halo, :]

        xbuf[halo:halo + ts, :] = x_ref[...]

        def lane_tile(j, carry):
            ln = pl.ds(pl.multiple_of(j * LANES, LANES), LANES)
            for r in range(ts // TAP_ROWS):
                acc = jnp.broadcast_to(b_ref[:, ln], (TAP_ROWS, LANES))
                for k in range(SSD_CONV_K):
                    acc = acc + w_ref[pl.ds(k, 1), ln] * xbuf[pl.ds(r * TAP_ROWS + base + k, TAP_ROWS), ln]
                o_ref[pl.ds(r * TAP_ROWS, TAP_ROWS), ln] = acc * _sig(acc)
            return carry

        lax.fori_loop(0, c // LANES, lane_tile, 0)

    t = nb * seq
    return pl.pallas_call(
        body,
        grid=(2, nb, ns),
        in_specs=[pl.BlockSpec((ts, c), lambda j, b, i: (b * ns + i, XBC_COL0 + j)),
                  pl.BlockSpec((8, c), lambda j, b, i: (0, j)), pl.BlockSpec((1, c), lambda j, b, i: (0, j))],
        out_specs=pl.BlockSpec((ts, c), lambda j, b, i: (b * ns + i, j)),
        out_shape=jax.ShapeDtypeStruct((t, D_XBC), F32),
        scratch_shapes=[pltpu.VMEM((halo + ts, c), F32)],
        compiler_params=_cparams(("parallel", "parallel", "arbitrary")),
        name="ssd_pre_fwd",
    )(proj, sw, sb)


def _ssd_pre_bwd(dproj, dxs, proj, sw, sb, *, nb, seq):
    ts, c, halo = CONV_ROWS, 1024, HALO4
    ns = seq // ts
    base = halo - SSD_CONV_K + 1
    hb = ts // halo

    def body(dproj_ref, d_ref, x_ref, xh_ref, w_ref, b_ref, dx_ref, dw_ref, db_ref, xbuf, dbuf, dwacc, dbacc):
        b, i = pl.program_id(1), pl.program_id(2)
        rc = ns - 1 - i

        @pl.when(jnp.logical_and(b == 0, i == 0))
        def _():
            dwacc[...] = jnp.zeros_like(dwacc)
            dbacc[...] = jnp.zeros_like(dbacc)

        @pl.when(i == 0)
        def _():
            dbuf[ts:ts + halo, :] = jnp.zeros((halo, c), F32)

        @pl.when(i > 0)
        def _():
            dbuf[ts:ts + halo, :] = dbuf[0:halo, :]

        @pl.when(rc == 0)
        def _():
            xbuf[0:halo, :] = jnp.zeros((halo, c), F32)

        @pl.when(rc > 0)
        def _():
            xbuf[0:halo, :] = xh_ref[...]

        xbuf[halo:halo + ts, :] = x_ref[...]

        def pre_tile(j, carry):
            ln = pl.ds(pl.multiple_of(j * LANES, LANES), LANES)
            for r in range(ts // TAP_ROWS):
                r0 = r * TAP_ROWS
                acc = jnp.broadcast_to(b_ref[:, ln], (TAP_ROWS, LANES))
                for k in range(SSD_CONV_K):
                    acc = acc + w_ref[pl.ds(k, 1), ln] * xbuf[pl.ds(r0 + base + k, TAP_ROWS), ln]
                s = _sig(acc)
                dc = d_ref[pl.ds(r0, TAP_ROWS), ln] * (s * (1.0 + acc * (1.0 - s)))
                dbuf[pl.ds(r0, TAP_ROWS), ln] = dc
                dbacc[:, ln] += _sum8(dc)
            return carry

        lax.fori_loop(0, c // LANES, pre_tile, 0)

        def lane_tile(j, carry):
            ln = pl.ds(pl.multiple_of(j * LANES, LANES), LANES)
            for r in range(ts // TAP_ROWS):
                r0 = r * TAP_ROWS
                d1 = dbuf[pl.ds(r0, TAP_ROWS), ln]
                acc = jnp.zeros((TAP_ROWS, LANES), F32)
                for k in range(SSD_CONV_K):
                    acc = acc + w_ref[pl.ds(k, 1), ln] * dbuf[pl.ds(r0 + SSD_CONV_K - 1 - k, TAP_ROWS), ln]
                    dwacc[pl.ds(k * 8, 8), ln] += _sum8(d1 * xbuf[pl.ds(r0 + base + k, TAP_ROWS), ln])
                dx_ref[pl.ds(r0, TAP_ROWS), ln] = acc.astype(BF16)
            return carry

        lax.fori_loop(0, c // LANES, lane_tile, 0)

        @pl.when(jnp.logical_and(b == nb - 1, i == ns - 1))
        def _():
            for k in range(SSD_CONV_K):
                dw_ref[pl.ds(k, 1), :] = jnp.sum(dwacc[pl.ds(k * 8, 8), :], axis=0, keepdims=True)
            dw_ref[pl.ds(SSD_CONV_K, 8 - SSD_CONV_K), :] = jnp.zeros((8 - SSD_CONV_K, c), F32)
            db_ref[...] = jnp.sum(dbacc[...], axis=0, keepdims=True)

    t = nb * seq
    rowblk = lambda b, i: b * ns + (ns - 1 - i)
    return pl.pallas_call(
        body,
        grid=(2, nb, ns),
        in_specs=[DEP_SPEC, pl.BlockSpec((ts, c), lambda j, b, i: (rowblk(b, i), j)),
                  pl.BlockSpec((ts, c), lambda j, b, i: (rowblk(b, i), XBC_COL0 + j)),
                  pl.BlockSpec((halo, c), lambda j, b, i: (jnp.maximum(rowblk(b, i) * hb - 1, 0), XBC_COL0 + j)),
                  pl.BlockSpec((8, c), lambda j, b, i: (0, j)), pl.BlockSpec((1, c), lambda j, b, i: (0, j))],
        out_specs=[pl.BlockSpec((ts, c), lambda j, b, i: (rowblk(b, i), XBC_COL0 + j)),
                   pl.BlockSpec((8, c), lambda j, b, i: (0, j)), pl.BlockSpec((1, c), lambda j, b, i: (0, j))],
        out_shape=[jax.ShapeDtypeStruct(dproj.shape, BF16), jax.ShapeDtypeStruct((8, D_XBC), F32),
                   jax.ShapeDtypeStruct((1, D_XBC), F32)],
        input_output_aliases={0: 0},
        scratch_shapes=[pltpu.VMEM((halo + ts, c), F32), pltpu.VMEM((ts + halo, c), F32),
                        pltpu.VMEM((SSD_CONV_K * 8, c), F32), pltpu.VMEM((8, c), F32)],
        compiler_params=_cparams(("arbitrary", "arbitrary", "arbitrary")),
        name="ssd_pre_bwd",
    )(dproj, dxs, proj, proj, sw, sb)


Z_COL = (2 * D_CONV) // 1024
GROUP_W = D_SSD // GROUPS


def _softplus(v):
    return jnp.maximum(v, 0.0) + jnp.log(1.0 + jnp.exp(-jnp.abs(v)))


def _dot(a, b):
    return jnp.dot(a, b, preferred_element_type=F32)


def _dot_nt(a, b):
    return lax.dot_general(a, b, (_DIMS["nt"], ((), ())), preferred_element_type=F32)


def _dot_tn(a, b):
    return lax.dot_general(a, b, (_DIMS["tn"], ((), ())), preferred_element_type=F32)


def _dot_exact(a, b):
    return jnp.dot(a, b, precision=lax.Precision.HIGHEST, preferred_element_type=F32)


def _chunk_decays(dtr_ref, bias_ref, alog_ref):
    q = CHUNK
    ii = lax.broadcasted_iota(jnp.int32, (q, q), 0)
    jj = lax.broadcasted_iota(jnp.int32, (q, q), 1)
    tri = jj <= ii
    dt = _softplus(dtr_ref[...] + bias_ref[...])
    a_head = -jnp.exp(alog_ref[...])
    cs = _dot_exact(tri.astype(F32), dt * a_head)
    return tri, dt, a_head, cs, cs.T


def _ssd_fwd(uy, xs_all, proj, dtr, dt_bias, a_log, d_lanes, norm_w, *, nb, seq):
    q = CHUNK
    nc = seq // q
    t = nb * seq

    def body(uy_ref, xs_ref, bm_ref, cm_ref, z_ref, dtr_ref, bias_ref, alog_ref, dl_ref, nw_ref,
             y_ref, ys_ref, st_ref, state):
        @pl.when(pl.program_id(1) == 0)
        def _():
            state[...] = jnp.zeros_like(state)

        tri, dt, _, cs, cst = _chunk_decays(dtr_ref, bias_ref, alog_ref)
        first = lax.broadcasted_iota(jnp.int32, (1, LANES), 1) < HEAD_DIM
        for g in range(GROUPS):
            gl = slice(g * STATE, (g + 1) * STATE)
            bb = bm_ref[:, gl].astype(BF16)
            cb = cm_ref[:, gl].astype(BF16)
            scores = _dot_nt(cb, bb)
            for p in range(2):
                pr = 2 * g + p
                h0 = 2 * pr
                sl = slice(pr * LANES, (pr + 1) * LANES)
                xv = xs_ref[:, sl]
                dtp = jnp.where(first, dt[:, h0:h0 + 1], dt[:, h0 + 1:h0 + 2])
                csp = jnp.where(first, cs[:, h0:h0 + 1], cs[:, h0 + 1:h0 + 2])
                xd = xv * dtp
                yv = None
                for hh, keep in ((h0, first), (h0 + 1, jnp.logical_not(first))):
                    decay = jnp.where(tri, jnp.exp(cs[:, hh:hh + 1] - cst[hh:hh + 1, :]), 0.0)
                    part = _dot((scores * decay).astype(BF16), jnp.where(keep, xd, 0.0).astype(BF16))
                    yv = part if yv is None else yv + part
                hp = state[pr]
                st_ref[0, pr] = hp
                yv = yv + jnp.exp(csp) * _dot(cb, hp.astype(BF16))
                last = csp[q - 1:q, :]
                state[pr] = jnp.exp(last) * hp + _dot_tn(bb, (xd * jnp.exp(last - csp)).astype(BF16))
                ys_ref[:, sl] = yv + dl_ref[:, sl] * xv
        zv = z_ref[...]
        gated = ys_ref[...] * (zv * _sig(zv))
        for g in range(GROUPS):
            gl = slice(g * GROUP_W, (g + 1) * GROUP_W)
            v = gated[:, gl]
            r = lax.rsqrt(jnp.mean(v * v, axis=-1, keepdims=True) + EPS)
            y_ref[:, gl] = (v * r * nw_ref[:, gl]).astype(BF16)

    blk = lambda w, col: pl.BlockSpec((q, w), lambda b, c: (b * nc + c, col))
    vec = lambda w: pl.BlockSpec((1, w), lambda b, c: (0, 0))
    return pl.pallas_call(
        body,
        grid=(nb, nc),
        in_specs=[DEP_SPEC, blk(D_SSD, 0), blk(GROUPS * STATE, 2), blk(GROUPS * STATE, 3), blk(D_SSD, Z_COL),
                  blk(LANES, 0), vec(LANES), vec(LANES), vec(D_SSD), vec(D_SSD)],
        out_specs=[blk(D_SSD, 1), blk(D_SSD, 0),
                   pl.BlockSpec((1, PAIRS, STATE, LANES), lambda b, c: (b * nc + c, 0, 0, 0))],
        out_shape=[jax.ShapeDtypeStruct(uy.shape, BF16), jax.ShapeDtypeStruct((t, D_SSD), F32),
                   jax.ShapeDtypeStruct((nb * nc, PAIRS, STATE, LANES), F32)],
        input_output_aliases={0: 0},
        scratch_shapes=[pltpu.VMEM((PAIRS, STATE, LANES), F32)],
        compiler_params=_cparams(("parallel", "arbitrary")),
        name="ssd_fwd",
    )(uy, xs_all, xs_all, xs_all, proj, dtr, dt_bias, a_log, d_lanes, norm_w)


def _ssd_bwd(dproj, duy, proj, ys, xs_all, dtr, states, dt_bias, a_log, d_lanes, norm_w, *, nb, seq):
    q = CHUNK
    nc = seq // q
    t = nb * seq

    def body(dproj_ref, dy_ref, z_ref, ys_ref, xs_ref, bm_ref, cm_ref, dtr_ref, st_ref, bias_ref, alog_ref, dl_ref,
             nw_ref, dz_ref, dx_ref, ddtr_ref, small_ref,
             dstate, dys_buf, dcsl, ddtl, dcst, dnw_acc, dd_acc, dbias_acc, da_acc):
        b, c = pl.program_id(0), pl.program_id(1)

        @pl.when(jnp.logical_and(b == 0, c == 0))
        def _():
            dnw_acc[...] = jnp.zeros_like(dnw_acc)
            dd_acc[...] = jnp.zeros_like(dd_acc)
            dbias_acc[...] = jnp.zeros_like(dbias_acc)
            da_acc[...] = jnp.zeros_like(da_acc)
            dcst[...] = jnp.zeros_like(dcst)

        @pl.when(c == 0)
        def _():
            dstate[...] = jnp.zeros_like(dstate)

        zv = z_ref[...]
        sz = _sig(zv)
        silz = zv * sz
        ysv = ys_ref[...]
        gated = ysv * silz
        dyv = dy_ref[...]
        nwv = nw_ref[...]
        for g in range(GROUPS):
            gl = slice(g * GROUP_W, (g + 1) * GROUP_W)
            v = gated[:, gl]
            r = lax.rsqrt(jnp.mean(v * v, axis=-1, keepdims=True) + EPS)
            yn = v * r
            dyn = dyv[:, gl] * nwv[:, gl]
            dnw_acc[:, gl] += jnp.sum(dyv[:, gl] * yn, axis=0, keepdims=True)
            dys_buf[:, gl] = r * (dyn - yn * jnp.mean(dyn * yn, axis=-1, keepdims=True))
        dgated = dys_buf[...]
        dz_ref[...] = (dgated * ysv * (sz * (1.0 + zv * (1.0 - sz)))).astype(BF16)
        dys_all = dgated * silz
        dys_buf[...] = dys_all
        dd_acc[...] += jnp.sum(dys_all * xs_ref[...], axis=0, keepdims=True)

        tri, dt, a_head, cs, cst = _chunk_decays(dtr_ref, bias_ref, alog_ref)
        lane = lax.broadcasted_iota(jnp.int32, (1, LANES), 1)
        first = lane < HEAD_DIM
        dcs_h = jnp.zeros((q, LANES), F32)
        for g in range(GROUPS):
            gl = slice(g * STATE, (g + 1) * STATE)
            bb = bm_ref[:, gl].astype(BF16)
            cb = cm_ref[:, gl].astype(BF16)
            scores = _dot_nt(cb, bb)
            dscores = jnp.zeros((q, q), F32)
            dbg = jnp.zeros((q, STATE), F32)
            dcg = jnp.zeros((q, STATE), F32)
            for p in range(2):
                pr = 2 * g + p
                h0 = 2 * pr
                sl = slice(pr * LANES, (pr + 1) * LANES)
                xv = xs_ref[:, sl]
                dyp = dys_buf[:, sl]
                dtp = jnp.where(first, dt[:, h0:h0 + 1], dt[:, h0 + 1:h0 + 2])
                csp = jnp.where(first, cs[:, h0:h0 + 1], cs[:, h0 + 1:h0 + 2])
                xd = xv * dtp
                xdb = xd.astype(BF16)
                hp = st_ref[0, pr]
                dhn = dstate[pr]
                hpb = hp.astype(BF16)
                dhnb = dhn.astype(BF16)
                lam = jnp.exp(csp)
                last = csp[q - 1:q, :]
                gam = jnp.exp(last)
                w = jnp.exp(last - csp)
                dxd = jnp.zeros((q, LANES), F32)
                for hh, keep in ((h0, first), (h0 + 1, jnp.logical_not(first))):
                    decay = jnp.where(tri, jnp.exp(cs[:, hh:hh + 1] - cst[hh:hh + 1, :]), 0.0)
                    m = scores * decay
                    dym = jnp.where(keep, dyp, 0.0).astype(BF16)
                    dm = _dot_nt(dym, xdb)
                    dxd = dxd + _dot_tn(m.astype(BF16), dym)
                    e = dm * m
                    dcs_h = dcs_h + jnp.where(lane == hh, jnp.sum(e, axis=1, keepdims=True), 0.0)
                    dcst[hh:hh + 1, :] = jnp.sum(e, axis=0, keepdims=True)
                    dscores = dscores + dm * decay
                yoff = lam * _dot(cb, hpb)
                ldy = (lam * dyp).astype(BF16)
                dcg = dcg + _dot_nt(ldy, hpb)
                dstate[pr] = gam * dhn + _dot_tn(cb, ldy)
                bdh = _dot(bb, dhnb)
                dxd = dxd + w * bdh
                xdw = xd * w
                dbg = dbg + _dot_nt(xdw.astype(BF16), dhnb)
                wd = xdw * bdh
                dcsl[:, sl] = dyp * yoff - wd
                dcsl[q - 1:q, sl] += (jnp.sum(wd, axis=0, keepdims=True)
                                      + gam * jnp.sum(dhn * hp, axis=0, keepdims=True))
                dx_ref[:, sl] = dxd * dtp + dyp * dl_ref[:, sl]
                ddtl[:, sl] = dxd * xv
            dsb = dscores.astype(BF16)
            dx_ref[:, D_SSD + g * STATE:D_SSD + (g + 1) * STATE] = dbg + _dot_tn(dsb, cb)
            dx_ref[:, D_SSD + (GROUPS + g) * STATE:D_SSD + (GROUPS + g + 1) * STATE] = dcg + _dot(dsb, bb)

        li = lax.broadcasted_iota(jnp.int32, (D_SSD, LANES), 0)
        hi = lax.broadcasted_iota(jnp.int32, (D_SSD, LANES), 1)
        sel = (li // HEAD_DIM == hi).astype(F32)
        dcs_h = dcs_h + _dot_exact(dcsl[...], sel) - dcst[...].T
        ddt = _dot_exact(ddtl[...], sel)
        upper = lax.broadcasted_iota(jnp.int32, (q, q), 1) >= lax.broadcasted_iota(jnp.int32, (q, q), 0)
        da = _dot_exact(upper.astype(F32), dcs_h)
        ddt = ddt + da * a_head
        da_acc[...] += jnp.sum(da * dt, axis=0, keepdims=True)
        ddtr = ddt * _sig(dtr_ref[...] + bias_ref[...])
        ddtr_ref[...] = ddtr
        dbias_acc[...] += jnp.sum(ddtr, axis=0, keepdims=True)

        @pl.when(jnp.logical_and(b == nb - 1, c == nc - 1))
        def _():
            small_ref[...] = jnp.zeros_like(small_ref)
            small_ref[0:1, :] = dnw_acc[...]
            small_ref[1:2, 0:LANES] = _dot_exact(jnp.broadcast_to(dd_acc[...], (8, D_SSD)), sel)[0:1, :]
            small_ref[2:3, 0:LANES] = dbias_acc[...]
            small_ref[3:4, 0:LANES] = da_acc[...] * a_head

    rowblk = lambda b, c: b * nc + (nc - 1 - c)
    blk = lambda w, col: pl.BlockSpec((q, w), lambda b, c: (rowblk(b, c), col))
    vec = lambda w: pl.BlockSpec((1, w), lambda b, c: (0, 0))
    return pl.pallas_call(
        body,
        grid=(nb, nc),
        in_specs=[DEP_SPEC, blk(D_SSD, 1), blk(D_SSD, Z_COL), blk(D_SSD, 0), blk(D_SSD, 0), blk(GROUPS * STATE, 2),
                  blk(GROUPS * STATE, 3), blk(LANES, 0),
                  pl.BlockSpec((1, PAIRS, STATE, LANES), lambda b, c: (rowblk(b, c), 0, 0, 0)),
                  vec(LANES), vec(LANES), vec(D_SSD), vec(D_SSD)],
        out_specs=[blk(D_SSD, Z_COL), blk(D_XBC, 0), blk(LANES, 0), pl.BlockSpec((8, D_SSD), lambda b, c: (0, 0))],
        out_shape=[jax.ShapeDtypeStruct(dproj.shape, BF16), jax.ShapeDtypeStruct((t, D_XBC), F32),
                   jax.ShapeDtypeStruct((t, LANES), F32), jax.ShapeDtypeStruct((8, D_SSD), F32)],
        input_output_aliases={0: 0},
        scratch_shapes=[pltpu.VMEM((PAIRS, STATE, LANES), F32), pltpu.VMEM((q, D_SSD), F32),
                        pltpu.VMEM((q, D_SSD), F32), pltpu.VMEM((q, D_SSD), F32), pltpu.VMEM((LANES, q), F32),
                        pltpu.VMEM((1, D_SSD), F32), pltpu.VMEM((1, D_SSD), F32), pltpu.VMEM((1, LANES), F32),
                        pltpu.VMEM((1, LANES), F32)],
        compiler_params=_cparams(("arbitrary", "arbitrary")),
        name="ssd_bwd",
    )(dproj, duy, proj, ys, xs_all, xs_all, xs_all, dtr, states, dt_bias, a_log, d_lanes, norm_w)


HBM_SPEC = pl.BlockSpec(memory_space=pltpu.HBM)
MESH_ID = pl.DeviceIdType.MESH


def _coords():
    return lax.axis_index("x"), lax.axis_index("y"), lax.axis_index("c")


def _chip_peer(xi, yi, ci, d):
    return (jnp.bitwise_xor(xi, d >> 1), jnp.bitwise_xor(yi, d & 1), ci)


def _remote(src, dst, send_sem, recv_sem, peer):
    return pltpu.make_async_remote_copy(src_ref=src, dst_ref=dst, send_sem=send_sem, recv_sem=recv_sem,
                                        device_id=peer, device_id_type=MESH_ID)


SEM_SPEC = pl.BlockSpec(memory_space=pltpu.SEMAPHORE)
ANY_SPEC = pl.BlockSpec(memory_space=pl.ANY)
EFFECT = pltpu.SideEffectType.DATAFLOW_SIDE_EFFECTING
COPIES = 3


def _half(ref, axis, which, lead=0):
    size = ref.shape[lead + axis] // 2
    part = pl.ds(which * size, size)
    idx = (slice(None),) * lead + ((part, slice(None)) if axis == 0 else (slice(None), part))
    return ref.at[idx]


def _halved_shape(shape, axis):
    lead = len(shape) - 2
    return tuple(d // 2 if i == lead + axis else d for i, d in enumerate(shape))


def _gather_plan(axis):
    def plan(xi, yi, ci, src, land):
        me = 2 * xi + yi
        out = []
        for d in (1, 2, 3):
            there = jnp.bitwise_xor(me, d)
            if axis is None:
                out.append((src, land.at[me], _chip_peer(xi, yi, ci, d), land.at[there]))
            else:
                out.append((_half(src, axis, ci), _half(land.at[me], axis, ci), _chip_peer(xi, yi, ci, d),
                            _half(land.at[there], axis, ci)))
        return out
    return plan


def _owners_plan(xi, yi, ci, src, land):
    me = 2 * xi + yi
    return [(src.at[jnp.bitwise_xor(me, d)], land.at[d - 1], _chip_peer(xi, yi, ci, d), land.at[d - 1])
            for d in (1, 2, 3)]


def _split_start(srcs, lands, plans, *, name):
    n = len(srcs)

    def body(*refs):
        src_refs, land_refs = refs[:n], refs[n:2 * n]
        ssems, rsems = refs[2 * n:3 * n], refs[3 * n:4 * n]
        token = refs[-1]
        xi, yi, ci = _coords()
        for t in range(n):
            for k, (src, dst, peer, _) in enumerate(plans[t](xi, yi, ci, src_refs[t], land_refs[t])):
                _remote(src, dst, ssems[t].at[k], rsems[t].at[k], peer).start()
        token[...] = jnp.zeros_like(token)

    bufs = list(srcs) + list(lands)
    outs = pl.pallas_call(
        body,
        name=name,
        in_specs=[HBM_SPEC] * (2 * n),
        out_specs=[SEM_SPEC] * (2 * n) + [HBM_SPEC] * (2 * n) + [pl.BlockSpec(memory_space=pltpu.VMEM)],
        out_shape=[pltpu.SemaphoreType.DMA((COPIES,))] * (2 * n) + [pltpu.HBM(a.shape, a.dtype) for a in bufs]
        + [jax.ShapeDtypeStruct((8, LANES), F32)],
        input_output_aliases={i: 2 * n + i for i in range(2 * n)},
        compiler_params=pltpu.CompilerParams(has_side_effects=EFFECT),
    )(*[pltpu.with_memory_space_constraint(a, pltpu.HBM) for a in bufs])
    return outs[:n], outs[n:2 * n], outs[2 * n:3 * n], outs[3 * n:4 * n], outs[-1]


def _split_wait(ssems, rsems, srcs, lands, plans, after, *, name):
    n = len(srcs)

    def body(*refs):
        src_refs, land_refs = refs[:n], refs[n:2 * n]
        ss, rs = refs[2 * n:3 * n], refs[3 * n:4 * n]
        xi, yi, ci = _coords()
        for t in range(n):
            for k, (src, _, peer, landed) in enumerate(plans[t](xi, yi, ci, src_refs[t], land_refs[t])):
                cp = _remote(src, landed, ss[t].at[k], rs[t].at[k], peer)
                cp.wait_send()
                cp.wait_recv()

    bufs = list(srcs) + list(lands)
    outs = pl.pallas_call(
        body,
        name=name,
        in_specs=[HBM_SPEC] * (2 * n) + [SEM_SPEC] * (2 * n) + [ANY_SPEC],
        out_specs=[HBM_SPEC] * (2 * n),
        out_shape=[pltpu.HBM(a.shape, a.dtype) for a in bufs],
        input_output_aliases={i: i for i in range(2 * n)},
        compiler_params=pltpu.CompilerParams(has_side_effects=EFFECT),
    )(*bufs, *ssems, *rsems, after)
    return outs[:n], outs[n:]


def _forward_halves(lands, axes, *, name):
    n = len(lands)

    def body(*refs):
        ins, outs = refs[:n], refs[n:2 * n]
        send_sems, recv_sems = refs[2 * n:]
        xi, yi, ci = _coords()
        me = 2 * xi + yi
        sibling = (xi, yi, 1 - ci)
        cps = []
        for t in range(n):
            for d in (1, 2, 3):
                slot = jnp.bitwise_xor(me, d)
                k = COPIES * t + d - 1
                cp = _remote(_half(ins[t].at[slot], axes[t], ci), _half(outs[t].at[slot], axes[t], ci),
                             send_sems.at[k], recv_sems.at[k], sibling)
                cp.start()
                cps.append(cp)
        for t in range(n):
            for d in (1, 2, 3):
                got = _half(outs[t].at[jnp.bitwise_xor(me, d)], axes[t], 1 - ci)
                k = COPIES * t + d - 1
                _remote(got, got, send_sems.at[k], recv_sems.at[k], sibling).wait_recv()
        for cp in cps:
            cp.wait_send()

    return pl.pallas_call(
        body,
        name=name,
        in_specs=[HBM_SPEC] * n,
        out_specs=[HBM_SPEC] * n,
        out_shape=[jax.ShapeDtypeStruct(a.shape, a.dtype) for a in lands],
        input_output_aliases={i: i for i in range(n)},
        scratch_shapes=[pltpu.SemaphoreType.DMA((COPIES * n,)), pltpu.SemaphoreType.DMA((COPIES * n,))],
    )(*lands)


def _swap_other_halves(gs, axes, *, name):
    n = len(gs)

    def body(*refs):
        ins, lands = refs[:n], refs[n:2 * n]
        send_sems, recv_sems = refs[2 * n:]
        xi, yi, ci = _coords()
        sibling = (xi, yi, 1 - ci)
        cps = []
        for t in range(n):
            cp = _remote(_half(ins[t], axes[t], 1 - ci, lead=1), lands[t], send_sems.at[t], recv_sems.at[t], sibling)
            cp.start()
            cps.append(cp)
        for cp in cps:
            cp.wait_recv()
        for cp in cps:
            cp.wait_send()

    return pl.pallas_call(
        body,
        in_specs=[HBM_SPEC] * n,
        out_specs=[HBM_SPEC] * n,
        out_shape=[jax.ShapeDtypeStruct(_halved_shape(g.shape, ax), g.dtype) for g, ax in zip(gs, axes)],
        scratch_shapes=[pltpu.SemaphoreType.DMA((n,)), pltpu.SemaphoreType.DMA((n,))],
        name=name,
    )(*gs)


def _row_tile(rows, cap=512, mult=16):
    best = mult
    for cand in range(mult, min(rows, cap) + 1, mult):
        if rows % cand == 0:
            best = cand
    assert rows % best == 0, rows
    return best


COL_TILE = 256


def _half_tiles(hr, hc, axis, cap=512, mult=16):
    if axis == 0:
        tr = _row_tile(hr, cap, mult)
        n = hr // tr
        return (tr, hc), n, lambda half, i: (half * n + i, 0)
    n = hc // COL_TILE
    return (hr, COL_TILE), n, lambda half, i: (0, half * n + i)


def _add_core_halves(g, land, where, axis):
    nslot, hr, hc = land.shape
    bshape, nr, idx = _half_tiles(hr, hc, axis)

    def body(where_ref, g_ref, l_ref, f_ref, b_ref):
        s = g_ref[...] + l_ref[...].astype(F32)
        b_ref[...] = s.astype(BF16)

        @pl.when(pl.program_id(1) == where_ref[1])
        def _():
            f_ref[...] = s

    blk = pl.BlockSpec((None,) + bshape, lambda i, s, w: (s,) + idx(0, i))
    mine = pl.BlockSpec((None,) + bshape, lambda i, s, w: (s,) + idx(w[0], i))
    return pl.pallas_call(
        body,
        grid_spec=pltpu.PrefetchScalarGridSpec(
            num_scalar_prefetch=1,
            grid=(nr, nslot),
            in_specs=[mine, blk],
            out_specs=[pl.BlockSpec(bshape, lambda i, s, w: idx(0, i)), blk],
        ),
        out_shape=[jax.ShapeDtypeStruct((hr, hc), F32), jax.ShapeDtypeStruct(land.shape, BF16)],
        compiler_params=_cparams(("parallel", "arbitrary")),
        name="add_core_halves",
    )(where, g, land)


def _add_chip_sums(pf, land, where, axis):
    hr, cols = pf.shape
    bshape, nr, idx = _half_tiles(hr, cols, axis)

    def body(where_ref, p_ref, l_ref, o_ref):
        acc = p_ref[...]
        for d in range(3):
            acc = acc + l_ref[d].astype(F32)
        o_ref[...] = acc

    return pl.pallas_call(
        body,
        grid_spec=pltpu.PrefetchScalarGridSpec(
            num_scalar_prefetch=1,
            grid=(nr,),
            in_specs=[pl.BlockSpec(bshape, lambda i, w: idx(0, i)),
                      pl.BlockSpec((3,) + bshape, lambda i, w: (0,) + idx(0, i))],
            out_specs=pl.BlockSpec(bshape, lambda i, w: idx(0, i)),
        ),
        out_shape=jax.ShapeDtypeStruct((hr, cols), F32),
        compiler_params=_cparams(("parallel",)),
        name="add_chip_sums",
    )(where, pf, land)


def _swap_reduced_halves(rs):
    n = len(rs)

    def body(*refs):
        ins, outs = refs[:n], refs[n:2 * n]
        send_sems, recv_sems = refs[2 * n:]
        xi, yi, ci = _coords()
        sibling = (xi, yi, 1 - ci)
        cps = [_remote(ins[t], outs[t], send_sems.at[t], recv_sems.at[t], sibling) for t in range(n)]
        for cp in cps:
            cp.start()
        for cp in cps:
            cp.wait_recv()
        for cp in cps:
            cp.wait_send()

    return pl.pallas_call(
        body,
        in_specs=[HBM_SPEC] * n,
        out_specs=[HBM_SPEC] * n,
        out_shape=[jax.ShapeDtypeStruct(r.shape, r.dtype) for r in rs],
        scratch_shapes=[pltpu.SemaphoreType.DMA((n,)), pltpu.SemaphoreType.DMA((n,))],
        name="swap_reduced_halves",
    )(*rs)


N_DEV = 8


def _all_reduce_small(part):
    r, w = part.shape

    def body(p_ref, o_ref, gath, send_sems, recv_sems):
        xi, yi, ci = _coords()
        me = 4 * xi + 2 * yi + ci
        gath[me] = p_ref[...]
        cps = []
        for d in range(1, N_DEV):
            peer = (jnp.bitwise_xor(xi, d >> 2), jnp.bitwise_xor(yi, (d >> 1) & 1), jnp.bitwise_xor(ci, d & 1))
            cp = _remote(p_ref, gath.at[me], send_sems.at[d - 1], recv_sems.at[d - 1], peer)
            cp.start()
            cps.append(cp)
        for d in range(1, N_DEV):
            src = gath.at[jnp.bitwise_xor(me, d)]
            _remote(src, src, send_sems.at[d - 1], recv_sems.at[d - 1], (xi, yi, ci)).wait_recv()
        acc = gath[0]
        for k in range(1, N_DEV):
            acc = acc + gath[k]
        o_ref[...] = acc
        for cp in cps:
            cp.wait_send()

    vm = pl.BlockSpec(memory_space=pltpu.VMEM)
    return pl.pallas_call(
        body,
        in_specs=[vm],
        out_specs=vm,
        out_shape=jax.ShapeDtypeStruct((r, w), F32),
        scratch_shapes=[pltpu.VMEM((N_DEV, r, w), F32), pltpu.SemaphoreType.DMA((N_DEV - 1,)),
                        pltpu.SemaphoreType.DMA((N_DEV - 1,))],
        name="all_reduce_small",
    )(part)


def _adamw_math(wv, gv, mv, vv):
    mn = ADAM_B1 * mv + (1.0 - ADAM_B1) * gv
    vn = ADAM_B2 * vv + (1.0 - ADAM_B2) * (gv * gv)
    m_hat = mn / (1.0 - ADAM_B1 ** ADAM_STEP)
    v_hat = vn / (1.0 - ADAM_B2 ** ADAM_STEP)
    return -ADAM_LR * (m_hat / (jnp.sqrt(v_hat) + ADAM_EPS) + ADAM_WD * wv), mn, vn


def _adamw_halves(w, g_mine, g_other, m, v, where, axis, *, name):
    rows, cols = w.shape
    hr, hc = g_mine.shape
    bshape, nr, idx = _half_tiles(hr, hc, axis, cap=256, mult=8)

    def body(where_ref, w_ref, gm_ref, go_ref, m_ref, v_ref, g_ref, d_ref, nm_ref, nv_ref):
        is_mine = pl.program_id(0) == where_ref[0]
        gv = jnp.where(is_mine, gm_ref[...], go_ref[...])
        g_ref[...] = gv
        d_ref[...], nm_ref[...], nv_ref[...] = _adamw_math(w_ref[...], gv, m_ref[...], v_ref[...])

    def parked(half, i, holder):
        return idx(0, jnp.where(half == holder, i, jnp.where(half < holder, 0, nr - 1)))

    blk = pl.BlockSpec(bshape, lambda hf, i, wh: idx(hf, i))
    o = jax.ShapeDtypeStruct((rows, cols), F32)
    return pl.pallas_call(
        body,
        grid_spec=pltpu.PrefetchScalarGridSpec(
            num_scalar_prefetch=1,
            grid=(2, nr),
            in_specs=[blk, pl.BlockSpec(bshape, lambda hf, i, wh: parked(hf, i, wh[0])),
                      pl.BlockSpec(bshape, lambda hf, i, wh: parked(hf, i, 1 - wh[0])), blk, blk],
            out_specs=[blk] * 4,
        ),
        out_shape=[o, o, o, o],
        compiler_params=_cparams(("arbitrary", "arbitrary")),
        name=name,
    )(where, w, g_mine, g_other, m, v)


def _adamw(w, g, m, v, *, name):
    rows, cols = w.shape
    tr = _row_tile(rows, cap=256, mult=8)

    def body(w_ref, g_ref, m_ref, v_ref, d_ref, nm_ref, nv_ref):
        d_ref[...], nm_ref[...], nv_ref[...] = _adamw_math(w_ref[...], g_ref[...], m_ref[...], v_ref[...])

    blk = pl.BlockSpec((tr, cols), lambda i: (i, 0))
    o = jax.ShapeDtypeStruct((rows, cols), F32)
    return pl.pallas_call(
        body,
        grid=(rows // tr,),
        in_specs=[blk] * 4,
        out_specs=[blk] * 3,
        out_shape=[o, o, o],
        compiler_params=_cparams(("parallel",)),
        name=name,
    )(w, g, m, v)


def _pack(arrs):
    flat = jnp.concatenate([a.reshape(-1) for a in arrs])
    pad = (-flat.shape[0]) % (8 * LANES)
    return jnp.pad(flat, (0, pad)).reshape(-1, LANES)


def _unpack(packed, shapes):
    flat = packed.reshape(-1)
    out, off = [], 0
    for s in shapes:
        n = 1
        for dim in s:
            n *= dim
        out.append(flat[off:off + n].reshape(s))
        off += n
    return out


def _pad_rows(a, rows):
    return jnp.pad(a, ((0, rows - a.shape[0]), (0, 0)))


def _pad_lanes(a, lanes=LANES):
    return jnp.pad(a, ((0, 0), (0, lanes - a.shape[1])))


def _local_grads(x2d, tgt2d, prm, get_w, on_grads, *, nb, seq, dep=None):
    g_pre, g_post, g_fpre, g_fpost = prm["norm_mix_pre"], prm["norm_mix_post"], prm["norm_ffn_pre"], prm["norm_ffn_post"]
    dt_bias, a_log = _pad_lanes(prm["ssd_dt_bias"]), _pad_lanes(prm["ssd_a_log"])
    d_lanes = jnp.repeat(prm["ssd_d"], HEAD_DIM, axis=1)

    h = _rms_fwd(x2d, g_pre, dep=dep, out_dtype=BF16, name="rms_mix_pre")
    t, d = x2d.shape
    w_in_t, w_dt_t, cw, sw = get_w("in", h)
    proj = _matmul([(h, w_in_t)], mode="nt", out_dtype=F32, tm=1024, tn=1024, tk=2048, name="mm_proj",
                   extent=(t, D_MAIN, d))
    dtr = _matmul([(h, w_dt_t)], mode="nt", out_dtype=F32, tm=1024, tn=128, tk=2048, name="mm_dt")
    u1, uy = _conv_branch_fwd(proj, cw, prm["conv_dw_b"], prm["conv_ln_g"], prm["conv_ln_b"], nb=nb, seq=seq)
    xs_all = _ssd_pre_fwd(proj, sw, prm["ssd_conv_b"], nb=nb, seq=seq)
    uy, ys, states = _ssd_fwd(uy, xs_all, proj, dtr, dt_bias, a_log, d_lanes, prm["ssd_norm_w"], nb=nb, seq=seq)
    w_out = get_w("out", uy)
    mix = _matmul([(uy, w_out)], mode="nn", out_dtype=F32, tm=1024, tn=1024, tk=2048, name="mm_mix")
    x1 = _rms_fwd(mix, g_post, res=x2d, out_dtype=F32, name="rms_mix_post")
    h2 = _rms_fwd(x1, g_fpre, out_dtype=BF16, name="rms_ffn_pre")
    w_gate, w_up = get_w("up", h2)
    gt, up, act = _ffn_up(h2, w_gate, w_up, tm=1024, tn=512)
    w_down = get_w("down", act)
    f = _matmul([(act, w_down)], mode="nn", out_dtype=F32, tm=1024, tn=1024, tk=2816, name="mm_down")
    dx2, df, loss, dg_fpost = _ffn_post_loss(f, x1, tgt2d, g_fpost)

    dgt, dup = _ffn_bwd_act(df, w_down, gt, up, tm=1024, tn=512)
    dw_down = _matmul([(act, df)], mode="tn", out_dtype=F32, tm=1408, tn=1024, tk=2048, name="mm_dw_down",
                      also_bf16=True)
    dw_gate = _matmul([(h2, dgt)], mode="tn", out_dtype=F32, tm=1024, tn=1408, tk=2048, name="mm_dw_gate",
                      slot_out=True, also_bf16=True)
    dw_up = _matmul([(h2, dup)], mode="tn", out_dtype=F32, tm=1024, tn=1408, tk=2048, name="mm_dw_up",
                    slot_out=True, also_bf16=True)
    dep = on_grads("ffn", (dw_down, dw_gate, dw_up))
    dh2 = _matmul([(dgt, w_gate), (dup, w_up)], mode="nt", out_dtype=F32, tm=1024, tn=1024, tk=1408, name="mm_dh2",
                  dep=dep)
    dx1, dg_fpre = _rms_bwd([dh2], x1, g_fpre, addend=dx2, out_dtype=F32, name="rms_ffn_pre_bwd")
    dmix, dg_post = _rms_bwd([dx1], mix, g_post, out_dtype=BF16, name="rms_mix_post_bwd")
    dw_out = _matmul([(uy, dmix)], mode="tn", out_dtype=F32, tm=1024, tn=1024, tk=2048, name="mm_dw_out",
                     also_bf16=True)
    dep = on_grads("out", (dw_out,))
    duy = _matmul([(dmix, w_out)], mode="nt", out_dtype=F32, tm=1024, tn=1024, tk=2048, name="mm_duy", dep=dep)
    dproj, dcw, dcb, dlg, dlb = _conv_branch_bwd(duy, u1, proj, cw, prm["conv_ln_g"], prm["conv_ln_b"], nb=nb, seq=seq)
    dproj, dxs, ddtr, ssd_small = _ssd_bwd(dproj, duy, proj, ys, xs_all, dtr, states, dt_bias, a_log, d_lanes,
                                           prm["ssd_norm_w"], nb=nb, seq=seq)
    dproj, dsw, dsb = _ssd_pre_bwd(dproj, dxs, proj, sw, prm["ssd_conv_b"], nb=nb, seq=seq)
    ddtr_b = ddtr.astype(BF16)
    dw_in_t = _matmul([(dproj, h)], mode="tn", out_dtype=F32, tm=1024, tn=1024, tk=2048, name="mm_dw_main",
                      extent=(D_MAIN, d, t), out_rows=D_IN, also_bf16=True)
    dw_in_t = _dw_dt_rows(*dw_in_t, ddtr_b, h)
    dep = on_grads("in", (dw_in_t,))
    dh_main = _matmul([(dproj, w_in_t)], mode="nn", out_dtype=F32, tm=1024, tn=1024, tk=2560, name="mm_dh_main",
                      extent=(t, d, D_MAIN), dep=dep)
    dh_dt = _matmul([(ddtr_b, w_dt_t)], mode="nn", out_dtype=F32, tm=1024, tn=1024, tk=128, name="mm_dh_dt")
    dx, dg_pre = _rms_bwd([dh_main, dh_dt], x2d, g_pre, addend=dx1, out_dtype=F32, name="rms_mix_pre_bwd")

    grads = {
        "norm_mix_pre": dg_pre,
        "w_in": dw_in_t[0],
        "conv_dw_w": dcw[:CONV_K], "conv_dw_b": dcb, "conv_ln_g": dlg, "conv_ln_b": dlb,
        "ssd_conv_w": dsw[:SSD_CONV_K], "ssd_conv_b": dsb,
        "ssd_dt_bias": ssd_small[2:3, :HEADS], "ssd_a_log": ssd_small[3:4, :HEADS], "ssd_d": ssd_small[1:2, :HEADS],
        "ssd_norm_w": ssd_small[0:1],
        "w_out": dw_out[0],
        "norm_mix_post": dg_post, "norm_ffn_pre": dg_fpre,
        "w_gate": dw_gate[0], "w_up": dw_up[0],
        "w_down": dw_down[0], "norm_ffn_post": dg_fpost,
    }
    return loss, dx, grads


BIG = ("w_in", "w_out", "w_gate", "w_up", "w_down")
HALF_AXIS = {"w_in": 1, "w_out": 0, "w_gate": 0, "w_up": 0, "w_down": 0}
GATHER_STAGES = {"in": ("w_in", "conv_dw_w", "ssd_conv_w"), "out": ("w_out",), "up": ("w_gate", "w_up"),
                 "down": ("w_down",)}
GATHER_ORDER = tuple(n for st in ("in", "out", "up", "down") for n in GATHER_STAGES[st])
SMALL = ("norm_mix_pre", "conv_dw_w", "conv_dw_b", "conv_ln_g", "conv_ln_b", "ssd_conv_w", "ssd_conv_b", "ssd_dt_bias",
         "ssd_a_log", "ssd_d", "ssd_norm_w", "norm_mix_post", "norm_ffn_pre", "norm_ffn_post")
WEIGHTS = ("norm_mix_pre", "w_in", "conv_dw_w", "conv_dw_b", "conv_ln_g", "conv_ln_b", "ssd_conv_w", "ssd_conv_b",
           "ssd_dt_bias", "ssd_a_log", "ssd_d", "ssd_norm_w", "w_out", "norm_mix_post", "norm_ffn_pre", "w_gate", "w_up",
           "w_down", "norm_ffn_post")


def _cols_from_slots(a):
    n, rows, w = a.shape
    return a.transpose(1, 0, 2).reshape(rows, n * w)


def kernel(x, norm_mix_pre, w_in, conv_dw_w, conv_dw_b, conv_ln_g, conv_ln_b, ssd_conv_w, ssd_conv_b, ssd_dt_bias, ssd_a_log, ssd_d, ssd_norm_w, w_out, norm_mix_post, norm_ffn_pre, w_gate, w_up, w_down, norm_ffn_post, loss_target, m_norm_mix_pre, m_w_in, m_conv_dw_w, m_conv_dw_b, m_conv_ln_g, m_conv_ln_b, m_ssd_conv_w, m_ssd_conv_b, m_ssd_dt_bias, m_ssd_a_log, m_ssd_d, m_ssd_norm_w, m_w_out, m_norm_mix_post, m_norm_ffn_pre, m_w_gate, m_w_up, m_w_down, m_norm_ffn_post, v_norm_mix_pre, v_w_in, v_conv_dw_w, v_conv_dw_b, v_conv_ln_g, v_conv_ln_b, v_ssd_conv_w, v_ssd_conv_b, v_ssd_dt_bias, v_ssd_a_log, v_ssd_d, v_ssd_norm_w, v_w_out, v_norm_mix_post, v_norm_ffn_pre, v_w_gate, v_w_up, v_w_down, v_norm_ffn_post):
    args = dict(locals())
    two_d = lambda n, a: jnp.swapaxes(a, 1, 2)[0] if n == "w_in" else a.reshape(a.shape[-2:])
    wts = {n: two_d(n, args[n]) for n in WEIGHTS}
    ms = {n: two_d(n, args["m_" + n]) for n in WEIGHTS}
    vs = {n: two_d(n, args["v_" + n]) for n in WEIGHTS}
    nb, seq, d = x.shape
    t = nb * seq
    xi, yi, ci = _coords()
    chip = 2 * xi + yi
    where = jnp.stack([ci, chip]).astype(jnp.int32)

    shards = {n: wts[n].astype(BF16) for n in BIG}
    shards.update(conv_dw_w=_pad_rows(wts["conv_dw_w"], 32), ssd_conv_w=_pad_rows(wts["ssd_conv_w"], 8))
    srcs = [shards[n] for n in GATHER_ORDER]
    plans = [_gather_plan(HALF_AXIS.get(n)) for n in GATHER_ORDER]
    ssems, rsems, srcs, lands, token = _split_start(
        srcs, [lax.empty((N_CHIPS,) + s.shape, s.dtype) for s in srcs], plans, name="gather_start")

    def get_w(stage, after):
        names = GATHER_STAGES[stage]
        pick = lambda seq_: [seq_[GATHER_ORDER.index(n)] for n in names]
        own, got = _split_wait(pick(ssems), pick(rsems), pick(srcs), pick(lands), pick(plans), after,
                               name="gather_wait_" + stage)
        got, own = dict(zip(names, got)), dict(zip(names, own))
        big = [n for n in names if n in BIG]
        got.update(zip(big, _forward_halves([got[n] for n in big], [HALF_AXIS[n] for n in big],
                                            name="gather_forward_" + stage)))
        full = {n: lax.dynamic_update_slice(got[n], own[n][None], (chip, 0, 0)) for n in got}
        if stage == "in":
            w_in_t = full["w_in"].reshape(D_IN, D_MODEL)
            return (w_in_t, _pad_rows(w_in_t[D_MAIN:], LANES), _cols_from_slots(full["conv_dw_w"]),
                    _cols_from_slots(full["ssd_conv_w"]))
        if stage == "out":
            return full["w_out"].reshape(D_MODEL, D_MODEL)
        if stage == "up":
            return _cols_from_slots(full["w_gate"]), _cols_from_slots(full["w_up"])
        return full["w_down"].reshape(D_FF, D_MODEL)

    reduce_groups = {"ffn": ("w_down", "w_gate", "w_up"), "out": ("w_out",), "in": ("w_in",)}
    in_flight = {}

    def on_grads(stage, gs):
        names = reduce_groups[stage]
        axes = [HALF_AXIS[n] for n in names]
        slot = lambda g: g if g.ndim == 3 else g.reshape((N_CHIPS, g.shape[0] // N_CHIPS, g.shape[1]))
        kept = _swap_other_halves([slot(b16) for _, b16 in gs], axes, name="swap_other_halves_" + stage)
        sums = [_add_core_halves(slot(f32), l, where, ax) for (f32, _), l, ax in zip(gs, kept, axes)]
        ps = [s[1] for s in sums]
        ssem, rsem, ps, recv, started = _split_start(
            ps, [lax.empty((COPIES,) + p.shape[1:], p.dtype) for p in ps], [_owners_plan] * len(ps),
            name="owners_start_" + stage)
        in_flight[stage] = (ssem, rsem, ps, recv, [s[0] for s in sums])
        return started

    prm = {n: wts[n] for n in SMALL}
    loss, dx, grads = _local_grads(x.reshape(t, d), loss_target.reshape(t, d), prm, get_w, on_grads,
                                   nb=nb, seq=seq, dep=token)
    loss = lax.psum(loss[0, 0], MESH_AXES)

    halves = {}
    for stage, names in reduce_groups.items():
        ssem, rsem, ps, recv, own_sums = in_flight[stage]
        _, recv = _split_wait(ssem, rsem, ps, recv, [_owners_plan] * len(ps), dx, name="owners_wait_" + stage)
        for n, f32_sum, r in zip(names, own_sums, recv):
            halves[n] = _add_chip_sums(f32_sum, r, where, HALF_AXIS[n])
    halves = [halves[n] for n in BIG]
    other_halves = _swap_reduced_halves(halves)

    small_shapes = [grads[n].shape for n in SMALL]
    small_sum = _unpack(_all_reduce_small(_pack([grads[n] for n in SMALL])), small_shapes)
    small_grads = dict(zip(SMALL, small_sum))
    cwid, swid = D_CONV // N_CHIPS, D_XBC // N_CHIPS
    small_grads["conv_dw_w"] = lax.dynamic_slice(small_grads["conv_dw_w"], (0, chip * cwid), (CONV_K, cwid))
    small_grads["ssd_conv_w"] = lax.dynamic_slice(small_grads["ssd_conv_w"], (0, chip * swid), (SSD_CONV_K, swid))

    out_g, out_d, out_m, out_v = {}, {}, {}, {}
    for n, mine, other in zip(BIG, halves, other_halves):
        out_g[n], out_d[n], out_m[n], out_v[n] = _adamw_halves(wts[n], mine, other, ms[n], vs[n], where,
                                                                 HALF_AXIS[n], name="adamw_" + n)
    shard_shapes = [wts[n].shape for n in SMALL]
    pd, pm, pv = _adamw(_pack([wts[n] for n in SMALL]), _pack([small_grads[n] for n in SMALL]),
                        _pack([ms[n] for n in SMALL]), _pack([vs[n] for n in SMALL]), name="adamw_small")
    for n, dd, mm, vv in zip(SMALL, _unpack(pd, shard_shapes), _unpack(pm, shard_shapes), _unpack(pv, shard_shapes)):
        out_g[n], out_d[n], out_m[n], out_v[n] = small_grads[n], dd, mm, vv

    back = lambda n, a: jnp.swapaxes(a[None], 1, 2) if n == "w_in" else a.reshape(args[n].shape)
    outs = [back(n, o[n]) for o in (out_g, out_d, out_m, out_v) for n in WEIGHTS]
    return (loss, dx.reshape(nb, seq, d), *outs)
```

```python
import functools

import jax
import jax.numpy as jnp
from jax import lax
from jax.experimental import pallas as pl
from jax.experimental.pallas import tpu as pltpu

F32 = jnp.float32
BF16 = jnp.bfloat16
EPS = 1e-6

D_MODEL = 2048
D_CONV = 1024
D_SSD = 1024
D_XBC = 2048
HEADS = 16
HEAD_DIM = 64
GROUPS = 4
STATE = 128
CONV_K = 31
SSD_CONV_K = 4
D_FF = 5632
D_MAIN = 2 * D_CONV + D_SSD + D_XBC
D_IN = D_MAIN + HEADS
N_CHIPS = 4
LANES = 128
CHUNK = 128
PAIRS = HEADS // 2

ADAM_LR = 0.001
ADAM_B1 = 0.9
ADAM_B2 = 0.999
ADAM_EPS = 1e-08
ADAM_WD = 0.01
ADAM_STEP = 10

MESH_AXES = ("x", "y", "c")
VMEM_LIMIT = 56 * 1024 * 1024


def _sig(v):
    return 1.0 / (1.0 + jnp.exp(-v))


def _cparams(sem, vmem=VMEM_LIMIT):
    return pltpu.CompilerParams(dimension_semantics=sem, vmem_limit_bytes=vmem)


_DIMS = {"nn": ((1,), (0,)), "nt": ((1,), (1,)), "tn": ((0,), (0,))}


def _matmul(pairs, *, mode, out_dtype, tm, tn, tk, name, slot_out=False, extent=None, out_rows=None, dep=None,
            also_bf16=False):
    a0, b0 = pairs[0]
    if mode == "nn":
        (m, k), n = a0.shape, b0.shape[1]
    elif mode == "nt":
        (m, k), n = a0.shape, b0.shape[0]
    else:
        (k, m), n = a0.shape, b0.shape[1]
    if extent is not None:
        m, n, k = extent
    tm, tn, tk = min(tm, m), min(tn, n), min(tk, k)
    assert m % tm == 0 and n % tn == 0 and k % tk == 0, (name, m, n, k, tm, tn, tk)
    nk = k // tk
    npairs = len(pairs)
    deps = [] if dep is None else [dep]
    dims = (_DIMS[mode], ((), ()))

    use_scratch = nk > 1 and out_dtype != F32

    def body(*refs):
        ins, o_ref = refs[: 2 * npairs], refs[2 * npairs + len(deps)]
        dot = lambda p: lax.dot_general(ins[2 * p][...], ins[2 * p + 1][...], dims, preferred_element_type=F32)
        if nk == 1:
            part = dot(0)
            for p in range(1, npairs):
                part = part + dot(p)
            o_ref[...] = part.astype(out_dtype)
            if also_bf16:
                refs[2 * npairs + len(deps) + 1][...] = part.astype(BF16)
            return
        acc = refs[-1] if use_scratch else o_ref
        kk = pl.program_id(2)

        @pl.when(kk == 0)
        def _():
            acc[...] = jnp.zeros_like(acc)

        for p in range(npairs):
            acc[...] += dot(p)

        if use_scratch:
            @pl.when(kk == nk - 1)
            def _():
                o_ref[...] = acc[...].astype(out_dtype)

        if also_bf16:
            @pl.when(kk == nk - 1)
            def _():
                refs[2 * npairs + len(deps) + 1][...] = acc[...].astype(BF16)

    if mode == "nn":
        a_spec = pl.BlockSpec((tm, tk), lambda i, j, kk: (i, kk))
        b_spec = pl.BlockSpec((tk, tn), lambda i, j, kk: (kk, j))
    elif mode == "nt":
        a_spec = pl.BlockSpec((tm, tk), lambda i, j, kk: (i, kk))
        b_spec = pl.BlockSpec((tn, tk), lambda i, j, kk: (j, kk))
    else:
        a_spec = pl.BlockSpec((tk, tm), lambda i, j, kk: (kk, i))
        b_spec = pl.BlockSpec((tk, tn), lambda i, j, kk: (kk, j))
    if slot_out:
        out_shape = jax.ShapeDtypeStruct((n // tn, m, tn), out_dtype)
        out_spec = pl.BlockSpec((None, tm, tn), lambda i, j, kk: (j, i, 0))
    else:
        out_shape = jax.ShapeDtypeStruct((m if out_rows is None else out_rows, n), out_dtype)
        out_spec = pl.BlockSpec((tm, tn), lambda i, j, kk: (i, j))
    flat = [t for ab in pairs for t in ab]
    if also_bf16:
        out_spec = [out_spec, out_spec]
        out_shape = [out_shape, jax.ShapeDtypeStruct(out_shape.shape, BF16)]
    return pl.pallas_call(
        body,
        grid=(m // tm, n // tn, nk),
        in_specs=[a_spec, b_spec] * npairs + [pl.BlockSpec(memory_space=pl.ANY)] * len(deps),
        out_specs=out_spec,
        out_shape=out_shape,
        scratch_shapes=[pltpu.VMEM((tm, tn), F32)] if use_scratch else [],
        compiler_params=_cparams(("parallel", "parallel", "arbitrary")),
        name=name,
    )(*flat, *deps)


SUB_ROWS = 256


def _ffn_up(h2, wg, wu, *, tm, tn):
    t, k = h2.shape
    n = wg.shape[1]
    tm = min(tm, t)
    assert t % tm == 0 and n % tn == 0, (t, n, tm, tn)

    sub = min(SUB_ROWS, tm)

    def body(h_ref, wg_ref, wu_ref, g_ref, u_ref, a_ref):
        for r in range(tm // sub):
            rows = pl.ds(r * sub, sub)
            hv = h_ref[rows, :]
            g = jnp.dot(hv, wg_ref[...], preferred_element_type=F32)
            u = jnp.dot(hv, wu_ref[...], preferred_element_type=F32)
            g_ref[rows, :] = g.astype(BF16)
            u_ref[rows, :] = u.astype(BF16)
            a_ref[rows, :] = (g * _sig(g) * u).astype(BF16)

    o = jax.ShapeDtypeStruct((t, n), BF16)
    ospec = pl.BlockSpec((tm, tn), lambda i, j: (i, j))
    return pl.pallas_call(
        body,
        grid=(t // tm, n // tn),
        in_specs=[pl.BlockSpec((tm, k), lambda i, j: (i, 0)), pl.BlockSpec((k, tn), lambda i, j: (0, j)),
                  pl.BlockSpec((k, tn), lambda i, j: (0, j))],
        out_specs=[ospec, ospec, ospec],
        out_shape=[o, o, o],
        compiler_params=_cparams(("parallel", "parallel")),
        name="ffn_up",
    )(h2, wg, wu)


def _ffn_bwd_act(df, wd, gt, up, *, tm, tn):
    t, k = df.shape
    n = wd.shape[0]
    tm = min(tm, t)
    assert t % tm == 0 and n % tn == 0, (t, n, tm, tn)

    sub = min(SUB_ROWS, tm)

    def body(df_ref, wd_ref, g_ref, u_ref, dg_ref, du_ref):
        for r in range(tm // sub):
            rows = pl.ds(r * sub, sub)
            da = lax.dot_general(df_ref[rows, :], wd_ref[...], (_DIMS["nt"], ((), ())), preferred_element_type=F32)
            g = g_ref[rows, :].astype(F32)
            u = u_ref[rows, :].astype(F32)
            s = _sig(g)
            dg_ref[rows, :] = (da * u * s * (1.0 + g * (1.0 - s))).astype(BF16)
            du_ref[rows, :] = (da * g * s).astype(BF16)

    o = jax.ShapeDtypeStruct((t, n), BF16)
    blk = pl.BlockSpec((tm, tn), lambda i, j: (i, j))
    return pl.pallas_call(
        body,
        grid=(t // tm, n // tn),
        in_specs=[pl.BlockSpec((tm, k), lambda i, j: (i, 0)), pl.BlockSpec((tn, k), lambda i, j: (j, 0)), blk, blk],
        out_specs=[blk, blk],
        out_shape=[o, o],
        compiler_params=_cparams(("parallel", "parallel")),
        name="ffn_bwd_act",
    )(df, wd, gt, up)


def _dw_dt_rows(dw_in_t, ddtr_b, h, *, tk=1024):
    t, d = h.shape
    tk = min(tk, t)
    nk = t // tk

    def body(buf_ref, d_ref, h_ref, o_ref, acc):
        kk = pl.program_id(0)

        @pl.when(kk == 0)
        def _():
            acc[...] = jnp.zeros_like(acc)

        acc[...] += lax.dot_general(d_ref[...], h_ref[...], (_DIMS["tn"], ((), ())), preferred_element_type=F32)

        @pl.when(kk == nk - 1)
        def _():
            o_ref[...] = acc[0:HEADS, :]

    return pl.pallas_call(
        body,
        grid=(nk,),
        in_specs=[DEP_SPEC, pl.BlockSpec((tk, LANES), lambda kk: (kk, 0)), pl.BlockSpec((tk, d), lambda kk: (kk, 0))],
        out_specs=pl.BlockSpec((HEADS, d), lambda kk: (D_MAIN // HEADS, 0)),
        out_shape=jax.ShapeDtypeStruct(dw_in_t.shape, F32),
        input_output_aliases={0: 0},
        scratch_shapes=[pltpu.VMEM((LANES, d), F32)],
        compiler_params=_cparams(("arbitrary",)),
        name="mm_dw_dt",
    )(dw_in_t, ddtr_b, h)


ROW_TILE = 256


DEP_SPEC = pl.BlockSpec(memory_space=pl.ANY)


def _rms_fwd(xv, g, *, res=None, dep=None, out_dtype, name):
    t, d = xv.shape
    has_res = res is not None
    deps = [] if dep is None else [dep]

    def body(*refs):
        x_ref, g_ref = refs[0], refs[1]
        o_ref = refs[-1]
        v = x_ref[...]
        r = lax.rsqrt(jnp.mean(v * v, axis=-1, keepdims=True) + EPS)
        y = v * r * g_ref[...]
        if has_res:
            y = refs[2][...] + y
        o_ref[...] = y.astype(out_dtype)

    row = pl.BlockSpec((ROW_TILE, d), lambda i: (i, 0))
    vec = pl.BlockSpec((1, d), lambda i: (0, 0))
    return pl.pallas_call(
        body,
        grid=(t // ROW_TILE,),
        in_specs=[row, vec] + ([row] if has_res else []) + [DEP_SPEC] * len(deps),
        out_specs=row,
        out_shape=jax.ShapeDtypeStruct((t, d), out_dtype),
        compiler_params=_cparams(("parallel",)),
        name=name,
    )(*([xv, g] + ([res] if has_res else []) + deps))


def _rms_bwd_rows(dy, v, gv):
    r = lax.rsqrt(jnp.mean(v * v, axis=-1, keepdims=True) + EPS)
    xh = v * r
    gdy = dy * gv
    dx = r * (gdy - xh * jnp.mean(gdy * xh, axis=-1, keepdims=True))
    return dx, jnp.sum(dy * xh, axis=0, keepdims=True)


FUSED_ROWS = 512


def _down_loss(act, w_down, x1, tgt, g, *, tk):
    t, kdim = act.shape
    d = w_down.shape[1]
    tm, tk = min(FUSED_ROWS, t), min(tk, kdim)
    nk = kdim // tk
    assert t % tm == 0 and kdim % tk == 0

    def body(a_ref, w_ref, x1_ref, t_ref, g_ref, dx2_ref, df_ref, loss_ref, dg_ref, acc):
        i, kk = pl.program_id(0), pl.program_id(1)

        @pl.when(kk == 0)
        def _():
            acc[...] = jnp.zeros_like(acc)

        @pl.when(jnp.logical_and(i == 0, kk == 0))
        def _():
            dg_ref[...] = jnp.zeros_like(dg_ref)
            loss_ref[...] = jnp.zeros_like(loss_ref)

        acc[...] += jnp.dot(a_ref[...], w_ref[...], preferred_element_type=F32)

        @pl.when(kk == nk - 1)
        def _():
            v = acc[...]
            gv = g_ref[...]
            fh = v * lax.rsqrt(jnp.mean(v * v, axis=-1, keepdims=True) + EPS)
            e = x1_ref[...] + fh * gv - t_ref[...]
            dx2 = e * (1.0 / d)
            dx2_ref[...] = dx2
            df, dg = _rms_bwd_rows(dx2, v, gv)
            df_ref[...] = df.astype(BF16)
            dg_ref[...] += dg
            loss_ref[...] += 0.5 * jnp.sum(jnp.mean(e * e, axis=-1, keepdims=True), axis=0, keepdims=True)

    row = pl.BlockSpec((tm, d), lambda i, kk: (i, 0))
    vec = pl.BlockSpec((1, d), lambda i, kk: (0, 0))
    return pl.pallas_call(
        body,
        grid=(t // tm, nk),
        in_specs=[pl.BlockSpec((tm, tk), lambda i, kk: (i, kk)), pl.BlockSpec((tk, d), lambda i, kk: (kk, 0)),
                  row, row, vec],
        out_specs=[row, row, pl.BlockSpec((1, 1), lambda i, kk: (0, 0)), vec],
        out_shape=[jax.ShapeDtypeStruct((t, d), F32), jax.ShapeDtypeStruct((t, d), BF16),
                   jax.ShapeDtypeStruct((1, 1), F32), jax.ShapeDtypeStruct((1, d), F32)],
        scratch_shapes=[pltpu.VMEM((tm, d), F32)],
        compiler_params=_cparams(("arbitrary", "arbitrary")),
        name="mm_down_loss",
    )(act, w_down, x1, tgt, g)


def _dh_dx(dproj, w_in_t, ddtr_b, w_dt_t, xv, dx1, g, *, tk, dep):
    t, kdim = dproj.shape
    d = xv.shape[1]
    tm, tk = min(FUSED_ROWS, t), min(tk, kdim)
    nk = kdim // tk
    assert t % tm == 0 and kdim % tk == 0

    def body(a_ref, w_ref, dt_ref, wdt_ref, x_ref, dx1_ref, g_ref, dep_ref, dx_ref, dg_ref, acc):
        i, kk = pl.program_id(0), pl.program_id(1)

        @pl.when(kk == 0)
        def _():
            acc[...] = jnp.dot(dt_ref[...], wdt_ref[...], preferred_element_type=F32)

        @pl.when(jnp.logical_and(i == 0, kk == 0))
        def _():
            dg_ref[...] = jnp.zeros_like(dg_ref)

        acc[...] += jnp.dot(a_ref[...], w_ref[...], preferred_element_type=F32)

        @pl.when(kk == nk - 1)
        def _():
            dx, dg = _rms_bwd_rows(acc[...], x_ref[...], g_ref[...])
            dx_ref[...] = dx + dx1_ref[...]
            dg_ref[...] += dg

    row = pl.BlockSpec((tm, d), lambda i, kk: (i, 0))
    vec = pl.BlockSpec((1, d), lambda i, kk: (0, 0))
    return pl.pallas_call(
        body,
        grid=(t // tm, nk),
        in_specs=[pl.BlockSpec((tm, tk), lambda i, kk: (i, kk)), pl.BlockSpec((tk, d), lambda i, kk: (kk, 0)),
                  pl.BlockSpec((tm, LANES), lambda i, kk: (i, 0)), pl.BlockSpec((LANES, d), lambda i, kk: (0, 0)),
                  row, row, vec, pl.BlockSpec(memory_space=pl.ANY)],
        out_specs=[row, vec],
        out_shape=[jax.ShapeDtypeStruct((t, d), F32), jax.ShapeDtypeStruct((1, d), F32)],
        scratch_shapes=[pltpu.VMEM((tm, d), F32)],
        compiler_params=_cparams(("arbitrary", "arbitrary")),
        name="mm_dh_dx",
    )(dproj, w_in_t, ddtr_b, w_dt_t, xv, dx1, g, dep)


def _rms_post_pre(mix, xv, g_post, g_pre):
    t, d = mix.shape

    def body(m_ref, x_ref, gp_ref, gf_ref, x1_ref, h2_ref):
        v = m_ref[...]
        x1 = x_ref[...] + v * lax.rsqrt(jnp.mean(v * v, axis=-1, keepdims=True) + EPS) * gp_ref[...]
        x1_ref[...] = x1
        h2_ref[...] = (x1 * lax.rsqrt(jnp.mean(x1 * x1, axis=-1, keepdims=True) + EPS) * gf_ref[...]).astype(BF16)

    row = pl.BlockSpec((ROW_TILE, d), lambda i: (i, 0))
    vec = pl.BlockSpec((1, d), lambda i: (0, 0))
    return pl.pallas_call(
        body,
        grid=(t // ROW_TILE,),
        in_specs=[row, row, vec, vec],
        out_specs=[row, row],
        out_shape=[jax.ShapeDtypeStruct((t, d), F32), jax.ShapeDtypeStruct((t, d), BF16)],
        compiler_params=_cparams(("parallel",)),
        name="rms_mix_post_ffn_pre",
    )(mix, xv, g_post, g_pre)


def _rms_bwd_pre_post(dh2, x1, g_pre, dx2, mix, g_post):
    t, d = x1.shape

    def body(dh_ref, x1_ref, gf_ref, dx2_ref, m_ref, gp_ref, dx1_ref, dmix_ref, dgf_ref, dgp_ref):
        @pl.when(pl.program_id(0) == 0)
        def _():
            dgf_ref[...] = jnp.zeros_like(dgf_ref)
            dgp_ref[...] = jnp.zeros_like(dgp_ref)

        dx, dgf = _rms_bwd_rows(dh_ref[...], x1_ref[...], gf_ref[...])
        dx1 = dx + dx2_ref[...]
        dx1_ref[...] = dx1
        dmix, dgp = _rms_bwd_rows(dx1, m_ref[...], gp_ref[...])
        dmix_ref[...] = dmix.astype(BF16)
        dgf_ref[...] += dgf
        dgp_ref[...] += dgp

    row = pl.BlockSpec((ROW_TILE, d), lambda i: (i, 0))
    vec = pl.BlockSpec((1, d), lambda i: (0, 0))
    return pl.pallas_call(
        body,
        grid=(t // ROW_TILE,),
        in_specs=[row, row, vec, row, row, vec],
        out_specs=[row, row, vec, vec],
        out_shape=[jax.ShapeDtypeStruct((t, d), F32), jax.ShapeDtypeStruct((t, d), BF16),
                   jax.ShapeDtypeStruct((1, d), F32), jax.ShapeDtypeStruct((1, d), F32)],
        compiler_params=_cparams(("arbitrary",)),
        name="rms_ffn_pre_mix_post_bwd",
    )(dh2, x1, g_pre, dx2, mix, g_post)


CONV_ROWS = 256
TAP_ROWS = 64
HALO31 = 32
HALO4 = 8


def _sum8(v):
    return jnp.sum(v.reshape(v.shape[0] // 8, 8, v.shape[1]), axis=0)


SUBLANES = 8
PHASE_SPAN = (CONV_K - 1) // SUBLANES * SUBLANES


def _phase_scratch(ts):
    return pltpu.VMEM((SUBLANES, ts + PHASE_SPAN, LANES), F32)


def _phase_copies(ph, buf, ln, base, ts):
    for s in range(SUBLANES):
        n = ts + (CONV_K - 1 - s) // SUBLANES * SUBLANES
        ph[s, 0:n, :] = buf[pl.ds(base + s, n), ln]


def _tap_rows(ph, off, r0):
    s = off % SUBLANES
    return ph[s, pl.ds(r0 + off - s, TAP_ROWS), :]


def _conv_branch_fwd(proj, cw, cb, lg, lb, *, nb, seq):
    ts, c, halo = CONV_ROWS, D_CONV, HALO31
    ns = seq // ts
    base = halo - CONV_K + 1

    def body(ca_ref, cg_ref, w_ref, b_ref, lg_ref, lb_ref, u1_ref, u_ref, ubuf, uph):
        i = pl.program_id(1)

        @pl.when(i == 0)
        def _():
            ubuf[0:halo, :] = jnp.zeros((halo, c), F32)

        @pl.when(i > 0)
        def _():
            ubuf[0:halo, :] = ubuf[ts:ts + halo, :]

        ubuf[halo:halo + ts, :] = ca_ref[...] * _sig(cg_ref[...])

        def lane_tile(j, carry):
            ln = pl.ds(pl.multiple_of(j * LANES, LANES), LANES)
            _phase_copies(uph, ubuf, ln, base, ts)
            for r in range(ts // TAP_ROWS):
                acc = jnp.broadcast_to(b_ref[:, ln], (TAP_ROWS, LANES))
                for k in range(CONV_K):
                    acc = acc + w_ref[pl.ds(k, 1), ln] * _tap_rows(uph, k, r * TAP_ROWS)
                u1_ref[pl.ds(r * TAP_ROWS, TAP_ROWS), ln] = acc
            return carry

        lax.fori_loop(0, c // LANES, lane_tile, 0)
        v = u1_ref[...]
        mu = jnp.mean(v, axis=-1, keepdims=True)
        dv = v - mu
        xh = dv * lax.rsqrt(jnp.mean(dv * dv, axis=-1, keepdims=True) + EPS)
        u2 = xh * lg_ref[...] + lb_ref[...]
        u_ref[...] = (u2 * _sig(u2)).astype(BF16)

    t = nb * seq
    row = lambda col: pl.BlockSpec((ts, c), lambda b, i: (b * ns + i, col))
    vec = pl.BlockSpec((1, c), lambda b, i: (0, 0))
    return pl.pallas_call(
        body,
        grid=(nb, ns),
        in_specs=[row(0), row(1), pl.BlockSpec((32, c), lambda b, i: (0, 0)), vec, vec, vec],
        out_specs=[row(0), row(0)],
        out_shape=[jax.ShapeDtypeStruct((t, c), F32), jax.ShapeDtypeStruct((t, c + D_SSD), BF16)],
        scratch_shapes=[pltpu.VMEM((halo + ts, c), F32), _phase_scratch(ts)],
        compiler_params=_cparams(("parallel", "arbitrary")),
        name="conv_branch_fwd",
    )(proj, proj, cw, cb, lg, lb)


def _conv_branch_bwd(duy, u1, proj, cw, lg, lb, *, nb, seq):
    ts, c, halo = CONV_ROWS, D_CONV, HALO31
    ns = seq // ts
    base = halo - CONV_K + 1
    hb = ts // halo

    def body(du_ref, u1_ref, ca_ref, cg_ref, cah_ref, cgh_ref, w_ref, lg_ref, lb_ref,
             dcacg_ref, dw_ref, db_ref, dlg_ref, dlb_ref,
             ubuf, dbuf, du0buf, dwacc, dbacc, dlgacc, dlbacc, uph, dph):
        b, i = pl.program_id(0), pl.program_id(1)
        rc = ns - 1 - i

        @pl.when(jnp.logical_and(b == 0, i == 0))
        def _():
            dwacc[...] = jnp.zeros_like(dwacc)
            dbacc[...] = jnp.zeros_like(dbacc)
            dlgacc[...] = jnp.zeros_like(dlgacc)
            dlbacc[...] = jnp.zeros_like(dlbacc)

        @pl.when(i == 0)
        def _():
            dbuf[ts:ts + halo, :] = jnp.zeros((halo, c), F32)

        @pl.when(i > 0)
        def _():
            dbuf[ts:ts + halo, :] = dbuf[0:halo, :]

        v = u1_ref[...]
        mu = jnp.mean(v, axis=-1, keepdims=True)
        dv = v - mu
        rstd = lax.rsqrt(jnp.mean(dv * dv, axis=-1, keepdims=True) + EPS)
        xh = dv * rstd
        lgv = lg_ref[...]
        u2 = xh * lgv + lb_ref[...]
        s2 = _sig(u2)
        du2 = du_ref[...] * (s2 * (1.0 + u2 * (1.0 - s2)))
        dlgacc[...] += jnp.sum(du2 * xh, axis=0, keepdims=True)
        dlbacc[...] += jnp.sum(du2, axis=0, keepdims=True)
        gd = du2 * lgv
        du1 = rstd * (gd - jnp.mean(gd, axis=-1, keepdims=True) - xh * jnp.mean(gd * xh, axis=-1, keepdims=True))
        dbacc[...] += jnp.sum(du1, axis=0, keepdims=True)
        dbuf[0:ts, :] = du1

        @pl.when(rc == 0)
        def _():
            ubuf[0:halo, :] = jnp.zeros((halo, c), F32)

        @pl.when(rc > 0)
        def _():
            ubuf[0:halo, :] = cah_ref[...] * _sig(cgh_ref[...])

        cav = ca_ref[...]
        sg = _sig(cg_ref[...])
        ubuf[halo:halo + ts, :] = cav * sg

        def lane_tile(j, carry):
            ln = pl.ds(pl.multiple_of(j * LANES, LANES), LANES)
            _phase_copies(uph, ubuf, ln, base, ts)
            _phase_copies(dph, dbuf, ln, 0, ts)
            for r in range(ts // TAP_ROWS):
                r0 = r * TAP_ROWS
                d1 = dbuf[pl.ds(r0, TAP_ROWS), ln]
                acc = jnp.zeros((TAP_ROWS, LANES), F32)
                for k in range(CONV_K):
                    acc = acc + w_ref[pl.ds(k, 1), ln] * _tap_rows(dph, CONV_K - 1 - k, r0)
                    dwacc[pl.ds(k * 8, 8), ln] += _sum8(d1 * _tap_rows(uph, k, r0))
                du0buf[pl.ds(r0, TAP_ROWS), ln] = acc
            return carry

        lax.fori_loop(0, c // LANES, lane_tile, 0)
        du0 = du0buf[...]
        dcacg_ref[:, 0:c] = (du0 * sg).astype(BF16)
        dcacg_ref[:, c:2 * c] = (du0 * cav * sg * (1.0 - sg)).astype(BF16)

        @pl.when(jnp.logical_and(b == nb - 1, i == ns - 1))
        def _():
            for k in range(CONV_K):
                dw_ref[pl.ds(k, 1), :] = jnp.sum(dwacc[pl.ds(k * 8, 8), :], axis=0, keepdims=True)
            dw_ref[pl.ds(CONV_K, 1), :] = jnp.zeros((1, c), F32)
            db_ref[...] = dbacc[...]
            dlg_ref[...] = dlgacc[...]
            dlb_ref[...] = dlbacc[...]

    t = nb * seq
    rowblk = lambda b, i: b * ns + (ns - 1 - i)
    row = lambda col: pl.BlockSpec((ts, c), lambda b, i: (rowblk(b, i), col))
    hrow = lambda col: pl.BlockSpec((halo, c), lambda b, i: (jnp.maximum(rowblk(b, i) * hb - 1, 0), col))
    vec = pl.BlockSpec((1, c), lambda b, i: (0, 0))
    wspec = pl.BlockSpec((32, c), lambda b, i: (0, 0))
    return pl.pallas_call(
        body,
        grid=(nb, ns),
        in_specs=[row(0), row(0), row(0), row(1), hrow(0), hrow(1), wspec, vec, vec],
        out_specs=[pl.BlockSpec((ts, 2 * c), lambda b, i: (rowblk(b, i), 0)), wspec, vec, vec, vec],
        out_shape=[jax.ShapeDtypeStruct((t, D_MAIN), BF16), jax.ShapeDtypeStruct((32, c), F32),
                   jax.ShapeDtypeStruct((1, c), F32), jax.ShapeDtypeStruct((1, c), F32), jax.ShapeDtypeStruct((1, c), F32)],
        scratch_shapes=[pltpu.VMEM((halo + ts, c), F32), pltpu.VMEM((ts + halo, c), F32), pltpu.VMEM((ts, c), F32),
                        pltpu.VMEM((CONV_K * 8, c), F32), pltpu.VMEM((1, c), F32), pltpu.VMEM((1, c), F32),
                        pltpu.VMEM((1, c), F32), _phase_scratch(ts), _phase_scratch(ts)],
        compiler_params=_cparams(("arbitrary", "arbitrary")),
        name="conv_branch_bwd",
    )(duy, u1, proj, proj, proj, proj, cw, lg, lb)


XBC_COL0 = (2 * D_CONV + D_SSD) // 1024


def _ssd_pre_fwd(proj, sw, sb, *, nb, seq):
    ts, c, halo = CONV_ROWS, 1024, HALO4
    ns = seq // ts
    base = halo - SSD_CONV_K + 1

    def body(x_ref, w_ref, b_ref, o_ref, xbuf):
        i = pl.program_id(2)

        @pl.when(i == 0)
        def _():
            xbuf[0:halo, :] = jnp.zeros((halo, c), F32)

        @pl.when(i > 0)
        def _():
            xbuf[0:halo, :] = xbuf[ts:ts + halo, :]

        xbuf[halo:halo + ts, :] = x_ref[...]

        def lane_tile(j, carry):
            ln = pl.ds(pl.multiple_of(j * LANES, LANES), LANES)
            for r in range(ts // TAP_ROWS):
                acc = jnp.broadcast_to(b_ref[:, ln], (TAP_ROWS, LANES))
                for k in range(SSD_CONV_K):
                    acc = acc + w_ref[pl.ds(k, 1), ln] * xbuf[pl.ds(r * TAP_ROWS + base + k, TAP_ROWS), ln]
                o_ref[pl.ds(r * TAP_ROWS, TAP_ROWS), ln] = acc * _sig(acc)
            return carry

        lax.fori_loop(0, c // LANES, lane_tile, 0)

    t = nb * seq
    return pl.pallas_call(
        body,
        grid=(2, nb, ns),
        in_specs=[pl.BlockSpec((ts, c), lambda j, b, i: (b * ns + i, XBC_COL0 + j)),
                  pl.BlockSpec((8, c), lambda j, b, i: (0, j)), pl.BlockSpec((1, c), lambda j, b, i: (0, j))],
        out_specs=pl.BlockSpec((ts, c), lambda j, b, i: (b * ns + i, j)),
        out_shape=jax.ShapeDtypeStruct((t, D_XBC), F32),
        scratch_shapes=[pltpu.VMEM((halo + ts, c), F32)],
        compiler_params=_cparams(("parallel", "parallel", "arbitrary")),
        name="ssd_pre_fwd",
    )(proj, sw, sb)


def _ssd_pre_bwd(dproj, dxs, proj, sw, sb, *, nb, seq):
    ts, c, halo = CONV_ROWS, 1024, HALO4
    ns = seq // ts
    base = halo - SSD_CONV_K + 1
    hb = ts // halo

    def body(dproj_ref, d_ref, x_ref, xh_ref, w_ref, b_ref, dx_ref, dw_ref, db_ref, xbuf, dbuf, dwacc, dbacc):
        b, i = pl.program_id(1), pl.program_id(2)
        rc = ns - 1 - i

        @pl.when(jnp.logical_and(b == 0, i == 0))
        def _():
            dwacc[...] = jnp.zeros_like(dwacc)
            dbacc[...] = jnp.zeros_like(dbacc)

        @pl.when(i == 0)
        def _():
            dbuf[ts:ts + halo, :] = jnp.zeros((halo, c), F32)

        @pl.when(i > 0)
        def _():
            dbuf[ts:ts + halo, :] = dbuf[0:halo, :]

        @pl.when(rc == 0)
        def _():
            xbuf[0:halo, :] = jnp.zeros((halo, c), F32)

        @pl.when(rc > 0)
        def _():
            xbuf[0:halo, :] = xh_ref[...]

        xbuf[halo:halo + ts, :] = x_ref[...]

        def pre_tile(j, carry):
            ln = pl.ds(pl.multiple_of(j * LANES, LANES), LANES)
            for r in range(ts // TAP_ROWS):
                r0 = r * TAP_ROWS
                acc = jnp.broadcast_to(b_ref[:, ln], (TAP_ROWS, LANES))
                for k in range(SSD_CONV_K):
                    acc = acc + w_ref[pl.ds(k, 1), ln] * xbuf[pl.ds(r0 + base + k, TAP_ROWS), ln]
                s = _sig(acc)
                dc = d_ref[pl.ds(r0, TAP_ROWS), ln] * (s * (1.0 + acc * (1.0 - s)))
                dbuf[pl.ds(r0, TAP_ROWS), ln] = dc
                dbacc[:, ln] += _sum8(dc)
            return carry

        lax.fori_loop(0, c // LANES, pre_tile, 0)

        def lane_tile(j, carry):
            ln = pl.ds(pl.multiple_of(j * LANES, LANES), LANES)
            for r in range(ts // TAP_ROWS):
                r0 = r * TAP_ROWS
                d1 = dbuf[pl.ds(r0, TAP_ROWS), ln]
                acc = jnp.zeros((TAP_ROWS, LANES), F32)
                for k in range(SSD_CONV_K):
                    acc = acc + w_ref[pl.ds(k, 1), ln] * dbuf[pl.ds(r0 + SSD_CONV_K - 1 - k, TAP_ROWS), ln]
                    dwacc[pl.ds(k * 8, 8), ln] += _sum8(d1 * xbuf[pl.ds(r0 + base + k, TAP_ROWS), ln])
                dx_ref[pl.ds(r0, TAP_ROWS), ln] = acc.astype(BF16)
            return carry

        lax.fori_loop(0, c // LANES, lane_tile, 0)

        @pl.when(jnp.logical_and(b == nb - 1, i == ns - 1))
        def _():
            for k in range(SSD_CONV_K):
                dw_ref[pl.ds(k, 1), :] = jnp.sum(dwacc[pl.ds(k * 8, 8), :], axis=0, keepdims=True)
            dw_ref[pl.ds(SSD_CONV_K, 8 - SSD_CONV_K), :] = jnp.zeros((8 - SSD_CONV_K, c), F32)
            db_ref[...] = jnp.sum(dbacc[...], axis=0, keepdims=True)

    t = nb * seq
    rowblk = lambda b, i: b * ns + (ns - 1 - i)
    return pl.pallas_call(
        body,
        grid=(2, nb, ns),
        in_specs=[DEP_SPEC, pl.BlockSpec((ts, c), lambda j, b, i: (rowblk(b, i), j)),
                  pl.BlockSpec((ts, c), lambda j, b, i: (rowblk(b, i), XBC_COL0 + j)),
                  pl.BlockSpec((halo, c), lambda j, b, i: (jnp.maximum(rowblk(b, i) * hb - 1, 0), XBC_COL0 + j)),
                  pl.BlockSpec((8, c), lambda j, b, i: (0, j)), pl.BlockSpec((1, c), lambda j, b, i: (0, j))],
        out_specs=[pl.BlockSpec((ts, c), lambda j, b, i: (rowblk(b, i), XBC_COL0 + j)),
                   pl.BlockSpec((8, c), lambda j, b, i: (0, j)), pl.BlockSpec((1, c), lambda j, b, i: (0, j))],
        out_shape=[jax.ShapeDtypeStruct(dproj.shape, BF16), jax.ShapeDtypeStruct((8, D_XBC), F32),
                   jax.ShapeDtypeStruct((1, D_XBC), F32)],
        input_output_aliases={0: 0},
        scratch_shapes=[pltpu.VMEM((halo + ts, c), F32), pltpu.VMEM((ts + halo, c), F32),
                        pltpu.VMEM((SSD_CONV_K * 8, c), F32), pltpu.VMEM((8, c), F32)],
        compiler_params=_cparams(("arbitrary", "arbitrary", "arbitrary")),
        name="ssd_pre_bwd",
    )(dproj, dxs, proj, proj, sw, sb)


Z_COL = (2 * D_CONV) // 1024
GROUP_W = D_SSD // GROUPS


def _softplus(v):
    return jnp.maximum(v, 0.0) + jnp.log(1.0 + jnp.exp(-jnp.abs(v)))


def _dot(a, b):
    return jnp.dot(a, b, preferred_element_type=F32)


def _dot_nt(a, b):
    return lax.dot_general(a, b, (_DIMS["nt"], ((), ())), preferred_element_type=F32)


def _dot_tn(a, b):
    return lax.dot_general(a, b, (_DIMS["tn"], ((), ())), preferred_element_type=F32)


def _dot_exact(a, b):
    return jnp.dot(a, b, precision=lax.Precision.HIGHEST, preferred_element_type=F32)


def _chunk_decays(dtr_ref, bias_ref, alog_ref):
    q = CHUNK
    ii = lax.broadcasted_iota(jnp.int32, (q, q), 0)
    jj = lax.broadcasted_iota(jnp.int32, (q, q), 1)
    tri = jj <= ii
    dt = _softplus(dtr_ref[...] + bias_ref[...])
    a_head = -jnp.exp(alog_ref[...])
    cs = _dot_exact(tri.astype(F32), dt * a_head)
    return tri, dt, a_head, cs, cs.T


def _ssd_fwd(uy, xs_all, proj, dtr, dt_bias, a_log, d_lanes, norm_w, *, nb, seq):
    q = CHUNK
    nc = seq // q
    t = nb * seq

    def body(uy_ref, xs_ref, bm_ref, cm_ref, z_ref, dtr_ref, bias_ref, alog_ref, dl_ref, nw_ref,
             y_ref, ys_ref, st_ref, state):
        @pl.when(pl.program_id(1) == 0)
        def _():
            state[...] = jnp.zeros_like(state)

        tri, dt, _, cs, cst = _chunk_decays(dtr_ref, bias_ref, alog_ref)
        first = lax.broadcasted_iota(jnp.int32, (1, LANES), 1) < HEAD_DIM
        for g in range(GROUPS):
            gl = slice(g * STATE, (g + 1) * STATE)
            bb = bm_ref[:, gl].astype(BF16)
            cb = cm_ref[:, gl].astype(BF16)
            scores = _dot_nt(cb, bb)
            for p in range(2):
                pr = 2 * g + p
                h0 = 2 * pr
                sl = slice(pr * LANES, (pr + 1) * LANES)
                xv = xs_ref[:, sl]
                dtp = jnp.where(first, dt[:, h0:h0 + 1], dt[:, h0 + 1:h0 + 2])
                csp = jnp.where(first, cs[:, h0:h0 + 1], cs[:, h0 + 1:h0 + 2])
                xd = xv * dtp
                yv = None
                for hh, keep in ((h0, first), (h0 + 1, jnp.logical_not(first))):
                    decay = jnp.where(tri, jnp.exp(cs[:, hh:hh + 1] - cst[hh:hh + 1, :]), 0.0)
                    part = _dot((scores * decay).astype(BF16), jnp.where(keep, xd, 0.0).astype(BF16))
                    yv = part if yv is None else yv + part
                hp = state[pr]
                st_ref[0, pr] = hp
                yv = yv + jnp.exp(csp) * _dot(cb, hp.astype(BF16))
                last = csp[q - 1:q, :]
                state[pr] = jnp.exp(last) * hp + _dot_tn(bb, (xd * jnp.exp(last - csp)).astype(BF16))
                ys_ref[:, sl] = yv + dl_ref[:, sl] * xv
        zv = z_ref[...]
        gated = ys_ref[...] * (zv * _sig(zv))
        for g in range(GROUPS):
            gl = slice(g * GROUP_W, (g + 1) * GROUP_W)
            v = gated[:, gl]
            r = lax.rsqrt(jnp.mean(v * v, axis=-1, keepdims=True) + EPS)
            y_ref[:, gl] = (v * r * nw_ref[:, gl]).astype(BF16)

    blk = lambda w, col: pl.BlockSpec((q, w), lambda b, c: (b * nc + c, col))
    vec = lambda w: pl.BlockSpec((1, w), lambda b, c: (0, 0))
    return pl.pallas_call(
        body,
        grid=(nb, nc),
        in_specs=[DEP_SPEC, blk(D_SSD, 0), blk(GROUPS * STATE, 2), blk(GROUPS * STATE, 3), blk(D_SSD, Z_COL),
                  blk(LANES, 0), vec(LANES), vec(LANES), vec(D_SSD), vec(D_SSD)],
        out_specs=[blk(D_SSD, 1), blk(D_SSD, 0),
                   pl.BlockSpec((1, PAIRS, STATE, LANES), lambda b, c: (b * nc + c, 0, 0, 0))],
        out_shape=[jax.ShapeDtypeStruct(uy.shape, BF16), jax.ShapeDtypeStruct((t, D_SSD), F32),
                   jax.ShapeDtypeStruct((nb * nc, PAIRS, STATE, LANES), F32)],
        input_output_aliases={0: 0},
        scratch_shapes=[pltpu.VMEM((PAIRS, STATE, LANES), F32)],
        compiler_params=_cparams(("parallel", "arbitrary")),
        name="ssd_fwd",
    )(uy, xs_all, xs_all, xs_all, proj, dtr, dt_bias, a_log, d_lanes, norm_w)


def _ssd_bwd(dproj, duy, proj, ys, xs_all, dtr, states, dt_bias, a_log, d_lanes, norm_w, *, nb, seq):
    q = CHUNK
    nc = seq // q
    t = nb * seq

    def body(dproj_ref, dy_ref, z_ref, ys_ref, xs_ref, bm_ref, cm_ref, dtr_ref, st_ref, bias_ref, alog_ref, dl_ref,
             nw_ref, dz_ref, dx_ref, ddtr_ref, small_ref,
             dstate, dys_buf, dcsl, ddtl, dcst, dnw_acc, dd_acc, dbias_acc, da_acc):
        b, c = pl.program_id(0), pl.program_id(1)

        @pl.when(jnp.logical_and(b == 0, c == 0))
        def _():
            dnw_acc[...] = jnp.zeros_like(dnw_acc)
            dd_acc[...] = jnp.zeros_like(dd_acc)
            dbias_acc[...] = jnp.zeros_like(dbias_acc)
            da_acc[...] = jnp.zeros_like(da_acc)
            dcst[...] = jnp.zeros_like(dcst)

        @pl.when(c == 0)
        def _():
            dstate[...] = jnp.zeros_like(dstate)

        zv = z_ref[...]
        sz = _sig(zv)
        silz = zv * sz
        ysv = ys_ref[...]
        gated = ysv * silz
        dyv = dy_ref[...]
        nwv = nw_ref[...]
        for g in range(GROUPS):
            gl = slice(g * GROUP_W, (g + 1) * GROUP_W)
            v = gated[:, gl]
            r = lax.rsqrt(jnp.mean(v * v, axis=-1, keepdims=True) + EPS)
            yn = v * r
            dyn = dyv[:, gl] * nwv[:, gl]
            dnw_acc[:, gl] += jnp.sum(dyv[:, gl] * yn, axis=0, keepdims=True)
            dys_buf[:, gl] = r * (dyn - yn * jnp.mean(dyn * yn, axis=-1, keepdims=True))
        dgated = dys_buf[...]
        dz_ref[...] = (dgated * ysv * (sz * (1.0 + zv * (1.0 - sz)))).astype(BF16)
        dys_all = dgated * silz
        dys_buf[...] = dys_all
        dd_acc[...] += jnp.sum(dys_all * xs_ref[...], axis=0, keepdims=True)

        tri, dt, a_head, cs, cst = _chunk_decays(dtr_ref, bias_ref, alog_ref)
        lane = lax.broadcasted_iota(jnp.int32, (1, LANES), 1)
        first = lane < HEAD_DIM
        dcs_h = jnp.zeros((q, LANES), F32)
        for g in range(GROUPS):
            gl = slice(g * STATE, (g + 1) * STATE)
            bb = bm_ref[:, gl].astype(BF16)
            cb = cm_ref[:, gl].astype(BF16)
            scores = _dot_nt(cb, bb)
            dscores = jnp.zeros((q, q), F32)
            dbg = jnp.zeros((q, STATE), F32)
            dcg = jnp.zeros((q, STATE), F32)
            for p in range(2):
                pr = 2 * g + p
                h0 = 2 * pr
                sl = slice(pr * LANES, (pr + 1) * LANES)
                xv = xs_ref[:, sl]
                dyp = dys_buf[:, sl]
                dtp = jnp.where(first, dt[:, h0:h0 + 1], dt[:, h0 + 1:h0 + 2])
                csp = jnp.where(first, cs[:, h0:h0 + 1], cs[:, h0 + 1:h0 + 2])
                xd = xv * dtp
                xdb = xd.astype(BF16)
                hp = st_ref[0, pr]
                dhn = dstate[pr]
                hpb = hp.astype(BF16)
                dhnb = dhn.astype(BF16)
                lam = jnp.exp(csp)
                last = csp[q - 1:q, :]
                gam = jnp.exp(last)
                w = jnp.exp(last - csp)
                dxd = jnp.zeros((q, LANES), F32)
                for hh, keep in ((h0, first), (h0 + 1, jnp.logical_not(first))):
                    decay = jnp.where(tri, jnp.exp(cs[:, hh:hh + 1] - cst[hh:hh + 1, :]), 0.0)
                    m = scores * decay
                    dym = jnp.where(keep, dyp, 0.0).astype(BF16)
                    dm = _dot_nt(dym, xdb)
                    dxd = dxd + _dot_tn(m.astype(BF16), dym)
                    e = dm * m
                    dcs_h = dcs_h + jnp.where(lane == hh, jnp.sum(e, axis=1, keepdims=True), 0.0)
                    dcst[hh:hh + 1, :] = jnp.sum(e, axis=0, keepdims=True)
                    dscores = dscores + dm * decay
                yoff = lam * _dot(cb, hpb)
                ldy = (lam * dyp).astype(BF16)
                dcg = dcg + _dot_nt(ldy, hpb)
                dstate[pr] = gam * dhn + _dot_tn(cb, ldy)
                bdh = _dot(bb, dhnb)
                dxd = dxd + w * bdh
                xdw = xd * w
                dbg = dbg + _dot_nt(xdw.astype(BF16), dhnb)
                wd = xdw * bdh
                dcsl[:, sl] = dyp * yoff - wd
                dcsl[q - 1:q, sl] += (jnp.sum(wd, axis=0, keepdims=True)
                                      + gam * jnp.sum(dhn * hp, axis=0, keepdims=True))
                dx_ref[:, sl] = dxd * dtp + dyp * dl_ref[:, sl]
                ddtl[:, sl] = dxd * xv
            dsb = dscores.astype(BF16)
            dx_ref[:, D_SSD + g * STATE:D_SSD + (g + 1) * STATE] = dbg + _dot_tn(dsb, cb)
            dx_ref[:, D_SSD + (GROUPS + g) * STATE:D_SSD + (GROUPS + g + 1) * STATE] = dcg + _dot(dsb, bb)

        li = lax.broadcasted_iota(jnp.int32, (D_SSD, LANES), 0)
        hi = lax.broadcasted_iota(jnp.int32, (D_SSD, LANES), 1)
        sel = (li // HEAD_DIM == hi).astype(F32)
        dcs_h = dcs_h + _dot_exact(dcsl[...], sel) - dcst[...].T
        ddt = _dot_exact(ddtl[...], sel)
        upper = lax.broadcasted_iota(jnp.int32, (q, q), 1) >= lax.broadcasted_iota(jnp.int32, (q, q), 0)
        da = _dot_exact(upper.astype(F32), dcs_h)
        ddt = ddt + da * a_head
        da_acc[...] += jnp.sum(da * dt, axis=0, keepdims=True)
        ddtr = ddt * _sig(dtr_ref[...] + bias_ref[...])
        ddtr_ref[...] = ddtr
        dbias_acc[...] += jnp.sum(ddtr, axis=0, keepdims=True)

        @pl.when(jnp.logical_and(b == nb - 1, c == nc - 1))
        def _():
            small_ref[...] = jnp.zeros_like(small_ref)
            small_ref[0:1, :] = dnw_acc[...]
            small_ref[1:2, 0:LANES] = _dot_exact(jnp.broadcast_to(dd_acc[...], (8, D_SSD)), sel)[0:1, :]
            small_ref[2:3, 0:LANES] = dbias_acc[...]
            small_ref[3:4, 0:LANES] = da_acc[...] * a_head

    rowblk = lambda b, c: b * nc + (nc - 1 - c)
    blk = lambda w, col: pl.BlockSpec((q, w), lambda b, c: (rowblk(b, c), col))
    vec = lambda w: pl.BlockSpec((1, w), lambda b, c: (0, 0))
    return pl.pallas_call(
        body,
        grid=(nb, nc),
        in_specs=[DEP_SPEC, blk(D_SSD, 1), blk(D_SSD, Z_COL), blk(D_SSD, 0), blk(D_SSD, 0), blk(GROUPS * STATE, 2),
                  blk(GROUPS * STATE, 3), blk(LANES, 0),
                  pl.BlockSpec((1, PAIRS, STATE, LANES), lambda b, c: (rowblk(b, c), 0, 0, 0)),
                  vec(LANES), vec(LANES), vec(D_SSD), vec(D_SSD)],
        out_specs=[blk(D_SSD, Z_COL), blk(D_XBC, 0), blk(LANES, 0), pl.BlockSpec((8, D_SSD), lambda b, c: (0, 0))],
        out_shape=[jax.ShapeDtypeStruct(dproj.shape, BF16), jax.ShapeDtypeStruct((t, D_XBC), F32),
                   jax.ShapeDtypeStruct((t, LANES), F32), jax.ShapeDtypeStruct((8, D_SSD), F32)],
        input_output_aliases={0: 0},
        scratch_shapes=[pltpu.VMEM((PAIRS, STATE, LANES), F32), pltpu.VMEM((q, D_SSD), F32),
                        pltpu.VMEM((q, D_SSD), F32), pltpu.VMEM((q, D_SSD), F32), pltpu.VMEM((LANES, q), F32),
                        pltpu.VMEM((1, D_SSD), F32), pltpu.VMEM((1, D_SSD), F32), pltpu.VMEM((1, LANES), F32),
                        pltpu.VMEM((1, LANES), F32)],
        compiler_params=_cparams(("arbitrary", "arbitrary")),
        name="ssd_bwd",
    )(dproj, duy, proj, ys, xs_all, xs_all, xs_all, dtr, states, dt_bias, a_log, d_lanes, norm_w)


HBM_SPEC = pl.BlockSpec(memory_space=pltpu.HBM)
MESH_ID = pl.DeviceIdType.MESH


def _coords():
    return lax.axis_index("x"), lax.axis_index("y"), lax.axis_index("c")


def _chip_peer(xi, yi, ci, d):
    return (jnp.bitwise_xor(xi, d >> 1), jnp.bitwise_xor(yi, d & 1), ci)


def _remote(src, dst, send_sem, recv_sem, peer):
    return pltpu.make_async_remote_copy(src_ref=src, dst_ref=dst, send_sem=send_sem, recv_sem=recv_sem,
                                        device_id=peer, device_id_type=MESH_ID)


SEM_SPEC = pl.BlockSpec(memory_space=pltpu.SEMAPHORE)
ANY_SPEC = pl.BlockSpec(memory_space=pl.ANY)
EFFECT = pltpu.SideEffectType.DATAFLOW_SIDE_EFFECTING
COPIES = 3


def _half(ref, axis, which, lead=0):
    size = ref.shape[lead + axis] // 2
    part = pl.ds(which * size, size)
    idx = (slice(None),) * lead + ((part, slice(None)) if axis == 0 else (slice(None), part))
    return ref.at[idx]


def _halved_shape(shape, axis):
    lead = len(shape) - 2
    return tuple(d // 2 if i == lead + axis else d for i, d in enumerate(shape))


def _gather_plan(axis):
    def plan(xi, yi, ci, src, land):
        me = 2 * xi + yi
        out = []
        for d in (1, 2, 3):
            there = jnp.bitwise_xor(me, d)
            if axis is None:
                out.append((src, land.at[me], _chip_peer(xi, yi, ci, d), land.at[there]))
            else:
                out.append((_half(src, axis, ci), _half(land.at[me], axis, ci), _chip_peer(xi, yi, ci, d),
                            _half(land.at[there], axis, ci)))
        return out
    return plan


def _owners_plan(xi, yi, ci, src, land):
    me = 2 * xi + yi
    return [(src.at[jnp.bitwise_xor(me, d)], land.at[d - 1], _chip_peer(xi, yi, ci, d), land.at[d - 1])
            for d in (1, 2, 3)]


def _split_start(srcs, lands, plans, *, name):
    n = len(srcs)

    def body(*refs):
        src_refs, land_refs = refs[:n], refs[n:2 * n]
        ssems, rsems = refs[2 * n:3 * n], refs[3 * n:4 * n]
        token = refs[-1]
        xi, yi, ci = _coords()
        for t in range(n):
            for k, (src, dst, peer, _) in enumerate(plans[t](xi, yi, ci, src_refs[t], land_refs[t])):
                _remote(src, dst, ssems[t].at[k], rsems[t].at[k], peer).start()
        token[...] = jnp.zeros_like(token)

    bufs = list(srcs) + list(lands)
    outs = pl.pallas_call(
        body,
        name=name,
        in_specs=[HBM_SPEC] * (2 * n),
        out_specs=[SEM_SPEC] * (2 * n) + [HBM_SPEC] * (2 * n) + [pl.BlockSpec(memory_space=pltpu.VMEM)],
        out_shape=[pltpu.SemaphoreType.DMA((COPIES,))] * (2 * n) + [pltpu.HBM(a.shape, a.dtype) for a in bufs]
        + [jax.ShapeDtypeStruct((8, LANES), F32)],
        input_output_aliases={i: 2 * n + i for i in range(2 * n)},
        compiler_params=pltpu.CompilerParams(has_side_effects=EFFECT),
    )(*[pltpu.with_memory_space_constraint(a, pltpu.HBM) for a in bufs])
    return outs[:n], outs[n:2 * n], outs[2 * n:3 * n], outs[3 * n:4 * n], outs[-1]


def _split_wait(ssems, rsems, srcs, lands, plans, after, *, name):
    n = len(srcs)

    def body(*refs):
        src_refs, land_refs = refs[:n], refs[n:2 * n]
        ss, rs = refs[2 * n:3 * n], refs[3 * n:4 * n]
        xi, yi, ci = _coords()
        for t in range(n):
            for k, (src, _, peer, landed) in enumerate(plans[t](xi, yi, ci, src_refs[t], land_refs[t])):
                cp = _remote(src, landed, ss[t].at[k], rs[t].at[k], peer)
                cp.wait_send()
                cp.wait_recv()

    bufs = list(srcs) + list(lands)
    outs = pl.pallas_call(
        body,
        name=name,
        in_specs=[HBM_SPEC] * (2 * n) + [SEM_SPEC] * (2 * n) + [ANY_SPEC],
        out_specs=[HBM_SPEC] * (2 * n),
        out_shape=[pltpu.HBM(a.shape, a.dtype) for a in bufs],
        input_output_aliases={i: i for i in range(2 * n)},
        compiler_params=pltpu.CompilerParams(has_side_effects=EFFECT),
    )(*bufs, *ssems, *rsems, after)
    return outs[:n], outs[n:]


def _forward_halves(lands, axes, *, name):
    n = len(lands)

    def body(*refs):
        ins, outs = refs[:n], refs[n:2 * n]
        send_sems, recv_sems = refs[2 * n:]
        xi, yi, ci = _coords()
        me = 2 * xi + yi
        sibling = (xi, yi, 1 - ci)
        cps = []
        for t in range(n):
            for d in (1, 2, 3):
                slot = jnp.bitwise_xor(me, d)
                k = COPIES * t + d - 1
                cp = _remote(_half(ins[t].at[slot], axes[t], ci), _half(outs[t].at[slot], axes[t], ci),
                             send_sems.at[k], recv_sems.at[k], sibling)
                cp.start()
                cps.append(cp)
        for t in range(n):
            for d in (1, 2, 3):
                got = _half(outs[t].at[jnp.bitwise_xor(me, d)], axes[t], 1 - ci)
                k = COPIES * t + d - 1
                _remote(got, got, send_sems.at[k], recv_sems.at[k], sibling).wait_recv()
        for cp in cps:
            cp.wait_send()

    return pl.pallas_call(
        body,
        name=name,
        in_specs=[HBM_SPEC] * n,
        out_specs=[HBM_SPEC] * n,
        out_shape=[jax.ShapeDtypeStruct(a.shape, a.dtype) for a in lands],
        input_output_aliases={i: i for i in range(n)},
        scratch_shapes=[pltpu.SemaphoreType.DMA((COPIES * n,)), pltpu.SemaphoreType.DMA((COPIES * n,))],
    )(*lands)


def _swap_other_halves(gs, axes, *, name):
    n = len(gs)

    def body(*refs):
        ins, lands = refs[:n], refs[n:2 * n]
        send_sems, recv_sems = refs[2 * n:]
        xi, yi, ci = _coords()
        sibling = (xi, yi, 1 - ci)
        cps = []
        for t in range(n):
            cp = _remote(_half(ins[t], axes[t], 1 - ci, lead=1), lands[t], send_sems.at[t], recv_sems.at[t], sibling)
            cp.start()
            cps.append(cp)
        for cp in cps:
            cp.wait_recv()
        for cp in cps:
            cp.wait_send()

    return pl.pallas_call(
        body,
        in_specs=[HBM_SPEC] * n,
        out_specs=[HBM_SPEC] * n,
        out_shape=[jax.ShapeDtypeStruct(_halved_shape(g.shape, ax), g.dtype) for g, ax in zip(gs, axes)],
        scratch_shapes=[pltpu.SemaphoreType.DMA((n,)), pltpu.SemaphoreType.DMA((n,))],
        name=name,
    )(*gs)


def _row_tile(rows, cap=512, mult=16):
    best = mult
    for cand in range(mult, min(rows, cap) + 1, mult):
        if rows % cand == 0:
            best = cand
    assert rows % best == 0, rows
    return best


COL_TILE = 256


def _half_tiles(hr, hc, axis, cap=512, mult=16):
    if axis == 0:
        tr = _row_tile(hr, cap, mult)
        n = hr // tr
        return (tr, hc), n, lambda half, i: (half * n + i, 0)
    n = hc // COL_TILE
    return (hr, COL_TILE), n, lambda half, i: (0, half * n + i)


def _add_core_halves(g, land, where, axis):
    nslot, hr, hc = land.shape
    bshape, nr, idx = _half_tiles(hr, hc, axis)

    def body(where_ref, g_ref, l_ref, f_ref, b_ref):
        s = g_ref[...] + l_ref[...].astype(F32)
        b_ref[...] = s.astype(BF16)

        @pl.when(pl.program_id(1) == where_ref[1])
        def _():
            f_ref[...] = s

    blk = pl.BlockSpec((None,) + bshape, lambda i, s, w: (s,) + idx(0, i))
    mine = pl.BlockSpec((None,) + bshape, lambda i, s, w: (s,) + idx(w[0], i))
    return pl.pallas_call(
        body,
        grid_spec=pltpu.PrefetchScalarGridSpec(
            num_scalar_prefetch=1,
            grid=(nr, nslot),
            in_specs=[mine, blk],
            out_specs=[pl.BlockSpec(bshape, lambda i, s, w: idx(0, i)), blk],
        ),
        out_shape=[jax.ShapeDtypeStruct((hr, hc), F32), jax.ShapeDtypeStruct(land.shape, BF16)],
        compiler_params=_cparams(("parallel", "arbitrary")),
        name="add_core_halves",
    )(where, g, land)


def _add_chip_sums(pf, land, where, axis):
    hr, cols = pf.shape
    bshape, nr, idx = _half_tiles(hr, cols, axis)

    def body(where_ref, p_ref, l_ref, o_ref):
        acc = p_ref[...]
        for d in range(3):
            acc = acc + l_ref[d].astype(F32)
        o_ref[...] = acc

    return pl.pallas_call(
        body,
        grid_spec=pltpu.PrefetchScalarGridSpec(
            num_scalar_prefetch=1,
            grid=(nr,),
            in_specs=[pl.BlockSpec(bshape, lambda i, w: idx(0, i)),
                      pl.BlockSpec((3,) + bshape, lambda i, w: (0,) + idx(0, i))],
            out_specs=pl.BlockSpec(bshape, lambda i, w: idx(0, i)),
        ),
        out_shape=jax.ShapeDtypeStruct((hr, cols), F32),
        compiler_params=_cparams(("parallel",)),
        name="add_chip_sums",
    )(where, pf, land)


def _swap_reduced_halves(rs):
    n = len(rs)

    def body(*refs):
        ins, outs = refs[:n], refs[n:2 * n]
        send_sems, recv_sems = refs[2 * n:]
        xi, yi, ci = _coords()
        sibling = (xi, yi, 1 - ci)
        cps = [_remote(ins[t], outs[t], send_sems.at[t], recv_sems.at[t], sibling) for t in range(n)]
        for cp in cps:
            cp.start()
        for cp in cps:
            cp.wait_recv()
        for cp in cps:
            cp.wait_send()

    return pl.pallas_call(
        body,
        in_specs=[HBM_SPEC] * n,
        out_specs=[HBM_SPEC] * n,
        out_shape=[jax.ShapeDtypeStruct(r.shape, r.dtype) for r in rs],
        scratch_shapes=[pltpu.SemaphoreType.DMA((n,)), pltpu.SemaphoreType.DMA((n,))],
        name="swap_reduced_halves",
    )(*rs)


N_DEV = 8


def _all_reduce_small(part):
    r, w = part.shape

    def body(p_ref, o_ref, gath, send_sems, recv_sems):
        xi, yi, ci = _coords()
        me = 4 * xi + 2 * yi + ci
        gath[me] = p_ref[...]
        cps = []
        for d in range(1, N_DEV):
            peer = (jnp.bitwise_xor(xi, d >> 2), jnp.bitwise_xor(yi, (d >> 1) & 1), jnp.bitwise_xor(ci, d & 1))
            cp = _remote(p_ref, gath.at[me], send_sems.at[d - 1], recv_sems.at[d - 1], peer)
            cp.start()
            cps.append(cp)
        for d in range(1, N_DEV):
            src = gath.at[jnp.bitwise_xor(me, d)]
            _remote(src, src, send_sems.at[d - 1], recv_sems.at[d - 1], (xi, yi, ci)).wait_recv()
        acc = gath[0]
        for k in range(1, N_DEV):
            acc = acc + gath[k]
        o_ref[...] = acc
        for cp in cps:
            cp.wait_send()

    vm = pl.BlockSpec(memory_space=pltpu.VMEM)
    return pl.pallas_call(
        body,
        in_specs=[vm],
        out_specs=vm,
        out_shape=jax.ShapeDtypeStruct((r, w), F32),
        scratch_shapes=[pltpu.VMEM((N_DEV, r, w), F32), pltpu.SemaphoreType.DMA((N_DEV - 1,)),
                        pltpu.SemaphoreType.DMA((N_DEV - 1,))],
        name="all_reduce_small",
    )(part)


def _adamw_math(wv, gv, mv, vv):
    mn = ADAM_B1 * mv + (1.0 - ADAM_B1) * gv
    vn = ADAM_B2 * vv + (1.0 - ADAM_B2) * (gv * gv)
    m_hat = mn / (1.0 - ADAM_B1 ** ADAM_STEP)
    v_hat = vn / (1.0 - ADAM_B2 ** ADAM_STEP)
    return -ADAM_LR * (m_hat / (jnp.sqrt(v_hat) + ADAM_EPS) + ADAM_WD * wv), mn, vn


def _adamw_halves(w, g_mine, g_other, m, v, where, axis, *, name):
    rows, cols = w.shape
    hr, hc = g_mine.shape
    bshape, nr, idx = _half_tiles(hr, hc, axis, cap=256, mult=8)

    def body(where_ref, w_ref, gm_ref, go_ref, m_ref, v_ref, g_ref, d_ref, nm_ref, nv_ref):
        is_mine = pl.program_id(0) == where_ref[0]
        gv = jnp.where(is_mine, gm_ref[...], go_ref[...])
        g_ref[...] = gv
        d_ref[...], nm_ref[...], nv_ref[...] = _adamw_math(w_ref[...], gv, m_ref[...], v_ref[...])

    def parked(half, i, holder):
        return idx(0, jnp.where(half == holder, i, jnp.where(half < holder, 0, nr - 1)))

    blk = pl.BlockSpec(bshape, lambda hf, i, wh: idx(hf, i))
    o = jax.ShapeDtypeStruct((rows, cols), F32)
    return pl.pallas_call(
        body,
        grid_spec=pltpu.PrefetchScalarGridSpec(
            num_scalar_prefetch=1,
            grid=(2, nr),
            in_specs=[blk, pl.BlockSpec(bshape, lambda hf, i, wh: parked(hf, i, wh[0])),
                      pl.BlockSpec(bshape, lambda hf, i, wh: parked(hf, i, 1 - wh[0])), blk, blk],
            out_specs=[blk] * 4,
        ),
        out_shape=[o, o, o, o],
        compiler_params=_cparams(("arbitrary", "arbitrary")),
        name=name,
    )(where, w, g_mine, g_other, m, v)


def _adamw(w, g, m, v, *, name):
    rows, cols = w.shape
    tr = _row_tile(rows, cap=256, mult=8)

    def body(w_ref, g_ref, m_ref, v_ref, d_ref, nm_ref, nv_ref):
        d_ref[...], nm_ref[...], nv_ref[...] = _adamw_math(w_ref[...], g_ref[...], m_ref[...], v_ref[...])

    blk = pl.BlockSpec((tr, cols), lambda i: (i, 0))
    o = jax.ShapeDtypeStruct((rows, cols), F32)
    return pl.pallas_call(
        body,
        grid=(rows // tr,),
        in_specs=[blk] * 4,
        out_specs=[blk] * 3,
        out_shape=[o, o, o],
        compiler_params=_cparams(("parallel",)),
        name=name,
    )(w, g, m, v)


def _pack(arrs):
    flat = jnp.concatenate([a.reshape(-1) for a in arrs])
    pad = (-flat.shape[0]) % (8 * LANES)
    return jnp.pad(flat, (0, pad)).reshape(-1, LANES)


def _unpack(packed, shapes):
    flat = packed.reshape(-1)
    out, off = [], 0
    for s in shapes:
        n = 1
        for dim in s:
            n *= dim
        out.append(flat[off:off + n].reshape(s))
        off += n
    return out


def _pad_rows(a, rows):
    return jnp.pad(a, ((0, rows - a.shape[0]), (0, 0)))


def _pad_lanes(a, lanes=LANES):
    return jnp.pad(a, ((0, 0), (0, lanes - a.shape[1])))


def _local_grads(x2d, tgt2d, prm, get_w, on_grads, *, nb, seq, dep=None):
    g_pre, g_post, g_fpre, g_fpost = prm["norm_mix_pre"], prm["norm_mix_post"], prm["norm_ffn_pre"], prm["norm_ffn_post"]
    dt_bias, a_log = _pad_lanes(prm["ssd_dt_bias"]), _pad_lanes(prm["ssd_a_log"])
    d_lanes = jnp.repeat(prm["ssd_d"], HEAD_DIM, axis=1)

    h = _rms_fwd(x2d, g_pre, dep=dep, out_dtype=BF16, name="rms_mix_pre")
    t, d = x2d.shape
    w_in_t, w_dt_t, cw, sw = get_w("in", h)
    proj = _matmul([(h, w_in_t)], mode="nt", out_dtype=F32, tm=1024, tn=1024, tk=2048, name="mm_proj",
                   extent=(t, D_MAIN, d))
    dtr = _matmul([(h, w_dt_t)], mode="nt", out_dtype=F32, tm=1024, tn=128, tk=2048, name="mm_dt")
    u1, uy = _conv_branch_fwd(proj, cw, prm["conv_dw_b"], prm["conv_ln_g"], prm["conv_ln_b"], nb=nb, seq=seq)
    xs_all = _ssd_pre_fwd(proj, sw, prm["ssd_conv_b"], nb=nb, seq=seq)
    uy, ys, states = _ssd_fwd(uy, xs_all, proj, dtr, dt_bias, a_log, d_lanes, prm["ssd_norm_w"], nb=nb, seq=seq)
    w_out = get_w("out", uy)
    mix = _matmul([(uy, w_out)], mode="nn", out_dtype=F32, tm=1024, tn=1024, tk=2048, name="mm_mix")
    x1, h2 = _rms_post_pre(mix, x2d, g_post, g_fpre)
    w_gate, w_up = get_w("up", h2)
    gt, up, act = _ffn_up(h2, w_gate, w_up, tm=1024, tn=512)
    w_down = get_w("down", act)
    dx2, df, loss, dg_fpost = _down_loss(act, w_down, x1, tgt2d, g_fpost, tk=1408)

    dgt, dup = _ffn_bwd_act(df, w_down, gt, up, tm=1024, tn=512)
    dw_down = _matmul([(act, df)], mode="tn", out_dtype=F32, tm=1408, tn=1024, tk=2048, name="mm_dw_down",
                      also_bf16=True)
    dw_gate = _matmul([(h2, dgt)], mode="tn", out_dtype=F32, tm=1024, tn=1408, tk=2048, name="mm_dw_gate",
                      slot_out=True, also_bf16=True)
    dw_up = _matmul([(h2, dup)], mode="tn", out_dtype=F32, tm=1024, tn=1408, tk=2048, name="mm_dw_up",
                    slot_out=True, also_bf16=True)
    dep = on_grads("ffn", (dw_down, dw_gate, dw_up))
    dh2 = _matmul([(dgt, w_gate), (dup, w_up)], mode="nt", out_dtype=F32, tm=1024, tn=1024, tk=1408, name="mm_dh2",
                  dep=dep)
    dx1, dmix, dg_fpre, dg_post = _rms_bwd_pre_post(dh2, x1, g_fpre, dx2, mix, g_post)
    dw_out = _matmul([(uy, dmix)], mode="tn", out_dtype=F32, tm=1024, tn=1024, tk=2048, name="mm_dw_out",
                     also_bf16=True)
    dep = on_grads("out", (dw_out,))
    duy = _matmul([(dmix, w_out)], mode="nt", out_dtype=F32, tm=1024, tn=1024, tk=2048, name="mm_duy", dep=dep)
    dproj, dcw, dcb, dlg, dlb = _conv_branch_bwd(duy, u1, proj, cw, prm["conv_ln_g"], prm["conv_ln_b"], nb=nb, seq=seq)
    dproj, dxs, ddtr, ssd_small = _ssd_bwd(dproj, duy, proj, ys, xs_all, dtr, states, dt_bias, a_log, d_lanes,
                                           prm["ssd_norm_w"], nb=nb, seq=seq)
    dproj, dsw, dsb = _ssd_pre_bwd(dproj, dxs, proj, sw, prm["ssd_conv_b"], nb=nb, seq=seq)
    ddtr_b = ddtr.astype(BF16)
    dw_in_t = _matmul([(dproj, h)], mode="tn", out_dtype=F32, tm=1024, tn=1024, tk=2048, name="mm_dw_main",
                      extent=(D_MAIN, d, t), out_rows=D_IN)
    dw_in_t = _dw_dt_rows(dw_in_t, ddtr_b, h)
    dep = on_grads("in", ((dw_in_t, dw_in_t),))
    dx, dg_pre = _dh_dx(dproj, w_in_t, ddtr_b, w_dt_t, x2d, dx1, g_pre, tk=1280, dep=g_pre if dep is None else dep)

    grads = {
        "norm_mix_pre": dg_pre,
        "w_in": dw_in_t,
        "conv_dw_w": dcw[:CONV_K], "conv_dw_b": dcb, "conv_ln_g": dlg, "conv_ln_b": dlb,
        "ssd_conv_w": dsw[:SSD_CONV_K], "ssd_conv_b": dsb,
        "ssd_dt_bias": ssd_small[2:3, :HEADS], "ssd_a_log": ssd_small[3:4, :HEADS], "ssd_d": ssd_small[1:2, :HEADS],
        "ssd_norm_w": ssd_small[0:1],
        "w_out": dw_out[0],
        "norm_mix_post": dg_post, "norm_ffn_pre": dg_fpre,
        "w_gate": dw_gate[0], "w_up": dw_up[0],
        "w_down": dw_down[0], "norm_ffn_post": dg_fpost,
    }
    return loss, dx, grads


BIG = ("w_in", "w_out", "w_gate", "w_up", "w_down")
HALF_AXIS = {"w_in": 1, "w_out": 0, "w_gate": 0, "w_up": 0, "w_down": 0}
GATHER_STAGES = {"in": ("w_in", "conv_dw_w", "ssd_conv_w"), "out": ("w_out",), "up": ("w_gate", "w_up"),
                 "down": ("w_down",)}
GATHER_ORDER = tuple(n for st in ("in", "out", "up", "down") for n in GATHER_STAGES[st])
SMALL = ("norm_mix_pre", "conv_dw_w", "conv_dw_b", "conv_ln_g", "conv_ln_b", "ssd_conv_w", "ssd_conv_b", "ssd_dt_bias",
         "ssd_a_log", "ssd_d", "ssd_norm_w", "norm_mix_post", "norm_ffn_pre", "norm_ffn_post")
WEIGHTS = ("norm_mix_pre", "w_in", "conv_dw_w", "conv_dw_b", "conv_ln_g", "conv_ln_b", "ssd_conv_w", "ssd_conv_b",
           "ssd_dt_bias", "ssd_a_log", "ssd_d", "ssd_norm_w", "w_out", "norm_mix_post", "norm_ffn_pre", "w_gate", "w_up",
           "w_down", "norm_ffn_post")


def _cols_from_slots(a):
    n, rows, w = a.shape
    return a.transpose(1, 0, 2).reshape(rows, n * w)


def kernel(x, norm_mix_pre, w_in, conv_dw_w, conv_dw_b, conv_ln_g, conv_ln_b, ssd_conv_w, ssd_conv_b, ssd_dt_bias, ssd_a_log, ssd_d, ssd_norm_w, w_out, norm_mix_post, norm_ffn_pre, w_gate, w_up, w_down, norm_ffn_post, loss_target, m_norm_mix_pre, m_w_in, m_conv_dw_w, m_conv_dw_b, m_conv_ln_g, m_conv_ln_b, m_ssd_conv_w, m_ssd_conv_b, m_ssd_dt_bias, m_ssd_a_log, m_ssd_d, m_ssd_norm_w, m_w_out, m_norm_mix_post, m_norm_ffn_pre, m_w_gate, m_w_up, m_w_down, m_norm_ffn_post, v_norm_mix_pre, v_w_in, v_conv_dw_w, v_conv_dw_b, v_conv_ln_g, v_conv_ln_b, v_ssd_conv_w, v_ssd_conv_b, v_ssd_dt_bias, v_ssd_a_log, v_ssd_d, v_ssd_norm_w, v_w_out, v_norm_mix_post, v_norm_ffn_pre, v_w_gate, v_w_up, v_w_down, v_norm_ffn_post):
    args = dict(locals())
    two_d = lambda n, a: jnp.swapaxes(a, 1, 2)[0] if n == "w_in" else a.reshape(a.shape[-2:])
    wts = {n: two_d(n, args[n]) for n in WEIGHTS}
    ms = {n: two_d(n, args["m_" + n]) for n in WEIGHTS}
    vs = {n: two_d(n, args["v_" + n]) for n in WEIGHTS}
    nb, seq, d = x.shape
    t = nb * seq
    xi, yi, ci = _coords()
    chip = 2 * xi + yi
    where = jnp.stack([ci, chip]).astype(jnp.int32)

    shards = {n: wts[n].astype(BF16) for n in BIG}
    shards.update(conv_dw_w=_pad_rows(wts["conv_dw_w"], 32), ssd_conv_w=_pad_rows(wts["ssd_conv_w"], 8))
    srcs = [shards[n] for n in GATHER_ORDER]
    plans = [_gather_plan(HALF_AXIS.get(n)) for n in GATHER_ORDER]
    ssems, rsems, srcs, lands, token = _split_start(
        srcs, [lax.empty((N_CHIPS,) + s.shape, s.dtype) for s in srcs], plans, name="gather_start")

    def get_w(stage, after):
        names = GATHER_STAGES[stage]
        pick = lambda seq_: [seq_[GATHER_ORDER.index(n)] for n in names]
        own, got = _split_wait(pick(ssems), pick(rsems), pick(srcs), pick(lands), pick(plans), after,
                               name="gather_wait_" + stage)
        got, own = dict(zip(names, got)), dict(zip(names, own))
        big = [n for n in names if n in BIG]
        got.update(zip(big, _forward_halves([got[n] for n in big], [HALF_AXIS[n] for n in big],
                                            name="gather_forward_" + stage)))
        full = {n: lax.dynamic_update_slice(got[n], own[n][None], (chip, 0, 0)) for n in got}
        if stage == "in":
            w_in_t = full["w_in"].reshape(D_IN, D_MODEL)
            return (w_in_t, _pad_rows(w_in_t[D_MAIN:], LANES), _cols_from_slots(full["conv_dw_w"]),
                    _cols_from_slots(full["ssd_conv_w"]))
        if stage == "out":
            return full["w_out"].reshape(D_MODEL, D_MODEL)
        if stage == "up":
            return _cols_from_slots(full["w_gate"]), _cols_from_slots(full["w_up"])
        return full["w_down"].reshape(D_FF, D_MODEL)

    reduce_groups = {"ffn": ("w_down", "w_gate", "w_up"), "out": ("w_out",), "in": ("w_in",)}
    in_flight = {}

    def on_grads(stage, gs):
        names = reduce_groups[stage]
        axes = [HALF_AXIS[n] for n in names]
        slot = lambda g: g if g.ndim == 3 else g.reshape((N_CHIPS, g.shape[0] // N_CHIPS, g.shape[1]))
        kept = _swap_other_halves([slot(b16) for _, b16 in gs], axes, name="swap_other_halves_" + stage)
        sums = [_add_core_halves(slot(f32), l, where, ax) for (f32, _), l, ax in zip(gs, kept, axes)]
        ps = [s[1] for s in sums]
        ssem, rsem, ps, recv, started = _split_start(
            ps, [lax.empty((COPIES,) + p.shape[1:], p.dtype) for p in ps], [_owners_plan] * len(ps),
            name="owners_start_" + stage)
        in_flight[stage] = (ssem, rsem, ps, recv, [s[0] for s in sums])
        return started

    prm = {n: wts[n] for n in SMALL}
    loss, dx, grads = _local_grads(x.reshape(t, d), loss_target.reshape(t, d), prm, get_w, on_grads,
                                   nb=nb, seq=seq, dep=token)
    loss = lax.psum(loss[0, 0], MESH_AXES)

    halves = {}
    for stage, names in reduce_groups.items():
        ssem, rsem, ps, recv, own_sums = in_flight[stage]
        _, recv = _split_wait(ssem, rsem, ps, recv, [_owners_plan] * len(ps), dx, name="owners_wait_" + stage)
        for n, f32_sum, r in zip(names, own_sums, recv):
            halves[n] = _add_chip_sums(f32_sum, r, where, HALF_AXIS[n])
    halves = [halves[n] for n in BIG]
    other_halves = _swap_reduced_halves(halves)

    small_shapes = [grads[n].shape for n in SMALL]
    small_sum = _unpack(_all_reduce_small(_pack([grads[n] for n in SMALL])), small_shapes)
    small_grads = dict(zip(SMALL, small_sum))
    cwid, swid = D_CONV // N_CHIPS, D_XBC // N_CHIPS
    small_grads["conv_dw_w"] = lax.dynamic_slice(small_grads["conv_dw_w"], (0, chip * cwid), (CONV_K, cwid))
    small_grads["ssd_conv_w"] = lax.dynamic_slice(small_grads["ssd_conv_w"], (0, chip * swid), (SSD_CONV_K, swid))

    out_g, out_d, out_m, out_v = {}, {}, {}, {}
    for n, mine, other in zip(BIG, halves, other_halves):
        out_g[n], out_d[n], out_m[n], out_v[n] = _adamw_halves(wts[n], mine, other, ms[n], vs[n], where,
                                                                 HALF_AXIS[n], name="adamw_" + n)
    shard_shapes = [wts[n].shape for n in SMALL]
    pd, pm, pv = _adamw(_pack([wts[n] for n in SMALL]), _pack([small_grads[n] for n in SMALL]),
                        _pack([ms[n] for n in SMALL]), _pack([vs[n] for n in SMALL]), name="adamw_small")
    for n, dd, mm, vv in zip(SMALL, _unpack(pd, shard_shapes), _unpack(pm, shard_shapes), _unpack(pv, shard_shapes)):
        out_g[n], out_d[n], out_m[n], out_v[n] = small_grads[n], dd, mm, vv

    back = lambda n, a: jnp.swapaxes(a[None], 1, 2) if n == "w_in" else a.reshape(args[n].shape)
    outs = [back(n, o[n]) for o in (out_g, out_d, out_m, out_v) for n in WEIGHTS]
    return (loss, dx.reshape(nb, seq, d), *outs)
```

```python
import functools

import jax
import jax.numpy as jnp
from jax import lax
from jax.experimental import pallas as pl
from jax.experimental.pallas import tpu as pltpu

F32 = jnp.float32
BF16 = jnp.bfloat16
EPS = 1e-6

D_MODEL = 2048
D_CONV = 1024
D_SSD = 1024
D_XBC = 2048
HEADS = 16
HEAD_DIM = 64
GROUPS = 4
STATE = 128
CONV_K = 31
SSD_CONV_K = 4
D_FF = 5632
D_MAIN = 2 * D_CONV + D_SSD + D_XBC
D_IN = D_MAIN + HEADS
N_CHIPS = 4
LANES = 128
CHUNK = 128
PAIRS = HEADS // 2

ADAM_LR = 0.001
ADAM_B1 = 0.9
ADAM_B2 = 0.999
ADAM_EPS = 1e-08
ADAM_WD = 0.01
ADAM_STEP = 10

MESH_AXES = ("x", "y", "c")
VMEM_LIMIT = 56 * 1024 * 1024


def _sig(v):
    return 1.0 / (1.0 + jnp.exp(-v))


def _cparams(sem, vmem=VMEM_LIMIT):
    return pltpu.CompilerParams(dimension_semantics=sem, vmem_limit_bytes=vmem)


_DIMS = {"nn": ((1,), (0,)), "nt": ((1,), (1,)), "tn": ((0,), (0,))}


def _matmul(pairs, *, mode, out_dtype, tm, tn, tk, name, slot_out=False, extent=None, out_rows=None, dep=None,
            also_bf16=False):
    a0, b0 = pairs[0]
    if mode == "nn":
        (m, k), n = a0.shape, b0.shape[1]
    elif mode == "nt":
        (m, k), n = a0.shape, b0.shape[0]
    else:
        (k, m), n = a0.shape, b0.shape[1]
    if extent is not None:
        m, n, k = extent
    tm, tn, tk = min(tm, m), min(tn, n), min(tk, k)
    assert m % tm == 0 and n % tn == 0 and k % tk == 0, (name, m, n, k, tm, tn, tk)
    nk = k // tk
    npairs = len(pairs)
    deps = [] if dep is None else [dep]
    dims = (_DIMS[mode], ((), ()))

    use_scratch = nk > 1 and out_dtype != F32

    def body(*refs):
        ins, o_ref = refs[: 2 * npairs], refs[2 * npairs + len(deps)]
        dot = lambda p: lax.dot_general(ins[2 * p][...], ins[2 * p + 1][...], dims, preferred_element_type=F32)
        if nk == 1:
            part = dot(0)
            for p in range(1, npairs):
                part = part + dot(p)
            o_ref[...] = part.astype(out_dtype)
            if also_bf16:
                refs[2 * npairs + len(deps) + 1][...] = part.astype(BF16)
            return
        acc = refs[-1] if use_scratch else o_ref
        kk = pl.program_id(2)

        @pl.when(kk == 0)
        def _():
            acc[...] = jnp.zeros_like(acc)

        for p in range(npairs):
            acc[...] += dot(p)

        if use_scratch:
            @pl.when(kk == nk - 1)
            def _():
                o_ref[...] = acc[...].astype(out_dtype)

        if also_bf16:
            @pl.when(kk == nk - 1)
            def _():
                refs[2 * npairs + len(deps) + 1][...] = acc[...].astype(BF16)

    if mode == "nn":
        a_spec = pl.BlockSpec((tm, tk), lambda i, j, kk: (i, kk))
        b_spec = pl.BlockSpec((tk, tn), lambda i, j, kk: (kk, j))
    elif mode == "nt":
        a_spec = pl.BlockSpec((tm, tk), lambda i, j, kk: (i, kk))
        b_spec = pl.BlockSpec((tn, tk), lambda i, j, kk: (j, kk))
    else:
        a_spec = pl.BlockSpec((tk, tm), lambda i, j, kk: (kk, i))
        b_spec = pl.BlockSpec((tk, tn), lambda i, j, kk: (kk, j))
    if slot_out:
        out_shape = jax.ShapeDtypeStruct((n // tn, m, tn), out_dtype)
        out_spec = pl.BlockSpec((None, tm, tn), lambda i, j, kk: (j, i, 0))
    else:
        out_shape = jax.ShapeDtypeStruct((m if out_rows is None else out_rows, n), out_dtype)
        out_spec = pl.BlockSpec((tm, tn), lambda i, j, kk: (i, j))
    flat = [t for ab in pairs for t in ab]
    if also_bf16:
        out_spec = [out_spec, out_spec]
        out_shape = [out_shape, jax.ShapeDtypeStruct(out_shape.shape, BF16)]
    return pl.pallas_call(
        body,
        grid=(m // tm, n // tn, nk),
        in_specs=[a_spec, b_spec] * npairs + [pl.BlockSpec(memory_space=pl.ANY)] * len(deps),
        out_specs=out_spec,
        out_shape=out_shape,
        scratch_shapes=[pltpu.VMEM((tm, tn), F32)] if use_scratch else [],
        compiler_params=_cparams(("parallel", "parallel", "arbitrary")),
        name=name,
    )(*flat, *deps)


SUB_ROWS = 256


def _ffn_up(h2, wg, wu, *, tm, tn):
    t, k = h2.shape
    n = wg.shape[1]
    tm = min(tm, t)
    assert t % tm == 0 and n % tn == 0, (t, n, tm, tn)

    sub = min(SUB_ROWS, tm)

    def body(h_ref, wg_ref, wu_ref, g_ref, u_ref, a_ref):
        for r in range(tm // sub):
            rows = pl.ds(r * sub, sub)
            hv = h_ref[rows, :]
            g = jnp.dot(hv, wg_ref[...], preferred_element_type=F32)
            u = jnp.dot(hv, wu_ref[...], preferred_element_type=F32)
            g_ref[rows, :] = g.astype(BF16)
            u_ref[rows, :] = u.astype(BF16)
            a_ref[rows, :] = (g * _sig(g) * u).astype(BF16)

    o = jax.ShapeDtypeStruct((t, n), BF16)
    ospec = pl.BlockSpec((tm, tn), lambda i, j: (i, j))
    return pl.pallas_call(
        body,
        grid=(t // tm, n // tn),
        in_specs=[pl.BlockSpec((tm, k), lambda i, j: (i, 0)), pl.BlockSpec((k, tn), lambda i, j: (0, j)),
                  pl.BlockSpec((k, tn), lambda i, j: (0, j))],
        out_specs=[ospec, ospec, ospec],
        out_shape=[o, o, o],
        compiler_params=_cparams(("parallel", "parallel")),
        name="ffn_up",
    )(h2, wg, wu)


def _ffn_bwd_act(df, wd, gt, up, *, tm, tn):
    t, k = df.shape
    n = wd.shape[0]
    tm = min(tm, t)
    assert t % tm == 0 and n % tn == 0, (t, n, tm, tn)

    sub = min(SUB_ROWS, tm)

    def body(df_ref, wd_ref, g_ref, u_ref, dg_ref, du_ref):
        for r in range(tm // sub):
            rows = pl.ds(r * sub, sub)
            da = lax.dot_general(df_ref[rows, :], wd_ref[...], (_DIMS["nt"], ((), ())), preferred_element_type=F32)
            g = g_ref[rows, :].astype(F32)
            u = u_ref[rows, :].astype(F32)
            s = _sig(g)
            dg_ref[rows, :] = (da * u * s * (1.0 + g * (1.0 - s))).astype(BF16)
            du_ref[rows, :] = (da * g * s).astype(BF16)

    o = jax.ShapeDtypeStruct((t, n), BF16)
    blk = pl.BlockSpec((tm, tn), lambda i, j: (i, j))
    return pl.pallas_call(
        body,
        grid=(t // tm, n // tn),
        in_specs=[pl.BlockSpec((tm, k), lambda i, j: (i, 0)), pl.BlockSpec((tn, k), lambda i, j: (j, 0)), blk, blk],
        out_specs=[blk, blk],
        out_shape=[o, o],
        compiler_params=_cparams(("parallel", "parallel")),
        name="ffn_bwd_act",
    )(df, wd, gt, up)


def _dw_dt_rows(dw_in_t, ddtr_b, h, *, tk=1024):
    t, d = h.shape
    tk = min(tk, t)
    nk = t // tk

    def body(buf_ref, d_ref, h_ref, o_ref, acc):
        kk = pl.program_id(0)

        @pl.when(kk == 0)
        def _():
            acc[...] = jnp.zeros_like(acc)

        acc[...] += lax.dot_general(d_ref[...], h_ref[...], (_DIMS["tn"], ((), ())), preferred_element_type=F32)

        @pl.when(kk == nk - 1)
        def _():
            o_ref[...] = acc[0:HEADS, :]

    return pl.pallas_call(
        body,
        grid=(nk,),
        in_specs=[DEP_SPEC, pl.BlockSpec((tk, LANES), lambda kk: (kk, 0)), pl.BlockSpec((tk, d), lambda kk: (kk, 0))],
        out_specs=pl.BlockSpec((HEADS, d), lambda kk: (D_MAIN // HEADS, 0)),
        out_shape=jax.ShapeDtypeStruct(dw_in_t.shape, F32),
        input_output_aliases={0: 0},
        scratch_shapes=[pltpu.VMEM((LANES, d), F32)],
        compiler_params=_cparams(("arbitrary",)),
        name="mm_dw_dt",
    )(dw_in_t, ddtr_b, h)


ROW_TILE = 256


DEP_SPEC = pl.BlockSpec(memory_space=pl.ANY)


def _rms_fwd(xv, g, *, res=None, dep=None, out_dtype, name):
    t, d = xv.shape
    has_res = res is not None
    deps = [] if dep is None else [dep]

    def body(*refs):
        x_ref, g_ref = refs[0], refs[1]
        o_ref = refs[-1]
        v = x_ref[...]
        r = lax.rsqrt(jnp.mean(v * v, axis=-1, keepdims=True) + EPS)
        y = v * r * g_ref[...]
        if has_res:
            y = refs[2][...] + y
        o_ref[...] = y.astype(out_dtype)

    row = pl.BlockSpec((ROW_TILE, d), lambda i: (i, 0))
    vec = pl.BlockSpec((1, d), lambda i: (0, 0))
    return pl.pallas_call(
        body,
        grid=(t // ROW_TILE,),
        in_specs=[row, vec] + ([row] if has_res else []) + [DEP_SPEC] * len(deps),
        out_specs=row,
        out_shape=jax.ShapeDtypeStruct((t, d), out_dtype),
        compiler_params=_cparams(("parallel",)),
        name=name,
    )(*([xv, g] + ([res] if has_res else []) + deps))


def _rms_bwd_rows(dy, v, gv):
    r = lax.rsqrt(jnp.mean(v * v, axis=-1, keepdims=True) + EPS)
    xh = v * r
    gdy = dy * gv
    dx = r * (gdy - xh * jnp.mean(gdy * xh, axis=-1, keepdims=True))
    return dx, jnp.sum(dy * xh, axis=0, keepdims=True)


FUSED_ROWS = 512


def _matmul_rows_tail(a, w, tail, *, tk, row_ins, vec_ins, row_outs, vec_outs, first=None, dep=None, name):
    t, kdim = a.shape
    d = w.shape[1]
    tm, tk = min(FUSED_ROWS, t), min(tk, kdim)
    nk, nb = kdim // tk, t // tm
    sub = tm // nk
    assert t % tm == 0 and kdim % tk == 0 and tm % nk == 0 and sub % 8 == 0
    n_ri, n_vi, n_ro, n_vo = len(row_ins), len(vec_ins), len(row_outs), len(vec_outs)
    n_first = 0 if first is None else 2
    deps = [] if dep is None else [dep]

    def body(*refs):
        a_ref, w_ref = refs[0], refs[1]
        first_refs = refs[2:2 + n_first]
        p = 2 + n_first
        ri = refs[p:p + n_ri]
        vi = refs[p + n_ri:p + n_ri + n_vi]
        p += n_ri + n_vi + len(deps)
        ro = refs[p:p + n_ro]
        vo = refs[p + n_ro:p + n_ro + n_vo]
        accs = refs[-2:]
        i, kk = pl.program_id(0), pl.program_id(1)

        @pl.when(jnp.logical_and(i == 0, kk == 0))
        def _():
            accs[1][...] = jnp.zeros_like(accs[1])
            for ref in vo:
                ref[...] = jnp.zeros_like(ref)

        def tail_quarter(prev):
            rows = pl.ds(pl.multiple_of(kk * sub, sub), sub)
            outs, parts = tail(prev[rows, :], [r[rows, :] for r in ri], [v[...] for v in vi])
            for ref, val in zip(ro, outs):
                ref[rows, :] = val.astype(ref.dtype)
            for ref, part in zip(vo, parts):
                ref[...] += jnp.where(i > 0, part, 0.0)

        for parity in (0, 1):
            cur, prev = accs[parity], accs[1 - parity]
            mine = jnp.logical_and(i < nb, i % 2 == parity)

            @pl.when(jnp.logical_and(mine, kk == 0))
            def _(cur=cur):
                if first is None:
                    cur[...] = jnp.zeros_like(cur)
                else:
                    cur[...] = jnp.dot(first_refs[0][...], first_refs[1][...], preferred_element_type=F32)

            @pl.when(mine)
            def _(cur=cur, prev=prev):
                cur[...] += jnp.dot(a_ref[...], w_ref[...], preferred_element_type=F32)
                tail_quarter(prev)

        @pl.when(i == nb)
        def _():
            tail_quarter(accs[(nb - 1) % 2])

    behind = lambda i, kk: (jnp.maximum(i - 1, 0), 0)
    row = pl.BlockSpec((tm, d), behind)
    const = lambda shape: pl.BlockSpec(shape, lambda i, kk: (0,) * len(shape))
    in_specs = [pl.BlockSpec((tm, tk), lambda i, kk: (jnp.minimum(i, nb - 1), kk)),
                pl.BlockSpec((tk, d), lambda i, kk: (kk, 0))]
    if first is not None:
        in_specs += [pl.BlockSpec((tm, first[0].shape[1]), lambda i, kk: (jnp.minimum(i, nb - 1), 0)),
                     const(first[1].shape)]
    in_specs += [row] * n_ri + [const(v.shape) for v in vec_ins] + [pl.BlockSpec(memory_space=pl.ANY)] * len(deps)
    return pl.pallas_call(
        body,
        grid=(nb + 1, nk),
        in_specs=in_specs,
        out_specs=[row] * n_ro + [const(sh) for sh in vec_outs],
        out_shape=[jax.ShapeDtypeStruct((t, d), dt) for dt in row_outs]
        + [jax.ShapeDtypeStruct(sh, F32) for sh in vec_outs],
        scratch_shapes=[pltpu.VMEM((tm, d), F32), pltpu.VMEM((tm, d), F32)],
        compiler_params=_cparams(("arbitrary", "arbitrary")),
        name=name,
    )(a, w, *([] if first is None else list(first)), *row_ins, *vec_ins, *deps)


def _down_loss(act, w_down, x1, tgt, g, *, tk):
    d = w_down.shape[1]

    def tail(v, rows, vecs):
        x1v, tv = rows
        gv, = vecs
        fh = v * lax.rsqrt(jnp.mean(v * v, axis=-1, keepdims=True) + EPS)
        e = x1v + fh * gv - tv
        dx2 = e * (1.0 / d)
        df, dg = _rms_bwd_rows(dx2, v, gv)
        loss = 0.5 * jnp.sum(jnp.mean(e * e, axis=-1, keepdims=True), axis=0, keepdims=True)
        return (dx2, df), (loss, dg)

    return _matmul_rows_tail(act, w_down, tail, tk=tk, row_ins=[x1, tgt], vec_ins=[g], row_outs=[F32, BF16],
                             vec_outs=[(1, 1), (1, d)], name="mm_down_loss")


def _dh_dx(dproj, w_in_t, ddtr_b, w_dt_t, xv, dx1, g, *, tk, dep):
    def tail(v, rows, vecs):
        xr, dx1r = rows
        dx, dg = _rms_bwd_rows(v, xr, vecs[0])
        return (dx + dx1r,), (dg,)

    return _matmul_rows_tail(dproj, w_in_t, tail, tk=tk, row_ins=[xv, dx1], vec_ins=[g], row_outs=[F32],
                             vec_outs=[(1, xv.shape[1])], first=(ddtr_b, w_dt_t), dep=dep, name="mm_dh_dx")


def _rms_post_pre(mix, xv, g_post, g_pre):
    t, d = mix.shape

    def body(m_ref, x_ref, gp_ref, gf_ref, x1_ref, h2_ref):
        v = m_ref[...]
        x1 = x_ref[...] + v * lax.rsqrt(jnp.mean(v * v, axis=-1, keepdims=True) + EPS) * gp_ref[...]
        x1_ref[...] = x1
        h2_ref[...] = (x1 * lax.rsqrt(jnp.mean(x1 * x1, axis=-1, keepdims=True) + EPS) * gf_ref[...]).astype(BF16)

    row = pl.BlockSpec((ROW_TILE, d), lambda i: (i, 0))
    vec = pl.BlockSpec((1, d), lambda i: (0, 0))
    return pl.pallas_call(
        body,
        grid=(t // ROW_TILE,),
        in_specs=[row, row, vec, vec],
        out_specs=[row, row],
        out_shape=[jax.ShapeDtypeStruct((t, d), F32), jax.ShapeDtypeStruct((t, d), BF16)],
        compiler_params=_cparams(("parallel",)),
        name="rms_mix_post_ffn_pre",
    )(mix, xv, g_post, g_pre)


def _rms_bwd_pre_post(dh2, x1, g_pre, dx2, mix, g_post):
    t, d = x1.shape

    def body(dh_ref, x1_ref, gf_ref, dx2_ref, m_ref, gp_ref, dx1_ref, dmix_ref, dgf_ref, dgp_ref):
        @pl.when(pl.program_id(0) == 0)
        def _():
            dgf_ref[...] = jnp.zeros_like(dgf_ref)
            dgp_ref[...] = jnp.zeros_like(dgp_ref)

        dx, dgf = _rms_bwd_rows(dh_ref[...], x1_ref[...], gf_ref[...])
        dx1 = dx + dx2_ref[...]
        dx1_ref[...] = dx1
        dmix, dgp = _rms_bwd_rows(dx1, m_ref[...], gp_ref[...])
        dmix_ref[...] = dmix.astype(BF16)
        dgf_ref[...] += dgf
        dgp_ref[...] += dgp

    row = pl.BlockSpec((ROW_TILE, d), lambda i: (i, 0))
    vec = pl.BlockSpec((1, d), lambda i: (0, 0))
    return pl.pallas_call(
        body,
        grid=(t // ROW_TILE,),
        in_specs=[row, row, vec, row, row, vec],
        out_specs=[row, row, vec, vec],
        out_shape=[jax.ShapeDtypeStruct((t, d), F32), jax.ShapeDtypeStruct((t, d), BF16),
                   jax.ShapeDtypeStruct((1, d), F32), jax.ShapeDtypeStruct((1, d), F32)],
        compiler_params=_cparams(("arbitrary",)),
        name="rms_ffn_pre_mix_post_bwd",
    )(dh2, x1, g_pre, dx2, mix, g_post)


CONV_ROWS = 256
TAP_ROWS = 64
HALO31 = 32
HALO4 = 8


def _sum8(v):
    return jnp.sum(v.reshape(v.shape[0] // 8, 8, v.shape[1]), axis=0)


SUBLANES = 8
PHASE_SPAN = (CONV_K - 1) // SUBLANES * SUBLANES


def _phase_scratch(ts):
    return pltpu.VMEM((SUBLANES, ts + PHASE_SPAN, LANES), F32)


def _phase_copies(ph, buf, ln, base, ts):
    for s in range(SUBLANES):
        n = ts + (CONV_K - 1 - s) // SUBLANES * SUBLANES
        ph[s, 0:n, :] = buf[pl.ds(base + s, n), ln]


def _tap_rows(ph, off, r0):
    s = off % SUBLANES
    return ph[s, pl.ds(r0 + off - s, TAP_ROWS), :]


def _conv_branch_fwd(proj, cw, cb, lg, lb, *, nb, seq):
    ts, c, halo = CONV_ROWS, D_CONV, HALO31
    ns = seq // ts
    base = halo - CONV_K + 1

    def body(ca_ref, cg_ref, w_ref, b_ref, lg_ref, lb_ref, u1_ref, u_ref, ubuf, uph):
        i = pl.program_id(1)

        @pl.when(i == 0)
        def _():
            ubuf[0:halo, :] = jnp.zeros((halo, c), F32)

        @pl.when(i > 0)
        def _():
            ubuf[0:halo, :] = ubuf[ts:ts + halo, :]

        ubuf[halo:halo + ts, :] = ca_ref[...] * _sig(cg_ref[...])

        def lane_tile(j, carry):
            ln = pl.ds(pl.multiple_of(j * LANES, LANES), LANES)
            _phase_copies(uph, ubuf, ln, base, ts)
            for r in range(ts // TAP_ROWS):
                acc = jnp.broadcast_to(b_ref[:, ln], (TAP_ROWS, LANES))
                for k in range(CONV_K):
                    acc = acc + w_ref[pl.ds(k, 1), ln] * _tap_rows(uph, k, r * TAP_ROWS)
                u1_ref[pl.ds(r * TAP_ROWS, TAP_ROWS), ln] = acc
            return carry

        lax.fori_loop(0, c // LANES, lane_tile, 0)
        v = u1_ref[...]
        mu = jnp.mean(v, axis=-1, keepdims=True)
        dv = v - mu
        xh = dv * lax.rsqrt(jnp.mean(dv * dv, axis=-1, keepdims=True) + EPS)
        u2 = xh * lg_ref[...] + lb_ref[...]
        u_ref[...] = (u2 * _sig(u2)).astype(BF16)

    t = nb * seq
    row = lambda col: pl.BlockSpec((ts, c), lambda b, i: (b * ns + i, col))
    vec = pl.BlockSpec((1, c), lambda b, i: (0, 0))
    return pl.pallas_call(
        body,
        grid=(nb, ns),
        in_specs=[row(0), row(1), pl.BlockSpec((32, c), lambda b, i: (0, 0)), vec, vec, vec],
        out_specs=[row(0), row(0)],
        out_shape=[jax.ShapeDtypeStruct((t, c), F32), jax.ShapeDtypeStruct((t, c + D_SSD), BF16)],
        scratch_shapes=[pltpu.VMEM((halo + ts, c), F32), _phase_scratch(ts)],
        compiler_params=_cparams(("parallel", "arbitrary")),
        name="conv_branch_fwd",
    )(proj, proj, cw, cb, lg, lb)


def _conv_branch_bwd(duy, u1, proj, cw, lg, lb, *, nb, seq):
    ts, c, halo = CONV_ROWS, D_CONV, HALO31
    ns = seq // ts
    base = halo - CONV_K + 1
    hb = ts // halo

    def body(du_ref, u1_ref, ca_ref, cg_ref, cah_ref, cgh_ref, w_ref, lg_ref, lb_ref,
             dcacg_ref, dw_ref, db_ref, dlg_ref, dlb_ref,
             ubuf, dbuf, du0buf, dwacc, dbacc, dlgacc, dlbacc, uph, dph):
        b, i = pl.program_id(0), pl.program_id(1)
        rc = ns - 1 - i

        @pl.when(jnp.logical_and(b == 0, i == 0))
        def _():
            dwacc[...] = jnp.zeros_like(dwacc)
            dbacc[...] = jnp.zeros_like(dbacc)
            dlgacc[...] = jnp.zeros_like(dlgacc)
            dlbacc[...] = jnp.zeros_like(dlbacc)

        @pl.when(i == 0)
        def _():
            dbuf[ts:ts + halo, :] = jnp.zeros((halo, c), F32)

        @pl.when(i > 0)
        def _():
            dbuf[ts:ts + halo, :] = dbuf[0:halo, :]

        v = u1_ref[...]
        mu = jnp.mean(v, axis=-1, keepdims=True)
        dv = v - mu
        rstd = lax.rsqrt(jnp.mean(dv * dv, axis=-1, keepdims=True) + EPS)
        xh = dv * rstd
        lgv = lg_ref[...]
        u2 = xh * lgv + lb_ref[...]
        s2 = _sig(u2)
        du2 = du_ref[...] * (s2 * (1.0 + u2 * (1.0 - s2)))
        dlgacc[...] += jnp.sum(du2 * xh, axis=0, keepdims=True)
        dlbacc[...] += jnp.sum(du2, axis=0, keepdims=True)
        gd = du2 * lgv
        du1 = rstd * (gd - jnp.mean(gd, axis=-1, keepdims=True) - xh * jnp.mean(gd * xh, axis=-1, keepdims=True))
        dbacc[...] += jnp.sum(du1, axis=0, keepdims=True)
        dbuf[0:ts, :] = du1

        @pl.when(rc == 0)
        def _():
            ubuf[0:halo, :] = jnp.zeros((halo, c), F32)

        @pl.when(rc > 0)
        def _():
            ubuf[0:halo, :] = cah_ref[...] * _sig(cgh_ref[...])

        cav = ca_ref[...]
        sg = _sig(cg_ref[...])
        ubuf[halo:halo + ts, :] = cav * sg

        def lane_tile(j, carry):
            ln = pl.ds(pl.multiple_of(j * LANES, LANES), LANES)
            _phase_copies(uph, ubuf, ln, base, ts)
            _phase_copies(dph, dbuf, ln, 0, ts)
            for r in range(ts // TAP_ROWS):
                r0 = r * TAP_ROWS
                d1 = dbuf[pl.ds(r0, TAP_ROWS), ln]
                acc = jnp.zeros((TAP_ROWS, LANES), F32)
                for k in range(CONV_K):
                    acc = acc + w_ref[pl.ds(k, 1), ln] * _tap_rows(dph, CONV_K - 1 - k, r0)
                    dwacc[pl.ds(k * 8, 8), ln] += _sum8(d1 * _tap_rows(uph, k, r0))
                du0buf[pl.ds(r0, TAP_ROWS), ln] = acc
            return carry

        lax.fori_loop(0, c // LANES, lane_tile, 0)
        du0 = du0buf[...]
        dcacg_ref[:, 0:c] = (du0 * sg).astype(BF16)
        dcacg_ref[:, c:2 * c] = (du0 * cav * sg * (1.0 - sg)).astype(BF16)

        @pl.when(jnp.logical_and(b == nb - 1, i == ns - 1))
        def _():
            for k in range(CONV_K):
                dw_ref[pl.ds(k, 1), :] = jnp.sum(dwacc[pl.ds(k * 8, 8), :], axis=0, keepdims=True)
            dw_ref[pl.ds(CONV_K, 1), :] = jnp.zeros((1, c), F32)
            db_ref[...] = dbacc[...]
            dlg_ref[...] = dlgacc[...]
            dlb_ref[...] = dlbacc[...]

    t = nb * seq
    rowblk = lambda b, i: b * ns + (ns - 1 - i)
    row = lambda col: pl.BlockSpec((ts, c), lambda b, i: (rowblk(b, i), col))
    hrow = lambda col: pl.BlockSpec((halo, c), lambda b, i: (jnp.maximum(rowblk(b, i) * hb - 1, 0), col))
    vec = pl.BlockSpec((1, c), lambda b, i: (0, 0))
    wspec = pl.BlockSpec((32, c), lambda b, i: (0, 0))
    return pl.pallas_call(
        body,
        grid=(nb, ns),
        in_specs=[row(0), row(0), row(0), row(1), hrow(0), hrow(1), wspec, vec, vec],
        out_specs=[pl.BlockSpec((ts, 2 * c), lambda b, i: (rowblk(b, i), 0)), wspec, vec, vec, vec],
        out_shape=[jax.ShapeDtypeStruct((t, D_MAIN), BF16), jax.ShapeDtypeStruct((32, c), F32),
                   jax.ShapeDtypeStruct((1, c), F32), jax.ShapeDtypeStruct((1, c), F32), jax.ShapeDtypeStruct((1, c), F32)],
        scratch_shapes=[pltpu.VMEM((halo + ts, c), F32), pltpu.VMEM((ts + halo, c), F32), pltpu.VMEM((ts, c), F32),
                        pltpu.VMEM((CONV_K * 8, c), F32), pltpu.VMEM((1, c), F32), pltpu.VMEM((1, c), F32),
                        pltpu.VMEM((1, c), F32), _phase_scratch(ts), _phase_scratch(ts)],
        compiler_params=_cparams(("arbitrary", "arbitrary")),
        name="conv_branch_bwd",
    )(duy, u1, proj, proj, proj, proj, cw, lg, lb)


XBC_COL0 = (2 * D_CONV + D_SSD) // 1024


def _ssd_pre_fwd(proj, sw, sb, *, nb, seq):
    ts, c, halo = CONV_ROWS, 1024, HALO4
    ns = seq // ts
    base = halo - SSD_CONV_K + 1

    def body(x_ref, w_ref, b_ref, o_ref, xbuf):
        i = pl.program_id(2)

        @pl.when(i == 0)
        def _():
            xbuf[0:halo, :] = jnp.zeros((halo, c), F32)

        @pl.when(i > 0)
        def _():
            xbuf[0:halo, :] = xbuf[ts:ts + halo, :]

        xbuf[halo:halo + ts, :] = x_ref[...]

        def lane_tile(j, carry):
            ln = pl.ds(pl.multiple_of(j * LANES, LANES), LANES)
            for r in range(ts // TAP_ROWS):
                acc = jnp.broadcast_to(b_ref[:, ln], (TAP_ROWS, LANES))
                for k in range(SSD_CONV_K):
                    acc = acc + w_ref[pl.ds(k, 1), ln] * xbuf[pl.ds(r * TAP_ROWS + base + k, TAP_ROWS), ln]
                o_ref[pl.ds(r * TAP_ROWS, TAP_ROWS), ln] = acc * _sig(acc)
            return carry

        lax.fori_loop(0, c // LANES, lane_tile, 0)

    t = nb * seq
    return pl.pallas_call(
        body,
        grid=(2, nb, ns),
        in_specs=[pl.BlockSpec((ts, c), lambda j, b, i: (b * ns + i, XBC_COL0 + j)),
                  pl.BlockSpec((8, c), lambda j, b, i: (0, j)), pl.BlockSpec((1, c), lambda j, b, i: (0, j))],
        out_specs=pl.BlockSpec((ts, c), lambda j, b, i: (b * ns + i, j)),
        out_shape=jax.ShapeDtypeStruct((t, D_XBC), F32),
        scratch_shapes=[pltpu.VMEM((halo + ts, c), F32)],
        compiler_params=_cparams(("parallel", "parallel", "arbitrary")),
        name="ssd_pre_fwd",
    )(proj, sw, sb)


def _ssd_pre_bwd(dproj, dxs, proj, sw, sb, *, nb, seq):
    ts, c, halo = CONV_ROWS, 1024, HALO4
    ns = seq // ts
    base = halo - SSD_CONV_K + 1
    hb = ts // halo

    def body(dproj_ref, d_ref, x_ref, xh_ref, w_ref, b_ref, dx_ref, dw_ref, db_ref, xbuf, dbuf, dwacc, dbacc):
        b, i = pl.program_id(1), pl.program_id(2)
        rc = ns - 1 - i

        @pl.when(jnp.logical_and(b == 0, i == 0))
        def _():
            dwacc[...] = jnp.zeros_like(dwacc)
            dbacc[...] = jnp.zeros_like(dbacc)

        @pl.when(i == 0)
        def _():
            dbuf[ts:ts + halo, :] = jnp.zeros((halo, c), F32)

        @pl.when(i > 0)
        def _():
            dbuf[ts:ts + halo, :] = dbuf[0:halo, :]

        @pl.when(rc == 0)
        def _():
            xbuf[0:halo, :] = jnp.zeros((halo, c), F32)

        @pl.when(rc > 0)
        def _():
            xbuf[0:halo, :] = xh_ref[...]

        xbuf[halo:halo + ts, :] = x_ref[...]

        def pre_tile(j, carry):
            ln = pl.ds(pl.multiple_of(j * LANES, LANES), LANES)
            for r in range(ts // TAP_ROWS):
                r0 = r * TAP_ROWS
                acc = jnp.broadcast_to(b_ref[:, ln], (TAP_ROWS, LANES))
                for k in range(SSD_CONV_K):
                    acc = acc + w_ref[pl.ds(k, 1), ln] * xbuf[pl.ds(r0 + base + k, TAP_ROWS), ln]
                s = _sig(acc)
                dc = d_ref[pl.ds(r0, TAP_ROWS), ln] * (s * (1.0 + acc * (1.0 - s)))
                dbuf[pl.ds(r0, TAP_ROWS), ln] = dc
                dbacc[:, ln] += _sum8(dc)
            return carry

        lax.fori_loop(0, c // LANES, pre_tile, 0)

        def lane_tile(j, carry):
            ln = pl.ds(pl.multiple_of(j * LANES, LANES), LANES)
            for r in range(ts // TAP_ROWS):
                r0 = r * TAP_ROWS
                d1 = dbuf[pl.ds(r0, TAP_ROWS), ln]
                acc = jnp.zeros((TAP_ROWS, LANES), F32)
                for k in range(SSD_CONV_K):
                    acc = acc + w_ref[pl.ds(k, 1), ln] * dbuf[pl.ds(r0 + SSD_CONV_K - 1 - k, TAP_ROWS), ln]
                    dwacc[pl.ds(k * 8, 8), ln] += _sum8(d1 * xbuf[pl.ds(r0 + base + k, TAP_ROWS), ln])
                dx_ref[pl.ds(r0, TAP_ROWS), ln] = acc.astype(BF16)
            return carry

        lax.fori_loop(0, c // LANES, lane_tile, 0)

        @pl.when(jnp.logical_and(b == nb - 1, i == ns - 1))
        def _():
            for k in range(SSD_CONV_K):
                dw_ref[pl.ds(k, 1), :] = jnp.sum(dwacc[pl.ds(k * 8, 8), :], axis=0, keepdims=True)
            dw_ref[pl.ds(SSD_CONV_K, 8 - SSD_CONV_K), :] = jnp.zeros((8 - SSD_CONV_K, c), F32)
            db_ref[...] = jnp.sum(dbacc[...], axis=0, keepdims=True)

    t = nb * seq
    rowblk = lambda b, i: b * ns + (ns - 1 - i)
    return pl.pallas_call(
        body,
        grid=(2, nb, ns),
        in_specs=[DEP_SPEC, pl.BlockSpec((ts, c), lambda j, b, i: (rowblk(b, i), j)),
                  pl.BlockSpec((ts, c), lambda j, b, i: (rowblk(b, i), XBC_COL0 + j)),
                  pl.BlockSpec((halo, c), lambda j, b, i: (jnp.maximum(rowblk(b, i) * hb - 1, 0), XBC_COL0 + j)),
                  pl.BlockSpec((8, c), lambda j, b, i: (0, j)), pl.BlockSpec((1, c), lambda j, b, i: (0, j))],
        out_specs=[pl.BlockSpec((ts, c), lambda j, b, i: (rowblk(b, i), XBC_COL0 + j)),
                   pl.BlockSpec((8, c), lambda j, b, i: (0, j)), pl.BlockSpec((1, c), lambda j, b, i: (0, j))],
        out_shape=[jax.ShapeDtypeStruct(dproj.shape, BF16), jax.ShapeDtypeStruct((8, D_XBC), F32),
                   jax.ShapeDtypeStruct((1, D_XBC), F32)],
        input_output_aliases={0: 0},
        scratch_shapes=[pltpu.VMEM((halo + ts, c), F32), pltpu.VMEM((ts + halo, c), F32),
                        pltpu.VMEM((SSD_CONV_K * 8, c), F32), pltpu.VMEM((8, c), F32)],
        compiler_params=_cparams(("arbitrary", "arbitrary", "arbitrary")),
        name="ssd_pre_bwd",
    )(dproj, dxs, proj, proj, sw, sb)


Z_COL = (2 * D_CONV) // 1024
GROUP_W = D_SSD // GROUPS


def _softplus(v):
    return jnp.maximum(v, 0.0) + jnp.log(1.0 + jnp.exp(-jnp.abs(v)))


def _dot(a, b):
    return jnp.dot(a, b, preferred_element_type=F32)


def _dot_nt(a, b):
    return lax.dot_general(a, b, (_DIMS["nt"], ((), ())), preferred_element_type=F32)


def _dot_tn(a, b):
    return lax.dot_general(a, b, (_DIMS["tn"], ((), ())), preferred_element_type=F32)


def _bf16_terms(v):
    hi = v.astype(BF16)
    r1 = v - hi.astype(F32)
    mid = r1.astype(BF16)
    return hi, mid, (r1 - mid.astype(F32)).astype(BF16)


def _dot_exact_left(sel, v):
    hi, mid, lo = _bf16_terms(v)
    return _dot(sel, hi) + (_dot(sel, mid) + _dot(sel, lo))


def _dot_exact_right(v, sel):
    hi, mid, lo = _bf16_terms(v)
    return _dot(hi, sel) + (_dot(mid, sel) + _dot(lo, sel))


def _chunk_decays(dtr_ref, bias_ref, alog_ref):
    q = CHUNK
    ii = lax.broadcasted_iota(jnp.int32, (q, q), 0)
    jj = lax.broadcasted_iota(jnp.int32, (q, q), 1)
    tri = jj <= ii
    dt = _softplus(dtr_ref[...] + bias_ref[...])
    a_head = -jnp.exp(alog_ref[...])
    cs = _dot_exact_left(tri.astype(BF16), dt * a_head)
    return tri, dt, a_head, cs, cs.T


def _ssd_fwd(uy, xs_all, proj, dtr, dt_bias, a_log, d_lanes, norm_w, *, nb, seq):
    q = CHUNK
    nc = seq // q
    t = nb * seq

    def body(uy_ref, xs_ref, bm_ref, cm_ref, z_ref, dtr_ref, bias_ref, alog_ref, dl_ref, nw_ref,
             y_ref, ys_ref, st_ref, state):
        @pl.when(pl.program_id(1) == 0)
        def _():
            state[...] = jnp.zeros_like(state)

        tri, dt, _, cs, cst = _chunk_decays(dtr_ref, bias_ref, alog_ref)
        first = lax.broadcasted_iota(jnp.int32, (1, LANES), 1) < HEAD_DIM
        for g in range(GROUPS):
            gl = slice(g * STATE, (g + 1) * STATE)
            bb = bm_ref[:, gl].astype(BF16)
            cb = cm_ref[:, gl].astype(BF16)
            scores = _dot_nt(cb, bb)
            for p in range(2):
                pr = 2 * g + p
                h0 = 2 * pr
                sl = slice(pr * LANES, (pr + 1) * LANES)
                xv = xs_ref[:, sl]
                dtp = jnp.where(first, dt[:, h0:h0 + 1], dt[:, h0 + 1:h0 + 2])
                csp = jnp.where(first, cs[:, h0:h0 + 1], cs[:, h0 + 1:h0 + 2])
                xd = xv * dtp
                yv = None
                for hh, keep in ((h0, first), (h0 + 1, jnp.logical_not(first))):
                    decay = jnp.where(tri, jnp.exp(cs[:, hh:hh + 1] - cst[hh:hh + 1, :]), 0.0)
                    part = _dot((scores * decay).astype(BF16), jnp.where(keep, xd, 0.0).astype(BF16))
                    yv = part if yv is None else yv + part
                hp = state[pr]
                st_ref[0, pr] = hp
                yv = yv + jnp.exp(csp) * _dot(cb, hp.astype(BF16))
                last = csp[q - 1:q, :]
                state[pr] = jnp.exp(last) * hp + _dot_tn(bb, (xd * jnp.exp(last - csp)).astype(BF16))
                ys_ref[:, sl] = yv + dl_ref[:, sl] * xv
        zv = z_ref[...]
        gated = ys_ref[...] * (zv * _sig(zv))
        for g in range(GROUPS):
            gl = slice(g * GROUP_W, (g + 1) * GROUP_W)
            v = gated[:, gl]
            r = lax.rsqrt(jnp.mean(v * v, axis=-1, keepdims=True) + EPS)
            y_ref[:, gl] = (v * r * nw_ref[:, gl]).astype(BF16)

    blk = lambda w, col: pl.BlockSpec((q, w), lambda b, c: (b * nc + c, col))
    vec = lambda w: pl.BlockSpec((1, w), lambda b, c: (0, 0))
    return pl.pallas_call(
        body,
        grid=(nb, nc),
        in_specs=[DEP_SPEC, blk(D_SSD, 0), blk(GROUPS * STATE, 2), blk(GROUPS * STATE, 3), blk(D_SSD, Z_COL),
                  blk(LANES, 0), vec(LANES), vec(LANES), vec(D_SSD), vec(D_SSD)],
        out_specs=[blk(D_SSD, 1), blk(D_SSD, 0),
                   pl.BlockSpec((1, PAIRS, STATE, LANES), lambda b, c: (b * nc + c, 0, 0, 0))],
        out_shape=[jax.ShapeDtypeStruct(uy.shape, BF16), jax.ShapeDtypeStruct((t, D_SSD), F32),
                   jax.ShapeDtypeStruct((nb * nc, PAIRS, STATE, LANES), F32)],
        input_output_aliases={0: 0},
        scratch_shapes=[pltpu.VMEM((PAIRS, STATE, LANES), F32)],
        compiler_params=_cparams(("parallel", "arbitrary")),
        name="ssd_fwd",
    )(uy, xs_all, xs_all, xs_all, proj, dtr, dt_bias, a_log, d_lanes, norm_w)


def _ssd_bwd(dproj, duy, proj, ys, xs_all, dtr, states, dt_bias, a_log, d_lanes, norm_w, *, nb, seq):
    q = CHUNK
    nc = seq // q
    t = nb * seq
    head_of_lane = (jnp.arange(D_SSD)[:, None] // HEAD_DIM == jnp.arange(LANES)[None, :]).astype(BF16)

    def body(dproj_ref, dy_ref, z_ref, ys_ref, xs_ref, bm_ref, cm_ref, dtr_ref, st_ref, bias_ref, alog_ref, dl_ref,
             nw_ref, sel_ref, dz_ref, dx_ref, ddtr_ref, small_ref,
             dstate, dys_buf, dcsl, ddtl, dcst, dnw_acc, dd_acc, dbias_acc, da_acc):
        b, c = pl.program_id(0), pl.program_id(1)

        @pl.when(jnp.logical_and(b == 0, c == 0))
        def _():
            dnw_acc[...] = jnp.zeros_like(dnw_acc)
            dd_acc[...] = jnp.zeros_like(dd_acc)
            dbias_acc[...] = jnp.zeros_like(dbias_acc)
            da_acc[...] = jnp.zeros_like(da_acc)
            dcst[...] = jnp.zeros_like(dcst)

        @pl.when(c == 0)
        def _():
            dstate[...] = jnp.zeros_like(dstate)

        zv = z_ref[...]
        sz = _sig(zv)
        silz = zv * sz
        ysv = ys_ref[...]
        gated = ysv * silz
        dyv = dy_ref[...]
        nwv = nw_ref[...]
        for g in range(GROUPS):
            gl = slice(g * GROUP_W, (g + 1) * GROUP_W)
            v = gated[:, gl]
            r = lax.rsqrt(jnp.mean(v * v, axis=-1, keepdims=True) + EPS)
            yn = v * r
            dyn = dyv[:, gl] * nwv[:, gl]
            dnw_acc[:, gl] += jnp.sum(dyv[:, gl] * yn, axis=0, keepdims=True)
            dys_buf[:, gl] = r * (dyn - yn * jnp.mean(dyn * yn, axis=-1, keepdims=True))
        dgated = dys_buf[...]
        dz_ref[...] = (dgated * ysv * (sz * (1.0 + zv * (1.0 - sz)))).astype(BF16)
        dys_all = dgated * silz
        dys_buf[...] = dys_all
        dd_acc[...] += jnp.sum(dys_all * xs_ref[...], axis=0, keepdims=True)

        tri, dt, a_head, cs, cst = _chunk_decays(dtr_ref, bias_ref, alog_ref)
        lane = lax.broadcasted_iota(jnp.int32, (1, LANES), 1)
        first = lane < HEAD_DIM
        dcs_h = jnp.zeros((q, LANES), F32)
        for g in range(GROUPS):
            gl = slice(g * STATE, (g + 1) * STATE)
            bb = bm_ref[:, gl].astype(BF16)
            cb = cm_ref[:, gl].astype(BF16)
            scores = _dot_nt(cb, bb)
            dscores = jnp.zeros((q, q), F32)
            dbg = jnp.zeros((q, STATE), F32)
            dcg = jnp.zeros((q, STATE), F32)
            for p in range(2):
                pr = 2 * g + p
                h0 = 2 * pr
                sl = slice(pr * LANES, (pr + 1) * LANES)
                xv = xs_ref[:, sl]
                dyp = dys_buf[:, sl]
                dtp = jnp.where(first, dt[:, h0:h0 + 1], dt[:, h0 + 1:h0 + 2])
                csp = jnp.where(first, cs[:, h0:h0 + 1], cs[:, h0 + 1:h0 + 2])
                xd = xv * dtp
                xdb = xd.astype(BF16)
                hp = st_ref[0, pr]
                dhn = dstate[pr]
                hpb = hp.astype(BF16)
                dhnb = dhn.astype(BF16)
                lam = jnp.exp(csp)
                last = csp[q - 1:q, :]
                gam = jnp.exp(last)
                w = jnp.exp(last - csp)
                dxd = jnp.zeros((q, LANES), F32)
                for hh, keep in ((h0, first), (h0 + 1, jnp.logical_not(first))):
                    decay = jnp.where(tri, jnp.exp(cs[:, hh:hh + 1] - cst[hh:hh + 1, :]), 0.0)
                    m = scores * decay
                    dym = jnp.where(keep, dyp, 0.0).astype(BF16)
                    dm = _dot_nt(dym, xdb)
                    dxd = dxd + _dot_tn(m.astype(BF16), dym)
                    e = dm * m
                    dcs_h = dcs_h + jnp.where(lane == hh, jnp.sum(e, axis=1, keepdims=True), 0.0)
                    dcst[hh:hh + 1, :] = jnp.sum(e, axis=0, keepdims=True)
                    dscores = dscores + dm * decay
                yoff = lam * _dot(cb, hpb)
                ldy = (lam * dyp).astype(BF16)
                dcg = dcg + _dot_nt(ldy, hpb)
                dstate[pr] = gam * dhn + _dot_tn(cb, ldy)
                bdh = _dot(bb, dhnb)
                dxd = dxd + w * bdh
                xdw = xd * w
                dbg = dbg + _dot_nt(xdw.astype(BF16), dhnb)
                wd = xdw * bdh
                dcsl[:, sl] = dyp * yoff - wd
                dcsl[q - 1:q, sl] += (jnp.sum(wd, axis=0, keepdims=True)
                                      + gam * jnp.sum(dhn * hp, axis=0, keepdims=True))
                dx_ref[:, sl] = dxd * dtp + dyp * dl_ref[:, sl]
                ddtl[:, sl] = dxd * xv
            dsb = dscores.astype(BF16)
            dx_ref[:, D_SSD + g * STATE:D_SSD + (g + 1) * STATE] = dbg + _dot_tn(dsb, cb)
            dx_ref[:, D_SSD + (GROUPS + g) * STATE:D_SSD + (GROUPS + g + 1) * STATE] = dcg + _dot(dsb, bb)

        sel = sel_ref[...]
        dcs_h = dcs_h + _dot_exact_right(dcsl[...], sel) - dcst[...].T
        ddt = _dot_exact_right(ddtl[...], sel)
        upper = lax.broadcasted_iota(jnp.int32, (q, q), 1) >= lax.broadcasted_iota(jnp.int32, (q, q), 0)
        da = _dot_exact_left(upper.astype(BF16), dcs_h)
        ddt = ddt + da * a_head
        da_acc[...] += jnp.sum(da * dt, axis=0, keepdims=True)
        ddtr = ddt * _sig(dtr_ref[...] + bias_ref[...])
        ddtr_ref[...] = ddtr
        dbias_acc[...] += jnp.sum(ddtr, axis=0, keepdims=True)

        @pl.when(jnp.logical_and(b == nb - 1, c == nc - 1))
        def _():
            small_ref[...] = jnp.zeros_like(small_ref)
            small_ref[0:1, :] = dnw_acc[...]
            small_ref[1:2, 0:LANES] = _dot_exact_right(jnp.broadcast_to(dd_acc[...], (8, D_SSD)), sel)[0:1, :]
            small_ref[2:3, 0:LANES] = dbias_acc[...]
            small_ref[3:4, 0:LANES] = da_acc[...] * a_head

    rowblk = lambda b, c: b * nc + (nc - 1 - c)
    blk = lambda w, col: pl.BlockSpec((q, w), lambda b, c: (rowblk(b, c), col))
    vec = lambda w: pl.BlockSpec((1, w), lambda b, c: (0, 0))
    return pl.pallas_call(
        body,
        grid=(nb, nc),
        in_specs=[DEP_SPEC, blk(D_SSD, 1), blk(D_SSD, Z_COL), blk(D_SSD, 0), blk(D_SSD, 0), blk(GROUPS * STATE, 2),
                  blk(GROUPS * STATE, 3), blk(LANES, 0),
                  pl.BlockSpec((1, PAIRS, STATE, LANES), lambda b, c: (rowblk(b, c), 0, 0, 0)),
                  vec(LANES), vec(LANES), vec(D_SSD), vec(D_SSD), pl.BlockSpec((D_SSD, LANES), lambda b, c: (0, 0))],
        out_specs=[blk(D_SSD, Z_COL), blk(D_XBC, 0), blk(LANES, 0), pl.BlockSpec((8, D_SSD), lambda b, c: (0, 0))],
        out_shape=[jax.ShapeDtypeStruct(dproj.shape, BF16), jax.ShapeDtypeStruct((t, D_XBC), F32),
                   jax.ShapeDtypeStruct((t, LANES), F32), jax.ShapeDtypeStruct((8, D_SSD), F32)],
        input_output_aliases={0: 0},
        scratch_shapes=[pltpu.VMEM((PAIRS, STATE, LANES), F32), pltpu.VMEM((q, D_SSD), F32),
                        pltpu.VMEM((q, D_SSD), F32), pltpu.VMEM((q, D_SSD), F32), pltpu.VMEM((LANES, q), F32),
                        pltpu.VMEM((1, D_SSD), F32), pltpu.VMEM((1, D_SSD), F32), pltpu.VMEM((1, LANES), F32),
                        pltpu.VMEM((1, LANES), F32)],
        compiler_params=_cparams(("arbitrary", "arbitrary")),
        name="ssd_bwd",
    )(dproj, duy, proj, ys, xs_all, xs_all, xs_all, dtr, states, dt_bias, a_log, d_lanes, norm_w, head_of_lane)


HBM_SPEC = pl.BlockSpec(memory_space=pltpu.HBM)
MESH_ID = pl.DeviceIdType.MESH


def _coords():
    return lax.axis_index("x"), lax.axis_index("y"), lax.axis_index("c")


def _chip_peer(xi, yi, ci, d):
    return (jnp.bitwise_xor(xi, d >> 1), jnp.bitwise_xor(yi, d & 1), ci)


def _remote(src, dst, send_sem, recv_sem, peer):
    return pltpu.make_async_remote_copy(src_ref=src, dst_ref=dst, send_sem=send_sem, recv_sem=recv_sem,
                                        device_id=peer, device_id_type=MESH_ID)


SEM_SPEC = pl.BlockSpec(memory_space=pltpu.SEMAPHORE)
ANY_SPEC = pl.BlockSpec(memory_space=pl.ANY)
EFFECT = pltpu.SideEffectType.DATAFLOW_SIDE_EFFECTING
COPIES = 3


def _half(ref, axis, which, lead=0):
    size = ref.shape[lead + axis] // 2
    part = pl.ds(which * size, size)
    idx = (slice(None),) * lead + ((part, slice(None)) if axis == 0 else (slice(None), part))
    return ref.at[idx]


def _halved_shape(shape, axis):
    lead = len(shape) - 2
    return tuple(d // 2 if i == lead + axis else d for i, d in enumerate(shape))


def _gather_plan(axis):
    def plan(xi, yi, ci, src, land):
        me = 2 * xi + yi
        out = []
        for d in (1, 2, 3):
            there = jnp.bitwise_xor(me, d)
            if axis is None:
                out.append((src, land.at[me], _chip_peer(xi, yi, ci, d), land.at[there]))
            else:
                out.append((_half(src, axis, ci), _half(land.at[me], axis, ci), _chip_peer(xi, yi, ci, d),
                            _half(land.at[there], axis, ci)))
        return out
    return plan


def _owners_plan(xi, yi, ci, src, land):
    me = 2 * xi + yi
    return [(src.at[jnp.bitwise_xor(me, d)], land.at[d - 1], _chip_peer(xi, yi, ci, d), land.at[d - 1])
            for d in (1, 2, 3)]


def _split_start(srcs, lands, plans, *, name):
    n = len(srcs)

    def body(*refs):
        src_refs, land_refs = refs[:n], refs[n:2 * n]
        ssems, rsems = refs[2 * n:3 * n], refs[3 * n:4 * n]
        token = refs[-1]
        xi, yi, ci = _coords()
        for t in range(n):
            for k, (src, dst, peer, _) in enumerate(plans[t](xi, yi, ci, src_refs[t], land_refs[t])):
                _remote(src, dst, ssems[t].at[k], rsems[t].at[k], peer).start()
        token[...] = jnp.zeros_like(token)

    bufs = list(srcs) + list(lands)
    outs = pl.pallas_call(
        body,
        name=name,
        in_specs=[HBM_SPEC] * (2 * n),
        out_specs=[SEM_SPEC] * (2 * n) + [HBM_SPEC] * (2 * n) + [pl.BlockSpec(memory_space=pltpu.VMEM)],
        out_shape=[pltpu.SemaphoreType.DMA((COPIES,))] * (2 * n) + [pltpu.HBM(a.shape, a.dtype) for a in bufs]
        + [jax.ShapeDtypeStruct((8, LANES), F32)],
        input_output_aliases={i: 2 * n + i for i in range(2 * n)},
        compiler_params=pltpu.CompilerParams(has_side_effects=EFFECT),
    )(*[pltpu.with_memory_space_constraint(a, pltpu.HBM) for a in bufs])
    return outs[:n], outs[n:2 * n], outs[2 * n:3 * n], outs[3 * n:4 * n], outs[-1]


def _split_wait(ssems, rsems, srcs, lands, plans, after, *, name):
    n = len(srcs)

    def body(*refs):
        src_refs, land_refs = refs[:n], refs[n:2 * n]
        ss, rs = refs[2 * n:3 * n], refs[3 * n:4 * n]
        xi, yi, ci = _coords()
        for t in range(n):
            for k, (src, _, peer, landed) in enumerate(plans[t](xi, yi, ci, src_refs[t], land_refs[t])):
                cp = _remote(src, landed, ss[t].at[k], rs[t].at[k], peer)
                cp.wait_send()
                cp.wait_recv()

    bufs = list(srcs) + list(lands)
    outs = pl.pallas_call(
        body,
        name=name,
        in_specs=[HBM_SPEC] * (2 * n) + [SEM_SPEC] * (2 * n) + [ANY_SPEC],
        out_specs=[HBM_SPEC] * (2 * n),
        out_shape=[pltpu.HBM(a.shape, a.dtype) for a in bufs],
        input_output_aliases={i: i for i in range(2 * n)},
        compiler_params=pltpu.CompilerParams(has_side_effects=EFFECT),
    )(*bufs, *ssems, *rsems, after)
    return outs[:n], outs[n:]


def _forward_halves(lands, axes, *, name):
    n = len(lands)

    def body(*refs):
        ins, outs = refs[:n], refs[n:2 * n]
        send_sems, recv_sems = refs[2 * n:]
        xi, yi, ci = _coords()
        me = 2 * xi + yi
        sibling = (xi, yi, 1 - ci)
        cps = []
        for t in range(n):
            for d in (1, 2, 3):
                slot = jnp.bitwise_xor(me, d)
                k = COPIES * t + d - 1
                cp = _remote(_half(ins[t].at[slot], axes[t], ci), _half(outs[t].at[slot], axes[t], ci),
                             send_sems.at[k], recv_sems.at[k], sibling)
                cp.start()
                cps.append(cp)
        for t in range(n):
            for d in (1, 2, 3):
                got = _half(outs[t].at[jnp.bitwise_xor(me, d)], axes[t], 1 - ci)
                k = COPIES * t + d - 1
                _remote(got, got, send_sems.at[k], recv_sems.at[k], sibling).wait_recv()
        for cp in cps:
            cp.wait_send()

    return pl.pallas_call(
        body,
        name=name,
        in_specs=[HBM_SPEC] * n,
        out_specs=[HBM_SPEC] * n,
        out_shape=[jax.ShapeDtypeStruct(a.shape, a.dtype) for a in lands],
        input_output_aliases={i: i for i in range(n)},
        scratch_shapes=[pltpu.SemaphoreType.DMA((COPIES * n,)), pltpu.SemaphoreType.DMA((COPIES * n,))],
    )(*lands)


def _swap_other_halves(gs, axes, *, name):
    n = len(gs)

    def body(*refs):
        ins, lands = refs[:n], refs[n:2 * n]
        send_sems, recv_sems = refs[2 * n:]
        xi, yi, ci = _coords()
        sibling = (xi, yi, 1 - ci)
        cps = []
        for t in range(n):
            cp = _remote(_half(ins[t], axes[t], 1 - ci, lead=1), lands[t], send_sems.at[t], recv_sems.at[t], sibling)
            cp.start()
            cps.append(cp)
        for cp in cps:
            cp.wait_recv()
        for cp in cps:
            cp.wait_send()

    return pl.pallas_call(
        body,
        in_specs=[HBM_SPEC] * n,
        out_specs=[HBM_SPEC] * n,
        out_shape=[jax.ShapeDtypeStruct(_halved_shape(g.shape, ax), g.dtype) for g, ax in zip(gs, axes)],
        scratch_shapes=[pltpu.SemaphoreType.DMA((n,)), pltpu.SemaphoreType.DMA((n,))],
        name=name,
    )(*gs)


def _row_tile(rows, cap=512, mult=16):
    best = mult
    for cand in range(mult, min(rows, cap) + 1, mult):
        if rows % cand == 0:
            best = cand
    assert rows % best == 0, rows
    return best


COL_TILE = 256


def _half_tiles(hr, hc, axis, cap=512, mult=16):
    if axis == 0:
        tr = _row_tile(hr, cap, mult)
        n = hr // tr
        return (tr, hc), n, lambda half, i: (half * n + i, 0)
    n = hc // COL_TILE
    return (hr, COL_TILE), n, lambda half, i: (0, half * n + i)


def _add_core_halves(g, land, where, axis):
    nslot, hr, hc = land.shape
    bshape, nr, idx = _half_tiles(hr, hc, axis)

    def body(where_ref, g_ref, l_ref, f_ref, b_ref):
        s = g_ref[...] + l_ref[...].astype(F32)
        b_ref[...] = s.astype(BF16)

        @pl.when(pl.program_id(1) == where_ref[1])
        def _():
            f_ref[...] = s

    blk = pl.BlockSpec((None,) + bshape, lambda i, s, w: (s,) + idx(0, i))
    mine = pl.BlockSpec((None,) + bshape, lambda i, s, w: (s,) + idx(w[0], i))
    return pl.pallas_call(
        body,
        grid_spec=pltpu.PrefetchScalarGridSpec(
            num_scalar_prefetch=1,
            grid=(nr, nslot),
            in_specs=[mine, blk],
            out_specs=[pl.BlockSpec(bshape, lambda i, s, w: idx(0, i)), blk],
        ),
        out_shape=[jax.ShapeDtypeStruct((hr, hc), F32), jax.ShapeDtypeStruct(land.shape, BF16)],
        compiler_params=_cparams(("parallel", "arbitrary")),
        name="add_core_halves",
    )(where, g, land)


def _add_chip_sums(pf, land, where, axis):
    hr, cols = pf.shape
    bshape, nr, idx = _half_tiles(hr, cols, axis)

    def body(where_ref, p_ref, l_ref, o_ref):
        acc = p_ref[...]
        for d in range(3):
            acc = acc + l_ref[d].astype(F32)
        o_ref[...] = acc

    return pl.pallas_call(
        body,
        grid_spec=pltpu.PrefetchScalarGridSpec(
            num_scalar_prefetch=1,
            grid=(nr,),
            in_specs=[pl.BlockSpec(bshape, lambda i, w: idx(0, i)),
                      pl.BlockSpec((3,) + bshape, lambda i, w: (0,) + idx(0, i))],
            out_specs=pl.BlockSpec(bshape, lambda i, w: idx(0, i)),
        ),
        out_shape=jax.ShapeDtypeStruct((hr, cols), F32),
        compiler_params=_cparams(("parallel",)),
        name="add_chip_sums",
    )(where, pf, land)


def _swap_reduced_halves(rs):
    n = len(rs)

    def body(*refs):
        ins, outs = refs[:n], refs[n:2 * n]
        send_sems, recv_sems = refs[2 * n:]
        xi, yi, ci = _coords()
        sibling = (xi, yi, 1 - ci)
        cps = [_remote(ins[t], outs[t], send_sems.at[t], recv_sems.at[t], sibling) for t in range(n)]
        for cp in cps:
            cp.start()
        for cp in cps:
            cp.wait_recv()
        for cp in cps:
            cp.wait_send()

    return pl.pallas_call(
        body,
        in_specs=[HBM_SPEC] * n,
        out_specs=[HBM_SPEC] * n,
        out_shape=[jax.ShapeDtypeStruct(r.shape, r.dtype) for r in rs],
        scratch_shapes=[pltpu.SemaphoreType.DMA((n,)), pltpu.SemaphoreType.DMA((n,))],
        name="swap_reduced_halves",
    )(*rs)


N_DEV = 8


def _all_reduce_small(part):
    r, w = part.shape

    def body(p_ref, o_ref, gath, send_sems, recv_sems):
        xi, yi, ci = _coords()
        me = 4 * xi + 2 * yi + ci
        gath[me] = p_ref[...]
        cps = []
        for d in range(1, N_DEV):
            peer = (jnp.bitwise_xor(xi, d >> 2), jnp.bitwise_xor(yi, (d >> 1) & 1), jnp.bitwise_xor(ci, d & 1))
            cp = _remote(p_ref, gath.at[me], send_sems.at[d - 1], recv_sems.at[d - 1], peer)
            cp.start()
            cps.append(cp)
        for d in range(1, N_DEV):
            src = gath.at[jnp.bitwise_xor(me, d)]
            _remote(src, src, send_sems.at[d - 1], recv_sems.at[d - 1], (xi, yi, ci)).wait_recv()
        acc = gath[0]
        for k in range(1, N_DEV):
            acc = acc + gath[k]
        o_ref[...] = acc
        for cp in cps:
            cp.wait_send()

    vm = pl.BlockSpec(memory_space=pltpu.VMEM)
    return pl.pallas_call(
        body,
        in_specs=[vm],
        out_specs=vm,
        out_shape=jax.ShapeDtypeStruct((r, w), F32),
        scratch_shapes=[pltpu.VMEM((N_DEV, r, w), F32), pltpu.SemaphoreType.DMA((N_DEV - 1,)),
                        pltpu.SemaphoreType.DMA((N_DEV - 1,))],
        name="all_reduce_small",
    )(part)


def _adamw_math(wv, gv, mv, vv):
    mn = ADAM_B1 * mv + (1.0 - ADAM_B1) * gv
    vn = ADAM_B2 * vv + (1.0 - ADAM_B2) * (gv * gv)
    m_hat = mn / (1.0 - ADAM_B1 ** ADAM_STEP)
    v_hat = vn / (1.0 - ADAM_B2 ** ADAM_STEP)
    return -ADAM_LR * (m_hat / (jnp.sqrt(v_hat) + ADAM_EPS) + ADAM_WD * wv), mn, vn


def _adamw_halves(w, g_mine, g_other, m, v, where, axis, *, name):
    rows, cols = w.shape
    hr, hc = g_mine.shape
    bshape, nr, idx = _half_tiles(hr, hc, axis, cap=256, mult=8)

    def body(where_ref, w_ref, gm_ref, go_ref, m_ref, v_ref, g_ref, d_ref, nm_ref, nv_ref):
        is_mine = pl.program_id(0) == where_ref[0]
        gv = jnp.where(is_mine, gm_ref[...], go_ref[...])
        g_ref[...] = gv
        d_ref[...], nm_ref[...], nv_ref[...] = _adamw_math(w_ref[...], gv, m_ref[...], v_ref[...])

    def parked(half, i, holder):
        return idx(0, jnp.where(half == holder, i, jnp.where(half < holder, 0, nr - 1)))

    blk = pl.BlockSpec(bshape, lambda hf, i, wh: idx(hf, i))
    o = jax.ShapeDtypeStruct((rows, cols), F32)
    return pl.pallas_call(
        body,
        grid_spec=pltpu.PrefetchScalarGridSpec(
            num_scalar_prefetch=1,
            grid=(2, nr),
            in_specs=[blk, pl.BlockSpec(bshape, lambda hf, i, wh: parked(hf, i, wh[0])),
                      pl.BlockSpec(bshape, lambda hf, i, wh: parked(hf, i, 1 - wh[0])), blk, blk],
            out_specs=[blk] * 4,
        ),
        out_shape=[o, o, o, o],
        compiler_params=_cparams(("arbitrary", "arbitrary")),
        name=name,
    )(where, w, g_mine, g_other, m, v)


def _adamw(w, g, m, v, *, name):
    rows, cols = w.shape
    tr = _row_tile(rows, cap=256, mult=8)

    def body(w_ref, g_ref, m_ref, v_ref, d_ref, nm_ref, nv_ref):
        d_ref[...], nm_ref[...], nv_ref[...] = _adamw_math(w_ref[...], g_ref[...], m_ref[...], v_ref[...])

    blk = pl.BlockSpec((tr, cols), lambda i: (i, 0))
    o = jax.ShapeDtypeStruct((rows, cols), F32)
    return pl.pallas_call(
        body,
        grid=(rows // tr,),
        in_specs=[blk] * 4,
        out_specs=[blk] * 3,
        out_shape=[o, o, o],
        compiler_params=_cparams(("parallel",)),
        name=name,
    )(w, g, m, v)


def _pack(arrs):
    flat = jnp.concatenate([a.reshape(-1) for a in arrs])
    pad = (-flat.shape[0]) % (8 * LANES)
    return jnp.pad(flat, (0, pad)).reshape(-1, LANES)


def _unpack(packed, shapes):
    flat = packed.reshape(-1)
    out, off = [], 0
    for s in shapes:
        n = 1
        for dim in s:
            n *= dim
        out.append(flat[off:off + n].reshape(s))
        off += n
    return out


def _pad_rows(a, rows):
    return jnp.pad(a, ((0, rows - a.shape[0]), (0, 0)))


def _pad_lanes(a, lanes=LANES):
    return jnp.pad(a, ((0, 0), (0, lanes - a.shape[1])))


def _local_grads(x2d, tgt2d, prm, get_w, on_grads, *, nb, seq, dep=None):
    g_pre, g_post, g_fpre, g_fpost = prm["norm_mix_pre"], prm["norm_mix_post"], prm["norm_ffn_pre"], prm["norm_ffn_post"]
    dt_bias, a_log = _pad_lanes(prm["ssd_dt_bias"]), _pad_lanes(prm["ssd_a_log"])
    d_lanes = jnp.repeat(prm["ssd_d"], HEAD_DIM, axis=1)

    h = _rms_fwd(x2d, g_pre, dep=dep, out_dtype=BF16, name="rms_mix_pre")
    t, d = x2d.shape
    w_in_t, w_dt_t, cw, sw = get_w("in", h)
    proj = _matmul([(h, w_in_t)], mode="nt", out_dtype=F32, tm=1024, tn=1024, tk=2048, name="mm_proj",
                   extent=(t, D_MAIN, d))
    dtr = _matmul([(h, w_dt_t)], mode="nt", out_dtype=F32, tm=1024, tn=128, tk=2048, name="mm_dt")
    u1, uy = _conv_branch_fwd(proj, cw, prm["conv_dw_b"], prm["conv_ln_g"], prm["conv_ln_b"], nb=nb, seq=seq)
    xs_all = _ssd_pre_fwd(proj, sw, prm["ssd_conv_b"], nb=nb, seq=seq)
    uy, ys, states = _ssd_fwd(uy, xs_all, proj, dtr, dt_bias, a_log, d_lanes, prm["ssd_norm_w"], nb=nb, seq=seq)
    w_out = get_w("out", uy)
    mix = _matmul([(uy, w_out)], mode="nn", out_dtype=F32, tm=1024, tn=1024, tk=2048, name="mm_mix")
    x1, h2 = _rms_post_pre(mix, x2d, g_post, g_fpre)
    w_gate, w_up = get_w("up", h2)
    gt, up, act = _ffn_up(h2, w_gate, w_up, tm=1024, tn=512)
    w_down = get_w("down", act)
    dx2, df, loss, dg_fpost = _down_loss(act, w_down, x1, tgt2d, g_fpost, tk=1408)

    dgt, dup = _ffn_bwd_act(df, w_down, gt, up, tm=1024, tn=512)
    dw_down = _matmul([(act, df)], mode="tn", out_dtype=F32, tm=1408, tn=1024, tk=2048, name="mm_dw_down",
                      also_bf16=True)
    dw_gate = _matmul([(h2, dgt)], mode="tn", out_dtype=F32, tm=1024, tn=1408, tk=2048, name="mm_dw_gate",
                      slot_out=True, also_bf16=True)
    dw_up = _matmul([(h2, dup)], mode="tn", out_dtype=F32, tm=1024, tn=1408, tk=2048, name="mm_dw_up",
                    slot_out=True, also_bf16=True)
    dep = on_grads("ffn", (dw_down, dw_gate, dw_up))
    dh2 = _matmul([(dgt, w_gate), (dup, w_up)], mode="nt", out_dtype=F32, tm=1024, tn=1024, tk=1408, name="mm_dh2",
                  dep=dep)
    dx1, dmix, dg_fpre, dg_post = _rms_bwd_pre_post(dh2, x1, g_fpre, dx2, mix, g_post)
    dw_out = _matmul([(uy, dmix)], mode="tn", out_dtype=F32, tm=1024, tn=1024, tk=2048, name="mm_dw_out",
                     also_bf16=True)
    dep = on_grads("out", (dw_out,))
    duy = _matmul([(dmix, w_out)], mode="nt", out_dtype=F32, tm=1024, tn=1024, tk=2048, name="mm_duy", dep=dep)
    dproj, dcw, dcb, dlg, dlb = _conv_branch_bwd(duy, u1, proj, cw, prm["conv_ln_g"], prm["conv_ln_b"], nb=nb, seq=seq)
    dproj, dxs, ddtr, ssd_small = _ssd_bwd(dproj, duy, proj, ys, xs_all, dtr, states, dt_bias, a_log, d_lanes,
                                           prm["ssd_norm_w"], nb=nb, seq=seq)
    dproj, dsw, dsb = _ssd_pre_bwd(dproj, dxs, proj, sw, prm["ssd_conv_b"], nb=nb, seq=seq)
    ddtr_b = ddtr.astype(BF16)
    dw_in_t = _matmul([(dproj, h)], mode="tn", out_dtype=F32, tm=1024, tn=1024, tk=2048, name="mm_dw_main",
                      extent=(D_MAIN, d, t), out_rows=D_IN)
    dw_in_t = _dw_dt_rows(dw_in_t, ddtr_b, h)
    dep = on_grads("in", ((dw_in_t, dw_in_t),))
    dx, dg_pre = _dh_dx(dproj, w_in_t, ddtr_b, w_dt_t, x2d, dx1, g_pre, tk=1280, dep=g_pre if dep is None else dep)

    grads = {
        "norm_mix_pre": dg_pre,
        "w_in": dw_in_t,
        "conv_dw_w": dcw[:CONV_K], "conv_dw_b": dcb, "conv_ln_g": dlg, "conv_ln_b": dlb,
        "ssd_conv_w": dsw[:SSD_CONV_K], "ssd_conv_b": dsb,
        "ssd_dt_bias": ssd_small[2:3, :HEADS], "ssd_a_log": ssd_small[3:4, :HEADS], "ssd_d": ssd_small[1:2, :HEADS],
        "ssd_norm_w": ssd_small[0:1],
        "w_out": dw_out[0],
        "norm_mix_post": dg_post, "norm_ffn_pre": dg_fpre,
        "w_gate": dw_gate[0], "w_up": dw_up[0],
        "w_down": dw_down[0], "norm_ffn_post": dg_fpost,
    }
    return loss, dx, grads


BIG = ("w_in", "w_out", "w_gate", "w_up", "w_down")
HALF_AXIS = {"w_in": 1, "w_out": 0, "w_gate": 0, "w_up": 0, "w_down": 0}
GATHER_STAGES = {"in": ("w_in", "conv_dw_w", "ssd_conv_w"), "out": ("w_out",), "up": ("w_gate", "w_up"),
                 "down": ("w_down",)}
GATHER_ORDER = tuple(n for st in ("in", "out", "up", "down") for n in GATHER_STAGES[st])
SMALL = ("norm_mix_pre", "conv_dw_w", "conv_dw_b", "conv_ln_g", "conv_ln_b", "ssd_conv_w", "ssd_conv_b", "ssd_dt_bias",
         "ssd_a_log", "ssd_d", "ssd_norm_w", "norm_mix_post", "norm_ffn_pre", "norm_ffn_post")
WEIGHTS = ("norm_mix_pre", "w_in", "conv_dw_w", "conv_dw_b", "conv_ln_g", "conv_ln_b", "ssd_conv_w", "ssd_conv_b",
           "ssd_dt_bias", "ssd_a_log", "ssd_d", "ssd_norm_w", "w_out", "norm_mix_post", "norm_ffn_pre", "w_gate", "w_up",
           "w_down", "norm_ffn_post")


def _cols_from_slots(a):
    n, rows, w = a.shape
    return a.transpose(1, 0, 2).reshape(rows, n * w)


def kernel(x, norm_mix_pre, w_in, conv_dw_w, conv_dw_b, conv_ln_g, conv_ln_b, ssd_conv_w, ssd_conv_b, ssd_dt_bias, ssd_a_log, ssd_d, ssd_norm_w, w_out, norm_mix_post, norm_ffn_pre, w_gate, w_up, w_down, norm_ffn_post, loss_target, m_norm_mix_pre, m_w_in, m_conv_dw_w, m_conv_dw_b, m_conv_ln_g, m_conv_ln_b, m_ssd_conv_w, m_ssd_conv_b, m_ssd_dt_bias, m_ssd_a_log, m_ssd_d, m_ssd_norm_w, m_w_out, m_norm_mix_post, m_norm_ffn_pre, m_w_gate, m_w_up, m_w_down, m_norm_ffn_post, v_norm_mix_pre, v_w_in, v_conv_dw_w, v_conv_dw_b, v_conv_ln_g, v_conv_ln_b, v_ssd_conv_w, v_ssd_conv_b, v_ssd_dt_bias, v_ssd_a_log, v_ssd_d, v_ssd_norm_w, v_w_out, v_norm_mix_post, v_norm_ffn_pre, v_w_gate, v_w_up, v_w_down, v_norm_ffn_post):
    args = dict(locals())
    two_d = lambda n, a: jnp.swapaxes(a, 1, 2)[0] if n == "w_in" else a.reshape(a.shape[-2:])
    wts = {n: two_d(n, args[n]) for n in WEIGHTS}
    ms = {n: two_d(n, args["m_" + n]) for n in WEIGHTS}
    vs = {n: two_d(n, args["v_" + n]) for n in WEIGHTS}
    nb, seq, d = x.shape
    t = nb * seq
    xi, yi, ci = _coords()
    chip = 2 * xi + yi
    where = jnp.stack([ci, chip]).astype(jnp.int32)

    shards = {n: wts[n].astype(BF16) for n in BIG}
    shards.update(conv_dw_w=_pad_rows(wts["conv_dw_w"], 32), ssd_conv_w=_pad_rows(wts["ssd_conv_w"], 8))
    srcs = [shards[n] for n in GATHER_ORDER]
    plans = [_gather_plan(HALF_AXIS.get(n)) for n in GATHER_ORDER]
    ssems, rsems, srcs, lands, token = _split_start(
        srcs, [lax.empty((N_CHIPS,) + s.shape, s.dtype) for s in srcs], plans, name="gather_start")

    def get_w(stage, after):
        names = GATHER_STAGES[stage]
        pick = lambda seq_: [seq_[GATHER_ORDER.index(n)] for n in names]
        own, got = _split_wait(pick(ssems), pick(rsems), pick(srcs), pick(lands), pick(plans), after,
                               name="gather_wait_" + stage)
        got, own = dict(zip(names, got)), dict(zip(names, own))
        big = [n for n in names if n in BIG]
        got.update(zip(big, _forward_halves([got[n] for n in big], [HALF_AXIS[n] for n in big],
                                            name="gather_forward_" + stage)))
        full = {n: lax.dynamic_update_slice(got[n], own[n][None], (chip, 0, 0)) for n in got}
        if stage == "in":
            w_in_t = full["w_in"].reshape(D_IN, D_MODEL)
            return (w_in_t, _pad_rows(w_in_t[D_MAIN:], LANES), _cols_from_slots(full["conv_dw_w"]),
                    _cols_from_slots(full["ssd_conv_w"]))
        if stage == "out":
            return full["w_out"].reshape(D_MODEL, D_MODEL)
        if stage == "up":
            return _cols_from_slots(full["w_gate"]), _cols_from_slots(full["w_up"])
        return full["w_down"].reshape(D_FF, D_MODEL)

    reduce_groups = {"ffn": ("w_down", "w_gate", "w_up"), "out": ("w_out",), "in": ("w_in",)}
    in_flight = {}

    def on_grads(stage, gs):
        names = reduce_groups[stage]
        axes = [HALF_AXIS[n] for n in names]
        slot = lambda g: g if g.ndim == 3 else g.reshape((N_CHIPS, g.shape[0] // N_CHIPS, g.shape[1]))
        kept = _swap_other_halves([slot(b16) for _, b16 in gs], axes, name="swap_other_halves_" + stage)
        sums = [_add_core_halves(slot(f32), l, where, ax) for (f32, _), l, ax in zip(gs, kept, axes)]
        ps = [s[1] for s in sums]
        ssem, rsem, ps, recv, started = _split_start(
            ps, [lax.empty((COPIES,) + p.shape[1:], p.dtype) for p in ps], [_owners_plan] * len(ps),
            name="owners_start_" + stage)
        in_flight[stage] = (ssem, rsem, ps, recv, [s[0] for s in sums])
        return started

    prm = {n: wts[n] for n in SMALL}
    loss, dx, grads = _local_grads(x.reshape(t, d), loss_target.reshape(t, d), prm, get_w, on_grads,
                                   nb=nb, seq=seq, dep=token)
    loss = lax.psum(loss[0, 0], MESH_AXES)

    halves = {}
    for stage, names in reduce_groups.items():
        ssem, rsem, ps, recv, own_sums = in_flight[stage]
        _, recv = _split_wait(ssem, rsem, ps, recv, [_owners_plan] * len(ps), dx, name="owners_wait_" + stage)
        for n, f32_sum, r in zip(names, own_sums, recv):
            halves[n] = _add_chip_sums(f32_sum, r, where, HALF_AXIS[n])
    halves = [halves[n] for n in BIG]
    other_halves = _swap_reduced_halves(halves)

    small_shapes = [grads[n].shape for n in SMALL]
    small_sum = _unpack(_all_reduce_small(_pack([grads[n] for n in SMALL])), small_shapes)
    small_grads = dict(zip(SMALL, small_sum))
    cwid, swid = D_CONV // N_CHIPS, D_XBC // N_CHIPS
    small_grads["conv_dw_w"] = lax.dynamic_slice(small_grads["conv_dw_w"], (0, chip * cwid), (CONV_K, cwid))
    small_grads["ssd_conv_w"] = lax.dynamic_slice(small_grads["ssd_conv_w"], (0, chip * swid), (SSD_CONV_K, swid))

    out_g, out_d, out_m, out_v = {}, {}, {}, {}
    for n, mine, other in zip(BIG, halves, other_halves):
        out_g[n], out_d[n], out_m[n], out_v[n] = _adamw_halves(wts[n], mine, other, ms[n], vs[n], where,
                                                                 HALF_AXIS[n], name="adamw_" + n)
    shard_shapes = [wts[n].shape for n in SMALL]
    pd, pm, pv = _adamw(_pack([wts[n] for n in SMALL]), _pack([small_grads[n] for n in SMALL]),
                        _pack([ms[n] for n in SMALL]), _pack([vs[n] for n in SMALL]), name="adamw_small")
    for n, dd, mm, vv in zip(SMALL, _unpack(pd, shard_shapes), _unpack(pm, shard_shapes), _unpack(pv, shard_shapes)):
        out_g[n], out_d[n], out_m[n], out_v[n] = small_grads[n], dd, mm, vv

    back = lambda n, a: jnp.swapaxes(a[None], 1, 2) if n == "w_in" else a.reshape(args[n].shape)
    outs = [back(n, o[n]) for o in (out_g, out_d, out_m, out_v) for n in WEIGHTS]
    return (loss, dx.reshape(nb, seq, d), *outs)
```

```python
import functools

import jax
import jax.numpy as jnp
from jax import lax
from jax.experimental import pallas as pl
from jax.experimental.pallas import tpu as pltpu

F32 = jnp.float32
BF16 = jnp.bfloat16
EPS = 1e-6

D_MODEL = 2048
D_CONV = 1024
D_SSD = 1024
D_XBC = 2048
HEADS = 16
HEAD_DIM = 64
GROUPS = 4
STATE = 128
CONV_K = 31
SSD_CONV_K = 4
D_FF = 5632
D_MAIN = 2 * D_CONV + D_SSD + D_XBC
D_IN = D_MAIN + HEADS
N_CHIPS = 4
LANES = 128
CHUNK = 128
PAIRS = HEADS // 2

ADAM_LR = 0.001
ADAM_B1 = 0.9
ADAM_B2 = 0.999
ADAM_EPS = 1e-08
ADAM_WD = 0.01
ADAM_STEP = 10

MESH_AXES = ("x", "y", "c")
VMEM_LIMIT = 56 * 1024 * 1024


def _sig(v):
    return 1.0 / (1.0 + jnp.exp(-v))


def _cparams(sem, vmem=VMEM_LIMIT):
    return pltpu.CompilerParams(dimension_semantics=sem, vmem_limit_bytes=vmem)


_DIMS = {"nn": ((1,), (0,)), "nt": ((1,), (1,)), "tn": ((0,), (0,))}


def _matmul(pairs, *, mode, out_dtype, tm, tn, tk, name, slot_out=False, extent=None, out_rows=None, dep=None,
            also_bf16=False):
    a0, b0 = pairs[0]
    if mode == "nn":
        (m, k), n = a0.shape, b0.shape[1]
    elif mode == "nt":
        (m, k), n = a0.shape, b0.shape[0]
    else:
        (k, m), n = a0.shape, b0.shape[1]
    if extent is not None:
        m, n, k = extent
    tm, tn, tk = min(tm, m), min(tn, n), min(tk, k)
    assert m % tm == 0 and n % tn == 0 and k % tk == 0, (name, m, n, k, tm, tn, tk)
    nk = k // tk
    npairs = len(pairs)
    deps = [] if dep is None else [dep]
    dims = (_DIMS[mode], ((), ()))

    use_scratch = nk > 1 and out_dtype != F32

    def body(*refs):
        ins, o_ref = refs[: 2 * npairs], refs[2 * npairs + len(deps)]
        dot = lambda p: lax.dot_general(ins[2 * p][...], ins[2 * p + 1][...], dims, preferred_element_type=F32)
        if nk == 1:
            part = dot(0)
            for p in range(1, npairs):
                part = part + dot(p)
            o_ref[...] = part.astype(out_dtype)
            if also_bf16:
                refs[2 * npairs + len(deps) + 1][...] = part.astype(BF16)
            return
        acc = refs[-1] if use_scratch else o_ref
        kk = pl.program_id(2)

        @pl.when(kk == 0)
        def _():
            acc[...] = jnp.zeros_like(acc)

        for p in range(npairs):
            acc[...] += dot(p)

        if use_scratch:
            @pl.when(kk == nk - 1)
            def _():
                o_ref[...] = acc[...].astype(out_dtype)

        if also_bf16:
            @pl.when(kk == nk - 1)
            def _():
                refs[2 * npairs + len(deps) + 1][...] = acc[...].astype(BF16)

    if mode == "nn":
        a_spec = pl.BlockSpec((tm, tk), lambda i, j, kk: (i, kk))
        b_spec = pl.BlockSpec((tk, tn), lambda i, j, kk: (kk, j))
    elif mode == "nt":
        a_spec = pl.BlockSpec((tm, tk), lambda i, j, kk: (i, kk))
        b_spec = pl.BlockSpec((tn, tk), lambda i, j, kk: (j, kk))
    else:
        a_spec = pl.BlockSpec((tk, tm), lambda i, j, kk: (kk, i))
        b_spec = pl.BlockSpec((tk, tn), lambda i, j, kk: (kk, j))
    if slot_out:
        out_shape = jax.ShapeDtypeStruct((n // tn, m, tn), out_dtype)
        out_spec = pl.BlockSpec((None, tm, tn), lambda i, j, kk: (j, i, 0))
    else:
        out_shape = jax.ShapeDtypeStruct((m if out_rows is None else out_rows, n), out_dtype)
        out_spec = pl.BlockSpec((tm, tn), lambda i, j, kk: (i, j))
    flat = [t for ab in pairs for t in ab]
    if also_bf16:
        out_spec = [out_spec, out_spec]
        out_shape = [out_shape, jax.ShapeDtypeStruct(out_shape.shape, BF16)]
    return pl.pallas_call(
        body,
        grid=(m // tm, n // tn, nk),
        in_specs=[a_spec, b_spec] * npairs + [pl.BlockSpec(memory_space=pl.ANY)] * len(deps),
        out_specs=out_spec,
        out_shape=out_shape,
        scratch_shapes=[pltpu.VMEM((tm, tn), F32)] if use_scratch else [],
        compiler_params=_cparams(("parallel", "parallel", "arbitrary")),
        name=name,
    )(*flat, *deps)


SUB_ROWS = 256


def _ffn_up(h2, wg, wu, *, tm, tn):
    t, k = h2.shape
    n = wg.shape[1]
    tm = min(tm, t)
    assert t % tm == 0 and n % tn == 0, (t, n, tm, tn)

    sub = min(SUB_ROWS, tm)

    def body(h_ref, wg_ref, wu_ref, g_ref, u_ref, a_ref):
        for r in range(tm // sub):
            rows = pl.ds(r * sub, sub)
            hv = h_ref[rows, :]
            g = jnp.dot(hv, wg_ref[...], preferred_element_type=F32)
            u = jnp.dot(hv, wu_ref[...], preferred_element_type=F32)
            g_ref[rows, :] = g.astype(BF16)
            u_ref[rows, :] = u.astype(BF16)
            a_ref[rows, :] = (g * _sig(g) * u).astype(BF16)

    o = jax.ShapeDtypeStruct((t, n), BF16)
    ospec = pl.BlockSpec((tm, tn), lambda i, j: (i, j))
    return pl.pallas_call(
        body,
        grid=(t // tm, n // tn),
        in_specs=[pl.BlockSpec((tm, k), lambda i, j: (i, 0)), pl.BlockSpec((k, tn), lambda i, j: (0, j)),
                  pl.BlockSpec((k, tn), lambda i, j: (0, j))],
        out_specs=[ospec, ospec, ospec],
        out_shape=[o, o, o],
        compiler_params=_cparams(("parallel", "parallel")),
        name="ffn_up",
    )(h2, wg, wu)


def _ffn_bwd_act(df, wd, gt, up, *, tm, tn):
    t, k = df.shape
    n = wd.shape[0]
    tm = min(tm, t)
    assert t % tm == 0 and n % tn == 0, (t, n, tm, tn)

    sub = min(SUB_ROWS, tm)

    def body(df_ref, wd_ref, g_ref, u_ref, dg_ref, du_ref):
        for r in range(tm // sub):
            rows = pl.ds(r * sub, sub)
            da = lax.dot_general(df_ref[rows, :], wd_ref[...], (_DIMS["nt"], ((), ())), preferred_element_type=F32)
            g = g_ref[rows, :].astype(F32)
            u = u_ref[rows, :].astype(F32)
            s = _sig(g)
            dg_ref[rows, :] = (da * u * s * (1.0 + g * (1.0 - s))).astype(BF16)
            du_ref[rows, :] = (da * g * s).astype(BF16)

    o = jax.ShapeDtypeStruct((t, n), BF16)
    blk = pl.BlockSpec((tm, tn), lambda i, j: (i, j))
    return pl.pallas_call(
        body,
        grid=(t // tm, n // tn),
        in_specs=[pl.BlockSpec((tm, k), lambda i, j: (i, 0)), pl.BlockSpec((tn, k), lambda i, j: (j, 0)), blk, blk],
        out_specs=[blk, blk],
        out_shape=[o, o],
        compiler_params=_cparams(("parallel", "parallel")),
        name="ffn_bwd_act",
    )(df, wd, gt, up)


def _dw_dt_rows(dw_in_t, ddtr_b, h, *, tk=1024):
    t, d = h.shape
    tk = min(tk, t)
    nk = t // tk

    def body(buf_ref, d_ref, h_ref, o_ref, acc):
        kk = pl.program_id(0)

        @pl.when(kk == 0)
        def _():
            acc[...] = jnp.zeros_like(acc)

        acc[...] += lax.dot_general(d_ref[...], h_ref[...], (_DIMS["tn"], ((), ())), preferred_element_type=F32)

        @pl.when(kk == nk - 1)
        def _():
            o_ref[...] = acc[0:HEADS, :]

    return pl.pallas_call(
        body,
        grid=(nk,),
        in_specs=[DEP_SPEC, pl.BlockSpec((tk, LANES), lambda kk: (kk, 0)), pl.BlockSpec((tk, d), lambda kk: (kk, 0))],
        out_specs=pl.BlockSpec((HEADS, d), lambda kk: (D_MAIN // HEADS, 0)),
        out_shape=jax.ShapeDtypeStruct(dw_in_t.shape, F32),
        input_output_aliases={0: 0},
        scratch_shapes=[pltpu.VMEM((LANES, d), F32)],
        compiler_params=_cparams(("arbitrary",)),
        name="mm_dw_dt",
    )(dw_in_t, ddtr_b, h)


ROW_TILE = 256


DEP_SPEC = pl.BlockSpec(memory_space=pl.ANY)


def _rms_fwd(xv, g, *, res=None, dep=None, out_dtype, name):
    t, d = xv.shape
    has_res = res is not None
    deps = [] if dep is None else [dep]

    def body(*refs):
        x_ref, g_ref = refs[0], refs[1]
        o_ref = refs[-1]
        v = x_ref[...]
        r = lax.rsqrt(jnp.mean(v * v, axis=-1, keepdims=True) + EPS)
        y = v * r * g_ref[...]
        if has_res:
            y = refs[2][...] + y
        o_ref[...] = y.astype(out_dtype)

    row = pl.BlockSpec((ROW_TILE, d), lambda i: (i, 0))
    vec = pl.BlockSpec((1, d), lambda i: (0, 0))
    return pl.pallas_call(
        body,
        grid=(t // ROW_TILE,),
        in_specs=[row, vec] + ([row] if has_res else []) + [DEP_SPEC] * len(deps),
        out_specs=row,
        out_shape=jax.ShapeDtypeStruct((t, d), out_dtype),
        compiler_params=_cparams(("parallel",)),
        name=name,
    )(*([xv, g] + ([res] if has_res else []) + deps))


def _rms_bwd_rows(dy, v, gv):
    r = lax.rsqrt(jnp.mean(v * v, axis=-1, keepdims=True) + EPS)
    xh = v * r
    gdy = dy * gv
    dx = r * (gdy - xh * jnp.mean(gdy * xh, axis=-1, keepdims=True))
    return dx, jnp.sum(dy * xh, axis=0, keepdims=True)


FUSED_ROWS = 512


def _matmul_rows_tail(a, w, tail, *, tk, row_ins, vec_ins, row_outs, vec_outs, first=None, dep=None, name):
    t, kdim = a.shape
    d = w.shape[1]
    tm, tk = min(FUSED_ROWS, t), min(tk, kdim)
    nk, nb = kdim // tk, t // tm
    assert t % tm == 0 and kdim % tk == 0
    n_ri, n_vi, n_ro, n_vo = len(row_ins), len(vec_ins), len(row_outs), len(vec_outs)
    n_first = 0 if first is None else 2
    deps = [] if dep is None else [dep]

    def body(*refs):
        a_ref, w_ref = refs[0], refs[1]
        first_refs = refs[2:2 + n_first]
        p = 2 + n_first
        ri = refs[p:p + n_ri]
        vi = refs[p + n_ri:p + n_ri + n_vi]
        p += n_ri + n_vi + len(deps)
        ro = refs[p:p + n_ro]
        vo = refs[p + n_ro:p + n_ro + n_vo]
        acc = refs[-1]
        i, kk = pl.program_id(0), pl.program_id(1)

        @pl.when(jnp.logical_and(i == 0, kk == 0))
        def _():
            for ref in vo:
                ref[...] = jnp.zeros_like(ref)

        @pl.when(kk == 0)
        def _():
            if first is None:
                acc[...] = jnp.zeros_like(acc)
            else:
                acc[...] = jnp.dot(first_refs[0][...], first_refs[1][...], preferred_element_type=F32)

        acc[...] += jnp.dot(a_ref[...], w_ref[...], preferred_element_type=F32)

        @pl.when(kk == nk - 1)
        def _():
            outs, parts = tail(acc[...], [r[...] for r in ri], [v[...] for v in vi])
            for ref, val in zip(ro, outs):
                ref[...] = val.astype(ref.dtype)
            for ref, part in zip(vo, parts):
                ref[...] += part

    row = pl.BlockSpec((tm, d), lambda i, kk: (i, 0))
    const = lambda shape: pl.BlockSpec(shape, lambda i, kk: (0,) * len(shape))
    in_specs = [pl.BlockSpec((tm, tk), lambda i, kk: (i, kk)), pl.BlockSpec((tk, d), lambda i, kk: (kk, 0))]
    if first is not None:
        in_specs += [pl.BlockSpec((tm, first[0].shape[1]), lambda i, kk: (i, 0)), const(first[1].shape)]
    in_specs += [row] * n_ri + [const(v.shape) for v in vec_ins] + [pl.BlockSpec(memory_space=pl.ANY)] * len(deps)
    return pl.pallas_call(
        body,
        grid=(nb, nk),
        in_specs=in_specs,
        out_specs=[row] * n_ro + [const(sh) for sh in vec_outs],
        out_shape=[jax.ShapeDtypeStruct((t, d), dt) for dt in row_outs]
        + [jax.ShapeDtypeStruct(sh, F32) for sh in vec_outs],
        scratch_shapes=[pltpu.VMEM((tm, d), F32)],
        compiler_params=_cparams(("arbitrary", "arbitrary")),
        name=name,
    )(a, w, *([] if first is None else list(first)), *row_ins, *vec_ins, *deps)


def _down_loss(act, w_down, x1, tgt, g, *, tk):
    d = w_down.shape[1]

    def tail(v, rows, vecs):
        x1v, tv = rows
        gv, = vecs
        fh = v * lax.rsqrt(jnp.mean(v * v, axis=-1, keepdims=True) + EPS)
        e = x1v + fh * gv - tv
        dx2 = e * (1.0 / d)
        df, dg = _rms_bwd_rows(dx2, v, gv)
        loss = 0.5 * jnp.sum(jnp.mean(e * e, axis=-1, keepdims=True), axis=0, keepdims=True)
        return (dx2, df), (loss, dg)

    return _matmul_rows_tail(act, w_down, tail, tk=tk, row_ins=[x1, tgt], vec_ins=[g], row_outs=[F32, BF16],
                             vec_outs=[(1, 1), (1, d)], name="mm_down_loss")


def _dh_dx(dproj, w_in_t, ddtr_b, w_dt_t, xv, dx1, g, *, tk, dep):
    def tail(v, rows, vecs):
        xr, dx1r = rows
        dx, dg = _rms_bwd_rows(v, xr, vecs[0])
        return (dx + dx1r,), (dg,)

    return _matmul_rows_tail(dproj, w_in_t, tail, tk=tk, row_ins=[xv, dx1], vec_ins=[g], row_outs=[F32],
                             vec_outs=[(1, xv.shape[1])], first=(ddtr_b, w_dt_t), dep=dep, name="mm_dh_dx")


def _rms_post_pre(mix, xv, g_post, g_pre):
    t, d = mix.shape

    def body(m_ref, x_ref, gp_ref, gf_ref, x1_ref, h2_ref):
        v = m_ref[...]
        x1 = x_ref[...] + v * lax.rsqrt(jnp.mean(v * v, axis=-1, keepdims=True) + EPS) * gp_ref[...]
        x1_ref[...] = x1
        h2_ref[...] = (x1 * lax.rsqrt(jnp.mean(x1 * x1, axis=-1, keepdims=True) + EPS) * gf_ref[...]).astype(BF16)

    row = pl.BlockSpec((ROW_TILE, d), lambda i: (i, 0))
    vec = pl.BlockSpec((1, d), lambda i: (0, 0))
    return pl.pallas_call(
        body,
        grid=(t // ROW_TILE,),
        in_specs=[row, row, vec, vec],
        out_specs=[row, row],
        out_shape=[jax.ShapeDtypeStruct((t, d), F32), jax.ShapeDtypeStruct((t, d), BF16)],
        compiler_params=_cparams(("parallel",)),
        name="rms_mix_post_ffn_pre",
    )(mix, xv, g_post, g_pre)


def _rms_bwd_pre_post(dh2, x1, g_pre, dx2, mix, g_post):
    t, d = x1.shape

    def body(dh_ref, x1_ref, gf_ref, dx2_ref, m_ref, gp_ref, dx1_ref, dmix_ref, dgf_ref, dgp_ref):
        @pl.when(pl.program_id(0) == 0)
        def _():
            dgf_ref[...] = jnp.zeros_like(dgf_ref)
            dgp_ref[...] = jnp.zeros_like(dgp_ref)

        dx, dgf = _rms_bwd_rows(dh_ref[...], x1_ref[...], gf_ref[...])
        dx1 = dx + dx2_ref[...]
        dx1_ref[...] = dx1
        dmix, dgp = _rms_bwd_rows(dx1, m_ref[...], gp_ref[...])
        dmix_ref[...] = dmix.astype(BF16)
        dgf_ref[...] += dgf
        dgp_ref[...] += dgp

    row = pl.BlockSpec((ROW_TILE, d), lambda i: (i, 0))
    vec = pl.BlockSpec((1, d), lambda i: (0, 0))
    return pl.pallas_call(
        body,
        grid=(t // ROW_TILE,),
        in_specs=[row, row, vec, row, row, vec],
        out_specs=[row, row, vec, vec],
        out_shape=[jax.ShapeDtypeStruct((t, d), F32), jax.ShapeDtypeStruct((t, d), BF16),
                   jax.ShapeDtypeStruct((1, d), F32), jax.ShapeDtypeStruct((1, d), F32)],
        compiler_params=_cparams(("arbitrary",)),
        name="rms_ffn_pre_mix_post_bwd",
    )(dh2, x1, g_pre, dx2, mix, g_post)


CONV_ROWS = 256
TAP_ROWS = 64
HALO31 = 32
HALO4 = 8


def _sum8(v):
    return jnp.sum(v.reshape(v.shape[0] // 8, 8, v.shape[1]), axis=0)


SUBLANES = 8
PHASE_SPAN = (CONV_K - 1) // SUBLANES * SUBLANES


def _phase_scratch(ts):
    return pltpu.VMEM((SUBLANES, ts + PHASE_SPAN, LANES), F32)


def _phase_copies(ph, buf, ln, base, ts):
    for s in range(SUBLANES):
        n = ts + (CONV_K - 1 - s) // SUBLANES * SUBLANES
        ph[s, 0:n, :] = buf[pl.ds(base + s, n), ln]


def _tap_rows(ph, off, r0):
    s = off % SUBLANES
    return ph[s, pl.ds(r0 + off - s, TAP_ROWS), :]


def _conv_branch_fwd(proj, cw, cb, lg, lb, *, nb, seq):
    ts, c, halo = CONV_ROWS, D_CONV, HALO31
    ns = seq // ts
    base = halo - CONV_K + 1

    def body(ca_ref, cg_ref, w_ref, b_ref, lg_ref, lb_ref, u1_ref, u_ref, ubuf, uph):
        i = pl.program_id(1)

        @pl.when(i == 0)
        def _():
            ubuf[0:halo, :] = jnp.zeros((halo, c), F32)

        @pl.when(i > 0)
        def _():
            ubuf[0:halo, :] = ubuf[ts:ts + halo, :]

        ubuf[halo:halo + ts, :] = ca_ref[...] * _sig(cg_ref[...])

        def lane_tile(j, carry):
            ln = pl.ds(pl.multiple_of(j * LANES, LANES), LANES)
            _phase_copies(uph, ubuf, ln, base, ts)
            for r in range(ts // TAP_ROWS):
                acc = jnp.broadcast_to(b_ref[:, ln], (TAP_ROWS, LANES))
                for k in range(CONV_K):
                    acc = acc + w_ref[pl.ds(k, 1), ln] * _tap_rows(uph, k, r * TAP_ROWS)
                u1_ref[pl.ds(r * TAP_ROWS, TAP_ROWS), ln] = acc
            return carry

        lax.fori_loop(0, c // LANES, lane_tile, 0)
        v = u1_ref[...]
        mu = jnp.mean(v, axis=-1, keepdims=True)
        dv = v - mu
        xh = dv * lax.rsqrt(jnp.mean(dv * dv, axis=-1, keepdims=True) + EPS)
        u2 = xh * lg_ref[...] + lb_ref[...]
        u_ref[...] = (u2 * _sig(u2)).astype(BF16)

    t = nb * seq
    row = lambda col: pl.BlockSpec((ts, c), lambda b, i: (b * ns + i, col))
    vec = pl.BlockSpec((1, c), lambda b, i: (0, 0))
    return pl.pallas_call(
        body,
        grid=(nb, ns),
        in_specs=[row(0), row(1), pl.BlockSpec((32, c), lambda b, i: (0, 0)), vec, vec, vec],
        out_specs=[row(0), row(0)],
        out_shape=[jax.ShapeDtypeStruct((t, c), F32), jax.ShapeDtypeStruct((t, c + D_SSD), BF16)],
        scratch_shapes=[pltpu.VMEM((halo + ts, c), F32), _phase_scratch(ts)],
        compiler_params=_cparams(("parallel", "arbitrary")),
        name="conv_branch_fwd",
    )(proj, proj, cw, cb, lg, lb)


def _conv_branch_bwd(duy, u1, proj, cw, lg, lb, *, nb, seq):
    ts, c, halo = CONV_ROWS, D_CONV, HALO31
    ns = seq // ts
    base = halo - CONV_K + 1
    hb = ts // halo

    def body(du_ref, u1_ref, ca_ref, cg_ref, cah_ref, cgh_ref, w_ref, lg_ref, lb_ref,
             dcacg_ref, dw_ref, db_ref, dlg_ref, dlb_ref,
             ubuf, dbuf, du0buf, dwacc, dbacc, dlgacc, dlbacc, uph, dph):
        b, i = pl.program_id(0), pl.program_id(1)
        rc = ns - 1 - i

        @pl.when(jnp.logical_and(b == 0, i == 0))
        def _():
            dwacc[...] = jnp.zeros_like(dwacc)
            dbacc[...] = jnp.zeros_like(dbacc)
            dlgacc[...] = jnp.zeros_like(dlgacc)
            dlbacc[...] = jnp.zeros_like(dlbacc)

        @pl.when(i == 0)
        def _():
            dbuf[ts:ts + halo, :] = jnp.zeros((halo, c), F32)

        @pl.when(i > 0)
        def _():
            dbuf[ts:ts + halo, :] = dbuf[0:halo, :]

        v = u1_ref[...]
        mu = jnp.mean(v, axis=-1, keepdims=True)
        dv = v - mu
        rstd = lax.rsqrt(jnp.mean(dv * dv, axis=-1, keepdims=True) + EPS)
        xh = dv * rstd
        lgv = lg_ref[...]
        u2 = xh * lgv + lb_ref[...]
        s2 = _sig(u2)
        du2 = du_ref[...] * (s2 * (1.0 + u2 * (1.0 - s2)))
        dlgacc[...] += jnp.sum(du2 * xh, axis=0, keepdims=True)
        dlbacc[...] += jnp.sum(du2, axis=0, keepdims=True)
        gd = du2 * lgv
        du1 = rstd * (gd - jnp.mean(gd, axis=-1, keepdims=True) - xh * jnp.mean(gd * xh, axis=-1, keepdims=True))
        dbacc[...] += jnp.sum(du1, axis=0, keepdims=True)
        dbuf[0:ts, :] = du1

        @pl.when(rc == 0)
        def _():
            ubuf[0:halo, :] = jnp.zeros((halo, c), F32)

        @pl.when(rc > 0)
        def _():
            ubuf[0:halo, :] = cah_ref[...] * _sig(cgh_ref[...])

        cav = ca_ref[...]
        sg = _sig(cg_ref[...])
        ubuf[halo:halo + ts, :] = cav * sg

        def lane_tile(j, carry):
            ln = pl.ds(pl.multiple_of(j * LANES, LANES), LANES)
            _phase_copies(uph, ubuf, ln, base, ts)
            _phase_copies(dph, dbuf, ln, 0, ts)
            for r in range(ts // TAP_ROWS):
                r0 = r * TAP_ROWS
                d1 = dbuf[pl.ds(r0, TAP_ROWS), ln]
                acc = jnp.zeros((TAP_ROWS, LANES), F32)
                for k in range(CONV_K):
                    acc = acc + w_ref[pl.ds(k, 1), ln] * _tap_rows(dph, CONV_K - 1 - k, r0)
                    dwacc[pl.ds(k * 8, 8), ln] += _sum8(d1 * _tap_rows(uph, k, r0))
                du0buf[pl.ds(r0, TAP_ROWS), ln] = acc
            return carry

        lax.fori_loop(0, c // LANES, lane_tile, 0)
        du0 = du0buf[...]
        dcacg_ref[:, 0:c] = (du0 * sg).astype(BF16)
        dcacg_ref[:, c:2 * c] = (du0 * cav * sg * (1.0 - sg)).astype(BF16)

        @pl.when(jnp.logical_and(b == nb - 1, i == ns - 1))
        def _():
            for k in range(CONV_K):
                dw_ref[pl.ds(k, 1), :] = jnp.sum(dwacc[pl.ds(k * 8, 8), :], axis=0, keepdims=True)
            dw_ref[pl.ds(CONV_K, 1), :] = jnp.zeros((1, c), F32)
            db_ref[...] = dbacc[...]
            dlg_ref[...] = dlgacc[...]
            dlb_ref[...] = dlbacc[...]

    t = nb * seq
    rowblk = lambda b, i: b * ns + (ns - 1 - i)
    row = lambda col: pl.BlockSpec((ts, c), lambda b, i: (rowblk(b, i), col))
    hrow = lambda col: pl.BlockSpec((halo, c), lambda b, i: (jnp.maximum(rowblk(b, i) * hb - 1, 0), col))
    vec = pl.BlockSpec((1, c), lambda b, i: (0, 0))
    wspec = pl.BlockSpec((32, c), lambda b, i: (0, 0))
    return pl.pallas_call(
        body,
        grid=(nb, ns),
        in_specs=[row(0), row(0), row(0), row(1), hrow(0), hrow(1), wspec, vec, vec],
        out_specs=[pl.BlockSpec((ts, 2 * c), lambda b, i: (rowblk(b, i), 0)), wspec, vec, vec, vec],
        out_shape=[jax.ShapeDtypeStruct((t, D_MAIN), BF16), jax.ShapeDtypeStruct((32, c), F32),
                   jax.ShapeDtypeStruct((1, c), F32), jax.ShapeDtypeStruct((1, c), F32), jax.ShapeDtypeStruct((1, c), F32)],
        scratch_shapes=[pltpu.VMEM((halo + ts, c), F32), pltpu.VMEM((ts + halo, c), F32), pltpu.VMEM((ts, c), F32),
                        pltpu.VMEM((CONV_K * 8, c), F32), pltpu.VMEM((1, c), F32), pltpu.VMEM((1, c), F32),
                        pltpu.VMEM((1, c), F32), _phase_scratch(ts), _phase_scratch(ts)],
        compiler_params=_cparams(("arbitrary", "arbitrary")),
        name="conv_branch_bwd",
    )(duy, u1, proj, proj, proj, proj, cw, lg, lb)


XBC_COL0 = (2 * D_CONV + D_SSD) // 1024


def _ssd_pre_fwd(proj, sw, sb, *, nb, seq):
    ts, c, halo = CONV_ROWS, 1024, HALO4
    ns = seq // ts
    base = halo - SSD_CONV_K + 1

    def body(x_ref, w_ref, b_ref, o_ref, xbuf):
        i = pl.program_id(2)

        @pl.when(i == 0)
        def _():
            xbuf[0:halo, :] = jnp.zeros((halo, c), F32)

        @pl.when(i > 0)
        def _():
            xbuf[0:halo, :] = xbuf[ts:ts + halo, :]

        xbuf[halo:halo + ts, :] = x_ref[...]

        def lane_tile(j, carry):
            ln = pl.ds(pl.multiple_of(j * LANES, LANES), LANES)
            for r in range(ts // TAP_ROWS):
                acc = jnp.broadcast_to(b_ref[:, ln], (TAP_ROWS, LANES))
                for k in range(SSD_CONV_K):
                    acc = acc + w_ref[pl.ds(k, 1), ln] * xbuf[pl.ds(r * TAP_ROWS + base + k, TAP_ROWS), ln]
                o_ref[pl.ds(r * TAP_ROWS, TAP_ROWS), ln] = acc * _sig(acc)
            return carry

        lax.fori_loop(0, c // LANES, lane_tile, 0)

    t = nb * seq
    return pl.pallas_call(
        body,
        grid=(2, nb, ns),
        in_specs=[pl.BlockSpec((ts, c), lambda j, b, i: (b * ns + i, XBC_COL0 + j)),
                  pl.BlockSpec((8, c), lambda j, b, i: (0, j)), pl.BlockSpec((1, c), lambda j, b, i: (0, j))],
        out_specs=pl.BlockSpec((ts, c), lambda j, b, i: (b * ns + i, j)),
        out_shape=jax.ShapeDtypeStruct((t, D_XBC), F32),
        scratch_shapes=[pltpu.VMEM((halo + ts, c), F32)],
        compiler_params=_cparams(("parallel", "parallel", "arbitrary")),
        name="ssd_pre_fwd",
    )(proj, sw, sb)


def _ssd_pre_bwd(dproj, dxs, proj, sw, sb, *, nb, seq):
    ts, c, halo = CONV_ROWS, 1024, HALO4
    ns = seq // ts
    base = halo - SSD_CONV_K + 1
    hb = ts // halo

    def body(dproj_ref, d_ref, x_ref, xh_ref, w_ref, b_ref, dx_ref, dw_ref, db_ref, xbuf, dbuf, dwacc, dbacc):
        b, i = pl.program_id(1), pl.program_id(2)
        rc = ns - 1 - i

        @pl.when(jnp.logical_and(b == 0, i == 0))
        def _():
            dwacc[...] = jnp.zeros_like(dwacc)
            dbacc[...] = jnp.zeros_like(dbacc)

        @pl.when(i == 0)
        def _():
            dbuf[ts:ts + halo, :] = jnp.zeros((halo, c), F32)

        @pl.when(i > 0)
        def _():
            dbuf[ts:ts + halo, :] = dbuf[0:halo, :]

        @pl.when(rc == 0)
        def _():
            xbuf[0:halo, :] = jnp.zeros((halo, c), F32)

        @pl.when(rc > 0)
        def _():
            xbuf[0:halo, :] = xh_ref[...]

        xbuf[halo:halo + ts, :] = x_ref[...]

        def pre_tile(j, carry):
            ln = pl.ds(pl.multiple_of(j * LANES, LANES), LANES)
            for r in range(ts // TAP_ROWS):
                r0 = r * TAP_ROWS
                acc = jnp.broadcast_to(b_ref[:, ln], (TAP_ROWS, LANES))
                for k in range(SSD_CONV_K):
                    acc = acc + w_ref[pl.ds(k, 1), ln] * xbuf[pl.ds(r0 + base + k, TAP_ROWS), ln]
                s = _sig(acc)
                dc = d_ref[pl.ds(r0, TAP_ROWS), ln] * (s * (1.0 + acc * (1.0 - s)))
                dbuf[pl.ds(r0, TAP_ROWS), ln] = dc
                dbacc[:, ln] += _sum8(dc)
            return carry

        lax.fori_loop(0, c // LANES, pre_tile, 0)

        def lane_tile(j, carry):
            ln = pl.ds(pl.multiple_of(j * LANES, LANES), LANES)
            for r in range(ts // TAP_ROWS):
                r0 = r * TAP_ROWS
                d1 = dbuf[pl.ds(r0, TAP_ROWS), ln]
                acc = jnp.zeros((TAP_ROWS, LANES), F32)
                for k in range(SSD_CONV_K):
                    acc = acc + w_ref[pl.ds(k, 1), ln] * dbuf[pl.ds(r0 + SSD_CONV_K - 1 - k, TAP_ROWS), ln]
                    dwacc[pl.ds(k * 8, 8), ln] += _sum8(d1 * xbuf[pl.ds(r0 + base + k, TAP_ROWS), ln])
                dx_ref[pl.ds(r0, TAP_ROWS), ln] = acc.astype(BF16)
            return carry

        lax.fori_loop(0, c // LANES, lane_tile, 0)

        @pl.when(jnp.logical_and(b == nb - 1, i == ns - 1))
        def _():
            for k in range(SSD_CONV_K):
                dw_ref[pl.ds(k, 1), :] = jnp.sum(dwacc[pl.ds(k * 8, 8), :], axis=0, keepdims=True)
            dw_ref[pl.ds(SSD_CONV_K, 8 - SSD_CONV_K), :] = jnp.zeros((8 - SSD_CONV_K, c), F32)
            db_ref[...] = jnp.sum(dbacc[...], axis=0, keepdims=True)

    t = nb * seq
    rowblk = lambda b, i: b * ns + (ns - 1 - i)
    return pl.pallas_call(
        body,
        grid=(2, nb, ns),
        in_specs=[DEP_SPEC, pl.BlockSpec((ts, c), lambda j, b, i: (rowblk(b, i), j)),
                  pl.BlockSpec((ts, c), lambda j, b, i: (rowblk(b, i), XBC_COL0 + j)),
                  pl.BlockSpec((halo, c), lambda j, b, i: (jnp.maximum(rowblk(b, i) * hb - 1, 0), XBC_COL0 + j)),
                  pl.BlockSpec((8, c), lambda j, b, i: (0, j)), pl.BlockSpec((1, c), lambda j, b, i: (0, j))],
        out_specs=[pl.BlockSpec((ts, c), lambda j, b, i: (rowblk(b, i), XBC_COL0 + j)),
                   pl.BlockSpec((8, c), lambda j, b, i: (0, j)), pl.BlockSpec((1, c), lambda j, b, i: (0, j))],
        out_shape=[jax.ShapeDtypeStruct(dproj.shape, BF16), jax.ShapeDtypeStruct((8, D_XBC), F32),
                   jax.ShapeDtypeStruct((1, D_XBC), F32)],
        input_output_aliases={0: 0},
        scratch_shapes=[pltpu.VMEM((halo + ts, c), F32), pltpu.VMEM((ts + halo, c), F32),
                        pltpu.VMEM((SSD_CONV_K * 8, c), F32), pltpu.VMEM((8, c), F32)],
        compiler_params=_cparams(("arbitrary", "arbitrary", "arbitrary")),
        name="ssd_pre_bwd",
    )(dproj, dxs, proj, proj, sw, sb)


Z_COL = (2 * D_CONV) // 1024
GROUP_W = D_SSD // GROUPS


def _softplus(v):
    return jnp.maximum(v, 0.0) + jnp.log(1.0 + jnp.exp(-jnp.abs(v)))


def _dot(a, b):
    return jnp.dot(a, b, preferred_element_type=F32)


def _dot_nt(a, b):
    return lax.dot_general(a, b, (_DIMS["nt"], ((), ())), preferred_element_type=F32)


def _dot_tn(a, b):
    return lax.dot_general(a, b, (_DIMS["tn"], ((), ())), preferred_element_type=F32)


def _bf16_terms(v):
    hi = v.astype(BF16)
    r1 = v - hi.astype(F32)
    mid = r1.astype(BF16)
    return hi, mid, (r1 - mid.astype(F32)).astype(BF16)


def _dot_exact_left(sel, v):
    hi, mid, lo = _bf16_terms(v)
    return _dot(sel, hi) + (_dot(sel, mid) + _dot(sel, lo))


def _dot_exact_right(v, sel):
    hi, mid, lo = _bf16_terms(v)
    return _dot(hi, sel) + (_dot(mid, sel) + _dot(lo, sel))


def _chunk_decays(dtr_ref, bias_ref, alog_ref):
    q = CHUNK
    ii = lax.broadcasted_iota(jnp.int32, (q, q), 0)
    jj = lax.broadcasted_iota(jnp.int32, (q, q), 1)
    tri = jj <= ii
    dt = _softplus(dtr_ref[...] + bias_ref[...])
    a_head = -jnp.exp(alog_ref[...])
    cs = _dot_exact_left(tri.astype(BF16), dt * a_head)
    return tri, dt, a_head, cs, cs.T


def _ssd_fwd(uy, xs_all, proj, dtr, dt_bias, a_log, d_lanes, norm_w, *, nb, seq):
    q = CHUNK
    nc = seq // q
    t = nb * seq

    def body(uy_ref, xs_ref, bm_ref, cm_ref, z_ref, dtr_ref, bias_ref, alog_ref, dl_ref, nw_ref,
             y_ref, ys_ref, st_ref, state):
        @pl.when(pl.program_id(1) == 0)
        def _():
            state[...] = jnp.zeros_like(state)

        tri, dt, _, cs, cst = _chunk_decays(dtr_ref, bias_ref, alog_ref)
        first = lax.broadcasted_iota(jnp.int32, (1, LANES), 1) < HEAD_DIM
        for g in range(GROUPS):
            gl = slice(g * STATE, (g + 1) * STATE)
            bb = bm_ref[:, gl].astype(BF16)
            cb = cm_ref[:, gl].astype(BF16)
            scores = _dot_nt(cb, bb)
            for p in range(2):
                pr = 2 * g + p
                h0 = 2 * pr
                sl = slice(pr * LANES, (pr + 1) * LANES)
                xv = xs_ref[:, sl]
                dtp = jnp.where(first, dt[:, h0:h0 + 1], dt[:, h0 + 1:h0 + 2])
                csp = jnp.where(first, cs[:, h0:h0 + 1], cs[:, h0 + 1:h0 + 2])
                xd = xv * dtp
                yv = None
                for hh, keep in ((h0, first), (h0 + 1, jnp.logical_not(first))):
                    decay = jnp.where(tri, jnp.exp(cs[:, hh:hh + 1] - cst[hh:hh + 1, :]), 0.0)
                    part = _dot((scores * decay).astype(BF16), jnp.where(keep, xd, 0.0).astype(BF16))
                    yv = part if yv is None else yv + part
                hp = state[pr]
                st_ref[0, pr] = hp
                yv = yv + jnp.exp(csp) * _dot(cb, hp.astype(BF16))
                last = csp[q - 1:q, :]
                state[pr] = jnp.exp(last) * hp + _dot_tn(bb, (xd * jnp.exp(last - csp)).astype(BF16))
                ys_ref[:, sl] = yv + dl_ref[:, sl] * xv
        zv = z_ref[...]
        gated = ys_ref[...] * (zv * _sig(zv))
        for g in range(GROUPS):
            gl = slice(g * GROUP_W, (g + 1) * GROUP_W)
            v = gated[:, gl]
            r = lax.rsqrt(jnp.mean(v * v, axis=-1, keepdims=True) + EPS)
            y_ref[:, gl] = (v * r * nw_ref[:, gl]).astype(BF16)

    blk = lambda w, col: pl.BlockSpec((q, w), lambda b, c: (b * nc + c, col))
    vec = lambda w: pl.BlockSpec((1, w), lambda b, c: (0, 0))
    return pl.pallas_call(
        body,
        grid=(nb, nc),
        in_specs=[DEP_SPEC, blk(D_SSD, 0), blk(GROUPS * STATE, 2), blk(GROUPS * STATE, 3), blk(D_SSD, Z_COL),
                  blk(LANES, 0), vec(LANES), vec(LANES), vec(D_SSD), vec(D_SSD)],
        out_specs=[blk(D_SSD, 1), blk(D_SSD, 0),
                   pl.BlockSpec((1, PAIRS, STATE, LANES), lambda b, c: (b * nc + c, 0, 0, 0))],
        out_shape=[jax.ShapeDtypeStruct(uy.shape, BF16), jax.ShapeDtypeStruct((t, D_SSD), F32),
                   jax.ShapeDtypeStruct((nb * nc, PAIRS, STATE, LANES), F32)],
        input_output_aliases={0: 0},
        scratch_shapes=[pltpu.VMEM((PAIRS, STATE, LANES), F32)],
        compiler_params=_cparams(("parallel", "arbitrary")),
        name="ssd_fwd",
    )(uy, xs_all, xs_all, xs_all, proj, dtr, dt_bias, a_log, d_lanes, norm_w)


def _ssd_bwd(dproj, duy, proj, ys, xs_all, dtr, states, dt_bias, a_log, d_lanes, norm_w, *, nb, seq):
    q = CHUNK
    nc = seq // q
    t = nb * seq
    head_of_lane = (jnp.arange(D_SSD)[:, None] // HEAD_DIM == jnp.arange(LANES)[None, :]).astype(BF16)

    def body(dproj_ref, dy_ref, z_ref, ys_ref, xs_ref, bm_ref, cm_ref, dtr_ref, st_ref, bias_ref, alog_ref, dl_ref,
             nw_ref, sel_ref, dz_ref, dx_ref, ddtr_ref, small_ref,
             dstate, dys_buf, dcsl, ddtl, dcst, dnw_acc, dd_acc, dbias_acc, da_acc):
        b, c = pl.program_id(0), pl.program_id(1)

        @pl.when(jnp.logical_and(b == 0, c == 0))
        def _():
            dnw_acc[...] = jnp.zeros_like(dnw_acc)
            dd_acc[...] = jnp.zeros_like(dd_acc)
            dbias_acc[...] = jnp.zeros_like(dbias_acc)
            da_acc[...] = jnp.zeros_like(da_acc)
            dcst[...] = jnp.zeros_like(dcst)

        @pl.when(c == 0)
        def _():
            dstate[...] = jnp.zeros_like(dstate)

        zv = z_ref[...]
        sz = _sig(zv)
        silz = zv * sz
        ysv = ys_ref[...]
        gated = ysv * silz
        dyv = dy_ref[...]
        nwv = nw_ref[...]
        for g in range(GROUPS):
            gl = slice(g * GROUP_W, (g + 1) * GROUP_W)
            v = gated[:, gl]
            r = lax.rsqrt(jnp.mean(v * v, axis=-1, keepdims=True) + EPS)
            yn = v * r
            dyn = dyv[:, gl] * nwv[:, gl]
            dnw_acc[:, gl] += jnp.sum(dyv[:, gl] * yn, axis=0, keepdims=True)
            dys_buf[:, gl] = r * (dyn - yn * jnp.mean(dyn * yn, axis=-1, keepdims=True))
        dgated = dys_buf[...]
        dz_ref[...] = (dgated * ysv * (sz * (1.0 + zv * (1.0 - sz)))).astype(BF16)
        dys_all = dgated * silz
        dys_buf[...] = dys_all
        dd_acc[...] += jnp.sum(dys_all * xs_ref[...], axis=0, keepdims=True)

        tri, dt, a_head, cs, cst = _chunk_decays(dtr_ref, bias_ref, alog_ref)
        lane = lax.broadcasted_iota(jnp.int32, (1, LANES), 1)
        first = lane < HEAD_DIM
        dcs_h = jnp.zeros((q, LANES), F32)
        for g in range(GROUPS):
            gl = slice(g * STATE, (g + 1) * STATE)
            bb = bm_ref[:, gl].astype(BF16)
            cb = cm_ref[:, gl].astype(BF16)
            scores = _dot_nt(cb, bb)
            dscores = jnp.zeros((q, q), F32)
            dbg = jnp.zeros((q, STATE), F32)
            dcg = jnp.zeros((q, STATE), F32)
            for p in range(2):
                pr = 2 * g + p
                h0 = 2 * pr
                sl = slice(pr * LANES, (pr + 1) * LANES)
                xv = xs_ref[:, sl]
                dyp = dys_buf[:, sl]
                dtp = jnp.where(first, dt[:, h0:h0 + 1], dt[:, h0 + 1:h0 + 2])
                csp = jnp.where(first, cs[:, h0:h0 + 1], cs[:, h0 + 1:h0 + 2])
                xd = xv * dtp
                xdb = xd.astype(BF16)
                hp = st_ref[0, pr]
                dhn = dstate[pr]
                hpb = hp.astype(BF16)
                dhnb = dhn.astype(BF16)
                lam = jnp.exp(csp)
                last = csp[q - 1:q, :]
                gam = jnp.exp(last)
                w = jnp.exp(last - csp)
                dxd = jnp.zeros((q, LANES), F32)
                for hh, keep in ((h0, first), (h0 + 1, jnp.logical_not(first))):
                    decay = jnp.where(tri, jnp.exp(cs[:, hh:hh + 1] - cst[hh:hh + 1, :]), 0.0)
                    m = scores * decay
                    dym = jnp.where(keep, dyp, 0.0).astype(BF16)
                    dm = _dot_nt(dym, xdb)
                    dxd = dxd + _dot_tn(m.astype(BF16), dym)
                    e = dm * m
                    dcs_h = dcs_h + jnp.where(lane == hh, jnp.sum(e, axis=1, keepdims=True), 0.0)
                    dcst[hh:hh + 1, :] = jnp.sum(e, axis=0, keepdims=True)
                    dscores = dscores + dm * decay
                yoff = lam * _dot(cb, hpb)
                ldy = (lam * dyp).astype(BF16)
                dcg = dcg + _dot_nt(ldy, hpb)
                dstate[pr] = gam * dhn + _dot_tn(cb, ldy)
                bdh = _dot(bb, dhnb)
                dxd = dxd + w * bdh
                xdw = xd * w
                dbg = dbg + _dot_nt(xdw.astype(BF16), dhnb)
                wd = xdw * bdh
                dcsl[:, sl] = dyp * yoff - wd
                dcsl[q - 1:q, sl] += (jnp.sum(wd, axis=0, keepdims=True)
                                      + gam * jnp.sum(dhn * hp, axis=0, keepdims=True))
                dx_ref[:, sl] = dxd * dtp + dyp * dl_ref[:, sl]
                ddtl[:, sl] = dxd * xv
            dsb = dscores.astype(BF16)
            dx_ref[:, D_SSD + g * STATE:D_SSD + (g + 1) * STATE] = dbg + _dot_tn(dsb, cb)
            dx_ref[:, D_SSD + (GROUPS + g) * STATE:D_SSD + (GROUPS + g + 1) * STATE] = dcg + _dot(dsb, bb)

        sel = sel_ref[...]
        dcs_h = dcs_h + _dot_exact_right(dcsl[...], sel) - dcst[...].T
        ddt = _dot_exact_right(ddtl[...], sel)
        upper = lax.broadcasted_iota(jnp.int32, (q, q), 1) >= lax.broadcasted_iota(jnp.int32, (q, q), 0)
        da = _dot_exact_left(upper.astype(BF16), dcs_h)
        ddt = ddt + da * a_head
        da_acc[...] += jnp.sum(da * dt, axis=0, keepdims=True)
        ddtr = ddt * _sig(dtr_ref[...] + bias_ref[...])
        ddtr_ref[...] = ddtr
        dbias_acc[...] += jnp.sum(ddtr, axis=0, keepdims=True)

        @pl.when(jnp.logical_and(b == nb - 1, c == nc - 1))
        def _():
            small_ref[...] = jnp.zeros_like(small_ref)
            small_ref[0:1, :] = dnw_acc[...]
            small_ref[1:2, 0:LANES] = _dot_exact_right(jnp.broadcast_to(dd_acc[...], (8, D_SSD)), sel)[0:1, :]
            small_ref[2:3, 0:LANES] = dbias_acc[...]
            small_ref[3:4, 0:LANES] = da_acc[...] * a_head

    rowblk = lambda b, c: b * nc + (nc - 1 - c)
    blk = lambda w, col: pl.BlockSpec((q, w), lambda b, c: (rowblk(b, c), col))
    vec = lambda w: pl.BlockSpec((1, w), lambda b, c: (0, 0))
    return pl.pallas_call(
        body,
        grid=(nb, nc),
        in_specs=[DEP_SPEC, blk(D_SSD, 1), blk(D_SSD, Z_COL), blk(D_SSD, 0), blk(D_SSD, 0), blk(GROUPS * STATE, 2),
                  blk(GROUPS * STATE, 3), blk(LANES, 0),
                  pl.BlockSpec((1, PAIRS, STATE, LANES), lambda b, c: (rowblk(b, c), 0, 0, 0)),
                  vec(LANES), vec(LANES), vec(D_SSD), vec(D_SSD), pl.BlockSpec((D_SSD, LANES), lambda b, c: (0, 0))],
        out_specs=[blk(D_SSD, Z_COL), blk(D_XBC, 0), blk(LANES, 0), pl.BlockSpec((8, D_SSD), lambda b, c: (0, 0))],
        out_shape=[jax.ShapeDtypeStruct(dproj.shape, BF16), jax.ShapeDtypeStruct((t, D_XBC), F32),
                   jax.ShapeDtypeStruct((t, LANES), F32), jax.ShapeDtypeStruct((8, D_SSD), F32)],
        input_output_aliases={0: 0},
        scratch_shapes=[pltpu.VMEM((PAIRS, STATE, LANES), F32), pltpu.VMEM((q, D_SSD), F32),
                        pltpu.VMEM((q, D_SSD), F32), pltpu.VMEM((q, D_SSD), F32), pltpu.VMEM((LANES, q), F32),
                        pltpu.VMEM((1, D_SSD), F32), pltpu.VMEM((1, D_SSD), F32), pltpu.VMEM((1, LANES), F32),
                        pltpu.VMEM((1, LANES), F32)],
        compiler_params=_cparams(("arbitrary", "arbitrary")),
        name="ssd_bwd",
    )(dproj, duy, proj, ys, xs_all, xs_all, xs_all, dtr, states, dt_bias, a_log, d_lanes, norm_w, head_of_lane)


HBM_SPEC = pl.BlockSpec(memory_space=pltpu.HBM)
MESH_ID = pl.DeviceIdType.MESH


def _coords():
    return lax.axis_index("x"), lax.axis_index("y"), lax.axis_index("c")


def _chip_peer(xi, yi, ci, d):
    return (jnp.bitwise_xor(xi, d >> 1), jnp.bitwise_xor(yi, d & 1), ci)


def _remote(src, dst, send_sem, recv_sem, peer):
    return pltpu.make_async_remote_copy(src_ref=src, dst_ref=dst, send_sem=send_sem, recv_sem=recv_sem,
                                        device_id=peer, device_id_type=MESH_ID)


SEM_SPEC = pl.BlockSpec(memory_space=pltpu.SEMAPHORE)
ANY_SPEC = pl.BlockSpec(memory_space=pl.ANY)
EFFECT = pltpu.SideEffectType.DATAFLOW_SIDE_EFFECTING
COPIES = 3


def _half(ref, axis, which, lead=0):
    size = ref.shape[lead + axis] // 2
    part = pl.ds(which * size, size)
    idx = (slice(None),) * lead + ((part, slice(None)) if axis == 0 else (slice(None), part))
    return ref.at[idx]


def _halved_shape(shape, axis):
    lead = len(shape) - 2
    return tuple(d // 2 if i == lead + axis else d for i, d in enumerate(shape))


def _gather_plan(axis):
    def plan(xi, yi, ci, src, land):
        me = 2 * xi + yi
        out = []
        for d in (1, 2, 3):
            there = jnp.bitwise_xor(me, d)
            if axis is None:
                out.append((src, land.at[me], _chip_peer(xi, yi, ci, d), land.at[there]))
            else:
                out.append((_half(src, axis, ci), _half(land.at[me], axis, ci), _chip_peer(xi, yi, ci, d),
                            _half(land.at[there], axis, ci)))
        return out
    return plan


def _owners_plan(xi, yi, ci, src, land):
    me = 2 * xi + yi
    return [(src.at[jnp.bitwise_xor(me, d)], land.at[d - 1], _chip_peer(xi, yi, ci, d), land.at[d - 1])
            for d in (1, 2, 3)]


def _split_start(srcs, lands, plans, *, name, dep=None):
    n = len(srcs)
    deps = [] if dep is None else [dep]

    def body(*refs):
        src_refs, land_refs = refs[:n], refs[n:2 * n]
        outs = refs[2 * n + len(deps):]
        ssems, rsems = outs[:n], outs[n:2 * n]
        token = refs[-1]
        xi, yi, ci = _coords()
        for t in range(n):
            for k, (src, dst, peer, _) in enumerate(plans[t](xi, yi, ci, src_refs[t], land_refs[t])):
                _remote(src, dst, ssems[t].at[k], rsems[t].at[k], peer).start()
        token[...] = jnp.zeros_like(token)

    bufs = list(srcs) + list(lands)
    outs = pl.pallas_call(
        body,
        name=name,
        in_specs=[HBM_SPEC] * (2 * n) + [ANY_SPEC] * len(deps),
        out_specs=[SEM_SPEC] * (2 * n) + [HBM_SPEC] * (2 * n) + [pl.BlockSpec(memory_space=pltpu.VMEM)],
        out_shape=[pltpu.SemaphoreType.DMA((COPIES,))] * (2 * n) + [pltpu.HBM(a.shape, a.dtype) for a in bufs]
        + [jax.ShapeDtypeStruct((8, LANES), F32)],
        input_output_aliases={i: 2 * n + i for i in range(2 * n)},
        compiler_params=pltpu.CompilerParams(has_side_effects=EFFECT),
    )(*[pltpu.with_memory_space_constraint(a, pltpu.HBM) for a in bufs], *deps)
    return outs[:n], outs[n:2 * n], outs[2 * n:3 * n], outs[3 * n:4 * n], outs[-1]


def _split_wait(ssems, rsems, srcs, lands, plans, after, *, name):
    n = len(srcs)

    def body(*refs):
        src_refs, land_refs = refs[:n], refs[n:2 * n]
        ss, rs = refs[2 * n:3 * n], refs[3 * n:4 * n]
        xi, yi, ci = _coords()
        for t in range(n):
            for k, (src, _, peer, landed) in enumerate(plans[t](xi, yi, ci, src_refs[t], land_refs[t])):
                cp = _remote(src, landed, ss[t].at[k], rs[t].at[k], peer)
                cp.wait_send()
                cp.wait_recv()

    bufs = list(srcs) + list(lands)
    outs = pl.pallas_call(
        body,
        name=name,
        in_specs=[HBM_SPEC] * (2 * n) + [SEM_SPEC] * (2 * n) + [ANY_SPEC],
        out_specs=[HBM_SPEC] * (2 * n),
        out_shape=[pltpu.HBM(a.shape, a.dtype) for a in bufs],
        input_output_aliases={i: i for i in range(2 * n)},
        compiler_params=pltpu.CompilerParams(has_side_effects=EFFECT),
    )(*bufs, *ssems, *rsems, after)
    return outs[:n], outs[n:]


def _forward_halves(lands, axes, *, name):
    n = len(lands)

    def body(*refs):
        ins, outs = refs[:n], refs[n:2 * n]
        send_sems, recv_sems = refs[2 * n:]
        xi, yi, ci = _coords()
        me = 2 * xi + yi
        sibling = (xi, yi, 1 - ci)
        cps = []
        for t in range(n):
            for d in (1, 2, 3):
                slot = jnp.bitwise_xor(me, d)
                k = COPIES * t + d - 1
                cp = _remote(_half(ins[t].at[slot], axes[t], ci), _half(outs[t].at[slot], axes[t], ci),
                             send_sems.at[k], recv_sems.at[k], sibling)
                cp.start()
                cps.append(cp)
        for t in range(n):
            for d in (1, 2, 3):
                got = _half(outs[t].at[jnp.bitwise_xor(me, d)], axes[t], 1 - ci)
                k = COPIES * t + d - 1
                _remote(got, got, send_sems.at[k], recv_sems.at[k], sibling).wait_recv()
        for cp in cps:
            cp.wait_send()

    return pl.pallas_call(
        body,
        name=name,
        in_specs=[HBM_SPEC] * n,
        out_specs=[HBM_SPEC] * n,
        out_shape=[jax.ShapeDtypeStruct(a.shape, a.dtype) for a in lands],
        input_output_aliases={i: i for i in range(n)},
        scratch_shapes=[pltpu.SemaphoreType.DMA((COPIES * n,)), pltpu.SemaphoreType.DMA((COPIES * n,))],
    )(*lands)


def _swap_other_halves(gs, axes, *, name):
    n = len(gs)

    def body(*refs):
        ins, lands = refs[:n], refs[n:2 * n]
        send_sems, recv_sems = refs[2 * n:]
        xi, yi, ci = _coords()
        sibling = (xi, yi, 1 - ci)
        cps = []
        for t in range(n):
            cp = _remote(_half(ins[t], axes[t], 1 - ci, lead=1), lands[t], send_sems.at[t], recv_sems.at[t], sibling)
            cp.start()
            cps.append(cp)
        for cp in cps:
            cp.wait_recv()
        for cp in cps:
            cp.wait_send()

    return pl.pallas_call(
        body,
        in_specs=[HBM_SPEC] * n,
        out_specs=[HBM_SPEC] * n,
        out_shape=[jax.ShapeDtypeStruct(_halved_shape(g.shape, ax), g.dtype) for g, ax in zip(gs, axes)],
        scratch_shapes=[pltpu.SemaphoreType.DMA((n,)), pltpu.SemaphoreType.DMA((n,))],
        name=name,
    )(*gs)


def _row_tile(rows, cap=512, mult=16):
    best = mult
    for cand in range(mult, min(rows, cap) + 1, mult):
        if rows % cand == 0:
            best = cand
    assert rows % best == 0, rows
    return best


COL_TILE = 256


def _half_tiles(hr, hc, axis, cap=512, mult=16):
    if axis == 0:
        tr = _row_tile(hr, cap, mult)
        n = hr // tr
        return (tr, hc), n, lambda half, i: (half * n + i, 0)
    n = hc // COL_TILE
    return (hr, COL_TILE), n, lambda half, i: (0, half * n + i)


def _add_core_halves(g, land, where, axis):
    nslot, hr, hc = land.shape
    bshape, nr, idx = _half_tiles(hr, hc, axis)

    def body(where_ref, g_ref, l_ref, f_ref, b_ref):
        s = g_ref[...] + l_ref[...].astype(F32)
        b_ref[...] = s.astype(BF16)

        @pl.when(pl.program_id(1) == where_ref[1])
        def _():
            f_ref[...] = s

    blk = pl.BlockSpec((None,) + bshape, lambda i, s, w: (s,) + idx(0, i))
    mine = pl.BlockSpec((None,) + bshape, lambda i, s, w: (s,) + idx(w[0], i))
    return pl.pallas_call(
        body,
        grid_spec=pltpu.PrefetchScalarGridSpec(
            num_scalar_prefetch=1,
            grid=(nr, nslot),
            in_specs=[mine, blk],
            out_specs=[pl.BlockSpec(bshape, lambda i, s, w: idx(0, i)), blk],
        ),
        out_shape=[jax.ShapeDtypeStruct((hr, hc), F32), jax.ShapeDtypeStruct(land.shape, BF16)],
        compiler_params=_cparams(("parallel", "arbitrary")),
        name="add_core_halves",
    )(where, g, land)


def _add_chip_sums(pf, land, where, axis):
    hr, cols = pf.shape
    bshape, nr, idx = _half_tiles(hr, cols, axis)

    def body(where_ref, p_ref, l_ref, o_ref):
        acc = p_ref[...]
        for d in range(3):
            acc = acc + l_ref[d].astype(F32)
        o_ref[...] = acc

    return pl.pallas_call(
        body,
        grid_spec=pltpu.PrefetchScalarGridSpec(
            num_scalar_prefetch=1,
            grid=(nr,),
            in_specs=[pl.BlockSpec(bshape, lambda i, w: idx(0, i)),
                      pl.BlockSpec((3,) + bshape, lambda i, w: (0,) + idx(0, i))],
            out_specs=pl.BlockSpec(bshape, lambda i, w: idx(0, i)),
        ),
        out_shape=jax.ShapeDtypeStruct((hr, cols), F32),
        compiler_params=_cparams(("parallel",)),
        name="add_chip_sums",
    )(where, pf, land)


def _swap_reduced_halves(rs):
    n = len(rs)

    def body(*refs):
        ins, outs = refs[:n], refs[n:2 * n]
        send_sems, recv_sems = refs[2 * n:]
        xi, yi, ci = _coords()
        sibling = (xi, yi, 1 - ci)
        cps = [_remote(ins[t], outs[t], send_sems.at[t], recv_sems.at[t], sibling) for t in range(n)]
        for cp in cps:
            cp.start()
        for cp in cps:
            cp.wait_recv()
        for cp in cps:
            cp.wait_send()

    return pl.pallas_call(
        body,
        in_specs=[HBM_SPEC] * n,
        out_specs=[HBM_SPEC] * n,
        out_shape=[jax.ShapeDtypeStruct(r.shape, r.dtype) for r in rs],
        scratch_shapes=[pltpu.SemaphoreType.DMA((n,)), pltpu.SemaphoreType.DMA((n,))],
        name="swap_reduced_halves",
    )(*rs)


N_DEV = 8


def _all_reduce_small(part):
    r, w = part.shape

    def body(p_ref, o_ref, gath, send_sems, recv_sems):
        xi, yi, ci = _coords()
        me = 4 * xi + 2 * yi + ci
        gath[me] = p_ref[...]
        cps = []
        for d in range(1, N_DEV):
            peer = (jnp.bitwise_xor(xi, d >> 2), jnp.bitwise_xor(yi, (d >> 1) & 1), jnp.bitwise_xor(ci, d & 1))
            cp = _remote(p_ref, gath.at[me], send_sems.at[d - 1], recv_sems.at[d - 1], peer)
            cp.start()
            cps.append(cp)
        for d in range(1, N_DEV):
            src = gath.at[jnp.bitwise_xor(me, d)]
            _remote(src, src, send_sems.at[d - 1], recv_sems.at[d - 1], (xi, yi, ci)).wait_recv()
        acc = gath[0]
        for k in range(1, N_DEV):
            acc = acc + gath[k]
        o_ref[...] = acc
        for cp in cps:
            cp.wait_send()

    vm = pl.BlockSpec(memory_space=pltpu.VMEM)
    return pl.pallas_call(
        body,
        in_specs=[vm],
        out_specs=vm,
        out_shape=jax.ShapeDtypeStruct((r, w), F32),
        scratch_shapes=[pltpu.VMEM((N_DEV, r, w), F32), pltpu.SemaphoreType.DMA((N_DEV - 1,)),
                        pltpu.SemaphoreType.DMA((N_DEV - 1,))],
        name="all_reduce_small",
    )(part)


def _adamw_math(wv, gv, mv, vv):
    mn = ADAM_B1 * mv + (1.0 - ADAM_B1) * gv
    vn = ADAM_B2 * vv + (1.0 - ADAM_B2) * (gv * gv)
    m_hat = mn / (1.0 - ADAM_B1 ** ADAM_STEP)
    v_hat = vn / (1.0 - ADAM_B2 ** ADAM_STEP)
    return -ADAM_LR * (m_hat / (jnp.sqrt(v_hat) + ADAM_EPS) + ADAM_WD * wv), mn, vn


def _adamw_halves(w, g_mine, g_other, m, v, where, axis, *, name):
    rows, cols = w.shape
    hr, hc = g_mine.shape
    bshape, nr, idx = _half_tiles(hr, hc, axis, cap=256, mult=8)

    def body(where_ref, w_ref, gm_ref, go_ref, m_ref, v_ref, g_ref, d_ref, nm_ref, nv_ref):
        is_mine = pl.program_id(0) == where_ref[0]
        gv = jnp.where(is_mine, gm_ref[...], go_ref[...])
        g_ref[...] = gv
        d_ref[...], nm_ref[...], nv_ref[...] = _adamw_math(w_ref[...], gv, m_ref[...], v_ref[...])

    def parked(half, i, holder):
        return idx(0, jnp.where(half == holder, i, jnp.where(half < holder, 0, nr - 1)))

    blk = pl.BlockSpec(bshape, lambda hf, i, wh: idx(hf, i))
    o = jax.ShapeDtypeStruct((rows, cols), F32)
    return pl.pallas_call(
        body,
        grid_spec=pltpu.PrefetchScalarGridSpec(
            num_scalar_prefetch=1,
            grid=(2, nr),
            in_specs=[blk, pl.BlockSpec(bshape, lambda hf, i, wh: parked(hf, i, wh[0])),
                      pl.BlockSpec(bshape, lambda hf, i, wh: parked(hf, i, 1 - wh[0])), blk, blk],
            out_specs=[blk] * 4,
        ),
        out_shape=[o, o, o, o],
        compiler_params=_cparams(("arbitrary", "arbitrary")),
        name=name,
    )(where, w, g_mine, g_other, m, v)


def _adamw(w, g, m, v, *, name):
    rows, cols = w.shape
    tr = _row_tile(rows, cap=256, mult=8)

    def body(w_ref, g_ref, m_ref, v_ref, d_ref, nm_ref, nv_ref):
        d_ref[...], nm_ref[...], nv_ref[...] = _adamw_math(w_ref[...], g_ref[...], m_ref[...], v_ref[...])

    blk = pl.BlockSpec((tr, cols), lambda i: (i, 0))
    o = jax.ShapeDtypeStruct((rows, cols), F32)
    return pl.pallas_call(
        body,
        grid=(rows // tr,),
        in_specs=[blk] * 4,
        out_specs=[blk] * 3,
        out_shape=[o, o, o],
        compiler_params=_cparams(("parallel",)),
        name=name,
    )(w, g, m, v)


def _pack(arrs):
    flat = jnp.concatenate([a.reshape(-1) for a in arrs])
    pad = (-flat.shape[0]) % (8 * LANES)
    return jnp.pad(flat, (0, pad)).reshape(-1, LANES)


def _unpack(packed, shapes):
    flat = packed.reshape(-1)
    out, off = [], 0
    for s in shapes:
        n = 1
        for dim in s:
            n *= dim
        out.append(flat[off:off + n].reshape(s))
        off += n
    return out


def _pad_rows(a, rows):
    return jnp.pad(a, ((0, rows - a.shape[0]), (0, 0)))


def _pad_lanes(a, lanes=LANES):
    return jnp.pad(a, ((0, 0), (0, lanes - a.shape[1])))


def _local_grads(x2d, tgt2d, prm, get_w, on_grads, *, nb, seq, dep=None):
    g_pre, g_post, g_fpre, g_fpost = prm["norm_mix_pre"], prm["norm_mix_post"], prm["norm_ffn_pre"], prm["norm_ffn_post"]
    dt_bias, a_log = _pad_lanes(prm["ssd_dt_bias"]), _pad_lanes(prm["ssd_a_log"])
    d_lanes = jnp.repeat(prm["ssd_d"], HEAD_DIM, axis=1)

    h = _rms_fwd(x2d, g_pre, dep=dep, out_dtype=BF16, name="rms_mix_pre")
    t, d = x2d.shape
    w_in_t, w_dt_t, cw, sw = get_w("in", h)
    proj = _matmul([(h, w_in_t)], mode="nt", out_dtype=F32, tm=1024, tn=1024, tk=2048, name="mm_proj",
                   extent=(t, D_MAIN, d))
    dtr = _matmul([(h, w_dt_t)], mode="nt", out_dtype=F32, tm=1024, tn=128, tk=2048, name="mm_dt")
    u1, uy = _conv_branch_fwd(proj, cw, prm["conv_dw_b"], prm["conv_ln_g"], prm["conv_ln_b"], nb=nb, seq=seq)
    xs_all = _ssd_pre_fwd(proj, sw, prm["ssd_conv_b"], nb=nb, seq=seq)
    uy, ys, states = _ssd_fwd(uy, xs_all, proj, dtr, dt_bias, a_log, d_lanes, prm["ssd_norm_w"], nb=nb, seq=seq)
    w_out = get_w("out", uy)
    mix = _matmul([(uy, w_out)], mode="nn", out_dtype=F32, tm=1024, tn=1024, tk=2048, name="mm_mix")
    x1, h2 = _rms_post_pre(mix, x2d, g_post, g_fpre)
    w_gate, w_up = get_w("up", h2)
    gt, up, act = _ffn_up(h2, w_gate, w_up, tm=1024, tn=512)
    w_down = get_w("down", act)
    dx2, df, loss, dg_fpost = _down_loss(act, w_down, x1, tgt2d, g_fpost, tk=1408)

    dgt, dup = _ffn_bwd_act(df, w_down, gt, up, tm=1024, tn=512)
    dw_down = _matmul([(act, df)], mode="tn", out_dtype=F32, tm=1408, tn=1024, tk=2048, name="mm_dw_down",
                      also_bf16=True)
    dw_gate = _matmul([(h2, dgt)], mode="tn", out_dtype=F32, tm=1024, tn=1408, tk=2048, name="mm_dw_gate",
                      slot_out=True, also_bf16=True)
    dw_up = _matmul([(h2, dup)], mode="tn", out_dtype=F32, tm=1024, tn=1408, tk=2048, name="mm_dw_up",
                    slot_out=True, also_bf16=True)
    dep = on_grads("ffn", (dw_down, dw_gate, dw_up))
    dh2 = _matmul([(dgt, w_gate), (dup, w_up)], mode="nt", out_dtype=F32, tm=1024, tn=1024, tk=1408, name="mm_dh2",
                  dep=dep)
    dx1, dmix, dg_fpre, dg_post = _rms_bwd_pre_post(dh2, x1, g_fpre, dx2, mix, g_post)
    dw_out = _matmul([(uy, dmix)], mode="tn", out_dtype=F32, tm=1024, tn=1024, tk=2048, name="mm_dw_out",
                     also_bf16=True)
    dep = on_grads("out", (dw_out,))
    duy = _matmul([(dmix, w_out)], mode="nt", out_dtype=F32, tm=1024, tn=1024, tk=2048, name="mm_duy", dep=dep)
    dproj, dcw, dcb, dlg, dlb = _conv_branch_bwd(duy, u1, proj, cw, prm["conv_ln_g"], prm["conv_ln_b"], nb=nb, seq=seq)
    dproj, dxs, ddtr, ssd_small = _ssd_bwd(dproj, duy, proj, ys, xs_all, dtr, states, dt_bias, a_log, d_lanes,
                                           prm["ssd_norm_w"], nb=nb, seq=seq)
    dproj, dsw, dsb = _ssd_pre_bwd(dproj, dxs, proj, sw, prm["ssd_conv_b"], nb=nb, seq=seq)
    ddtr_b = ddtr.astype(BF16)
    dw_in_t = _matmul([(dproj, h)], mode="tn", out_dtype=F32, tm=1024, tn=1024, tk=2048, name="mm_dw_main",
                      extent=(D_MAIN, d, t), out_rows=D_IN)
    dw_in_t = _dw_dt_rows(dw_in_t, ddtr_b, h)
    dep = on_grads("in", ((dw_in_t, dw_in_t),))
    dx, dg_pre = _dh_dx(dproj, w_in_t, ddtr_b, w_dt_t, x2d, dx1, g_pre, tk=1280, dep=g_pre if dep is None else dep)

    grads = {
        "norm_mix_pre": dg_pre,
        "w_in": dw_in_t,
        "conv_dw_w": dcw[:CONV_K], "conv_dw_b": dcb, "conv_ln_g": dlg, "conv_ln_b": dlb,
        "ssd_conv_w": dsw[:SSD_CONV_K], "ssd_conv_b": dsb,
        "ssd_dt_bias": ssd_small[2:3, :HEADS], "ssd_a_log": ssd_small[3:4, :HEADS], "ssd_d": ssd_small[1:2, :HEADS],
        "ssd_norm_w": ssd_small[0:1],
        "w_out": dw_out[0],
        "norm_mix_post": dg_post, "norm_ffn_pre": dg_fpre,
        "w_gate": dw_gate[0], "w_up": dw_up[0],
        "w_down": dw_down[0], "norm_ffn_post": dg_fpost,
    }
    return loss, dx, grads


BIG = ("w_in", "w_out", "w_gate", "w_up", "w_down")
HALF_AXIS = {"w_in": 1, "w_out": 0, "w_gate": 0, "w_up": 0, "w_down": 0}
GATHER_STAGES = {"in": ("w_in", "conv_dw_w", "ssd_conv_w"), "out": ("w_out",), "up": ("w_gate", "w_up"),
                 "down": ("w_down",)}
GATHER_ORDER = tuple(n for st in ("in", "out", "up", "down") for n in GATHER_STAGES[st])
SMALL = ("norm_mix_pre", "conv_dw_w", "conv_dw_b", "conv_ln_g", "conv_ln_b", "ssd_conv_w", "ssd_conv_b", "ssd_dt_bias",
         "ssd_a_log", "ssd_d", "ssd_norm_w", "norm_mix_post", "norm_ffn_pre", "norm_ffn_post")
WEIGHTS = ("norm_mix_pre", "w_in", "conv_dw_w", "conv_dw_b", "conv_ln_g", "conv_ln_b", "ssd_conv_w", "ssd_conv_b",
           "ssd_dt_bias", "ssd_a_log", "ssd_d", "ssd_norm_w", "w_out", "norm_mix_post", "norm_ffn_pre", "w_gate", "w_up",
           "w_down", "norm_ffn_post")


def _cols_from_slots(a):
    n, rows, w = a.shape
    return a.transpose(1, 0, 2).reshape(rows, n * w)


def kernel(x, norm_mix_pre, w_in, conv_dw_w, conv_dw_b, conv_ln_g, conv_ln_b, ssd_conv_w, ssd_conv_b, ssd_dt_bias, ssd_a_log, ssd_d, ssd_norm_w, w_out, norm_mix_post, norm_ffn_pre, w_gate, w_up, w_down, norm_ffn_post, loss_target, m_norm_mix_pre, m_w_in, m_conv_dw_w, m_conv_dw_b, m_conv_ln_g, m_conv_ln_b, m_ssd_conv_w, m_ssd_conv_b, m_ssd_dt_bias, m_ssd_a_log, m_ssd_d, m_ssd_norm_w, m_w_out, m_norm_mix_post, m_norm_ffn_pre, m_w_gate, m_w_up, m_w_down, m_norm_ffn_post, v_norm_mix_pre, v_w_in, v_conv_dw_w, v_conv_dw_b, v_conv_ln_g, v_conv_ln_b, v_ssd_conv_w, v_ssd_conv_b, v_ssd_dt_bias, v_ssd_a_log, v_ssd_d, v_ssd_norm_w, v_w_out, v_norm_mix_post, v_norm_ffn_pre, v_w_gate, v_w_up, v_w_down, v_norm_ffn_post):
    args = dict(locals())
    two_d = lambda n, a: jnp.swapaxes(a, 1, 2)[0] if n == "w_in" else a.reshape(a.shape[-2:])
    wts = {n: two_d(n, args[n]) for n in WEIGHTS}
    ms = {n: two_d(n, args["m_" + n]) for n in WEIGHTS}
    vs = {n: two_d(n, args["v_" + n]) for n in WEIGHTS}
    nb, seq, d = x.shape
    t = nb * seq
    xi, yi, ci = _coords()
    chip = 2 * xi + yi
    where = jnp.stack([ci, chip]).astype(jnp.int32)

    shards = {n: wts[n].astype(BF16) for n in BIG}
    shards.update(conv_dw_w=_pad_rows(wts["conv_dw_w"], 32), ssd_conv_w=_pad_rows(wts["ssd_conv_w"], 8))
    plans = [_gather_plan(HALF_AXIS.get(n)) for n in GATHER_ORDER]
    n_first = len(GATHER_STAGES["in"])

    def start(part, name, dep=None):
        src = [shards[n] for n in GATHER_ORDER[part]]
        return _split_start(src, [lax.empty((N_CHIPS,) + s.shape, s.dtype) for s in src], plans[part], name=name,
                            dep=dep)

    head = start(slice(0, n_first), "gather_start_in")
    rest = start(slice(n_first, None), "gather_start_rest", dep=head[4])
    ssems, rsems, srcs, lands = (list(h) + list(r) for h, r in zip(head[:4], rest[:4]))
    token = rest[4]

    def get_w(stage, after):
        names = GATHER_STAGES[stage]
        pick = lambda seq_: [seq_[GATHER_ORDER.index(n)] for n in names]
        own, got = _split_wait(pick(ssems), pick(rsems), pick(srcs), pick(lands), pick(plans), after,
                               name="gather_wait_" + stage)
        got, own = dict(zip(names, got)), dict(zip(names, own))
        big = [n for n in names if n in BIG]
        got.update(zip(big, _forward_halves([got[n] for n in big], [HALF_AXIS[n] for n in big],
                                            name="gather_forward_" + stage)))
        full = {n: lax.dynamic_update_slice(got[n], own[n][None], (chip, 0, 0)) for n in got}
        if stage == "in":
            w_in_t = full["w_in"].reshape(D_IN, D_MODEL)
            return (w_in_t, _pad_rows(w_in_t[D_MAIN:], LANES), _cols_from_slots(full["conv_dw_w"]),
                    _cols_from_slots(full["ssd_conv_w"]))
        if stage == "out":
            return full["w_out"].reshape(D_MODEL, D_MODEL)
        if stage == "up":
            return _cols_from_slots(full["w_gate"]), _cols_from_slots(full["w_up"])
        return full["w_down"].reshape(D_FF, D_MODEL)

    reduce_groups = {"ffn": ("w_down", "w_gate", "w_up"), "out": ("w_out",), "in": ("w_in",)}
    in_flight = {}

    def on_grads(stage, gs):
        names = reduce_groups[stage]
        axes = [HALF_AXIS[n] for n in names]
        slot = lambda g: g if g.ndim == 3 else g.reshape((N_CHIPS, g.shape[0] // N_CHIPS, g.shape[1]))
        kept = _swap_other_halves([slot(b16) for _, b16 in gs], axes, name="swap_other_halves_" + stage)
        sums = [_add_core_halves(slot(f32), l, where, ax) for (f32, _), l, ax in zip(gs, kept, axes)]
        ps = [s[1] for s in sums]
        ssem, rsem, ps, recv, started = _split_start(
            ps, [lax.empty((COPIES,) + p.shape[1:], p.dtype) for p in ps], [_owners_plan] * len(ps),
            name="owners_start_" + stage)
        in_flight[stage] = (ssem, rsem, ps, recv, [s[0] for s in sums])
        return started

    prm = {n: wts[n] for n in SMALL}
    loss, dx, grads = _local_grads(x.reshape(t, d), loss_target.reshape(t, d), prm, get_w, on_grads,
                                   nb=nb, seq=seq, dep=token)
    loss = lax.psum(loss[0, 0], MESH_AXES)

    halves = {}
    for stage, names in reduce_groups.items():
        ssem, rsem, ps, recv, own_sums = in_flight[stage]
        _, recv = _split_wait(ssem, rsem, ps, recv, [_owners_plan] * len(ps), dx, name="owners_wait_" + stage)
        for n, f32_sum, r in zip(names, own_sums, recv):
            halves[n] = _add_chip_sums(f32_sum, r, where, HALF_AXIS[n])
    halves = [halves[n] for n in BIG]
    other_halves = _swap_reduced_halves(halves)

    small_shapes = [grads[n].shape for n in SMALL]
    small_sum = _unpack(_all_reduce_small(_pack([grads[n] for n in SMALL])), small_shapes)
    small_grads = dict(zip(SMALL, small_sum))
    cwid, swid = D_CONV // N_CHIPS, D_XBC // N_CHIPS
    small_grads["conv_dw_w"] = lax.dynamic_slice(small_grads["conv_dw_w"], (0, chip * cwid), (CONV_K, cwid))
    small_grads["ssd_conv_w"] = lax.dynamic_slice(small_grads["ssd_conv_w"], (0, chip * swid), (SSD_CONV_K, swid))

    out_g, out_d, out_m, out_v = {}, {}, {}, {}
    for n, mine, other in zip(BIG, halves, other_halves):
        out_g[n], out_d[n], out_m[n], out_v[n] = _adamw_halves(wts[n], mine, other, ms[n], vs[n], where,
                                                                 HALF_AXIS[n], name="adamw_" + n)
    shard_shapes = [wts[n].shape for n in SMALL]
    pd, pm, pv = _adamw(_pack([wts[n] for n in SMALL]), _pack([small_grads[n] for n in SMALL]),
                        _pack([ms[n] for n in SMALL]), _pack([vs[n] for n in SMALL]), name="adamw_small")
    for n, dd, mm, vv in zip(SMALL, _unpack(pd, shard_shapes), _unpack(pm, shard_shapes), _unpack(pv, shard_shapes)):
        out_g[n], out_d[n], out_m[n], out_v[n] = small_grads[n], dd, mm, vv

    back = lambda n, a: jnp.swapaxes(a[None], 1, 2) if n == "w_in" else a.reshape(args[n].shape)
    outs = [back(n, o[n]) for o in (out_g, out_d, out_m, out_v) for n in WEIGHTS]
    return (loss, dx.reshape(nb, seq, d), *outs)
```

```python
import functools

import jax
import jax.numpy as jnp
from jax import lax
from jax.experimental import pallas as pl
from jax.experimental.pallas import tpu as pltpu

F32 = jnp.float32
BF16 = jnp.bfloat16
EPS = 1e-6

D_MODEL = 2048
D_CONV = 1024
D_SSD = 1024
D_XBC = 2048
HEADS = 16
HEAD_DIM = 64
GROUPS = 4
STATE = 128
CONV_K = 31
SSD_CONV_K = 4
D_FF = 5632
D_MAIN = 2 * D_CONV + D_SSD + D_XBC
D_IN = D_MAIN + HEADS
N_CHIPS = 4
LANES = 128
CHUNK = 128
PAIRS = HEADS // 2

ADAM_LR = 0.001
ADAM_B1 = 0.9
ADAM_B2 = 0.999
ADAM_EPS = 1e-08
ADAM_WD = 0.01
ADAM_STEP = 10

MESH_AXES = ("x", "y", "c")
VMEM_LIMIT = 56 * 1024 * 1024


def _sig(v):
    return 1.0 / (1.0 + jnp.exp(-v))


def _cparams(sem, vmem=VMEM_LIMIT):
    return pltpu.CompilerParams(dimension_semantics=sem, vmem_limit_bytes=vmem)


_DIMS = {"nn": ((1,), (0,)), "nt": ((1,), (1,)), "tn": ((0,), (0,))}


def _matmul(pairs, *, mode, out_dtype, tm, tn, tk, name, slot_out=False, extent=None, out_rows=None, dep=None,
            also_bf16=False):
    a0, b0 = pairs[0]
    if mode == "nn":
        (m, k), n = a0.shape, b0.shape[1]
    elif mode == "nt":
        (m, k), n = a0.shape, b0.shape[0]
    else:
        (k, m), n = a0.shape, b0.shape[1]
    if extent is not None:
        m, n, k = extent
    tm, tn, tk = min(tm, m), min(tn, n), min(tk, k)
    assert m % tm == 0 and n % tn == 0 and k % tk == 0, (name, m, n, k, tm, tn, tk)
    nk = k // tk
    npairs = len(pairs)
    deps = [] if dep is None else [dep]
    dims = (_DIMS[mode], ((), ()))

    use_scratch = nk > 1 and out_dtype != F32

    def body(*refs):
        ins, o_ref = refs[: 2 * npairs], refs[2 * npairs + len(deps)]
        dot = lambda p: lax.dot_general(ins[2 * p][...], ins[2 * p + 1][...], dims, preferred_element_type=F32)
        if nk == 1:
            part = dot(0)
            for p in range(1, npairs):
                part = part + dot(p)
            o_ref[...] = part.astype(out_dtype)
            if also_bf16:
                refs[2 * npairs + len(deps) + 1][...] = part.astype(BF16)
            return
        acc = refs[-1] if use_scratch else o_ref
        kk = pl.program_id(2)

        @pl.when(kk == 0)
        def _():
            acc[...] = jnp.zeros_like(acc)

        for p in range(npairs):
            acc[...] += dot(p)

        if use_scratch:
            @pl.when(kk == nk - 1)
            def _():
                o_ref[...] = acc[...].astype(out_dtype)

        if also_bf16:
            @pl.when(kk == nk - 1)
            def _():
                refs[2 * npairs + len(deps) + 1][...] = acc[...].astype(BF16)

    if mode == "nn":
        a_spec = pl.BlockSpec((tm, tk), lambda i, j, kk: (i, kk))
        b_spec = pl.BlockSpec((tk, tn), lambda i, j, kk: (kk, j))
    elif mode == "nt":
        a_spec = pl.BlockSpec((tm, tk), lambda i, j, kk: (i, kk))
        b_spec = pl.BlockSpec((tn, tk), lambda i, j, kk: (j, kk))
    else:
        a_spec = pl.BlockSpec((tk, tm), lambda i, j, kk: (kk, i))
        b_spec = pl.BlockSpec((tk, tn), lambda i, j, kk: (kk, j))
    if slot_out:
        out_shape = jax.ShapeDtypeStruct((n // tn, m, tn), out_dtype)
        out_spec = pl.BlockSpec((None, tm, tn), lambda i, j, kk: (j, i, 0))
    else:
        out_shape = jax.ShapeDtypeStruct((m if out_rows is None else out_rows, n), out_dtype)
        out_spec = pl.BlockSpec((tm, tn), lambda i, j, kk: (i, j))
    flat = [t for ab in pairs for t in ab]
    if also_bf16:
        out_spec = [out_spec, out_spec]
        out_shape = [out_shape, jax.ShapeDtypeStruct(out_shape.shape, BF16)]
    return pl.pallas_call(
        body,
        grid=(m // tm, n // tn, nk),
        in_specs=[a_spec, b_spec] * npairs + [pl.BlockSpec(memory_space=pl.ANY)] * len(deps),
        out_specs=out_spec,
        out_shape=out_shape,
        scratch_shapes=[pltpu.VMEM((tm, tn), F32)] if use_scratch else [],
        compiler_params=_cparams(("parallel", "parallel", "arbitrary")),
        name=name,
    )(*flat, *deps)


SUB_ROWS = 256


def _ffn_up(h2, wg, wu, *, tm, tn):
    t, k = h2.shape
    n = wg.shape[1]
    tm = min(tm, t)
    assert t % tm == 0 and n % tn == 0, (t, n, tm, tn)

    sub = min(SUB_ROWS, tm)

    def body(h_ref, wg_ref, wu_ref, g_ref, u_ref, a_ref):
        for r in range(tm // sub):
            rows = pl.ds(r * sub, sub)
            hv = h_ref[rows, :]
            g = jnp.dot(hv, wg_ref[...], preferred_element_type=F32)
            u = jnp.dot(hv, wu_ref[...], preferred_element_type=F32)
            g_ref[rows, :] = g.astype(BF16)
            u_ref[rows, :] = u.astype(BF16)
            a_ref[rows, :] = (g * _sig(g) * u).astype(BF16)

    o = jax.ShapeDtypeStruct((t, n), BF16)
    ospec = pl.BlockSpec((tm, tn), lambda i, j: (i, j))
    return pl.pallas_call(
        body,
        grid=(t // tm, n // tn),
        in_specs=[pl.BlockSpec((tm, k), lambda i, j: (i, 0)), pl.BlockSpec((k, tn), lambda i, j: (0, j)),
                  pl.BlockSpec((k, tn), lambda i, j: (0, j))],
        out_specs=[ospec, ospec, ospec],
        out_shape=[o, o, o],
        compiler_params=_cparams(("parallel", "parallel")),
        name="ffn_up",
    )(h2, wg, wu)


def _ffn_bwd_act(df, wd, gt, up, *, tm, tn):
    t, k = df.shape
    n = wd.shape[0]
    tm = min(tm, t)
    assert t % tm == 0 and n % tn == 0, (t, n, tm, tn)

    sub = min(SUB_ROWS, tm)

    def body(df_ref, wd_ref, g_ref, u_ref, dg_ref, du_ref):
        for r in range(tm // sub):
            rows = pl.ds(r * sub, sub)
            da = lax.dot_general(df_ref[rows, :], wd_ref[...], (_DIMS["nt"], ((), ())), preferred_element_type=F32)
            g = g_ref[rows, :].astype(F32)
            u = u_ref[rows, :].astype(F32)
            s = _sig(g)
            dg_ref[rows, :] = (da * u * s * (1.0 + g * (1.0 - s))).astype(BF16)
            du_ref[rows, :] = (da * g * s).astype(BF16)

    o = jax.ShapeDtypeStruct((t, n), BF16)
    blk = pl.BlockSpec((tm, tn), lambda i, j: (i, j))
    return pl.pallas_call(
        body,
        grid=(t // tm, n // tn),
        in_specs=[pl.BlockSpec((tm, k), lambda i, j: (i, 0)), pl.BlockSpec((tn, k), lambda i, j: (j, 0)), blk, blk],
        out_specs=[blk, blk],
        out_shape=[o, o],
        compiler_params=_cparams(("parallel", "parallel")),
        name="ffn_bwd_act",
    )(df, wd, gt, up)


def _dw_dt_rows(dw_in_t, ddtr_b, h, *, tk=1024):
    t, d = h.shape
    tk = min(tk, t)
    nk = t // tk

    def body(buf_ref, d_ref, h_ref, o_ref, acc):
        kk = pl.program_id(0)

        @pl.when(kk == 0)
        def _():
            acc[...] = jnp.zeros_like(acc)

        acc[...] += lax.dot_general(d_ref[...], h_ref[...], (_DIMS["tn"], ((), ())), preferred_element_type=F32)

        @pl.when(kk == nk - 1)
        def _():
            o_ref[...] = acc[0:HEADS, :]

    return pl.pallas_call(
        body,
        grid=(nk,),
        in_specs=[DEP_SPEC, pl.BlockSpec((tk, LANES), lambda kk: (kk, 0)), pl.BlockSpec((tk, d), lambda kk: (kk, 0))],
        out_specs=pl.BlockSpec((HEADS, d), lambda kk: (D_MAIN // HEADS, 0)),
        out_shape=jax.ShapeDtypeStruct(dw_in_t.shape, F32),
        input_output_aliases={0: 0},
        scratch_shapes=[pltpu.VMEM((LANES, d), F32)],
        compiler_params=_cparams(("arbitrary",)),
        name="mm_dw_dt",
    )(dw_in_t, ddtr_b, h)


ROW_TILE = 256


DEP_SPEC = pl.BlockSpec(memory_space=pl.ANY)


def _rms_fwd(xv, g, *, dep=None, out_dtype, name):
    t, d = xv.shape
    deps = [] if dep is None else [dep]

    def body(*refs):
        x_ref, g_ref = refs[0], refs[1]
        o_ref = refs[-1]
        v = x_ref[...]
        r = lax.rsqrt(jnp.mean(v * v, axis=-1, keepdims=True) + EPS)
        o_ref[...] = (v * r * g_ref[...]).astype(out_dtype)

    row = pl.BlockSpec((ROW_TILE, d), lambda i: (i, 0))
    vec = pl.BlockSpec((1, d), lambda i: (0, 0))
    return pl.pallas_call(
        body,
        grid=(t // ROW_TILE,),
        in_specs=[row, vec] + [DEP_SPEC] * len(deps),
        out_specs=row,
        out_shape=jax.ShapeDtypeStruct((t, d), out_dtype),
        compiler_params=_cparams(("parallel",)),
        name=name,
    )(*([xv, g] + deps))


def _rms_bwd_rows(dy, v, gv):
    r = lax.rsqrt(jnp.mean(v * v, axis=-1, keepdims=True) + EPS)
    xh = v * r
    gdy = dy * gv
    dx = r * (gdy - xh * jnp.mean(gdy * xh, axis=-1, keepdims=True))
    return dx, jnp.sum(dy * xh, axis=0, keepdims=True)


FUSED_ROWS = 512


def _matmul_rows_tail(a, w, tail, *, tk, row_ins, vec_ins, row_outs, vec_outs, first=None, dep=None, name):
    t, kdim = a.shape
    d = w.shape[1]
    tm, tk = min(FUSED_ROWS, t), min(tk, kdim)
    nk, nb = kdim // tk, t // tm
    assert t % tm == 0 and kdim % tk == 0
    n_ri, n_vi, n_ro, n_vo = len(row_ins), len(vec_ins), len(row_outs), len(vec_outs)
    n_first = 0 if first is None else 2
    deps = [] if dep is None else [dep]

    def body(*refs):
        a_ref, w_ref = refs[0], refs[1]
        first_refs = refs[2:2 + n_first]
        p = 2 + n_first
        ri = refs[p:p + n_ri]
        vi = refs[p + n_ri:p + n_ri + n_vi]
        p += n_ri + n_vi + len(deps)
        ro = refs[p:p + n_ro]
        vo = refs[p + n_ro:p + n_ro + n_vo]
        acc = refs[-1]
        i, kk = pl.program_id(0), pl.program_id(1)

        @pl.when(jnp.logical_and(i == 0, kk == 0))
        def _():
            for ref in vo:
                ref[...] = jnp.zeros_like(ref)

        @pl.when(kk == 0)
        def _():
            if first is None:
                acc[...] = jnp.zeros_like(acc)
            else:
                acc[...] = jnp.dot(first_refs[0][...], first_refs[1][...], preferred_element_type=F32)

        acc[...] += jnp.dot(a_ref[...], w_ref[...], preferred_element_type=F32)

        @pl.when(kk == nk - 1)
        def _():
            outs, parts = tail(acc[...], [r[...] for r in ri], [v[...] for v in vi])
            for ref, val in zip(ro, outs):
                ref[...] = val.astype(ref.dtype)
            for ref, part in zip(vo, parts):
                ref[...] += part

    row = pl.BlockSpec((tm, d), lambda i, kk: (i, 0))
    const = lambda shape: pl.BlockSpec(shape, lambda i, kk: (0,) * len(shape))
    in_specs = [pl.BlockSpec((tm, tk), lambda i, kk: (i, kk)), pl.BlockSpec((tk, d), lambda i, kk: (kk, 0))]
    if first is not None:
        in_specs += [pl.BlockSpec((tm, first[0].shape[1]), lambda i, kk: (i, 0)), const(first[1].shape)]
    in_specs += [row] * n_ri + [const(v.shape) for v in vec_ins] + [pl.BlockSpec(memory_space=pl.ANY)] * len(deps)
    return pl.pallas_call(
        body,
        grid=(nb, nk),
        in_specs=in_specs,
        out_specs=[row] * n_ro + [const(sh) for sh in vec_outs],
        out_shape=[jax.ShapeDtypeStruct((t, d), dt) for dt in row_outs]
        + [jax.ShapeDtypeStruct(sh, F32) for sh in vec_outs],
        scratch_shapes=[pltpu.VMEM((tm, d), F32)],
        compiler_params=_cparams(("arbitrary", "arbitrary")),
        name=name,
    )(a, w, *([] if first is None else list(first)), *row_ins, *vec_ins, *deps)


def _down_loss(act, w_down, x1, tgt, g, *, tk):
    d = w_down.shape[1]

    def tail(v, rows, vecs):
        x1v, tv = rows
        gv, = vecs
        fh = v * lax.rsqrt(jnp.mean(v * v, axis=-1, keepdims=True) + EPS)
        e = x1v + fh * gv - tv
        dx2 = e * (1.0 / d)
        df, dg = _rms_bwd_rows(dx2, v, gv)
        loss = 0.5 * jnp.sum(jnp.mean(e * e, axis=-1, keepdims=True), axis=0, keepdims=True)
        return (dx2, df), (loss, dg)

    return _matmul_rows_tail(act, w_down, tail, tk=tk, row_ins=[x1, tgt], vec_ins=[g], row_outs=[F32, BF16],
                             vec_outs=[(1, 1), (1, d)], name="mm_down_loss")


def _dh_dx(dproj, w_in_t, ddtr_b, w_dt_t, xv, dx1, g, *, tk, dep):
    def tail(v, rows, vecs):
        xr, dx1r = rows
        dx, dg = _rms_bwd_rows(v, xr, vecs[0])
        return (dx + dx1r,), (dg,)

    return _matmul_rows_tail(dproj, w_in_t, tail, tk=tk, row_ins=[xv, dx1], vec_ins=[g], row_outs=[F32],
                             vec_outs=[(1, xv.shape[1])], first=(ddtr_b, w_dt_t), dep=dep, name="mm_dh_dx")


def _rms_post_pre(mix, xv, g_post, g_pre):
    t, d = mix.shape

    def body(m_ref, x_ref, gp_ref, gf_ref, x1_ref, h2_ref):
        v = m_ref[...]
        x1 = x_ref[...] + v * lax.rsqrt(jnp.mean(v * v, axis=-1, keepdims=True) + EPS) * gp_ref[...]
        x1_ref[...] = x1
        h2_ref[...] = (x1 * lax.rsqrt(jnp.mean(x1 * x1, axis=-1, keepdims=True) + EPS) * gf_ref[...]).astype(BF16)

    row = pl.BlockSpec((ROW_TILE, d), lambda i: (i, 0))
    vec = pl.BlockSpec((1, d), lambda i: (0, 0))
    return pl.pallas_call(
        body,
        grid=(t // ROW_TILE,),
        in_specs=[row, row, vec, vec],
        out_specs=[row, row],
        out_shape=[jax.ShapeDtypeStruct((t, d), F32), jax.ShapeDtypeStruct((t, d), BF16)],
        compiler_params=_cparams(("parallel",)),
        name="rms_mix_post_ffn_pre",
    )(mix, xv, g_post, g_pre)


def _rms_bwd_pre_post(dh2, x1, g_pre, dx2, mix, g_post):
    t, d = x1.shape

    def body(dh_ref, x1_ref, gf_ref, dx2_ref, m_ref, gp_ref, dx1_ref, dmix_ref, dgf_ref, dgp_ref):
        @pl.when(pl.program_id(0) == 0)
        def _():
            dgf_ref[...] = jnp.zeros_like(dgf_ref)
            dgp_ref[...] = jnp.zeros_like(dgp_ref)

        dx, dgf = _rms_bwd_rows(dh_ref[...], x1_ref[...], gf_ref[...])
        dx1 = dx + dx2_ref[...]
        dx1_ref[...] = dx1
        dmix, dgp = _rms_bwd_rows(dx1, m_ref[...], gp_ref[...])
        dmix_ref[...] = dmix.astype(BF16)
        dgf_ref[...] += dgf
        dgp_ref[...] += dgp

    row = pl.BlockSpec((ROW_TILE, d), lambda i: (i, 0))
    vec = pl.BlockSpec((1, d), lambda i: (0, 0))
    return pl.pallas_call(
        body,
        grid=(t // ROW_TILE,),
        in_specs=[row, row, vec, row, row, vec],
        out_specs=[row, row, vec, vec],
        out_shape=[jax.ShapeDtypeStruct((t, d), F32), jax.ShapeDtypeStruct((t, d), BF16),
                   jax.ShapeDtypeStruct((1, d), F32), jax.ShapeDtypeStruct((1, d), F32)],
        compiler_params=_cparams(("arbitrary",)),
        name="rms_ffn_pre_mix_post_bwd",
    )(dh2, x1, g_pre, dx2, mix, g_post)


CONV_ROWS = 256
TAP_ROWS = 64
HALO31 = 32
HALO4 = 8


def _sum8(v):
    return jnp.sum(v.reshape(v.shape[0] // 8, 8, v.shape[1]), axis=0)


SUBLANES = 8
PHASE_SPAN = (CONV_K - 1) // SUBLANES * SUBLANES


def _phase_scratch(ts):
    return pltpu.VMEM((SUBLANES, ts + PHASE_SPAN, LANES), F32)


def _phase_copies(ph, buf, ln, base, ts):
    for s in range(SUBLANES):
        n = ts + (CONV_K - 1 - s) // SUBLANES * SUBLANES
        ph[s, 0:n, :] = buf[pl.ds(base + s, n), ln]


def _tap_rows(ph, off, r0):
    s = off % SUBLANES
    return ph[s, pl.ds(r0 + off - s, TAP_ROWS), :]


def _conv_branch_fwd(proj, cw, cb, lg, lb, *, nb, seq):
    ts, c, halo = CONV_ROWS, D_CONV, HALO31
    ns = seq // ts
    base = halo - CONV_K + 1

    def body(ca_ref, cg_ref, w_ref, b_ref, lg_ref, lb_ref, u1_ref, u_ref, ubuf, uph):
        i = pl.program_id(1)

        @pl.when(i == 0)
        def _():
            ubuf[0:halo, :] = jnp.zeros((halo, c), F32)

        @pl.when(i > 0)
        def _():
            ubuf[0:halo, :] = ubuf[ts:ts + halo, :]

        ubuf[halo:halo + ts, :] = ca_ref[...] * _sig(cg_ref[...])

        def lane_tile(j, carry):
            ln = pl.ds(pl.multiple_of(j * LANES, LANES), LANES)
            _phase_copies(uph, ubuf, ln, base, ts)
            for r in range(ts // TAP_ROWS):
                acc = jnp.broadcast_to(b_ref[:, ln], (TAP_ROWS, LANES))
                for k in range(CONV_K):
                    acc = acc + w_ref[pl.ds(k, 1), ln] * _tap_rows(uph, k, r * TAP_ROWS)
                u1_ref[pl.ds(r * TAP_ROWS, TAP_ROWS), ln] = acc
            return carry

        lax.fori_loop(0, c // LANES, lane_tile, 0)
        v = u1_ref[...]
        mu = jnp.mean(v, axis=-1, keepdims=True)
        dv = v - mu
        xh = dv * lax.rsqrt(jnp.mean(dv * dv, axis=-1, keepdims=True) + EPS)
        u2 = xh * lg_ref[...] + lb_ref[...]
        u_ref[...] = (u2 * _sig(u2)).astype(BF16)

    t = nb * seq
    row = lambda col: pl.BlockSpec((ts, c), lambda b, i: (b * ns + i, col))
    vec = pl.BlockSpec((1, c), lambda b, i: (0, 0))
    return pl.pallas_call(
        body,
        grid=(nb, ns),
        in_specs=[row(0), row(1), pl.BlockSpec((32, c), lambda b, i: (0, 0)), vec, vec, vec],
        out_specs=[row(0), row(0)],
        out_shape=[jax.ShapeDtypeStruct((t, c), F32), jax.ShapeDtypeStruct((t, c + D_SSD), BF16)],
        scratch_shapes=[pltpu.VMEM((halo + ts, c), F32), _phase_scratch(ts)],
        compiler_params=_cparams(("parallel", "arbitrary")),
        name="conv_branch_fwd",
    )(proj, proj, cw, cb, lg, lb)


def _conv_branch_bwd(duy, u1, proj, cw, lg, lb, *, nb, seq):
    ts, c, halo = CONV_ROWS, D_CONV, HALO31
    ns = seq // ts
    base = halo - CONV_K + 1
    hb = ts // halo

    def body(du_ref, u1_ref, ca_ref, cg_ref, cah_ref, cgh_ref, w_ref, lg_ref, lb_ref,
             dcacg_ref, dw_ref, db_ref, dlg_ref, dlb_ref,
             ubuf, dbuf, du0buf, dwacc, dbacc, dlgacc, dlbacc, uph, dph):
        b, i = pl.program_id(0), pl.program_id(1)
        rc = ns - 1 - i

        @pl.when(jnp.logical_and(b == 0, i == 0))
        def _():
            dwacc[...] = jnp.zeros_like(dwacc)
            dbacc[...] = jnp.zeros_like(dbacc)
            dlgacc[...] = jnp.zeros_like(dlgacc)
            dlbacc[...] = jnp.zeros_like(dlbacc)

        @pl.when(i == 0)
        def _():
            dbuf[ts:ts + halo, :] = jnp.zeros((halo, c), F32)

        @pl.when(i > 0)
        def _():
            dbuf[ts:ts + halo, :] = dbuf[0:halo, :]

        v = u1_ref[...]
        mu = jnp.mean(v, axis=-1, keepdims=True)
        dv = v - mu
        rstd = lax.rsqrt(jnp.mean(dv * dv, axis=-1, keepdims=True) + EPS)
        xh = dv * rstd
        lgv = lg_ref[...]
        u2 = xh * lgv + lb_ref[...]
        s2 = _sig(u2)
        du2 = du_ref[...] * (s2 * (1.0 + u2 * (1.0 - s2)))
        dlgacc[...] += jnp.sum(du2 * xh, axis=0, keepdims=True)
        dlbacc[...] += jnp.sum(du2, axis=0, keepdims=True)
        gd = du2 * lgv
        du1 = rstd * (gd - jnp.mean(gd, axis=-1, keepdims=True) - xh * jnp.mean(gd * xh, axis=-1, keepdims=True))
        dbacc[...] += jnp.sum(du1, axis=0, keepdims=True)
        dbuf[0:ts, :] = du1

        @pl.when(rc == 0)
        def _():
            ubuf[0:halo, :] = jnp.zeros((halo, c), F32)

        @pl.when(rc > 0)
        def _():
            ubuf[0:halo, :] = cah_ref[...] * _sig(cgh_ref[...])

        cav = ca_ref[...]
        sg = _sig(cg_ref[...])
        ubuf[halo:halo + ts, :] = cav * sg

        def lane_tile(j, carry):
            ln = pl.ds(pl.multiple_of(j * LANES, LANES), LANES)
            _phase_copies(uph, ubuf, ln, base, ts)
            _phase_copies(dph, dbuf, ln, 0, ts)
            for r in range(ts // TAP_ROWS):
                r0 = r * TAP_ROWS
                d1 = dbuf[pl.ds(r0, TAP_ROWS), ln]
                acc = jnp.zeros((TAP_ROWS, LANES), F32)
                for k in range(CONV_K):
                    acc = acc + w_ref[pl.ds(k, 1), ln] * _tap_rows(dph, CONV_K - 1 - k, r0)
                    dwacc[pl.ds(k * 8, 8), ln] += _sum8(d1 * _tap_rows(uph, k, r0))
                du0buf[pl.ds(r0, TAP_ROWS), ln] = acc
            return carry

        lax.fori_loop(0, c // LANES, lane_tile, 0)
        du0 = du0buf[...]
        dcacg_ref[:, 0:c] = (du0 * sg).astype(BF16)
        dcacg_ref[:, c:2 * c] = (du0 * cav * sg * (1.0 - sg)).astype(BF16)

        @pl.when(jnp.logical_and(b == nb - 1, i == ns - 1))
        def _():
            for k in range(CONV_K):
                dw_ref[pl.ds(k, 1), :] = jnp.sum(dwacc[pl.ds(k * 8, 8), :], axis=0, keepdims=True)
            dw_ref[pl.ds(CONV_K, 1), :] = jnp.zeros((1, c), F32)
            db_ref[...] = dbacc[...]
            dlg_ref[...] = dlgacc[...]
            dlb_ref[...] = dlbacc[...]

    t = nb * seq
    rowblk = lambda b, i: b * ns + (ns - 1 - i)
    row = lambda col: pl.BlockSpec((ts, c), lambda b, i: (rowblk(b, i), col))
    hrow = lambda col: pl.BlockSpec((halo, c), lambda b, i: (jnp.maximum(rowblk(b, i) * hb - 1, 0), col))
    vec = pl.BlockSpec((1, c), lambda b, i: (0, 0))
    wspec = pl.BlockSpec((32, c), lambda b, i: (0, 0))
    return pl.pallas_call(
        body,
        grid=(nb, ns),
        in_specs=[row(0), row(0), row(0), row(1), hrow(0), hrow(1), wspec, vec, vec],
        out_specs=[pl.BlockSpec((ts, 2 * c), lambda b, i: (rowblk(b, i), 0)), wspec, vec, vec, vec],
        out_shape=[jax.ShapeDtypeStruct((t, D_MAIN), BF16), jax.ShapeDtypeStruct((32, c), F32),
                   jax.ShapeDtypeStruct((1, c), F32), jax.ShapeDtypeStruct((1, c), F32), jax.ShapeDtypeStruct((1, c), F32)],
        scratch_shapes=[pltpu.VMEM((halo + ts, c), F32), pltpu.VMEM((ts + halo, c), F32), pltpu.VMEM((ts, c), F32),
                        pltpu.VMEM((CONV_K * 8, c), F32), pltpu.VMEM((1, c), F32), pltpu.VMEM((1, c), F32),
                        pltpu.VMEM((1, c), F32), _phase_scratch(ts), _phase_scratch(ts)],
        compiler_params=_cparams(("arbitrary", "arbitrary")),
        name="conv_branch_bwd",
    )(duy, u1, proj, proj, proj, proj, cw, lg, lb)


XBC_COL0 = (2 * D_CONV + D_SSD) // 1024


def _ssd_pre_fwd(proj, sw, sb, *, nb, seq):
    ts, c, halo = CONV_ROWS, 1024, HALO4
    ns = seq // ts
    base = halo - SSD_CONV_K + 1

    def body(x_ref, w_ref, b_ref, o_ref, xbuf):
        i = pl.program_id(2)

        @pl.when(i == 0)
        def _():
            xbuf[0:halo, :] = jnp.zeros((halo, c), F32)

        @pl.when(i > 0)
        def _():
            xbuf[0:halo, :] = xbuf[ts:ts + halo, :]

        xbuf[halo:halo + ts, :] = x_ref[...]

        def lane_tile(j, carry):
            ln = pl.ds(pl.multiple_of(j * LANES, LANES), LANES)
            for r in range(ts // TAP_ROWS):
                acc = jnp.broadcast_to(b_ref[:, ln], (TAP_ROWS, LANES))
                for k in range(SSD_CONV_K):
                    acc = acc + w_ref[pl.ds(k, 1), ln] * xbuf[pl.ds(r * TAP_ROWS + base + k, TAP_ROWS), ln]
                o_ref[pl.ds(r * TAP_ROWS, TAP_ROWS), ln] = acc * _sig(acc)
            return carry

        lax.fori_loop(0, c // LANES, lane_tile, 0)

    t = nb * seq
    return pl.pallas_call(
        body,
        grid=(2, nb, ns),
        in_specs=[pl.BlockSpec((ts, c), lambda j, b, i: (b * ns + i, XBC_COL0 + j)),
                  pl.BlockSpec((8, c), lambda j, b, i: (0, j)), pl.BlockSpec((1, c), lambda j, b, i: (0, j))],
        out_specs=pl.BlockSpec((ts, c), lambda j, b, i: (b * ns + i, j)),
        out_shape=jax.ShapeDtypeStruct((t, D_XBC), F32),
        scratch_shapes=[pltpu.VMEM((halo + ts, c), F32)],
        compiler_params=_cparams(("parallel", "parallel", "arbitrary")),
        name="ssd_pre_fwd",
    )(proj, sw, sb)


def _ssd_pre_bwd(dproj, dxs, proj, sw, sb, *, nb, seq):
    ts, c, halo = CONV_ROWS, 1024, HALO4
    ns = seq // ts
    base = halo - SSD_CONV_K + 1
    hb = ts // halo

    def body(dproj_ref, d_ref, x_ref, xh_ref, w_ref, b_ref, dx_ref, dw_ref, db_ref, xbuf, dbuf, dwacc, dbacc):
        b, i = pl.program_id(1), pl.program_id(2)
        rc = ns - 1 - i

        @pl.when(jnp.logical_and(b == 0, i == 0))
        def _():
            dwacc[...] = jnp.zeros_like(dwacc)
            dbacc[...] = jnp.zeros_like(dbacc)

        @pl.when(i == 0)
        def _():
            dbuf[ts:ts + halo, :] = jnp.zeros((halo, c), F32)

        @pl.when(i > 0)
        def _():
            dbuf[ts:ts + halo, :] = dbuf[0:halo, :]

        @pl.when(rc == 0)
        def _():
            xbuf[0:halo, :] = jnp.zeros((halo, c), F32)

        @pl.when(rc > 0)
        def _():
            xbuf[0:halo, :] = xh_ref[...]

        xbuf[halo:halo + ts, :] = x_ref[...]

        def pre_tile(j, carry):
            ln = pl.ds(pl.multiple_of(j * LANES, LANES), LANES)
            for r in range(ts // TAP_ROWS):
                r0 = r * TAP_ROWS
                acc = jnp.broadcast_to(b_ref[:, ln], (TAP_ROWS, LANES))
                for k in range(SSD_CONV_K):
                    acc = acc + w_ref[pl.ds(k, 1), ln] * xbuf[pl.ds(r0 + base + k, TAP_ROWS), ln]
                s = _sig(acc)
                dc = d_ref[pl.ds(r0, TAP_ROWS), ln] * (s * (1.0 + acc * (1.0 - s)))
                dbuf[pl.ds(r0, TAP_ROWS), ln] = dc
                dbacc[:, ln] += _sum8(dc)
            return carry

        lax.fori_loop(0, c // LANES, pre_tile, 0)

        def lane_tile(j, carry):
            ln = pl.ds(pl.multiple_of(j * LANES, LANES), LANES)
            for r in range(ts // TAP_ROWS):
                r0 = r * TAP_ROWS
                d1 = dbuf[pl.ds(r0, TAP_ROWS), ln]
                acc = jnp.zeros((TAP_ROWS, LANES), F32)
                for k in range(SSD_CONV_K):
                    acc = acc + w_ref[pl.ds(k, 1), ln] * dbuf[pl.ds(r0 + SSD_CONV_K - 1 - k, TAP_ROWS), ln]
                    dwacc[pl.ds(k * 8, 8), ln] += _sum8(d1 * xbuf[pl.ds(r0 + base + k, TAP_ROWS), ln])
                dx_ref[pl.ds(r0, TAP_ROWS), ln] = acc.astype(BF16)
            return carry

        lax.fori_loop(0, c // LANES, lane_tile, 0)

        @pl.when(jnp.logical_and(b == nb - 1, i == ns - 1))
        def _():
            for k in range(SSD_CONV_K):
                dw_ref[pl.ds(k, 1), :] = jnp.sum(dwacc[pl.ds(k * 8, 8), :], axis=0, keepdims=True)
            dw_ref[pl.ds(SSD_CONV_K, 8 - SSD_CONV_K), :] = jnp.zeros((8 - SSD_CONV_K, c), F32)
            db_ref[...] = jnp.sum(dbacc[...], axis=0, keepdims=True)

    t = nb * seq
    rowblk = lambda b, i: b * ns + (ns - 1 - i)
    return pl.pallas_call(
        body,
        grid=(2, nb, ns),
        in_specs=[DEP_SPEC, pl.BlockSpec((ts, c), lambda j, b, i: (rowblk(b, i), j)),
                  pl.BlockSpec((ts, c), lambda j, b, i: (rowblk(b, i), XBC_COL0 + j)),
                  pl.BlockSpec((halo, c), lambda j, b, i: (jnp.maximum(rowblk(b, i) * hb - 1, 0), XBC_COL0 + j)),
                  pl.BlockSpec((8, c), lambda j, b, i: (0, j)), pl.BlockSpec((1, c), lambda j, b, i: (0, j))],
        out_specs=[pl.BlockSpec((ts, c), lambda j, b, i: (rowblk(b, i), XBC_COL0 + j)),
                   pl.BlockSpec((8, c), lambda j, b, i: (0, j)), pl.BlockSpec((1, c), lambda j, b, i: (0, j))],
        out_shape=[jax.ShapeDtypeStruct(dproj.shape, BF16), jax.ShapeDtypeStruct((8, D_XBC), F32),
                   jax.ShapeDtypeStruct((1, D_XBC), F32)],
        input_output_aliases={0: 0},
        scratch_shapes=[pltpu.VMEM((halo + ts, c), F32), pltpu.VMEM((ts + halo, c), F32),
                        pltpu.VMEM((SSD_CONV_K * 8, c), F32), pltpu.VMEM((8, c), F32)],
        compiler_params=_cparams(("arbitrary", "arbitrary", "arbitrary")),
        name="ssd_pre_bwd",
    )(dproj, dxs, proj, proj, sw, sb)


Z_COL = (2 * D_CONV) // 1024
GROUP_W = D_SSD // GROUPS


def _softplus(v):
    return jnp.maximum(v, 0.0) + jnp.log(1.0 + jnp.exp(-jnp.abs(v)))


def _dot(a, b):
    return jnp.dot(a, b, preferred_element_type=F32)


def _dot_nt(a, b):
    return lax.dot_general(a, b, (_DIMS["nt"], ((), ())), preferred_element_type=F32)


def _dot_tn(a, b):
    return lax.dot_general(a, b, (_DIMS["tn"], ((), ())), preferred_element_type=F32)


def _bf16_terms(v):
    hi = v.astype(BF16)
    r1 = v - hi.astype(F32)
    mid = r1.astype(BF16)
    return hi, mid, (r1 - mid.astype(F32)).astype(BF16)


def _dot_exact_left(sel, v):
    hi, mid, lo = _bf16_terms(v)
    return _dot(sel, hi) + (_dot(sel, mid) + _dot(sel, lo))


def _dot_exact_right(v, sel):
    hi, mid, lo = _bf16_terms(v)
    return _dot(hi, sel) + (_dot(mid, sel) + _dot(lo, sel))


def _chunk_decays(dtr_ref, bias_ref, alog_ref):
    q = CHUNK
    ii = lax.broadcasted_iota(jnp.int32, (q, q), 0)
    jj = lax.broadcasted_iota(jnp.int32, (q, q), 1)
    tri = jj <= ii
    dt = _softplus(dtr_ref[...] + bias_ref[...])
    a_head = -jnp.exp(alog_ref[...])
    cs = _dot_exact_left(tri.astype(BF16), dt * a_head)
    return tri, dt, a_head, cs, cs.T


def _ssd_fwd(uy, xs_all, proj, dtr, dt_bias, a_log, d_lanes, norm_w, *, nb, seq):
    q = CHUNK
    nc = seq // q
    t = nb * seq

    def body(uy_ref, xs_ref, bm_ref, cm_ref, z_ref, dtr_ref, bias_ref, alog_ref, dl_ref, nw_ref,
             y_ref, ys_ref, st_ref, state):
        @pl.when(pl.program_id(1) == 0)
        def _():
            state[...] = jnp.zeros_like(state)

        tri, dt, _, cs, cst = _chunk_decays(dtr_ref, bias_ref, alog_ref)
        first = lax.broadcasted_iota(jnp.int32, (1, LANES), 1) < HEAD_DIM
        for g in range(GROUPS):
            gl = slice(g * STATE, (g + 1) * STATE)
            bb = bm_ref[:, gl].astype(BF16)
            cb = cm_ref[:, gl].astype(BF16)
            scores = _dot_nt(cb, bb)
            for p in range(2):
                pr = 2 * g + p
                h0 = 2 * pr
                sl = slice(pr * LANES, (pr + 1) * LANES)
                xv = xs_ref[:, sl]
                dtp = jnp.where(first, dt[:, h0:h0 + 1], dt[:, h0 + 1:h0 + 2])
                csp = jnp.where(first, cs[:, h0:h0 + 1], cs[:, h0 + 1:h0 + 2])
                xd = xv * dtp
                yv = None
                for hh, keep in ((h0, first), (h0 + 1, jnp.logical_not(first))):
                    decay = jnp.where(tri, jnp.exp(cs[:, hh:hh + 1] - cst[hh:hh + 1, :]), 0.0)
                    part = _dot((scores * decay).astype(BF16), jnp.where(keep, xd, 0.0).astype(BF16))
                    yv = part if yv is None else yv + part
                hp = state[pr]
                st_ref[0, pr] = hp
                yv = yv + jnp.exp(csp) * _dot(cb, hp.astype(BF16))
                last = csp[q - 1:q, :]
                state[pr] = jnp.exp(last) * hp + _dot_tn(bb, (xd * jnp.exp(last - csp)).astype(BF16))
                ys_ref[:, sl] = yv + dl_ref[:, sl] * xv
        zv = z_ref[...]
        gated = ys_ref[...] * (zv * _sig(zv))
        for g in range(GROUPS):
            gl = slice(g * GROUP_W, (g + 1) * GROUP_W)
            v = gated[:, gl]
            r = lax.rsqrt(jnp.mean(v * v, axis=-1, keepdims=True) + EPS)
            y_ref[:, gl] = (v * r * nw_ref[:, gl]).astype(BF16)

    blk = lambda w, col: pl.BlockSpec((q, w), lambda b, c: (b * nc + c, col))
    vec = lambda w: pl.BlockSpec((1, w), lambda b, c: (0, 0))
    return pl.pallas_call(
        body,
        grid=(nb, nc),
        in_specs=[DEP_SPEC, blk(D_SSD, 0), blk(GROUPS * STATE, 2), blk(GROUPS * STATE, 3), blk(D_SSD, Z_COL),
                  blk(LANES, 0), vec(LANES), vec(LANES), vec(D_SSD), vec(D_SSD)],
        out_specs=[blk(D_SSD, 1), blk(D_SSD, 0),
                   pl.BlockSpec((1, PAIRS, STATE, LANES), lambda b, c: (b * nc + c, 0, 0, 0))],
        out_shape=[jax.ShapeDtypeStruct(uy.shape, BF16), jax.ShapeDtypeStruct((t, D_SSD), F32),
                   jax.ShapeDtypeStruct((nb * nc, PAIRS, STATE, LANES), F32)],
        input_output_aliases={0: 0},
        scratch_shapes=[pltpu.VMEM((PAIRS, STATE, LANES), F32)],
        compiler_params=_cparams(("parallel", "arbitrary")),
        name="ssd_fwd",
    )(uy, xs_all, xs_all, xs_all, proj, dtr, dt_bias, a_log, d_lanes, norm_w)


def _ssd_bwd(dproj, duy, proj, ys, xs_all, dtr, states, dt_bias, a_log, d_lanes, norm_w, *, nb, seq):
    q = CHUNK
    nc = seq // q
    t = nb * seq
    head_of_lane = (jnp.arange(D_SSD)[:, None] // HEAD_DIM == jnp.arange(LANES)[None, :]).astype(BF16)

    def body(dproj_ref, dy_ref, z_ref, ys_ref, xs_ref, bm_ref, cm_ref, dtr_ref, st_ref, bias_ref, alog_ref, dl_ref,
             nw_ref, sel_ref, dz_ref, dx_ref, ddtr_ref, small_ref,
             dstate, dys_buf, dcsl, ddtl, dcst, dnw_acc, dd_acc, dbias_acc, da_acc):
        b, c = pl.program_id(0), pl.program_id(1)

        @pl.when(jnp.logical_and(b == 0, c == 0))
        def _():
            dnw_acc[...] = jnp.zeros_like(dnw_acc)
            dd_acc[...] = jnp.zeros_like(dd_acc)
            dbias_acc[...] = jnp.zeros_like(dbias_acc)
            da_acc[...] = jnp.zeros_like(da_acc)
            dcst[...] = jnp.zeros_like(dcst)

        @pl.when(c == 0)
        def _():
            dstate[...] = jnp.zeros_like(dstate)

        zv = z_ref[...]
        sz = _sig(zv)
        silz = zv * sz
        ysv = ys_ref[...]
        gated = ysv * silz
        dyv = dy_ref[...]
        nwv = nw_ref[...]
        for g in range(GROUPS):
            gl = slice(g * GROUP_W, (g + 1) * GROUP_W)
            v = gated[:, gl]
            r = lax.rsqrt(jnp.mean(v * v, axis=-1, keepdims=True) + EPS)
            yn = v * r
            dyn = dyv[:, gl] * nwv[:, gl]
            dnw_acc[:, gl] += jnp.sum(dyv[:, gl] * yn, axis=0, keepdims=True)
            dys_buf[:, gl] = r * (dyn - yn * jnp.mean(dyn * yn, axis=-1, keepdims=True))
        dgated = dys_buf[...]
        dz_ref[...] = (dgated * ysv * (sz * (1.0 + zv * (1.0 - sz)))).astype(BF16)
        dys_all = dgated * silz
        dys_buf[...] = dys_all
        dd_acc[...] += jnp.sum(dys_all * xs_ref[...], axis=0, keepdims=True)

        tri, dt, a_head, cs, cst = _chunk_decays(dtr_ref, bias_ref, alog_ref)
        lane = lax.broadcasted_iota(jnp.int32, (1, LANES), 1)
        first = lane < HEAD_DIM
        dcs_h = jnp.zeros((q, LANES), F32)
        for g in range(GROUPS):
            gl = slice(g * STATE, (g + 1) * STATE)
            bb = bm_ref[:, gl].astype(BF16)
            cb = cm_ref[:, gl].astype(BF16)
            scores = _dot_nt(cb, bb)
            dscores = jnp.zeros((q, q), F32)
            dbg = jnp.zeros((q, STATE), F32)
            dcg = jnp.zeros((q, STATE), F32)
            for p in range(2):
                pr = 2 * g + p
                h0 = 2 * pr
                sl = slice(pr * LANES, (pr + 1) * LANES)
                xv = xs_ref[:, sl]
                dyp = dys_buf[:, sl]
                dtp = jnp.where(first, dt[:, h0:h0 + 1], dt[:, h0 + 1:h0 + 2])
                csp = jnp.where(first, cs[:, h0:h0 + 1], cs[:, h0 + 1:h0 + 2])
                xd = xv * dtp
                xdb = xd.astype(BF16)
                hp = st_ref[0, pr]
                dhn = dstate[pr]
                hpb = hp.astype(BF16)
                dhnb = dhn.astype(BF16)
                lam = jnp.exp(csp)
                last = csp[q - 1:q, :]
                gam = jnp.exp(last)
                w = jnp.exp(last - csp)
                dxd = jnp.zeros((q, LANES), F32)
                for hh, keep in ((h0, first), (h0 + 1, jnp.logical_not(first))):
                    decay = jnp.where(tri, jnp.exp(cs[:, hh:hh + 1] - cst[hh:hh + 1, :]), 0.0)
                    m = scores * decay
                    dym = jnp.where(keep, dyp, 0.0).astype(BF16)
                    dm = _dot_nt(dym, xdb)
                    dxd = dxd + _dot_tn(m.astype(BF16), dym)
                    e = dm * m
                    dcs_h = dcs_h + jnp.where(lane == hh, jnp.sum(e, axis=1, keepdims=True), 0.0)
                    dcst[hh:hh + 1, :] = jnp.sum(e, axis=0, keepdims=True)
                    dscores = dscores + dm * decay
                yoff = lam * _dot(cb, hpb)
                ldy = (lam * dyp).astype(BF16)
                dcg = dcg + _dot_nt(ldy, hpb)
                dstate[pr] = gam * dhn + _dot_tn(cb, ldy)
                bdh = _dot(bb, dhnb)
                dxd = dxd + w * bdh
                xdw = xd * w
                dbg = dbg + _dot_nt(xdw.astype(BF16), dhnb)
                wd = xdw * bdh
                dcsl[:, sl] = dyp * yoff - wd
                dcsl[q - 1:q, sl] += (jnp.sum(wd, axis=0, keepdims=True)
                                      + gam * jnp.sum(dhn * hp, axis=0, keepdims=True))
                dx_ref[:, sl] = dxd * dtp + dyp * dl_ref[:, sl]
                ddtl[:, sl] = dxd * xv
            dsb = dscores.astype(BF16)
            dx_ref[:, D_SSD + g * STATE:D_SSD + (g + 1) * STATE] = dbg + _dot_tn(dsb, cb)
            dx_ref[:, D_SSD + (GROUPS + g) * STATE:D_SSD + (GROUPS + g + 1) * STATE] = dcg + _dot(dsb, bb)

        sel = sel_ref[...]
        dcs_h = dcs_h + _dot_exact_right(dcsl[...], sel) - dcst[...].T
        ddt = _dot_exact_right(ddtl[...], sel)
        upper = lax.broadcasted_iota(jnp.int32, (q, q), 1) >= lax.broadcasted_iota(jnp.int32, (q, q), 0)
        da = _dot_exact_left(upper.astype(BF16), dcs_h)
        ddt = ddt + da * a_head
        da_acc[...] += jnp.sum(da * dt, axis=0, keepdims=True)
        ddtr = ddt * _sig(dtr_ref[...] + bias_ref[...])
        ddtr_ref[...] = ddtr
        dbias_acc[...] += jnp.sum(ddtr, axis=0, keepdims=True)

        @pl.when(jnp.logical_and(b == nb - 1, c == nc - 1))
        def _():
            small_ref[...] = jnp.zeros_like(small_ref)
            small_ref[0:1, :] = dnw_acc[...]
            small_ref[1:2, 0:LANES] = _dot_exact_right(jnp.broadcast_to(dd_acc[...], (8, D_SSD)), sel)[0:1, :]
            small_ref[2:3, 0:LANES] = dbias_acc[...]
            small_ref[3:4, 0:LANES] = da_acc[...] * a_head

    rowblk = lambda b, c: b * nc + (nc - 1 - c)
    blk = lambda w, col: pl.BlockSpec((q, w), lambda b, c: (rowblk(b, c), col))
    vec = lambda w: pl.BlockSpec((1, w), lambda b, c: (0, 0))
    return pl.pallas_call(
        body,
        grid=(nb, nc),
        in_specs=[DEP_SPEC, blk(D_SSD, 1), blk(D_SSD, Z_COL), blk(D_SSD, 0), blk(D_SSD, 0), blk(GROUPS * STATE, 2),
                  blk(GROUPS * STATE, 3), blk(LANES, 0),
                  pl.BlockSpec((1, PAIRS, STATE, LANES), lambda b, c: (rowblk(b, c), 0, 0, 0)),
                  vec(LANES), vec(LANES), vec(D_SSD), vec(D_SSD), pl.BlockSpec((D_SSD, LANES), lambda b, c: (0, 0))],
        out_specs=[blk(D_SSD, Z_COL), blk(D_XBC, 0), blk(LANES, 0), pl.BlockSpec((8, D_SSD), lambda b, c: (0, 0))],
        out_shape=[jax.ShapeDtypeStruct(dproj.shape, BF16), jax.ShapeDtypeStruct((t, D_XBC), F32),
                   jax.ShapeDtypeStruct((t, LANES), F32), jax.ShapeDtypeStruct((8, D_SSD), F32)],
        input_output_aliases={0: 0},
        scratch_shapes=[pltpu.VMEM((PAIRS, STATE, LANES), F32), pltpu.VMEM((q, D_SSD), F32),
                        pltpu.VMEM((q, D_SSD), F32), pltpu.VMEM((q, D_SSD), F32), pltpu.VMEM((LANES, q), F32),
                        pltpu.VMEM((1, D_SSD), F32), pltpu.VMEM((1, D_SSD), F32), pltpu.VMEM((1, LANES), F32),
                        pltpu.VMEM((1, LANES), F32)],
        compiler_params=_cparams(("arbitrary", "arbitrary")),
        name="ssd_bwd",
    )(dproj, duy, proj, ys, xs_all, xs_all, xs_all, dtr, states, dt_bias, a_log, d_lanes, norm_w, head_of_lane)


HBM_SPEC = pl.BlockSpec(memory_space=pltpu.HBM)
MESH_ID = pl.DeviceIdType.MESH


def _coords():
    return lax.axis_index("x"), lax.axis_index("y"), lax.axis_index("c")


def _chip_peer(xi, yi, ci, d):
    return (jnp.bitwise_xor(xi, d >> 1), jnp.bitwise_xor(yi, d & 1), ci)


def _remote(src, dst, send_sem, recv_sem, peer):
    return pltpu.make_async_remote_copy(src_ref=src, dst_ref=dst, send_sem=send_sem, recv_sem=recv_sem,
                                        device_id=peer, device_id_type=MESH_ID)


SEM_SPEC = pl.BlockSpec(memory_space=pltpu.SEMAPHORE)
ANY_SPEC = pl.BlockSpec(memory_space=pl.ANY)
EFFECT = pltpu.SideEffectType.DATAFLOW_SIDE_EFFECTING
COPIES = 3


def _half(ref, axis, which, lead=0):
    size = ref.shape[lead + axis] // 2
    part = pl.ds(which * size, size)
    idx = (slice(None),) * lead + ((part, slice(None)) if axis == 0 else (slice(None), part))
    return ref.at[idx]


def _halved_shape(shape, axis):
    lead = len(shape) - 2
    return tuple(d // 2 if i == lead + axis else d for i, d in enumerate(shape))


def _gather_plan(axis):
    def plan(xi, yi, ci, src, land):
        me = 2 * xi + yi
        out = []
        for d in (1, 2, 3):
            there = jnp.bitwise_xor(me, d)
            if axis is None:
                out.append((src, land.at[me], _chip_peer(xi, yi, ci, d), land.at[there]))
            else:
                out.append((_half(src, axis, ci), _half(land.at[me], axis, ci), _chip_peer(xi, yi, ci, d),
                            _half(land.at[there], axis, ci)))
        return out
    return plan


def _sibling_plan(xi, yi, ci, src, land):
    return [(src, land, (xi, yi, 1 - ci), land)]


def _owners_plan(xi, yi, ci, src, land):
    me = 2 * xi + yi
    return [(src.at[jnp.bitwise_xor(me, d)], land.at[d - 1], _chip_peer(xi, yi, ci, d), land.at[d - 1])
            for d in (1, 2, 3)]


def _split_start(srcs, lands, plans, *, name, dep=None):
    n = len(srcs)
    deps = [] if dep is None else [dep]

    def body(*refs):
        src_refs, land_refs = refs[:n], refs[n:2 * n]
        outs = refs[2 * n + len(deps):]
        ssems, rsems = outs[:n], outs[n:2 * n]
        token = refs[-1]
        xi, yi, ci = _coords()
        for t in range(n):
            for k, (src, dst, peer, _) in enumerate(plans[t](xi, yi, ci, src_refs[t], land_refs[t])):
                _remote(src, dst, ssems[t].at[k], rsems[t].at[k], peer).start()
        token[...] = jnp.zeros_like(token)

    bufs = list(srcs) + list(lands)
    outs = pl.pallas_call(
        body,
        name=name,
        in_specs=[HBM_SPEC] * (2 * n) + [ANY_SPEC] * len(deps),
        out_specs=[SEM_SPEC] * (2 * n) + [HBM_SPEC] * (2 * n) + [pl.BlockSpec(memory_space=pltpu.VMEM)],
        out_shape=[pltpu.SemaphoreType.DMA((COPIES,))] * (2 * n) + [pltpu.HBM(a.shape, a.dtype) for a in bufs]
        + [jax.ShapeDtypeStruct((8, LANES), F32)],
        input_output_aliases={i: 2 * n + i for i in range(2 * n)},
        compiler_params=pltpu.CompilerParams(has_side_effects=EFFECT),
    )(*[pltpu.with_memory_space_constraint(a, pltpu.HBM) for a in bufs], *deps)
    return outs[:n], outs[n:2 * n], outs[2 * n:3 * n], outs[3 * n:4 * n], outs[-1]


def _split_wait(ssems, rsems, srcs, lands, plans, after, *, name):
    n = len(srcs)

    def body(*refs):
        src_refs, land_refs = refs[:n], refs[n:2 * n]
        ss, rs = refs[2 * n:3 * n], refs[3 * n:4 * n]
        xi, yi, ci = _coords()
        for t in range(n):
            for k, (src, _, peer, landed) in enumerate(plans[t](xi, yi, ci, src_refs[t], land_refs[t])):
                cp = _remote(src, landed, ss[t].at[k], rs[t].at[k], peer)
                cp.wait_send()
                cp.wait_recv()

    bufs = list(srcs) + list(lands)
    outs = pl.pallas_call(
        body,
        name=name,
        in_specs=[HBM_SPEC] * (2 * n) + [SEM_SPEC] * (2 * n) + [ANY_SPEC],
        out_specs=[HBM_SPEC] * (2 * n),
        out_shape=[pltpu.HBM(a.shape, a.dtype) for a in bufs],
        input_output_aliases={i: i for i in range(2 * n)},
        compiler_params=pltpu.CompilerParams(has_side_effects=EFFECT),
    )(*bufs, *ssems, *rsems, after)
    return outs[:n], outs[n:]


def _forward_halves(lands, axes, *, name):
    n = len(lands)

    def body(*refs):
        ins, outs = refs[:n], refs[n:2 * n]
        send_sems, recv_sems = refs[2 * n:]
        xi, yi, ci = _coords()
        me = 2 * xi + yi
        sibling = (xi, yi, 1 - ci)
        cps = []
        for t in range(n):
            for d in (1, 2, 3):
                slot = jnp.bitwise_xor(me, d)
                k = COPIES * t + d - 1
                cp = _remote(_half(ins[t].at[slot], axes[t], ci), _half(outs[t].at[slot], axes[t], ci),
                             send_sems.at[k], recv_sems.at[k], sibling)
                cp.start()
                cps.append(cp)
        for t in range(n):
            for d in (1, 2, 3):
                got = _half(outs[t].at[jnp.bitwise_xor(me, d)], axes[t], 1 - ci)
                k = COPIES * t + d - 1
                _remote(got, got, send_sems.at[k], recv_sems.at[k], sibling).wait_recv()
        for cp in cps:
            cp.wait_send()

    return pl.pallas_call(
        body,
        name=name,
        in_specs=[HBM_SPEC] * n,
        out_specs=[HBM_SPEC] * n,
        out_shape=[jax.ShapeDtypeStruct(a.shape, a.dtype) for a in lands],
        input_output_aliases={i: i for i in range(n)},
        scratch_shapes=[pltpu.SemaphoreType.DMA((COPIES * n,)), pltpu.SemaphoreType.DMA((COPIES * n,))],
    )(*lands)


def _swap_other_halves(gs, axes, *, name):
    n = len(gs)

    def body(*refs):
        ins, lands = refs[:n], refs[n:2 * n]
        send_sems, recv_sems = refs[2 * n:]
        xi, yi, ci = _coords()
        sibling = (xi, yi, 1 - ci)
        cps = []
        for t in range(n):
            cp = _remote(_half(ins[t], axes[t], 1 - ci, lead=1), lands[t], send_sems.at[t], recv_sems.at[t], sibling)
            cp.start()
            cps.append(cp)
        for cp in cps:
            cp.wait_recv()
        for cp in cps:
            cp.wait_send()

    return pl.pallas_call(
        body,
        in_specs=[HBM_SPEC] * n,
        out_specs=[HBM_SPEC] * n,
        out_shape=[jax.ShapeDtypeStruct(_halved_shape(g.shape, ax), g.dtype) for g, ax in zip(gs, axes)],
        scratch_shapes=[pltpu.SemaphoreType.DMA((n,)), pltpu.SemaphoreType.DMA((n,))],
        name=name,
    )(*gs)


def _row_tile(rows, cap=512, mult=16):
    best = mult
    for cand in range(mult, min(rows, cap) + 1, mult):
        if rows % cand == 0:
            best = cand
    assert rows % best == 0, rows
    return best


COL_TILE = 256


def _half_tiles(hr, hc, axis, cap=512, mult=16):
    if axis == 0:
        tr = _row_tile(hr, cap, mult)
        n = hr // tr
        return (tr, hc), n, lambda half, i: (half * n + i, 0)
    n = hc // COL_TILE
    return (hr, COL_TILE), n, lambda half, i: (0, half * n + i)


def _add_core_halves(g, land, where, axis):
    nslot, hr, hc = land.shape
    bshape, nr, idx = _half_tiles(hr, hc, axis)

    def body(where_ref, g_ref, l_ref, f_ref, b_ref):
        s = g_ref[...] + l_ref[...].astype(F32)
        b_ref[...] = s.astype(BF16)

        @pl.when(pl.program_id(1) == where_ref[1])
        def _():
            f_ref[...] = s

    blk = pl.BlockSpec((None,) + bshape, lambda i, s, w: (s,) + idx(0, i))
    mine = pl.BlockSpec((None,) + bshape, lambda i, s, w: (s,) + idx(w[0], i))
    return pl.pallas_call(
        body,
        grid_spec=pltpu.PrefetchScalarGridSpec(
            num_scalar_prefetch=1,
            grid=(nr, nslot),
            in_specs=[mine, blk],
            out_specs=[pl.BlockSpec(bshape, lambda i, s, w: idx(0, i)), blk],
        ),
        out_shape=[jax.ShapeDtypeStruct((hr, hc), F32), jax.ShapeDtypeStruct(land.shape, BF16)],
        compiler_params=_cparams(("parallel", "arbitrary")),
        name="add_core_halves",
    )(where, g, land)


def _add_chip_sums(pf, land, where, axis):
    hr, cols = pf.shape
    bshape, nr, idx = _half_tiles(hr, cols, axis)

    def body(where_ref, p_ref, l_ref, o_ref):
        acc = p_ref[...]
        for d in range(3):
            acc = acc + l_ref[d].astype(F32)
        o_ref[...] = acc

    return pl.pallas_call(
        body,
        grid_spec=pltpu.PrefetchScalarGridSpec(
            num_scalar_prefetch=1,
            grid=(nr,),
            in_specs=[pl.BlockSpec(bshape, lambda i, w: idx(0, i)),
                      pl.BlockSpec((3,) + bshape, lambda i, w: (0,) + idx(0, i))],
            out_specs=pl.BlockSpec(bshape, lambda i, w: idx(0, i)),
        ),
        out_shape=jax.ShapeDtypeStruct((hr, cols), F32),
        compiler_params=_cparams(("parallel",)),
        name="add_chip_sums",
    )(where, pf, land)


N_DEV = 8


def _all_reduce_small(part):
    r, w = part.shape

    def body(p_ref, o_ref, gath, send_sems, recv_sems):
        xi, yi, ci = _coords()
        me = 4 * xi + 2 * yi + ci
        gath[me] = p_ref[...]
        cps = []
        for d in range(1, N_DEV):
            peer = (jnp.bitwise_xor(xi, d >> 2), jnp.bitwise_xor(yi, (d >> 1) & 1), jnp.bitwise_xor(ci, d & 1))
            cp = _remote(p_ref, gath.at[me], send_sems.at[d - 1], recv_sems.at[d - 1], peer)
            cp.start()
            cps.append(cp)
        for d in range(1, N_DEV):
            src = gath.at[jnp.bitwise_xor(me, d)]
            _remote(src, src, send_sems.at[d - 1], recv_sems.at[d - 1], (xi, yi, ci)).wait_recv()
        acc = gath[0]
        for k in range(1, N_DEV):
            acc = acc + gath[k]
        o_ref[...] = acc
        for cp in cps:
            cp.wait_send()

    vm = pl.BlockSpec(memory_space=pltpu.VMEM)
    return pl.pallas_call(
        body,
        in_specs=[vm],
        out_specs=vm,
        out_shape=jax.ShapeDtypeStruct((r, w), F32),
        scratch_shapes=[pltpu.VMEM((N_DEV, r, w), F32), pltpu.SemaphoreType.DMA((N_DEV - 1,)),
                        pltpu.SemaphoreType.DMA((N_DEV - 1,))],
        name="all_reduce_small",
    )(part)


def _adamw_math(wv, gv, mv, vv):
    mn = ADAM_B1 * mv + (1.0 - ADAM_B1) * gv
    vn = ADAM_B2 * vv + (1.0 - ADAM_B2) * (gv * gv)
    m_hat = mn / (1.0 - ADAM_B1 ** ADAM_STEP)
    v_hat = vn / (1.0 - ADAM_B2 ** ADAM_STEP)
    return -ADAM_LR * (m_hat / (jnp.sqrt(v_hat) + ADAM_EPS) + ADAM_WD * wv), mn, vn


def _adamw_halves(w, g_mine, g_other, m, v, where, axis, *, name):
    rows, cols = w.shape
    hr, hc = g_mine.shape
    bshape, nr, idx = _half_tiles(hr, hc, axis, cap=256, mult=8)

    def body(where_ref, w_ref, gm_ref, go_ref, m_ref, v_ref, g_ref, d_ref, nm_ref, nv_ref):
        is_mine = pl.program_id(0) == where_ref[0]
        gv = jnp.where(is_mine, gm_ref[...], go_ref[...])
        g_ref[...] = gv
        d_ref[...], nm_ref[...], nv_ref[...] = _adamw_math(w_ref[...], gv, m_ref[...], v_ref[...])

    def parked(half, i, holder):
        return idx(0, jnp.where(half == holder, i, jnp.where(half < holder, 0, nr - 1)))

    blk = pl.BlockSpec(bshape, lambda hf, i, wh: idx(hf, i))
    o = jax.ShapeDtypeStruct((rows, cols), F32)
    return pl.pallas_call(
        body,
        grid_spec=pltpu.PrefetchScalarGridSpec(
            num_scalar_prefetch=1,
            grid=(2, nr),
            in_specs=[blk, pl.BlockSpec(bshape, lambda hf, i, wh: parked(hf, i, wh[0])),
                      pl.BlockSpec(bshape, lambda hf, i, wh: parked(hf, i, 1 - wh[0])), blk, blk],
            out_specs=[blk] * 4,
        ),
        out_shape=[o, o, o, o],
        compiler_params=_cparams(("arbitrary", "arbitrary")),
        name=name,
    )(where, w, g_mine, g_other, m, v)


def _adamw(w, g, m, v, *, name):
    rows, cols = w.shape
    tr = _row_tile(rows, cap=256, mult=8)

    def body(w_ref, g_ref, m_ref, v_ref, d_ref, nm_ref, nv_ref):
        d_ref[...], nm_ref[...], nv_ref[...] = _adamw_math(w_ref[...], g_ref[...], m_ref[...], v_ref[...])

    blk = pl.BlockSpec((tr, cols), lambda i: (i, 0))
    o = jax.ShapeDtypeStruct((rows, cols), F32)
    return pl.pallas_call(
        body,
        grid=(rows // tr,),
        in_specs=[blk] * 4,
        out_specs=[blk] * 3,
        out_shape=[o, o, o],
        compiler_params=_cparams(("parallel",)),
        name=name,
    )(w, g, m, v)


def _pack(arrs):
    flat = jnp.concatenate([a.reshape(-1) for a in arrs])
    pad = (-flat.shape[0]) % (8 * LANES)
    return jnp.pad(flat, (0, pad)).reshape(-1, LANES)


def _unpack(packed, shapes):
    flat = packed.reshape(-1)
    out, off = [], 0
    for s in shapes:
        n = 1
        for dim in s:
            n *= dim
        out.append(flat[off:off + n].reshape(s))
        off += n
    return out


def _pad_rows(a, rows):
    return jnp.pad(a, ((0, rows - a.shape[0]), (0, 0)))


def _pad_lanes(a, lanes=LANES):
    return jnp.pad(a, ((0, 0), (0, lanes - a.shape[1])))


def _local_grads(x2d, tgt2d, prm, get_w, on_grads, *, nb, seq, dep=None):
    g_pre, g_post, g_fpre, g_fpost = prm["norm_mix_pre"], prm["norm_mix_post"], prm["norm_ffn_pre"], prm["norm_ffn_post"]
    dt_bias, a_log = _pad_lanes(prm["ssd_dt_bias"]), _pad_lanes(prm["ssd_a_log"])
    d_lanes = jnp.repeat(prm["ssd_d"], HEAD_DIM, axis=1)

    h = _rms_fwd(x2d, g_pre, dep=dep, out_dtype=BF16, name="rms_mix_pre")
    t, d = x2d.shape
    w_in_t, w_dt_t, cw, sw = get_w("in", h)
    proj = _matmul([(h, w_in_t)], mode="nt", out_dtype=F32, tm=1024, tn=1024, tk=2048, name="mm_proj",
                   extent=(t, D_MAIN, d))
    dtr = _matmul([(h, w_dt_t)], mode="nt", out_dtype=F32, tm=1024, tn=128, tk=2048, name="mm_dt")
    u1, uy = _conv_branch_fwd(proj, cw, prm["conv_dw_b"], prm["conv_ln_g"], prm["conv_ln_b"], nb=nb, seq=seq)
    xs_all = _ssd_pre_fwd(proj, sw, prm["ssd_conv_b"], nb=nb, seq=seq)
    uy, ys, states = _ssd_fwd(uy, xs_all, proj, dtr, dt_bias, a_log, d_lanes, prm["ssd_norm_w"], nb=nb, seq=seq)
    w_out = get_w("out", uy)
    mix = _matmul([(uy, w_out)], mode="nn", out_dtype=F32, tm=1024, tn=1024, tk=2048, name="mm_mix")
    x1, h2 = _rms_post_pre(mix, x2d, g_post, g_fpre)
    w_gate, w_up = get_w("up", h2)
    gt, up, act = _ffn_up(h2, w_gate, w_up, tm=1024, tn=512)
    w_down = get_w("down", act)
    dx2, df, loss, dg_fpost = _down_loss(act, w_down, x1, tgt2d, g_fpost, tk=1408)

    dgt, dup = _ffn_bwd_act(df, w_down, gt, up, tm=1024, tn=512)
    dw_down = _matmul([(act, df)], mode="tn", out_dtype=F32, tm=1408, tn=1024, tk=2048, name="mm_dw_down",
                      also_bf16=True)
    dw_gate = _matmul([(h2, dgt)], mode="tn", out_dtype=F32, tm=1024, tn=1408, tk=2048, name="mm_dw_gate",
                      slot_out=True, also_bf16=True)
    dw_up = _matmul([(h2, dup)], mode="tn", out_dtype=F32, tm=1024, tn=1408, tk=2048, name="mm_dw_up",
                    slot_out=True, also_bf16=True)
    dep = on_grads("ffn", (dw_down, dw_gate, dw_up))
    dh2 = _matmul([(dgt, w_gate), (dup, w_up)], mode="nt", out_dtype=F32, tm=1024, tn=1024, tk=1408, name="mm_dh2",
                  dep=dep)
    dx1, dmix, dg_fpre, dg_post = _rms_bwd_pre_post(dh2, x1, g_fpre, dx2, mix, g_post)
    dw_out = _matmul([(uy, dmix)], mode="tn", out_dtype=F32, tm=1024, tn=1024, tk=2048, name="mm_dw_out",
                     also_bf16=True)
    dep = on_grads("out", (dw_out,))
    duy = _matmul([(dmix, w_out)], mode="nt", out_dtype=F32, tm=1024, tn=1024, tk=2048, name="mm_duy", dep=dep)
    dproj, dcw, dcb, dlg, dlb = _conv_branch_bwd(duy, u1, proj, cw, prm["conv_ln_g"], prm["conv_ln_b"], nb=nb, seq=seq)
    dproj, dxs, ddtr, ssd_small = _ssd_bwd(dproj, duy, proj, ys, xs_all, dtr, states, dt_bias, a_log, d_lanes,
                                           prm["ssd_norm_w"], nb=nb, seq=seq)
    dproj, dsw, dsb = _ssd_pre_bwd(dproj, dxs, proj, sw, prm["ssd_conv_b"], nb=nb, seq=seq)
    ddtr_b = ddtr.astype(BF16)
    dw_in_t = _matmul([(dproj, h)], mode="tn", out_dtype=F32, tm=1024, tn=1024, tk=2048, name="mm_dw_main",
                      extent=(D_MAIN, d, t), out_rows=D_IN)
    dw_in_t = _dw_dt_rows(dw_in_t, ddtr_b, h)
    dep = on_grads("in", ((dw_in_t, dw_in_t),))
    dx, dg_pre = _dh_dx(dproj, w_in_t, ddtr_b, w_dt_t, x2d, dx1, g_pre, tk=1280, dep=g_pre if dep is None else dep)

    grads = {
        "norm_mix_pre": dg_pre,
        "w_in": dw_in_t,
        "conv_dw_w": dcw[:CONV_K], "conv_dw_b": dcb, "conv_ln_g": dlg, "conv_ln_b": dlb,
        "ssd_conv_w": dsw[:SSD_CONV_K], "ssd_conv_b": dsb,
        "ssd_dt_bias": ssd_small[2:3, :HEADS], "ssd_a_log": ssd_small[3:4, :HEADS], "ssd_d": ssd_small[1:2, :HEADS],
        "ssd_norm_w": ssd_small[0:1],
        "w_out": dw_out[0],
        "norm_mix_post": dg_post, "norm_ffn_pre": dg_fpre,
        "w_gate": dw_gate[0], "w_up": dw_up[0],
        "w_down": dw_down[0], "norm_ffn_post": dg_fpost,
    }
    return loss, dx, grads


BIG = ("w_in", "w_out", "w_gate", "w_up", "w_down")
HALF_AXIS = {"w_in": 1, "w_out": 0, "w_gate": 0, "w_up": 0, "w_down": 0}
GATHER_STAGES = {"in": ("w_in", "conv_dw_w", "ssd_conv_w"), "out": ("w_out",), "up": ("w_gate", "w_up"),
                 "down": ("w_down",)}
GATHER_ORDER = tuple(n for st in ("in", "out", "up", "down") for n in GATHER_STAGES[st])
SMALL = ("norm_mix_pre", "conv_dw_w", "conv_dw_b", "conv_ln_g", "conv_ln_b", "ssd_conv_w", "ssd_conv_b", "ssd_dt_bias",
         "ssd_a_log", "ssd_d", "ssd_norm_w", "norm_mix_post", "norm_ffn_pre", "norm_ffn_post")
WEIGHTS = ("norm_mix_pre", "w_in", "conv_dw_w", "conv_dw_b", "conv_ln_g", "conv_ln_b", "ssd_conv_w", "ssd_conv_b",
           "ssd_dt_bias", "ssd_a_log", "ssd_d", "ssd_norm_w", "w_out", "norm_mix_post", "norm_ffn_pre", "w_gate", "w_up",
           "w_down", "norm_ffn_post")


def _cols_from_slots(a):
    n, rows, w = a.shape
    return a.transpose(1, 0, 2).reshape(rows, n * w)


def kernel(x, norm_mix_pre, w_in, conv_dw_w, conv_dw_b, conv_ln_g, conv_ln_b, ssd_conv_w, ssd_conv_b, ssd_dt_bias, ssd_a_log, ssd_d, ssd_norm_w, w_out, norm_mix_post, norm_ffn_pre, w_gate, w_up, w_down, norm_ffn_post, loss_target, m_norm_mix_pre, m_w_in, m_conv_dw_w, m_conv_dw_b, m_conv_ln_g, m_conv_ln_b, m_ssd_conv_w, m_ssd_conv_b, m_ssd_dt_bias, m_ssd_a_log, m_ssd_d, m_ssd_norm_w, m_w_out, m_norm_mix_post, m_norm_ffn_pre, m_w_gate, m_w_up, m_w_down, m_norm_ffn_post, v_norm_mix_pre, v_w_in, v_conv_dw_w, v_conv_dw_b, v_conv_ln_g, v_conv_ln_b, v_ssd_conv_w, v_ssd_conv_b, v_ssd_dt_bias, v_ssd_a_log, v_ssd_d, v_ssd_norm_w, v_w_out, v_norm_mix_post, v_norm_ffn_pre, v_w_gate, v_w_up, v_w_down, v_norm_ffn_post):
    args = dict(locals())
    two_d = lambda n, a: jnp.swapaxes(a, 1, 2)[0] if n == "w_in" else a.reshape(a.shape[-2:])
    wts = {n: two_d(n, args[n]) for n in WEIGHTS}
    ms = {n: two_d(n, args["m_" + n]) for n in WEIGHTS}
    vs = {n: two_d(n, args["v_" + n]) for n in WEIGHTS}
    nb, seq, d = x.shape
    t = nb * seq
    xi, yi, ci = _coords()
    chip = 2 * xi + yi
    where = jnp.stack([ci, chip]).astype(jnp.int32)

    shards = {n: wts[n].astype(BF16) for n in BIG}
    shards.update(conv_dw_w=_pad_rows(wts["conv_dw_w"], 32), ssd_conv_w=_pad_rows(wts["ssd_conv_w"], 8))
    plans = [_gather_plan(HALF_AXIS.get(n)) for n in GATHER_ORDER]
    n_first = len(GATHER_STAGES["in"])

    def start(part, name, dep=None):
        src = [shards[n] for n in GATHER_ORDER[part]]
        return _split_start(src, [lax.empty((N_CHIPS,) + s.shape, s.dtype) for s in src], plans[part], name=name,
                            dep=dep)

    head = start(slice(0, n_first), "gather_start_in")
    rest = start(slice(n_first, None), "gather_start_rest", dep=head[4])
    ssems, rsems, srcs, lands = (list(h) + list(r) for h, r in zip(head[:4], rest[:4]))
    token = rest[4]

    def get_w(stage, after):
        names = GATHER_STAGES[stage]
        pick = lambda seq_: [seq_[GATHER_ORDER.index(n)] for n in names]
        own, got = _split_wait(pick(ssems), pick(rsems), pick(srcs), pick(lands), pick(plans), after,
                               name="gather_wait_" + stage)
        got, own = dict(zip(names, got)), dict(zip(names, own))
        big = [n for n in names if n in BIG]
        got.update(zip(big, _forward_halves([got[n] for n in big], [HALF_AXIS[n] for n in big],
                                            name="gather_forward_" + stage)))
        full = {n: lax.dynamic_update_slice(got[n], own[n][None], (chip, 0, 0)) for n in got}
        if stage == "in":
            w_in_t = full["w_in"].reshape(D_IN, D_MODEL)
            return (w_in_t, _pad_rows(w_in_t[D_MAIN:], LANES), _cols_from_slots(full["conv_dw_w"]),
                    _cols_from_slots(full["ssd_conv_w"]))
        if stage == "out":
            return full["w_out"].reshape(D_MODEL, D_MODEL)
        if stage == "up":
            return _cols_from_slots(full["w_gate"]), _cols_from_slots(full["w_up"])
        return full["w_down"].reshape(D_FF, D_MODEL)

    reduce_groups = {"ffn": ("w_down", "w_gate", "w_up"), "out": ("w_out",), "in": ("w_in",)}
    in_flight = {}

    def on_grads(stage, gs):
        names = reduce_groups[stage]
        axes = [HALF_AXIS[n] for n in names]
        slot = lambda g: g if g.ndim == 3 else g.reshape((N_CHIPS, g.shape[0] // N_CHIPS, g.shape[1]))
        kept = _swap_other_halves([slot(b16) for _, b16 in gs], axes, name="swap_other_halves_" + stage)
        sums = [_add_core_halves(slot(f32), l, where, ax) for (f32, _), l, ax in zip(gs, kept, axes)]
        ps = [s[1] for s in sums]
        ssem, rsem, ps, recv, started = _split_start(
            ps, [lax.empty((COPIES,) + p.shape[1:], p.dtype) for p in ps], [_owners_plan] * len(ps),
            name="owners_start_" + stage)
        in_flight[stage] = (ssem, rsem, ps, recv, [s[0] for s in sums])
        return started

    prm = {n: wts[n] for n in SMALL}
    loss, dx, grads = _local_grads(x.reshape(t, d), loss_target.reshape(t, d), prm, get_w, on_grads,
                                   nb=nb, seq=seq, dep=token)
    loss = lax.psum(loss[0, 0], MESH_AXES)

    def reduced(stage, after):
        ssem, rsem, ps, recv, own_sums = in_flight[stage]
        _, recv = _split_wait(ssem, rsem, ps, recv, [_owners_plan] * len(ps), after, name="owners_wait_" + stage)
        return {n: _add_chip_sums(f32_sum, r, where, HALF_AXIS[n])
                for n, f32_sum, r in zip(reduce_groups[stage], own_sums, recv)}

    def swap_start(names, name, dep=None):
        mine = [halves[n] for n in names]
        return _split_start(mine, [lax.empty(h.shape, h.dtype) for h in mine], [_sibling_plan] * len(mine), name=name,
                            dep=dep)

    halves = reduced("ffn", dx)
    ffn_swap = swap_start(reduce_groups["ffn"], "swap_reduced_start_ffn")
    halves.update(reduced("out", ffn_swap[4]))
    halves.update(reduced("in", ffn_swap[4]))
    mix_names = reduce_groups["out"] + reduce_groups["in"]
    mix_swap = swap_start(mix_names, "swap_reduced_start_mix", dep=ffn_swap[4])

    small_shapes = [grads[n].shape for n in SMALL]
    small_sum = _unpack(_all_reduce_small(_pack([grads[n] for n in SMALL])), small_shapes)
    small_grads = dict(zip(SMALL, small_sum))
    cwid, swid = D_CONV // N_CHIPS, D_XBC // N_CHIPS
    small_grads["conv_dw_w"] = lax.dynamic_slice(small_grads["conv_dw_w"], (0, chip * cwid), (CONV_K, cwid))
    small_grads["ssd_conv_w"] = lax.dynamic_slice(small_grads["ssd_conv_w"], (0, chip * swid), (SSD_CONV_K, swid))

    out_g, out_d, out_m, out_v = {}, {}, {}, {}
    shard_shapes = [wts[n].shape for n in SMALL]
    pd, pm, pv = _adamw(_pack([wts[n] for n in SMALL]), _pack([small_grads[n] for n in SMALL]),
                        _pack([ms[n] for n in SMALL]), _pack([vs[n] for n in SMALL]), name="adamw_small")
    for n, dd, mm, vv in zip(SMALL, _unpack(pd, shard_shapes), _unpack(pm, shard_shapes), _unpack(pv, shard_shapes)):
        out_g[n], out_d[n], out_m[n], out_v[n] = small_grads[n], dd, mm, vv

    def big_adamw(names, swap, after, name):
        mine, other = _split_wait(*swap[:4], [_sibling_plan] * len(names), after, name=name)
        for n, gm, go in zip(names, mine, other):
            out_g[n], out_d[n], out_m[n], out_v[n] = _adamw_halves(wts[n], gm, go, ms[n], vs[n], where, HALF_AXIS[n],
                                                                     name="adamw_" + n)

    big_adamw(reduce_groups["ffn"], ffn_swap, pd, "swap_reduced_wait_ffn")
    big_adamw(mix_names, mix_swap, out_d[reduce_groups["ffn"][-1]], "swap_reduced_wait_mix")

    back = lambda n, a: jnp.swapaxes(a[None], 1, 2) if n == "w_in" else a.reshape(args[n].shape)
    outs = [back(n, o[n]) for o in (out_g, out_d, out_m, out_v) for n in WEIGHTS]
    return (loss, dx.reshape(nb, seq, d), *outs)
```

```python
import functools

import jax
import jax.numpy as jnp
from jax import lax
from jax.experimental import pallas as pl
from jax.experimental.pallas import tpu as pltpu

F32 = jnp.float32
BF16 = jnp.bfloat16
EPS = 1e-6

D_MODEL = 2048
D_CONV = 1024
D_SSD = 1024
D_XBC = 2048
HEADS = 16
HEAD_DIM = 64
GROUPS = 4
STATE = 128
CONV_K = 31
SSD_CONV_K = 4
D_FF = 5632
D_MAIN = 2 * D_CONV + D_SSD + D_XBC
D_IN = D_MAIN + HEADS
N_CHIPS = 4
LANES = 128
CHUNK = 128
PAIRS = HEADS // 2

ADAM_LR = 0.001
ADAM_B1 = 0.9
ADAM_B2 = 0.999
ADAM_EPS = 1e-08
ADAM_WD = 0.01
ADAM_STEP = 10

MESH_AXES = ("x", "y", "c")
VMEM_LIMIT = 56 * 1024 * 1024


def _sig(v):
    return 1.0 / (1.0 + jnp.exp(-v))


def _cparams(sem, vmem=VMEM_LIMIT):
    return pltpu.CompilerParams(dimension_semantics=sem, vmem_limit_bytes=vmem)


_DIMS = {"nn": ((1,), (0,)), "nt": ((1,), (1,)), "tn": ((0,), (0,))}


def _matmul(pairs, *, mode, out_dtype, tm, tn, tk, name, slot_out=False, extent=None, out_rows=None, dep=None,
            also_bf16=False):
    a0, b0 = pairs[0]
    if mode == "nn":
        (m, k), n = a0.shape, b0.shape[1]
    elif mode == "nt":
        (m, k), n = a0.shape, b0.shape[0]
    else:
        (k, m), n = a0.shape, b0.shape[1]
    if extent is not None:
        m, n, k = extent
    tm, tn, tk = min(tm, m), min(tn, n), min(tk, k)
    assert m % tm == 0 and n % tn == 0 and k % tk == 0, (name, m, n, k, tm, tn, tk)
    nk = k // tk
    npairs = len(pairs)
    deps = [] if dep is None else [dep]
    dims = (_DIMS[mode], ((), ()))

    use_scratch = nk > 1 and out_dtype != F32

    def body(*refs):
        ins, o_ref = refs[: 2 * npairs], refs[2 * npairs + len(deps)]
        dot = lambda p: lax.dot_general(ins[2 * p][...], ins[2 * p + 1][...], dims, preferred_element_type=F32)
        if nk == 1:
            part = dot(0)
            for p in range(1, npairs):
                part = part + dot(p)
            o_ref[...] = part.astype(out_dtype)
            if also_bf16:
                refs[2 * npairs + len(deps) + 1][...] = part.astype(BF16)
            return
        acc = refs[-1] if use_scratch else o_ref
        kk = pl.program_id(2)

        @pl.when(kk == 0)
        def _():
            acc[...] = jnp.zeros_like(acc)

        for p in range(npairs):
            acc[...] += dot(p)

        if use_scratch:
            @pl.when(kk == nk - 1)
            def _():
                o_ref[...] = acc[...].astype(out_dtype)

        if also_bf16:
            @pl.when(kk == nk - 1)
            def _():
                refs[2 * npairs + len(deps) + 1][...] = acc[...].astype(BF16)

    if mode == "nn":
        a_spec = pl.BlockSpec((tm, tk), lambda i, j, kk: (i, kk))
        b_spec = pl.BlockSpec((tk, tn), lambda i, j, kk: (kk, j))
    elif mode == "nt":
        a_spec = pl.BlockSpec((tm, tk), lambda i, j, kk: (i, kk))
        b_spec = pl.BlockSpec((tn, tk), lambda i, j, kk: (j, kk))
    else:
        a_spec = pl.BlockSpec((tk, tm), lambda i, j, kk: (kk, i))
        b_spec = pl.BlockSpec((tk, tn), lambda i, j, kk: (kk, j))
    if slot_out:
        out_shape = jax.ShapeDtypeStruct((n // tn, m, tn), out_dtype)
        out_spec = pl.BlockSpec((None, tm, tn), lambda i, j, kk: (j, i, 0))
    else:
        out_shape = jax.ShapeDtypeStruct((m if out_rows is None else out_rows, n), out_dtype)
        out_spec = pl.BlockSpec((tm, tn), lambda i, j, kk: (i, j))
    flat = [t for ab in pairs for t in ab]
    if also_bf16:
        out_spec = [out_spec, out_spec]
        out_shape = [out_shape, jax.ShapeDtypeStruct(out_shape.shape, BF16)]
    return pl.pallas_call(
        body,
        grid=(m // tm, n // tn, nk),
        in_specs=[a_spec, b_spec] * npairs + [pl.BlockSpec(memory_space=pl.ANY)] * len(deps),
        out_specs=out_spec,
        out_shape=out_shape,
        scratch_shapes=[pltpu.VMEM((tm, tn), F32)] if use_scratch else [],
        compiler_params=_cparams(("parallel", "parallel", "arbitrary")),
        name=name,
    )(*flat, *deps)


SUB_ROWS = 256


def _ffn_up(h2, wg, wu, *, tm, tn):
    t, k = h2.shape
    n = wg.shape[1]
    tm = min(tm, t)
    assert t % tm == 0 and n % tn == 0, (t, n, tm, tn)

    sub = min(SUB_ROWS, tm)

    def body(h_ref, wg_ref, wu_ref, g_ref, u_ref, a_ref):
        for r in range(tm // sub):
            rows = pl.ds(r * sub, sub)
            hv = h_ref[rows, :]
            g = jnp.dot(hv, wg_ref[...], preferred_element_type=F32)
            u = jnp.dot(hv, wu_ref[...], preferred_element_type=F32)
            g_ref[rows, :] = g.astype(BF16)
            u_ref[rows, :] = u.astype(BF16)
            a_ref[rows, :] = (g * _sig(g) * u).astype(BF16)

    o = jax.ShapeDtypeStruct((t, n), BF16)
    ospec = pl.BlockSpec((tm, tn), lambda i, j: (i, j))
    return pl.pallas_call(
        body,
        grid=(t // tm, n // tn),
        in_specs=[pl.BlockSpec((tm, k), lambda i, j: (i, 0)), pl.BlockSpec((k, tn), lambda i, j: (0, j)),
                  pl.BlockSpec((k, tn), lambda i, j: (0, j))],
        out_specs=[ospec, ospec, ospec],
        out_shape=[o, o, o],
        compiler_params=_cparams(("parallel", "parallel")),
        name="ffn_up",
    )(h2, wg, wu)


def _ffn_bwd_act(df, wd, gt, up, *, tm, tn):
    t, k = df.shape
    n = wd.shape[0]
    tm = min(tm, t)
    assert t % tm == 0 and n % tn == 0, (t, n, tm, tn)

    sub = min(SUB_ROWS, tm)

    def body(df_ref, wd_ref, g_ref, u_ref, dg_ref, du_ref):
        for r in range(tm // sub):
            rows = pl.ds(r * sub, sub)
            da = lax.dot_general(df_ref[rows, :], wd_ref[...], (_DIMS["nt"], ((), ())), preferred_element_type=F32)
            g = g_ref[rows, :].astype(F32)
            u = u_ref[rows, :].astype(F32)
            s = _sig(g)
            dg_ref[rows, :] = (da * u * s * (1.0 + g * (1.0 - s))).astype(BF16)
            du_ref[rows, :] = (da * g * s).astype(BF16)

    o = jax.ShapeDtypeStruct((t, n), BF16)
    blk = pl.BlockSpec((tm, tn), lambda i, j: (i, j))
    return pl.pallas_call(
        body,
        grid=(t // tm, n // tn),
        in_specs=[pl.BlockSpec((tm, k), lambda i, j: (i, 0)), pl.BlockSpec((tn, k), lambda i, j: (j, 0)), blk, blk],
        out_specs=[blk, blk],
        out_shape=[o, o],
        compiler_params=_cparams(("parallel", "parallel")),
        name="ffn_bwd_act",
    )(df, wd, gt, up)


def _dw_dt_rows(dw_in_t, ddtr_b, h, *, tk=1024):
    t, d = h.shape
    tk = min(tk, t)
    nk = t // tk

    def body(buf_ref, d_ref, h_ref, o_ref, acc):
        kk = pl.program_id(0)

        @pl.when(kk == 0)
        def _():
            acc[...] = jnp.zeros_like(acc)

        acc[...] += lax.dot_general(d_ref[...], h_ref[...], (_DIMS["tn"], ((), ())), preferred_element_type=F32)

        @pl.when(kk == nk - 1)
        def _():
            o_ref[...] = acc[0:HEADS, :]

    return pl.pallas_call(
        body,
        grid=(nk,),
        in_specs=[DEP_SPEC, pl.BlockSpec((tk, LANES), lambda kk: (kk, 0)), pl.BlockSpec((tk, d), lambda kk: (kk, 0))],
        out_specs=pl.BlockSpec((HEADS, d), lambda kk: (D_MAIN // HEADS, 0)),
        out_shape=jax.ShapeDtypeStruct(dw_in_t.shape, F32),
        input_output_aliases={0: 0},
        scratch_shapes=[pltpu.VMEM((LANES, d), F32)],
        compiler_params=_cparams(("arbitrary",)),
        name="mm_dw_dt",
    )(dw_in_t, ddtr_b, h)


ROW_TILE = 256


DEP_SPEC = pl.BlockSpec(memory_space=pl.ANY)


def _rms_fwd(xv, g, *, dep=None, out_dtype, name):
    t, d = xv.shape
    deps = [] if dep is None else [dep]

    def body(*refs):
        x_ref, g_ref = refs[0], refs[1]
        o_ref = refs[-1]
        v = x_ref[...]
        r = lax.rsqrt(jnp.mean(v * v, axis=-1, keepdims=True) + EPS)
        o_ref[...] = (v * r * g_ref[...]).astype(out_dtype)

    row = pl.BlockSpec((ROW_TILE, d), lambda i: (i, 0))
    vec = pl.BlockSpec((1, d), lambda i: (0, 0))
    return pl.pallas_call(
        body,
        grid=(t // ROW_TILE,),
        in_specs=[row, vec] + [DEP_SPEC] * len(deps),
        out_specs=row,
        out_shape=jax.ShapeDtypeStruct((t, d), out_dtype),
        compiler_params=_cparams(("parallel",)),
        name=name,
    )(*([xv, g] + deps))


def _rms_bwd_rows(dy, v, gv):
    r = lax.rsqrt(jnp.mean(v * v, axis=-1, keepdims=True) + EPS)
    xh = v * r
    gdy = dy * gv
    dx = r * (gdy - xh * jnp.mean(gdy * xh, axis=-1, keepdims=True))
    return dx, jnp.sum(dy * xh, axis=0, keepdims=True)


FUSED_ROWS = 512


def _matmul_rows_tail(a, w, tail, *, tk, row_ins, vec_ins, row_outs, vec_outs, first=None, dep=None, name):
    t, kdim = a.shape
    d = w.shape[1]
    tm, tk = min(FUSED_ROWS, t), min(tk, kdim)
    nk, nb = kdim // tk, t // tm
    assert t % tm == 0 and kdim % tk == 0
    n_ri, n_vi, n_ro, n_vo = len(row_ins), len(vec_ins), len(row_outs), len(vec_outs)
    n_first = 0 if first is None else 2
    deps = [] if dep is None else [dep]

    def body(*refs):
        a_ref, w_ref = refs[0], refs[1]
        first_refs = refs[2:2 + n_first]
        p = 2 + n_first
        ri = refs[p:p + n_ri]
        vi = refs[p + n_ri:p + n_ri + n_vi]
        p += n_ri + n_vi + len(deps)
        ro = refs[p:p + n_ro]
        vo = refs[p + n_ro:p + n_ro + n_vo]
        acc = refs[-1]
        i, kk = pl.program_id(0), pl.program_id(1)

        @pl.when(jnp.logical_and(i == 0, kk == 0))
        def _():
            for ref in vo:
                ref[...] = jnp.zeros_like(ref)

        @pl.when(kk == 0)
        def _():
            if first is None:
                acc[...] = jnp.zeros_like(acc)
            else:
                acc[...] = jnp.dot(first_refs[0][...], first_refs[1][...], preferred_element_type=F32)

        acc[...] += jnp.dot(a_ref[...], w_ref[...], preferred_element_type=F32)

        @pl.when(kk == nk - 1)
        def _():
            outs, parts = tail(acc[...], [r[...] for r in ri], [v[...] for v in vi])
            for ref, val in zip(ro, outs):
                ref[...] = val.astype(ref.dtype)
            for ref, part in zip(vo, parts):
                ref[...] += part

    row = pl.BlockSpec((tm, d), lambda i, kk: (i, 0))
    const = lambda shape: pl.BlockSpec(shape, lambda i, kk: (0,) * len(shape))
    in_specs = [pl.BlockSpec((tm, tk), lambda i, kk: (i, kk)), pl.BlockSpec((tk, d), lambda i, kk: (kk, 0))]
    if first is not None:
        in_specs += [pl.BlockSpec((tm, first[0].shape[1]), lambda i, kk: (i, 0)), const(first[1].shape)]
    in_specs += [row] * n_ri + [const(v.shape) for v in vec_ins] + [pl.BlockSpec(memory_space=pl.ANY)] * len(deps)
    return pl.pallas_call(
        body,
        grid=(nb, nk),
        in_specs=in_specs,
        out_specs=[row] * n_ro + [const(sh) for sh in vec_outs],
        out_shape=[jax.ShapeDtypeStruct((t, d), dt) for dt in row_outs]
        + [jax.ShapeDtypeStruct(sh, F32) for sh in vec_outs],
        scratch_shapes=[pltpu.VMEM((tm, d), F32)],
        compiler_params=_cparams(("arbitrary", "arbitrary")),
        name=name,
    )(a, w, *([] if first is None else list(first)), *row_ins, *vec_ins, *deps)


def _down_loss(act, w_down, x1, tgt, g, *, tk):
    d = w_down.shape[1]

    def tail(v, rows, vecs):
        x1v, tv = rows
        gv, = vecs
        fh = v * lax.rsqrt(jnp.mean(v * v, axis=-1, keepdims=True) + EPS)
        e = x1v + fh * gv - tv
        dx2 = e * (1.0 / d)
        df, dg = _rms_bwd_rows(dx2, v, gv)
        loss = 0.5 * jnp.sum(jnp.mean(e * e, axis=-1, keepdims=True), axis=0, keepdims=True)
        return (dx2, df), (loss, dg)

    return _matmul_rows_tail(act, w_down, tail, tk=tk, row_ins=[x1, tgt], vec_ins=[g], row_outs=[F32, BF16],
                             vec_outs=[(1, 1), (1, d)], name="mm_down_loss")


def _dh_dx(dproj, w_in_t, ddtr_b, w_dt_t, xv, dx1, g, *, tk, dep):
    def tail(v, rows, vecs):
        xr, dx1r = rows
        dx, dg = _rms_bwd_rows(v, xr, vecs[0])
        return (dx + dx1r,), (dg,)

    return _matmul_rows_tail(dproj, w_in_t, tail, tk=tk, row_ins=[xv, dx1], vec_ins=[g], row_outs=[F32],
                             vec_outs=[(1, xv.shape[1])], first=(ddtr_b, w_dt_t), dep=dep, name="mm_dh_dx")


def _rms_post_pre(mix, xv, g_post, g_pre):
    t, d = mix.shape

    def body(m_ref, x_ref, gp_ref, gf_ref, x1_ref, h2_ref):
        v = m_ref[...]
        x1 = x_ref[...] + v * lax.rsqrt(jnp.mean(v * v, axis=-1, keepdims=True) + EPS) * gp_ref[...]
        x1_ref[...] = x1
        h2_ref[...] = (x1 * lax.rsqrt(jnp.mean(x1 * x1, axis=-1, keepdims=True) + EPS) * gf_ref[...]).astype(BF16)

    row = pl.BlockSpec((ROW_TILE, d), lambda i: (i, 0))
    vec = pl.BlockSpec((1, d), lambda i: (0, 0))
    return pl.pallas_call(
        body,
        grid=(t // ROW_TILE,),
        in_specs=[row, row, vec, vec],
        out_specs=[row, row],
        out_shape=[jax.ShapeDtypeStruct((t, d), F32), jax.ShapeDtypeStruct((t, d), BF16)],
        compiler_params=_cparams(("parallel",)),
        name="rms_mix_post_ffn_pre",
    )(mix, xv, g_post, g_pre)


def _rms_bwd_pre_post(dh2, x1, g_pre, dx2, mix, g_post, *, dep=None):
    t, d = x1.shape

    deps = [] if dep is None else [dep]

    def body(dh_ref, x1_ref, gf_ref, dx2_ref, m_ref, gp_ref, *rest):
        dx1_ref, dmix_ref, dgf_ref, dgp_ref = rest[len(deps):]

        @pl.when(pl.program_id(0) == 0)
        def _():
            dgf_ref[...] = jnp.zeros_like(dgf_ref)
            dgp_ref[...] = jnp.zeros_like(dgp_ref)

        dx, dgf = _rms_bwd_rows(dh_ref[...], x1_ref[...], gf_ref[...])
        dx1 = dx + dx2_ref[...]
        dx1_ref[...] = dx1
        dmix, dgp = _rms_bwd_rows(dx1, m_ref[...], gp_ref[...])
        dmix_ref[...] = dmix.astype(BF16)
        dgf_ref[...] += dgf
        dgp_ref[...] += dgp

    row = pl.BlockSpec((ROW_TILE, d), lambda i: (i, 0))
    vec = pl.BlockSpec((1, d), lambda i: (0, 0))
    return pl.pallas_call(
        body,
        grid=(t // ROW_TILE,),
        in_specs=[row, row, vec, row, row, vec] + [DEP_SPEC] * len(deps),
        out_specs=[row, row, vec, vec],
        out_shape=[jax.ShapeDtypeStruct((t, d), F32), jax.ShapeDtypeStruct((t, d), BF16),
                   jax.ShapeDtypeStruct((1, d), F32), jax.ShapeDtypeStruct((1, d), F32)],
        compiler_params=_cparams(("arbitrary",)),
        name="rms_ffn_pre_mix_post_bwd",
    )(dh2, x1, g_pre, dx2, mix, g_post, *deps)


CONV_ROWS = 256
TAP_ROWS = 64
HALO31 = 32
HALO4 = 8


def _sum8(v):
    return jnp.sum(v.reshape(v.shape[0] // 8, 8, v.shape[1]), axis=0)


SUBLANES = 8
PHASE_SPAN = (CONV_K - 1) // SUBLANES * SUBLANES


def _phase_scratch(ts):
    return pltpu.VMEM((SUBLANES, ts + PHASE_SPAN, LANES), F32)


def _phase_copies(ph, buf, ln, base, ts):
    for s in range(SUBLANES):
        n = ts + (CONV_K - 1 - s) // SUBLANES * SUBLANES
        ph[s, 0:n, :] = buf[pl.ds(base + s, n), ln]


def _tap_rows(ph, off, r0):
    s = off % SUBLANES
    return ph[s, pl.ds(r0 + off - s, TAP_ROWS), :]


def _conv_branch_fwd(proj, cw, cb, lg, lb, *, nb, seq):
    ts, c, halo = CONV_ROWS, D_CONV, HALO31
    ns = seq // ts
    base = halo - CONV_K + 1

    def body(ca_ref, cg_ref, w_ref, b_ref, lg_ref, lb_ref, u1_ref, u_ref, ubuf, uph):
        i = pl.program_id(1)

        @pl.when(i == 0)
        def _():
            ubuf[0:halo, :] = jnp.zeros((halo, c), F32)

        @pl.when(i > 0)
        def _():
            ubuf[0:halo, :] = ubuf[ts:ts + halo, :]

        ubuf[halo:halo + ts, :] = ca_ref[...] * _sig(cg_ref[...])

        def lane_tile(j, carry):
            ln = pl.ds(pl.multiple_of(j * LANES, LANES), LANES)
            _phase_copies(uph, ubuf, ln, base, ts)
            for r in range(ts // TAP_ROWS):
                acc = jnp.broadcast_to(b_ref[:, ln], (TAP_ROWS, LANES))
                for k in range(CONV_K):
                    acc = acc + w_ref[pl.ds(k, 1), ln] * _tap_rows(uph, k, r * TAP_ROWS)
                u1_ref[pl.ds(r * TAP_ROWS, TAP_ROWS), ln] = acc
            return carry

        lax.fori_loop(0, c // LANES, lane_tile, 0)
        v = u1_ref[...]
        mu = jnp.mean(v, axis=-1, keepdims=True)
        dv = v - mu
        xh = dv * lax.rsqrt(jnp.mean(dv * dv, axis=-1, keepdims=True) + EPS)
        u2 = xh * lg_ref[...] + lb_ref[...]
        u_ref[...] = (u2 * _sig(u2)).astype(BF16)

    t = nb * seq
    row = lambda col: pl.BlockSpec((ts, c), lambda b, i: (b * ns + i, col))
    vec = pl.BlockSpec((1, c), lambda b, i: (0, 0))
    return pl.pallas_call(
        body,
        grid=(nb, ns),
        in_specs=[row(0), row(1), pl.BlockSpec((32, c), lambda b, i: (0, 0)), vec, vec, vec],
        out_specs=[row(0), row(0)],
        out_shape=[jax.ShapeDtypeStruct((t, c), F32), jax.ShapeDtypeStruct((t, c + D_SSD), BF16)],
        scratch_shapes=[pltpu.VMEM((halo + ts, c), F32), _phase_scratch(ts)],
        compiler_params=_cparams(("parallel", "arbitrary")),
        name="conv_branch_fwd",
    )(proj, proj, cw, cb, lg, lb)


def _conv_branch_bwd(duy, u1, proj, cw, lg, lb, *, nb, seq):
    ts, c, halo = CONV_ROWS, D_CONV, HALO31
    ns = seq // ts
    base = halo - CONV_K + 1
    hb = ts // halo

    def body(du_ref, u1_ref, ca_ref, cg_ref, cah_ref, cgh_ref, w_ref, lg_ref, lb_ref,
             dcacg_ref, dw_ref, db_ref, dlg_ref, dlb_ref,
             ubuf, dbuf, du0buf, dwacc, dbacc, dlgacc, dlbacc, uph, dph):
        b, i = pl.program_id(0), pl.program_id(1)
        rc = ns - 1 - i

        @pl.when(jnp.logical_and(b == 0, i == 0))
        def _():
            dwacc[...] = jnp.zeros_like(dwacc)
            dbacc[...] = jnp.zeros_like(dbacc)
            dlgacc[...] = jnp.zeros_like(dlgacc)
            dlbacc[...] = jnp.zeros_like(dlbacc)

        @pl.when(i == 0)
        def _():
            dbuf[ts:ts + halo, :] = jnp.zeros((halo, c), F32)

        @pl.when(i > 0)
        def _():
            dbuf[ts:ts + halo, :] = dbuf[0:halo, :]

        v = u1_ref[...]
        mu = jnp.mean(v, axis=-1, keepdims=True)
        dv = v - mu
        rstd = lax.rsqrt(jnp.mean(dv * dv, axis=-1, keepdims=True) + EPS)
        xh = dv * rstd
        lgv = lg_ref[...]
        u2 = xh * lgv + lb_ref[...]
        s2 = _sig(u2)
        du2 = du_ref[...] * (s2 * (1.0 + u2 * (1.0 - s2)))
        dlgacc[...] += jnp.sum(du2 * xh, axis=0, keepdims=True)
        dlbacc[...] += jnp.sum(du2, axis=0, keepdims=True)
        gd = du2 * lgv
        du1 = rstd * (gd - jnp.mean(gd, axis=-1, keepdims=True) - xh * jnp.mean(gd * xh, axis=-1, keepdims=True))
        dbacc[...] += jnp.sum(du1, axis=0, keepdims=True)
        dbuf[0:ts, :] = du1

        @pl.when(rc == 0)
        def _():
            ubuf[0:halo, :] = jnp.zeros((halo, c), F32)

        @pl.when(rc > 0)
        def _():
            ubuf[0:halo, :] = cah_ref[...] * _sig(cgh_ref[...])

        cav = ca_ref[...]
        sg = _sig(cg_ref[...])
        ubuf[halo:halo + ts, :] = cav * sg

        def lane_tile(j, carry):
            ln = pl.ds(pl.multiple_of(j * LANES, LANES), LANES)
            _phase_copies(uph, ubuf, ln, base, ts)
            _phase_copies(dph, dbuf, ln, 0, ts)
            for r in range(ts // TAP_ROWS):
                r0 = r * TAP_ROWS
                d1 = dbuf[pl.ds(r0, TAP_ROWS), ln]
                acc = jnp.zeros((TAP_ROWS, LANES), F32)
                for k in range(CONV_K):
                    acc = acc + w_ref[pl.ds(k, 1), ln] * _tap_rows(dph, CONV_K - 1 - k, r0)
                    dwacc[pl.ds(k * 8, 8), ln] += _sum8(d1 * _tap_rows(uph, k, r0))
                du0buf[pl.ds(r0, TAP_ROWS), ln] = acc
            return carry

        lax.fori_loop(0, c // LANES, lane_tile, 0)
        du0 = du0buf[...]
        dcacg_ref[:, 0:c] = (du0 * sg).astype(BF16)
        dcacg_ref[:, c:2 * c] = (du0 * cav * sg * (1.0 - sg)).astype(BF16)

        @pl.when(jnp.logical_and(b == nb - 1, i == ns - 1))
        def _():
            for k in range(CONV_K):
                dw_ref[pl.ds(k, 1), :] = jnp.sum(dwacc[pl.ds(k * 8, 8), :], axis=0, keepdims=True)
            dw_ref[pl.ds(CONV_K, 1), :] = jnp.zeros((1, c), F32)
            db_ref[...] = dbacc[...]
            dlg_ref[...] = dlgacc[...]
            dlb_ref[...] = dlbacc[...]

    t = nb * seq
    rowblk = lambda b, i: b * ns + (ns - 1 - i)
    row = lambda col: pl.BlockSpec((ts, c), lambda b, i: (rowblk(b, i), col))
    hrow = lambda col: pl.BlockSpec((halo, c), lambda b, i: (jnp.maximum(rowblk(b, i) * hb - 1, 0), col))
    vec = pl.BlockSpec((1, c), lambda b, i: (0, 0))
    wspec = pl.BlockSpec((32, c), lambda b, i: (0, 0))
    return pl.pallas_call(
        body,
        grid=(nb, ns),
        in_specs=[row(0), row(0), row(0), row(1), hrow(0), hrow(1), wspec, vec, vec],
        out_specs=[pl.BlockSpec((ts, 2 * c), lambda b, i: (rowblk(b, i), 0)), wspec, vec, vec, vec],
        out_shape=[jax.ShapeDtypeStruct((t, D_MAIN), BF16), jax.ShapeDtypeStruct((32, c), F32),
                   jax.ShapeDtypeStruct((1, c), F32), jax.ShapeDtypeStruct((1, c), F32), jax.ShapeDtypeStruct((1, c), F32)],
        scratch_shapes=[pltpu.VMEM((halo + ts, c), F32), pltpu.VMEM((ts + halo, c), F32), pltpu.VMEM((ts, c), F32),
                        pltpu.VMEM((CONV_K * 8, c), F32), pltpu.VMEM((1, c), F32), pltpu.VMEM((1, c), F32),
                        pltpu.VMEM((1, c), F32), _phase_scratch(ts), _phase_scratch(ts)],
        compiler_params=_cparams(("arbitrary", "arbitrary")),
        name="conv_branch_bwd",
    )(duy, u1, proj, proj, proj, proj, cw, lg, lb)


XBC_COL0 = (2 * D_CONV + D_SSD) // 1024


def _ssd_pre_fwd(proj, sw, sb, *, nb, seq):
    ts, c, halo = CONV_ROWS, 1024, HALO4
    ns = seq // ts
    base = halo - SSD_CONV_K + 1

    def body(x_ref, w_ref, b_ref, o_ref, xbuf):
        i = pl.program_id(2)

        @pl.when(i == 0)
        def _():
            xbuf[0:halo, :] = jnp.zeros((halo, c), F32)

        @pl.when(i > 0)
        def _():
            xbuf[0:halo, :] = xbuf[ts:ts + halo, :]

        xbuf[halo:halo + ts, :] = x_ref[...]

        def lane_tile(j, carry):
            ln = pl.ds(pl.multiple_of(j * LANES, LANES), LANES)
            for r in range(ts // TAP_ROWS):
                acc = jnp.broadcast_to(b_ref[:, ln], (TAP_ROWS, LANES))
                for k in range(SSD_CONV_K):
                    acc = acc + w_ref[pl.ds(k, 1), ln] * xbuf[pl.ds(r * TAP_ROWS + base + k, TAP_ROWS), ln]
                o_ref[pl.ds(r * TAP_ROWS, TAP_ROWS), ln] = acc * _sig(acc)
            return carry

        lax.fori_loop(0, c // LANES, lane_tile, 0)

    t = nb * seq
    return pl.pallas_call(
        body,
        grid=(2, nb, ns),
        in_specs=[pl.BlockSpec((ts, c), lambda j, b, i: (b * ns + i, XBC_COL0 + j)),
                  pl.BlockSpec((8, c), lambda j, b, i: (0, j)), pl.BlockSpec((1, c), lambda j, b, i: (0, j))],
        out_specs=pl.BlockSpec((ts, c), lambda j, b, i: (b * ns + i, j)),
        out_shape=jax.ShapeDtypeStruct((t, D_XBC), F32),
        scratch_shapes=[pltpu.VMEM((halo + ts, c), F32)],
        compiler_params=_cparams(("parallel", "parallel", "arbitrary")),
        name="ssd_pre_fwd",
    )(proj, sw, sb)


def _ssd_pre_bwd(dproj, dxs, proj, sw, sb, *, nb, seq):
    ts, c, halo = CONV_ROWS, 1024, HALO4
    ns = seq // ts
    base = halo - SSD_CONV_K + 1
    hb = ts // halo

    def body(dproj_ref, d_ref, x_ref, xh_ref, w_ref, b_ref, dx_ref, dw_ref, db_ref, xbuf, dbuf, dwacc, dbacc):
        b, i = pl.program_id(1), pl.program_id(2)
        rc = ns - 1 - i

        @pl.when(jnp.logical_and(b == 0, i == 0))
        def _():
            dwacc[...] = jnp.zeros_like(dwacc)
            dbacc[...] = jnp.zeros_like(dbacc)

        @pl.when(i == 0)
        def _():
            dbuf[ts:ts + halo, :] = jnp.zeros((halo, c), F32)

        @pl.when(i > 0)
        def _():
            dbuf[ts:ts + halo, :] = dbuf[0:halo, :]

        @pl.when(rc == 0)
        def _():
            xbuf[0:halo, :] = jnp.zeros((halo, c), F32)

        @pl.when(rc > 0)
        def _():
            xbuf[0:halo, :] = xh_ref[...]

        xbuf[halo:halo + ts, :] = x_ref[...]

        def pre_tile(j, carry):
            ln = pl.ds(pl.multiple_of(j * LANES, LANES), LANES)
            for r in range(ts // TAP_ROWS):
                r0 = r * TAP_ROWS
                acc = jnp.broadcast_to(b_ref[:, ln], (TAP_ROWS, LANES))
                for k in range(SSD_CONV_K):
                    acc = acc + w_ref[pl.ds(k, 1), ln] * xbuf[pl.ds(r0 + base + k, TAP_ROWS), ln]
                s = _sig(acc)
                dc = d_ref[pl.ds(r0, TAP_ROWS), ln] * (s * (1.0 + acc * (1.0 - s)))
                dbuf[pl.ds(r0, TAP_ROWS), ln] = dc
                dbacc[:, ln] += _sum8(dc)
            return carry

        lax.fori_loop(0, c // LANES, pre_tile, 0)

        def lane_tile(j, carry):
            ln = pl.ds(pl.multiple_of(j * LANES, LANES), LANES)
            for r in range(ts // TAP_ROWS):
                r0 = r * TAP_ROWS
                d1 = dbuf[pl.ds(r0, TAP_ROWS), ln]
                acc = jnp.zeros((TAP_ROWS, LANES), F32)
                for k in range(SSD_CONV_K):
                    acc = acc + w_ref[pl.ds(k, 1), ln] * dbuf[pl.ds(r0 + SSD_CONV_K - 1 - k, TAP_ROWS), ln]
                    dwacc[pl.ds(k * 8, 8), ln] += _sum8(d1 * xbuf[pl.ds(r0 + base + k, TAP_ROWS), ln])
                dx_ref[pl.ds(r0, TAP_ROWS), ln] = acc.astype(BF16)
            return carry

        lax.fori_loop(0, c // LANES, lane_tile, 0)

        @pl.when(jnp.logical_and(b == nb - 1, i == ns - 1))
        def _():
            for k in range(SSD_CONV_K):
                dw_ref[pl.ds(k, 1), :] = jnp.sum(dwacc[pl.ds(k * 8, 8), :], axis=0, keepdims=True)
            dw_ref[pl.ds(SSD_CONV_K, 8 - SSD_CONV_K), :] = jnp.zeros((8 - SSD_CONV_K, c), F32)
            db_ref[...] = jnp.sum(dbacc[...], axis=0, keepdims=True)

    t = nb * seq
    rowblk = lambda b, i: b * ns + (ns - 1 - i)
    return pl.pallas_call(
        body,
        grid=(2, nb, ns),
        in_specs=[DEP_SPEC, pl.BlockSpec((ts, c), lambda j, b, i: (rowblk(b, i), j)),
                  pl.BlockSpec((ts, c), lambda j, b, i: (rowblk(b, i), XBC_COL0 + j)),
                  pl.BlockSpec((halo, c), lambda j, b, i: (jnp.maximum(rowblk(b, i) * hb - 1, 0), XBC_COL0 + j)),
                  pl.BlockSpec((8, c), lambda j, b, i: (0, j)), pl.BlockSpec((1, c), lambda j, b, i: (0, j))],
        out_specs=[pl.BlockSpec((ts, c), lambda j, b, i: (rowblk(b, i), XBC_COL0 + j)),
                   pl.BlockSpec((8, c), lambda j, b, i: (0, j)), pl.BlockSpec((1, c), lambda j, b, i: (0, j))],
        out_shape=[jax.ShapeDtypeStruct(dproj.shape, BF16), jax.ShapeDtypeStruct((8, D_XBC), F32),
                   jax.ShapeDtypeStruct((1, D_XBC), F32)],
        input_output_aliases={0: 0},
        scratch_shapes=[pltpu.VMEM((halo + ts, c), F32), pltpu.VMEM((ts + halo, c), F32),
                        pltpu.VMEM((SSD_CONV_K * 8, c), F32), pltpu.VMEM((8, c), F32)],
        compiler_params=_cparams(("arbitrary", "arbitrary", "arbitrary")),
        name="ssd_pre_bwd",
    )(dproj, dxs, proj, proj, sw, sb)


Z_COL = (2 * D_CONV) // 1024
GROUP_W = D_SSD // GROUPS


def _softplus(v):
    return jnp.maximum(v, 0.0) + jnp.log(1.0 + jnp.exp(-jnp.abs(v)))


def _dot(a, b):
    return jnp.dot(a, b, preferred_element_type=F32)


def _dot_nt(a, b):
    return lax.dot_general(a, b, (_DIMS["nt"], ((), ())), preferred_element_type=F32)


def _dot_tn(a, b):
    return lax.dot_general(a, b, (_DIMS["tn"], ((), ())), preferred_element_type=F32)


def _bf16_terms(v):
    hi = v.astype(BF16)
    r1 = v - hi.astype(F32)
    mid = r1.astype(BF16)
    return hi, mid, (r1 - mid.astype(F32)).astype(BF16)


def _dot_exact_left(sel, v):
    hi, mid, lo = _bf16_terms(v)
    return _dot(sel, hi) + (_dot(sel, mid) + _dot(sel, lo))


def _dot_exact_right(v, sel):
    hi, mid, lo = _bf16_terms(v)
    return _dot(hi, sel) + (_dot(mid, sel) + _dot(lo, sel))


def _chunk_decays(dtr_ref, bias_ref, alog_ref):
    q = CHUNK
    ii = lax.broadcasted_iota(jnp.int32, (q, q), 0)
    jj = lax.broadcasted_iota(jnp.int32, (q, q), 1)
    tri = jj <= ii
    dt = _softplus(dtr_ref[...] + bias_ref[...])
    a_head = -jnp.exp(alog_ref[...])
    cs = _dot_exact_left(tri.astype(BF16), dt * a_head)
    return tri, dt, a_head, cs, cs.T


def _ssd_fwd(uy, xs_all, proj, dtr, dt_bias, a_log, d_lanes, norm_w, *, nb, seq):
    q = CHUNK
    nc = seq // q
    t = nb * seq

    def body(uy_ref, xs_ref, bm_ref, cm_ref, z_ref, dtr_ref, bias_ref, alog_ref, dl_ref, nw_ref,
             y_ref, ys_ref, st_ref, state):
        @pl.when(pl.program_id(1) == 0)
        def _():
            state[...] = jnp.zeros_like(state)

        tri, dt, _, cs, cst = _chunk_decays(dtr_ref, bias_ref, alog_ref)
        first = lax.broadcasted_iota(jnp.int32, (1, LANES), 1) < HEAD_DIM
        for g in range(GROUPS):
            gl = slice(g * STATE, (g + 1) * STATE)
            bb = bm_ref[:, gl].astype(BF16)
            cb = cm_ref[:, gl].astype(BF16)
            scores = _dot_nt(cb, bb)
            for p in range(2):
                pr = 2 * g + p
                h0 = 2 * pr
                sl = slice(pr * LANES, (pr + 1) * LANES)
                xv = xs_ref[:, sl]
                dtp = jnp.where(first, dt[:, h0:h0 + 1], dt[:, h0 + 1:h0 + 2])
                csp = jnp.where(first, cs[:, h0:h0 + 1], cs[:, h0 + 1:h0 + 2])
                xd = xv * dtp
                yv = None
                for hh, keep in ((h0, first), (h0 + 1, jnp.logical_not(first))):
                    decay = jnp.where(tri, jnp.exp(cs[:, hh:hh + 1] - cst[hh:hh + 1, :]), 0.0)
                    part = _dot((scores * decay).astype(BF16), jnp.where(keep, xd, 0.0).astype(BF16))
                    yv = part if yv is None else yv + part
                hp = state[pr]
                st_ref[0, pr] = hp
                yv = yv + jnp.exp(csp) * _dot(cb, hp.astype(BF16))
                last = csp[q - 1:q, :]
                state[pr] = jnp.exp(last) * hp + _dot_tn(bb, (xd * jnp.exp(last - csp)).astype(BF16))
                ys_ref[:, sl] = yv + dl_ref[:, sl] * xv
        zv = z_ref[...]
        gated = ys_ref[...] * (zv * _sig(zv))
        for g in range(GROUPS):
            gl = slice(g * GROUP_W, (g + 1) * GROUP_W)
            v = gated[:, gl]
            r = lax.rsqrt(jnp.mean(v * v, axis=-1, keepdims=True) + EPS)
            y_ref[:, gl] = (v * r * nw_ref[:, gl]).astype(BF16)

    blk = lambda w, col: pl.BlockSpec((q, w), lambda b, c: (b * nc + c, col))
    vec = lambda w: pl.BlockSpec((1, w), lambda b, c: (0, 0))
    return pl.pallas_call(
        body,
        grid=(nb, nc),
        in_specs=[DEP_SPEC, blk(D_SSD, 0), blk(GROUPS * STATE, 2), blk(GROUPS * STATE, 3), blk(D_SSD, Z_COL),
                  blk(LANES, 0), vec(LANES), vec(LANES), vec(D_SSD), vec(D_SSD)],
        out_specs=[blk(D_SSD, 1), blk(D_SSD, 0),
                   pl.BlockSpec((1, PAIRS, STATE, LANES), lambda b, c: (b * nc + c, 0, 0, 0))],
        out_shape=[jax.ShapeDtypeStruct(uy.shape, BF16), jax.ShapeDtypeStruct((t, D_SSD), F32),
                   jax.ShapeDtypeStruct((nb * nc, PAIRS, STATE, LANES), F32)],
        input_output_aliases={0: 0},
        scratch_shapes=[pltpu.VMEM((PAIRS, STATE, LANES), F32)],
        compiler_params=_cparams(("parallel", "arbitrary")),
        name="ssd_fwd",
    )(uy, xs_all, xs_all, xs_all, proj, dtr, dt_bias, a_log, d_lanes, norm_w)


def _ssd_bwd(dproj, duy, proj, ys, xs_all, dtr, states, dt_bias, a_log, d_lanes, norm_w, *, nb, seq):
    q = CHUNK
    nc = seq // q
    t = nb * seq
    head_of_lane = (jnp.arange(D_SSD)[:, None] // HEAD_DIM == jnp.arange(LANES)[None, :]).astype(BF16)

    def body(dproj_ref, dy_ref, z_ref, ys_ref, xs_ref, bm_ref, cm_ref, dtr_ref, st_ref, bias_ref, alog_ref, dl_ref,
             nw_ref, sel_ref, dz_ref, dx_ref, ddtr_ref, small_ref,
             dstate, dys_buf, dcsl, ddtl, dcst, dnw_acc, dd_acc, dbias_acc, da_acc):
        b, c = pl.program_id(0), pl.program_id(1)

        @pl.when(jnp.logical_and(b == 0, c == 0))
        def _():
            dnw_acc[...] = jnp.zeros_like(dnw_acc)
            dd_acc[...] = jnp.zeros_like(dd_acc)
            dbias_acc[...] = jnp.zeros_like(dbias_acc)
            da_acc[...] = jnp.zeros_like(da_acc)
            dcst[...] = jnp.zeros_like(dcst)

        @pl.when(c == 0)
        def _():
            dstate[...] = jnp.zeros_like(dstate)

        zv = z_ref[...]
        sz = _sig(zv)
        silz = zv * sz
        ysv = ys_ref[...]
        gated = ysv * silz
        dyv = dy_ref[...]
        nwv = nw_ref[...]
        for g in range(GROUPS):
            gl = slice(g * GROUP_W, (g + 1) * GROUP_W)
            v = gated[:, gl]
            r = lax.rsqrt(jnp.mean(v * v, axis=-1, keepdims=True) + EPS)
            yn = v * r
            dyn = dyv[:, gl] * nwv[:, gl]
            dnw_acc[:, gl] += jnp.sum(dyv[:, gl] * yn, axis=0, keepdims=True)
            dys_buf[:, gl] = r * (dyn - yn * jnp.mean(dyn * yn, axis=-1, keepdims=True))
        dgated = dys_buf[...]
        dz_ref[...] = (dgated * ysv * (sz * (1.0 + zv * (1.0 - sz)))).astype(BF16)
        dys_all = dgated * silz
        dys_buf[...] = dys_all
        dd_acc[...] += jnp.sum(dys_all * xs_ref[...], axis=0, keepdims=True)

        tri, dt, a_head, cs, cst = _chunk_decays(dtr_ref, bias_ref, alog_ref)
        lane = lax.broadcasted_iota(jnp.int32, (1, LANES), 1)
        first = lane < HEAD_DIM
        dcs_h = jnp.zeros((q, LANES), F32)
        for g in range(GROUPS):
            gl = slice(g * STATE, (g + 1) * STATE)
            bb = bm_ref[:, gl].astype(BF16)
            cb = cm_ref[:, gl].astype(BF16)
            scores = _dot_nt(cb, bb)
            dscores = jnp.zeros((q, q), F32)
            dbg = jnp.zeros((q, STATE), F32)
            dcg = jnp.zeros((q, STATE), F32)
            for p in range(2):
                pr = 2 * g + p
                h0 = 2 * pr
                sl = slice(pr * LANES, (pr + 1) * LANES)
                xv = xs_ref[:, sl]
                dyp = dys_buf[:, sl]
                dtp = jnp.where(first, dt[:, h0:h0 + 1], dt[:, h0 + 1:h0 + 2])
                csp = jnp.where(first, cs[:, h0:h0 + 1], cs[:, h0 + 1:h0 + 2])
                xd = xv * dtp
                xdb = xd.astype(BF16)
                hp = st_ref[0, pr]
                dhn = dstate[pr]
                hpb = hp.astype(BF16)
                dhnb = dhn.astype(BF16)
                lam = jnp.exp(csp)
                last = csp[q - 1:q, :]
                gam = jnp.exp(last)
                w = jnp.exp(last - csp)
                dxd = jnp.zeros((q, LANES), F32)
                for hh, keep in ((h0, first), (h0 + 1, jnp.logical_not(first))):
                    decay = jnp.where(tri, jnp.exp(cs[:, hh:hh + 1] - cst[hh:hh + 1, :]), 0.0)
                    m = scores * decay
                    dym = jnp.where(keep, dyp, 0.0).astype(BF16)
                    dm = _dot_nt(dym, xdb)
                    dxd = dxd + _dot_tn(m.astype(BF16), dym)
                    e = dm * m
                    dcs_h = dcs_h + jnp.where(lane == hh, jnp.sum(e, axis=1, keepdims=True), 0.0)
                    dcst[hh:hh + 1, :] = jnp.sum(e, axis=0, keepdims=True)
                    dscores = dscores + dm * decay
                yoff = lam * _dot(cb, hpb)
                ldy = (lam * dyp).astype(BF16)
                dcg = dcg + _dot_nt(ldy, hpb)
                dstate[pr] = gam * dhn + _dot_tn(cb, ldy)
                bdh = _dot(bb, dhnb)
                dxd = dxd + w * bdh
                xdw = xd * w
                dbg = dbg + _dot_nt(xdw.astype(BF16), dhnb)
                wd = xdw * bdh
                dcsl[:, sl] = dyp * yoff - wd
                dcsl[q - 1:q, sl] += (jnp.sum(wd, axis=0, keepdims=True)
                                      + gam * jnp.sum(dhn * hp, axis=0, keepdims=True))
                dx_ref[:, sl] = dxd * dtp + dyp * dl_ref[:, sl]
                ddtl[:, sl] = dxd * xv
            dsb = dscores.astype(BF16)
            dx_ref[:, D_SSD + g * STATE:D_SSD + (g + 1) * STATE] = dbg + _dot_tn(dsb, cb)
            dx_ref[:, D_SSD + (GROUPS + g) * STATE:D_SSD + (GROUPS + g + 1) * STATE] = dcg + _dot(dsb, bb)

        sel = sel_ref[...]
        dcs_h = dcs_h + _dot_exact_right(dcsl[...], sel) - dcst[...].T
        ddt = _dot_exact_right(ddtl[...], sel)
        upper = lax.broadcasted_iota(jnp.int32, (q, q), 1) >= lax.broadcasted_iota(jnp.int32, (q, q), 0)
        da = _dot_exact_left(upper.astype(BF16), dcs_h)
        ddt = ddt + da * a_head
        da_acc[...] += jnp.sum(da * dt, axis=0, keepdims=True)
        ddtr = ddt * _sig(dtr_ref[...] + bias_ref[...])
        ddtr_ref[...] = ddtr
        dbias_acc[...] += jnp.sum(ddtr, axis=0, keepdims=True)

        @pl.when(jnp.logical_and(b == nb - 1, c == nc - 1))
        def _():
            small_ref[...] = jnp.zeros_like(small_ref)
            small_ref[0:1, :] = dnw_acc[...]
            small_ref[1:2, 0:LANES] = _dot_exact_right(jnp.broadcast_to(dd_acc[...], (8, D_SSD)), sel)[0:1, :]
            small_ref[2:3, 0:LANES] = dbias_acc[...]
            small_ref[3:4, 0:LANES] = da_acc[...] * a_head

    rowblk = lambda b, c: b * nc + (nc - 1 - c)
    blk = lambda w, col: pl.BlockSpec((q, w), lambda b, c: (rowblk(b, c), col))
    vec = lambda w: pl.BlockSpec((1, w), lambda b, c: (0, 0))
    return pl.pallas_call(
        body,
        grid=(nb, nc),
        in_specs=[DEP_SPEC, blk(D_SSD, 1), blk(D_SSD, Z_COL), blk(D_SSD, 0), blk(D_SSD, 0), blk(GROUPS * STATE, 2),
                  blk(GROUPS * STATE, 3), blk(LANES, 0),
                  pl.BlockSpec((1, PAIRS, STATE, LANES), lambda b, c: (rowblk(b, c), 0, 0, 0)),
                  vec(LANES), vec(LANES), vec(D_SSD), vec(D_SSD), pl.BlockSpec((D_SSD, LANES), lambda b, c: (0, 0))],
        out_specs=[blk(D_SSD, Z_COL), blk(D_XBC, 0), blk(LANES, 0), pl.BlockSpec((8, D_SSD), lambda b, c: (0, 0))],
        out_shape=[jax.ShapeDtypeStruct(dproj.shape, BF16), jax.ShapeDtypeStruct((t, D_XBC), F32),
                   jax.ShapeDtypeStruct((t, LANES), F32), jax.ShapeDtypeStruct((8, D_SSD), F32)],
        input_output_aliases={0: 0},
        scratch_shapes=[pltpu.VMEM((PAIRS, STATE, LANES), F32), pltpu.VMEM((q, D_SSD), F32),
                        pltpu.VMEM((q, D_SSD), F32), pltpu.VMEM((q, D_SSD), F32), pltpu.VMEM((LANES, q), F32),
                        pltpu.VMEM((1, D_SSD), F32), pltpu.VMEM((1, D_SSD), F32), pltpu.VMEM((1, LANES), F32),
                        pltpu.VMEM((1, LANES), F32)],
        compiler_params=_cparams(("arbitrary", "arbitrary")),
        name="ssd_bwd",
    )(dproj, duy, proj, ys, xs_all, xs_all, xs_all, dtr, states, dt_bias, a_log, d_lanes, norm_w, head_of_lane)


HBM_SPEC = pl.BlockSpec(memory_space=pltpu.HBM)
MESH_ID = pl.DeviceIdType.MESH


def _coords():
    return lax.axis_index("x"), lax.axis_index("y"), lax.axis_index("c")


def _chip_peer(xi, yi, ci, d):
    return (jnp.bitwise_xor(xi, d >> 1), jnp.bitwise_xor(yi, d & 1), ci)


def _remote(src, dst, send_sem, recv_sem, peer):
    return pltpu.make_async_remote_copy(src_ref=src, dst_ref=dst, send_sem=send_sem, recv_sem=recv_sem,
                                        device_id=peer, device_id_type=MESH_ID)


SEM_SPEC = pl.BlockSpec(memory_space=pltpu.SEMAPHORE)
ANY_SPEC = pl.BlockSpec(memory_space=pl.ANY)
EFFECT = pltpu.SideEffectType.DATAFLOW_SIDE_EFFECTING
COPIES = 3


def _half(ref, axis, which, lead=0):
    size = ref.shape[lead + axis] // 2
    part = pl.ds(which * size, size)
    idx = (slice(None),) * lead + ((part, slice(None)) if axis == 0 else (slice(None), part))
    return ref.at[idx]


def _halved_shape(shape, axis):
    lead = len(shape) - 2
    return tuple(d // 2 if i == lead + axis else d for i, d in enumerate(shape))


def _gather_plan(axis):
    def plan(xi, yi, ci, src, land):
        me = 2 * xi + yi
        out = []
        for d in (1, 2, 3):
            there = jnp.bitwise_xor(me, d)
            if axis is None:
                out.append((src, land.at[me], _chip_peer(xi, yi, ci, d), land.at[there]))
            else:
                out.append((_half(src, axis, ci), _half(land.at[me], axis, ci), _chip_peer(xi, yi, ci, d),
                            _half(land.at[there], axis, ci)))
        return out
    return plan


def _halves_plan(axis):
    def plan(xi, yi, ci, src, land):
        return [(_half(src, axis, 1 - ci, lead=1), land, (xi, yi, 1 - ci), land)]
    return plan


def _forward_plan(axis):
    def plan(xi, yi, ci, src, land):
        me = 2 * xi + yi
        out = []
        for d in (1, 2, 3):
            slot = land.at[jnp.bitwise_xor(me, d)]
            out.append((_half(slot, axis, ci), _half(slot, axis, ci), (xi, yi, 1 - ci), _half(slot, axis, 1 - ci)))
        return out
    return plan


def _sibling_plan(xi, yi, ci, src, land):
    return [(src, land, (xi, yi, 1 - ci), land)]


def _owners_plan(xi, yi, ci, src, land):
    me = 2 * xi + yi
    return [(src.at[jnp.bitwise_xor(me, d)], land.at[d - 1], _chip_peer(xi, yi, ci, d), land.at[d - 1])
            for d in (1, 2, 3)]


def _split_start(srcs, lands, plans, *, name, dep=None):
    n = len(srcs)
    deps = [] if dep is None else [dep]

    def body(*refs):
        src_refs, land_refs = refs[:n], refs[n:2 * n]
        outs = refs[2 * n + len(deps):]
        ssems, rsems = outs[:n], outs[n:2 * n]
        token = refs[-1]
        xi, yi, ci = _coords()
        for t in range(n):
            for k, (src, dst, peer, _) in enumerate(plans[t](xi, yi, ci, src_refs[t], land_refs[t])):
                _remote(src, dst, ssems[t].at[k], rsems[t].at[k], peer).start()
        token[...] = jnp.zeros_like(token)

    bufs = list(srcs) + list(lands)
    outs = pl.pallas_call(
        body,
        name=name,
        in_specs=[HBM_SPEC] * (2 * n) + [ANY_SPEC] * len(deps),
        out_specs=[SEM_SPEC] * (2 * n) + [HBM_SPEC] * (2 * n) + [pl.BlockSpec(memory_space=pltpu.VMEM)],
        out_shape=[pltpu.SemaphoreType.DMA((COPIES,))] * (2 * n) + [pltpu.HBM(a.shape, a.dtype) for a in bufs]
        + [jax.ShapeDtypeStruct((8, LANES), F32)],
        input_output_aliases={i: 2 * n + i for i in range(2 * n)},
        compiler_params=pltpu.CompilerParams(has_side_effects=EFFECT),
    )(*[pltpu.with_memory_space_constraint(a, pltpu.HBM) for a in bufs], *deps)
    return outs[:n], outs[n:2 * n], outs[2 * n:3 * n], outs[3 * n:4 * n], outs[-1]


def _split_wait(ssems, rsems, srcs, lands, plans, after, *, name):
    n = len(srcs)

    def body(*refs):
        src_refs, land_refs = refs[:n], refs[n:2 * n]
        ss, rs = refs[2 * n:3 * n], refs[3 * n:4 * n]
        xi, yi, ci = _coords()
        for t in range(n):
            for k, (src, _, peer, landed) in enumerate(plans[t](xi, yi, ci, src_refs[t], land_refs[t])):
                cp = _remote(src, landed, ss[t].at[k], rs[t].at[k], peer)
                cp.wait_send()
                cp.wait_recv()

    bufs = list(srcs) + list(lands)
    outs = pl.pallas_call(
        body,
        name=name,
        in_specs=[HBM_SPEC] * (2 * n) + [SEM_SPEC] * (2 * n) + [ANY_SPEC],
        out_specs=[HBM_SPEC] * (2 * n),
        out_shape=[pltpu.HBM(a.shape, a.dtype) for a in bufs],
        input_output_aliases={i: i for i in range(2 * n)},
        compiler_params=pltpu.CompilerParams(has_side_effects=EFFECT),
    )(*bufs, *ssems, *rsems, after)
    return outs[:n], outs[n:]


def _forward_halves(lands, axes, *, name):
    n = len(lands)

    def body(*refs):
        ins, outs = refs[:n], refs[n:2 * n]
        send_sems, recv_sems = refs[2 * n:]
        xi, yi, ci = _coords()
        me = 2 * xi + yi
        sibling = (xi, yi, 1 - ci)
        cps = []
        for t in range(n):
            for d in (1, 2, 3):
                slot = jnp.bitwise_xor(me, d)
                k = COPIES * t + d - 1
                cp = _remote(_half(ins[t].at[slot], axes[t], ci), _half(outs[t].at[slot], axes[t], ci),
                             send_sems.at[k], recv_sems.at[k], sibling)
                cp.start()
                cps.append(cp)
        for t in range(n):
            for d in (1, 2, 3):
                got = _half(outs[t].at[jnp.bitwise_xor(me, d)], axes[t], 1 - ci)
                k = COPIES * t + d - 1
                _remote(got, got, send_sems.at[k], recv_sems.at[k], sibling).wait_recv()
        for cp in cps:
            cp.wait_send()

    return pl.pallas_call(
        body,
        name=name,
        in_specs=[HBM_SPEC] * n,
        out_specs=[HBM_SPEC] * n,
        out_shape=[jax.ShapeDtypeStruct(a.shape, a.dtype) for a in lands],
        input_output_aliases={i: i for i in range(n)},
        scratch_shapes=[pltpu.SemaphoreType.DMA((COPIES * n,)), pltpu.SemaphoreType.DMA((COPIES * n,))],
    )(*lands)


def _swap_other_halves(gs, axes, *, name):
    n = len(gs)

    def body(*refs):
        ins, lands = refs[:n], refs[n:2 * n]
        send_sems, recv_sems = refs[2 * n:]
        xi, yi, ci = _coords()
        sibling = (xi, yi, 1 - ci)
        cps = []
        for t in range(n):
            cp = _remote(_half(ins[t], axes[t], 1 - ci, lead=1), lands[t], send_sems.at[t], recv_sems.at[t], sibling)
            cp.start()
            cps.append(cp)
        for cp in cps:
            cp.wait_recv()
        for cp in cps:
            cp.wait_send()

    return pl.pallas_call(
        body,
        in_specs=[HBM_SPEC] * n,
        out_specs=[HBM_SPEC] * n,
        out_shape=[jax.ShapeDtypeStruct(_halved_shape(g.shape, ax), g.dtype) for g, ax in zip(gs, axes)],
        scratch_shapes=[pltpu.SemaphoreType.DMA((n,)), pltpu.SemaphoreType.DMA((n,))],
        name=name,
    )(*gs)


def _row_tile(rows, cap=512, mult=16):
    best = mult
    for cand in range(mult, min(rows, cap) + 1, mult):
        if rows % cand == 0:
            best = cand
    assert rows % best == 0, rows
    return best


COL_TILE = 256


def _half_tiles(hr, hc, axis, cap=512, mult=16):
    if axis == 0:
        tr = _row_tile(hr, cap, mult)
        n = hr // tr
        return (tr, hc), n, lambda half, i: (half * n + i, 0)
    n = hc // COL_TILE
    return (hr, COL_TILE), n, lambda half, i: (0, half * n + i)


def _add_core_halves(g, land, where, axis):
    nslot, hr, hc = land.shape
    bshape, nr, idx = _half_tiles(hr, hc, axis)

    def body(where_ref, g_ref, l_ref, f_ref, b_ref):
        s = g_ref[...] + l_ref[...].astype(F32)
        b_ref[...] = s.astype(BF16)

        @pl.when(pl.program_id(1) == where_ref[1])
        def _():
            f_ref[...] = s

    blk = pl.BlockSpec((None,) + bshape, lambda i, s, w: (s,) + idx(0, i))
    mine = pl.BlockSpec((None,) + bshape, lambda i, s, w: (s,) + idx(w[0], i))
    return pl.pallas_call(
        body,
        grid_spec=pltpu.PrefetchScalarGridSpec(
            num_scalar_prefetch=1,
            grid=(nr, nslot),
            in_specs=[mine, blk],
            out_specs=[pl.BlockSpec(bshape, lambda i, s, w: idx(0, i)), blk],
        ),
        out_shape=[jax.ShapeDtypeStruct((hr, hc), F32), jax.ShapeDtypeStruct(land.shape, BF16)],
        compiler_params=_cparams(("parallel", "arbitrary")),
        name="add_core_halves",
    )(where, g, land)


def _add_chip_sums(pf, land, where, axis):
    hr, cols = pf.shape
    bshape, nr, idx = _half_tiles(hr, cols, axis)

    def body(where_ref, p_ref, l_ref, o_ref):
        acc = p_ref[...]
        for d in range(3):
            acc = acc + l_ref[d].astype(F32)
        o_ref[...] = acc

    return pl.pallas_call(
        body,
        grid_spec=pltpu.PrefetchScalarGridSpec(
            num_scalar_prefetch=1,
            grid=(nr,),
            in_specs=[pl.BlockSpec(bshape, lambda i, w: idx(0, i)),
                      pl.BlockSpec((3,) + bshape, lambda i, w: (0,) + idx(0, i))],
            out_specs=pl.BlockSpec(bshape, lambda i, w: idx(0, i)),
        ),
        out_shape=jax.ShapeDtypeStruct((hr, cols), F32),
        compiler_params=_cparams(("parallel",)),
        name="add_chip_sums",
    )(where, pf, land)


N_DEV = 8


def _all_reduce_small(part):
    r, w = part.shape

    def body(p_ref, o_ref, gath, send_sems, recv_sems):
        xi, yi, ci = _coords()
        me = 4 * xi + 2 * yi + ci
        gath[me] = p_ref[...]
        cps = []
        for d in range(1, N_DEV):
            peer = (jnp.bitwise_xor(xi, d >> 2), jnp.bitwise_xor(yi, (d >> 1) & 1), jnp.bitwise_xor(ci, d & 1))
            cp = _remote(p_ref, gath.at[me], send_sems.at[d - 1], recv_sems.at[d - 1], peer)
            cp.start()
            cps.append(cp)
        for d in range(1, N_DEV):
            src = gath.at[jnp.bitwise_xor(me, d)]
            _remote(src, src, send_sems.at[d - 1], recv_sems.at[d - 1], (xi, yi, ci)).wait_recv()
        acc = gath[0]
        for k in range(1, N_DEV):
            acc = acc + gath[k]
        o_ref[...] = acc
        for cp in cps:
            cp.wait_send()

    vm = pl.BlockSpec(memory_space=pltpu.VMEM)
    return pl.pallas_call(
        body,
        in_specs=[vm],
        out_specs=vm,
        out_shape=jax.ShapeDtypeStruct((r, w), F32),
        scratch_shapes=[pltpu.VMEM((N_DEV, r, w), F32), pltpu.SemaphoreType.DMA((N_DEV - 1,)),
                        pltpu.SemaphoreType.DMA((N_DEV - 1,))],
        name="all_reduce_small",
    )(part)


def _adamw_math(wv, gv, mv, vv):
    mn = ADAM_B1 * mv + (1.0 - ADAM_B1) * gv
    vn = ADAM_B2 * vv + (1.0 - ADAM_B2) * (gv * gv)
    m_hat = mn / (1.0 - ADAM_B1 ** ADAM_STEP)
    v_hat = vn / (1.0 - ADAM_B2 ** ADAM_STEP)
    return -ADAM_LR * (m_hat / (jnp.sqrt(v_hat) + ADAM_EPS) + ADAM_WD * wv), mn, vn


def _adamw_halves(w, g_mine, g_other, m, v, where, axis, *, name):
    rows, cols = w.shape
    hr, hc = g_mine.shape
    bshape, nr, idx = _half_tiles(hr, hc, axis, cap=256, mult=8)

    def body(where_ref, w_ref, gm_ref, go_ref, m_ref, v_ref, g_ref, d_ref, nm_ref, nv_ref):
        is_mine = pl.program_id(0) == where_ref[0]
        gv = jnp.where(is_mine, gm_ref[...], go_ref[...])
        g_ref[...] = gv
        d_ref[...], nm_ref[...], nv_ref[...] = _adamw_math(w_ref[...], gv, m_ref[...], v_ref[...])

    def parked(half, i, holder):
        return idx(0, jnp.where(half == holder, i, jnp.where(half < holder, 0, nr - 1)))

    blk = pl.BlockSpec(bshape, lambda hf, i, wh: idx(hf, i))
    o = jax.ShapeDtypeStruct((rows, cols), F32)
    return pl.pallas_call(
        body,
        grid_spec=pltpu.PrefetchScalarGridSpec(
            num_scalar_prefetch=1,
            grid=(2, nr),
            in_specs=[blk, pl.BlockSpec(bshape, lambda hf, i, wh: parked(hf, i, wh[0])),
                      pl.BlockSpec(bshape, lambda hf, i, wh: parked(hf, i, 1 - wh[0])), blk, blk],
            out_specs=[blk] * 4,
        ),
        out_shape=[o, o, o, o],
        compiler_params=_cparams(("arbitrary", "arbitrary")),
        name=name,
    )(where, w, g_mine, g_other, m, v)


def _adamw(w, g, m, v, *, name):
    rows, cols = w.shape
    tr = _row_tile(rows, cap=256, mult=8)

    def body(w_ref, g_ref, m_ref, v_ref, d_ref, nm_ref, nv_ref):
        d_ref[...], nm_ref[...], nv_ref[...] = _adamw_math(w_ref[...], g_ref[...], m_ref[...], v_ref[...])

    blk = pl.BlockSpec((tr, cols), lambda i: (i, 0))
    o = jax.ShapeDtypeStruct((rows, cols), F32)
    return pl.pallas_call(
        body,
        grid=(rows // tr,),
        in_specs=[blk] * 4,
        out_specs=[blk] * 3,
        out_shape=[o, o, o],
        compiler_params=_cparams(("parallel",)),
        name=name,
    )(w, g, m, v)


def _pack(arrs):
    flat = jnp.concatenate([a.reshape(-1) for a in arrs])
    pad = (-flat.shape[0]) % (8 * LANES)
    return jnp.pad(flat, (0, pad)).reshape(-1, LANES)


def _unpack(packed, shapes):
    flat = packed.reshape(-1)
    out, off = [], 0
    for s in shapes:
        n = 1
        for dim in s:
            n *= dim
        out.append(flat[off:off + n].reshape(s))
        off += n
    return out


def _pad_rows(a, rows):
    return jnp.pad(a, ((0, rows - a.shape[0]), (0, 0)))


def _pad_lanes(a, lanes=LANES):
    return jnp.pad(a, ((0, 0), (0, lanes - a.shape[1])))


def _local_grads(x2d, tgt2d, prm, get_w, on_grads, *, nb, seq, dep=None):
    g_pre, g_post, g_fpre, g_fpost = prm["norm_mix_pre"], prm["norm_mix_post"], prm["norm_ffn_pre"], prm["norm_ffn_post"]
    dt_bias, a_log = _pad_lanes(prm["ssd_dt_bias"]), _pad_lanes(prm["ssd_a_log"])
    d_lanes = jnp.repeat(prm["ssd_d"], HEAD_DIM, axis=1)

    h = _rms_fwd(x2d, g_pre, dep=dep, out_dtype=BF16, name="rms_mix_pre")
    t, d = x2d.shape
    w_in_t, w_dt_t, cw, sw = get_w("in", h)
    proj = _matmul([(h, w_in_t)], mode="nt", out_dtype=F32, tm=1024, tn=1024, tk=2048, name="mm_proj",
                   extent=(t, D_MAIN, d))
    dtr = _matmul([(h, w_dt_t)], mode="nt", out_dtype=F32, tm=1024, tn=128, tk=2048, name="mm_dt")
    u1, uy = _conv_branch_fwd(proj, cw, prm["conv_dw_b"], prm["conv_ln_g"], prm["conv_ln_b"], nb=nb, seq=seq)
    xs_all = _ssd_pre_fwd(proj, sw, prm["ssd_conv_b"], nb=nb, seq=seq)
    uy, ys, states = _ssd_fwd(uy, xs_all, proj, dtr, dt_bias, a_log, d_lanes, prm["ssd_norm_w"], nb=nb, seq=seq)
    w_out = get_w("out", uy)
    mix = _matmul([(uy, w_out)], mode="nn", out_dtype=F32, tm=1024, tn=1024, tk=2048, name="mm_mix")
    x1, h2 = _rms_post_pre(mix, x2d, g_post, g_fpre)
    w_gate, w_up = get_w("up", h2)
    gt, up, act = _ffn_up(h2, w_gate, w_up, tm=1024, tn=512)
    w_down = get_w("down", act)
    dx2, df, loss, dg_fpost = _down_loss(act, w_down, x1, tgt2d, g_fpost, tk=1408)

    dgt, dup = _ffn_bwd_act(df, w_down, gt, up, tm=1024, tn=512)
    dw_down = _matmul([(act, df)], mode="tn", out_dtype=F32, tm=1408, tn=1024, tk=2048, name="mm_dw_down",
                      also_bf16=True)
    dw_gate = _matmul([(h2, dgt)], mode="tn", out_dtype=F32, tm=1024, tn=1408, tk=2048, name="mm_dw_gate",
                      slot_out=True, also_bf16=True)
    dw_up = _matmul([(h2, dup)], mode="tn", out_dtype=F32, tm=1024, tn=1408, tk=2048, name="mm_dw_up",
                    slot_out=True, also_bf16=True)
    dep = on_grads("ffn", (dw_down, dw_gate, dw_up))
    dh2 = _matmul([(dgt, w_gate), (dup, w_up)], mode="nt", out_dtype=F32, tm=1024, tn=1024, tk=1408, name="mm_dh2",
                  dep=dep)
    dep = on_grads("ffn_sums", dh2)
    dx1, dmix, dg_fpre, dg_post = _rms_bwd_pre_post(dh2, x1, g_fpre, dx2, mix, g_post, dep=dep)
    dw_out = _matmul([(uy, dmix)], mode="tn", out_dtype=F32, tm=1024, tn=1024, tk=2048, name="mm_dw_out",
                     also_bf16=True)
    dep = on_grads("out", (dw_out,))
    duy = _matmul([(dmix, w_out)], mode="nt", out_dtype=F32, tm=1024, tn=1024, tk=2048, name="mm_duy", dep=dep)
    dproj, dcw, dcb, dlg, dlb = _conv_branch_bwd(duy, u1, proj, cw, prm["conv_ln_g"], prm["conv_ln_b"], nb=nb, seq=seq)
    dproj, dxs, ddtr, ssd_small = _ssd_bwd(dproj, duy, proj, ys, xs_all, dtr, states, dt_bias, a_log, d_lanes,
                                           prm["ssd_norm_w"], nb=nb, seq=seq)
    dproj, dsw, dsb = _ssd_pre_bwd(dproj, dxs, proj, sw, prm["ssd_conv_b"], nb=nb, seq=seq)
    ddtr_b = ddtr.astype(BF16)
    dw_in_t = _matmul([(dproj, h)], mode="tn", out_dtype=F32, tm=1024, tn=1024, tk=2048, name="mm_dw_main",
                      extent=(D_MAIN, d, t), out_rows=D_IN)
    dw_in_t = _dw_dt_rows(dw_in_t, ddtr_b, h)
    dep = on_grads("in", ((dw_in_t, dw_in_t),))
    dx, dg_pre = _dh_dx(dproj, w_in_t, ddtr_b, w_dt_t, x2d, dx1, g_pre, tk=1280, dep=g_pre if dep is None else dep)

    grads = {
        "norm_mix_pre": dg_pre,
        "w_in": dw_in_t,
        "conv_dw_w": dcw[:CONV_K], "conv_dw_b": dcb, "conv_ln_g": dlg, "conv_ln_b": dlb,
        "ssd_conv_w": dsw[:SSD_CONV_K], "ssd_conv_b": dsb,
        "ssd_dt_bias": ssd_small[2:3, :HEADS], "ssd_a_log": ssd_small[3:4, :HEADS], "ssd_d": ssd_small[1:2, :HEADS],
        "ssd_norm_w": ssd_small[0:1],
        "w_out": dw_out[0],
        "norm_mix_post": dg_post, "norm_ffn_pre": dg_fpre,
        "w_gate": dw_gate[0], "w_up": dw_up[0],
        "w_down": dw_down[0], "norm_ffn_post": dg_fpost,
    }
    return loss, dx, grads


BIG = ("w_in", "w_out", "w_gate", "w_up", "w_down")
HALF_AXIS = {"w_in": 1, "w_out": 0, "w_gate": 0, "w_up": 0, "w_down": 0}
GATHER_STAGES = {"in": ("w_in", "conv_dw_w", "ssd_conv_w"), "out": ("w_out",), "up": ("w_gate", "w_up"),
                 "down": ("w_down",)}
GATHER_ORDER = tuple(n for st in ("in", "out", "up", "down") for n in GATHER_STAGES[st])
SMALL = ("norm_mix_pre", "conv_dw_w", "conv_dw_b", "conv_ln_g", "conv_ln_b", "ssd_conv_w", "ssd_conv_b", "ssd_dt_bias",
         "ssd_a_log", "ssd_d", "ssd_norm_w", "norm_mix_post", "norm_ffn_pre", "norm_ffn_post")
WEIGHTS = ("norm_mix_pre", "w_in", "conv_dw_w", "conv_dw_b", "conv_ln_g", "conv_ln_b", "ssd_conv_w", "ssd_conv_b",
           "ssd_dt_bias", "ssd_a_log", "ssd_d", "ssd_norm_w", "w_out", "norm_mix_post", "norm_ffn_pre", "w_gate", "w_up",
           "w_down", "norm_ffn_post")


def _cols_from_slots(a):
    n, rows, w = a.shape
    return a.transpose(1, 0, 2).reshape(rows, n * w)


def kernel(x, norm_mix_pre, w_in, conv_dw_w, conv_dw_b, conv_ln_g, conv_ln_b, ssd_conv_w, ssd_conv_b, ssd_dt_bias, ssd_a_log, ssd_d, ssd_norm_w, w_out, norm_mix_post, norm_ffn_pre, w_gate, w_up, w_down, norm_ffn_post, loss_target, m_norm_mix_pre, m_w_in, m_conv_dw_w, m_conv_dw_b, m_conv_ln_g, m_conv_ln_b, m_ssd_conv_w, m_ssd_conv_b, m_ssd_dt_bias, m_ssd_a_log, m_ssd_d, m_ssd_norm_w, m_w_out, m_norm_mix_post, m_norm_ffn_pre, m_w_gate, m_w_up, m_w_down, m_norm_ffn_post, v_norm_mix_pre, v_w_in, v_conv_dw_w, v_conv_dw_b, v_conv_ln_g, v_conv_ln_b, v_ssd_conv_w, v_ssd_conv_b, v_ssd_dt_bias, v_ssd_a_log, v_ssd_d, v_ssd_norm_w, v_w_out, v_norm_mix_post, v_norm_ffn_pre, v_w_gate, v_w_up, v_w_down, v_norm_ffn_post):
    args = dict(locals())
    two_d = lambda n, a: jnp.swapaxes(a, 1, 2)[0] if n == "w_in" else a.reshape(a.shape[-2:])
    wts = {n: two_d(n, args[n]) for n in WEIGHTS}
    ms = {n: two_d(n, args["m_" + n]) for n in WEIGHTS}
    vs = {n: two_d(n, args["v_" + n]) for n in WEIGHTS}
    nb, seq, d = x.shape
    t = nb * seq
    xi, yi, ci = _coords()
    chip = 2 * xi + yi
    where = jnp.stack([ci, chip]).astype(jnp.int32)

    shards = {n: wts[n].astype(BF16) for n in BIG}
    shards.update(conv_dw_w=_pad_rows(wts["conv_dw_w"], 32), ssd_conv_w=_pad_rows(wts["ssd_conv_w"], 8))
    plans = [_gather_plan(HALF_AXIS.get(n)) for n in GATHER_ORDER]
    n_first = len(GATHER_STAGES["in"])

    def start(part, name, dep=None):
        src = [shards[n] for n in GATHER_ORDER[part]]
        return _split_start(src, [lax.empty((N_CHIPS,) + s.shape, s.dtype) for s in src], plans[part], name=name,
                            dep=dep)

    head = start(slice(0, n_first), "gather_start_in")
    rest = start(slice(n_first, None), "gather_start_rest", dep=head[4])
    ssems, rsems, srcs, lands = (list(h) + list(r) for h, r in zip(head[:4], rest[:4]))
    token = rest[4]

    forwarding = {}

    def arrived(stage, after):
        names = GATHER_STAGES[stage]
        pick = lambda seq_: [seq_[GATHER_ORDER.index(n)] for n in names]
        own, got = _split_wait(pick(ssems), pick(rsems), pick(srcs), pick(lands), pick(plans), after,
                               name="gather_wait_" + stage)
        return dict(zip(names, own)), dict(zip(names, got))

    def get_w(stage, after):
        names = GATHER_STAGES[stage]
        if stage in forwarding:
            own, sems = forwarding.pop(stage)
            fwd_own, fwd_got = _split_wait(*sems[:4], [_forward_plan(HALF_AXIS[n]) for n in names], after,
                                           name="gather_forward_wait_" + stage)
            got = dict(zip(names, fwd_got))
        else:
            own, got = arrived(stage, after)
            big = [n for n in names if n in BIG]
            got.update(zip(big, _forward_halves([got[n] for n in big], [HALF_AXIS[n] for n in big],
                                                name="gather_forward_" + stage)))
        if stage == "out":
            nxt = GATHER_STAGES["up"]
            up_own, up_got = arrived("up", after)
            forwarding["up"] = (up_own, _split_start([up_own[n] for n in nxt], [up_got[n] for n in nxt],
                                                     [_forward_plan(HALF_AXIS[n]) for n in nxt],
                                                     name="gather_forward_start_up"))
        full = {n: lax.dynamic_update_slice(got[n], own[n][None], (chip, 0, 0)) for n in got}
        if stage == "in":
            w_in_t = full["w_in"].reshape(D_IN, D_MODEL)
            return (w_in_t, _pad_rows(w_in_t[D_MAIN:], LANES), _cols_from_slots(full["conv_dw_w"]),
                    _cols_from_slots(full["ssd_conv_w"]))
        if stage == "out":
            return full["w_out"].reshape(D_MODEL, D_MODEL)
        if stage == "up":
            return _cols_from_slots(full["w_gate"]), _cols_from_slots(full["w_up"])
        return full["w_down"].reshape(D_FF, D_MODEL)

    reduce_groups = {"ffn": ("w_down", "w_gate", "w_up"), "out": ("w_out",), "in": ("w_in",)}
    in_flight = {}

    swapping = {}

    def send_to_owners(stage, f32s, kept, axes):
        sums = [_add_core_halves(f32, l, where, ax) for f32, l, ax in zip(f32s, kept, axes)]
        ps = [s[1] for s in sums]
        ssem, rsem, ps, recv, started = _split_start(
            ps, [lax.empty((COPIES,) + p.shape[1:], p.dtype) for p in ps], [_owners_plan] * len(ps),
            name="owners_start_" + stage)
        in_flight[stage] = (ssem, rsem, ps, recv, [s[0] for s in sums])
        return started

    def on_grads(stage, gs):
        if stage == "ffn_sums":
            ssem, rsem, b16s, kept, f32s, axes = swapping.pop("ffn")
            _, kept = _split_wait(ssem, rsem, b16s, kept, [_halves_plan(ax) for ax in axes], gs,
                                  name="swap_other_halves_wait_ffn")
            return send_to_owners("ffn", f32s, kept, axes)
        names = reduce_groups[stage]
        axes = [HALF_AXIS[n] for n in names]
        slot = lambda g: g if g.ndim == 3 else g.reshape((N_CHIPS, g.shape[0] // N_CHIPS, g.shape[1]))
        f32s, b16s = [slot(f32) for f32, _ in gs], [slot(b16) for _, b16 in gs]
        if stage == "ffn":
            kept = [lax.empty(_halved_shape(b.shape, ax), b.dtype) for b, ax in zip(b16s, axes)]
            ssem, rsem, b16s, kept, started = _split_start(b16s, kept, [_halves_plan(ax) for ax in axes],
                                                           name="swap_other_halves_start_ffn")
            swapping["ffn"] = (ssem, rsem, b16s, kept, f32s, axes)
            return started
        return send_to_owners(stage, f32s, _swap_other_halves(b16s, axes, name="swap_other_halves_" + stage), axes)

    prm = {n: wts[n] for n in SMALL}
    loss, dx, grads = _local_grads(x.reshape(t, d), loss_target.reshape(t, d), prm, get_w, on_grads,
                                   nb=nb, seq=seq, dep=token)
    loss = lax.psum(loss[0, 0], MESH_AXES)

    def reduced(stage, after):
        ssem, rsem, ps, recv, own_sums = in_flight[stage]
        _, recv = _split_wait(ssem, rsem, ps, recv, [_owners_plan] * len(ps), after, name="owners_wait_" + stage)
        return {n: _add_chip_sums(f32_sum, r, where, HALF_AXIS[n])
                for n, f32_sum, r in zip(reduce_groups[stage], own_sums, recv)}

    def swap_start(names, name, dep=None):
        mine = [halves[n] for n in names]
        return _split_start(mine, [lax.empty(h.shape, h.dtype) for h in mine], [_sibling_plan] * len(mine), name=name,
                            dep=dep)

    halves = reduced("ffn", dx)
    ffn_swap = swap_start(reduce_groups["ffn"], "swap_reduced_start_ffn")
    halves.update(reduced("out", ffn_swap[4]))
    halves.update(reduced("in", ffn_swap[4]))
    mix_names = reduce_groups["out"] + reduce_groups["in"]
    mix_swap = swap_start(mix_names, "swap_reduced_start_mix", dep=ffn_swap[4])

    small_shapes = [grads[n].shape for n in SMALL]
    small_sum = _unpack(_all_reduce_small(_pack([grads[n] for n in SMALL])), small_shapes)
    small_grads = dict(zip(SMALL, small_sum))
    cwid, swid = D_CONV // N_CHIPS, D_XBC // N_CHIPS
    small_grads["conv_dw_w"] = lax.dynamic_slice(small_grads["conv_dw_w"], (0, chip * cwid), (CONV_K, cwid))
    small_grads["ssd_conv_w"] = lax.dynamic_slice(small_grads["ssd_conv_w"], (0, chip * swid), (SSD_CONV_K, swid))

    out_g, out_d, out_m, out_v = {}, {}, {}, {}
    shard_shapes = [wts[n].shape for n in SMALL]
    pd, pm, pv = _adamw(_pack([wts[n] for n in SMALL]), _pack([small_grads[n] for n in SMALL]),
                        _pack([ms[n] for n in SMALL]), _pack([vs[n] for n in SMALL]), name="adamw_small")
    for n, dd, mm, vv in zip(SMALL, _unpack(pd, shard_shapes), _unpack(pm, shard_shapes), _unpack(pv, shard_shapes)):
        out_g[n], out_d[n], out_m[n], out_v[n] = small_grads[n], dd, mm, vv

    def big_adamw(names, swap, after, name):
        mine, other = _split_wait(*swap[:4], [_sibling_plan] * len(names), after, name=name)
        for n, gm, go in zip(names, mine, other):
            out_g[n], out_d[n], out_m[n], out_v[n] = _adamw_halves(wts[n], gm, go, ms[n], vs[n], where, HALF_AXIS[n],
                                                                     name="adamw_" + n)

    big_adamw(reduce_groups["ffn"], ffn_swap, pd, "swap_reduced_wait_ffn")
    big_adamw(mix_names, mix_swap, out_d[reduce_groups["ffn"][-1]], "swap_reduced_wait_mix")

    back = lambda n, a: jnp.swapaxes(a[None], 1, 2) if n == "w_in" else a.reshape(args[n].shape)
    outs = [back(n, o[n]) for o in (out_g, out_d, out_m, out_v) for n in WEIGHTS]
    return (loss, dx.reshape(nb, seq, d), *outs)
```

```python
import functools

import jax
import jax.numpy as jnp
from jax import lax
from jax.experimental import pallas as pl
from jax.experimental.pallas import tpu as pltpu

F32 = jnp.float32
BF16 = jnp.bfloat16
EPS = 1e-6

D_MODEL = 2048
D_CONV = 1024
D_SSD = 1024
D_XBC = 2048
HEADS = 16
HEAD_DIM = 64
GROUPS = 4
STATE = 128
CONV_K = 31
SSD_CONV_K = 4
D_FF = 5632
D_MAIN = 2 * D_CONV + D_SSD + D_XBC
D_IN = D_MAIN + HEADS
N_CHIPS = 4
LANES = 128
CHUNK = 128
PAIRS = HEADS // 2

ADAM_LR = 0.001
ADAM_B1 = 0.9
ADAM_B2 = 0.999
ADAM_EPS = 1e-08
ADAM_WD = 0.01
ADAM_STEP = 10

MESH_AXES = ("x", "y", "c")
VMEM_LIMIT = 56 * 1024 * 1024


def _sig(v):
    return 1.0 / (1.0 + jnp.exp(-v))


def _cparams(sem, vmem=VMEM_LIMIT):
    return pltpu.CompilerParams(dimension_semantics=sem, vmem_limit_bytes=vmem)


_DIMS = {"nn": ((1,), (0,)), "nt": ((1,), (1,)), "tn": ((0,), (0,))}


def _matmul(pairs, *, mode, out_dtype, tm, tn, tk, name, slot_out=False, extent=None, out_rows=None, dep=None,
            also_bf16=False):
    a0, b0 = pairs[0]
    if mode == "nn":
        (m, k), n = a0.shape, b0.shape[1]
    elif mode == "nt":
        (m, k), n = a0.shape, b0.shape[0]
    else:
        (k, m), n = a0.shape, b0.shape[1]
    if extent is not None:
        m, n, k = extent
    tm, tn, tk = min(tm, m), min(tn, n), min(tk, k)
    assert m % tm == 0 and n % tn == 0 and k % tk == 0, (name, m, n, k, tm, tn, tk)
    nk = k // tk
    npairs = len(pairs)
    deps = [] if dep is None else [dep]
    dims = (_DIMS[mode], ((), ()))

    use_scratch = nk > 1 and out_dtype != F32

    def body(*refs):
        ins, o_ref = refs[: 2 * npairs], refs[2 * npairs + len(deps)]
        dot = lambda p: lax.dot_general(ins[2 * p][...], ins[2 * p + 1][...], dims, preferred_element_type=F32)
        if nk == 1:
            part = dot(0)
            for p in range(1, npairs):
                part = part + dot(p)
            o_ref[...] = part.astype(out_dtype)
            if also_bf16:
                refs[2 * npairs + len(deps) + 1][...] = part.astype(BF16)
            return
        acc = refs[-1] if use_scratch else o_ref
        kk = pl.program_id(2)

        @pl.when(kk == 0)
        def _():
            acc[...] = jnp.zeros_like(acc)

        for p in range(npairs):
            acc[...] += dot(p)

        if use_scratch:
            @pl.when(kk == nk - 1)
            def _():
                o_ref[...] = acc[...].astype(out_dtype)

        if also_bf16:
            @pl.when(kk == nk - 1)
            def _():
                refs[2 * npairs + len(deps) + 1][...] = acc[...].astype(BF16)

    if mode == "nn":
        a_spec = pl.BlockSpec((tm, tk), lambda i, j, kk: (i, kk))
        b_spec = pl.BlockSpec((tk, tn), lambda i, j, kk: (kk, j))
    elif mode == "nt":
        a_spec = pl.BlockSpec((tm, tk), lambda i, j, kk: (i, kk))
        b_spec = pl.BlockSpec((tn, tk), lambda i, j, kk: (j, kk))
    else:
        a_spec = pl.BlockSpec((tk, tm), lambda i, j, kk: (kk, i))
        b_spec = pl.BlockSpec((tk, tn), lambda i, j, kk: (kk, j))
    if slot_out:
        out_shape = jax.ShapeDtypeStruct((n // tn, m, tn), out_dtype)
        out_spec = pl.BlockSpec((None, tm, tn), lambda i, j, kk: (j, i, 0))
    else:
        out_shape = jax.ShapeDtypeStruct((m if out_rows is None else out_rows, n), out_dtype)
        out_spec = pl.BlockSpec((tm, tn), lambda i, j, kk: (i, j))
    flat = [t for ab in pairs for t in ab]
    if also_bf16:
        out_spec = [out_spec, out_spec]
        out_shape = [out_shape, jax.ShapeDtypeStruct(out_shape.shape, BF16)]
    return pl.pallas_call(
        body,
        grid=(m // tm, n // tn, nk),
        in_specs=[a_spec, b_spec] * npairs + [pl.BlockSpec(memory_space=pl.ANY)] * len(deps),
        out_specs=out_spec,
        out_shape=out_shape,
        scratch_shapes=[pltpu.VMEM((tm, tn), F32)] if use_scratch else [],
        compiler_params=_cparams(("parallel", "parallel", "arbitrary")),
        name=name,
    )(*flat, *deps)


SUB_ROWS = 256


def _ffn_up(h2, wg, wu, *, tm, tn):
    t, k = h2.shape
    n = wg.shape[1]
    tm = min(tm, t)
    assert t % tm == 0 and n % tn == 0, (t, n, tm, tn)

    sub = min(SUB_ROWS, tm)

    def body(h_ref, wg_ref, wu_ref, g_ref, u_ref, a_ref, at_ref):
        for r in range(tm // sub):
            rows = pl.ds(r * sub, sub)
            hv = h_ref[rows, :]
            g = jnp.dot(hv, wg_ref[...], preferred_element_type=F32)
            u = jnp.dot(hv, wu_ref[...], preferred_element_type=F32)
            g_ref[rows, :] = g.astype(BF16)
            u_ref[rows, :] = u.astype(BF16)
            act = (g * _sig(g) * u).astype(BF16)
            a_ref[rows, :] = act
            at_ref[:, rows] = act.T

    o = jax.ShapeDtypeStruct((t, n), BF16)
    ospec = pl.BlockSpec((tm, tn), lambda i, j: (i, j))
    return pl.pallas_call(
        body,
        grid=(t // tm, n // tn),
        in_specs=[pl.BlockSpec((tm, k), lambda i, j: (i, 0)), pl.BlockSpec((k, tn), lambda i, j: (0, j)),
                  pl.BlockSpec((k, tn), lambda i, j: (0, j))],
        out_specs=[ospec, ospec, ospec, pl.BlockSpec((tn, tm), lambda i, j: (j, i))],
        out_shape=[o, o, o, jax.ShapeDtypeStruct((n, t), BF16)],
        compiler_params=_cparams(("parallel", "parallel")),
        name="ffn_up",
    )(h2, wg, wu)


def _ffn_bwd_act(df, wd, gt, up, *, tm, tn):
    t, k = df.shape
    n = wd.shape[0]
    tm = min(tm, t)
    assert t % tm == 0 and n % tn == 0, (t, n, tm, tn)

    sub = min(SUB_ROWS, tm)

    def body(df_ref, wd_ref, g_ref, u_ref, dg_ref, du_ref):
        for r in range(tm // sub):
            rows = pl.ds(r * sub, sub)
            da = lax.dot_general(df_ref[rows, :], wd_ref[...], (_DIMS["nt"], ((), ())), preferred_element_type=F32)
            g = g_ref[rows, :].astype(F32)
            u = u_ref[rows, :].astype(F32)
            s = _sig(g)
            dg_ref[rows, :] = (da * u * s * (1.0 + g * (1.0 - s))).astype(BF16)
            du_ref[rows, :] = (da * g * s).astype(BF16)

    o = jax.ShapeDtypeStruct((t, n), BF16)
    blk = pl.BlockSpec((tm, tn), lambda i, j: (i, j))
    return pl.pallas_call(
        body,
        grid=(t // tm, n // tn),
        in_specs=[pl.BlockSpec((tm, k), lambda i, j: (i, 0)), pl.BlockSpec((tn, k), lambda i, j: (j, 0)), blk, blk],
        out_specs=[blk, blk],
        out_shape=[o, o],
        compiler_params=_cparams(("parallel", "parallel")),
        name="ffn_bwd_act",
    )(df, wd, gt, up)


def _dw_dt_rows(dw_in_t, ddtr_b, h, *, tk=1024):
    t, d = h.shape
    tk = min(tk, t)
    nk = t // tk

    def body(buf_ref, d_ref, h_ref, o_ref, acc):
        kk = pl.program_id(0)

        @pl.when(kk == 0)
        def _():
            acc[...] = jnp.zeros_like(acc)

        acc[...] += lax.dot_general(d_ref[...], h_ref[...], (_DIMS["tn"], ((), ())), preferred_element_type=F32)

        @pl.when(kk == nk - 1)
        def _():
            o_ref[...] = acc[0:HEADS, :]

    return pl.pallas_call(
        body,
        grid=(nk,),
        in_specs=[DEP_SPEC, pl.BlockSpec((tk, LANES), lambda kk: (kk, 0)), pl.BlockSpec((tk, d), lambda kk: (kk, 0))],
        out_specs=pl.BlockSpec((HEADS, d), lambda kk: (D_MAIN // HEADS, 0)),
        out_shape=jax.ShapeDtypeStruct(dw_in_t.shape, F32),
        input_output_aliases={0: 0},
        scratch_shapes=[pltpu.VMEM((LANES, d), F32)],
        compiler_params=_cparams(("arbitrary",)),
        name="mm_dw_dt",
    )(dw_in_t, ddtr_b, h)


ROW_TILE = 256


DEP_SPEC = pl.BlockSpec(memory_space=pl.ANY)


def _rms_fwd(xv, g, *, dep=None, out_dtype, name):
    t, d = xv.shape
    deps = [] if dep is None else [dep]

    def body(*refs):
        x_ref, g_ref = refs[0], refs[1]
        o_ref = refs[-1]
        v = x_ref[...]
        r = lax.rsqrt(jnp.mean(v * v, axis=-1, keepdims=True) + EPS)
        o_ref[...] = (v * r * g_ref[...]).astype(out_dtype)

    row = pl.BlockSpec((ROW_TILE, d), lambda i: (i, 0))
    vec = pl.BlockSpec((1, d), lambda i: (0, 0))
    return pl.pallas_call(
        body,
        grid=(t // ROW_TILE,),
        in_specs=[row, vec] + [DEP_SPEC] * len(deps),
        out_specs=row,
        out_shape=jax.ShapeDtypeStruct((t, d), out_dtype),
        compiler_params=_cparams(("parallel",)),
        name=name,
    )(*([xv, g] + deps))


def _rms_bwd_rows(dy, v, gv):
    r = lax.rsqrt(jnp.mean(v * v, axis=-1, keepdims=True) + EPS)
    xh = v * r
    gdy = dy * gv
    dx = r * (gdy - xh * jnp.mean(gdy * xh, axis=-1, keepdims=True))
    return dx, jnp.sum(dy * xh, axis=0, keepdims=True)


FUSED_ROWS = 512


def _matmul_rows_tail(a, w, tail, *, tk, row_ins, vec_ins, row_outs, vec_outs, first=None, dep=None, name):
    t, kdim = a.shape
    d = w.shape[1]
    tm, tk = min(FUSED_ROWS, t), min(tk, kdim)
    nk, nb = kdim // tk, t // tm
    assert t % tm == 0 and kdim % tk == 0
    n_ri, n_vi, n_ro, n_vo = len(row_ins), len(vec_ins), len(row_outs), len(vec_outs)
    n_first = 0 if first is None else 2
    deps = [] if dep is None else [dep]

    def body(*refs):
        a_ref, w_ref = refs[0], refs[1]
        first_refs = refs[2:2 + n_first]
        p = 2 + n_first
        ri = refs[p:p + n_ri]
        vi = refs[p + n_ri:p + n_ri + n_vi]
        p += n_ri + n_vi + len(deps)
        ro = refs[p:p + n_ro]
        vo = refs[p + n_ro:p + n_ro + n_vo]
        acc = refs[-1]
        i, kk = pl.program_id(0), pl.program_id(1)

        @pl.when(jnp.logical_and(i == 0, kk == 0))
        def _():
            for ref in vo:
                ref[...] = jnp.zeros_like(ref)

        @pl.when(kk == 0)
        def _():
            if first is None:
                acc[...] = jnp.zeros_like(acc)
            else:
                acc[...] = jnp.dot(first_refs[0][...], first_refs[1][...], preferred_element_type=F32)

        acc[...] += jnp.dot(a_ref[...], w_ref[...], preferred_element_type=F32)

        @pl.when(kk == nk - 1)
        def _():
            outs, parts = tail(acc[...], [r[...] for r in ri], [v[...] for v in vi])
            for ref, val in zip(ro, outs):
                ref[...] = val.astype(ref.dtype)
            for ref, part in zip(vo, parts):
                ref[...] += part

    row = pl.BlockSpec((tm, d), lambda i, kk: (i, 0))
    const = lambda shape: pl.BlockSpec(shape, lambda i, kk: (0,) * len(shape))
    in_specs = [pl.BlockSpec((tm, tk), lambda i, kk: (i, kk)), pl.BlockSpec((tk, d), lambda i, kk: (kk, 0))]
    if first is not None:
        in_specs += [pl.BlockSpec((tm, first[0].shape[1]), lambda i, kk: (i, 0)), const(first[1].shape)]
    in_specs += [row] * n_ri + [const(v.shape) for v in vec_ins] + [pl.BlockSpec(memory_space=pl.ANY)] * len(deps)
    return pl.pallas_call(
        body,
        grid=(nb, nk),
        in_specs=in_specs,
        out_specs=[row] * n_ro + [const(sh) for sh in vec_outs],
        out_shape=[jax.ShapeDtypeStruct((t, d), dt) for dt in row_outs]
        + [jax.ShapeDtypeStruct(sh, F32) for sh in vec_outs],
        scratch_shapes=[pltpu.VMEM((tm, d), F32)],
        compiler_params=_cparams(("arbitrary", "arbitrary")),
        name=name,
    )(a, w, *([] if first is None else list(first)), *row_ins, *vec_ins, *deps)


def _down_loss(act, w_down, x1, tgt, g, *, tk):
    d = w_down.shape[1]

    def tail(v, rows, vecs):
        x1v, tv = rows
        gv, = vecs
        fh = v * lax.rsqrt(jnp.mean(v * v, axis=-1, keepdims=True) + EPS)
        e = x1v + fh * gv - tv
        dx2 = e * (1.0 / d)
        df, dg = _rms_bwd_rows(dx2, v, gv)
        loss = 0.5 * jnp.sum(jnp.mean(e * e, axis=-1, keepdims=True), axis=0, keepdims=True)
        return (dx2, df), (loss, dg)

    return _matmul_rows_tail(act, w_down, tail, tk=tk, row_ins=[x1, tgt], vec_ins=[g], row_outs=[F32, BF16],
                             vec_outs=[(1, 1), (1, d)], name="mm_down_loss")


def _dh_dx(dproj, w_in_t, ddtr_b, w_dt_t, xv, dx1, g, *, tk, dep):
    def tail(v, rows, vecs):
        xr, dx1r = rows
        dx, dg = _rms_bwd_rows(v, xr, vecs[0])
        return (dx + dx1r,), (dg,)

    return _matmul_rows_tail(dproj, w_in_t, tail, tk=tk, row_ins=[xv, dx1], vec_ins=[g], row_outs=[F32],
                             vec_outs=[(1, xv.shape[1])], first=(ddtr_b, w_dt_t), dep=dep, name="mm_dh_dx")


def _rms_post_pre(mix, xv, g_post, g_pre):
    t, d = mix.shape

    def body(m_ref, x_ref, gp_ref, gf_ref, x1_ref, h2_ref, h2t_ref):
        v = m_ref[...]
        x1 = x_ref[...] + v * lax.rsqrt(jnp.mean(v * v, axis=-1, keepdims=True) + EPS) * gp_ref[...]
        x1_ref[...] = x1
        h2 = (x1 * lax.rsqrt(jnp.mean(x1 * x1, axis=-1, keepdims=True) + EPS) * gf_ref[...]).astype(BF16)
        h2_ref[...] = h2
        h2t_ref[...] = h2.T

    row = pl.BlockSpec((ROW_TILE, d), lambda i: (i, 0))
    vec = pl.BlockSpec((1, d), lambda i: (0, 0))
    return pl.pallas_call(
        body,
        grid=(t // ROW_TILE,),
        in_specs=[row, row, vec, vec],
        out_specs=[row, row, pl.BlockSpec((d, ROW_TILE), lambda i: (0, i))],
        out_shape=[jax.ShapeDtypeStruct((t, d), F32), jax.ShapeDtypeStruct((t, d), BF16),
                   jax.ShapeDtypeStruct((d, t), BF16)],
        compiler_params=_cparams(("parallel",)),
        name="rms_mix_post_ffn_pre",
    )(mix, xv, g_post, g_pre)


def _rms_bwd_pre_post(dh2, x1, g_pre, dx2, mix, g_post, *, dep=None):
    t, d = x1.shape

    deps = [] if dep is None else [dep]

    def body(dh_ref, x1_ref, gf_ref, dx2_ref, m_ref, gp_ref, *rest):
        dx1_ref, dmix_ref, dgf_ref, dgp_ref = rest[len(deps):]

        @pl.when(pl.program_id(0) == 0)
        def _():
            dgf_ref[...] = jnp.zeros_like(dgf_ref)
            dgp_ref[...] = jnp.zeros_like(dgp_ref)

        dx, dgf = _rms_bwd_rows(dh_ref[...], x1_ref[...], gf_ref[...])
        dx1 = dx + dx2_ref[...]
        dx1_ref[...] = dx1
        dmix, dgp = _rms_bwd_rows(dx1, m_ref[...], gp_ref[...])
        dmix_ref[...] = dmix.astype(BF16)
        dgf_ref[...] += dgf
        dgp_ref[...] += dgp

    row = pl.BlockSpec((ROW_TILE, d), lambda i: (i, 0))
    vec = pl.BlockSpec((1, d), lambda i: (0, 0))
    return pl.pallas_call(
        body,
        grid=(t // ROW_TILE,),
        in_specs=[row, row, vec, row, row, vec] + [DEP_SPEC] * len(deps),
        out_specs=[row, row, vec, vec],
        out_shape=[jax.ShapeDtypeStruct((t, d), F32), jax.ShapeDtypeStruct((t, d), BF16),
                   jax.ShapeDtypeStruct((1, d), F32), jax.ShapeDtypeStruct((1, d), F32)],
        compiler_params=_cparams(("arbitrary",)),
        name="rms_ffn_pre_mix_post_bwd",
    )(dh2, x1, g_pre, dx2, mix, g_post, *deps)


CONV_ROWS = 256
TAP_ROWS = 64
HALO31 = 32
HALO4 = 8


def _sum8(v):
    return jnp.sum(v.reshape(v.shape[0] // 8, 8, v.shape[1]), axis=0)


SUBLANES = 8
PHASE_SPAN = (CONV_K - 1) // SUBLANES * SUBLANES


def _phase_scratch(ts):
    return pltpu.VMEM((SUBLANES, ts + PHASE_SPAN, LANES), F32)


def _phase_copies(ph, buf, ln, base, ts):
    for s in range(SUBLANES):
        n = ts + (CONV_K - 1 - s) // SUBLANES * SUBLANES
        ph[s, 0:n, :] = buf[pl.ds(base + s, n), ln]


def _tap_rows(ph, off, r0):
    s = off % SUBLANES
    return ph[s, pl.ds(r0 + off - s, TAP_ROWS), :]


def _conv_branch_fwd(proj, cw, cb, lg, lb, *, nb, seq):
    ts, c, halo = CONV_ROWS, D_CONV, HALO31
    ns = seq // ts
    base = halo - CONV_K + 1

    def body(ca_ref, cg_ref, w_ref, b_ref, lg_ref, lb_ref, u1_ref, u_ref, ubuf, uph):
        i = pl.program_id(1)

        @pl.when(i == 0)
        def _():
            ubuf[0:halo, :] = jnp.zeros((halo, c), F32)

        @pl.when(i > 0)
        def _():
            ubuf[0:halo, :] = ubuf[ts:ts + halo, :]

        ubuf[halo:halo + ts, :] = ca_ref[...] * _sig(cg_ref[...])

        def lane_tile(j, carry):
            ln = pl.ds(pl.multiple_of(j * LANES, LANES), LANES)
            _phase_copies(uph, ubuf, ln, base, ts)
            for r in range(ts // TAP_ROWS):
                acc = jnp.broadcast_to(b_ref[:, ln], (TAP_ROWS, LANES))
                for k in range(CONV_K):
                    acc = acc + w_ref[pl.ds(k, 1), ln] * _tap_rows(uph, k, r * TAP_ROWS)
                u1_ref[pl.ds(r * TAP_ROWS, TAP_ROWS), ln] = acc
            return carry

        lax.fori_loop(0, c // LANES, lane_tile, 0)
        v = u1_ref[...]
        mu = jnp.mean(v, axis=-1, keepdims=True)
        dv = v - mu
        xh = dv * lax.rsqrt(jnp.mean(dv * dv, axis=-1, keepdims=True) + EPS)
        u2 = xh * lg_ref[...] + lb_ref[...]
        u_ref[...] = (u2 * _sig(u2)).astype(BF16)

    t = nb * seq
    row = lambda col: pl.BlockSpec((ts, c), lambda b, i: (b * ns + i, col))
    vec = pl.BlockSpec((1, c), lambda b, i: (0, 0))
    return pl.pallas_call(
        body,
        grid=(nb, ns),
        in_specs=[row(0), row(1), pl.BlockSpec((32, c), lambda b, i: (0, 0)), vec, vec, vec],
        out_specs=[row(0), row(0)],
        out_shape=[jax.ShapeDtypeStruct((t, c), F32), jax.ShapeDtypeStruct((t, c + D_SSD), BF16)],
        scratch_shapes=[pltpu.VMEM((halo + ts, c), F32), _phase_scratch(ts)],
        compiler_params=_cparams(("parallel", "arbitrary")),
        name="conv_branch_fwd",
    )(proj, proj, cw, cb, lg, lb)


def _conv_branch_bwd(duy, u1, proj, cw, lg, lb, *, nb, seq):
    ts, c, halo = CONV_ROWS, D_CONV, HALO31
    ns = seq // ts
    base = halo - CONV_K + 1
    hb = ts // halo

    def body(du_ref, u1_ref, ca_ref, cg_ref, cah_ref, cgh_ref, w_ref, lg_ref, lb_ref,
             dcacg_ref, dw_ref, db_ref, dlg_ref, dlb_ref,
             ubuf, dbuf, du0buf, dwacc, dbacc, dlgacc, dlbacc, uph, dph):
        b, i = pl.program_id(0), pl.program_id(1)
        rc = ns - 1 - i

        @pl.when(jnp.logical_and(b == 0, i == 0))
        def _():
            dwacc[...] = jnp.zeros_like(dwacc)
            dbacc[...] = jnp.zeros_like(dbacc)
            dlgacc[...] = jnp.zeros_like(dlgacc)
            dlbacc[...] = jnp.zeros_like(dlbacc)

        @pl.when(i == 0)
        def _():
            dbuf[ts:ts + halo, :] = jnp.zeros((halo, c), F32)

        @pl.when(i > 0)
        def _():
            dbuf[ts:ts + halo, :] = dbuf[0:halo, :]

        v = u1_ref[...]
        mu = jnp.mean(v, axis=-1, keepdims=True)
        dv = v - mu
        rstd = lax.rsqrt(jnp.mean(dv * dv, axis=-1, keepdims=True) + EPS)
        xh = dv * rstd
        lgv = lg_ref[...]
        u2 = xh * lgv + lb_ref[...]
        s2 = _sig(u2)
        du2 = du_ref[...] * (s2 * (1.0 + u2 * (1.0 - s2)))
        dlgacc[...] += jnp.sum(du2 * xh, axis=0, keepdims=True)
        dlbacc[...] += jnp.sum(du2, axis=0, keepdims=True)
        gd = du2 * lgv
        du1 = rstd * (gd - jnp.mean(gd, axis=-1, keepdims=True) - xh * jnp.mean(gd * xh, axis=-1, keepdims=True))
        dbacc[...] += jnp.sum(du1, axis=0, keepdims=True)
        dbuf[0:ts, :] = du1

        @pl.when(rc == 0)
        def _():
            ubuf[0:halo, :] = jnp.zeros((halo, c), F32)

        @pl.when(rc > 0)
        def _():
            ubuf[0:halo, :] = cah_ref[...] * _sig(cgh_ref[...])

        cav = ca_ref[...]
        sg = _sig(cg_ref[...])
        ubuf[halo:halo + ts, :] = cav * sg

        def lane_tile(j, carry):
            ln = pl.ds(pl.multiple_of(j * LANES, LANES), LANES)
            _phase_copies(uph, ubuf, ln, base, ts)
            _phase_copies(dph, dbuf, ln, 0, ts)
            for r in range(ts // TAP_ROWS):
                r0 = r * TAP_ROWS
                d1 = dbuf[pl.ds(r0, TAP_ROWS), ln]
                acc = jnp.zeros((TAP_ROWS, LANES), F32)
                for k in range(CONV_K):
                    acc = acc + w_ref[pl.ds(k, 1), ln] * _tap_rows(dph, CONV_K - 1 - k, r0)
                    dwacc[pl.ds(k * 8, 8), ln] += _sum8(d1 * _tap_rows(uph, k, r0))
                du0buf[pl.ds(r0, TAP_ROWS), ln] = acc
            return carry

        lax.fori_loop(0, c // LANES, lane_tile, 0)
        du0 = du0buf[...]
        dcacg_ref[:, 0:c] = (du0 * sg).astype(BF16)
        dcacg_ref[:, c:2 * c] = (du0 * cav * sg * (1.0 - sg)).astype(BF16)

        @pl.when(jnp.logical_and(b == nb - 1, i == ns - 1))
        def _():
            for k in range(CONV_K):
                dw_ref[pl.ds(k, 1), :] = jnp.sum(dwacc[pl.ds(k * 8, 8), :], axis=0, keepdims=True)
            dw_ref[pl.ds(CONV_K, 1), :] = jnp.zeros((1, c), F32)
            db_ref[...] = dbacc[...]
            dlg_ref[...] = dlgacc[...]
            dlb_ref[...] = dlbacc[...]

    t = nb * seq
    rowblk = lambda b, i: b * ns + (ns - 1 - i)
    row = lambda col: pl.BlockSpec((ts, c), lambda b, i: (rowblk(b, i), col))
    hrow = lambda col: pl.BlockSpec((halo, c), lambda b, i: (jnp.maximum(rowblk(b, i) * hb - 1, 0), col))
    vec = pl.BlockSpec((1, c), lambda b, i: (0, 0))
    wspec = pl.BlockSpec((32, c), lambda b, i: (0, 0))
    return pl.pallas_call(
        body,
        grid=(nb, ns),
        in_specs=[row(0), row(0), row(0), row(1), hrow(0), hrow(1), wspec, vec, vec],
        out_specs=[pl.BlockSpec((ts, 2 * c), lambda b, i: (rowblk(b, i), 0)), wspec, vec, vec, vec],
        out_shape=[jax.ShapeDtypeStruct((t, D_MAIN), BF16), jax.ShapeDtypeStruct((32, c), F32),
                   jax.ShapeDtypeStruct((1, c), F32), jax.ShapeDtypeStruct((1, c), F32), jax.ShapeDtypeStruct((1, c), F32)],
        scratch_shapes=[pltpu.VMEM((halo + ts, c), F32), pltpu.VMEM((ts + halo, c), F32), pltpu.VMEM((ts, c), F32),
                        pltpu.VMEM((CONV_K * 8, c), F32), pltpu.VMEM((1, c), F32), pltpu.VMEM((1, c), F32),
                        pltpu.VMEM((1, c), F32), _phase_scratch(ts), _phase_scratch(ts)],
        compiler_params=_cparams(("arbitrary", "arbitrary")),
        name="conv_branch_bwd",
    )(duy, u1, proj, proj, proj, proj, cw, lg, lb)


XBC_COL0 = (2 * D_CONV + D_SSD) // 1024


def _ssd_pre_fwd(proj, sw, sb, *, nb, seq):
    ts, c, halo = CONV_ROWS, 1024, HALO4
    ns = seq // ts
    base = halo - SSD_CONV_K + 1

    def body(x_ref, w_ref, b_ref, o_ref, xbuf):
        i = pl.program_id(2)

        @pl.when(i == 0)
        def _():
            xbuf[0:halo, :] = jnp.zeros((halo, c), F32)

        @pl.when(i > 0)
        def _():
            xbuf[0:halo, :] = xbuf[ts:ts + halo, :]

        xbuf[halo:halo + ts, :] = x_ref[...]

        def lane_tile(j, carry):
            ln = pl.ds(pl.multiple_of(j * LANES, LANES), LANES)
            for r in range(ts // TAP_ROWS):
                acc = jnp.broadcast_to(b_ref[:, ln], (TAP_ROWS, LANES))
                for k in range(SSD_CONV_K):
                    acc = acc + w_ref[pl.ds(k, 1), ln] * xbuf[pl.ds(r * TAP_ROWS + base + k, TAP_ROWS), ln]
                o_ref[pl.ds(r * TAP_ROWS, TAP_ROWS), ln] = acc * _sig(acc)
            return carry

        lax.fori_loop(0, c // LANES, lane_tile, 0)

    t = nb * seq
    return pl.pallas_call(
        body,
        grid=(2, nb, ns),
        in_specs=[pl.BlockSpec((ts, c), lambda j, b, i: (b * ns + i, XBC_COL0 + j)),
                  pl.BlockSpec((8, c), lambda j, b, i: (0, j)), pl.BlockSpec((1, c), lambda j, b, i: (0, j))],
        out_specs=pl.BlockSpec((ts, c), lambda j, b, i: (b * ns + i, j)),
        out_shape=jax.ShapeDtypeStruct((t, D_XBC), F32),
        scratch_shapes=[pltpu.VMEM((halo + ts, c), F32)],
        compiler_params=_cparams(("parallel", "parallel", "arbitrary")),
        name="ssd_pre_fwd",
    )(proj, sw, sb)


def _ssd_pre_bwd(dproj, dxs, proj, sw, sb, *, nb, seq):
    ts, c, halo = CONV_ROWS, 1024, HALO4
    ns = seq // ts
    base = halo - SSD_CONV_K + 1
    hb = ts // halo

    def body(dproj_ref, d_ref, x_ref, xh_ref, w_ref, b_ref, dx_ref, dw_ref, db_ref, xbuf, dbuf, dwacc, dbacc):
        b, i = pl.program_id(1), pl.program_id(2)
        rc = ns - 1 - i

        @pl.when(jnp.logical_and(b == 0, i == 0))
        def _():
            dwacc[...] = jnp.zeros_like(dwacc)
            dbacc[...] = jnp.zeros_like(dbacc)

        @pl.when(i == 0)
        def _():
            dbuf[ts:ts + halo, :] = jnp.zeros((halo, c), F32)

        @pl.when(i > 0)
        def _():
            dbuf[ts:ts + halo, :] = dbuf[0:halo, :]

        @pl.when(rc == 0)
        def _():
            xbuf[0:halo, :] = jnp.zeros((halo, c), F32)

        @pl.when(rc > 0)
        def _():
            xbuf[0:halo, :] = xh_ref[...]

        xbuf[halo:halo + ts, :] = x_ref[...]

        def pre_tile(j, carry):
            ln = pl.ds(pl.multiple_of(j * LANES, LANES), LANES)
            for r in range(ts // TAP_ROWS):
                r0 = r * TAP_ROWS
                acc = jnp.broadcast_to(b_ref[:, ln], (TAP_ROWS, LANES))
                for k in range(SSD_CONV_K):
                    acc = acc + w_ref[pl.ds(k, 1), ln] * xbuf[pl.ds(r0 + base + k, TAP_ROWS), ln]
                s = _sig(acc)
                dc = d_ref[pl.ds(r0, TAP_ROWS), ln] * (s * (1.0 + acc * (1.0 - s)))
                dbuf[pl.ds(r0, TAP_ROWS), ln] = dc
                dbacc[:, ln] += _sum8(dc)
            return carry

        lax.fori_loop(0, c // LANES, pre_tile, 0)

        def lane_tile(j, carry):
            ln = pl.ds(pl.multiple_of(j * LANES, LANES), LANES)
            for r in range(ts // TAP_ROWS):
                r0 = r * TAP_ROWS
                d1 = dbuf[pl.ds(r0, TAP_ROWS), ln]
                acc = jnp.zeros((TAP_ROWS, LANES), F32)
                for k in range(SSD_CONV_K):
                    acc = acc + w_ref[pl.ds(k, 1), ln] * dbuf[pl.ds(r0 + SSD_CONV_K - 1 - k, TAP_ROWS), ln]
                    dwacc[pl.ds(k * 8, 8), ln] += _sum8(d1 * xbuf[pl.ds(r0 + base + k, TAP_ROWS), ln])
                dx_ref[pl.ds(r0, TAP_ROWS), ln] = acc.astype(BF16)
            return carry

        lax.fori_loop(0, c // LANES, lane_tile, 0)

        @pl.when(jnp.logical_and(b == nb - 1, i == ns - 1))
        def _():
            for k in range(SSD_CONV_K):
                dw_ref[pl.ds(k, 1), :] = jnp.sum(dwacc[pl.ds(k * 8, 8), :], axis=0, keepdims=True)
            dw_ref[pl.ds(SSD_CONV_K, 8 - SSD_CONV_K), :] = jnp.zeros((8 - SSD_CONV_K, c), F32)
            db_ref[...] = jnp.sum(dbacc[...], axis=0, keepdims=True)

    t = nb * seq
    rowblk = lambda b, i: b * ns + (ns - 1 - i)
    return pl.pallas_call(
        body,
        grid=(2, nb, ns),
        in_specs=[DEP_SPEC, pl.BlockSpec((ts, c), lambda j, b, i: (rowblk(b, i), j)),
                  pl.BlockSpec((ts, c), lambda j, b, i: (rowblk(b, i), XBC_COL0 + j)),
                  pl.BlockSpec((halo, c), lambda j, b, i: (jnp.maximum(rowblk(b, i) * hb - 1, 0), XBC_COL0 + j)),
                  pl.BlockSpec((8, c), lambda j, b, i: (0, j)), pl.BlockSpec((1, c), lambda j, b, i: (0, j))],
        out_specs=[pl.BlockSpec((ts, c), lambda j, b, i: (rowblk(b, i), XBC_COL0 + j)),
                   pl.BlockSpec((8, c), lambda j, b, i: (0, j)), pl.BlockSpec((1, c), lambda j, b, i: (0, j))],
        out_shape=[jax.ShapeDtypeStruct(dproj.shape, BF16), jax.ShapeDtypeStruct((8, D_XBC), F32),
                   jax.ShapeDtypeStruct((1, D_XBC), F32)],
        input_output_aliases={0: 0},
        scratch_shapes=[pltpu.VMEM((halo + ts, c), F32), pltpu.VMEM((ts + halo, c), F32),
                        pltpu.VMEM((SSD_CONV_K * 8, c), F32), pltpu.VMEM((8, c), F32)],
        compiler_params=_cparams(("arbitrary", "arbitrary", "arbitrary")),
        name="ssd_pre_bwd",
    )(dproj, dxs, proj, proj, sw, sb)


Z_COL = (2 * D_CONV) // 1024
GROUP_W = D_SSD // GROUPS


def _softplus(v):
    return jnp.maximum(v, 0.0) + jnp.log(1.0 + jnp.exp(-jnp.abs(v)))


def _dot(a, b):
    return jnp.dot(a, b, preferred_element_type=F32)


def _dot_nt(a, b):
    return lax.dot_general(a, b, (_DIMS["nt"], ((), ())), preferred_element_type=F32)


def _dot_tn(a, b):
    return lax.dot_general(a, b, (_DIMS["tn"], ((), ())), preferred_element_type=F32)


def _bf16_terms(v):
    hi = v.astype(BF16)
    r1 = v - hi.astype(F32)
    mid = r1.astype(BF16)
    return hi, mid, (r1 - mid.astype(F32)).astype(BF16)


def _dot_exact_left(sel, v):
    hi, mid, lo = _bf16_terms(v)
    return _dot(sel, hi) + (_dot(sel, mid) + _dot(sel, lo))


def _dot_exact_right(v, sel):
    hi, mid, lo = _bf16_terms(v)
    return _dot(hi, sel) + (_dot(mid, sel) + _dot(lo, sel))


def _chunk_decays(dtr_ref, bias_ref, alog_ref):
    q = CHUNK
    ii = lax.broadcasted_iota(jnp.int32, (q, q), 0)
    jj = lax.broadcasted_iota(jnp.int32, (q, q), 1)
    tri = jj <= ii
    dt = _softplus(dtr_ref[...] + bias_ref[...])
    a_head = -jnp.exp(alog_ref[...])
    cs = _dot_exact_left(tri.astype(BF16), dt * a_head)
    return tri, dt, a_head, cs, cs.T


def _ssd_fwd(uy, xs_all, proj, dtr, dt_bias, a_log, d_lanes, norm_w, *, nb, seq):
    q = CHUNK
    nc = seq // q
    t = nb * seq

    def body(uy_ref, xs_ref, bm_ref, cm_ref, z_ref, dtr_ref, bias_ref, alog_ref, dl_ref, nw_ref,
             y_ref, ys_ref, st_ref, state):
        @pl.when(pl.program_id(1) == 0)
        def _():
            state[...] = jnp.zeros_like(state)

        tri, dt, _, cs, cst = _chunk_decays(dtr_ref, bias_ref, alog_ref)
        first = lax.broadcasted_iota(jnp.int32, (1, LANES), 1) < HEAD_DIM
        for g in range(GROUPS):
            gl = slice(g * STATE, (g + 1) * STATE)
            bb = bm_ref[:, gl].astype(BF16)
            cb = cm_ref[:, gl].astype(BF16)
            scores = _dot_nt(cb, bb)
            for p in range(2):
                pr = 2 * g + p
                h0 = 2 * pr
                sl = slice(pr * LANES, (pr + 1) * LANES)
                xv = xs_ref[:, sl]
                dtp = jnp.where(first, dt[:, h0:h0 + 1], dt[:, h0 + 1:h0 + 2])
                csp = jnp.where(first, cs[:, h0:h0 + 1], cs[:, h0 + 1:h0 + 2])
                xd = xv * dtp
                yv = None
                for hh, keep in ((h0, first), (h0 + 1, jnp.logical_not(first))):
                    decay = jnp.where(tri, jnp.exp(cs[:, hh:hh + 1] - cst[hh:hh + 1, :]), 0.0)
                    part = _dot((scores * decay).astype(BF16), jnp.where(keep, xd, 0.0).astype(BF16))
                    yv = part if yv is None else yv + part
                hp = state[pr]
                st_ref[0, pr] = hp
                yv = yv + jnp.exp(csp) * _dot(cb, hp.astype(BF16))
                last = csp[q - 1:q, :]
                state[pr] = jnp.exp(last) * hp + _dot_tn(bb, (xd * jnp.exp(last - csp)).astype(BF16))
                ys_ref[:, sl] = yv + dl_ref[:, sl] * xv
        zv = z_ref[...]
        gated = ys_ref[...] * (zv * _sig(zv))
        for g in range(GROUPS):
            gl = slice(g * GROUP_W, (g + 1) * GROUP_W)
            v = gated[:, gl]
            r = lax.rsqrt(jnp.mean(v * v, axis=-1, keepdims=True) + EPS)
            y_ref[:, gl] = (v * r * nw_ref[:, gl]).astype(BF16)

    blk = lambda w, col: pl.BlockSpec((q, w), lambda b, c: (b * nc + c, col))
    vec = lambda w: pl.BlockSpec((1, w), lambda b, c: (0, 0))
    return pl.pallas_call(
        body,
        grid=(nb, nc),
        in_specs=[DEP_SPEC, blk(D_SSD, 0), blk(GROUPS * STATE, 2), blk(GROUPS * STATE, 3), blk(D_SSD, Z_COL),
                  blk(LANES, 0), vec(LANES), vec(LANES), vec(D_SSD), vec(D_SSD)],
        out_specs=[blk(D_SSD, 1), blk(D_SSD, 0),
                   pl.BlockSpec((1, PAIRS, STATE, LANES), lambda b, c: (b * nc + c, 0, 0, 0))],
        out_shape=[jax.ShapeDtypeStruct(uy.shape, BF16), jax.ShapeDtypeStruct((t, D_SSD), F32),
                   jax.ShapeDtypeStruct((nb * nc, PAIRS, STATE, LANES), F32)],
        input_output_aliases={0: 0},
        scratch_shapes=[pltpu.VMEM((PAIRS, STATE, LANES), F32)],
        compiler_params=_cparams(("parallel", "arbitrary")),
        name="ssd_fwd",
    )(uy, xs_all, xs_all, xs_all, proj, dtr, dt_bias, a_log, d_lanes, norm_w)


def _ssd_bwd(dproj, duy, proj, ys, xs_all, dtr, states, dt_bias, a_log, d_lanes, norm_w, *, nb, seq):
    q = CHUNK
    nc = seq // q
    t = nb * seq
    head_of_lane = (jnp.arange(D_SSD)[:, None] // HEAD_DIM == jnp.arange(LANES)[None, :]).astype(BF16)

    def body(dproj_ref, dy_ref, z_ref, ys_ref, xs_ref, bm_ref, cm_ref, dtr_ref, st_ref, bias_ref, alog_ref, dl_ref,
             nw_ref, sel_ref, dz_ref, dx_ref, ddtr_ref, small_ref,
             dstate, dys_buf, dcsl, ddtl, dcst, dnw_acc, dd_acc, dbias_acc, da_acc):
        b, c = pl.program_id(0), pl.program_id(1)

        @pl.when(jnp.logical_and(b == 0, c == 0))
        def _():
            dnw_acc[...] = jnp.zeros_like(dnw_acc)
            dd_acc[...] = jnp.zeros_like(dd_acc)
            dbias_acc[...] = jnp.zeros_like(dbias_acc)
            da_acc[...] = jnp.zeros_like(da_acc)
            dcst[...] = jnp.zeros_like(dcst)

        @pl.when(c == 0)
        def _():
            dstate[...] = jnp.zeros_like(dstate)

        zv = z_ref[...]
        sz = _sig(zv)
        silz = zv * sz
        ysv = ys_ref[...]
        gated = ysv * silz
        dyv = dy_ref[...]
        nwv = nw_ref[...]
        for g in range(GROUPS):
            gl = slice(g * GROUP_W, (g + 1) * GROUP_W)
            v = gated[:, gl]
            r = lax.rsqrt(jnp.mean(v * v, axis=-1, keepdims=True) + EPS)
            yn = v * r
            dyn = dyv[:, gl] * nwv[:, gl]
            dnw_acc[:, gl] += jnp.sum(dyv[:, gl] * yn, axis=0, keepdims=True)
            dys_buf[:, gl] = r * (dyn - yn * jnp.mean(dyn * yn, axis=-1, keepdims=True))
        dgated = dys_buf[...]
        dz_ref[...] = (dgated * ysv * (sz * (1.0 + zv * (1.0 - sz)))).astype(BF16)
        dys_all = dgated * silz
        dys_buf[...] = dys_all
        dd_acc[...] += jnp.sum(dys_all * xs_ref[...], axis=0, keepdims=True)

        tri, dt, a_head, cs, cst = _chunk_decays(dtr_ref, bias_ref, alog_ref)
        lane = lax.broadcasted_iota(jnp.int32, (1, LANES), 1)
        first = lane < HEAD_DIM
        dcs_h = jnp.zeros((q, LANES), F32)
        for g in range(GROUPS):
            gl = slice(g * STATE, (g + 1) * STATE)
            bb = bm_ref[:, gl].astype(BF16)
            cb = cm_ref[:, gl].astype(BF16)
            scores = _dot_nt(cb, bb)
            dscores = jnp.zeros((q, q), F32)
            dbg = jnp.zeros((q, STATE), F32)
            dcg = jnp.zeros((q, STATE), F32)
            for p in range(2):
                pr = 2 * g + p
                h0 = 2 * pr
                sl = slice(pr * LANES, (pr + 1) * LANES)
                xv = xs_ref[:, sl]
                dyp = dys_buf[:, sl]
                dtp = jnp.where(first, dt[:, h0:h0 + 1], dt[:, h0 + 1:h0 + 2])
                csp = jnp.where(first, cs[:, h0:h0 + 1], cs[:, h0 + 1:h0 + 2])
                xd = xv * dtp
                xdb = xd.astype(BF16)
                hp = st_ref[0, pr]
                dhn = dstate[pr]
                hpb = hp.astype(BF16)
                dhnb = dhn.astype(BF16)
                lam = jnp.exp(csp)
                last = csp[q - 1:q, :]
                gam = jnp.exp(last)
                w = jnp.exp(last - csp)
                dxd = jnp.zeros((q, LANES), F32)
                for hh, keep in ((h0, first), (h0 + 1, jnp.logical_not(first))):
                    decay = jnp.where(tri, jnp.exp(cs[:, hh:hh + 1] - cst[hh:hh + 1, :]), 0.0)
                    m = scores * decay
                    dym = jnp.where(keep, dyp, 0.0).astype(BF16)
                    dm = _dot_nt(dym, xdb)
                    dxd = dxd + _dot_tn(m.astype(BF16), dym)
                    e = dm * m
                    dcs_h = dcs_h + jnp.where(lane == hh, jnp.sum(e, axis=1, keepdims=True), 0.0)
                    dcst[hh:hh + 1, :] = jnp.sum(e, axis=0, keepdims=True)
                    dscores = dscores + dm * decay
                yoff = lam * _dot(cb, hpb)
                ldy = (lam * dyp).astype(BF16)
                dcg = dcg + _dot_nt(ldy, hpb)
                dstate[pr] = gam * dhn + _dot_tn(cb, ldy)
                bdh = _dot(bb, dhnb)
                dxd = dxd + w * bdh
                xdw = xd * w
                dbg = dbg + _dot_nt(xdw.astype(BF16), dhnb)
                wd = xdw * bdh
                dcsl[:, sl] = dyp * yoff - wd
                dcsl[q - 1:q, sl] += (jnp.sum(wd, axis=0, keepdims=True)
                                      + gam * jnp.sum(dhn * hp, axis=0, keepdims=True))
                dx_ref[:, sl] = dxd * dtp + dyp * dl_ref[:, sl]
                ddtl[:, sl] = dxd * xv
            dsb = dscores.astype(BF16)
            dx_ref[:, D_SSD + g * STATE:D_SSD + (g + 1) * STATE] = dbg + _dot_tn(dsb, cb)
            dx_ref[:, D_SSD + (GROUPS + g) * STATE:D_SSD + (GROUPS + g + 1) * STATE] = dcg + _dot(dsb, bb)

        sel = sel_ref[...]
        dcs_h = dcs_h + _dot_exact_right(dcsl[...], sel) - dcst[...].T
        ddt = _dot_exact_right(ddtl[...], sel)
        upper = lax.broadcasted_iota(jnp.int32, (q, q), 1) >= lax.broadcasted_iota(jnp.int32, (q, q), 0)
        da = _dot_exact_left(upper.astype(BF16), dcs_h)
        ddt = ddt + da * a_head
        da_acc[...] += jnp.sum(da * dt, axis=0, keepdims=True)
        ddtr = ddt * _sig(dtr_ref[...] + bias_ref[...])
        ddtr_ref[...] = ddtr
        dbias_acc[...] += jnp.sum(ddtr, axis=0, keepdims=True)

        @pl.when(jnp.logical_and(b == nb - 1, c == nc - 1))
        def _():
            small_ref[...] = jnp.zeros_like(small_ref)
            small_ref[0:1, :] = dnw_acc[...]
            small_ref[1:2, 0:LANES] = _dot_exact_right(jnp.broadcast_to(dd_acc[...], (8, D_SSD)), sel)[0:1, :]
            small_ref[2:3, 0:LANES] = dbias_acc[...]
            small_ref[3:4, 0:LANES] = da_acc[...] * a_head

    rowblk = lambda b, c: b * nc + (nc - 1 - c)
    blk = lambda w, col: pl.BlockSpec((q, w), lambda b, c: (rowblk(b, c), col))
    vec = lambda w: pl.BlockSpec((1, w), lambda b, c: (0, 0))
    return pl.pallas_call(
        body,
        grid=(nb, nc),
        in_specs=[DEP_SPEC, blk(D_SSD, 1), blk(D_SSD, Z_COL), blk(D_SSD, 0), blk(D_SSD, 0), blk(GROUPS * STATE, 2),
                  blk(GROUPS * STATE, 3), blk(LANES, 0),
                  pl.BlockSpec((1, PAIRS, STATE, LANES), lambda b, c: (rowblk(b, c), 0, 0, 0)),
                  vec(LANES), vec(LANES), vec(D_SSD), vec(D_SSD), pl.BlockSpec((D_SSD, LANES), lambda b, c: (0, 0))],
        out_specs=[blk(D_SSD, Z_COL), blk(D_XBC, 0), blk(LANES, 0), pl.BlockSpec((8, D_SSD), lambda b, c: (0, 0))],
        out_shape=[jax.ShapeDtypeStruct(dproj.shape, BF16), jax.ShapeDtypeStruct((t, D_XBC), F32),
                   jax.ShapeDtypeStruct((t, LANES), F32), jax.ShapeDtypeStruct((8, D_SSD), F32)],
        input_output_aliases={0: 0},
        scratch_shapes=[pltpu.VMEM((PAIRS, STATE, LANES), F32), pltpu.VMEM((q, D_SSD), F32),
                        pltpu.VMEM((q, D_SSD), F32), pltpu.VMEM((q, D_SSD), F32), pltpu.VMEM((LANES, q), F32),
                        pltpu.VMEM((1, D_SSD), F32), pltpu.VMEM((1, D_SSD), F32), pltpu.VMEM((1, LANES), F32),
                        pltpu.VMEM((1, LANES), F32)],
        compiler_params=_cparams(("arbitrary", "arbitrary")),
        name="ssd_bwd",
    )(dproj, duy, proj, ys, xs_all, xs_all, xs_all, dtr, states, dt_bias, a_log, d_lanes, norm_w, head_of_lane)


HBM_SPEC = pl.BlockSpec(memory_space=pltpu.HBM)
MESH_ID = pl.DeviceIdType.MESH


def _coords():
    return lax.axis_index("x"), lax.axis_index("y"), lax.axis_index("c")


def _chip_peer(xi, yi, ci, d):
    return (jnp.bitwise_xor(xi, d >> 1), jnp.bitwise_xor(yi, d & 1), ci)


def _remote(src, dst, send_sem, recv_sem, peer):
    return pltpu.make_async_remote_copy(src_ref=src, dst_ref=dst, send_sem=send_sem, recv_sem=recv_sem,
                                        device_id=peer, device_id_type=MESH_ID)


SEM_SPEC = pl.BlockSpec(memory_space=pltpu.SEMAPHORE)
ANY_SPEC = pl.BlockSpec(memory_space=pl.ANY)
EFFECT = pltpu.SideEffectType.DATAFLOW_SIDE_EFFECTING
COPIES = 3


def _half(ref, axis, which, lead=0):
    size = ref.shape[lead + axis] // 2
    part = pl.ds(which * size, size)
    idx = (slice(None),) * lead + ((part, slice(None)) if axis == 0 else (slice(None), part))
    return ref.at[idx]


def _halved_shape(shape, axis):
    lead = len(shape) - 2
    return tuple(d // 2 if i == lead + axis else d for i, d in enumerate(shape))


def _gather_plan(axis):
    def plan(xi, yi, ci, src, land):
        me = 2 * xi + yi
        out = []
        for d in (1, 2, 3):
            there = jnp.bitwise_xor(me, d)
            if axis is None:
                out.append((src, land.at[me], _chip_peer(xi, yi, ci, d), land.at[there]))
            else:
                out.append((_half(src, axis, ci), _half(land.at[me], axis, ci), _chip_peer(xi, yi, ci, d),
                            _half(land.at[there], axis, ci)))
        return out
    return plan


def _halves_plan(axis):
    def plan(xi, yi, ci, src, land):
        return [(_half(src, axis, 1 - ci, lead=1), land, (xi, yi, 1 - ci), land)]
    return plan


def _forward_plan(axis):
    def plan(xi, yi, ci, src, land):
        me = 2 * xi + yi
        out = []
        for d in (1, 2, 3):
            slot = land.at[jnp.bitwise_xor(me, d)]
            out.append((_half(slot, axis, ci), _half(slot, axis, ci), (xi, yi, 1 - ci), _half(slot, axis, 1 - ci)))
        return out
    return plan


def _sibling_plan(xi, yi, ci, src, land):
    return [(src, land, (xi, yi, 1 - ci), land)]


def _owners_plan(xi, yi, ci, src, land):
    me = 2 * xi + yi
    return [(src.at[jnp.bitwise_xor(me, d)], land.at[d - 1], _chip_peer(xi, yi, ci, d), land.at[d - 1])
            for d in (1, 2, 3)]


def _split_start(srcs, lands, plans, *, name, dep=None):
    n = len(srcs)
    deps = [] if dep is None else [dep]

    def body(*refs):
        src_refs, land_refs = refs[:n], refs[n:2 * n]
        outs = refs[2 * n + len(deps):]
        ssems, rsems = outs[:n], outs[n:2 * n]
        token = refs[-1]
        xi, yi, ci = _coords()
        for t in range(n):
            for k, (src, dst, peer, _) in enumerate(plans[t](xi, yi, ci, src_refs[t], land_refs[t])):
                _remote(src, dst, ssems[t].at[k], rsems[t].at[k], peer).start()
        token[...] = jnp.zeros_like(token)

    bufs = list(srcs) + list(lands)
    outs = pl.pallas_call(
        body,
        name=name,
        in_specs=[HBM_SPEC] * (2 * n) + [ANY_SPEC] * len(deps),
        out_specs=[SEM_SPEC] * (2 * n) + [HBM_SPEC] * (2 * n) + [pl.BlockSpec(memory_space=pltpu.VMEM)],
        out_shape=[pltpu.SemaphoreType.DMA((COPIES,))] * (2 * n) + [pltpu.HBM(a.shape, a.dtype) for a in bufs]
        + [jax.ShapeDtypeStruct((8, LANES), F32)],
        input_output_aliases={i: 2 * n + i for i in range(2 * n)},
        compiler_params=pltpu.CompilerParams(has_side_effects=EFFECT),
    )(*[pltpu.with_memory_space_constraint(a, pltpu.HBM) for a in bufs], *deps)
    return outs[:n], outs[n:2 * n], outs[2 * n:3 * n], outs[3 * n:4 * n], outs[-1]


def _split_wait(ssems, rsems, srcs, lands, plans, after, *, name):
    n = len(srcs)

    def body(*refs):
        src_refs, land_refs = refs[:n], refs[n:2 * n]
        ss, rs = refs[2 * n:3 * n], refs[3 * n:4 * n]
        xi, yi, ci = _coords()
        for t in range(n):
            for k, (src, _, peer, landed) in enumerate(plans[t](xi, yi, ci, src_refs[t], land_refs[t])):
                cp = _remote(src, landed, ss[t].at[k], rs[t].at[k], peer)
                cp.wait_send()
                cp.wait_recv()

    bufs = list(srcs) + list(lands)
    outs = pl.pallas_call(
        body,
        name=name,
        in_specs=[HBM_SPEC] * (2 * n) + [SEM_SPEC] * (2 * n) + [ANY_SPEC],
        out_specs=[HBM_SPEC] * (2 * n),
        out_shape=[pltpu.HBM(a.shape, a.dtype) for a in bufs],
        input_output_aliases={i: i for i in range(2 * n)},
        compiler_params=pltpu.CompilerParams(has_side_effects=EFFECT),
    )(*bufs, *ssems, *rsems, after)
    return outs[:n], outs[n:]


def _forward_halves(lands, axes, *, name):
    n = len(lands)

    def body(*refs):
        ins, outs = refs[:n], refs[n:2 * n]
        send_sems, recv_sems = refs[2 * n:]
        xi, yi, ci = _coords()
        me = 2 * xi + yi
        sibling = (xi, yi, 1 - ci)
        cps = []
        for t in range(n):
            for d in (1, 2, 3):
                slot = jnp.bitwise_xor(me, d)
                k = COPIES * t + d - 1
                cp = _remote(_half(ins[t].at[slot], axes[t], ci), _half(outs[t].at[slot], axes[t], ci),
                             send_sems.at[k], recv_sems.at[k], sibling)
                cp.start()
                cps.append(cp)
        for t in range(n):
            for d in (1, 2, 3):
                got = _half(outs[t].at[jnp.bitwise_xor(me, d)], axes[t], 1 - ci)
                k = COPIES * t + d - 1
                _remote(got, got, send_sems.at[k], recv_sems.at[k], sibling).wait_recv()
        for cp in cps:
            cp.wait_send()

    return pl.pallas_call(
        body,
        name=name,
        in_specs=[HBM_SPEC] * n,
        out_specs=[HBM_SPEC] * n,
        out_shape=[jax.ShapeDtypeStruct(a.shape, a.dtype) for a in lands],
        input_output_aliases={i: i for i in range(n)},
        scratch_shapes=[pltpu.SemaphoreType.DMA((COPIES * n,)), pltpu.SemaphoreType.DMA((COPIES * n,))],
    )(*lands)


def _swap_other_halves(gs, axes, *, name):
    n = len(gs)

    def body(*refs):
        ins, lands = refs[:n], refs[n:2 * n]
        send_sems, recv_sems = refs[2 * n:]
        xi, yi, ci = _coords()
        sibling = (xi, yi, 1 - ci)
        cps = []
        for t in range(n):
            cp = _remote(_half(ins[t], axes[t], 1 - ci, lead=1), lands[t], send_sems.at[t], recv_sems.at[t], sibling)
            cp.start()
            cps.append(cp)
        for cp in cps:
            cp.wait_recv()
        for cp in cps:
            cp.wait_send()

    return pl.pallas_call(
        body,
        in_specs=[HBM_SPEC] * n,
        out_specs=[HBM_SPEC] * n,
        out_shape=[jax.ShapeDtypeStruct(_halved_shape(g.shape, ax), g.dtype) for g, ax in zip(gs, axes)],
        scratch_shapes=[pltpu.SemaphoreType.DMA((n,)), pltpu.SemaphoreType.DMA((n,))],
        name=name,
    )(*gs)


def _row_tile(rows, cap=512, mult=16):
    best = mult
    for cand in range(mult, min(rows, cap) + 1, mult):
        if rows % cand == 0:
            best = cand
    assert rows % best == 0, rows
    return best


COL_TILE = 256


def _half_tiles(hr, hc, axis, cap=512, mult=16):
    if axis == 0:
        tr = _row_tile(hr, cap, mult)
        n = hr // tr
        return (tr, hc), n, lambda half, i: (half * n + i, 0)
    n = hc // COL_TILE
    return (hr, COL_TILE), n, lambda half, i: (0, half * n + i)


def _add_core_halves(g, land, where, axis):
    nslot, hr, hc = land.shape
    bshape, nr, idx = _half_tiles(hr, hc, axis)

    def body(where_ref, g_ref, l_ref, f_ref, b_ref):
        s = g_ref[...] + l_ref[...].astype(F32)
        b_ref[...] = s.astype(BF16)

        @pl.when(pl.program_id(1) == where_ref[1])
        def _():
            f_ref[...] = s

    blk = pl.BlockSpec((None,) + bshape, lambda i, s, w: (s,) + idx(0, i))
    mine = pl.BlockSpec((None,) + bshape, lambda i, s, w: (s,) + idx(w[0], i))
    return pl.pallas_call(
        body,
        grid_spec=pltpu.PrefetchScalarGridSpec(
            num_scalar_prefetch=1,
            grid=(nr, nslot),
            in_specs=[mine, blk],
            out_specs=[pl.BlockSpec(bshape, lambda i, s, w: idx(0, i)), blk],
        ),
        out_shape=[jax.ShapeDtypeStruct((hr, hc), F32), jax.ShapeDtypeStruct(land.shape, BF16)],
        compiler_params=_cparams(("parallel", "arbitrary")),
        name="add_core_halves",
    )(where, g, land)


def _add_chip_sums(pf, land, where, axis):
    hr, cols = pf.shape
    bshape, nr, idx = _half_tiles(hr, cols, axis)

    def body(where_ref, p_ref, l_ref, o_ref):
        acc = p_ref[...]
        for d in range(3):
            acc = acc + l_ref[d].astype(F32)
        o_ref[...] = acc

    return pl.pallas_call(
        body,
        grid_spec=pltpu.PrefetchScalarGridSpec(
            num_scalar_prefetch=1,
            grid=(nr,),
            in_specs=[pl.BlockSpec(bshape, lambda i, w: idx(0, i)),
                      pl.BlockSpec((3,) + bshape, lambda i, w: (0,) + idx(0, i))],
            out_specs=pl.BlockSpec(bshape, lambda i, w: idx(0, i)),
        ),
        out_shape=jax.ShapeDtypeStruct((hr, cols), F32),
        compiler_params=_cparams(("parallel",)),
        name="add_chip_sums",
    )(where, pf, land)


N_DEV = 8


def _all_reduce_small(part):
    r, w = part.shape

    def body(p_ref, o_ref, gath, send_sems, recv_sems):
        xi, yi, ci = _coords()
        me = 4 * xi + 2 * yi + ci
        gath[me] = p_ref[...]
        cps = []
        for d in range(1, N_DEV):
            peer = (jnp.bitwise_xor(xi, d >> 2), jnp.bitwise_xor(yi, (d >> 1) & 1), jnp.bitwise_xor(ci, d & 1))
            cp = _remote(p_ref, gath.at[me], send_sems.at[d - 1], recv_sems.at[d - 1], peer)
            cp.start()
            cps.append(cp)
        for d in range(1, N_DEV):
            src = gath.at[jnp.bitwise_xor(me, d)]
            _remote(src, src, send_sems.at[d - 1], recv_sems.at[d - 1], (xi, yi, ci)).wait_recv()
        acc = gath[0]
        for k in range(1, N_DEV):
            acc = acc + gath[k]
        o_ref[...] = acc
        for cp in cps:
            cp.wait_send()

    vm = pl.BlockSpec(memory_space=pltpu.VMEM)
    return pl.pallas_call(
        body,
        in_specs=[vm],
        out_specs=vm,
        out_shape=jax.ShapeDtypeStruct((r, w), F32),
        scratch_shapes=[pltpu.VMEM((N_DEV, r, w), F32), pltpu.SemaphoreType.DMA((N_DEV - 1,)),
                        pltpu.SemaphoreType.DMA((N_DEV - 1,))],
        name="all_reduce_small",
    )(part)


def _adamw_math(wv, gv, mv, vv):
    mn = ADAM_B1 * mv + (1.0 - ADAM_B1) * gv
    vn = ADAM_B2 * vv + (1.0 - ADAM_B2) * (gv * gv)
    m_hat = mn / (1.0 - ADAM_B1 ** ADAM_STEP)
    v_hat = vn / (1.0 - ADAM_B2 ** ADAM_STEP)
    return -ADAM_LR * (m_hat / (jnp.sqrt(v_hat) + ADAM_EPS) + ADAM_WD * wv), mn, vn


def _adamw_halves(w, g_mine, g_other, m, v, where, axis, *, name):
    rows, cols = w.shape
    hr, hc = g_mine.shape
    bshape, nr, idx = _half_tiles(hr, hc, axis, cap=256, mult=8)

    def body(where_ref, w_ref, gm_ref, go_ref, m_ref, v_ref, g_ref, d_ref, nm_ref, nv_ref):
        is_mine = pl.program_id(0) == where_ref[0]
        gv = jnp.where(is_mine, gm_ref[...], go_ref[...])
        g_ref[...] = gv
        d_ref[...], nm_ref[...], nv_ref[...] = _adamw_math(w_ref[...], gv, m_ref[...], v_ref[...])

    def parked(half, i, holder):
        return idx(0, jnp.where(half == holder, i, jnp.where(half < holder, 0, nr - 1)))

    blk = pl.BlockSpec(bshape, lambda hf, i, wh: idx(hf, i))
    o = jax.ShapeDtypeStruct((rows, cols), F32)
    return pl.pallas_call(
        body,
        grid_spec=pltpu.PrefetchScalarGridSpec(
            num_scalar_prefetch=1,
            grid=(2, nr),
            in_specs=[blk, pl.BlockSpec(bshape, lambda hf, i, wh: parked(hf, i, wh[0])),
                      pl.BlockSpec(bshape, lambda hf, i, wh: parked(hf, i, 1 - wh[0])), blk, blk],
            out_specs=[blk] * 4,
        ),
        out_shape=[o, o, o, o],
        compiler_params=_cparams(("arbitrary", "arbitrary")),
        name=name,
    )(where, w, g_mine, g_other, m, v)


def _adamw(w, g, m, v, *, name):
    rows, cols = w.shape
    tr = _row_tile(rows, cap=256, mult=8)

    def body(w_ref, g_ref, m_ref, v_ref, d_ref, nm_ref, nv_ref):
        d_ref[...], nm_ref[...], nv_ref[...] = _adamw_math(w_ref[...], g_ref[...], m_ref[...], v_ref[...])

    blk = pl.BlockSpec((tr, cols), lambda i: (i, 0))
    o = jax.ShapeDtypeStruct((rows, cols), F32)
    return pl.pallas_call(
        body,
        grid=(rows // tr,),
        in_specs=[blk] * 4,
        out_specs=[blk] * 3,
        out_shape=[o, o, o],
        compiler_params=_cparams(("parallel",)),
        name=name,
    )(w, g, m, v)


def _pack(arrs):
    flat = jnp.concatenate([a.reshape(-1) for a in arrs])
    pad = (-flat.shape[0]) % (8 * LANES)
    return jnp.pad(flat, (0, pad)).reshape(-1, LANES)


def _unpack(packed, shapes):
    flat = packed.reshape(-1)
    out, off = [], 0
    for s in shapes:
        n = 1
        for dim in s:
            n *= dim
        out.append(flat[off:off + n].reshape(s))
        off += n
    return out


def _pad_rows(a, rows):
    return jnp.pad(a, ((0, rows - a.shape[0]), (0, 0)))


def _pad_lanes(a, lanes=LANES):
    return jnp.pad(a, ((0, 0), (0, lanes - a.shape[1])))


def _local_grads(x2d, tgt2d, prm, get_w, on_grads, *, nb, seq, dep=None):
    g_pre, g_post, g_fpre, g_fpost = prm["norm_mix_pre"], prm["norm_mix_post"], prm["norm_ffn_pre"], prm["norm_ffn_post"]
    dt_bias, a_log = _pad_lanes(prm["ssd_dt_bias"]), _pad_lanes(prm["ssd_a_log"])
    d_lanes = jnp.repeat(prm["ssd_d"], HEAD_DIM, axis=1)

    h = _rms_fwd(x2d, g_pre, dep=dep, out_dtype=BF16, name="rms_mix_pre")
    t, d = x2d.shape
    w_in_t, w_dt_t, cw, sw = get_w("in", h)
    proj = _matmul([(h, w_in_t)], mode="nt", out_dtype=F32, tm=1024, tn=1024, tk=2048, name="mm_proj",
                   extent=(t, D_MAIN, d))
    dtr = _matmul([(h, w_dt_t)], mode="nt", out_dtype=F32, tm=1024, tn=128, tk=2048, name="mm_dt")
    u1, uy = _conv_branch_fwd(proj, cw, prm["conv_dw_b"], prm["conv_ln_g"], prm["conv_ln_b"], nb=nb, seq=seq)
    xs_all = _ssd_pre_fwd(proj, sw, prm["ssd_conv_b"], nb=nb, seq=seq)
    uy, ys, states = _ssd_fwd(uy, xs_all, proj, dtr, dt_bias, a_log, d_lanes, prm["ssd_norm_w"], nb=nb, seq=seq)
    w_out = get_w("out", uy)
    mix = _matmul([(uy, w_out)], mode="nn", out_dtype=F32, tm=1024, tn=1024, tk=2048, name="mm_mix")
    x1, h2, h2_t = _rms_post_pre(mix, x2d, g_post, g_fpre)
    w_gate, w_up = get_w("up", h2)
    gt, up, act, act_t = _ffn_up(h2, w_gate, w_up, tm=1024, tn=512)
    w_down = get_w("down", act)
    dx2, df, loss, dg_fpost = _down_loss(act, w_down, x1, tgt2d, g_fpost, tk=1408)

    dgt, dup = _ffn_bwd_act(df, w_down, gt, up, tm=1024, tn=512)
    dw_down = _matmul([(act_t, df)], mode="nn", out_dtype=F32, tm=1408, tn=1024, tk=2048, name="mm_dw_down",
                      also_bf16=True)
    dw_gate = _matmul([(h2_t, dgt)], mode="nn", out_dtype=F32, tm=1024, tn=1408, tk=2048, name="mm_dw_gate",
                      slot_out=True, also_bf16=True)
    dw_up = _matmul([(h2_t, dup)], mode="nn", out_dtype=F32, tm=1024, tn=1408, tk=2048, name="mm_dw_up",
                    slot_out=True, also_bf16=True)
    dep = on_grads("ffn", (dw_down, dw_gate, dw_up))
    dh2 = _matmul([(dgt, w_gate), (dup, w_up)], mode="nt", out_dtype=F32, tm=1024, tn=1024, tk=1408, name="mm_dh2",
                  dep=dep)
    dep = on_grads("ffn_sums", dh2)
    dx1, dmix, dg_fpre, dg_post = _rms_bwd_pre_post(dh2, x1, g_fpre, dx2, mix, g_post, dep=dep)
    dw_out = _matmul([(uy, dmix)], mode="tn", out_dtype=F32, tm=1024, tn=1024, tk=2048, name="mm_dw_out",
                     also_bf16=True)
    dep = on_grads("out", (dw_out,))
    duy = _matmul([(dmix, w_out)], mode="nt", out_dtype=F32, tm=1024, tn=1024, tk=2048, name="mm_duy", dep=dep)
    dproj, dcw, dcb, dlg, dlb = _conv_branch_bwd(duy, u1, proj, cw, prm["conv_ln_g"], prm["conv_ln_b"], nb=nb, seq=seq)
    dproj, dxs, ddtr, ssd_small = _ssd_bwd(dproj, duy, proj, ys, xs_all, dtr, states, dt_bias, a_log, d_lanes,
                                           prm["ssd_norm_w"], nb=nb, seq=seq)
    dproj, dsw, dsb = _ssd_pre_bwd(dproj, dxs, proj, sw, prm["ssd_conv_b"], nb=nb, seq=seq)
    ddtr_b = ddtr.astype(BF16)
    dw_in_t = _matmul([(dproj, h)], mode="tn", out_dtype=F32, tm=1024, tn=1024, tk=2048, name="mm_dw_main",
                      extent=(D_MAIN, d, t), out_rows=D_IN)
    dw_in_t = _dw_dt_rows(dw_in_t, ddtr_b, h)
    dep = on_grads("in", ((dw_in_t, dw_in_t),))
    dx, dg_pre = _dh_dx(dproj, w_in_t, ddtr_b, w_dt_t, x2d, dx1, g_pre, tk=1280, dep=g_pre if dep is None else dep)

    grads = {
        "norm_mix_pre": dg_pre,
        "w_in": dw_in_t,
        "conv_dw_w": dcw[:CONV_K], "conv_dw_b": dcb, "conv_ln_g": dlg, "conv_ln_b": dlb,
        "ssd_conv_w": dsw[:SSD_CONV_K], "ssd_conv_b": dsb,
        "ssd_dt_bias": ssd_small[2:3, :HEADS], "ssd_a_log": ssd_small[3:4, :HEADS], "ssd_d": ssd_small[1:2, :HEADS],
        "ssd_norm_w": ssd_small[0:1],
        "w_out": dw_out[0],
        "norm_mix_post": dg_post, "norm_ffn_pre": dg_fpre,
        "w_gate": dw_gate[0], "w_up": dw_up[0],
        "w_down": dw_down[0], "norm_ffn_post": dg_fpost,
    }
    return loss, dx, grads


BIG = ("w_in", "w_out", "w_gate", "w_up", "w_down")
HALF_AXIS = {"w_in": 1, "w_out": 0, "w_gate": 0, "w_up": 0, "w_down": 0}
GATHER_STAGES = {"in": ("w_in", "conv_dw_w", "ssd_conv_w"), "out": ("w_out",), "up": ("w_gate", "w_up"),
                 "down": ("w_down",)}
GATHER_ORDER = tuple(n for st in ("in", "out", "up", "down") for n in GATHER_STAGES[st])
SMALL = ("norm_mix_pre", "conv_dw_w", "conv_dw_b", "conv_ln_g", "conv_ln_b", "ssd_conv_w", "ssd_conv_b", "ssd_dt_bias",
         "ssd_a_log", "ssd_d", "ssd_norm_w", "norm_mix_post", "norm_ffn_pre", "norm_ffn_post")
WEIGHTS = ("norm_mix_pre", "w_in", "conv_dw_w", "conv_dw_b", "conv_ln_g", "conv_ln_b", "ssd_conv_w", "ssd_conv_b",
           "ssd_dt_bias", "ssd_a_log", "ssd_d", "ssd_norm_w", "w_out", "norm_mix_post", "norm_ffn_pre", "w_gate", "w_up",
           "w_down", "norm_ffn_post")


def _cols_from_slots(a):
    n, rows, w = a.shape
    return a.transpose(1, 0, 2).reshape(rows, n * w)


def kernel(x, norm_mix_pre, w_in, conv_dw_w, conv_dw_b, conv_ln_g, conv_ln_b, ssd_conv_w, ssd_conv_b, ssd_dt_bias, ssd_a_log, ssd_d, ssd_norm_w, w_out, norm_mix_post, norm_ffn_pre, w_gate, w_up, w_down, norm_ffn_post, loss_target, m_norm_mix_pre, m_w_in, m_conv_dw_w, m_conv_dw_b, m_conv_ln_g, m_conv_ln_b, m_ssd_conv_w, m_ssd_conv_b, m_ssd_dt_bias, m_ssd_a_log, m_ssd_d, m_ssd_norm_w, m_w_out, m_norm_mix_post, m_norm_ffn_pre, m_w_gate, m_w_up, m_w_down, m_norm_ffn_post, v_norm_mix_pre, v_w_in, v_conv_dw_w, v_conv_dw_b, v_conv_ln_g, v_conv_ln_b, v_ssd_conv_w, v_ssd_conv_b, v_ssd_dt_bias, v_ssd_a_log, v_ssd_d, v_ssd_norm_w, v_w_out, v_norm_mix_post, v_norm_ffn_pre, v_w_gate, v_w_up, v_w_down, v_norm_ffn_post):
    args = dict(locals())
    two_d = lambda n, a: jnp.swapaxes(a, 1, 2)[0] if n == "w_in" else a.reshape(a.shape[-2:])
    wts = {n: two_d(n, args[n]) for n in WEIGHTS}
    ms = {n: two_d(n, args["m_" + n]) for n in WEIGHTS}
    vs = {n: two_d(n, args["v_" + n]) for n in WEIGHTS}
    nb, seq, d = x.shape
    t = nb * seq
    xi, yi, ci = _coords()
    chip = 2 * xi + yi
    where = jnp.stack([ci, chip]).astype(jnp.int32)

    shards = {n: wts[n].astype(BF16) for n in BIG}
    shards.update(conv_dw_w=_pad_rows(wts["conv_dw_w"], 32), ssd_conv_w=_pad_rows(wts["ssd_conv_w"], 8))
    plans = [_gather_plan(HALF_AXIS.get(n)) for n in GATHER_ORDER]
    n_first = len(GATHER_STAGES["in"])

    def start(part, name, dep=None):
        src = [shards[n] for n in GATHER_ORDER[part]]
        return _split_start(src, [lax.empty((N_CHIPS,) + s.shape, s.dtype) for s in src], plans[part], name=name,
                            dep=dep)

    head = start(slice(0, n_first), "gather_start_in")
    rest = start(slice(n_first, None), "gather_start_rest", dep=head[4])
    ssems, rsems, srcs, lands = (list(h) + list(r) for h, r in zip(head[:4], rest[:4]))
    token = rest[4]

    forwarding = {}

    def arrived(stage, after):
        names = GATHER_STAGES[stage]
        pick = lambda seq_: [seq_[GATHER_ORDER.index(n)] for n in names]
        own, got = _split_wait(pick(ssems), pick(rsems), pick(srcs), pick(lands), pick(plans), after,
                               name="gather_wait_" + stage)
        return dict(zip(names, own)), dict(zip(names, got))

    def get_w(stage, after):
        names = GATHER_STAGES[stage]
        if stage in forwarding:
            own, sems = forwarding.pop(stage)
            fwd_own, fwd_got = _split_wait(*sems[:4], [_forward_plan(HALF_AXIS[n]) for n in names], after,
                                           name="gather_forward_wait_" + stage)
            got = dict(zip(names, fwd_got))
        else:
            own, got = arrived(stage, after)
            big = [n for n in names if n in BIG]
            got.update(zip(big, _forward_halves([got[n] for n in big], [HALF_AXIS[n] for n in big],
                                                name="gather_forward_" + stage)))
        if stage == "out":
            nxt = GATHER_STAGES["up"]
            up_own, up_got = arrived("up", after)
            forwarding["up"] = (up_own, _split_start([up_own[n] for n in nxt], [up_got[n] for n in nxt],
                                                     [_forward_plan(HALF_AXIS[n]) for n in nxt],
                                                     name="gather_forward_start_up"))
        full = {n: lax.dynamic_update_slice(got[n], own[n][None], (chip, 0, 0)) for n in got}
        if stage == "in":
            w_in_t = full["w_in"].reshape(D_IN, D_MODEL)
            return (w_in_t, _pad_rows(w_in_t[D_MAIN:], LANES), _cols_from_slots(full["conv_dw_w"]),
                    _cols_from_slots(full["ssd_conv_w"]))
        if stage == "out":
            return full["w_out"].reshape(D_MODEL, D_MODEL)
        if stage == "up":
            return _cols_from_slots(full["w_gate"]), _cols_from_slots(full["w_up"])
        return full["w_down"].reshape(D_FF, D_MODEL)

    reduce_groups = {"ffn": ("w_down", "w_gate", "w_up"), "out": ("w_out",), "in": ("w_in",)}
    in_flight = {}

    swapping = {}

    def send_to_owners(stage, f32s, kept, axes):
        sums = [_add_core_halves(f32, l, where, ax) for f32, l, ax in zip(f32s, kept, axes)]
        ps = [s[1] for s in sums]
        ssem, rsem, ps, recv, started = _split_start(
            ps, [lax.empty((COPIES,) + p.shape[1:], p.dtype) for p in ps], [_owners_plan] * len(ps),
            name="owners_start_" + stage)
        in_flight[stage] = (ssem, rsem, ps, recv, [s[0] for s in sums])
        return started

    def on_grads(stage, gs):
        if stage == "ffn_sums":
            ssem, rsem, b16s, kept, f32s, axes = swapping.pop("ffn")
            _, kept = _split_wait(ssem, rsem, b16s, kept, [_halves_plan(ax) for ax in axes], gs,
                                  name="swap_other_halves_wait_ffn")
            return send_to_owners("ffn", f32s, kept, axes)
        names = reduce_groups[stage]
        axes = [HALF_AXIS[n] for n in names]
        slot = lambda g: g if g.ndim == 3 else g.reshape((N_CHIPS, g.shape[0] // N_CHIPS, g.shape[1]))
        f32s, b16s = [slot(f32) for f32, _ in gs], [slot(b16) for _, b16 in gs]
        if stage == "ffn":
            kept = [lax.empty(_halved_shape(b.shape, ax), b.dtype) for b, ax in zip(b16s, axes)]
            ssem, rsem, b16s, kept, started = _split_start(b16s, kept, [_halves_plan(ax) for ax in axes],
                                                           name="swap_other_halves_start_ffn")
            swapping["ffn"] = (ssem, rsem, b16s, kept, f32s, axes)
            return started
        return send_to_owners(stage, f32s, _swap_other_halves(b16s, axes, name="swap_other_halves_" + stage), axes)

    prm = {n: wts[n] for n in SMALL}
    loss, dx, grads = _local_grads(x.reshape(t, d), loss_target.reshape(t, d), prm, get_w, on_grads,
                                   nb=nb, seq=seq, dep=token)
    loss = lax.psum(loss[0, 0], MESH_AXES)

    def reduced(stage, after):
        ssem, rsem, ps, recv, own_sums = in_flight[stage]
        _, recv = _split_wait(ssem, rsem, ps, recv, [_owners_plan] * len(ps), after, name="owners_wait_" + stage)
        return {n: _add_chip_sums(f32_sum, r, where, HALF_AXIS[n])
                for n, f32_sum, r in zip(reduce_groups[stage], own_sums, recv)}

    def swap_start(names, name, dep=None):
        mine = [halves[n] for n in names]
        return _split_start(mine, [lax.empty(h.shape, h.dtype) for h in mine], [_sibling_plan] * len(mine), name=name,
                            dep=dep)

    halves = reduced("ffn", dx)
    ffn_swap = swap_start(reduce_groups["ffn"], "swap_reduced_start_ffn")
    halves.update(reduced("out", ffn_swap[4]))
    halves.update(reduced("in", ffn_swap[4]))
    mix_names = reduce_groups["out"] + reduce_groups["in"]
    mix_swap = swap_start(mix_names, "swap_reduced_start_mix", dep=ffn_swap[4])

    small_shapes = [grads[n].shape for n in SMALL]
    small_sum = _unpack(_all_reduce_small(_pack([grads[n] for n in SMALL])), small_shapes)
    small_grads = dict(zip(SMALL, small_sum))
    cwid, swid = D_CONV // N_CHIPS, D_XBC // N_CHIPS
    small_grads["conv_dw_w"] = lax.dynamic_slice(small_grads["conv_dw_w"], (0, chip * cwid), (CONV_K, cwid))
    small_grads["ssd_conv_w"] = lax.dynamic_slice(small_grads["ssd_conv_w"], (0, chip * swid), (SSD_CONV_K, swid))

    out_g, out_d, out_m, out_v = {}, {}, {}, {}
    shard_shapes = [wts[n].shape for n in SMALL]
    pd, pm, pv = _adamw(_pack([wts[n] for n in SMALL]), _pack([small_grads[n] for n in SMALL]),
                        _pack([ms[n] for n in SMALL]), _pack([vs[n] for n in SMALL]), name="adamw_small")
    for n, dd, mm, vv in zip(SMALL, _unpack(pd, shard_shapes), _unpack(pm, shard_shapes), _unpack(pv, shard_shapes)):
        out_g[n], out_d[n], out_m[n], out_v[n] = small_grads[n], dd, mm, vv

    def big_adamw(names, swap, after, name):
        mine, other = _split_wait(*swap[:4], [_sibling_plan] * len(names), after, name=name)
        for n, gm, go in zip(names, mine, other):
            out_g[n], out_d[n], out_m[n], out_v[n] = _adamw_halves(wts[n], gm, go, ms[n], vs[n], where, HALF_AXIS[n],
                                                                     name="adamw_" + n)

    big_adamw(reduce_groups["ffn"], ffn_swap, pd, "swap_reduced_wait_ffn")
    big_adamw(mix_names, mix_swap, out_d[reduce_groups["ffn"][-1]], "swap_reduced_wait_mix")

    back = lambda n, a: jnp.swapaxes(a[None], 1, 2) if n == "w_in" else a.reshape(args[n].shape)
    outs = [back(n, o[n]) for o in (out_g, out_d, out_m, out_v) for n in WEIGHTS]
    return (loss, dx.reshape(nb, seq, d), *outs)
```

```python
import functools

import jax
import jax.numpy as jnp
from jax import lax
from jax.experimental import pallas as pl
from jax.experimental.pallas import tpu as pltpu

F32 = jnp.float32
BF16 = jnp.bfloat16
EPS = 1e-6

D_MODEL = 2048
D_CONV = 1024
D_SSD = 1024
D_XBC = 2048
HEADS = 16
HEAD_DIM = 64
GROUPS = 4
STATE = 128
CONV_K = 31
SSD_CONV_K = 4
D_FF = 5632
D_MAIN = 2 * D_CONV + D_SSD + D_XBC
D_IN = D_MAIN + HEADS
N_CHIPS = 4
LANES = 128
CHUNK = 128
PAIRS = HEADS // 2

ADAM_LR = 0.001
ADAM_B1 = 0.9
ADAM_B2 = 0.999
ADAM_EPS = 1e-08
ADAM_WD = 0.01
ADAM_STEP = 10

MESH_AXES = ("x", "y", "c")
VMEM_LIMIT = 56 * 1024 * 1024


def _sig(v):
    return 1.0 / (1.0 + jnp.exp(-v))


def _cparams(sem, vmem=VMEM_LIMIT):
    return pltpu.CompilerParams(dimension_semantics=sem, vmem_limit_bytes=vmem)


_DIMS = {"nn": ((1,), (0,)), "nt": ((1,), (1,)), "tn": ((0,), (0,))}


def _matmul(pairs, *, mode, out_dtype, tm, tn, tk, name, slot_out=False, extent=None, out_rows=None, dep=None,
            also_bf16=False):
    a0, b0 = pairs[0]
    if mode == "nn":
        (m, k), n = a0.shape, b0.shape[1]
    elif mode == "nt":
        (m, k), n = a0.shape, b0.shape[0]
    else:
        (k, m), n = a0.shape, b0.shape[1]
    if extent is not None:
        m, n, k = extent
    tm, tn, tk = min(tm, m), min(tn, n), min(tk, k)
    assert m % tm == 0 and n % tn == 0 and k % tk == 0, (name, m, n, k, tm, tn, tk)
    nk = k // tk
    npairs = len(pairs)
    deps = [] if dep is None else [dep]
    dims = (_DIMS[mode], ((), ()))

    use_scratch = nk > 1 and out_dtype != F32

    def body(*refs):
        ins, o_ref = refs[: 2 * npairs], refs[2 * npairs + len(deps)]
        dot = lambda p: lax.dot_general(ins[2 * p][...], ins[2 * p + 1][...], dims, preferred_element_type=F32)
        if nk == 1:
            part = dot(0)
            for p in range(1, npairs):
                part = part + dot(p)
            o_ref[...] = part.astype(out_dtype)
            if also_bf16:
                refs[2 * npairs + len(deps) + 1][...] = part.astype(BF16)
            return
        acc = refs[-1] if use_scratch else o_ref
        kk = pl.program_id(2)

        @pl.when(kk == 0)
        def _():
            acc[...] = jnp.zeros_like(acc)

        for p in range(npairs):
            acc[...] += dot(p)

        if use_scratch:
            @pl.when(kk == nk - 1)
            def _():
                o_ref[...] = acc[...].astype(out_dtype)

        if also_bf16:
            @pl.when(kk == nk - 1)
            def _():
                refs[2 * npairs + len(deps) + 1][...] = acc[...].astype(BF16)

    if mode == "nn":
        a_spec = pl.BlockSpec((tm, tk), lambda i, j, kk: (i, kk))
        b_spec = pl.BlockSpec((tk, tn), lambda i, j, kk: (kk, j))
    elif mode == "nt":
        a_spec = pl.BlockSpec((tm, tk), lambda i, j, kk: (i, kk))
        b_spec = pl.BlockSpec((tn, tk), lambda i, j, kk: (j, kk))
    else:
        a_spec = pl.BlockSpec((tk, tm), lambda i, j, kk: (kk, i))
        b_spec = pl.BlockSpec((tk, tn), lambda i, j, kk: (kk, j))
    if slot_out:
        out_shape = jax.ShapeDtypeStruct((n // tn, m, tn), out_dtype)
        out_spec = pl.BlockSpec((None, tm, tn), lambda i, j, kk: (j, i, 0))
    else:
        out_shape = jax.ShapeDtypeStruct((m if out_rows is None else out_rows, n), out_dtype)
        out_spec = pl.BlockSpec((tm, tn), lambda i, j, kk: (i, j))
    flat = [t for ab in pairs for t in ab]
    if also_bf16:
        out_spec = [out_spec, out_spec]
        out_shape = [out_shape, jax.ShapeDtypeStruct(out_shape.shape, BF16)]
    return pl.pallas_call(
        body,
        grid=(m // tm, n // tn, nk),
        in_specs=[a_spec, b_spec] * npairs + [pl.BlockSpec(memory_space=pl.ANY)] * len(deps),
        out_specs=out_spec,
        out_shape=out_shape,
        scratch_shapes=[pltpu.VMEM((tm, tn), F32)] if use_scratch else [],
        compiler_params=_cparams(("parallel", "parallel", "arbitrary")),
        name=name,
    )(*flat, *deps)


SUB_ROWS = 256


def _ffn_up(h2, wg, wu, *, tm, tn):
    t, k = h2.shape
    n = wg.shape[1]
    tm = min(tm, t)
    assert t % tm == 0 and n % tn == 0, (t, n, tm, tn)

    sub = min(SUB_ROWS, tm)

    def body(h_ref, wg_ref, wu_ref, g_ref, u_ref, a_ref, at_ref):
        for r in range(tm // sub):
            rows = pl.ds(r * sub, sub)
            hv = h_ref[rows, :]
            g = jnp.dot(hv, wg_ref[...], preferred_element_type=F32)
            u = jnp.dot(hv, wu_ref[...], preferred_element_type=F32)
            g_ref[rows, :] = g.astype(BF16)
            u_ref[rows, :] = u.astype(BF16)
            act = (g * _sig(g) * u).astype(BF16)
            a_ref[rows, :] = act
            at_ref[:, rows] = act.T

    o = jax.ShapeDtypeStruct((t, n), BF16)
    ospec = pl.BlockSpec((tm, tn), lambda i, j: (i, j))
    return pl.pallas_call(
        body,
        grid=(t // tm, n // tn),
        in_specs=[pl.BlockSpec((tm, k), lambda i, j: (i, 0)), pl.BlockSpec((k, tn), lambda i, j: (0, j)),
                  pl.BlockSpec((k, tn), lambda i, j: (0, j))],
        out_specs=[ospec, ospec, ospec, pl.BlockSpec((tn, tm), lambda i, j: (j, i))],
        out_shape=[o, o, o, jax.ShapeDtypeStruct((n, t), BF16)],
        compiler_params=_cparams(("parallel", "parallel")),
        name="ffn_up",
    )(h2, wg, wu)


def _ffn_bwd_act(df, wd, gt, up, *, tm, tn):
    t, k = df.shape
    n = wd.shape[0]
    tm = min(tm, t)
    assert t % tm == 0 and n % tn == 0, (t, n, tm, tn)

    sub = min(SUB_ROWS, tm)

    def body(df_ref, wd_ref, g_ref, u_ref, dg_ref, du_ref):
        for r in range(tm // sub):
            rows = pl.ds(r * sub, sub)
            da = lax.dot_general(df_ref[rows, :], wd_ref[...], (_DIMS["nt"], ((), ())), preferred_element_type=F32)
            g = g_ref[rows, :].astype(F32)
            u = u_ref[rows, :].astype(F32)
            s = _sig(g)
            dg_ref[rows, :] = (da * u * s * (1.0 + g * (1.0 - s))).astype(BF16)
            du_ref[rows, :] = (da * g * s).astype(BF16)

    o = jax.ShapeDtypeStruct((t, n), BF16)
    blk = pl.BlockSpec((tm, tn), lambda i, j: (i, j))
    return pl.pallas_call(
        body,
        grid=(t // tm, n // tn),
        in_specs=[pl.BlockSpec((tm, k), lambda i, j: (i, 0)), pl.BlockSpec((tn, k), lambda i, j: (j, 0)), blk, blk],
        out_specs=[blk, blk],
        out_shape=[o, o],
        compiler_params=_cparams(("parallel", "parallel")),
        name="ffn_bwd_act",
    )(df, wd, gt, up)


def _dw_dt_rows(dw_in_t, ddtr_b, h, *, tk=1024):
    t, d = h.shape
    tk = min(tk, t)
    nk = t // tk

    def body(buf_ref, d_ref, h_ref, o_ref, acc):
        kk = pl.program_id(0)

        @pl.when(kk == 0)
        def _():
            acc[...] = jnp.zeros_like(acc)

        acc[...] += lax.dot_general(d_ref[...], h_ref[...], (_DIMS["tn"], ((), ())), preferred_element_type=F32)

        @pl.when(kk == nk - 1)
        def _():
            o_ref[...] = acc[0:HEADS, :]

    return pl.pallas_call(
        body,
        grid=(nk,),
        in_specs=[DEP_SPEC, pl.BlockSpec((tk, LANES), lambda kk: (kk, 0)), pl.BlockSpec((tk, d), lambda kk: (kk, 0))],
        out_specs=pl.BlockSpec((HEADS, d), lambda kk: (D_MAIN // HEADS, 0)),
        out_shape=jax.ShapeDtypeStruct(dw_in_t.shape, F32),
        input_output_aliases={0: 0},
        scratch_shapes=[pltpu.VMEM((LANES, d), F32)],
        compiler_params=_cparams(("arbitrary",)),
        name="mm_dw_dt",
    )(dw_in_t, ddtr_b, h)


ROW_TILE = 256


DEP_SPEC = pl.BlockSpec(memory_space=pl.ANY)


def _rms_fwd(xv, g, *, dep=None, out_dtype, name):
    t, d = xv.shape
    deps = [] if dep is None else [dep]

    def body(*refs):
        x_ref, g_ref = refs[0], refs[1]
        o_ref = refs[-1]
        v = x_ref[...]
        r = lax.rsqrt(jnp.mean(v * v, axis=-1, keepdims=True) + EPS)
        o_ref[...] = (v * r * g_ref[...]).astype(out_dtype)

    row = pl.BlockSpec((ROW_TILE, d), lambda i: (i, 0))
    vec = pl.BlockSpec((1, d), lambda i: (0, 0))
    return pl.pallas_call(
        body,
        grid=(t // ROW_TILE,),
        in_specs=[row, vec] + [DEP_SPEC] * len(deps),
        out_specs=row,
        out_shape=jax.ShapeDtypeStruct((t, d), out_dtype),
        compiler_params=_cparams(("parallel",)),
        name=name,
    )(*([xv, g] + deps))


def _rms_bwd_rows(dy, v, gv):
    r = lax.rsqrt(jnp.mean(v * v, axis=-1, keepdims=True) + EPS)
    xh = v * r
    gdy = dy * gv
    dx = r * (gdy - xh * jnp.mean(gdy * xh, axis=-1, keepdims=True))
    return dx, jnp.sum(dy * xh, axis=0, keepdims=True)


FUSED_ROWS = 512


def _matmul_rows_tail(a, w, tail, *, tk, row_ins, vec_ins, row_outs, vec_outs, first=None, dep=None, name):
    t, kdim = a.shape
    d = w.shape[1]
    tm, tk = min(FUSED_ROWS, t), min(tk, kdim)
    nk, nb = kdim // tk, t // tm
    assert t % tm == 0 and kdim % tk == 0
    n_ri, n_vi, n_ro, n_vo = len(row_ins), len(vec_ins), len(row_outs), len(vec_outs)
    n_first = 0 if first is None else 2
    deps = [] if dep is None else [dep]

    def body(*refs):
        a_ref, w_ref = refs[0], refs[1]
        first_refs = refs[2:2 + n_first]
        p = 2 + n_first
        ri = refs[p:p + n_ri]
        vi = refs[p + n_ri:p + n_ri + n_vi]
        p += n_ri + n_vi + len(deps)
        ro = refs[p:p + n_ro]
        vo = refs[p + n_ro:p + n_ro + n_vo]
        acc = refs[-1]
        i, kk = pl.program_id(0), pl.program_id(1)

        @pl.when(jnp.logical_and(i == 0, kk == 0))
        def _():
            for ref in vo:
                ref[...] = jnp.zeros_like(ref)

        @pl.when(kk == 0)
        def _():
            if first is None:
                acc[...] = jnp.zeros_like(acc)
            else:
                acc[...] = jnp.dot(first_refs[0][...], first_refs[1][...], preferred_element_type=F32)

        acc[...] += jnp.dot(a_ref[...], w_ref[...], preferred_element_type=F32)

        @pl.when(kk == nk - 1)
        def _():
            outs, parts = tail(acc[...], [r[...] for r in ri], [v[...] for v in vi])
            for ref, val in zip(ro, outs):
                ref[...] = val.astype(ref.dtype)
            for ref, part in zip(vo, parts):
                ref[...] += part

    row = pl.BlockSpec((tm, d), lambda i, kk: (i, 0))
    const = lambda shape: pl.BlockSpec(shape, lambda i, kk: (0,) * len(shape))
    in_specs = [pl.BlockSpec((tm, tk), lambda i, kk: (i, kk)), pl.BlockSpec((tk, d), lambda i, kk: (kk, 0))]
    if first is not None:
        in_specs += [pl.BlockSpec((tm, first[0].shape[1]), lambda i, kk: (i, 0)), const(first[1].shape)]
    in_specs += [row] * n_ri + [const(v.shape) for v in vec_ins] + [pl.BlockSpec(memory_space=pl.ANY)] * len(deps)
    return pl.pallas_call(
        body,
        grid=(nb, nk),
        in_specs=in_specs,
        out_specs=[row] * n_ro + [const(sh) for sh in vec_outs],
        out_shape=[jax.ShapeDtypeStruct((t, d), dt) for dt in row_outs]
        + [jax.ShapeDtypeStruct(sh, F32) for sh in vec_outs],
        scratch_shapes=[pltpu.VMEM((tm, d), F32)],
        compiler_params=_cparams(("arbitrary", "arbitrary")),
        name=name,
    )(a, w, *([] if first is None else list(first)), *row_ins, *vec_ins, *deps)


def _down_loss(act, w_down, x1, tgt, g, *, tk):
    d = w_down.shape[1]

    def tail(v, rows, vecs):
        x1v, tv = rows
        gv, = vecs
        fh = v * lax.rsqrt(jnp.mean(v * v, axis=-1, keepdims=True) + EPS)
        e = x1v + fh * gv - tv
        dx2 = e * (1.0 / d)
        df, dg = _rms_bwd_rows(dx2, v, gv)
        loss = 0.5 * jnp.sum(jnp.mean(e * e, axis=-1, keepdims=True), axis=0, keepdims=True)
        return (dx2, df), (loss, dg)

    return _matmul_rows_tail(act, w_down, tail, tk=tk, row_ins=[x1, tgt], vec_ins=[g], row_outs=[F32, BF16],
                             vec_outs=[(1, 1), (1, d)], name="mm_down_loss")


def _dh_dx(dproj, w_in_t, ddtr_b, w_dt_t, xv, dx1, g, *, tk, dep):
    def tail(v, rows, vecs):
        xr, dx1r = rows
        dx, dg = _rms_bwd_rows(v, xr, vecs[0])
        return (dx + dx1r,), (dg,)

    return _matmul_rows_tail(dproj, w_in_t, tail, tk=tk, row_ins=[xv, dx1], vec_ins=[g], row_outs=[F32],
                             vec_outs=[(1, xv.shape[1])], first=(ddtr_b, w_dt_t), dep=dep, name="mm_dh_dx")


def _rms_post_pre(mix, xv, g_post, g_pre):
    t, d = mix.shape

    def body(m_ref, x_ref, gp_ref, gf_ref, x1_ref, h2_ref, h2t_ref):
        v = m_ref[...]
        x1 = x_ref[...] + v * lax.rsqrt(jnp.mean(v * v, axis=-1, keepdims=True) + EPS) * gp_ref[...]
        x1_ref[...] = x1
        h2 = (x1 * lax.rsqrt(jnp.mean(x1 * x1, axis=-1, keepdims=True) + EPS) * gf_ref[...]).astype(BF16)
        h2_ref[...] = h2
        h2t_ref[...] = h2.T

    row = pl.BlockSpec((ROW_TILE, d), lambda i: (i, 0))
    vec = pl.BlockSpec((1, d), lambda i: (0, 0))
    return pl.pallas_call(
        body,
        grid=(t // ROW_TILE,),
        in_specs=[row, row, vec, vec],
        out_specs=[row, row, pl.BlockSpec((d, ROW_TILE), lambda i: (0, i))],
        out_shape=[jax.ShapeDtypeStruct((t, d), F32), jax.ShapeDtypeStruct((t, d), BF16),
                   jax.ShapeDtypeStruct((d, t), BF16)],
        compiler_params=_cparams(("parallel",)),
        name="rms_mix_post_ffn_pre",
    )(mix, xv, g_post, g_pre)


def _rms_bwd_pre_post(dh2, x1, g_pre, dx2, mix, g_post, *, dep=None):
    t, d = x1.shape

    deps = [] if dep is None else [dep]

    def body(dh_ref, x1_ref, gf_ref, dx2_ref, m_ref, gp_ref, *rest):
        dx1_ref, dmix_ref, dgf_ref, dgp_ref = rest[len(deps):]

        @pl.when(pl.program_id(0) == 0)
        def _():
            dgf_ref[...] = jnp.zeros_like(dgf_ref)
            dgp_ref[...] = jnp.zeros_like(dgp_ref)

        dx, dgf = _rms_bwd_rows(dh_ref[...], x1_ref[...], gf_ref[...])
        dx1 = dx + dx2_ref[...]
        dx1_ref[...] = dx1
        dmix, dgp = _rms_bwd_rows(dx1, m_ref[...], gp_ref[...])
        dmix_ref[...] = dmix.astype(BF16)
        dgf_ref[...] += dgf
        dgp_ref[...] += dgp

    row = pl.BlockSpec((ROW_TILE, d), lambda i: (i, 0))
    vec = pl.BlockSpec((1, d), lambda i: (0, 0))
    return pl.pallas_call(
        body,
        grid=(t // ROW_TILE,),
        in_specs=[row, row, vec, row, row, vec] + [DEP_SPEC] * len(deps),
        out_specs=[row, row, vec, vec],
        out_shape=[jax.ShapeDtypeStruct((t, d), F32), jax.ShapeDtypeStruct((t, d), BF16),
                   jax.ShapeDtypeStruct((1, d), F32), jax.ShapeDtypeStruct((1, d), F32)],
        compiler_params=_cparams(("arbitrary",)),
        name="rms_ffn_pre_mix_post_bwd",
    )(dh2, x1, g_pre, dx2, mix, g_post, *deps)


CONV_ROWS = 256
TAP_ROWS = 64
HALO31 = 32
HALO4 = 8


def _sum8(v):
    return jnp.sum(v.reshape(v.shape[0] // 8, 8, v.shape[1]), axis=0)


SUBLANES = 8
PHASE_SPAN = (CONV_K - 1) // SUBLANES * SUBLANES


def _phase_scratch(ts):
    return pltpu.VMEM((SUBLANES, ts + PHASE_SPAN, LANES), F32)


def _phase_copies(ph, buf, ln, base, ts):
    for s in range(SUBLANES):
        n = ts + (CONV_K - 1 - s) // SUBLANES * SUBLANES
        ph[s, 0:n, :] = buf[pl.ds(base + s, n), ln]


def _tap_rows(ph, off, r0):
    s = off % SUBLANES
    return ph[s, pl.ds(r0 + off - s, TAP_ROWS), :]


def _conv_branch_fwd(proj, cw, cb, lg, lb, *, nb, seq):
    ts, c, halo = CONV_ROWS, D_CONV, HALO31
    ns = seq // ts
    base = halo - CONV_K + 1

    def body(ca_ref, cg_ref, w_ref, b_ref, lg_ref, lb_ref, u1_ref, u_ref, ubuf, uph):
        i = pl.program_id(1)

        @pl.when(i == 0)
        def _():
            ubuf[0:halo, :] = jnp.zeros((halo, c), F32)

        @pl.when(i > 0)
        def _():
            ubuf[0:halo, :] = ubuf[ts:ts + halo, :]

        ubuf[halo:halo + ts, :] = ca_ref[...] * _sig(cg_ref[...])

        def lane_tile(j, carry):
            ln = pl.ds(pl.multiple_of(j * LANES, LANES), LANES)
            _phase_copies(uph, ubuf, ln, base, ts)
            for r in range(ts // TAP_ROWS):
                acc = jnp.broadcast_to(b_ref[:, ln], (TAP_ROWS, LANES))
                for k in range(CONV_K):
                    acc = acc + w_ref[pl.ds(k, 1), ln] * _tap_rows(uph, k, r * TAP_ROWS)
                u1_ref[pl.ds(r * TAP_ROWS, TAP_ROWS), ln] = acc
            return carry

        lax.fori_loop(0, c // LANES, lane_tile, 0)
        v = u1_ref[...]
        mu = jnp.mean(v, axis=-1, keepdims=True)
        dv = v - mu
        xh = dv * lax.rsqrt(jnp.mean(dv * dv, axis=-1, keepdims=True) + EPS)
        u2 = xh * lg_ref[...] + lb_ref[...]
        u_ref[...] = (u2 * _sig(u2)).astype(BF16)

    t = nb * seq
    row = lambda col: pl.BlockSpec((ts, c), lambda b, i: (b * ns + i, col))
    vec = pl.BlockSpec((1, c), lambda b, i: (0, 0))
    return pl.pallas_call(
        body,
        grid=(nb, ns),
        in_specs=[row(0), row(1), pl.BlockSpec((32, c), lambda b, i: (0, 0)), vec, vec, vec],
        out_specs=[row(0), row(0)],
        out_shape=[jax.ShapeDtypeStruct((t, c), F32), jax.ShapeDtypeStruct((t, c + D_SSD), BF16)],
        scratch_shapes=[pltpu.VMEM((halo + ts, c), F32), _phase_scratch(ts)],
        compiler_params=_cparams(("parallel", "arbitrary")),
        name="conv_branch_fwd",
    )(proj, proj, cw, cb, lg, lb)


def _conv_branch_bwd(duy, u1, proj, cw, lg, lb, *, nb, seq):
    ts, c, halo = CONV_ROWS, D_CONV, HALO31
    ns = seq // ts
    base = halo - CONV_K + 1
    hb = ts // halo

    def body(du_ref, u1_ref, ca_ref, cg_ref, cah_ref, cgh_ref, w_ref, lg_ref, lb_ref,
             dcacg_ref, dw_ref, db_ref, dlg_ref, dlb_ref,
             ubuf, dbuf, du0buf, dwacc, dbacc, dlgacc, dlbacc, uph, dph):
        b, i = pl.program_id(0), pl.program_id(1)
        rc = ns - 1 - i

        @pl.when(jnp.logical_and(b == 0, i == 0))
        def _():
            dwacc[...] = jnp.zeros_like(dwacc)
            dbacc[...] = jnp.zeros_like(dbacc)
            dlgacc[...] = jnp.zeros_like(dlgacc)
            dlbacc[...] = jnp.zeros_like(dlbacc)

        @pl.when(i == 0)
        def _():
            dbuf[ts:ts + halo, :] = jnp.zeros((halo, c), F32)

        @pl.when(i > 0)
        def _():
            dbuf[ts:ts + halo, :] = dbuf[0:halo, :]

        v = u1_ref[...]
        mu = jnp.mean(v, axis=-1, keepdims=True)
        dv = v - mu
        rstd = lax.rsqrt(jnp.mean(dv * dv, axis=-1, keepdims=True) + EPS)
        xh = dv * rstd
        lgv = lg_ref[...]
        u2 = xh * lgv + lb_ref[...]
        s2 = _sig(u2)
        du2 = du_ref[...] * (s2 * (1.0 + u2 * (1.0 - s2)))
        dlgacc[...] += jnp.sum(du2 * xh, axis=0, keepdims=True)
        dlbacc[...] += jnp.sum(du2, axis=0, keepdims=True)
        gd = du2 * lgv
        du1 = rstd * (gd - jnp.mean(gd, axis=-1, keepdims=True) - xh * jnp.mean(gd * xh, axis=-1, keepdims=True))
        dbacc[...] += jnp.sum(du1, axis=0, keepdims=True)
        dbuf[0:ts, :] = du1

        @pl.when(rc == 0)
        def _():
            ubuf[0:halo, :] = jnp.zeros((halo, c), F32)

        @pl.when(rc > 0)
        def _():
            ubuf[0:halo, :] = cah_ref[...] * _sig(cgh_ref[...])

        cav = ca_ref[...]
        sg = _sig(cg_ref[...])
        ubuf[halo:halo + ts, :] = cav * sg

        def lane_tile(j, carry):
            ln = pl.ds(pl.multiple_of(j * LANES, LANES), LANES)
            _phase_copies(uph, ubuf, ln, base, ts)
            _phase_copies(dph, dbuf, ln, 0, ts)
            for r in range(ts // TAP_ROWS):
                r0 = r * TAP_ROWS
                d1 = dbuf[pl.ds(r0, TAP_ROWS), ln]
                acc = jnp.zeros((TAP_ROWS, LANES), F32)
                for k in range(CONV_K):
                    acc = acc + w_ref[pl.ds(k, 1), ln] * _tap_rows(dph, CONV_K - 1 - k, r0)
                    dwacc[pl.ds(k * 8, 8), ln] += _sum8(d1 * _tap_rows(uph, k, r0))
                du0buf[pl.ds(r0, TAP_ROWS), ln] = acc
            return carry

        lax.fori_loop(0, c // LANES, lane_tile, 0)
        du0 = du0buf[...]
        dcacg_ref[:, 0:c] = (du0 * sg).astype(BF16)
        dcacg_ref[:, c:2 * c] = (du0 * cav * sg * (1.0 - sg)).astype(BF16)

        @pl.when(jnp.logical_and(b == nb - 1, i == ns - 1))
        def _():
            for k in range(CONV_K):
                dw_ref[pl.ds(k, 1), :] = jnp.sum(dwacc[pl.ds(k * 8, 8), :], axis=0, keepdims=True)
            dw_ref[pl.ds(CONV_K, 1), :] = jnp.zeros((1, c), F32)
            db_ref[...] = dbacc[...]
            dlg_ref[...] = dlgacc[...]
            dlb_ref[...] = dlbacc[...]

    t = nb * seq
    rowblk = lambda b, i: b * ns + (ns - 1 - i)
    row = lambda col: pl.BlockSpec((ts, c), lambda b, i: (rowblk(b, i), col))
    hrow = lambda col: pl.BlockSpec((halo, c), lambda b, i: (jnp.maximum(rowblk(b, i) * hb - 1, 0), col))
    vec = pl.BlockSpec((1, c), lambda b, i: (0, 0))
    wspec = pl.BlockSpec((32, c), lambda b, i: (0, 0))
    return pl.pallas_call(
        body,
        grid=(nb, ns),
        in_specs=[row(0), row(0), row(0), row(1), hrow(0), hrow(1), wspec, vec, vec],
        out_specs=[pl.BlockSpec((ts, 2 * c), lambda b, i: (rowblk(b, i), 0)), wspec, vec, vec, vec],
        out_shape=[jax.ShapeDtypeStruct((t, D_MAIN), BF16), jax.ShapeDtypeStruct((32, c), F32),
                   jax.ShapeDtypeStruct((1, c), F32), jax.ShapeDtypeStruct((1, c), F32), jax.ShapeDtypeStruct((1, c), F32)],
        scratch_shapes=[pltpu.VMEM((halo + ts, c), F32), pltpu.VMEM((ts + halo, c), F32), pltpu.VMEM((ts, c), F32),
                        pltpu.VMEM((CONV_K * 8, c), F32), pltpu.VMEM((1, c), F32), pltpu.VMEM((1, c), F32),
                        pltpu.VMEM((1, c), F32), _phase_scratch(ts), _phase_scratch(ts)],
        compiler_params=_cparams(("arbitrary", "arbitrary")),
        name="conv_branch_bwd",
    )(duy, u1, proj, proj, proj, proj, cw, lg, lb)


XBC_COL0 = (2 * D_CONV + D_SSD) // 1024


def _ssd_pre_fwd(proj, sw, sb, *, nb, seq):
    ts, c, halo = CONV_ROWS, 1024, HALO4
    ns = seq // ts
    base = halo - SSD_CONV_K + 1

    def body(x_ref, w_ref, b_ref, o_ref, xbuf):
        i = pl.program_id(2)

        @pl.when(i == 0)
        def _():
            xbuf[0:halo, :] = jnp.zeros((halo, c), F32)

        @pl.when(i > 0)
        def _():
            xbuf[0:halo, :] = xbuf[ts:ts + halo, :]

        xbuf[halo:halo + ts, :] = x_ref[...]

        def lane_tile(j, carry):
            ln = pl.ds(pl.multiple_of(j * LANES, LANES), LANES)
            for r in range(ts // TAP_ROWS):
                acc = jnp.broadcast_to(b_ref[:, ln], (TAP_ROWS, LANES))
                for k in range(SSD_CONV_K):
                    acc = acc + w_ref[pl.ds(k, 1), ln] * xbuf[pl.ds(r * TAP_ROWS + base + k, TAP_ROWS), ln]
                o_ref[pl.ds(r * TAP_ROWS, TAP_ROWS), ln] = acc * _sig(acc)
            return carry

        lax.fori_loop(0, c // LANES, lane_tile, 0)

    t = nb * seq
    return pl.pallas_call(
        body,
        grid=(2, nb, ns),
        in_specs=[pl.BlockSpec((ts, c), lambda j, b, i: (b * ns + i, XBC_COL0 + j)),
                  pl.BlockSpec((8, c), lambda j, b, i: (0, j)), pl.BlockSpec((1, c), lambda j, b, i: (0, j))],
        out_specs=pl.BlockSpec((ts, c), lambda j, b, i: (b * ns + i, j)),
        out_shape=jax.ShapeDtypeStruct((t, D_XBC), F32),
        scratch_shapes=[pltpu.VMEM((halo + ts, c), F32)],
        compiler_params=_cparams(("parallel", "parallel", "arbitrary")),
        name="ssd_pre_fwd",
    )(proj, sw, sb)


def _ssd_pre_bwd(dproj, dxs, proj, sw, sb, *, nb, seq):
    ts, c, halo = CONV_ROWS, 1024, HALO4
    ns = seq // ts
    base = halo - SSD_CONV_K + 1
    hb = ts // halo

    def body(dproj_ref, d_ref, x_ref, xh_ref, w_ref, b_ref, dx_ref, dw_ref, db_ref, xbuf, dbuf, dwacc, dbacc):
        b, i = pl.program_id(1), pl.program_id(2)
        rc = ns - 1 - i

        @pl.when(jnp.logical_and(b == 0, i == 0))
        def _():
            dwacc[...] = jnp.zeros_like(dwacc)
            dbacc[...] = jnp.zeros_like(dbacc)

        @pl.when(i == 0)
        def _():
            dbuf[ts:ts + halo, :] = jnp.zeros((halo, c), F32)

        @pl.when(i > 0)
        def _():
            dbuf[ts:ts + halo, :] = dbuf[0:halo, :]

        @pl.when(rc == 0)
        def _():
            xbuf[0:halo, :] = jnp.zeros((halo, c), F32)

        @pl.when(rc > 0)
        def _():
            xbuf[0:halo, :] = xh_ref[...]

        xbuf[halo:halo + ts, :] = x_ref[...]

        def pre_tile(j, carry):
            ln = pl.ds(pl.multiple_of(j * LANES, LANES), LANES)
            for r in range(ts // TAP_ROWS):
                r0 = r * TAP_ROWS
                acc = jnp.broadcast_to(b_ref[:, ln], (TAP_ROWS, LANES))
                for k in range(SSD_CONV_K):
                    acc = acc + w_ref[pl.ds(k, 1), ln] * xbuf[pl.ds(r0 + base + k, TAP_ROWS), ln]
                s = _sig(acc)
                dc = d_ref[pl.ds(r0, TAP_ROWS), ln] * (s * (1.0 + acc * (1.0 - s)))
                dbuf[pl.ds(r0, TAP_ROWS), ln] = dc
                dbacc[:, ln] += _sum8(dc)
            return carry

        lax.fori_loop(0, c // LANES, pre_tile, 0)

        def lane_tile(j, carry):
            ln = pl.ds(pl.multiple_of(j * LANES, LANES), LANES)
            for r in range(ts // TAP_ROWS):
                r0 = r * TAP_ROWS
                d1 = dbuf[pl.ds(r0, TAP_ROWS), ln]
                acc = jnp.zeros((TAP_ROWS, LANES), F32)
                for k in range(SSD_CONV_K):
                    acc = acc + w_ref[pl.ds(k, 1), ln] * dbuf[pl.ds(r0 + SSD_CONV_K - 1 - k, TAP_ROWS), ln]
                    dwacc[pl.ds(k * 8, 8), ln] += _sum8(d1 * xbuf[pl.ds(r0 + base + k, TAP_ROWS), ln])
                dx_ref[pl.ds(r0, TAP_ROWS), ln] = acc.astype(BF16)
            return carry

        lax.fori_loop(0, c // LANES, lane_tile, 0)

        @pl.when(jnp.logical_and(b == nb - 1, i == ns - 1))
        def _():
            for k in range(SSD_CONV_K):
                dw_ref[pl.ds(k, 1), :] = jnp.sum(dwacc[pl.ds(k * 8, 8), :], axis=0, keepdims=True)
            dw_ref[pl.ds(SSD_CONV_K, 8 - SSD_CONV_K), :] = jnp.zeros((8 - SSD_CONV_K, c), F32)
            db_ref[...] = jnp.sum(dbacc[...], axis=0, keepdims=True)

    t = nb * seq
    rowblk = lambda b, i: b * ns + (ns - 1 - i)
    return pl.pallas_call(
        body,
        grid=(2, nb, ns),
        in_specs=[DEP_SPEC, pl.BlockSpec((ts, c), lambda j, b, i: (rowblk(b, i), j)),
                  pl.BlockSpec((ts, c), lambda j, b, i: (rowblk(b, i), XBC_COL0 + j)),
                  pl.BlockSpec((halo, c), lambda j, b, i: (jnp.maximum(rowblk(b, i) * hb - 1, 0), XBC_COL0 + j)),
                  pl.BlockSpec((8, c), lambda j, b, i: (0, j)), pl.BlockSpec((1, c), lambda j, b, i: (0, j))],
        out_specs=[pl.BlockSpec((ts, c), lambda j, b, i: (rowblk(b, i), XBC_COL0 + j)),
                   pl.BlockSpec((8, c), lambda j, b, i: (0, j)), pl.BlockSpec((1, c), lambda j, b, i: (0, j))],
        out_shape=[jax.ShapeDtypeStruct(dproj.shape, BF16), jax.ShapeDtypeStruct((8, D_XBC), F32),
                   jax.ShapeDtypeStruct((1, D_XBC), F32)],
        input_output_aliases={0: 0},
        scratch_shapes=[pltpu.VMEM((halo + ts, c), F32), pltpu.VMEM((ts + halo, c), F32),
                        pltpu.VMEM((SSD_CONV_K * 8, c), F32), pltpu.VMEM((8, c), F32)],
        compiler_params=_cparams(("arbitrary", "arbitrary", "arbitrary")),
        name="ssd_pre_bwd",
    )(dproj, dxs, proj, proj, sw, sb)


Z_COL = (2 * D_CONV) // 1024
GROUP_W = D_SSD // GROUPS


def _softplus(v):
    return jnp.maximum(v, 0.0) + jnp.log(1.0 + jnp.exp(-jnp.abs(v)))


def _dot(a, b):
    return jnp.dot(a, b, preferred_element_type=F32)


def _dot_nt(a, b):
    return lax.dot_general(a, b, (_DIMS["nt"], ((), ())), preferred_element_type=F32)


def _dot_tn(a, b):
    return lax.dot_general(a, b, (_DIMS["tn"], ((), ())), preferred_element_type=F32)


def _bf16_terms(v):
    hi = v.astype(BF16)
    r1 = v - hi.astype(F32)
    mid = r1.astype(BF16)
    return hi, mid, (r1 - mid.astype(F32)).astype(BF16)


def _dot_exact_left(sel, v):
    hi, mid, lo = _bf16_terms(v)
    return _dot(sel, hi) + (_dot(sel, mid) + _dot(sel, lo))


def _dot_exact_right(v, sel):
    hi, mid, lo = _bf16_terms(v)
    return _dot(hi, sel) + (_dot(mid, sel) + _dot(lo, sel))


def _chunk_decays(dtr_ref, bias_ref, alog_ref):
    q = CHUNK
    ii = lax.broadcasted_iota(jnp.int32, (q, q), 0)
    jj = lax.broadcasted_iota(jnp.int32, (q, q), 1)
    tri = jj <= ii
    dt = _softplus(dtr_ref[...] + bias_ref[...])
    a_head = -jnp.exp(alog_ref[...])
    cs = _dot_exact_left(tri.astype(BF16), dt * a_head)
    return tri, dt, a_head, cs, cs.T


def _ssd_fwd(uy, xs_all, proj, dtr, dt_bias, a_log, d_lanes, norm_w, *, nb, seq):
    q = CHUNK
    nc = seq // q
    t = nb * seq

    def body(uy_ref, xs_ref, bm_ref, cm_ref, z_ref, dtr_ref, bias_ref, alog_ref, dl_ref, nw_ref,
             y_ref, ys_ref, st_ref, state):
        @pl.when(pl.program_id(1) == 0)
        def _():
            state[...] = jnp.zeros_like(state)

        tri, dt, _, cs, cst = _chunk_decays(dtr_ref, bias_ref, alog_ref)
        first = lax.broadcasted_iota(jnp.int32, (1, LANES), 1) < HEAD_DIM
        for g in range(GROUPS):
            gl = slice(g * STATE, (g + 1) * STATE)
            bb = bm_ref[:, gl].astype(BF16)
            cb = cm_ref[:, gl].astype(BF16)
            scores = _dot_nt(cb, bb)
            for p in range(2):
                pr = 2 * g + p
                h0 = 2 * pr
                sl = slice(pr * LANES, (pr + 1) * LANES)
                xv = xs_ref[:, sl]
                dtp = jnp.where(first, dt[:, h0:h0 + 1], dt[:, h0 + 1:h0 + 2])
                csp = jnp.where(first, cs[:, h0:h0 + 1], cs[:, h0 + 1:h0 + 2])
                xd = xv * dtp
                yv = None
                for hh, keep in ((h0, first), (h0 + 1, jnp.logical_not(first))):
                    decay = jnp.where(tri, jnp.exp(cs[:, hh:hh + 1] - cst[hh:hh + 1, :]), 0.0)
                    part = _dot((scores * decay).astype(BF16), jnp.where(keep, xd, 0.0).astype(BF16))
                    yv = part if yv is None else yv + part
                hp = state[pr]
                st_ref[0, pr] = hp
                yv = yv + jnp.exp(csp) * _dot(cb, hp.astype(BF16))
                last = csp[q - 1:q, :]
                state[pr] = jnp.exp(last) * hp + _dot_tn(bb, (xd * jnp.exp(last - csp)).astype(BF16))
                ys_ref[:, sl] = yv + dl_ref[:, sl] * xv
        zv = z_ref[...]
        gated = ys_ref[...] * (zv * _sig(zv))
        for g in range(GROUPS):
            gl = slice(g * GROUP_W, (g + 1) * GROUP_W)
            v = gated[:, gl]
            r = lax.rsqrt(jnp.mean(v * v, axis=-1, keepdims=True) + EPS)
            y_ref[:, gl] = (v * r * nw_ref[:, gl]).astype(BF16)

    blk = lambda w, col: pl.BlockSpec((q, w), lambda b, c: (b * nc + c, col))
    vec = lambda w: pl.BlockSpec((1, w), lambda b, c: (0, 0))
    return pl.pallas_call(
        body,
        grid=(nb, nc),
        in_specs=[DEP_SPEC, blk(D_SSD, 0), blk(GROUPS * STATE, 2), blk(GROUPS * STATE, 3), blk(D_SSD, Z_COL),
                  blk(LANES, 0), vec(LANES), vec(LANES), vec(D_SSD), vec(D_SSD)],
        out_specs=[blk(D_SSD, 1), blk(D_SSD, 0),
                   pl.BlockSpec((1, PAIRS, STATE, LANES), lambda b, c: (b * nc + c, 0, 0, 0))],
        out_shape=[jax.ShapeDtypeStruct(uy.shape, BF16), jax.ShapeDtypeStruct((t, D_SSD), F32),
                   jax.ShapeDtypeStruct((nb * nc, PAIRS, STATE, LANES), F32)],
        input_output_aliases={0: 0},
        scratch_shapes=[pltpu.VMEM((PAIRS, STATE, LANES), F32)],
        compiler_params=_cparams(("parallel", "arbitrary")),
        name="ssd_fwd",
    )(uy, xs_all, xs_all, xs_all, proj, dtr, dt_bias, a_log, d_lanes, norm_w)


def _ssd_bwd(dproj, duy, proj, ys, xs_all, dtr, states, dt_bias, a_log, d_lanes, norm_w, *, nb, seq):
    q = CHUNK
    nc = seq // q
    t = nb * seq
    head_of_lane = (jnp.arange(D_SSD)[:, None] // HEAD_DIM == jnp.arange(LANES)[None, :]).astype(BF16)

    def body(dproj_ref, dy_ref, z_ref, ys_ref, xs_ref, bm_ref, cm_ref, dtr_ref, st_ref, bias_ref, alog_ref, dl_ref,
             nw_ref, sel_ref, dz_ref, dx_ref, ddtr_ref, small_ref,
             dstate, dys_buf, dcsl, ddtl, dcst, dnw_acc, dd_acc, dbias_acc, da_acc):
        b, c = pl.program_id(0), pl.program_id(1)

        @pl.when(jnp.logical_and(b == 0, c == 0))
        def _():
            dnw_acc[...] = jnp.zeros_like(dnw_acc)
            dd_acc[...] = jnp.zeros_like(dd_acc)
            dbias_acc[...] = jnp.zeros_like(dbias_acc)
            da_acc[...] = jnp.zeros_like(da_acc)
            dcst[...] = jnp.zeros_like(dcst)

        @pl.when(c == 0)
        def _():
            dstate[...] = jnp.zeros_like(dstate)

        zv = z_ref[...]
        sz = _sig(zv)
        silz = zv * sz
        ysv = ys_ref[...]
        gated = ysv * silz
        dyv = dy_ref[...]
        nwv = nw_ref[...]
        for g in range(GROUPS):
            gl = slice(g * GROUP_W, (g + 1) * GROUP_W)
            v = gated[:, gl]
            r = lax.rsqrt(jnp.mean(v * v, axis=-1, keepdims=True) + EPS)
            yn = v * r
            dyn = dyv[:, gl] * nwv[:, gl]
            dnw_acc[:, gl] += jnp.sum(dyv[:, gl] * yn, axis=0, keepdims=True)
            dys_buf[:, gl] = r * (dyn - yn * jnp.mean(dyn * yn, axis=-1, keepdims=True))
        dgated = dys_buf[...]
        dz_ref[...] = (dgated * ysv * (sz * (1.0 + zv * (1.0 - sz)))).astype(BF16)
        dys_all = dgated * silz
        dys_buf[...] = dys_all
        dd_acc[...] += jnp.sum(dys_all * xs_ref[...], axis=0, keepdims=True)

        tri, dt, a_head, cs, cst = _chunk_decays(dtr_ref, bias_ref, alog_ref)
        lane = lax.broadcasted_iota(jnp.int32, (1, LANES), 1)
        first = lane < HEAD_DIM
        dcs_h = jnp.zeros((q, LANES), F32)
        for g in range(GROUPS):
            gl = slice(g * STATE, (g + 1) * STATE)
            bb = bm_ref[:, gl].astype(BF16)
            cb = cm_ref[:, gl].astype(BF16)
            scores = _dot_nt(cb, bb)
            dscores = jnp.zeros((q, q), F32)
            dbg = jnp.zeros((q, STATE), F32)
            dcg = jnp.zeros((q, STATE), F32)
            for p in range(2):
                pr = 2 * g + p
                h0 = 2 * pr
                sl = slice(pr * LANES, (pr + 1) * LANES)
                xv = xs_ref[:, sl]
                dyp = dys_buf[:, sl]
                dtp = jnp.where(first, dt[:, h0:h0 + 1], dt[:, h0 + 1:h0 + 2])
                csp = jnp.where(first, cs[:, h0:h0 + 1], cs[:, h0 + 1:h0 + 2])
                xd = xv * dtp
                xdb = xd.astype(BF16)
                hp = st_ref[0, pr]
                dhn = dstate[pr]
                hpb = hp.astype(BF16)
                dhnb = dhn.astype(BF16)
                lam = jnp.exp(csp)
                last = csp[q - 1:q, :]
                gam = jnp.exp(last)
                w = jnp.exp(last - csp)
                dxd = jnp.zeros((q, LANES), F32)
                for hh, keep in ((h0, first), (h0 + 1, jnp.logical_not(first))):
                    decay = jnp.where(tri, jnp.exp(cs[:, hh:hh + 1] - cst[hh:hh + 1, :]), 0.0)
                    m = scores * decay
                    dym = jnp.where(keep, dyp, 0.0).astype(BF16)
                    dm = _dot_nt(dym, xdb)
                    dxd = dxd + _dot_tn(m.astype(BF16), dym)
                    e = dm * m
                    dcs_h = dcs_h + jnp.where(lane == hh, jnp.sum(e, axis=1, keepdims=True), 0.0)
                    dcst[hh:hh + 1, :] = jnp.sum(e, axis=0, keepdims=True)
                    dscores = dscores + dm * decay
                yoff = lam * _dot(cb, hpb)
                ldy = (lam * dyp).astype(BF16)
                dcg = dcg + _dot_nt(ldy, hpb)
                dstate[pr] = gam * dhn + _dot_tn(cb, ldy)
                bdh = _dot(bb, dhnb)
                dxd = dxd + w * bdh
                xdw = xd * w
                dbg = dbg + _dot_nt(xdw.astype(BF16), dhnb)
                wd = xdw * bdh
                dcsl[:, sl] = dyp * yoff - wd
                dcsl[q - 1:q, sl] += (jnp.sum(wd, axis=0, keepdims=True)
                                      + gam * jnp.sum(dhn * hp, axis=0, keepdims=True))
                dx_ref[:, sl] = dxd * dtp + dyp * dl_ref[:, sl]
                ddtl[:, sl] = dxd * xv
            dsb = dscores.astype(BF16)
            dx_ref[:, D_SSD + g * STATE:D_SSD + (g + 1) * STATE] = dbg + _dot_tn(dsb, cb)
            dx_ref[:, D_SSD + (GROUPS + g) * STATE:D_SSD + (GROUPS + g + 1) * STATE] = dcg + _dot(dsb, bb)

        sel = sel_ref[...]
        dcs_h = dcs_h + _dot_exact_right(dcsl[...], sel) - dcst[...].T
        ddt = _dot_exact_right(ddtl[...], sel)
        upper = lax.broadcasted_iota(jnp.int32, (q, q), 1) >= lax.broadcasted_iota(jnp.int32, (q, q), 0)
        da = _dot_exact_left(upper.astype(BF16), dcs_h)
        ddt = ddt + da * a_head
        da_acc[...] += jnp.sum(da * dt, axis=0, keepdims=True)
        ddtr = ddt * _sig(dtr_ref[...] + bias_ref[...])
        ddtr_ref[...] = ddtr
        dbias_acc[...] += jnp.sum(ddtr, axis=0, keepdims=True)

        @pl.when(jnp.logical_and(b == nb - 1, c == nc - 1))
        def _():
            small_ref[...] = jnp.zeros_like(small_ref)
            small_ref[0:1, :] = dnw_acc[...]
            small_ref[1:2, 0:LANES] = _dot_exact_right(jnp.broadcast_to(dd_acc[...], (8, D_SSD)), sel)[0:1, :]
            small_ref[2:3, 0:LANES] = dbias_acc[...]
            small_ref[3:4, 0:LANES] = da_acc[...] * a_head

    rowblk = lambda b, c: b * nc + (nc - 1 - c)
    blk = lambda w, col: pl.BlockSpec((q, w), lambda b, c: (rowblk(b, c), col))
    vec = lambda w: pl.BlockSpec((1, w), lambda b, c: (0, 0))
    return pl.pallas_call(
        body,
        grid=(nb, nc),
        in_specs=[DEP_SPEC, blk(D_SSD, 1), blk(D_SSD, Z_COL), blk(D_SSD, 0), blk(D_SSD, 0), blk(GROUPS * STATE, 2),
                  blk(GROUPS * STATE, 3), blk(LANES, 0),
                  pl.BlockSpec((1, PAIRS, STATE, LANES), lambda b, c: (rowblk(b, c), 0, 0, 0)),
                  vec(LANES), vec(LANES), vec(D_SSD), vec(D_SSD), pl.BlockSpec((D_SSD, LANES), lambda b, c: (0, 0))],
        out_specs=[blk(D_SSD, Z_COL), blk(D_XBC, 0), blk(LANES, 0), pl.BlockSpec((8, D_SSD), lambda b, c: (0, 0))],
        out_shape=[jax.ShapeDtypeStruct(dproj.shape, BF16), jax.ShapeDtypeStruct((t, D_XBC), F32),
                   jax.ShapeDtypeStruct((t, LANES), F32), jax.ShapeDtypeStruct((8, D_SSD), F32)],
        input_output_aliases={0: 0},
        scratch_shapes=[pltpu.VMEM((PAIRS, STATE, LANES), F32), pltpu.VMEM((q, D_SSD), F32),
                        pltpu.VMEM((q, D_SSD), F32), pltpu.VMEM((q, D_SSD), F32), pltpu.VMEM((LANES, q), F32),
                        pltpu.VMEM((1, D_SSD), F32), pltpu.VMEM((1, D_SSD), F32), pltpu.VMEM((1, LANES), F32),
                        pltpu.VMEM((1, LANES), F32)],
        compiler_params=_cparams(("arbitrary", "arbitrary")),
        name="ssd_bwd",
    )(dproj, duy, proj, ys, xs_all, xs_all, xs_all, dtr, states, dt_bias, a_log, d_lanes, norm_w, head_of_lane)


HBM_SPEC = pl.BlockSpec(memory_space=pltpu.HBM)
MESH_ID = pl.DeviceIdType.MESH


def _coords():
    return lax.axis_index("x"), lax.axis_index("y"), lax.axis_index("c")


def _chip_peer(xi, yi, ci, d):
    return (jnp.bitwise_xor(xi, d >> 1), jnp.bitwise_xor(yi, d & 1), ci)


def _remote(src, dst, send_sem, recv_sem, peer):
    return pltpu.make_async_remote_copy(src_ref=src, dst_ref=dst, send_sem=send_sem, recv_sem=recv_sem,
                                        device_id=peer, device_id_type=MESH_ID)


SEM_SPEC = pl.BlockSpec(memory_space=pltpu.SEMAPHORE)
ANY_SPEC = pl.BlockSpec(memory_space=pl.ANY)
EFFECT = pltpu.SideEffectType.DATAFLOW_SIDE_EFFECTING
COPIES = 3


def _half(ref, axis, which, lead=0):
    size = ref.shape[lead + axis] // 2
    part = pl.ds(which * size, size)
    idx = (slice(None),) * lead + ((part, slice(None)) if axis == 0 else (slice(None), part))
    return ref.at[idx]


def _halved_shape(shape, axis):
    lead = len(shape) - 2
    return tuple(d // 2 if i == lead + axis else d for i, d in enumerate(shape))


def _gather_plan(axis):
    def plan(xi, yi, ci, src, land):
        me = 2 * xi + yi
        out = []
        for d in (1, 2, 3):
            there = jnp.bitwise_xor(me, d)
            if axis is None:
                out.append((src, land.at[me], _chip_peer(xi, yi, ci, d), land.at[there]))
            else:
                out.append((_half(src, axis, ci), _half(land.at[me], axis, ci), _chip_peer(xi, yi, ci, d),
                            _half(land.at[there], axis, ci)))
        return out
    return plan


def _halves_plan(axis):
    def plan(xi, yi, ci, src, land):
        return [(_half(src, axis, 1 - ci, lead=1), land, (xi, yi, 1 - ci), land)]
    return plan


def _forward_plan(axis):
    def plan(xi, yi, ci, src, land):
        me = 2 * xi + yi
        out = []
        for d in (1, 2, 3):
            slot = land.at[jnp.bitwise_xor(me, d)]
            out.append((_half(slot, axis, ci), _half(slot, axis, ci), (xi, yi, 1 - ci), _half(slot, axis, 1 - ci)))
        return out
    return plan


def _sibling_plan(xi, yi, ci, src, land):
    return [(src, land, (xi, yi, 1 - ci), land)]


def _owners_plan(xi, yi, ci, src, land):
    me = 2 * xi + yi
    return [(src.at[jnp.bitwise_xor(me, d)], land.at[d - 1], _chip_peer(xi, yi, ci, d), land.at[d - 1])
            for d in (1, 2, 3)]


def _split_start(srcs, lands, plans, *, name, dep=None):
    n = len(srcs)
    deps = [] if dep is None else [dep]

    def body(*refs):
        src_refs, land_refs = refs[:n], refs[n:2 * n]
        outs = refs[2 * n + len(deps):]
        ssems, rsems = outs[:n], outs[n:2 * n]
        token = refs[-1]
        xi, yi, ci = _coords()
        for t in range(n):
            for k, (src, dst, peer, _) in enumerate(plans[t](xi, yi, ci, src_refs[t], land_refs[t])):
                _remote(src, dst, ssems[t].at[k], rsems[t].at[k], peer).start()
        token[...] = jnp.zeros_like(token)

    bufs = list(srcs) + list(lands)
    outs = pl.pallas_call(
        body,
        name=name,
        in_specs=[HBM_SPEC] * (2 * n) + [ANY_SPEC] * len(deps),
        out_specs=[SEM_SPEC] * (2 * n) + [HBM_SPEC] * (2 * n) + [pl.BlockSpec(memory_space=pltpu.VMEM)],
        out_shape=[pltpu.SemaphoreType.DMA((COPIES,))] * (2 * n) + [pltpu.HBM(a.shape, a.dtype) for a in bufs]
        + [jax.ShapeDtypeStruct((8, LANES), F32)],
        input_output_aliases={i: 2 * n + i for i in range(2 * n)},
        compiler_params=pltpu.CompilerParams(has_side_effects=EFFECT),
    )(*[pltpu.with_memory_space_constraint(a, pltpu.HBM) for a in bufs], *deps)
    return outs[:n], outs[n:2 * n], outs[2 * n:3 * n], outs[3 * n:4 * n], outs[-1]


def _split_wait(ssems, rsems, srcs, lands, plans, after, *, name):
    n = len(srcs)

    def body(*refs):
        src_refs, land_refs = refs[:n], refs[n:2 * n]
        ss, rs = refs[2 * n:3 * n], refs[3 * n:4 * n]
        xi, yi, ci = _coords()
        for t in range(n):
            for k, (src, _, peer, landed) in enumerate(plans[t](xi, yi, ci, src_refs[t], land_refs[t])):
                cp = _remote(src, landed, ss[t].at[k], rs[t].at[k], peer)
                cp.wait_send()
                cp.wait_recv()

    bufs = list(srcs) + list(lands)
    outs = pl.pallas_call(
        body,
        name=name,
        in_specs=[HBM_SPEC] * (2 * n) + [SEM_SPEC] * (2 * n) + [ANY_SPEC],
        out_specs=[HBM_SPEC] * (2 * n),
        out_shape=[pltpu.HBM(a.shape, a.dtype) for a in bufs],
        input_output_aliases={i: i for i in range(2 * n)},
        compiler_params=pltpu.CompilerParams(has_side_effects=EFFECT),
    )(*bufs, *ssems, *rsems, after)
    return outs[:n], outs[n:]


def _forward_halves(lands, axes, *, name):
    n = len(lands)

    def body(*refs):
        ins, outs = refs[:n], refs[n:2 * n]
        send_sems, recv_sems = refs[2 * n:]
        xi, yi, ci = _coords()
        me = 2 * xi + yi
        sibling = (xi, yi, 1 - ci)
        cps = []
        for t in range(n):
            for d in (1, 2, 3):
                slot = jnp.bitwise_xor(me, d)
                k = COPIES * t + d - 1
                cp = _remote(_half(ins[t].at[slot], axes[t], ci), _half(outs[t].at[slot], axes[t], ci),
                             send_sems.at[k], recv_sems.at[k], sibling)
                cp.start()
                cps.append(cp)
        for t in range(n):
            for d in (1, 2, 3):
                got = _half(outs[t].at[jnp.bitwise_xor(me, d)], axes[t], 1 - ci)
                k = COPIES * t + d - 1
                _remote(got, got, send_sems.at[k], recv_sems.at[k], sibling).wait_recv()
        for cp in cps:
            cp.wait_send()

    return pl.pallas_call(
        body,
        name=name,
        in_specs=[HBM_SPEC] * n,
        out_specs=[HBM_SPEC] * n,
        out_shape=[jax.ShapeDtypeStruct(a.shape, a.dtype) for a in lands],
        input_output_aliases={i: i for i in range(n)},
        scratch_shapes=[pltpu.SemaphoreType.DMA((COPIES * n,)), pltpu.SemaphoreType.DMA((COPIES * n,))],
    )(*lands)


def _swap_other_halves(gs, axes, *, name):
    n = len(gs)

    def body(*refs):
        ins, lands = refs[:n], refs[n:2 * n]
        send_sems, recv_sems = refs[2 * n:]
        xi, yi, ci = _coords()
        sibling = (xi, yi, 1 - ci)
        cps = []
        for t in range(n):
            cp = _remote(_half(ins[t], axes[t], 1 - ci, lead=1), lands[t], send_sems.at[t], recv_sems.at[t], sibling)
            cp.start()
            cps.append(cp)
        for cp in cps:
            cp.wait_recv()
        for cp in cps:
            cp.wait_send()

    return pl.pallas_call(
        body,
        in_specs=[HBM_SPEC] * n,
        out_specs=[HBM_SPEC] * n,
        out_shape=[jax.ShapeDtypeStruct(_halved_shape(g.shape, ax), g.dtype) for g, ax in zip(gs, axes)],
        scratch_shapes=[pltpu.SemaphoreType.DMA((n,)), pltpu.SemaphoreType.DMA((n,))],
        name=name,
    )(*gs)


def _row_tile(rows, cap=512, mult=16):
    best = mult
    for cand in range(mult, min(rows, cap) + 1, mult):
        if rows % cand == 0:
            best = cand
    assert rows % best == 0, rows
    return best


COL_TILE = 256


def _half_tiles(hr, hc, axis, cap=512, mult=16):
    if axis == 0:
        tr = _row_tile(hr, cap, mult)
        n = hr // tr
        return (tr, hc), n, lambda half, i: (half * n + i, 0)
    n = hc // COL_TILE
    return (hr, COL_TILE), n, lambda half, i: (0, half * n + i)


def _add_core_halves(g, land, where, axis):
    nslot, hr, hc = land.shape
    bshape, nr, idx = _half_tiles(hr, hc, axis)

    def body(where_ref, g_ref, l_ref, f_ref, b_ref):
        s = g_ref[...] + l_ref[...].astype(F32)
        b_ref[...] = s.astype(BF16)

        @pl.when(pl.program_id(1) == where_ref[1])
        def _():
            f_ref[...] = s

    blk = pl.BlockSpec((None,) + bshape, lambda i, s, w: (s,) + idx(0, i))
    mine = pl.BlockSpec((None,) + bshape, lambda i, s, w: (s,) + idx(w[0], i))
    return pl.pallas_call(
        body,
        grid_spec=pltpu.PrefetchScalarGridSpec(
            num_scalar_prefetch=1,
            grid=(nr, nslot),
            in_specs=[mine, blk],
            out_specs=[pl.BlockSpec(bshape, lambda i, s, w: idx(0, i)), blk],
        ),
        out_shape=[jax.ShapeDtypeStruct((hr, hc), F32), jax.ShapeDtypeStruct(land.shape, BF16)],
        compiler_params=_cparams(("parallel", "arbitrary")),
        name="add_core_halves",
    )(where, g, land)


def _add_chip_sums(pf, land, where, axis):
    hr, cols = pf.shape
    bshape, nr, idx = _half_tiles(hr, cols, axis)

    def body(where_ref, p_ref, l_ref, o_ref):
        acc = p_ref[...]
        for d in range(3):
            acc = acc + l_ref[d].astype(F32)
        o_ref[...] = acc

    return pl.pallas_call(
        body,
        grid_spec=pltpu.PrefetchScalarGridSpec(
            num_scalar_prefetch=1,
            grid=(nr,),
            in_specs=[pl.BlockSpec(bshape, lambda i, w: idx(0, i)),
                      pl.BlockSpec((3,) + bshape, lambda i, w: (0,) + idx(0, i))],
            out_specs=pl.BlockSpec(bshape, lambda i, w: idx(0, i)),
        ),
        out_shape=jax.ShapeDtypeStruct((hr, cols), F32),
        compiler_params=_cparams(("parallel",)),
        name="add_chip_sums",
    )(where, pf, land)


N_DEV = 8


def _all_reduce_small(part):
    r, w = part.shape

    def body(p_ref, o_ref, gath, send_sems, recv_sems):
        xi, yi, ci = _coords()
        me = 4 * xi + 2 * yi + ci
        gath[me] = p_ref[...]
        cps = []
        for d in range(1, N_DEV):
            peer = (jnp.bitwise_xor(xi, d >> 2), jnp.bitwise_xor(yi, (d >> 1) & 1), jnp.bitwise_xor(ci, d & 1))
            cp = _remote(p_ref, gath.at[me], send_sems.at[d - 1], recv_sems.at[d - 1], peer)
            cp.start()
            cps.append(cp)
        for d in range(1, N_DEV):
            src = gath.at[jnp.bitwise_xor(me, d)]
            _remote(src, src, send_sems.at[d - 1], recv_sems.at[d - 1], (xi, yi, ci)).wait_recv()
        acc = gath[0]
        for k in range(1, N_DEV):
            acc = acc + gath[k]
        o_ref[...] = acc
        for cp in cps:
            cp.wait_send()

    vm = pl.BlockSpec(memory_space=pltpu.VMEM)
    return pl.pallas_call(
        body,
        in_specs=[vm],
        out_specs=vm,
        out_shape=jax.ShapeDtypeStruct((r, w), F32),
        scratch_shapes=[pltpu.VMEM((N_DEV, r, w), F32), pltpu.SemaphoreType.DMA((N_DEV - 1,)),
                        pltpu.SemaphoreType.DMA((N_DEV - 1,))],
        name="all_reduce_small",
    )(part)


def _adamw_math(wv, gv, mv, vv):
    mn = ADAM_B1 * mv + (1.0 - ADAM_B1) * gv
    vn = ADAM_B2 * vv + (1.0 - ADAM_B2) * (gv * gv)
    m_hat = mn / (1.0 - ADAM_B1 ** ADAM_STEP)
    v_hat = vn / (1.0 - ADAM_B2 ** ADAM_STEP)
    return -ADAM_LR * (m_hat / (jnp.sqrt(v_hat) + ADAM_EPS) + ADAM_WD * wv), mn, vn


def _adamw_halves(w, g_mine, g_other, m, v, where, axis, *, name):
    rows, cols = w.shape
    hr, hc = g_mine.shape
    bshape, nr, idx = _half_tiles(hr, hc, axis, cap=256, mult=8)

    def body(where_ref, w_ref, gm_ref, go_ref, m_ref, v_ref, g_ref, d_ref, nm_ref, nv_ref):
        is_mine = pl.program_id(0) == where_ref[0]
        gv = jnp.where(is_mine, gm_ref[...], go_ref[...])
        g_ref[...] = gv
        d_ref[...], nm_ref[...], nv_ref[...] = _adamw_math(w_ref[...], gv, m_ref[...], v_ref[...])

    def parked(half, i, holder):
        return idx(0, jnp.where(half == holder, i, jnp.where(half < holder, 0, nr - 1)))

    blk = pl.BlockSpec(bshape, lambda hf, i, wh: idx(hf, i))
    o = jax.ShapeDtypeStruct((rows, cols), F32)
    return pl.pallas_call(
        body,
        grid_spec=pltpu.PrefetchScalarGridSpec(
            num_scalar_prefetch=1,
            grid=(2, nr),
            in_specs=[blk, pl.BlockSpec(bshape, lambda hf, i, wh: parked(hf, i, wh[0])),
                      pl.BlockSpec(bshape, lambda hf, i, wh: parked(hf, i, 1 - wh[0])), blk, blk],
            out_specs=[blk] * 4,
        ),
        out_shape=[o, o, o, o],
        compiler_params=_cparams(("arbitrary", "arbitrary")),
        name=name,
    )(where, w, g_mine, g_other, m, v)


def _adamw(w, g, m, v, *, name):
    rows, cols = w.shape
    tr = _row_tile(rows, cap=256, mult=8)

    def body(w_ref, g_ref, m_ref, v_ref, d_ref, nm_ref, nv_ref):
        d_ref[...], nm_ref[...], nv_ref[...] = _adamw_math(w_ref[...], g_ref[...], m_ref[...], v_ref[...])

    blk = pl.BlockSpec((tr, cols), lambda i: (i, 0))
    o = jax.ShapeDtypeStruct((rows, cols), F32)
    return pl.pallas_call(
        body,
        grid=(rows // tr,),
        in_specs=[blk] * 4,
        out_specs=[blk] * 3,
        out_shape=[o, o, o],
        compiler_params=_cparams(("parallel",)),
        name=name,
    )(w, g, m, v)


def _pack(arrs):
    flat = jnp.concatenate([a.reshape(-1) for a in arrs])
    pad = (-flat.shape[0]) % (8 * LANES)
    return jnp.pad(flat, (0, pad)).reshape(-1, LANES)


def _unpack(packed, shapes):
    flat = packed.reshape(-1)
    out, off = [], 0
    for s in shapes:
        n = 1
        for dim in s:
            n *= dim
        out.append(flat[off:off + n].reshape(s))
        off += n
    return out


def _pad_rows(a, rows):
    return jnp.pad(a, ((0, rows - a.shape[0]), (0, 0)))


def _pad_lanes(a, lanes=LANES):
    return jnp.pad(a, ((0, 0), (0, lanes - a.shape[1])))


def _local_grads(x2d, tgt2d, prm, get_w, on_grads, *, nb, seq, dep=None):
    g_pre, g_post, g_fpre, g_fpost = prm["norm_mix_pre"], prm["norm_mix_post"], prm["norm_ffn_pre"], prm["norm_ffn_post"]
    dt_bias, a_log = _pad_lanes(prm["ssd_dt_bias"]), _pad_lanes(prm["ssd_a_log"])
    d_lanes = jnp.repeat(prm["ssd_d"], HEAD_DIM, axis=1)

    h = _rms_fwd(x2d, g_pre, dep=dep, out_dtype=BF16, name="rms_mix_pre")
    t, d = x2d.shape
    w_in_t, w_dt_t, cw, sw = get_w("in", h)
    proj = _matmul([(h, w_in_t)], mode="nt", out_dtype=F32, tm=1024, tn=1024, tk=2048, name="mm_proj",
                   extent=(t, D_MAIN, d))
    dtr = _matmul([(h, w_dt_t)], mode="nt", out_dtype=F32, tm=1024, tn=128, tk=2048, name="mm_dt")
    u1, uy = _conv_branch_fwd(proj, cw, prm["conv_dw_b"], prm["conv_ln_g"], prm["conv_ln_b"], nb=nb, seq=seq)
    xs_all = _ssd_pre_fwd(proj, sw, prm["ssd_conv_b"], nb=nb, seq=seq)
    uy, ys, states = _ssd_fwd(uy, xs_all, proj, dtr, dt_bias, a_log, d_lanes, prm["ssd_norm_w"], nb=nb, seq=seq)
    w_out = get_w("out", uy)
    mix = _matmul([(uy, w_out)], mode="nn", out_dtype=F32, tm=1024, tn=1024, tk=2048, name="mm_mix")
    x1, h2, h2_t = _rms_post_pre(mix, x2d, g_post, g_fpre)
    w_gate, w_up = get_w("up", h2)
    gt, up, act, act_t = _ffn_up(h2, w_gate, w_up, tm=1024, tn=512)
    w_down = get_w("down", act)
    dx2, df, loss, dg_fpost = _down_loss(act, w_down, x1, tgt2d, g_fpost, tk=1408)

    dgt, dup = _ffn_bwd_act(df, w_down, gt, up, tm=1024, tn=512)
    dw_down = _matmul([(act_t, df)], mode="nn", out_dtype=F32, tm=1408, tn=1024, tk=2048, name="mm_dw_down",
                      also_bf16=True)
    dw_gate = _matmul([(h2_t, dgt)], mode="nn", out_dtype=F32, tm=1024, tn=1408, tk=2048, name="mm_dw_gate",
                      slot_out=True, also_bf16=True)
    dw_up = _matmul([(h2_t, dup)], mode="nn", out_dtype=F32, tm=1024, tn=1408, tk=2048, name="mm_dw_up",
                    slot_out=True, also_bf16=True)
    dep = on_grads("ffn", (dw_down, dw_gate, dw_up))
    dh2 = _matmul([(dgt, w_gate), (dup, w_up)], mode="nt", out_dtype=F32, tm=1024, tn=1024, tk=1408, name="mm_dh2",
                  dep=dep)
    dep = on_grads("ffn_sums", dh2)
    dx1, dmix, dg_fpre, dg_post = _rms_bwd_pre_post(dh2, x1, g_fpre, dx2, mix, g_post, dep=dep)
    dw_out = _matmul([(uy, dmix)], mode="tn", out_dtype=F32, tm=1024, tn=1024, tk=2048, name="mm_dw_out",
                     also_bf16=True)
    dep = on_grads("out", (dw_out,))
    duy = _matmul([(dmix, w_out)], mode="nt", out_dtype=F32, tm=1024, tn=1024, tk=2048, name="mm_duy", dep=dep)
    dproj, dcw, dcb, dlg, dlb = _conv_branch_bwd(duy, u1, proj, cw, prm["conv_ln_g"], prm["conv_ln_b"], nb=nb, seq=seq)
    dproj, dxs, ddtr, ssd_small = _ssd_bwd(dproj, duy, proj, ys, xs_all, dtr, states, dt_bias, a_log, d_lanes,
                                           prm["ssd_norm_w"], nb=nb, seq=seq)
    dproj, dsw, dsb = _ssd_pre_bwd(dproj, dxs, proj, sw, prm["ssd_conv_b"], nb=nb, seq=seq)
    ddtr_b = ddtr.astype(BF16)
    dw_in_t = _matmul([(dproj, h)], mode="tn", out_dtype=F32, tm=1024, tn=1024, tk=2048, name="mm_dw_main",
                      extent=(D_MAIN, d, t), out_rows=D_IN)
    dw_in_t = _dw_dt_rows(dw_in_t, ddtr_b, h)
    dep = on_grads("in", ((dw_in_t, dw_in_t),))
    dx, dg_pre = _dh_dx(dproj, w_in_t, ddtr_b, w_dt_t, x2d, dx1, g_pre, tk=1280, dep=g_pre if dep is None else dep)

    grads = {
        "norm_mix_pre": dg_pre,
        "w_in": dw_in_t,
        "conv_dw_w": dcw[:CONV_K], "conv_dw_b": dcb, "conv_ln_g": dlg, "conv_ln_b": dlb,
        "ssd_conv_w": dsw[:SSD_CONV_K], "ssd_conv_b": dsb,
        "ssd_dt_bias": ssd_small[2:3, :HEADS], "ssd_a_log": ssd_small[3:4, :HEADS], "ssd_d": ssd_small[1:2, :HEADS],
        "ssd_norm_w": ssd_small[0:1],
        "w_out": dw_out[0],
        "norm_mix_post": dg_post, "norm_ffn_pre": dg_fpre,
        "w_gate": dw_gate[0], "w_up": dw_up[0],
        "w_down": dw_down[0], "norm_ffn_post": dg_fpost,
    }
    return loss, dx, grads


BIG = ("w_in", "w_out", "w_gate", "w_up", "w_down")
HALF_AXIS = {"w_in": 1, "w_out": 0, "w_gate": 0, "w_up": 0, "w_down": 0}
GATHER_STAGES = {"in": ("w_in", "conv_dw_w", "ssd_conv_w"), "out": ("w_out",), "up": ("w_gate", "w_up"),
                 "down": ("w_down",)}
GATHER_ORDER = tuple(n for st in ("in", "out", "up", "down") for n in GATHER_STAGES[st])
SMALL = ("norm_mix_pre", "conv_dw_w", "conv_dw_b", "conv_ln_g", "conv_ln_b", "ssd_conv_w", "ssd_conv_b", "ssd_dt_bias",
         "ssd_a_log", "ssd_d", "ssd_norm_w", "norm_mix_post", "norm_ffn_pre", "norm_ffn_post")
WEIGHTS = ("norm_mix_pre", "w_in", "conv_dw_w", "conv_dw_b", "conv_ln_g", "conv_ln_b", "ssd_conv_w", "ssd_conv_b",
           "ssd_dt_bias", "ssd_a_log", "ssd_d", "ssd_norm_w", "w_out", "norm_mix_post", "norm_ffn_pre", "w_gate", "w_up",
           "w_down", "norm_ffn_post")


def _cols_from_slots(a):
    n, rows, w = a.shape
    return a.transpose(1, 0, 2).reshape(rows, n * w)


def kernel(x, norm_mix_pre, w_in, conv_dw_w, conv_dw_b, conv_ln_g, conv_ln_b, ssd_conv_w, ssd_conv_b, ssd_dt_bias, ssd_a_log, ssd_d, ssd_norm_w, w_out, norm_mix_post, norm_ffn_pre, w_gate, w_up, w_down, norm_ffn_post, loss_target, m_norm_mix_pre, m_w_in, m_conv_dw_w, m_conv_dw_b, m_conv_ln_g, m_conv_ln_b, m_ssd_conv_w, m_ssd_conv_b, m_ssd_dt_bias, m_ssd_a_log, m_ssd_d, m_ssd_norm_w, m_w_out, m_norm_mix_post, m_norm_ffn_pre, m_w_gate, m_w_up, m_w_down, m_norm_ffn_post, v_norm_mix_pre, v_w_in, v_conv_dw_w, v_conv_dw_b, v_conv_ln_g, v_conv_ln_b, v_ssd_conv_w, v_ssd_conv_b, v_ssd_dt_bias, v_ssd_a_log, v_ssd_d, v_ssd_norm_w, v_w_out, v_norm_mix_post, v_norm_ffn_pre, v_w_gate, v_w_up, v_w_down, v_norm_ffn_post):
    args = dict(locals())
    two_d = lambda n, a: jnp.swapaxes(a, 1, 2)[0] if n == "w_in" else a.reshape(a.shape[-2:])
    wts = {n: two_d(n, args[n]) for n in WEIGHTS}
    ms = {n: two_d(n, args["m_" + n]) for n in WEIGHTS}
    vs = {n: two_d(n, args["v_" + n]) for n in WEIGHTS}
    nb, seq, d = x.shape
    t = nb * seq
    xi, yi, ci = _coords()
    chip = 2 * xi + yi
    where = jnp.stack([ci, chip]).astype(jnp.int32)

    shards = {n: wts[n].astype(BF16) for n in BIG}
    shards.update(conv_dw_w=_pad_rows(wts["conv_dw_w"], 32), ssd_conv_w=_pad_rows(wts["ssd_conv_w"], 8))
    plans = [_gather_plan(HALF_AXIS.get(n)) for n in GATHER_ORDER]
    n_first = len(GATHER_STAGES["in"])

    def start(part, name, dep=None):
        src = [shards[n] for n in GATHER_ORDER[part]]
        return _split_start(src, [lax.empty((N_CHIPS,) + s.shape, s.dtype) for s in src], plans[part], name=name,
                            dep=dep)

    head = start(slice(0, n_first), "gather_start_in")
    rest = start(slice(n_first, None), "gather_start_rest", dep=head[4])
    ssems, rsems, srcs, lands = (list(h) + list(r) for h, r in zip(head[:4], rest[:4]))
    token = rest[4]

    forwarding = {}

    def arrived(stage, after):
        names = GATHER_STAGES[stage]
        pick = lambda seq_: [seq_[GATHER_ORDER.index(n)] for n in names]
        own, got = _split_wait(pick(ssems), pick(rsems), pick(srcs), pick(lands), pick(plans), after,
                               name="gather_wait_" + stage)
        return dict(zip(names, own)), dict(zip(names, got))

    def get_w(stage, after):
        names = GATHER_STAGES[stage]
        if stage in forwarding:
            own, sems = forwarding.pop(stage)
            fwd_own, fwd_got = _split_wait(*sems[:4], [_forward_plan(HALF_AXIS[n]) for n in names], after,
                                           name="gather_forward_wait_" + stage)
            got = dict(zip(names, fwd_got))
        else:
            own, got = arrived(stage, after)
            big = [n for n in names if n in BIG]
            got.update(zip(big, _forward_halves([got[n] for n in big], [HALF_AXIS[n] for n in big],
                                                name="gather_forward_" + stage)))
        if stage == "out":
            nxt = GATHER_STAGES["up"]
            up_own, up_got = arrived("up", after)
            forwarding["up"] = (up_own, _split_start([up_own[n] for n in nxt], [up_got[n] for n in nxt],
                                                     [_forward_plan(HALF_AXIS[n]) for n in nxt],
                                                     name="gather_forward_start_up"))
        full = {n: lax.dynamic_update_slice(got[n], own[n][None], (chip, 0, 0)) for n in got}
        if stage == "in":
            w_in_t = full["w_in"].reshape(D_IN, D_MODEL)
            return (w_in_t, _pad_rows(w_in_t[D_MAIN:], LANES), _cols_from_slots(full["conv_dw_w"]),
                    _cols_from_slots(full["ssd_conv_w"]))
        if stage == "out":
            return full["w_out"].reshape(D_MODEL, D_MODEL)
        if stage == "up":
            return _cols_from_slots(full["w_gate"]), _cols_from_slots(full["w_up"])
        return full["w_down"].reshape(D_FF, D_MODEL)

    reduce_groups = {"ffn": ("w_down", "w_gate", "w_up"), "out": ("w_out",), "in": ("w_in",)}
    in_flight = {}

    swapping = {}

    def send_to_owners(stage, f32s, kept, axes):
        sums = [_add_core_halves(f32, l, where, ax) for f32, l, ax in zip(f32s, kept, axes)]
        ps = [s[1] for s in sums]
        ssem, rsem, ps, recv, started = _split_start(
            ps, [lax.empty((COPIES,) + p.shape[1:], p.dtype) for p in ps], [_owners_plan] * len(ps),
            name="owners_start_" + stage)
        in_flight[stage] = (ssem, rsem, ps, recv, [s[0] for s in sums])
        return started

    def finish_swap(stage, after):
        ssem, rsem, sent, kept, f32s, axes = swapping.pop(stage)
        _, kept = _split_wait(ssem, rsem, sent, kept, [_halves_plan(ax) for ax in axes], after,
                              name="swap_other_halves_wait_" + stage)
        return send_to_owners(stage, f32s, kept, axes)

    def on_grads(stage, gs):
        if stage == "ffn_sums":
            return finish_swap("ffn", gs)
        names = reduce_groups[stage]
        axes = [HALF_AXIS[n] for n in names]
        slot = lambda g: g if g.ndim == 3 else g.reshape((N_CHIPS, g.shape[0] // N_CHIPS, g.shape[1]))
        f32s, sent = [slot(f32) for f32, _ in gs], [slot(b16) for _, b16 in gs]
        if stage in ("ffn", "in"):
            kept = [lax.empty(_halved_shape(b.shape, ax), b.dtype) for b, ax in zip(sent, axes)]
            ssem, rsem, sent, kept, started = _split_start(sent, kept, [_halves_plan(ax) for ax in axes],
                                                          name="swap_other_halves_start_" + stage)
            swapping[stage] = (ssem, rsem, sent, kept, f32s, axes)
            return started
        return send_to_owners(stage, f32s, _swap_other_halves(sent, axes, name="swap_other_halves_" + stage), axes)

    prm = {n: wts[n] for n in SMALL}
    loss, dx, grads = _local_grads(x.reshape(t, d), loss_target.reshape(t, d), prm, get_w, on_grads,
                                   nb=nb, seq=seq, dep=token)
    loss = lax.psum(loss[0, 0], MESH_AXES)
    in_started = finish_swap("in", dx)

    def reduced(stage, after):
        ssem, rsem, ps, recv, own_sums = in_flight[stage]
        _, recv = _split_wait(ssem, rsem, ps, recv, [_owners_plan] * len(ps), after, name="owners_wait_" + stage)
        return {n: _add_chip_sums(f32_sum, r, where, HALF_AXIS[n])
                for n, f32_sum, r in zip(reduce_groups[stage], own_sums, recv)}

    def swap_start(names, name, dep=None):
        mine = [halves[n] for n in names]
        return _split_start(mine, [lax.empty(h.shape, h.dtype) for h in mine], [_sibling_plan] * len(mine), name=name,
                            dep=dep)

    halves = reduced("ffn", in_started)
    ffn_swap = swap_start(reduce_groups["ffn"], "swap_reduced_start_ffn")
    halves.update(reduced("out", ffn_swap[4]))
    out_swap = swap_start(reduce_groups["out"], "swap_reduced_start_out", dep=ffn_swap[4])

    small_shapes = [grads[n].shape for n in SMALL]
    small_sum = _unpack(_all_reduce_small(_pack([grads[n] for n in SMALL])), small_shapes)
    small_grads = dict(zip(SMALL, small_sum))
    cwid, swid = D_CONV // N_CHIPS, D_XBC // N_CHIPS
    small_grads["conv_dw_w"] = lax.dynamic_slice(small_grads["conv_dw_w"], (0, chip * cwid), (CONV_K, cwid))
    small_grads["ssd_conv_w"] = lax.dynamic_slice(small_grads["ssd_conv_w"], (0, chip * swid), (SSD_CONV_K, swid))

    out_g, out_d, out_m, out_v = {}, {}, {}, {}
    shard_shapes = [wts[n].shape for n in SMALL]
    pd, pm, pv = _adamw(_pack([wts[n] for n in SMALL]), _pack([small_grads[n] for n in SMALL]),
                        _pack([ms[n] for n in SMALL]), _pack([vs[n] for n in SMALL]), name="adamw_small")
    for n, dd, mm, vv in zip(SMALL, _unpack(pd, shard_shapes), _unpack(pm, shard_shapes), _unpack(pv, shard_shapes)):
        out_g[n], out_d[n], out_m[n], out_v[n] = small_grads[n], dd, mm, vv

    def big_adamw(names, swap, after, name):
        mine, other = _split_wait(*swap[:4], [_sibling_plan] * len(names), after, name=name)
        for n, gm, go in zip(names, mine, other):
            out_g[n], out_d[n], out_m[n], out_v[n] = _adamw_halves(wts[n], gm, go, ms[n], vs[n], where, HALF_AXIS[n],
                                                                     name="adamw_" + n)

    big_adamw(reduce_groups["ffn"], ffn_swap, pd, "swap_reduced_wait_ffn")
    halves.update(reduced("in", out_d[reduce_groups["ffn"][-1]]))
    in_swap = swap_start(reduce_groups["in"], "swap_reduced_start_in")
    big_adamw(reduce_groups["out"], out_swap, in_swap[4], "swap_reduced_wait_out")
    big_adamw(reduce_groups["in"], in_swap, out_d[reduce_groups["out"][-1]], "swap_reduced_wait_in")

    back = lambda n, a: jnp.swapaxes(a[None], 1, 2) if n == "w_in" else a.reshape(args[n].shape)
    outs = [back(n, o[n]) for o in (out_g, out_d, out_m, out_v) for n in WEIGHTS]
    return (loss, dx.reshape(nb, seq, d), *outs)
```

```python
import functools

import jax
import jax.numpy as jnp
from jax import lax
from jax.experimental import pallas as pl
from jax.experimental.pallas import tpu as pltpu

F32 = jnp.float32
BF16 = jnp.bfloat16
EPS = 1e-6

D_MODEL = 2048
D_CONV = 1024
D_SSD = 1024
D_XBC = 2048
HEADS = 16
HEAD_DIM = 64
GROUPS = 4
STATE = 128
CONV_K = 31
SSD_CONV_K = 4
D_FF = 5632
D_MAIN = 2 * D_CONV + D_SSD + D_XBC
D_IN = D_MAIN + HEADS
N_CHIPS = 4
LANES = 128
CHUNK = 128
PAIRS = HEADS // 2

ADAM_LR = 0.001
ADAM_B1 = 0.9
ADAM_B2 = 0.999
ADAM_EPS = 1e-08
ADAM_WD = 0.01
ADAM_STEP = 10

MESH_AXES = ("x", "y", "c")
VMEM_LIMIT = 56 * 1024 * 1024


def _sig(v):
    return 1.0 / (1.0 + jnp.exp(-v))


def _cparams(sem, vmem=VMEM_LIMIT):
    return pltpu.CompilerParams(dimension_semantics=sem, vmem_limit_bytes=vmem)


_DIMS = {"nn": ((1,), (0,)), "nt": ((1,), (1,)), "tn": ((0,), (0,))}


def _matmul(pairs, *, mode, out_dtype, tm, tn, tk, name, slot_out=False, extent=None, out_rows=None, dep=None,
            also_bf16=False):
    a0, b0 = pairs[0]
    if mode == "nn":
        (m, k), n = a0.shape, b0.shape[1]
    elif mode == "nt":
        (m, k), n = a0.shape, b0.shape[0]
    else:
        (k, m), n = a0.shape, b0.shape[1]
    if extent is not None:
        m, n, k = extent
    tm, tn, tk = min(tm, m), min(tn, n), min(tk, k)
    assert m % tm == 0 and n % tn == 0 and k % tk == 0, (name, m, n, k, tm, tn, tk)
    nk = k // tk
    npairs = len(pairs)
    deps = [] if dep is None else [dep]
    dims = (_DIMS[mode], ((), ()))

    use_scratch = nk > 1 and out_dtype != F32

    def body(*refs):
        ins, o_ref = refs[: 2 * npairs], refs[2 * npairs + len(deps)]
        dot = lambda p: lax.dot_general(ins[2 * p][...], ins[2 * p + 1][...], dims, preferred_element_type=F32)
        if nk == 1:
            part = dot(0)
            for p in range(1, npairs):
                part = part + dot(p)
            o_ref[...] = part.astype(out_dtype)
            if also_bf16:
                refs[2 * npairs + len(deps) + 1][...] = part.astype(BF16)
            return
        acc = refs[-1] if use_scratch else o_ref
        kk = pl.program_id(2)

        @pl.when(kk == 0)
        def _():
            acc[...] = jnp.zeros_like(acc)

        for p in range(npairs):
            acc[...] += dot(p)

        if use_scratch:
            @pl.when(kk == nk - 1)
            def _():
                o_ref[...] = acc[...].astype(out_dtype)

        if also_bf16:
            @pl.when(kk == nk - 1)
            def _():
                refs[2 * npairs + len(deps) + 1][...] = acc[...].astype(BF16)

    if mode == "nn":
        a_spec = pl.BlockSpec((tm, tk), lambda i, j, kk: (i, kk))
        b_spec = pl.BlockSpec((tk, tn), lambda i, j, kk: (kk, j))
    elif mode == "nt":
        a_spec = pl.BlockSpec((tm, tk), lambda i, j, kk: (i, kk))
        b_spec = pl.BlockSpec((tn, tk), lambda i, j, kk: (j, kk))
    else:
        a_spec = pl.BlockSpec((tk, tm), lambda i, j, kk: (kk, i))
        b_spec = pl.BlockSpec((tk, tn), lambda i, j, kk: (kk, j))
    if slot_out:
        out_shape = jax.ShapeDtypeStruct((n // tn, m, tn), out_dtype)
        out_spec = pl.BlockSpec((None, tm, tn), lambda i, j, kk: (j, i, 0))
    else:
        out_shape = jax.ShapeDtypeStruct((m if out_rows is None else out_rows, n), out_dtype)
        out_spec = pl.BlockSpec((tm, tn), lambda i, j, kk: (i, j))
    flat = [t for ab in pairs for t in ab]
    if also_bf16:
        out_spec = [out_spec, out_spec]
        out_shape = [out_shape, jax.ShapeDtypeStruct(out_shape.shape, BF16)]
    return pl.pallas_call(
        body,
        grid=(m // tm, n // tn, nk),
        in_specs=[a_spec, b_spec] * npairs + [pl.BlockSpec(memory_space=pl.ANY)] * len(deps),
        out_specs=out_spec,
        out_shape=out_shape,
        scratch_shapes=[pltpu.VMEM((tm, tn), F32)] if use_scratch else [],
        compiler_params=_cparams(("parallel", "parallel", "arbitrary")),
        name=name,
    )(*flat, *deps)


SUB_ROWS = 256


def _ffn_up(h2, wg, wu, *, tm, tn):
    t, k = h2.shape
    n = wg.shape[1]
    tm = min(tm, t)
    assert t % tm == 0 and n % tn == 0, (t, n, tm, tn)

    sub = min(SUB_ROWS, tm)

    def body(h_ref, wg_ref, wu_ref, g_ref, u_ref, a_ref, at_ref):
        for r in range(tm // sub):
            rows = pl.ds(r * sub, sub)
            hv = h_ref[rows, :]
            g = jnp.dot(hv, wg_ref[...], preferred_element_type=F32)
            u = jnp.dot(hv, wu_ref[...], preferred_element_type=F32)
            g_ref[rows, :] = g.astype(BF16)
            u_ref[rows, :] = u.astype(BF16)
            act = (g * _sig(g) * u).astype(BF16)
            a_ref[rows, :] = act
            at_ref[:, rows] = act.T

    o = jax.ShapeDtypeStruct((t, n), BF16)
    ospec = pl.BlockSpec((tm, tn), lambda i, j: (i, j))
    return pl.pallas_call(
        body,
        grid=(t // tm, n // tn),
        in_specs=[pl.BlockSpec((tm, k), lambda i, j: (i, 0)), pl.BlockSpec((k, tn), lambda i, j: (0, j)),
                  pl.BlockSpec((k, tn), lambda i, j: (0, j))],
        out_specs=[ospec, ospec, ospec, pl.BlockSpec((tn, tm), lambda i, j: (j, i))],
        out_shape=[o, o, o, jax.ShapeDtypeStruct((n, t), BF16)],
        compiler_params=_cparams(("parallel", "parallel")),
        name="ffn_up",
    )(h2, wg, wu)


def _ffn_bwd_act(df, wd, gt, up, *, tm, tn):
    t, k = df.shape
    n = wd.shape[0]
    tm = min(tm, t)
    assert t % tm == 0 and n % tn == 0, (t, n, tm, tn)

    sub = min(SUB_ROWS, tm)

    def body(df_ref, wd_ref, g_ref, u_ref, dg_ref, du_ref):
        for r in range(tm // sub):
            rows = pl.ds(r * sub, sub)
            da = lax.dot_general(df_ref[rows, :], wd_ref[...], (_DIMS["nt"], ((), ())), preferred_element_type=F32)
            g = g_ref[rows, :].astype(F32)
            u = u_ref[rows, :].astype(F32)
            s = _sig(g)
            dg_ref[rows, :] = (da * u * s * (1.0 + g * (1.0 - s))).astype(BF16)
            du_ref[rows, :] = (da * g * s).astype(BF16)

    o = jax.ShapeDtypeStruct((t, n), BF16)
    blk = pl.BlockSpec((tm, tn), lambda i, j: (i, j))
    return pl.pallas_call(
        body,
        grid=(t // tm, n // tn),
        in_specs=[pl.BlockSpec((tm, k), lambda i, j: (i, 0)), pl.BlockSpec((tn, k), lambda i, j: (j, 0)), blk, blk],
        out_specs=[blk, blk],
        out_shape=[o, o],
        compiler_params=_cparams(("parallel", "parallel")),
        name="ffn_bwd_act",
    )(df, wd, gt, up)


def _dw_dt_rows(dw_in_t, ddtr_b, h, *, tk=1024):
    t, d = h.shape
    tk = min(tk, t)
    nk = t // tk

    def body(buf_ref, d_ref, h_ref, o_ref, acc):
        kk = pl.program_id(0)

        @pl.when(kk == 0)
        def _():
            acc[...] = jnp.zeros_like(acc)

        acc[...] += lax.dot_general(d_ref[...], h_ref[...], (_DIMS["tn"], ((), ())), preferred_element_type=F32)

        @pl.when(kk == nk - 1)
        def _():
            o_ref[...] = acc[0:HEADS, :]

    return pl.pallas_call(
        body,
        grid=(nk,),
        in_specs=[DEP_SPEC, pl.BlockSpec((tk, LANES), lambda kk: (kk, 0)), pl.BlockSpec((tk, d), lambda kk: (kk, 0))],
        out_specs=pl.BlockSpec((HEADS, d), lambda kk: (D_MAIN // HEADS, 0)),
        out_shape=jax.ShapeDtypeStruct(dw_in_t.shape, F32),
        input_output_aliases={0: 0},
        scratch_shapes=[pltpu.VMEM((LANES, d), F32)],
        compiler_params=_cparams(("arbitrary",)),
        name="mm_dw_dt",
    )(dw_in_t, ddtr_b, h)


ROW_TILE = 256


DEP_SPEC = pl.BlockSpec(memory_space=pl.ANY)


def _rms_fwd(xv, g, *, dep=None, out_dtype, name):
    t, d = xv.shape
    deps = [] if dep is None else [dep]

    def body(*refs):
        x_ref, g_ref = refs[0], refs[1]
        o_ref = refs[-1]
        v = x_ref[...]
        r = lax.rsqrt(jnp.mean(v * v, axis=-1, keepdims=True) + EPS)
        o_ref[...] = (v * r * g_ref[...]).astype(out_dtype)

    row = pl.BlockSpec((ROW_TILE, d), lambda i: (i, 0))
    vec = pl.BlockSpec((1, d), lambda i: (0, 0))
    return pl.pallas_call(
        body,
        grid=(t // ROW_TILE,),
        in_specs=[row, vec] + [DEP_SPEC] * len(deps),
        out_specs=row,
        out_shape=jax.ShapeDtypeStruct((t, d), out_dtype),
        compiler_params=_cparams(("parallel",)),
        name=name,
    )(*([xv, g] + deps))


def _rms_bwd_rows(dy, v, gv):
    r = lax.rsqrt(jnp.mean(v * v, axis=-1, keepdims=True) + EPS)
    xh = v * r
    gdy = dy * gv
    dx = r * (gdy - xh * jnp.mean(gdy * xh, axis=-1, keepdims=True))
    return dx, jnp.sum(dy * xh, axis=0, keepdims=True)


FUSED_ROWS = 512


def _matmul_rows_tail(a, w, tail, *, tk, row_ins, vec_ins, row_outs, vec_outs, first=None, dep=None, name):
    t, kdim = a.shape
    d = w.shape[1]
    tm, tk = min(FUSED_ROWS, t), min(tk, kdim)
    nk, nb = kdim // tk, t // tm
    assert t % tm == 0 and kdim % tk == 0
    n_ri, n_vi, n_ro, n_vo = len(row_ins), len(vec_ins), len(row_outs), len(vec_outs)
    n_first = 0 if first is None else 2
    deps = [] if dep is None else [dep]

    def body(*refs):
        a_ref, w_ref = refs[0], refs[1]
        first_refs = refs[2:2 + n_first]
        p = 2 + n_first
        ri = refs[p:p + n_ri]
        vi = refs[p + n_ri:p + n_ri + n_vi]
        p += n_ri + n_vi + len(deps)
        ro = refs[p:p + n_ro]
        vo = refs[p + n_ro:p + n_ro + n_vo]
        acc = refs[-1]
        i, kk = pl.program_id(0), pl.program_id(1)

        @pl.when(jnp.logical_and(i == 0, kk == 0))
        def _():
            for ref in vo:
                ref[...] = jnp.zeros_like(ref)

        @pl.when(kk == 0)
        def _():
            if first is None:
                acc[...] = jnp.zeros_like(acc)
            else:
                acc[...] = jnp.dot(first_refs[0][...], first_refs[1][...], preferred_element_type=F32)

        acc[...] += jnp.dot(a_ref[...], w_ref[...], preferred_element_type=F32)

        @pl.when(kk == nk - 1)
        def _():
            outs, parts = tail(acc[...], [r[...] for r in ri], [v[...] for v in vi])
            for ref, val in zip(ro, outs):
                ref[...] = val.astype(ref.dtype)
            for ref, part in zip(vo, parts):
                ref[...] += part

    row = pl.BlockSpec((tm, d), lambda i, kk: (i, 0))
    const = lambda shape: pl.BlockSpec(shape, lambda i, kk: (0,) * len(shape))
    in_specs = [pl.BlockSpec((tm, tk), lambda i, kk: (i, kk)), pl.BlockSpec((tk, d), lambda i, kk: (kk, 0))]
    if first is not None:
        in_specs += [pl.BlockSpec((tm, first[0].shape[1]), lambda i, kk: (i, 0)), const(first[1].shape)]
    in_specs += [row] * n_ri + [const(v.shape) for v in vec_ins] + [pl.BlockSpec(memory_space=pl.ANY)] * len(deps)
    return pl.pallas_call(
        body,
        grid=(nb, nk),
        in_specs=in_specs,
        out_specs=[row] * n_ro + [const(sh) for sh in vec_outs],
        out_shape=[jax.ShapeDtypeStruct((t, d), dt) for dt in row_outs]
        + [jax.ShapeDtypeStruct(sh, F32) for sh in vec_outs],
        scratch_shapes=[pltpu.VMEM((tm, d), F32)],
        compiler_params=_cparams(("arbitrary", "arbitrary")),
        name=name,
    )(a, w, *([] if first is None else list(first)), *row_ins, *vec_ins, *deps)


def _down_loss(act, w_down, x1, tgt, g, *, tk):
    d = w_down.shape[1]

    def tail(v, rows, vecs):
        x1v, tv = rows
        gv, = vecs
        fh = v * lax.rsqrt(jnp.mean(v * v, axis=-1, keepdims=True) + EPS)
        e = x1v + fh * gv - tv
        dx2 = e * (1.0 / d)
        df, dg = _rms_bwd_rows(dx2, v, gv)
        loss = 0.5 * jnp.sum(jnp.mean(e * e, axis=-1, keepdims=True), axis=0, keepdims=True)
        return (dx2, df), (loss, dg)

    return _matmul_rows_tail(act, w_down, tail, tk=tk, row_ins=[x1, tgt], vec_ins=[g], row_outs=[F32, BF16],
                             vec_outs=[(1, 1), (1, d)], name="mm_down_loss")


def _dh_dx(dproj, w_in_t, ddtr_b, w_dt_t, xv, dx1, g, *, tk, dep):
    def tail(v, rows, vecs):
        xr, dx1r = rows
        dx, dg = _rms_bwd_rows(v, xr, vecs[0])
        return (dx + dx1r,), (dg,)

    return _matmul_rows_tail(dproj, w_in_t, tail, tk=tk, row_ins=[xv, dx1], vec_ins=[g], row_outs=[F32],
                             vec_outs=[(1, xv.shape[1])], first=(ddtr_b, w_dt_t), dep=dep, name="mm_dh_dx")


def _rms_post_pre(mix, xv, g_post, g_pre):
    t, d = mix.shape

    def body(m_ref, x_ref, gp_ref, gf_ref, x1_ref, h2_ref, h2t_ref):
        v = m_ref[...]
        x1 = x_ref[...] + v * lax.rsqrt(jnp.mean(v * v, axis=-1, keepdims=True) + EPS) * gp_ref[...]
        x1_ref[...] = x1
        h2 = (x1 * lax.rsqrt(jnp.mean(x1 * x1, axis=-1, keepdims=True) + EPS) * gf_ref[...]).astype(BF16)
        h2_ref[...] = h2
        h2t_ref[...] = h2.T

    row = pl.BlockSpec((ROW_TILE, d), lambda i: (i, 0))
    vec = pl.BlockSpec((1, d), lambda i: (0, 0))
    return pl.pallas_call(
        body,
        grid=(t // ROW_TILE,),
        in_specs=[row, row, vec, vec],
        out_specs=[row, row, pl.BlockSpec((d, ROW_TILE), lambda i: (0, i))],
        out_shape=[jax.ShapeDtypeStruct((t, d), F32), jax.ShapeDtypeStruct((t, d), BF16),
                   jax.ShapeDtypeStruct((d, t), BF16)],
        compiler_params=_cparams(("parallel",)),
        name="rms_mix_post_ffn_pre",
    )(mix, xv, g_post, g_pre)


def _rms_bwd_pre_post(dh2, x1, g_pre, dx2, mix, g_post, *, dep=None):
    t, d = x1.shape

    deps = [] if dep is None else [dep]

    def body(dh_ref, x1_ref, gf_ref, dx2_ref, m_ref, gp_ref, *rest):
        dx1_ref, dmix_ref, dgf_ref, dgp_ref = rest[len(deps):]

        @pl.when(pl.program_id(0) == 0)
        def _():
            dgf_ref[...] = jnp.zeros_like(dgf_ref)
            dgp_ref[...] = jnp.zeros_like(dgp_ref)

        dx, dgf = _rms_bwd_rows(dh_ref[...], x1_ref[...], gf_ref[...])
        dx1 = dx + dx2_ref[...]
        dx1_ref[...] = dx1
        dmix, dgp = _rms_bwd_rows(dx1, m_ref[...], gp_ref[...])
        dmix_ref[...] = dmix.astype(BF16)
        dgf_ref[...] += dgf
        dgp_ref[...] += dgp

    row = pl.BlockSpec((ROW_TILE, d), lambda i: (i, 0))
    vec = pl.BlockSpec((1, d), lambda i: (0, 0))
    return pl.pallas_call(
        body,
        grid=(t // ROW_TILE,),
        in_specs=[row, row, vec, row, row, vec] + [DEP_SPEC] * len(deps),
        out_specs=[row, row, vec, vec],
        out_shape=[jax.ShapeDtypeStruct((t, d), F32), jax.ShapeDtypeStruct((t, d), BF16),
                   jax.ShapeDtypeStruct((1, d), F32), jax.ShapeDtypeStruct((1, d), F32)],
        compiler_params=_cparams(("arbitrary",)),
        name="rms_ffn_pre_mix_post_bwd",
    )(dh2, x1, g_pre, dx2, mix, g_post, *deps)


CONV_ROWS = 512
TAP_ROWS = 64
HALO31 = 32
HALO4 = 8


def _sum8(v):
    return jnp.sum(v.reshape(v.shape[0] // 8, 8, v.shape[1]), axis=0)


SUBLANES = 8
PHASE_SPAN = (CONV_K - 1) // SUBLANES * SUBLANES


def _phase_scratch(ts):
    return pltpu.VMEM((SUBLANES, ts + PHASE_SPAN, LANES), F32)


def _phase_copies(ph, buf, ln, base, ts):
    for s in range(SUBLANES):
        n = ts + (CONV_K - 1 - s) // SUBLANES * SUBLANES
        ph[s, 0:n, :] = buf[pl.ds(base + s, n), ln]


def _tap_rows(ph, off, r0):
    s = off % SUBLANES
    return ph[s, pl.ds(r0 + off - s, TAP_ROWS), :]


def _conv_branch_fwd(proj, cw, cb, lg, lb, *, nb, seq):
    ts, c, halo = CONV_ROWS, D_CONV, HALO31
    ns = seq // ts
    base = halo - CONV_K + 1

    def body(ca_ref, cg_ref, w_ref, b_ref, lg_ref, lb_ref, u1_ref, u_ref, ubuf, uph):
        i = pl.program_id(1)

        @pl.when(i == 0)
        def _():
            ubuf[0:halo, :] = jnp.zeros((halo, c), F32)

        @pl.when(i > 0)
        def _():
            ubuf[0:halo, :] = ubuf[ts:ts + halo, :]

        ubuf[halo:halo + ts, :] = ca_ref[...] * _sig(cg_ref[...])

        def lane_tile(j, carry):
            ln = pl.ds(pl.multiple_of(j * LANES, LANES), LANES)
            _phase_copies(uph, ubuf, ln, base, ts)
            for r in range(ts // TAP_ROWS):
                acc = jnp.broadcast_to(b_ref[:, ln], (TAP_ROWS, LANES))
                for k in range(CONV_K):
                    acc = acc + w_ref[pl.ds(k, 1), ln] * _tap_rows(uph, k, r * TAP_ROWS)
                u1_ref[pl.ds(r * TAP_ROWS, TAP_ROWS), ln] = acc
            return carry

        lax.fori_loop(0, c // LANES, lane_tile, 0)
        v = u1_ref[...]
        mu = jnp.mean(v, axis=-1, keepdims=True)
        dv = v - mu
        xh = dv * lax.rsqrt(jnp.mean(dv * dv, axis=-1, keepdims=True) + EPS)
        u2 = xh * lg_ref[...] + lb_ref[...]
        u_ref[...] = (u2 * _sig(u2)).astype(BF16)

    t = nb * seq
    row = lambda col: pl.BlockSpec((ts, c), lambda b, i: (b * ns + i, col))
    vec = pl.BlockSpec((1, c), lambda b, i: (0, 0))
    return pl.pallas_call(
        body,
        grid=(nb, ns),
        in_specs=[row(0), row(1), pl.BlockSpec((32, c), lambda b, i: (0, 0)), vec, vec, vec],
        out_specs=[row(0), row(0)],
        out_shape=[jax.ShapeDtypeStruct((t, c), F32), jax.ShapeDtypeStruct((t, c + D_SSD), BF16)],
        scratch_shapes=[pltpu.VMEM((halo + ts, c), F32), _phase_scratch(ts)],
        compiler_params=_cparams(("parallel", "arbitrary")),
        name="conv_branch_fwd",
    )(proj, proj, cw, cb, lg, lb)


def _conv_branch_bwd(duy, u1, proj, cw, lg, lb, *, nb, seq):
    ts, c, halo = CONV_ROWS, D_CONV, HALO31
    ns = seq // ts
    base = halo - CONV_K + 1
    hb = ts // halo

    def body(du_ref, u1_ref, ca_ref, cg_ref, cah_ref, cgh_ref, w_ref, lg_ref, lb_ref,
             dcacg_ref, dw_ref, db_ref, dlg_ref, dlb_ref,
             ubuf, dbuf, du0buf, dwacc, dbacc, dlgacc, dlbacc, uph, dph):
        b, i = pl.program_id(0), pl.program_id(1)
        rc = ns - 1 - i

        @pl.when(jnp.logical_and(b == 0, i == 0))
        def _():
            dwacc[...] = jnp.zeros_like(dwacc)
            dbacc[...] = jnp.zeros_like(dbacc)
            dlgacc[...] = jnp.zeros_like(dlgacc)
            dlbacc[...] = jnp.zeros_like(dlbacc)

        @pl.when(i == 0)
        def _():
            dbuf[ts:ts + halo, :] = jnp.zeros((halo, c), F32)

        @pl.when(i > 0)
        def _():
            dbuf[ts:ts + halo, :] = dbuf[0:halo, :]

        v = u1_ref[...]
        mu = jnp.mean(v, axis=-1, keepdims=True)
        dv = v - mu
        rstd = lax.rsqrt(jnp.mean(dv * dv, axis=-1, keepdims=True) + EPS)
        xh = dv * rstd
        lgv = lg_ref[...]
        u2 = xh * lgv + lb_ref[...]
        s2 = _sig(u2)
        du2 = du_ref[...] * (s2 * (1.0 + u2 * (1.0 - s2)))
        dlgacc[...] += jnp.sum(du2 * xh, axis=0, keepdims=True)
        dlbacc[...] += jnp.sum(du2, axis=0, keepdims=True)
        gd = du2 * lgv
        du1 = rstd * (gd - jnp.mean(gd, axis=-1, keepdims=True) - xh * jnp.mean(gd * xh, axis=-1, keepdims=True))
        dbacc[...] += jnp.sum(du1, axis=0, keepdims=True)
        dbuf[0:ts, :] = du1

        @pl.when(rc == 0)
        def _():
            ubuf[0:halo, :] = jnp.zeros((halo, c), F32)

        @pl.when(rc > 0)
        def _():
            ubuf[0:halo, :] = cah_ref[...] * _sig(cgh_ref[...])

        cav = ca_ref[...]
        sg = _sig(cg_ref[...])
        ubuf[halo:halo + ts, :] = cav * sg

        def lane_tile(j, carry):
            ln = pl.ds(pl.multiple_of(j * LANES, LANES), LANES)
            _phase_copies(uph, ubuf, ln, base, ts)
            _phase_copies(dph, dbuf, ln, 0, ts)
            for r in range(ts // TAP_ROWS):
                r0 = r * TAP_ROWS
                d1 = dbuf[pl.ds(r0, TAP_ROWS), ln]
                acc = jnp.zeros((TAP_ROWS, LANES), F32)
                for k in range(CONV_K):
                    acc = acc + w_ref[pl.ds(k, 1), ln] * _tap_rows(dph, CONV_K - 1 - k, r0)
                    dwacc[pl.ds(k * 8, 8), ln] += _sum8(d1 * _tap_rows(uph, k, r0))
                du0buf[pl.ds(r0, TAP_ROWS), ln] = acc
            return carry

        lax.fori_loop(0, c // LANES, lane_tile, 0)
        du0 = du0buf[...]
        dcacg_ref[:, 0:c] = (du0 * sg).astype(BF16)
        dcacg_ref[:, c:2 * c] = (du0 * cav * sg * (1.0 - sg)).astype(BF16)

        @pl.when(jnp.logical_and(b == nb - 1, i == ns - 1))
        def _():
            for k in range(CONV_K):
                dw_ref[pl.ds(k, 1), :] = jnp.sum(dwacc[pl.ds(k * 8, 8), :], axis=0, keepdims=True)
            dw_ref[pl.ds(CONV_K, 1), :] = jnp.zeros((1, c), F32)
            db_ref[...] = dbacc[...]
            dlg_ref[...] = dlgacc[...]
            dlb_ref[...] = dlbacc[...]

    t = nb * seq
    rowblk = lambda b, i: b * ns + (ns - 1 - i)
    row = lambda col: pl.BlockSpec((ts, c), lambda b, i: (rowblk(b, i), col))
    hrow = lambda col: pl.BlockSpec((halo, c), lambda b, i: (jnp.maximum(rowblk(b, i) * hb - 1, 0), col))
    vec = pl.BlockSpec((1, c), lambda b, i: (0, 0))
    wspec = pl.BlockSpec((32, c), lambda b, i: (0, 0))
    return pl.pallas_call(
        body,
        grid=(nb, ns),
        in_specs=[row(0), row(0), row(0), row(1), hrow(0), hrow(1), wspec, vec, vec],
        out_specs=[pl.BlockSpec((ts, 2 * c), lambda b, i: (rowblk(b, i), 0)), wspec, vec, vec, vec],
        out_shape=[jax.ShapeDtypeStruct((t, D_MAIN), BF16), jax.ShapeDtypeStruct((32, c), F32),
                   jax.ShapeDtypeStruct((1, c), F32), jax.ShapeDtypeStruct((1, c), F32), jax.ShapeDtypeStruct((1, c), F32)],
        scratch_shapes=[pltpu.VMEM((halo + ts, c), F32), pltpu.VMEM((ts + halo, c), F32), pltpu.VMEM((ts, c), F32),
                        pltpu.VMEM((CONV_K * 8, c), F32), pltpu.VMEM((1, c), F32), pltpu.VMEM((1, c), F32),
                        pltpu.VMEM((1, c), F32), _phase_scratch(ts), _phase_scratch(ts)],
        compiler_params=_cparams(("arbitrary", "arbitrary")),
        name="conv_branch_bwd",
    )(duy, u1, proj, proj, proj, proj, cw, lg, lb)


XBC_COL0 = (2 * D_CONV + D_SSD) // 1024


def _ssd_pre_fwd(proj, sw, sb, *, nb, seq):
    ts, c, halo = CONV_ROWS, 1024, HALO4
    ns = seq // ts
    base = halo - SSD_CONV_K + 1

    def body(x_ref, w_ref, b_ref, o_ref, xbuf):
        i = pl.program_id(2)

        @pl.when(i == 0)
        def _():
            xbuf[0:halo, :] = jnp.zeros((halo, c), F32)

        @pl.when(i > 0)
        def _():
            xbuf[0:halo, :] = xbuf[ts:ts + halo, :]

        xbuf[halo:halo + ts, :] = x_ref[...]

        def lane_tile(j, carry):
            ln = pl.ds(pl.multiple_of(j * LANES, LANES), LANES)
            for r in range(ts // TAP_ROWS):
                acc = jnp.broadcast_to(b_ref[:, ln], (TAP_ROWS, LANES))
                for k in range(SSD_CONV_K):
                    acc = acc + w_ref[pl.ds(k, 1), ln] * xbuf[pl.ds(r * TAP_ROWS + base + k, TAP_ROWS), ln]
                o_ref[pl.ds(r * TAP_ROWS, TAP_ROWS), ln] = acc * _sig(acc)
            return carry

        lax.fori_loop(0, c // LANES, lane_tile, 0)

    t = nb * seq
    return pl.pallas_call(
        body,
        grid=(2, nb, ns),
        in_specs=[pl.BlockSpec((ts, c), lambda j, b, i: (b * ns + i, XBC_COL0 + j)),
                  pl.BlockSpec((8, c), lambda j, b, i: (0, j)), pl.BlockSpec((1, c), lambda j, b, i: (0, j))],
        out_specs=pl.BlockSpec((ts, c), lambda j, b, i: (b * ns + i, j)),
        out_shape=jax.ShapeDtypeStruct((t, D_XBC), F32),
        scratch_shapes=[pltpu.VMEM((halo + ts, c), F32)],
        compiler_params=_cparams(("parallel", "parallel", "arbitrary")),
        name="ssd_pre_fwd",
    )(proj, sw, sb)


def _ssd_pre_bwd(dproj, dxs, proj, sw, sb, *, nb, seq):
    ts, c, halo = CONV_ROWS, 1024, HALO4
    ns = seq // ts
    base = halo - SSD_CONV_K + 1
    hb = ts // halo

    def body(dproj_ref, d_ref, x_ref, xh_ref, w_ref, b_ref, dx_ref, dw_ref, db_ref, xbuf, dbuf, dwacc, dbacc):
        b, i = pl.program_id(1), pl.program_id(2)
        rc = ns - 1 - i

        @pl.when(jnp.logical_and(b == 0, i == 0))
        def _():
            dwacc[...] = jnp.zeros_like(dwacc)
            dbacc[...] = jnp.zeros_like(dbacc)

        @pl.when(i == 0)
        def _():
            dbuf[ts:ts + halo, :] = jnp.zeros((halo, c), F32)

        @pl.when(i > 0)
        def _():
            dbuf[ts:ts + halo, :] = dbuf[0:halo, :]

        @pl.when(rc == 0)
        def _():
            xbuf[0:halo, :] = jnp.zeros((halo, c), F32)

        @pl.when(rc > 0)
        def _():
            xbuf[0:halo, :] = xh_ref[...]

        xbuf[halo:halo + ts, :] = x_ref[...]

        def pre_tile(j, carry):
            ln = pl.ds(pl.multiple_of(j * LANES, LANES), LANES)
            for r in range(ts // TAP_ROWS):
                r0 = r * TAP_ROWS
                acc = jnp.broadcast_to(b_ref[:, ln], (TAP_ROWS, LANES))
                for k in range(SSD_CONV_K):
                    acc = acc + w_ref[pl.ds(k, 1), ln] * xbuf[pl.ds(r0 + base + k, TAP_ROWS), ln]
                s = _sig(acc)
                dc = d_ref[pl.ds(r0, TAP_ROWS), ln] * (s * (1.0 + acc * (1.0 - s)))
                dbuf[pl.ds(r0, TAP_ROWS), ln] = dc
                dbacc[:, ln] += _sum8(dc)
            return carry

        lax.fori_loop(0, c // LANES, pre_tile, 0)

        def lane_tile(j, carry):
            ln = pl.ds(pl.multiple_of(j * LANES, LANES), LANES)
            for r in range(ts // TAP_ROWS):
                r0 = r * TAP_ROWS
                d1 = dbuf[pl.ds(r0, TAP_ROWS), ln]
                acc = jnp.zeros((TAP_ROWS, LANES), F32)
                for k in range(SSD_CONV_K):
                    acc = acc + w_ref[pl.ds(k, 1), ln] * dbuf[pl.ds(r0 + SSD_CONV_K - 1 - k, TAP_ROWS), ln]
                    dwacc[pl.ds(k * 8, 8), ln] += _sum8(d1 * xbuf[pl.ds(r0 + base + k, TAP_ROWS), ln])
                dx_ref[pl.ds(r0, TAP_ROWS), ln] = acc.astype(BF16)
            return carry

        lax.fori_loop(0, c // LANES, lane_tile, 0)

        @pl.when(jnp.logical_and(b == nb - 1, i == ns - 1))
        def _():
            for k in range(SSD_CONV_K):
                dw_ref[pl.ds(k, 1), :] = jnp.sum(dwacc[pl.ds(k * 8, 8), :], axis=0, keepdims=True)
            dw_ref[pl.ds(SSD_CONV_K, 8 - SSD_CONV_K), :] = jnp.zeros((8 - SSD_CONV_K, c), F32)
            db_ref[...] = jnp.sum(dbacc[...], axis=0, keepdims=True)

    t = nb * seq
    rowblk = lambda b, i: b * ns + (ns - 1 - i)
    return pl.pallas_call(
        body,
        grid=(2, nb, ns),
        in_specs=[DEP_SPEC, pl.BlockSpec((ts, c), lambda j, b, i: (rowblk(b, i), j)),
                  pl.BlockSpec((ts, c), lambda j, b, i: (rowblk(b, i), XBC_COL0 + j)),
                  pl.BlockSpec((halo, c), lambda j, b, i: (jnp.maximum(rowblk(b, i) * hb - 1, 0), XBC_COL0 + j)),
                  pl.BlockSpec((8, c), lambda j, b, i: (0, j)), pl.BlockSpec((1, c), lambda j, b, i: (0, j))],
        out_specs=[pl.BlockSpec((ts, c), lambda j, b, i: (rowblk(b, i), XBC_COL0 + j)),
                   pl.BlockSpec((8, c), lambda j, b, i: (0, j)), pl.BlockSpec((1, c), lambda j, b, i: (0, j))],
        out_shape=[jax.ShapeDtypeStruct(dproj.shape, BF16), jax.ShapeDtypeStruct((8, D_XBC), F32),
                   jax.ShapeDtypeStruct((1, D_XBC), F32)],
        input_output_aliases={0: 0},
        scratch_shapes=[pltpu.VMEM((halo + ts, c), F32), pltpu.VMEM((ts + halo, c), F32),
                        pltpu.VMEM((SSD_CONV_K * 8, c), F32), pltpu.VMEM((8, c), F32)],
        compiler_params=_cparams(("arbitrary", "arbitrary", "arbitrary")),
        name="ssd_pre_bwd",
    )(dproj, dxs, proj, proj, sw, sb)


Z_COL = (2 * D_CONV) // 1024
GROUP_W = D_SSD // GROUPS


def _softplus(v):
    return jnp.maximum(v, 0.0) + jnp.log(1.0 + jnp.exp(-jnp.abs(v)))


def _dot(a, b):
    return jnp.dot(a, b, preferred_element_type=F32)


def _dot_nt(a, b):
    return lax.dot_general(a, b, (_DIMS["nt"], ((), ())), preferred_element_type=F32)


def _dot_tn(a, b):
    return lax.dot_general(a, b, (_DIMS["tn"], ((), ())), preferred_element_type=F32)


def _bf16_terms(v):
    hi = v.astype(BF16)
    r1 = v - hi.astype(F32)
    mid = r1.astype(BF16)
    return hi, mid, (r1 - mid.astype(F32)).astype(BF16)


def _dot_exact_left(sel, v):
    hi, mid, lo = _bf16_terms(v)
    return _dot(sel, hi) + (_dot(sel, mid) + _dot(sel, lo))


def _dot_exact_right(v, sel):
    hi, mid, lo = _bf16_terms(v)
    return _dot(hi, sel) + (_dot(mid, sel) + _dot(lo, sel))


def _chunk_decays(dtr_ref, bias_ref, alog_ref):
    q = CHUNK
    ii = lax.broadcasted_iota(jnp.int32, (q, q), 0)
    jj = lax.broadcasted_iota(jnp.int32, (q, q), 1)
    tri = jj <= ii
    dt = _softplus(dtr_ref[...] + bias_ref[...])
    a_head = -jnp.exp(alog_ref[...])
    cs = _dot_exact_left(tri.astype(BF16), dt * a_head)
    return tri, dt, a_head, cs, cs.T


def _ssd_fwd(uy, xs_all, proj, dtr, dt_bias, a_log, d_lanes, norm_w, *, nb, seq):
    q = CHUNK
    nc = seq // q
    t = nb * seq

    def body(uy_ref, xs_ref, bm_ref, cm_ref, z_ref, dtr_ref, bias_ref, alog_ref, dl_ref, nw_ref,
             y_ref, ys_ref, st_ref, state):
        @pl.when(pl.program_id(1) == 0)
        def _():
            state[...] = jnp.zeros_like(state)

        tri, dt, _, cs, cst = _chunk_decays(dtr_ref, bias_ref, alog_ref)
        first = lax.broadcasted_iota(jnp.int32, (1, LANES), 1) < HEAD_DIM
        for g in range(GROUPS):
            gl = slice(g * STATE, (g + 1) * STATE)
            bb = bm_ref[:, gl].astype(BF16)
            cb = cm_ref[:, gl].astype(BF16)
            scores = _dot_nt(cb, bb)
            for p in range(2):
                pr = 2 * g + p
                h0 = 2 * pr
                sl = slice(pr * LANES, (pr + 1) * LANES)
                xv = xs_ref[:, sl]
                dtp = jnp.where(first, dt[:, h0:h0 + 1], dt[:, h0 + 1:h0 + 2])
                csp = jnp.where(first, cs[:, h0:h0 + 1], cs[:, h0 + 1:h0 + 2])
                xd = xv * dtp
                yv = None
                for hh, keep in ((h0, first), (h0 + 1, jnp.logical_not(first))):
                    decay = jnp.where(tri, jnp.exp(cs[:, hh:hh + 1] - cst[hh:hh + 1, :]), 0.0)
                    part = _dot((scores * decay).astype(BF16), jnp.where(keep, xd, 0.0).astype(BF16))
                    yv = part if yv is None else yv + part
                hp = state[pr]
                st_ref[0, pr] = hp
                yv = yv + jnp.exp(csp) * _dot(cb, hp.astype(BF16))
                last = csp[q - 1:q, :]
                state[pr] = jnp.exp(last) * hp + _dot_tn(bb, (xd * jnp.exp(last - csp)).astype(BF16))
                ys_ref[:, sl] = yv + dl_ref[:, sl] * xv
        zv = z_ref[...]
        gated = ys_ref[...] * (zv * _sig(zv))
        for g in range(GROUPS):
            gl = slice(g * GROUP_W, (g + 1) * GROUP_W)
            v = gated[:, gl]
            r = lax.rsqrt(jnp.mean(v * v, axis=-1, keepdims=True) + EPS)
            y_ref[:, gl] = (v * r * nw_ref[:, gl]).astype(BF16)

    blk = lambda w, col: pl.BlockSpec((q, w), lambda b, c: (b * nc + c, col))
    vec = lambda w: pl.BlockSpec((1, w), lambda b, c: (0, 0))
    return pl.pallas_call(
        body,
        grid=(nb, nc),
        in_specs=[DEP_SPEC, blk(D_SSD, 0), blk(GROUPS * STATE, 2), blk(GROUPS * STATE, 3), blk(D_SSD, Z_COL),
                  blk(LANES, 0), vec(LANES), vec(LANES), vec(D_SSD), vec(D_SSD)],
        out_specs=[blk(D_SSD, 1), blk(D_SSD, 0),
                   pl.BlockSpec((1, PAIRS, STATE, LANES), lambda b, c: (b * nc + c, 0, 0, 0))],
        out_shape=[jax.ShapeDtypeStruct(uy.shape, BF16), jax.ShapeDtypeStruct((t, D_SSD), F32),
                   jax.ShapeDtypeStruct((nb * nc, PAIRS, STATE, LANES), F32)],
        input_output_aliases={0: 0},
        scratch_shapes=[pltpu.VMEM((PAIRS, STATE, LANES), F32)],
        compiler_params=_cparams(("parallel", "arbitrary")),
        name="ssd_fwd",
    )(uy, xs_all, xs_all, xs_all, proj, dtr, dt_bias, a_log, d_lanes, norm_w)


def _ssd_bwd(dproj, duy, proj, ys, xs_all, dtr, states, dt_bias, a_log, d_lanes, norm_w, *, nb, seq):
    q = CHUNK
    nc = seq // q
    t = nb * seq
    head_of_lane = (jnp.arange(D_SSD)[:, None] // HEAD_DIM == jnp.arange(LANES)[None, :]).astype(BF16)

    def body(dproj_ref, dy_ref, z_ref, ys_ref, xs_ref, bm_ref, cm_ref, dtr_ref, st_ref, bias_ref, alog_ref, dl_ref,
             nw_ref, sel_ref, dz_ref, dx_ref, ddtr_ref, small_ref,
             dstate, dys_buf, dcsl, ddtl, dcst, dnw_acc, dd_acc, dbias_acc, da_acc):
        b, c = pl.program_id(0), pl.program_id(1)

        @pl.when(jnp.logical_and(b == 0, c == 0))
        def _():
            dnw_acc[...] = jnp.zeros_like(dnw_acc)
            dd_acc[...] = jnp.zeros_like(dd_acc)
            dbias_acc[...] = jnp.zeros_like(dbias_acc)
            da_acc[...] = jnp.zeros_like(da_acc)
            dcst[...] = jnp.zeros_like(dcst)

        @pl.when(c == 0)
        def _():
            dstate[...] = jnp.zeros_like(dstate)

        zv = z_ref[...]
        sz = _sig(zv)
        silz = zv * sz
        ysv = ys_ref[...]
        gated = ysv * silz
        dyv = dy_ref[...]
        nwv = nw_ref[...]
        for g in range(GROUPS):
            gl = slice(g * GROUP_W, (g + 1) * GROUP_W)
            v = gated[:, gl]
            r = lax.rsqrt(jnp.mean(v * v, axis=-1, keepdims=True) + EPS)
            yn = v * r
            dyn = dyv[:, gl] * nwv[:, gl]
            dnw_acc[:, gl] += jnp.sum(dyv[:, gl] * yn, axis=0, keepdims=True)
            dys_buf[:, gl] = r * (dyn - yn * jnp.mean(dyn * yn, axis=-1, keepdims=True))
        dgated = dys_buf[...]
        dz_ref[...] = (dgated * ysv * (sz * (1.0 + zv * (1.0 - sz)))).astype(BF16)
        dys_all = dgated * silz
        dys_buf[...] = dys_all
        dd_acc[...] += jnp.sum(dys_all * xs_ref[...], axis=0, keepdims=True)

        tri, dt, a_head, cs, cst = _chunk_decays(dtr_ref, bias_ref, alog_ref)
        lane = lax.broadcasted_iota(jnp.int32, (1, LANES), 1)
        first = lane < HEAD_DIM
        dcs_h = jnp.zeros((q, LANES), F32)
        for g in range(GROUPS):
            gl = slice(g * STATE, (g + 1) * STATE)
            bb = bm_ref[:, gl].astype(BF16)
            cb = cm_ref[:, gl].astype(BF16)
            scores = _dot_nt(cb, bb)
            dscores = jnp.zeros((q, q), F32)
            dbg = jnp.zeros((q, STATE), F32)
            dcg = jnp.zeros((q, STATE), F32)
            for p in range(2):
                pr = 2 * g + p
                h0 = 2 * pr
                sl = slice(pr * LANES, (pr + 1) * LANES)
                xv = xs_ref[:, sl]
                dyp = dys_buf[:, sl]
                dtp = jnp.where(first, dt[:, h0:h0 + 1], dt[:, h0 + 1:h0 + 2])
                csp = jnp.where(first, cs[:, h0:h0 + 1], cs[:, h0 + 1:h0 + 2])
                xd = xv * dtp
                xdb = xd.astype(BF16)
                hp = st_ref[0, pr]
                dhn = dstate[pr]
                hpb = hp.astype(BF16)
                dhnb = dhn.astype(BF16)
                lam = jnp.exp(csp)
                last = csp[q - 1:q, :]
                gam = jnp.exp(last)
                w = jnp.exp(last - csp)
                dxd = jnp.zeros((q, LANES), F32)
                for hh, keep in ((h0, first), (h0 + 1, jnp.logical_not(first))):
                    decay = jnp.where(tri, jnp.exp(cs[:, hh:hh + 1] - cst[hh:hh + 1, :]), 0.0)
                    m = scores * decay
                    dym = jnp.where(keep, dyp, 0.0).astype(BF16)
                    dm = _dot_nt(dym, xdb)
                    dxd = dxd + _dot_tn(m.astype(BF16), dym)
                    e = dm * m
                    dcs_h = dcs_h + jnp.where(lane == hh, jnp.sum(e, axis=1, keepdims=True), 0.0)
                    dcst[hh:hh + 1, :] = jnp.sum(e, axis=0, keepdims=True)
                    dscores = dscores + dm * decay
                yoff = lam * _dot(cb, hpb)
                ldy = (lam * dyp).astype(BF16)
                dcg = dcg + _dot_nt(ldy, hpb)
                dstate[pr] = gam * dhn + _dot_tn(cb, ldy)
                bdh = _dot(bb, dhnb)
                dxd = dxd + w * bdh
                xdw = xd * w
                dbg = dbg + _dot_nt(xdw.astype(BF16), dhnb)
                wd = xdw * bdh
                dcsl[:, sl] = dyp * yoff - wd
                dcsl[q - 1:q, sl] += (jnp.sum(wd, axis=0, keepdims=True)
                                      + gam * jnp.sum(dhn * hp, axis=0, keepdims=True))
                dx_ref[:, sl] = dxd * dtp + dyp * dl_ref[:, sl]
                ddtl[:, sl] = dxd * xv
            dsb = dscores.astype(BF16)
            dx_ref[:, D_SSD + g * STATE:D_SSD + (g + 1) * STATE] = dbg + _dot_tn(dsb, cb)
            dx_ref[:, D_SSD + (GROUPS + g) * STATE:D_SSD + (GROUPS + g + 1) * STATE] = dcg + _dot(dsb, bb)

        sel = sel_ref[...]
        dcs_h = dcs_h + _dot_exact_right(dcsl[...], sel) - dcst[...].T
        ddt = _dot_exact_right(ddtl[...], sel)
        upper = lax.broadcasted_iota(jnp.int32, (q, q), 1) >= lax.broadcasted_iota(jnp.int32, (q, q), 0)
        da = _dot_exact_left(upper.astype(BF16), dcs_h)
        ddt = ddt + da * a_head
        da_acc[...] += jnp.sum(da * dt, axis=0, keepdims=True)
        ddtr = ddt * _sig(dtr_ref[...] + bias_ref[...])
        ddtr_ref[...] = ddtr
        dbias_acc[...] += jnp.sum(ddtr, axis=0, keepdims=True)

        @pl.when(jnp.logical_and(b == nb - 1, c == nc - 1))
        def _():
            small_ref[...] = jnp.zeros_like(small_ref)
            small_ref[0:1, :] = dnw_acc[...]
            small_ref[1:2, 0:LANES] = _dot_exact_right(jnp.broadcast_to(dd_acc[...], (8, D_SSD)), sel)[0:1, :]
            small_ref[2:3, 0:LANES] = dbias_acc[...]
            small_ref[3:4, 0:LANES] = da_acc[...] * a_head

    rowblk = lambda b, c: b * nc + (nc - 1 - c)
    blk = lambda w, col: pl.BlockSpec((q, w), lambda b, c: (rowblk(b, c), col))
    vec = lambda w: pl.BlockSpec((1, w), lambda b, c: (0, 0))
    return pl.pallas_call(
        body,
        grid=(nb, nc),
        in_specs=[DEP_SPEC, blk(D_SSD, 1), blk(D_SSD, Z_COL), blk(D_SSD, 0), blk(D_SSD, 0), blk(GROUPS * STATE, 2),
                  blk(GROUPS * STATE, 3), blk(LANES, 0),
                  pl.BlockSpec((1, PAIRS, STATE, LANES), lambda b, c: (rowblk(b, c), 0, 0, 0)),
                  vec(LANES), vec(LANES), vec(D_SSD), vec(D_SSD), pl.BlockSpec((D_SSD, LANES), lambda b, c: (0, 0))],
        out_specs=[blk(D_SSD, Z_COL), blk(D_XBC, 0), blk(LANES, 0), pl.BlockSpec((8, D_SSD), lambda b, c: (0, 0))],
        out_shape=[jax.ShapeDtypeStruct(dproj.shape, BF16), jax.ShapeDtypeStruct((t, D_XBC), F32),
                   jax.ShapeDtypeStruct((t, LANES), F32), jax.ShapeDtypeStruct((8, D_SSD), F32)],
        input_output_aliases={0: 0},
        scratch_shapes=[pltpu.VMEM((PAIRS, STATE, LANES), F32), pltpu.VMEM((q, D_SSD), F32),
                        pltpu.VMEM((q, D_SSD), F32), pltpu.VMEM((q, D_SSD), F32), pltpu.VMEM((LANES, q), F32),
                        pltpu.VMEM((1, D_SSD), F32), pltpu.VMEM((1, D_SSD), F32), pltpu.VMEM((1, LANES), F32),
                        pltpu.VMEM((1, LANES), F32)],
        compiler_params=_cparams(("arbitrary", "arbitrary")),
        name="ssd_bwd",
    )(dproj, duy, proj, ys, xs_all, xs_all, xs_all, dtr, states, dt_bias, a_log, d_lanes, norm_w, head_of_lane)


HBM_SPEC = pl.BlockSpec(memory_space=pltpu.HBM)
MESH_ID = pl.DeviceIdType.MESH


def _coords():
    return lax.axis_index("x"), lax.axis_index("y"), lax.axis_index("c")


def _chip_peer(xi, yi, ci, d):
    return (jnp.bitwise_xor(xi, d >> 1), jnp.bitwise_xor(yi, d & 1), ci)


def _remote(src, dst, send_sem, recv_sem, peer):
    return pltpu.make_async_remote_copy(src_ref=src, dst_ref=dst, send_sem=send_sem, recv_sem=recv_sem,
                                        device_id=peer, device_id_type=MESH_ID)


SEM_SPEC = pl.BlockSpec(memory_space=pltpu.SEMAPHORE)
ANY_SPEC = pl.BlockSpec(memory_space=pl.ANY)
EFFECT = pltpu.SideEffectType.DATAFLOW_SIDE_EFFECTING
COPIES = 3


def _half(ref, axis, which, lead=0):
    size = ref.shape[lead + axis] // 2
    part = pl.ds(which * size, size)
    idx = (slice(None),) * lead + ((part, slice(None)) if axis == 0 else (slice(None), part))
    return ref.at[idx]


def _halved_shape(shape, axis):
    lead = len(shape) - 2
    return tuple(d // 2 if i == lead + axis else d for i, d in enumerate(shape))


def _gather_plan(axis):
    def plan(xi, yi, ci, src, land):
        me = 2 * xi + yi
        out = []
        for d in (1, 2, 3):
            there = jnp.bitwise_xor(me, d)
            if axis is None:
                out.append((src, land.at[me], _chip_peer(xi, yi, ci, d), land.at[there]))
            else:
                out.append((_half(src, axis, ci), _half(land.at[me], axis, ci), _chip_peer(xi, yi, ci, d),
                            _half(land.at[there], axis, ci)))
        return out
    return plan


def _halves_plan(axis):
    def plan(xi, yi, ci, src, land):
        return [(_half(src, axis, 1 - ci, lead=1), land, (xi, yi, 1 - ci), land)]
    return plan


def _forward_plan(axis):
    def plan(xi, yi, ci, src, land):
        me = 2 * xi + yi
        out = []
        for d in (1, 2, 3):
            slot = land.at[jnp.bitwise_xor(me, d)]
            out.append((_half(slot, axis, ci), _half(slot, axis, ci), (xi, yi, 1 - ci), _half(slot, axis, 1 - ci)))
        return out
    return plan


def _sibling_plan(xi, yi, ci, src, land):
    return [(src, land, (xi, yi, 1 - ci), land)]


def _owners_plan(xi, yi, ci, src, land):
    me = 2 * xi + yi
    return [(src.at[jnp.bitwise_xor(me, d)], land.at[d - 1], _chip_peer(xi, yi, ci, d), land.at[d - 1])
            for d in (1, 2, 3)]


def _split_start(srcs, lands, plans, *, name, dep=None):
    n = len(srcs)
    deps = [] if dep is None else [dep]

    def body(*refs):
        src_refs, land_refs = refs[:n], refs[n:2 * n]
        outs = refs[2 * n + len(deps):]
        ssems, rsems = outs[:n], outs[n:2 * n]
        token = refs[-1]
        xi, yi, ci = _coords()
        for t in range(n):
            for k, (src, dst, peer, _) in enumerate(plans[t](xi, yi, ci, src_refs[t], land_refs[t])):
                _remote(src, dst, ssems[t].at[k], rsems[t].at[k], peer).start()
        token[...] = jnp.zeros_like(token)

    bufs = list(srcs) + list(lands)
    outs = pl.pallas_call(
        body,
        name=name,
        in_specs=[HBM_SPEC] * (2 * n) + [ANY_SPEC] * len(deps),
        out_specs=[SEM_SPEC] * (2 * n) + [HBM_SPEC] * (2 * n) + [pl.BlockSpec(memory_space=pltpu.VMEM)],
        out_shape=[pltpu.SemaphoreType.DMA((COPIES,))] * (2 * n) + [pltpu.HBM(a.shape, a.dtype) for a in bufs]
        + [jax.ShapeDtypeStruct((8, LANES), F32)],
        input_output_aliases={i: 2 * n + i for i in range(2 * n)},
        compiler_params=pltpu.CompilerParams(has_side_effects=EFFECT),
    )(*[pltpu.with_memory_space_constraint(a, pltpu.HBM) for a in bufs], *deps)
    return outs[:n], outs[n:2 * n], outs[2 * n:3 * n], outs[3 * n:4 * n], outs[-1]


def _split_wait(ssems, rsems, srcs, lands, plans, after, *, name):
    n = len(srcs)

    def body(*refs):
        src_refs, land_refs = refs[:n], refs[n:2 * n]
        ss, rs = refs[2 * n:3 * n], refs[3 * n:4 * n]
        xi, yi, ci = _coords()
        for t in range(n):
            for k, (src, _, peer, landed) in enumerate(plans[t](xi, yi, ci, src_refs[t], land_refs[t])):
                cp = _remote(src, landed, ss[t].at[k], rs[t].at[k], peer)
                cp.wait_send()
                cp.wait_recv()

    bufs = list(srcs) + list(lands)
    outs = pl.pallas_call(
        body,
        name=name,
        in_specs=[HBM_SPEC] * (2 * n) + [SEM_SPEC] * (2 * n) + [ANY_SPEC],
        out_specs=[HBM_SPEC] * (2 * n),
        out_shape=[pltpu.HBM(a.shape, a.dtype) for a in bufs],
        input_output_aliases={i: i for i in range(2 * n)},
        compiler_params=pltpu.CompilerParams(has_side_effects=EFFECT),
    )(*bufs, *ssems, *rsems, after)
    return outs[:n], outs[n:]


def _forward_halves(lands, axes, *, name):
    n = len(lands)

    def body(*refs):
        ins, outs = refs[:n], refs[n:2 * n]
        send_sems, recv_sems = refs[2 * n:]
        xi, yi, ci = _coords()
        me = 2 * xi + yi
        sibling = (xi, yi, 1 - ci)
        cps = []
        for t in range(n):
            for d in (1, 2, 3):
                slot = jnp.bitwise_xor(me, d)
                k = COPIES * t + d - 1
                cp = _remote(_half(ins[t].at[slot], axes[t], ci), _half(outs[t].at[slot], axes[t], ci),
                             send_sems.at[k], recv_sems.at[k], sibling)
                cp.start()
                cps.append(cp)
        for t in range(n):
            for d in (1, 2, 3):
                got = _half(outs[t].at[jnp.bitwise_xor(me, d)], axes[t], 1 - ci)
                k = COPIES * t + d - 1
                _remote(got, got, send_sems.at[k], recv_sems.at[k], sibling).wait_recv()
        for cp in cps:
            cp.wait_send()

    return pl.pallas_call(
        body,
        name=name,
        in_specs=[HBM_SPEC] * n,
        out_specs=[HBM_SPEC] * n,
        out_shape=[jax.ShapeDtypeStruct(a.shape, a.dtype) for a in lands],
        input_output_aliases={i: i for i in range(n)},
        scratch_shapes=[pltpu.SemaphoreType.DMA((COPIES * n,)), pltpu.SemaphoreType.DMA((COPIES * n,))],
    )(*lands)


def _swap_other_halves(gs, axes, *, name):
    n = len(gs)

    def body(*refs):
        ins, lands = refs[:n], refs[n:2 * n]
        send_sems, recv_sems = refs[2 * n:]
        xi, yi, ci = _coords()
        sibling = (xi, yi, 1 - ci)
        cps = []
        for t in range(n):
            cp = _remote(_half(ins[t], axes[t], 1 - ci, lead=1), lands[t], send_sems.at[t], recv_sems.at[t], sibling)
            cp.start()
            cps.append(cp)
        for cp in cps:
            cp.wait_recv()
        for cp in cps:
            cp.wait_send()

    return pl.pallas_call(
        body,
        in_specs=[HBM_SPEC] * n,
        out_specs=[HBM_SPEC] * n,
        out_shape=[jax.ShapeDtypeStruct(_halved_shape(g.shape, ax), g.dtype) for g, ax in zip(gs, axes)],
        scratch_shapes=[pltpu.SemaphoreType.DMA((n,)), pltpu.SemaphoreType.DMA((n,))],
        name=name,
    )(*gs)


def _row_tile(rows, cap=512, mult=16):
    best = mult
    for cand in range(mult, min(rows, cap) + 1, mult):
        if rows % cand == 0:
            best = cand
    assert rows % best == 0, rows
    return best


COL_TILE = 256


def _half_tiles(hr, hc, axis, cap=512, mult=16):
    if axis == 0:
        tr = _row_tile(hr, cap, mult)
        n = hr // tr
        return (tr, hc), n, lambda half, i: (half * n + i, 0)
    n = hc // COL_TILE
    return (hr, COL_TILE), n, lambda half, i: (0, half * n + i)


def _add_core_halves(g, land, where, axis):
    nslot, hr, hc = land.shape
    bshape, nr, idx = _half_tiles(hr, hc, axis)

    def body(where_ref, g_ref, l_ref, f_ref, b_ref):
        s = g_ref[...] + l_ref[...].astype(F32)
        b_ref[...] = s.astype(BF16)

        @pl.when(pl.program_id(1) == where_ref[1])
        def _():
            f_ref[...] = s

    blk = pl.BlockSpec((None,) + bshape, lambda i, s, w: (s,) + idx(0, i))
    mine = pl.BlockSpec((None,) + bshape, lambda i, s, w: (s,) + idx(w[0], i))
    return pl.pallas_call(
        body,
        grid_spec=pltpu.PrefetchScalarGridSpec(
            num_scalar_prefetch=1,
            grid=(nr, nslot),
            in_specs=[mine, blk],
            out_specs=[pl.BlockSpec(bshape, lambda i, s, w: idx(0, i)), blk],
        ),
        out_shape=[jax.ShapeDtypeStruct((hr, hc), F32), jax.ShapeDtypeStruct(land.shape, BF16)],
        compiler_params=_cparams(("parallel", "arbitrary")),
        name="add_core_halves",
    )(where, g, land)


def _add_chip_sums(pf, land, where, axis):
    hr, cols = pf.shape
    bshape, nr, idx = _half_tiles(hr, cols, axis)

    def body(where_ref, p_ref, l_ref, o_ref):
        acc = p_ref[...]
        for d in range(3):
            acc = acc + l_ref[d].astype(F32)
        o_ref[...] = acc

    return pl.pallas_call(
        body,
        grid_spec=pltpu.PrefetchScalarGridSpec(
            num_scalar_prefetch=1,
            grid=(nr,),
            in_specs=[pl.BlockSpec(bshape, lambda i, w: idx(0, i)),
                      pl.BlockSpec((3,) + bshape, lambda i, w: (0,) + idx(0, i))],
            out_specs=pl.BlockSpec(bshape, lambda i, w: idx(0, i)),
        ),
        out_shape=jax.ShapeDtypeStruct((hr, cols), F32),
        compiler_params=_cparams(("parallel",)),
        name="add_chip_sums",
    )(where, pf, land)


N_DEV = 8


def _all_reduce_small(part):
    r, w = part.shape

    def body(p_ref, o_ref, gath, send_sems, recv_sems):
        xi, yi, ci = _coords()
        me = 4 * xi + 2 * yi + ci
        gath[me] = p_ref[...]
        cps = []
        for d in range(1, N_DEV):
            peer = (jnp.bitwise_xor(xi, d >> 2), jnp.bitwise_xor(yi, (d >> 1) & 1), jnp.bitwise_xor(ci, d & 1))
            cp = _remote(p_ref, gath.at[me], send_sems.at[d - 1], recv_sems.at[d - 1], peer)
            cp.start()
            cps.append(cp)
        for d in range(1, N_DEV):
            src = gath.at[jnp.bitwise_xor(me, d)]
            _remote(src, src, send_sems.at[d - 1], recv_sems.at[d - 1], (xi, yi, ci)).wait_recv()
        acc = gath[0]
        for k in range(1, N_DEV):
            acc = acc + gath[k]
        o_ref[...] = acc
        for cp in cps:
            cp.wait_send()

    vm = pl.BlockSpec(memory_space=pltpu.VMEM)
    return pl.pallas_call(
        body,
        in_specs=[vm],
        out_specs=vm,
        out_shape=jax.ShapeDtypeStruct((r, w), F32),
        scratch_shapes=[pltpu.VMEM((N_DEV, r, w), F32), pltpu.SemaphoreType.DMA((N_DEV - 1,)),
                        pltpu.SemaphoreType.DMA((N_DEV - 1,))],
        name="all_reduce_small",
    )(part)


def _adamw_math(wv, gv, mv, vv):
    mn = ADAM_B1 * mv + (1.0 - ADAM_B1) * gv
    vn = ADAM_B2 * vv + (1.0 - ADAM_B2) * (gv * gv)
    m_hat = mn / (1.0 - ADAM_B1 ** ADAM_STEP)
    v_hat = vn / (1.0 - ADAM_B2 ** ADAM_STEP)
    return -ADAM_LR * (m_hat / (jnp.sqrt(v_hat) + ADAM_EPS) + ADAM_WD * wv), mn, vn


def _adamw_halves(w, g_mine, g_other, m, v, where, axis, *, name):
    rows, cols = w.shape
    hr, hc = g_mine.shape
    bshape, nr, idx = _half_tiles(hr, hc, axis, cap=256, mult=8)

    def body(where_ref, w_ref, gm_ref, go_ref, m_ref, v_ref, g_ref, d_ref, nm_ref, nv_ref):
        is_mine = pl.program_id(0) == where_ref[0]
        gv = jnp.where(is_mine, gm_ref[...], go_ref[...])
        g_ref[...] = gv
        d_ref[...], nm_ref[...], nv_ref[...] = _adamw_math(w_ref[...], gv, m_ref[...], v_ref[...])

    def parked(half, i, holder):
        return idx(0, jnp.where(half == holder, i, jnp.where(half < holder, 0, nr - 1)))

    blk = pl.BlockSpec(bshape, lambda hf, i, wh: idx(hf, i))
    o = jax.ShapeDtypeStruct((rows, cols), F32)
    return pl.pallas_call(
        body,
        grid_spec=pltpu.PrefetchScalarGridSpec(
            num_scalar_prefetch=1,
            grid=(2, nr),
            in_specs=[blk, pl.BlockSpec(bshape, lambda hf, i, wh: parked(hf, i, wh[0])),
                      pl.BlockSpec(bshape, lambda hf, i, wh: parked(hf, i, 1 - wh[0])), blk, blk],
            out_specs=[blk] * 4,
        ),
        out_shape=[o, o, o, o],
        compiler_params=_cparams(("arbitrary", "arbitrary")),
        name=name,
    )(where, w, g_mine, g_other, m, v)


def _adamw(w, g, m, v, *, name):
    rows, cols = w.shape
    tr = _row_tile(rows, cap=256, mult=8)

    def body(w_ref, g_ref, m_ref, v_ref, d_ref, nm_ref, nv_ref):
        d_ref[...], nm_ref[...], nv_ref[...] = _adamw_math(w_ref[...], g_ref[...], m_ref[...], v_ref[...])

    blk = pl.BlockSpec((tr, cols), lambda i: (i, 0))
    o = jax.ShapeDtypeStruct((rows, cols), F32)
    return pl.pallas_call(
        body,
        grid=(rows // tr,),
        in_specs=[blk] * 4,
        out_specs=[blk] * 3,
        out_shape=[o, o, o],
        compiler_params=_cparams(("parallel",)),
        name=name,
    )(w, g, m, v)


def _pack(arrs):
    flat = jnp.concatenate([a.reshape(-1) for a in arrs])
    pad = (-flat.shape[0]) % (8 * LANES)
    return jnp.pad(flat, (0, pad)).reshape(-1, LANES)


def _unpack(packed, shapes):
    flat = packed.reshape(-1)
    out, off = [], 0
    for s in shapes:
        n = 1
        for dim in s:
            n *= dim
        out.append(flat[off:off + n].reshape(s))
        off += n
    return out


def _pad_rows(a, rows):
    return jnp.pad(a, ((0, rows - a.shape[0]), (0, 0)))


def _pad_lanes(a, lanes=LANES):
    return jnp.pad(a, ((0, 0), (0, lanes - a.shape[1])))


def _local_grads(x2d, tgt2d, prm, get_w, on_grads, *, nb, seq, dep=None):
    g_pre, g_post, g_fpre, g_fpost = prm["norm_mix_pre"], prm["norm_mix_post"], prm["norm_ffn_pre"], prm["norm_ffn_post"]
    dt_bias, a_log = _pad_lanes(prm["ssd_dt_bias"]), _pad_lanes(prm["ssd_a_log"])
    d_lanes = jnp.repeat(prm["ssd_d"], HEAD_DIM, axis=1)

    h = _rms_fwd(x2d, g_pre, dep=dep, out_dtype=BF16, name="rms_mix_pre")
    t, d = x2d.shape
    w_in_t, w_dt_t, cw, sw = get_w("in", h)
    proj = _matmul([(h, w_in_t)], mode="nt", out_dtype=F32, tm=1024, tn=1024, tk=2048, name="mm_proj",
                   extent=(t, D_MAIN, d))
    dtr = _matmul([(h, w_dt_t)], mode="nt", out_dtype=F32, tm=1024, tn=128, tk=2048, name="mm_dt")
    u1, uy = _conv_branch_fwd(proj, cw, prm["conv_dw_b"], prm["conv_ln_g"], prm["conv_ln_b"], nb=nb, seq=seq)
    xs_all = _ssd_pre_fwd(proj, sw, prm["ssd_conv_b"], nb=nb, seq=seq)
    uy, ys, states = _ssd_fwd(uy, xs_all, proj, dtr, dt_bias, a_log, d_lanes, prm["ssd_norm_w"], nb=nb, seq=seq)
    w_out = get_w("out", uy)
    mix = _matmul([(uy, w_out)], mode="nn", out_dtype=F32, tm=1024, tn=1024, tk=2048, name="mm_mix")
    x1, h2, h2_t = _rms_post_pre(mix, x2d, g_post, g_fpre)
    w_gate, w_up = get_w("up", h2)
    gt, up, act, act_t = _ffn_up(h2, w_gate, w_up, tm=1024, tn=512)
    w_down = get_w("down", act)
    dx2, df, loss, dg_fpost = _down_loss(act, w_down, x1, tgt2d, g_fpost, tk=1408)

    dgt, dup = _ffn_bwd_act(df, w_down, gt, up, tm=1024, tn=512)
    dw_down = _matmul([(act_t, df)], mode="nn", out_dtype=F32, tm=1408, tn=1024, tk=2048, name="mm_dw_down",
                      also_bf16=True)
    dw_gate = _matmul([(h2_t, dgt)], mode="nn", out_dtype=F32, tm=1024, tn=1408, tk=2048, name="mm_dw_gate",
                      slot_out=True, also_bf16=True)
    dw_up = _matmul([(h2_t, dup)], mode="nn", out_dtype=F32, tm=1024, tn=1408, tk=2048, name="mm_dw_up",
                    slot_out=True, also_bf16=True)
    dep = on_grads("ffn", (dw_down, dw_gate, dw_up))
    dh2 = _matmul([(dgt, w_gate), (dup, w_up)], mode="nt", out_dtype=F32, tm=1024, tn=1024, tk=1408, name="mm_dh2",
                  dep=dep)
    dep = on_grads("ffn_sums", dh2)
    dx1, dmix, dg_fpre, dg_post = _rms_bwd_pre_post(dh2, x1, g_fpre, dx2, mix, g_post, dep=dep)
    dw_out = _matmul([(uy, dmix)], mode="tn", out_dtype=F32, tm=1024, tn=1024, tk=2048, name="mm_dw_out",
                     also_bf16=True)
    dep = on_grads("out", (dw_out,))
    duy = _matmul([(dmix, w_out)], mode="nt", out_dtype=F32, tm=1024, tn=1024, tk=2048, name="mm_duy", dep=dep)
    dproj, dcw, dcb, dlg, dlb = _conv_branch_bwd(duy, u1, proj, cw, prm["conv_ln_g"], prm["conv_ln_b"], nb=nb, seq=seq)
    dproj, dxs, ddtr, ssd_small = _ssd_bwd(dproj, duy, proj, ys, xs_all, dtr, states, dt_bias, a_log, d_lanes,
                                           prm["ssd_norm_w"], nb=nb, seq=seq)
    dproj, dsw, dsb = _ssd_pre_bwd(dproj, dxs, proj, sw, prm["ssd_conv_b"], nb=nb, seq=seq)
    ddtr_b = ddtr.astype(BF16)
    dw_in_t = _matmul([(dproj, h)], mode="tn", out_dtype=F32, tm=1024, tn=1024, tk=2048, name="mm_dw_main",
                      extent=(D_MAIN, d, t), out_rows=D_IN)
    dw_in_t = _dw_dt_rows(dw_in_t, ddtr_b, h)
    dep = on_grads("in", ((dw_in_t, dw_in_t),))
    dx, dg_pre = _dh_dx(dproj, w_in_t, ddtr_b, w_dt_t, x2d, dx1, g_pre, tk=1280, dep=g_pre if dep is None else dep)

    grads = {
        "norm_mix_pre": dg_pre,
        "w_in": dw_in_t,
        "conv_dw_w": dcw[:CONV_K], "conv_dw_b": dcb, "conv_ln_g": dlg, "conv_ln_b": dlb,
        "ssd_conv_w": dsw[:SSD_CONV_K], "ssd_conv_b": dsb,
        "ssd_dt_bias": ssd_small[2:3, :HEADS], "ssd_a_log": ssd_small[3:4, :HEADS], "ssd_d": ssd_small[1:2, :HEADS],
        "ssd_norm_w": ssd_small[0:1],
        "w_out": dw_out[0],
        "norm_mix_post": dg_post, "norm_ffn_pre": dg_fpre,
        "w_gate": dw_gate[0], "w_up": dw_up[0],
        "w_down": dw_down[0], "norm_ffn_post": dg_fpost,
    }
    return loss, dx, grads


BIG = ("w_in", "w_out", "w_gate", "w_up", "w_down")
HALF_AXIS = {"w_in": 1, "w_out": 0, "w_gate": 0, "w_up": 0, "w_down": 0}
GATHER_STAGES = {"in": ("w_in", "conv_dw_w", "ssd_conv_w"), "out": ("w_out",), "up": ("w_gate", "w_up"),
                 "down": ("w_down",)}
GATHER_ORDER = tuple(n for st in ("in", "out", "up", "down") for n in GATHER_STAGES[st])
SMALL = ("norm_mix_pre", "conv_dw_w", "conv_dw_b", "conv_ln_g", "conv_ln_b", "ssd_conv_w", "ssd_conv_b", "ssd_dt_bias",
         "ssd_a_log", "ssd_d", "ssd_norm_w", "norm_mix_post", "norm_ffn_pre", "norm_ffn_post")
WEIGHTS = ("norm_mix_pre", "w_in", "conv_dw_w", "conv_dw_b", "conv_ln_g", "conv_ln_b", "ssd_conv_w", "ssd_conv_b",
           "ssd_dt_bias", "ssd_a_log", "ssd_d", "ssd_norm_w", "w_out", "norm_mix_post", "norm_ffn_pre", "w_gate", "w_up",
           "w_down", "norm_ffn_post")


def _cols_from_slots(a):
    n, rows, w = a.shape
    return a.transpose(1, 0, 2).reshape(rows, n * w)


def kernel(x, norm_mix_pre, w_in, conv_dw_w, conv_dw_b, conv_ln_g, conv_ln_b, ssd_conv_w, ssd_conv_b, ssd_dt_bias, ssd_a_log, ssd_d, ssd_norm_w, w_out, norm_mix_post, norm_ffn_pre, w_gate, w_up, w_down, norm_ffn_post, loss_target, m_norm_mix_pre, m_w_in, m_conv_dw_w, m_conv_dw_b, m_conv_ln_g, m_conv_ln_b, m_ssd_conv_w, m_ssd_conv_b, m_ssd_dt_bias, m_ssd_a_log, m_ssd_d, m_ssd_norm_w, m_w_out, m_norm_mix_post, m_norm_ffn_pre, m_w_gate, m_w_up, m_w_down, m_norm_ffn_post, v_norm_mix_pre, v_w_in, v_conv_dw_w, v_conv_dw_b, v_conv_ln_g, v_conv_ln_b, v_ssd_conv_w, v_ssd_conv_b, v_ssd_dt_bias, v_ssd_a_log, v_ssd_d, v_ssd_norm_w, v_w_out, v_norm_mix_post, v_norm_ffn_pre, v_w_gate, v_w_up, v_w_down, v_norm_ffn_post):
    args = dict(locals())
    two_d = lambda n, a: jnp.swapaxes(a, 1, 2)[0] if n == "w_in" else a.reshape(a.shape[-2:])
    wts = {n: two_d(n, args[n]) for n in WEIGHTS}
    ms = {n: two_d(n, args["m_" + n]) for n in WEIGHTS}
    vs = {n: two_d(n, args["v_" + n]) for n in WEIGHTS}
    nb, seq, d = x.shape
    t = nb * seq
    xi, yi, ci = _coords()
    chip = 2 * xi + yi
    where = jnp.stack([ci, chip]).astype(jnp.int32)

    shards = {n: wts[n].astype(BF16) for n in BIG}
    shards.update(conv_dw_w=_pad_rows(wts["conv_dw_w"], 32), ssd_conv_w=_pad_rows(wts["ssd_conv_w"], 8))
    plans = [_gather_plan(HALF_AXIS.get(n)) for n in GATHER_ORDER]
    n_first = len(GATHER_STAGES["in"])

    def start(part, name, dep=None):
        src = [shards[n] for n in GATHER_ORDER[part]]
        return _split_start(src, [lax.empty((N_CHIPS,) + s.shape, s.dtype) for s in src], plans[part], name=name,
                            dep=dep)

    head = start(slice(0, n_first), "gather_start_in")
    rest = start(slice(n_first, None), "gather_start_rest", dep=head[4])
    ssems, rsems, srcs, lands = (list(h) + list(r) for h, r in zip(head[:4], rest[:4]))
    token = rest[4]

    forwarding = {}

    def arrived(stage, after):
        names = GATHER_STAGES[stage]
        pick = lambda seq_: [seq_[GATHER_ORDER.index(n)] for n in names]
        own, got = _split_wait(pick(ssems), pick(rsems), pick(srcs), pick(lands), pick(plans), after,
                               name="gather_wait_" + stage)
        return dict(zip(names, own)), dict(zip(names, got))

    def get_w(stage, after):
        names = GATHER_STAGES[stage]
        if stage in forwarding:
            own, sems = forwarding.pop(stage)
            fwd_own, fwd_got = _split_wait(*sems[:4], [_forward_plan(HALF_AXIS[n]) for n in names], after,
                                           name="gather_forward_wait_" + stage)
            got = dict(zip(names, fwd_got))
        else:
            own, got = arrived(stage, after)
            big = [n for n in names if n in BIG]
            got.update(zip(big, _forward_halves([got[n] for n in big], [HALF_AXIS[n] for n in big],
                                                name="gather_forward_" + stage)))
        if stage == "out":
            nxt = GATHER_STAGES["up"]
            up_own, up_got = arrived("up", after)
            forwarding["up"] = (up_own, _split_start([up_own[n] for n in nxt], [up_got[n] for n in nxt],
                                                     [_forward_plan(HALF_AXIS[n]) for n in nxt],
                                                     name="gather_forward_start_up"))
        full = {n: lax.dynamic_update_slice(got[n], own[n][None], (chip, 0, 0)) for n in got}
        if stage == "in":
            w_in_t = full["w_in"].reshape(D_IN, D_MODEL)
            return (w_in_t, _pad_rows(w_in_t[D_MAIN:], LANES), _cols_from_slots(full["conv_dw_w"]),
                    _cols_from_slots(full["ssd_conv_w"]))
        if stage == "out":
            return full["w_out"].reshape(D_MODEL, D_MODEL)
        if stage == "up":
            return _cols_from_slots(full["w_gate"]), _cols_from_slots(full["w_up"])
        return full["w_down"].reshape(D_FF, D_MODEL)

    reduce_groups = {"ffn": ("w_down", "w_gate", "w_up"), "out": ("w_out",), "in": ("w_in",)}
    in_flight = {}

    swapping = {}

    def send_to_owners(stage, f32s, kept, axes):
        sums = [_add_core_halves(f32, l, where, ax) for f32, l, ax in zip(f32s, kept, axes)]
        ps = [s[1] for s in sums]
        ssem, rsem, ps, recv, started = _split_start(
            ps, [lax.empty((COPIES,) + p.shape[1:], p.dtype) for p in ps], [_owners_plan] * len(ps),
            name="owners_start_" + stage)
        in_flight[stage] = (ssem, rsem, ps, recv, [s[0] for s in sums])
        return started

    def on_grads(stage, gs):
        if stage == "ffn_sums":
            ssem, rsem, b16s, kept, f32s, axes = swapping.pop("ffn")
            _, kept = _split_wait(ssem, rsem, b16s, kept, [_halves_plan(ax) for ax in axes], gs,
                                  name="swap_other_halves_wait_ffn")
            return send_to_owners("ffn", f32s, kept, axes)
        names = reduce_groups[stage]
        axes = [HALF_AXIS[n] for n in names]
        slot = lambda g: g if g.ndim == 3 else g.reshape((N_CHIPS, g.shape[0] // N_CHIPS, g.shape[1]))
        f32s, b16s = [slot(f32) for f32, _ in gs], [slot(b16) for _, b16 in gs]
        if stage == "ffn":
            kept = [lax.empty(_halved_shape(b.shape, ax), b.dtype) for b, ax in zip(b16s, axes)]
            ssem, rsem, b16s, kept, started = _split_start(b16s, kept, [_halves_plan(ax) for ax in axes],
                                                           name="swap_other_halves_start_ffn")
            swapping["ffn"] = (ssem, rsem, b16s, kept, f32s, axes)
            return started
        return send_to_owners(stage, f32s, _swap_other_halves(b16s, axes, name="swap_other_halves_" + stage), axes)

    prm = {n: wts[n] for n in SMALL}
    loss, dx, grads = _local_grads(x.reshape(t, d), loss_target.reshape(t, d), prm, get_w, on_grads,
                                   nb=nb, seq=seq, dep=token)
    loss = lax.psum(loss[0, 0], MESH_AXES)

    def reduced(stage, after):
        ssem, rsem, ps, recv, own_sums = in_flight[stage]
        _, recv = _split_wait(ssem, rsem, ps, recv, [_owners_plan] * len(ps), after, name="owners_wait_" + stage)
        return {n: _add_chip_sums(f32_sum, r, where, HALF_AXIS[n])
                for n, f32_sum, r in zip(reduce_groups[stage], own_sums, recv)}

    def swap_start(names, name, dep=None):
        mine = [halves[n] for n in names]
        return _split_start(mine, [lax.empty(h.shape, h.dtype) for h in mine], [_sibling_plan] * len(mine), name=name,
                            dep=dep)

    halves = reduced("ffn", dx)
    ffn_swap = swap_start(reduce_groups["ffn"], "swap_reduced_start_ffn")
    halves.update(reduced("out", ffn_swap[4]))
    halves.update(reduced("in", ffn_swap[4]))
    mix_names = reduce_groups["out"] + reduce_groups["in"]
    mix_swap = swap_start(mix_names, "swap_reduced_start_mix", dep=ffn_swap[4])

    small_shapes = [grads[n].shape for n in SMALL]
    small_sum = _unpack(_all_reduce_small(_pack([grads[n] for n in SMALL])), small_shapes)
    small_grads = dict(zip(SMALL, small_sum))
    cwid, swid = D_CONV // N_CHIPS, D_XBC // N_CHIPS
    small_grads["conv_dw_w"] = lax.dynamic_slice(small_grads["conv_dw_w"], (0, chip * cwid), (CONV_K, cwid))
    small_grads["ssd_conv_w"] = lax.dynamic_slice(small_grads["ssd_conv_w"], (0, chip * swid), (SSD_CONV_K, swid))

    out_g, out_d, out_m, out_v = {}, {}, {}, {}
    shard_shapes = [wts[n].shape for n in SMALL]
    pd, pm, pv = _adamw(_pack([wts[n] for n in SMALL]), _pack([small_grads[n] for n in SMALL]),
                        _pack([ms[n] for n in SMALL]), _pack([vs[n] for n in SMALL]), name="adamw_small")
    for n, dd, mm, vv in zip(SMALL, _unpack(pd, shard_shapes), _unpack(pm, shard_shapes), _unpack(pv, shard_shapes)):
        out_g[n], out_d[n], out_m[n], out_v[n] = small_grads[n], dd, mm, vv

    def big_adamw(names, swap, after, name):
        mine, other = _split_wait(*swap[:4], [_sibling_plan] * len(names), after, name=name)
        for n, gm, go in zip(names, mine, other):
            out_g[n], out_d[n], out_m[n], out_v[n] = _adamw_halves(wts[n], gm, go, ms[n], vs[n], where, HALF_AXIS[n],
                                                                     name="adamw_" + n)

    big_adamw(reduce_groups["ffn"], ffn_swap, pd, "swap_reduced_wait_ffn")
    big_adamw(mix_names, mix_swap, out_d[reduce_groups["ffn"][-1]], "swap_reduced_wait_mix")

    back = lambda n, a: jnp.swapaxes(a[None], 1, 2) if n == "w_in" else a.reshape(args[n].shape)
    outs = [back(n, o[n]) for o in (out_g, out_d, out_m, out_v) for n in WEIGHTS]
    return (loss, dx.reshape(nb, seq, d), *outs)
```

```python
import functools

import jax
import jax.numpy as jnp
from jax import lax
from jax.experimental import pallas as pl
from jax.experimental.pallas import tpu as pltpu

F32 = jnp.float32
BF16 = jnp.bfloat16
EPS = 1e-6

D_MODEL = 2048
D_CONV = 1024
D_SSD = 1024
D_XBC = 2048
HEADS = 16
HEAD_DIM = 64
GROUPS = 4
STATE = 128
CONV_K = 31
SSD_CONV_K = 4
D_FF = 5632
D_MAIN = 2 * D_CONV + D_SSD + D_XBC
D_IN = D_MAIN + HEADS
N_CHIPS = 4
LANES = 128
CHUNK = 128
PAIRS = HEADS // 2

ADAM_LR = 0.001
ADAM_B1 = 0.9
ADAM_B2 = 0.999
ADAM_EPS = 1e-08
ADAM_WD = 0.01
ADAM_STEP = 10

MESH_AXES = ("x", "y", "c")
VMEM_LIMIT = 56 * 1024 * 1024


def _sig(v):
    return 1.0 / (1.0 + jnp.exp(-v))


def _cparams(sem, vmem=VMEM_LIMIT):
    return pltpu.CompilerParams(dimension_semantics=sem, vmem_limit_bytes=vmem)


_DIMS = {"nn": ((1,), (0,)), "nt": ((1,), (1,)), "tn": ((0,), (0,))}


def _matmul(pairs, *, mode, out_dtype, tm, tn, tk, name, slot_out=False, extent=None, out_rows=None, dep=None,
            also_bf16=False):
    a0, b0 = pairs[0]
    if mode == "nn":
        (m, k), n = a0.shape, b0.shape[1]
    elif mode == "nt":
        (m, k), n = a0.shape, b0.shape[0]
    else:
        (k, m), n = a0.shape, b0.shape[1]
    if extent is not None:
        m, n, k = extent
    tm, tn, tk = min(tm, m), min(tn, n), min(tk, k)
    assert m % tm == 0 and n % tn == 0 and k % tk == 0, (name, m, n, k, tm, tn, tk)
    nk = k // tk
    npairs = len(pairs)
    deps = [] if dep is None else [dep]
    dims = (_DIMS[mode], ((), ()))

    use_scratch = nk > 1 and out_dtype != F32

    def body(*refs):
        ins, o_ref = refs[: 2 * npairs], refs[2 * npairs + len(deps)]
        dot = lambda p: lax.dot_general(ins[2 * p][...], ins[2 * p + 1][...], dims, preferred_element_type=F32)
        if nk == 1:
            part = dot(0)
            for p in range(1, npairs):
                part = part + dot(p)
            o_ref[...] = part.astype(out_dtype)
            if also_bf16:
                refs[2 * npairs + len(deps) + 1][...] = part.astype(BF16)
            return
        acc = refs[-1] if use_scratch else o_ref
        kk = pl.program_id(2)

        @pl.when(kk == 0)
        def _():
            acc[...] = jnp.zeros_like(acc)

        for p in range(npairs):
            acc[...] += dot(p)

        if use_scratch:
            @pl.when(kk == nk - 1)
            def _():
                o_ref[...] = acc[...].astype(out_dtype)

        if also_bf16:
            @pl.when(kk == nk - 1)
            def _():
                refs[2 * npairs + len(deps) + 1][...] = acc[...].astype(BF16)

    if mode == "nn":
        a_spec = pl.BlockSpec((tm, tk), lambda i, j, kk: (i, kk))
        b_spec = pl.BlockSpec((tk, tn), lambda i, j, kk: (kk, j))
    elif mode == "nt":
        a_spec = pl.BlockSpec((tm, tk), lambda i, j, kk: (i, kk))
        b_spec = pl.BlockSpec((tn, tk), lambda i, j, kk: (j, kk))
    else:
        a_spec = pl.BlockSpec((tk, tm), lambda i, j, kk: (kk, i))
        b_spec = pl.BlockSpec((tk, tn), lambda i, j, kk: (kk, j))
    if slot_out:
        out_shape = jax.ShapeDtypeStruct((n // tn, m, tn), out_dtype)
        out_spec = pl.BlockSpec((None, tm, tn), lambda i, j, kk: (j, i, 0))
    else:
        out_shape = jax.ShapeDtypeStruct((m if out_rows is None else out_rows, n), out_dtype)
        out_spec = pl.BlockSpec((tm, tn), lambda i, j, kk: (i, j))
    flat = [t for ab in pairs for t in ab]
    if also_bf16:
        out_spec = [out_spec, out_spec]
        out_shape = [out_shape, jax.ShapeDtypeStruct(out_shape.shape, BF16)]
    return pl.pallas_call(
        body,
        grid=(m // tm, n // tn, nk),
        in_specs=[a_spec, b_spec] * npairs + [pl.BlockSpec(memory_space=pl.ANY)] * len(deps),
        out_specs=out_spec,
        out_shape=out_shape,
        scratch_shapes=[pltpu.VMEM((tm, tn), F32)] if use_scratch else [],
        compiler_params=_cparams(("parallel", "parallel", "arbitrary")),
        name=name,
    )(*flat, *deps)


SUB_ROWS = 256


def _ffn_up(h2, wg, wu, *, tm, tn):
    t, k = h2.shape
    n = wg.shape[1]
    tm = min(tm, t)
    assert t % tm == 0 and n % tn == 0, (t, n, tm, tn)

    sub = min(SUB_ROWS, tm)

    def body(h_ref, wg_ref, wu_ref, g_ref, u_ref, a_ref, at_ref):
        for r in range(tm // sub):
            rows = pl.ds(r * sub, sub)
            hv = h_ref[rows, :]
            g = jnp.dot(hv, wg_ref[...], preferred_element_type=F32)
            u = jnp.dot(hv, wu_ref[...], preferred_element_type=F32)
            g_ref[rows, :] = g.astype(BF16)
            u_ref[rows, :] = u.astype(BF16)
            act = (g * _sig(g) * u).astype(BF16)
            a_ref[rows, :] = act
            at_ref[:, rows] = act.T

    o = jax.ShapeDtypeStruct((t, n), BF16)
    ospec = pl.BlockSpec((tm, tn), lambda i, j: (i, j))
    return pl.pallas_call(
        body,
        grid=(t // tm, n // tn),
        in_specs=[pl.BlockSpec((tm, k), lambda i, j: (i, 0)), pl.BlockSpec((k, tn), lambda i, j: (0, j)),
                  pl.BlockSpec((k, tn), lambda i, j: (0, j))],
        out_specs=[ospec, ospec, ospec, pl.BlockSpec((tn, tm), lambda i, j: (j, i))],
        out_shape=[o, o, o, jax.ShapeDtypeStruct((n, t), BF16)],
        compiler_params=_cparams(("parallel", "parallel")),
        name="ffn_up",
    )(h2, wg, wu)


def _ffn_bwd_act(df, wd, gt, up, *, tm, tn):
    t, k = df.shape
    n = wd.shape[0]
    tm = min(tm, t)
    assert t % tm == 0 and n % tn == 0, (t, n, tm, tn)

    sub = min(SUB_ROWS, tm)

    def body(df_ref, wd_ref, g_ref, u_ref, dg_ref, du_ref):
        for r in range(tm // sub):
            rows = pl.ds(r * sub, sub)
            da = lax.dot_general(df_ref[rows, :], wd_ref[...], (_DIMS["nt"], ((), ())), preferred_element_type=F32)
            g = g_ref[rows, :].astype(F32)
            u = u_ref[rows, :].astype(F32)
            s = _sig(g)
            dg_ref[rows, :] = (da * u * s * (1.0 + g * (1.0 - s))).astype(BF16)
            du_ref[rows, :] = (da * g * s).astype(BF16)

    o = jax.ShapeDtypeStruct((t, n), BF16)
    blk = pl.BlockSpec((tm, tn), lambda i, j: (i, j))
    return pl.pallas_call(
        body,
        grid=(t // tm, n // tn),
        in_specs=[pl.BlockSpec((tm, k), lambda i, j: (i, 0)), pl.BlockSpec((tn, k), lambda i, j: (j, 0)), blk, blk],
        out_specs=[blk, blk],
        out_shape=[o, o],
        compiler_params=_cparams(("parallel", "parallel")),
        name="ffn_bwd_act",
    )(df, wd, gt, up)


def _dw_dt_rows(dw_in_t, ddtr_b, h, *, tk=1024):
    t, d = h.shape
    tk = min(tk, t)
    nk = t // tk

    def body(buf_ref, d_ref, h_ref, o_ref, acc):
        kk = pl.program_id(0)

        @pl.when(kk == 0)
        def _():
            acc[...] = jnp.zeros_like(acc)

        acc[...] += lax.dot_general(d_ref[...], h_ref[...], (_DIMS["tn"], ((), ())), preferred_element_type=F32)

        @pl.when(kk == nk - 1)
        def _():
            o_ref[...] = acc[0:HEADS, :]

    return pl.pallas_call(
        body,
        grid=(nk,),
        in_specs=[DEP_SPEC, pl.BlockSpec((tk, LANES), lambda kk: (kk, 0)), pl.BlockSpec((tk, d), lambda kk: (kk, 0))],
        out_specs=pl.BlockSpec((HEADS, d), lambda kk: (D_MAIN // HEADS, 0)),
        out_shape=jax.ShapeDtypeStruct(dw_in_t.shape, F32),
        input_output_aliases={0: 0},
        scratch_shapes=[pltpu.VMEM((LANES, d), F32)],
        compiler_params=_cparams(("arbitrary",)),
        name="mm_dw_dt",
    )(dw_in_t, ddtr_b, h)


ROW_TILE = 256
FWD_ROW_TILE = 512


DEP_SPEC = pl.BlockSpec(memory_space=pl.ANY)


def _rms_fwd(xv, g, *, dep=None, out_dtype, name):
    t, d = xv.shape
    deps = [] if dep is None else [dep]

    def body(*refs):
        x_ref, g_ref = refs[0], refs[1]
        o_ref = refs[-1]
        v = x_ref[...]
        r = lax.rsqrt(jnp.mean(v * v, axis=-1, keepdims=True) + EPS)
        o_ref[...] = (v * r * g_ref[...]).astype(out_dtype)

    row = pl.BlockSpec((FWD_ROW_TILE, d), lambda i: (i, 0))
    vec = pl.BlockSpec((1, d), lambda i: (0, 0))
    return pl.pallas_call(
        body,
        grid=(t // FWD_ROW_TILE,),
        in_specs=[row, vec] + [DEP_SPEC] * len(deps),
        out_specs=row,
        out_shape=jax.ShapeDtypeStruct((t, d), out_dtype),
        compiler_params=_cparams(("parallel",)),
        name=name,
    )(*([xv, g] + deps))


def _rms_bwd_rows(dy, v, gv):
    r = lax.rsqrt(jnp.mean(v * v, axis=-1, keepdims=True) + EPS)
    xh = v * r
    gdy = dy * gv
    dx = r * (gdy - xh * jnp.mean(gdy * xh, axis=-1, keepdims=True))
    return dx, jnp.sum(dy * xh, axis=0, keepdims=True)


FUSED_ROWS = 512


def _matmul_rows_tail(a, w, tail, *, tk, row_ins, vec_ins, row_outs, vec_outs, first=None, dep=None, name):
    t, kdim = a.shape
    d = w.shape[1]
    tm, tk = min(FUSED_ROWS, t), min(tk, kdim)
    nk, nb = kdim // tk, t // tm
    assert t % tm == 0 and kdim % tk == 0
    n_ri, n_vi, n_ro, n_vo = len(row_ins), len(vec_ins), len(row_outs), len(vec_outs)
    n_first = 0 if first is None else 2
    deps = [] if dep is None else [dep]

    def body(*refs):
        a_ref, w_ref = refs[0], refs[1]
        first_refs = refs[2:2 + n_first]
        p = 2 + n_first
        ri = refs[p:p + n_ri]
        vi = refs[p + n_ri:p + n_ri + n_vi]
        p += n_ri + n_vi + len(deps)
        ro = refs[p:p + n_ro]
        vo = refs[p + n_ro:p + n_ro + n_vo]
        acc = refs[-1]
        i, kk = pl.program_id(0), pl.program_id(1)

        @pl.when(jnp.logical_and(i == 0, kk == 0))
        def _():
            for ref in vo:
                ref[...] = jnp.zeros_like(ref)

        @pl.when(kk == 0)
        def _():
            if first is None:
                acc[...] = jnp.zeros_like(acc)
            else:
                acc[...] = jnp.dot(first_refs[0][...], first_refs[1][...], preferred_element_type=F32)

        acc[...] += jnp.dot(a_ref[...], w_ref[...], preferred_element_type=F32)

        @pl.when(kk == nk - 1)
        def _():
            outs, parts = tail(acc[...], [r[...] for r in ri], [v[...] for v in vi])
            for ref, val in zip(ro, outs):
                ref[...] = val.astype(ref.dtype)
            for ref, part in zip(vo, parts):
                ref[...] += part

    row = pl.BlockSpec((tm, d), lambda i, kk: (i, 0))
    const = lambda shape: pl.BlockSpec(shape, lambda i, kk: (0,) * len(shape))
    in_specs = [pl.BlockSpec((tm, tk), lambda i, kk: (i, kk)), pl.BlockSpec((tk, d), lambda i, kk: (kk, 0))]
    if first is not None:
        in_specs += [pl.BlockSpec((tm, first[0].shape[1]), lambda i, kk: (i, 0)), const(first[1].shape)]
    in_specs += [row] * n_ri + [const(v.shape) for v in vec_ins] + [pl.BlockSpec(memory_space=pl.ANY)] * len(deps)
    return pl.pallas_call(
        body,
        grid=(nb, nk),
        in_specs=in_specs,
        out_specs=[row] * n_ro + [const(sh) for sh in vec_outs],
        out_shape=[jax.ShapeDtypeStruct((t, d), dt) for dt in row_outs]
        + [jax.ShapeDtypeStruct(sh, F32) for sh in vec_outs],
        scratch_shapes=[pltpu.VMEM((tm, d), F32)],
        compiler_params=_cparams(("arbitrary", "arbitrary")),
        name=name,
    )(a, w, *([] if first is None else list(first)), *row_ins, *vec_ins, *deps)


def _down_loss(act, w_down, x1, tgt, g, *, tk):
    d = w_down.shape[1]

    def tail(v, rows, vecs):
        x1v, tv = rows
        gv, = vecs
        fh = v * lax.rsqrt(jnp.mean(v * v, axis=-1, keepdims=True) + EPS)
        e = x1v + fh * gv - tv
        dx2 = e * (1.0 / d)
        df, dg = _rms_bwd_rows(dx2, v, gv)
        loss = 0.5 * jnp.sum(jnp.mean(e * e, axis=-1, keepdims=True), axis=0, keepdims=True)
        return (dx2, df), (loss, dg)

    return _matmul_rows_tail(act, w_down, tail, tk=tk, row_ins=[x1, tgt], vec_ins=[g], row_outs=[F32, BF16],
                             vec_outs=[(1, 1), (1, d)], name="mm_down_loss")


def _dh_dx(dproj, w_in_t, ddtr_b, w_dt_t, xv, dx1, g, *, tk, dep):
    def tail(v, rows, vecs):
        xr, dx1r = rows
        dx, dg = _rms_bwd_rows(v, xr, vecs[0])
        return (dx + dx1r,), (dg,)

    return _matmul_rows_tail(dproj, w_in_t, tail, tk=tk, row_ins=[xv, dx1], vec_ins=[g], row_outs=[F32],
                             vec_outs=[(1, xv.shape[1])], first=(ddtr_b, w_dt_t), dep=dep, name="mm_dh_dx")


def _rms_post_pre(mix, xv, g_post, g_pre):
    t, d = mix.shape

    def body(m_ref, x_ref, gp_ref, gf_ref, x1_ref, h2_ref, h2t_ref):
        v = m_ref[...]
        x1 = x_ref[...] + v * lax.rsqrt(jnp.mean(v * v, axis=-1, keepdims=True) + EPS) * gp_ref[...]
        x1_ref[...] = x1
        h2 = (x1 * lax.rsqrt(jnp.mean(x1 * x1, axis=-1, keepdims=True) + EPS) * gf_ref[...]).astype(BF16)
        h2_ref[...] = h2
        h2t_ref[...] = h2.T

    row = pl.BlockSpec((FWD_ROW_TILE, d), lambda i: (i, 0))
    vec = pl.BlockSpec((1, d), lambda i: (0, 0))
    return pl.pallas_call(
        body,
        grid=(t // FWD_ROW_TILE,),
        in_specs=[row, row, vec, vec],
        out_specs=[row, row, pl.BlockSpec((d, FWD_ROW_TILE), lambda i: (0, i))],
        out_shape=[jax.ShapeDtypeStruct((t, d), F32), jax.ShapeDtypeStruct((t, d), BF16),
                   jax.ShapeDtypeStruct((d, t), BF16)],
        compiler_params=_cparams(("parallel",)),
        name="rms_mix_post_ffn_pre",
    )(mix, xv, g_post, g_pre)


def _rms_bwd_pre_post(dh2, x1, g_pre, dx2, mix, g_post, *, dep=None):
    t, d = x1.shape

    deps = [] if dep is None else [dep]

    def body(dh_ref, x1_ref, gf_ref, dx2_ref, m_ref, gp_ref, *rest):
        dx1_ref, dmix_ref, dgf_ref, dgp_ref = rest[len(deps):]

        @pl.when(pl.program_id(0) == 0)
        def _():
            dgf_ref[...] = jnp.zeros_like(dgf_ref)
            dgp_ref[...] = jnp.zeros_like(dgp_ref)

        dx, dgf = _rms_bwd_rows(dh_ref[...], x1_ref[...], gf_ref[...])
        dx1 = dx + dx2_ref[...]
        dx1_ref[...] = dx1
        dmix, dgp = _rms_bwd_rows(dx1, m_ref[...], gp_ref[...])
        dmix_ref[...] = dmix.astype(BF16)
        dgf_ref[...] += dgf
        dgp_ref[...] += dgp

    row = pl.BlockSpec((ROW_TILE, d), lambda i: (i, 0))
    vec = pl.BlockSpec((1, d), lambda i: (0, 0))
    return pl.pallas_call(
        body,
        grid=(t // ROW_TILE,),
        in_specs=[row, row, vec, row, row, vec] + [DEP_SPEC] * len(deps),
        out_specs=[row, row, vec, vec],
        out_shape=[jax.ShapeDtypeStruct((t, d), F32), jax.ShapeDtypeStruct((t, d), BF16),
                   jax.ShapeDtypeStruct((1, d), F32), jax.ShapeDtypeStruct((1, d), F32)],
        compiler_params=_cparams(("arbitrary",)),
        name="rms_ffn_pre_mix_post_bwd",
    )(dh2, x1, g_pre, dx2, mix, g_post, *deps)


CONV_ROWS = 512
TAP_ROWS = 64
HALO31 = 32
HALO4 = 8


def _sum8(v):
    return jnp.sum(v.reshape(v.shape[0] // 8, 8, v.shape[1]), axis=0)


SUBLANES = 8
PHASE_SPAN = (CONV_K - 1) // SUBLANES * SUBLANES


def _phase_scratch(ts):
    return pltpu.VMEM((SUBLANES, ts + PHASE_SPAN, LANES), F32)


def _phase_copies(ph, buf, ln, base, ts):
    for s in range(SUBLANES):
        n = ts + (CONV_K - 1 - s) // SUBLANES * SUBLANES
        ph[s, 0:n, :] = buf[pl.ds(base + s, n), ln]


def _tap_rows(ph, off, r0):
    s = off % SUBLANES
    return ph[s, pl.ds(r0 + off - s, TAP_ROWS), :]


def _conv_branch_fwd(proj, cw, cb, lg, lb, *, nb, seq):
    ts, c, halo = CONV_ROWS, D_CONV, HALO31
    ns = seq // ts
    base = halo - CONV_K + 1

    def body(ca_ref, cg_ref, w_ref, b_ref, lg_ref, lb_ref, u1_ref, u_ref, ubuf, uph):
        i = pl.program_id(1)

        @pl.when(i == 0)
        def _():
            ubuf[0:halo, :] = jnp.zeros((halo, c), F32)

        @pl.when(i > 0)
        def _():
            ubuf[0:halo, :] = ubuf[ts:ts + halo, :]

        ubuf[halo:halo + ts, :] = ca_ref[...] * _sig(cg_ref[...])

        def lane_tile(j, carry):
            ln = pl.ds(pl.multiple_of(j * LANES, LANES), LANES)
            _phase_copies(uph, ubuf, ln, base, ts)
            for r in range(ts // TAP_ROWS):
                acc = jnp.broadcast_to(b_ref[:, ln], (TAP_ROWS, LANES))
                for k in range(CONV_K):
                    acc = acc + w_ref[pl.ds(k, 1), ln] * _tap_rows(uph, k, r * TAP_ROWS)
                u1_ref[pl.ds(r * TAP_ROWS, TAP_ROWS), ln] = acc
            return carry

        lax.fori_loop(0, c // LANES, lane_tile, 0)
        v = u1_ref[...]
        mu = jnp.mean(v, axis=-1, keepdims=True)
        dv = v - mu
        xh = dv * lax.rsqrt(jnp.mean(dv * dv, axis=-1, keepdims=True) + EPS)
        u2 = xh * lg_ref[...] + lb_ref[...]
        u_ref[...] = (u2 * _sig(u2)).astype(BF16)

    t = nb * seq
    row = lambda col: pl.BlockSpec((ts, c), lambda b, i: (b * ns + i, col))
    vec = pl.BlockSpec((1, c), lambda b, i: (0, 0))
    return pl.pallas_call(
        body,
        grid=(nb, ns),
        in_specs=[row(0), row(1), pl.BlockSpec((32, c), lambda b, i: (0, 0)), vec, vec, vec],
        out_specs=[row(0), row(0)],
        out_shape=[jax.ShapeDtypeStruct((t, c), F32), jax.ShapeDtypeStruct((t, c + D_SSD), BF16)],
        scratch_shapes=[pltpu.VMEM((halo + ts, c), F32), _phase_scratch(ts)],
        compiler_params=_cparams(("parallel", "arbitrary")),
        name="conv_branch_fwd",
    )(proj, proj, cw, cb, lg, lb)


def _conv_branch_bwd(duy, u1, proj, cw, lg, lb, *, nb, seq):
    ts, c, halo = CONV_ROWS, D_CONV, HALO31
    ns = seq // ts
    base = halo - CONV_K + 1
    hb = ts // halo

    def body(du_ref, u1_ref, ca_ref, cg_ref, cah_ref, cgh_ref, w_ref, lg_ref, lb_ref,
             dcacg_ref, dw_ref, db_ref, dlg_ref, dlb_ref,
             ubuf, dbuf, du0buf, dwacc, dbacc, dlgacc, dlbacc, uph, dph):
        b, i = pl.program_id(0), pl.program_id(1)
        rc = ns - 1 - i

        @pl.when(jnp.logical_and(b == 0, i == 0))
        def _():
            dwacc[...] = jnp.zeros_like(dwacc)
            dbacc[...] = jnp.zeros_like(dbacc)
            dlgacc[...] = jnp.zeros_like(dlgacc)
            dlbacc[...] = jnp.zeros_like(dlbacc)

        @pl.when(i == 0)
        def _():
            dbuf[ts:ts + halo, :] = jnp.zeros((halo, c), F32)

        @pl.when(i > 0)
        def _():
            dbuf[ts:ts + halo, :] = dbuf[0:halo, :]

        v = u1_ref[...]
        mu = jnp.mean(v, axis=-1, keepdims=True)
        dv = v - mu
        rstd = lax.rsqrt(jnp.mean(dv * dv, axis=-1, keepdims=True) + EPS)
        xh = dv * rstd
        lgv = lg_ref[...]
        u2 = xh * lgv + lb_ref[...]
        s2 = _sig(u2)
        du2 = du_ref[...] * (s2 * (1.0 + u2 * (1.0 - s2)))
        dlgacc[...] += jnp.sum(du2 * xh, axis=0, keepdims=True)
        dlbacc[...] += jnp.sum(du2, axis=0, keepdims=True)
        gd = du2 * lgv
        du1 = rstd * (gd - jnp.mean(gd, axis=-1, keepdims=True) - xh * jnp.mean(gd * xh, axis=-1, keepdims=True))
        dbacc[...] += jnp.sum(du1, axis=0, keepdims=True)
        dbuf[0:ts, :] = du1

        @pl.when(rc == 0)
        def _():
            ubuf[0:halo, :] = jnp.zeros((halo, c), F32)

        @pl.when(rc > 0)
        def _():
            ubuf[0:halo, :] = cah_ref[...] * _sig(cgh_ref[...])

        cav = ca_ref[...]
        sg = _sig(cg_ref[...])
        ubuf[halo:halo + ts, :] = cav * sg

        def lane_tile(j, carry):
            ln = pl.ds(pl.multiple_of(j * LANES, LANES), LANES)
            _phase_copies(uph, ubuf, ln, base, ts)
            _phase_copies(dph, dbuf, ln, 0, ts)
            for r in range(ts // TAP_ROWS):
                r0 = r * TAP_ROWS
                d1 = dbuf[pl.ds(r0, TAP_ROWS), ln]
                acc = jnp.zeros((TAP_ROWS, LANES), F32)
                for k in range(CONV_K):
                    acc = acc + w_ref[pl.ds(k, 1), ln] * _tap_rows(dph, CONV_K - 1 - k, r0)
                    dwacc[pl.ds(k * 8, 8), ln] += _sum8(d1 * _tap_rows(uph, k, r0))
                du0buf[pl.ds(r0, TAP_ROWS), ln] = acc
            return carry

        lax.fori_loop(0, c // LANES, lane_tile, 0)
        du0 = du0buf[...]
        dcacg_ref[:, 0:c] = (du0 * sg).astype(BF16)
        dcacg_ref[:, c:2 * c] = (du0 * cav * sg * (1.0 - sg)).astype(BF16)

        @pl.when(jnp.logical_and(b == nb - 1, i == ns - 1))
        def _():
            for k in range(CONV_K):
                dw_ref[pl.ds(k, 1), :] = jnp.sum(dwacc[pl.ds(k * 8, 8), :], axis=0, keepdims=True)
            dw_ref[pl.ds(CONV_K, 1), :] = jnp.zeros((1, c), F32)
            db_ref[...] = dbacc[...]
            dlg_ref[...] = dlgacc[...]
            dlb_ref[...] = dlbacc[...]

    t = nb * seq
    rowblk = lambda b, i: b * ns + (ns - 1 - i)
    row = lambda col: pl.BlockSpec((ts, c), lambda b, i: (rowblk(b, i), col))
    hrow = lambda col: pl.BlockSpec((halo, c), lambda b, i: (jnp.maximum(rowblk(b, i) * hb - 1, 0), col))
    vec = pl.BlockSpec((1, c), lambda b, i: (0, 0))
    wspec = pl.BlockSpec((32, c), lambda b, i: (0, 0))
    return pl.pallas_call(
        body,
        grid=(nb, ns),
        in_specs=[row(0), row(0), row(0), row(1), hrow(0), hrow(1), wspec, vec, vec],
        out_specs=[pl.BlockSpec((ts, 2 * c), lambda b, i: (rowblk(b, i), 0)), wspec, vec, vec, vec],
        out_shape=[jax.ShapeDtypeStruct((t, D_MAIN), BF16), jax.ShapeDtypeStruct((32, c), F32),
                   jax.ShapeDtypeStruct((1, c), F32), jax.ShapeDtypeStruct((1, c), F32), jax.ShapeDtypeStruct((1, c), F32)],
        scratch_shapes=[pltpu.VMEM((halo + ts, c), F32), pltpu.VMEM((ts + halo, c), F32), pltpu.VMEM((ts, c), F32),
                        pltpu.VMEM((CONV_K * 8, c), F32), pltpu.VMEM((1, c), F32), pltpu.VMEM((1, c), F32),
                        pltpu.VMEM((1, c), F32), _phase_scratch(ts), _phase_scratch(ts)],
        compiler_params=_cparams(("arbitrary", "arbitrary")),
        name="conv_branch_bwd",
    )(duy, u1, proj, proj, proj, proj, cw, lg, lb)


XBC_COL0 = (2 * D_CONV + D_SSD) // 1024


def _ssd_pre_fwd(proj, sw, sb, *, nb, seq):
    ts, c, halo = CONV_ROWS, 1024, HALO4
    ns = seq // ts
    base = halo - SSD_CONV_K + 1

    def body(x_ref, w_ref, b_ref, o_ref, xbuf):
        i = pl.program_id(2)

        @pl.when(i == 0)
        def _():
            xbuf[0:halo, :] = jnp.zeros((halo, c), F32)

        @pl.when(i > 0)
        def _():
            xbuf[0:halo, :] = xbuf[ts:ts + halo, :]

        xbuf[halo:halo + ts, :] = x_ref[...]

        def lane_tile(j, carry):
            ln = pl.ds(pl.multiple_of(j * LANES, LANES), LANES)
            for r in range(ts // TAP_ROWS):
                acc = jnp.broadcast_to(b_ref[:, ln], (TAP_ROWS, LANES))
                for k in range(SSD_CONV_K):
                    acc = acc + w_ref[pl.ds(k, 1), ln] * xbuf[pl.ds(r * TAP_ROWS + base + k, TAP_ROWS), ln]
                o_ref[pl.ds(r * TAP_ROWS, TAP_ROWS), ln] = acc * _sig(acc)
            return carry

        lax.fori_loop(0, c // LANES, lane_tile, 0)

    t = nb * seq
    return pl.pallas_call(
        body,
        grid=(2, nb, ns),
        in_specs=[pl.BlockSpec((ts, c), lambda j, b, i: (b * ns + i, XBC_COL0 + j)),
                  pl.BlockSpec((8, c), lambda j, b, i: (0, j)), pl.BlockSpec((1, c), lambda j, b, i: (0, j))],
        out_specs=pl.BlockSpec((ts, c), lambda j, b, i: (b * ns + i, j)),
        out_shape=jax.ShapeDtypeStruct((t, D_XBC), F32),
        scratch_shapes=[pltpu.VMEM((halo + ts, c), F32)],
        compiler_params=_cparams(("parallel", "parallel", "arbitrary")),
        name="ssd_pre_fwd",
    )(proj, sw, sb)


def _ssd_pre_bwd(dproj, dxs, proj, sw, sb, *, nb, seq):
    ts, c, halo = CONV_ROWS, 1024, HALO4
    ns = seq // ts
    base = halo - SSD_CONV_K + 1
    hb = ts // halo

    def body(dproj_ref, d_ref, x_ref, xh_ref, w_ref, b_ref, dx_ref, dw_ref, db_ref, xbuf, dbuf, dwacc, dbacc):
        b, i = pl.program_id(1), pl.program_id(2)
        rc = ns - 1 - i

        @pl.when(jnp.logical_and(b == 0, i == 0))
        def _():
            dwacc[...] = jnp.zeros_like(dwacc)
            dbacc[...] = jnp.zeros_like(dbacc)

        @pl.when(i == 0)
        def _():
            dbuf[ts:ts + halo, :] = jnp.zeros((halo, c), F32)

        @pl.when(i > 0)
        def _():
            dbuf[ts:ts + halo, :] = dbuf[0:halo, :]

        @pl.when(rc == 0)
        def _():
            xbuf[0:halo, :] = jnp.zeros((halo, c), F32)

        @pl.when(rc > 0)
        def _():
            xbuf[0:halo, :] = xh_ref[...]

        xbuf[halo:halo + ts, :] = x_ref[...]

        def pre_tile(j, carry):
            ln = pl.ds(pl.multiple_of(j * LANES, LANES), LANES)
            for r in range(ts // TAP_ROWS):
                r0 = r * TAP_ROWS
                acc = jnp.broadcast_to(b_ref[:, ln], (TAP_ROWS, LANES))
                for k in range(SSD_CONV_K):
                    acc = acc + w_ref[pl.ds(k, 1), ln] * xbuf[pl.ds(r0 + base + k, TAP_ROWS), ln]
                s = _sig(acc)
                dc = d_ref[pl.ds(r0, TAP_ROWS), ln] * (s * (1.0 + acc * (1.0 - s)))
                dbuf[pl.ds(r0, TAP_ROWS), ln] = dc
                dbacc[:, ln] += _sum8(dc)
            return carry

        lax.fori_loop(0, c // LANES, pre_tile, 0)

        def lane_tile(j, carry):
            ln = pl.ds(pl.multiple_of(j * LANES, LANES), LANES)
            for r in range(ts // TAP_ROWS):
                r0 = r * TAP_ROWS
                d1 = dbuf[pl.ds(r0, TAP_ROWS), ln]
                acc = jnp.zeros((TAP_ROWS, LANES), F32)
                for k in range(SSD_CONV_K):
                    acc = acc + w_ref[pl.ds(k, 1), ln] * dbuf[pl.ds(r0 + SSD_CONV_K - 1 - k, TAP_ROWS), ln]
                    dwacc[pl.ds(k * 8, 8), ln] += _sum8(d1 * xbuf[pl.ds(r0 + base + k, TAP_ROWS), ln])
                dx_ref[pl.ds(r0, TAP_ROWS), ln] = acc.astype(BF16)
            return carry

        lax.fori_loop(0, c // LANES, lane_tile, 0)

        @pl.when(jnp.logical_and(b == nb - 1, i == ns - 1))
        def _():
            for k in range(SSD_CONV_K):
                dw_ref[pl.ds(k, 1), :] = jnp.sum(dwacc[pl.ds(k * 8, 8), :], axis=0, keepdims=True)
            dw_ref[pl.ds(SSD_CONV_K, 8 - SSD_CONV_K), :] = jnp.zeros((8 - SSD_CONV_K, c), F32)
            db_ref[...] = jnp.sum(dbacc[...], axis=0, keepdims=True)

    t = nb * seq
    rowblk = lambda b, i: b * ns + (ns - 1 - i)
    return pl.pallas_call(
        body,
        grid=(2, nb, ns),
        in_specs=[DEP_SPEC, pl.BlockSpec((ts, c), lambda j, b, i: (rowblk(b, i), j)),
                  pl.BlockSpec((ts, c), lambda j, b, i: (rowblk(b, i), XBC_COL0 + j)),
                  pl.BlockSpec((halo, c), lambda j, b, i: (jnp.maximum(rowblk(b, i) * hb - 1, 0), XBC_COL0 + j)),
                  pl.BlockSpec((8, c), lambda j, b, i: (0, j)), pl.BlockSpec((1, c), lambda j, b, i: (0, j))],
        out_specs=[pl.BlockSpec((ts, c), lambda j, b, i: (rowblk(b, i), XBC_COL0 + j)),
                   pl.BlockSpec((8, c), lambda j, b, i: (0, j)), pl.BlockSpec((1, c), lambda j, b, i: (0, j))],
        out_shape=[jax.ShapeDtypeStruct(dproj.shape, BF16), jax.ShapeDtypeStruct((8, D_XBC), F32),
                   jax.ShapeDtypeStruct((1, D_XBC), F32)],
        input_output_aliases={0: 0},
        scratch_shapes=[pltpu.VMEM((halo + ts, c), F32), pltpu.VMEM((ts + halo, c), F32),
                        pltpu.VMEM((SSD_CONV_K * 8, c), F32), pltpu.VMEM((8, c), F32)],
        compiler_params=_cparams(("arbitrary", "arbitrary", "arbitrary")),
        name="ssd_pre_bwd",
    )(dproj, dxs, proj, proj, sw, sb)


Z_COL = (2 * D_CONV) // 1024
GROUP_W = D_SSD // GROUPS


def _softplus(v):
    return jnp.maximum(v, 0.0) + jnp.log(1.0 + jnp.exp(-jnp.abs(v)))


def _dot(a, b):
    return jnp.dot(a, b, preferred_element_type=F32)


def _dot_nt(a, b):
    return lax.dot_general(a, b, (_DIMS["nt"], ((), ())), preferred_element_type=F32)


def _dot_tn(a, b):
    return lax.dot_general(a, b, (_DIMS["tn"], ((), ())), preferred_element_type=F32)


def _bf16_terms(v):
    hi = v.astype(BF16)
    r1 = v - hi.astype(F32)
    mid = r1.astype(BF16)
    return hi, mid, (r1 - mid.astype(F32)).astype(BF16)


def _dot_exact_left(sel, v):
    hi, mid, lo = _bf16_terms(v)
    return _dot(sel, hi) + (_dot(sel, mid) + _dot(sel, lo))


def _dot_exact_right(v, sel):
    hi, mid, lo = _bf16_terms(v)
    return _dot(hi, sel) + (_dot(mid, sel) + _dot(lo, sel))


def _chunk_decays(dtr_ref, bias_ref, alog_ref):
    q = CHUNK
    ii = lax.broadcasted_iota(jnp.int32, (q, q), 0)
    jj = lax.broadcasted_iota(jnp.int32, (q, q), 1)
    tri = jj <= ii
    dt = _softplus(dtr_ref[...] + bias_ref[...])
    a_head = -jnp.exp(alog_ref[...])
    cs = _dot_exact_left(tri.astype(BF16), dt * a_head)
    return tri, dt, a_head, cs, cs.T


def _ssd_fwd(uy, xs_all, proj, dtr, dt_bias, a_log, d_lanes, norm_w, *, nb, seq):
    q = CHUNK
    nc = seq // q
    t = nb * seq

    def body(uy_ref, xs_ref, bm_ref, cm_ref, z_ref, dtr_ref, bias_ref, alog_ref, dl_ref, nw_ref,
             y_ref, ys_ref, st_ref, state):
        @pl.when(pl.program_id(1) == 0)
        def _():
            state[...] = jnp.zeros_like(state)

        tri, dt, _, cs, cst = _chunk_decays(dtr_ref, bias_ref, alog_ref)
        first = lax.broadcasted_iota(jnp.int32, (1, LANES), 1) < HEAD_DIM
        for g in range(GROUPS):
            gl = slice(g * STATE, (g + 1) * STATE)
            bb = bm_ref[:, gl].astype(BF16)
            cb = cm_ref[:, gl].astype(BF16)
            scores = _dot_nt(cb, bb)
            for p in range(2):
                pr = 2 * g + p
                h0 = 2 * pr
                sl = slice(pr * LANES, (pr + 1) * LANES)
                xv = xs_ref[:, sl]
                dtp = jnp.where(first, dt[:, h0:h0 + 1], dt[:, h0 + 1:h0 + 2])
                csp = jnp.where(first, cs[:, h0:h0 + 1], cs[:, h0 + 1:h0 + 2])
                xd = xv * dtp
                yv = None
                for hh, keep in ((h0, first), (h0 + 1, jnp.logical_not(first))):
                    decay = jnp.where(tri, jnp.exp(cs[:, hh:hh + 1] - cst[hh:hh + 1, :]), 0.0)
                    part = _dot((scores * decay).astype(BF16), jnp.where(keep, xd, 0.0).astype(BF16))
                    yv = part if yv is None else yv + part
                hp = state[pr]
                st_ref[0, pr] = hp
                yv = yv + jnp.exp(csp) * _dot(cb, hp.astype(BF16))
                last = csp[q - 1:q, :]
                state[pr] = jnp.exp(last) * hp + _dot_tn(bb, (xd * jnp.exp(last - csp)).astype(BF16))
                ys_ref[:, sl] = yv + dl_ref[:, sl] * xv
        zv = z_ref[...]
        gated = ys_ref[...] * (zv * _sig(zv))
        for g in range(GROUPS):
            gl = slice(g * GROUP_W, (g + 1) * GROUP_W)
            v = gated[:, gl]
            r = lax.rsqrt(jnp.mean(v * v, axis=-1, keepdims=True) + EPS)
            y_ref[:, gl] = (v * r * nw_ref[:, gl]).astype(BF16)

    blk = lambda w, col: pl.BlockSpec((q, w), lambda b, c: (b * nc + c, col))
    vec = lambda w: pl.BlockSpec((1, w), lambda b, c: (0, 0))
    return pl.pallas_call(
        body,
        grid=(nb, nc),
        in_specs=[DEP_SPEC, blk(D_SSD, 0), blk(GROUPS * STATE, 2), blk(GROUPS * STATE, 3), blk(D_SSD, Z_COL),
                  blk(LANES, 0), vec(LANES), vec(LANES), vec(D_SSD), vec(D_SSD)],
        out_specs=[blk(D_SSD, 1), blk(D_SSD, 0),
                   pl.BlockSpec((1, PAIRS, STATE, LANES), lambda b, c: (b * nc + c, 0, 0, 0))],
        out_shape=[jax.ShapeDtypeStruct(uy.shape, BF16), jax.ShapeDtypeStruct((t, D_SSD), F32),
                   jax.ShapeDtypeStruct((nb * nc, PAIRS, STATE, LANES), F32)],
        input_output_aliases={0: 0},
        scratch_shapes=[pltpu.VMEM((PAIRS, STATE, LANES), F32)],
        compiler_params=_cparams(("parallel", "arbitrary")),
        name="ssd_fwd",
    )(uy, xs_all, xs_all, xs_all, proj, dtr, dt_bias, a_log, d_lanes, norm_w)


def _ssd_bwd(dproj, duy, proj, ys, xs_all, dtr, states, dt_bias, a_log, d_lanes, norm_w, *, nb, seq):
    q = CHUNK
    nc = seq // q
    t = nb * seq
    head_of_lane = (jnp.arange(D_SSD)[:, None] // HEAD_DIM == jnp.arange(LANES)[None, :]).astype(BF16)

    def body(dproj_ref, dy_ref, z_ref, ys_ref, xs_ref, bm_ref, cm_ref, dtr_ref, st_ref, bias_ref, alog_ref, dl_ref,
             nw_ref, sel_ref, dz_ref, dx_ref, ddtr_ref, small_ref,
             dstate, dys_buf, dcsl, ddtl, dcst, dnw_acc, dd_acc, dbias_acc, da_acc):
        b, c = pl.program_id(0), pl.program_id(1)

        @pl.when(jnp.logical_and(b == 0, c == 0))
        def _():
            dnw_acc[...] = jnp.zeros_like(dnw_acc)
            dd_acc[...] = jnp.zeros_like(dd_acc)
            dbias_acc[...] = jnp.zeros_like(dbias_acc)
            da_acc[...] = jnp.zeros_like(da_acc)
            dcst[...] = jnp.zeros_like(dcst)

        @pl.when(c == 0)
        def _():
            dstate[...] = jnp.zeros_like(dstate)

        zv = z_ref[...]
        sz = _sig(zv)
        silz = zv * sz
        ysv = ys_ref[...]
        gated = ysv * silz
        dyv = dy_ref[...]
        nwv = nw_ref[...]
        for g in range(GROUPS):
            gl = slice(g * GROUP_W, (g + 1) * GROUP_W)
            v = gated[:, gl]
            r = lax.rsqrt(jnp.mean(v * v, axis=-1, keepdims=True) + EPS)
            yn = v * r
            dyn = dyv[:, gl] * nwv[:, gl]
            dnw_acc[:, gl] += jnp.sum(dyv[:, gl] * yn, axis=0, keepdims=True)
            dys_buf[:, gl] = r * (dyn - yn * jnp.mean(dyn * yn, axis=-1, keepdims=True))
        dgated = dys_buf[...]
        dz_ref[...] = (dgated * ysv * (sz * (1.0 + zv * (1.0 - sz)))).astype(BF16)
        dys_all = dgated * silz
        dys_buf[...] = dys_all
        dd_acc[...] += jnp.sum(dys_all * xs_ref[...], axis=0, keepdims=True)

        tri, dt, a_head, cs, cst = _chunk_decays(dtr_ref, bias_ref, alog_ref)
        lane = lax.broadcasted_iota(jnp.int32, (1, LANES), 1)
        first = lane < HEAD_DIM
        dcs_h = jnp.zeros((q, LANES), F32)
        for g in range(GROUPS):
            gl = slice(g * STATE, (g + 1) * STATE)
            bb = bm_ref[:, gl].astype(BF16)
            cb = cm_ref[:, gl].astype(BF16)
            scores = _dot_nt(cb, bb)
            dscores = jnp.zeros((q, q), F32)
            dbg = jnp.zeros((q, STATE), F32)
            dcg = jnp.zeros((q, STATE), F32)
            for p in range(2):
                pr = 2 * g + p
                h0 = 2 * pr
                sl = slice(pr * LANES, (pr + 1) * LANES)
                xv = xs_ref[:, sl]
                dyp = dys_buf[:, sl]
                dtp = jnp.where(first, dt[:, h0:h0 + 1], dt[:, h0 + 1:h0 + 2])
                csp = jnp.where(first, cs[:, h0:h0 + 1], cs[:, h0 + 1:h0 + 2])
                xd = xv * dtp
                xdb = xd.astype(BF16)
                hp = st_ref[0, pr]
                dhn = dstate[pr]
                hpb = hp.astype(BF16)
                dhnb = dhn.astype(BF16)
                lam = jnp.exp(csp)
                last = csp[q - 1:q, :]
                gam = jnp.exp(last)
                w = jnp.exp(last - csp)
                dxd = jnp.zeros((q, LANES), F32)
                for hh, keep in ((h0, first), (h0 + 1, jnp.logical_not(first))):
                    decay = jnp.where(tri, jnp.exp(cs[:, hh:hh + 1] - cst[hh:hh + 1, :]), 0.0)
                    m = scores * decay
                    dym = jnp.where(keep, dyp, 0.0).astype(BF16)
                    dm = _dot_nt(dym, xdb)
                    dxd = dxd + _dot_tn(m.astype(BF16), dym)
                    e = dm * m
                    dcs_h = dcs_h + jnp.where(lane == hh, jnp.sum(e, axis=1, keepdims=True), 0.0)
                    dcst[hh:hh + 1, :] = jnp.sum(e, axis=0, keepdims=True)
                    dscores = dscores + dm * decay
                yoff = lam * _dot(cb, hpb)
                ldy = (lam * dyp).astype(BF16)
                dcg = dcg + _dot_nt(ldy, hpb)
                dstate[pr] = gam * dhn + _dot_tn(cb, ldy)
                bdh = _dot(bb, dhnb)
                dxd = dxd + w * bdh
                xdw = xd * w
                dbg = dbg + _dot_nt(xdw.astype(BF16), dhnb)
                wd = xdw * bdh
                dcsl[:, sl] = dyp * yoff - wd
                dcsl[q - 1:q, sl] += (jnp.sum(wd, axis=0, keepdims=True)
                                      + gam * jnp.sum(dhn * hp, axis=0, keepdims=True))
                dx_ref[:, sl] = dxd * dtp + dyp * dl_ref[:, sl]
                ddtl[:, sl] = dxd * xv
            dsb = dscores.astype(BF16)
            dx_ref[:, D_SSD + g * STATE:D_SSD + (g + 1) * STATE] = dbg + _dot_tn(dsb, cb)
            dx_ref[:, D_SSD + (GROUPS + g) * STATE:D_SSD + (GROUPS + g + 1) * STATE] = dcg + _dot(dsb, bb)

        sel = sel_ref[...]
        dcs_h = dcs_h + _dot_exact_right(dcsl[...], sel) - dcst[...].T
        ddt = _dot_exact_right(ddtl[...], sel)
        upper = lax.broadcasted_iota(jnp.int32, (q, q), 1) >= lax.broadcasted_iota(jnp.int32, (q, q), 0)
        da = _dot_exact_left(upper.astype(BF16), dcs_h)
        ddt = ddt + da * a_head
        da_acc[...] += jnp.sum(da * dt, axis=0, keepdims=True)
        ddtr = ddt * _sig(dtr_ref[...] + bias_ref[...])
        ddtr_ref[...] = ddtr
        dbias_acc[...] += jnp.sum(ddtr, axis=0, keepdims=True)

        @pl.when(jnp.logical_and(b == nb - 1, c == nc - 1))
        def _():
            small_ref[...] = jnp.zeros_like(small_ref)
            small_ref[0:1, :] = dnw_acc[...]
            small_ref[1:2, 0:LANES] = _dot_exact_right(jnp.broadcast_to(dd_acc[...], (8, D_SSD)), sel)[0:1, :]
            small_ref[2:3, 0:LANES] = dbias_acc[...]
            small_ref[3:4, 0:LANES] = da_acc[...] * a_head

    rowblk = lambda b, c: b * nc + (nc - 1 - c)
    blk = lambda w, col: pl.BlockSpec((q, w), lambda b, c: (rowblk(b, c), col))
    vec = lambda w: pl.BlockSpec((1, w), lambda b, c: (0, 0))
    return pl.pallas_call(
        body,
        grid=(nb, nc),
        in_specs=[DEP_SPEC, blk(D_SSD, 1), blk(D_SSD, Z_COL), blk(D_SSD, 0), blk(D_SSD, 0), blk(GROUPS * STATE, 2),
                  blk(GROUPS * STATE, 3), blk(LANES, 0),
                  pl.BlockSpec((1, PAIRS, STATE, LANES), lambda b, c: (rowblk(b, c), 0, 0, 0)),
                  vec(LANES), vec(LANES), vec(D_SSD), vec(D_SSD), pl.BlockSpec((D_SSD, LANES), lambda b, c: (0, 0))],
        out_specs=[blk(D_SSD, Z_COL), blk(D_XBC, 0), blk(LANES, 0), pl.BlockSpec((8, D_SSD), lambda b, c: (0, 0))],
        out_shape=[jax.ShapeDtypeStruct(dproj.shape, BF16), jax.ShapeDtypeStruct((t, D_XBC), F32),
                   jax.ShapeDtypeStruct((t, LANES), F32), jax.ShapeDtypeStruct((8, D_SSD), F32)],
        input_output_aliases={0: 0},
        scratch_shapes=[pltpu.VMEM((PAIRS, STATE, LANES), F32), pltpu.VMEM((q, D_SSD), F32),
                        pltpu.VMEM((q, D_SSD), F32), pltpu.VMEM((q, D_SSD), F32), pltpu.VMEM((LANES, q), F32),
                        pltpu.VMEM((1, D_SSD), F32), pltpu.VMEM((1, D_SSD), F32), pltpu.VMEM((1, LANES), F32),
                        pltpu.VMEM((1, LANES), F32)],
        compiler_params=_cparams(("arbitrary", "arbitrary")),
        name="ssd_bwd",
    )(dproj, duy, proj, ys, xs_all, xs_all, xs_all, dtr, states, dt_bias, a_log, d_lanes, norm_w, head_of_lane)


HBM_SPEC = pl.BlockSpec(memory_space=pltpu.HBM)
MESH_ID = pl.DeviceIdType.MESH


def _coords():
    return lax.axis_index("x"), lax.axis_index("y"), lax.axis_index("c")


def _chip_peer(xi, yi, ci, d):
    return (jnp.bitwise_xor(xi, d >> 1), jnp.bitwise_xor(yi, d & 1), ci)


def _remote(src, dst, send_sem, recv_sem, peer):
    return pltpu.make_async_remote_copy(src_ref=src, dst_ref=dst, send_sem=send_sem, recv_sem=recv_sem,
                                        device_id=peer, device_id_type=MESH_ID)


SEM_SPEC = pl.BlockSpec(memory_space=pltpu.SEMAPHORE)
ANY_SPEC = pl.BlockSpec(memory_space=pl.ANY)
EFFECT = pltpu.SideEffectType.DATAFLOW_SIDE_EFFECTING
COPIES = 3


def _half(ref, axis, which, lead=0):
    size = ref.shape[lead + axis] // 2
    part = pl.ds(which * size, size)
    idx = (slice(None),) * lead + ((part, slice(None)) if axis == 0 else (slice(None), part))
    return ref.at[idx]


def _halved_shape(shape, axis):
    lead = len(shape) - 2
    return tuple(d // 2 if i == lead + axis else d for i, d in enumerate(shape))


def _gather_plan(axis):
    def plan(xi, yi, ci, src, land):
        me = 2 * xi + yi
        out = []
        for d in (1, 2, 3):
            there = jnp.bitwise_xor(me, d)
            if axis is None:
                out.append((src, land.at[me], _chip_peer(xi, yi, ci, d), land.at[there]))
            else:
                out.append((_half(src, axis, ci), _half(land.at[me], axis, ci), _chip_peer(xi, yi, ci, d),
                            _half(land.at[there], axis, ci)))
        return out
    return plan


def _halves_plan(axis):
    def plan(xi, yi, ci, src, land):
        return [(_half(src, axis, 1 - ci, lead=1), land, (xi, yi, 1 - ci), land)]
    return plan


def _forward_plan(axis):
    def plan(xi, yi, ci, src, land):
        me = 2 * xi + yi
        out = []
        for d in (1, 2, 3):
            slot = land.at[jnp.bitwise_xor(me, d)]
            out.append((_half(slot, axis, ci), _half(slot, axis, ci), (xi, yi, 1 - ci), _half(slot, axis, 1 - ci)))
        return out
    return plan


def _sibling_plan(xi, yi, ci, src, land):
    return [(src, land, (xi, yi, 1 - ci), land)]


def _owners_plan(xi, yi, ci, src, land):
    me = 2 * xi + yi
    return [(src.at[jnp.bitwise_xor(me, d)], land.at[d - 1], _chip_peer(xi, yi, ci, d), land.at[d - 1])
            for d in (1, 2, 3)]


def _split_start(srcs, lands, plans, *, name, dep=None):
    n = len(srcs)
    deps = [] if dep is None else [dep]

    def body(*refs):
        src_refs, land_refs = refs[:n], refs[n:2 * n]
        outs = refs[2 * n + len(deps):]
        ssems, rsems = outs[:n], outs[n:2 * n]
        token = refs[-1]
        xi, yi, ci = _coords()
        for t in range(n):
            for k, (src, dst, peer, _) in enumerate(plans[t](xi, yi, ci, src_refs[t], land_refs[t])):
                _remote(src, dst, ssems[t].at[k], rsems[t].at[k], peer).start()
        token[...] = jnp.zeros_like(token)

    bufs = list(srcs) + list(lands)
    outs = pl.pallas_call(
        body,
        name=name,
        in_specs=[HBM_SPEC] * (2 * n) + [ANY_SPEC] * len(deps),
        out_specs=[SEM_SPEC] * (2 * n) + [HBM_SPEC] * (2 * n) + [pl.BlockSpec(memory_space=pltpu.VMEM)],
        out_shape=[pltpu.SemaphoreType.DMA((COPIES,))] * (2 * n) + [pltpu.HBM(a.shape, a.dtype) for a in bufs]
        + [jax.ShapeDtypeStruct((8, LANES), F32)],
        input_output_aliases={i: 2 * n + i for i in range(2 * n)},
        compiler_params=pltpu.CompilerParams(has_side_effects=EFFECT),
    )(*[pltpu.with_memory_space_constraint(a, pltpu.HBM) for a in bufs], *deps)
    return outs[:n], outs[n:2 * n], outs[2 * n:3 * n], outs[3 * n:4 * n], outs[-1]


def _split_wait(ssems, rsems, srcs, lands, plans, after, *, name):
    n = len(srcs)

    def body(*refs):
        src_refs, land_refs = refs[:n], refs[n:2 * n]
        ss, rs = refs[2 * n:3 * n], refs[3 * n:4 * n]
        xi, yi, ci = _coords()
        for t in range(n):
            for k, (src, _, peer, landed) in enumerate(plans[t](xi, yi, ci, src_refs[t], land_refs[t])):
                cp = _remote(src, landed, ss[t].at[k], rs[t].at[k], peer)
                cp.wait_send()
                cp.wait_recv()

    bufs = list(srcs) + list(lands)
    outs = pl.pallas_call(
        body,
        name=name,
        in_specs=[HBM_SPEC] * (2 * n) + [SEM_SPEC] * (2 * n) + [ANY_SPEC],
        out_specs=[HBM_SPEC] * (2 * n),
        out_shape=[pltpu.HBM(a.shape, a.dtype) for a in bufs],
        input_output_aliases={i: i for i in range(2 * n)},
        compiler_params=pltpu.CompilerParams(has_side_effects=EFFECT),
    )(*bufs, *ssems, *rsems, after)
    return outs[:n], outs[n:]


def _forward_halves(lands, axes, *, name):
    n = len(lands)

    def body(*refs):
        ins, outs = refs[:n], refs[n:2 * n]
        send_sems, recv_sems = refs[2 * n:]
        xi, yi, ci = _coords()
        me = 2 * xi + yi
        sibling = (xi, yi, 1 - ci)
        cps = []
        for t in range(n):
            for d in (1, 2, 3):
                slot = jnp.bitwise_xor(me, d)
                k = COPIES * t + d - 1
                cp = _remote(_half(ins[t].at[slot], axes[t], ci), _half(outs[t].at[slot], axes[t], ci),
                             send_sems.at[k], recv_sems.at[k], sibling)
                cp.start()
                cps.append(cp)
        for t in range(n):
            for d in (1, 2, 3):
                got = _half(outs[t].at[jnp.bitwise_xor(me, d)], axes[t], 1 - ci)
                k = COPIES * t + d - 1
                _remote(got, got, send_sems.at[k], recv_sems.at[k], sibling).wait_recv()
        for cp in cps:
            cp.wait_send()

    return pl.pallas_call(
        body,
        name=name,
        in_specs=[HBM_SPEC] * n,
        out_specs=[HBM_SPEC] * n,
        out_shape=[jax.ShapeDtypeStruct(a.shape, a.dtype) for a in lands],
        input_output_aliases={i: i for i in range(n)},
        scratch_shapes=[pltpu.SemaphoreType.DMA((COPIES * n,)), pltpu.SemaphoreType.DMA((COPIES * n,))],
    )(*lands)


def _swap_other_halves(gs, axes, *, name):
    n = len(gs)

    def body(*refs):
        ins, lands = refs[:n], refs[n:2 * n]
        send_sems, recv_sems = refs[2 * n:]
        xi, yi, ci = _coords()
        sibling = (xi, yi, 1 - ci)
        cps = []
        for t in range(n):
            cp = _remote(_half(ins[t], axes[t], 1 - ci, lead=1), lands[t], send_sems.at[t], recv_sems.at[t], sibling)
            cp.start()
            cps.append(cp)
        for cp in cps:
            cp.wait_recv()
        for cp in cps:
            cp.wait_send()

    return pl.pallas_call(
        body,
        in_specs=[HBM_SPEC] * n,
        out_specs=[HBM_SPEC] * n,
        out_shape=[jax.ShapeDtypeStruct(_halved_shape(g.shape, ax), g.dtype) for g, ax in zip(gs, axes)],
        scratch_shapes=[pltpu.SemaphoreType.DMA((n,)), pltpu.SemaphoreType.DMA((n,))],
        name=name,
    )(*gs)


def _row_tile(rows, cap=512, mult=16):
    best = mult
    for cand in range(mult, min(rows, cap) + 1, mult):
        if rows % cand == 0:
            best = cand
    assert rows % best == 0, rows
    return best


COL_TILE = 256


def _half_tiles(hr, hc, axis, cap=512, mult=16):
    if axis == 0:
        tr = _row_tile(hr, cap, mult)
        n = hr // tr
        return (tr, hc), n, lambda half, i: (half * n + i, 0)
    n = hc // COL_TILE
    return (hr, COL_TILE), n, lambda half, i: (0, half * n + i)


def _add_core_halves(g, land, where, axis):
    nslot, hr, hc = land.shape
    bshape, nr, idx = _half_tiles(hr, hc, axis)

    def body(where_ref, g_ref, l_ref, f_ref, b_ref):
        s = g_ref[...] + l_ref[...].astype(F32)
        b_ref[...] = s.astype(BF16)

        @pl.when(pl.program_id(1) == where_ref[1])
        def _():
            f_ref[...] = s

    blk = pl.BlockSpec((None,) + bshape, lambda i, s, w: (s,) + idx(0, i))
    mine = pl.BlockSpec((None,) + bshape, lambda i, s, w: (s,) + idx(w[0], i))
    return pl.pallas_call(
        body,
        grid_spec=pltpu.PrefetchScalarGridSpec(
            num_scalar_prefetch=1,
            grid=(nr, nslot),
            in_specs=[mine, blk],
            out_specs=[pl.BlockSpec(bshape, lambda i, s, w: idx(0, i)), blk],
        ),
        out_shape=[jax.ShapeDtypeStruct((hr, hc), F32), jax.ShapeDtypeStruct(land.shape, BF16)],
        compiler_params=_cparams(("parallel", "arbitrary")),
        name="add_core_halves",
    )(where, g, land)


def _add_chip_sums(pf, land, where, axis):
    hr, cols = pf.shape
    bshape, nr, idx = _half_tiles(hr, cols, axis)

    def body(where_ref, p_ref, l_ref, o_ref):
        acc = p_ref[...]
        for d in range(3):
            acc = acc + l_ref[d].astype(F32)
        o_ref[...] = acc

    return pl.pallas_call(
        body,
        grid_spec=pltpu.PrefetchScalarGridSpec(
            num_scalar_prefetch=1,
            grid=(nr,),
            in_specs=[pl.BlockSpec(bshape, lambda i, w: idx(0, i)),
                      pl.BlockSpec((3,) + bshape, lambda i, w: (0,) + idx(0, i))],
            out_specs=pl.BlockSpec(bshape, lambda i, w: idx(0, i)),
        ),
        out_shape=jax.ShapeDtypeStruct((hr, cols), F32),
        compiler_params=_cparams(("parallel",)),
        name="add_chip_sums",
    )(where, pf, land)


N_DEV = 8


def _all_reduce_small(part):
    r, w = part.shape

    def body(p_ref, o_ref, gath, send_sems, recv_sems):
        xi, yi, ci = _coords()
        me = 4 * xi + 2 * yi + ci
        gath[me] = p_ref[...]
        cps = []
        for d in range(1, N_DEV):
            peer = (jnp.bitwise_xor(xi, d >> 2), jnp.bitwise_xor(yi, (d >> 1) & 1), jnp.bitwise_xor(ci, d & 1))
            cp = _remote(p_ref, gath.at[me], send_sems.at[d - 1], recv_sems.at[d - 1], peer)
            cp.start()
            cps.append(cp)
        for d in range(1, N_DEV):
            src = gath.at[jnp.bitwise_xor(me, d)]
            _remote(src, src, send_sems.at[d - 1], recv_sems.at[d - 1], (xi, yi, ci)).wait_recv()
        acc = gath[0]
        for k in range(1, N_DEV):
            acc = acc + gath[k]
        o_ref[...] = acc
        for cp in cps:
            cp.wait_send()

    vm = pl.BlockSpec(memory_space=pltpu.VMEM)
    return pl.pallas_call(
        body,
        in_specs=[vm],
        out_specs=vm,
        out_shape=jax.ShapeDtypeStruct((r, w), F32),
        scratch_shapes=[pltpu.VMEM((N_DEV, r, w), F32), pltpu.SemaphoreType.DMA((N_DEV - 1,)),
                        pltpu.SemaphoreType.DMA((N_DEV - 1,))],
        name="all_reduce_small",
    )(part)


def _adamw_math(wv, gv, mv, vv):
    mn = ADAM_B1 * mv + (1.0 - ADAM_B1) * gv
    vn = ADAM_B2 * vv + (1.0 - ADAM_B2) * (gv * gv)
    m_hat = mn / (1.0 - ADAM_B1 ** ADAM_STEP)
    v_hat = vn / (1.0 - ADAM_B2 ** ADAM_STEP)
    return -ADAM_LR * (m_hat / (jnp.sqrt(v_hat) + ADAM_EPS) + ADAM_WD * wv), mn, vn


def _adamw_halves(w, g_mine, g_other, m, v, where, axis, *, name):
    rows, cols = w.shape
    hr, hc = g_mine.shape
    bshape, nr, idx = _half_tiles(hr, hc, axis, cap=256, mult=8)

    def body(where_ref, w_ref, gm_ref, go_ref, m_ref, v_ref, g_ref, d_ref, nm_ref, nv_ref):
        is_mine = pl.program_id(0) == where_ref[0]
        gv = jnp.where(is_mine, gm_ref[...], go_ref[...])
        g_ref[...] = gv
        d_ref[...], nm_ref[...], nv_ref[...] = _adamw_math(w_ref[...], gv, m_ref[...], v_ref[...])

    def parked(half, i, holder):
        return idx(0, jnp.where(half == holder, i, jnp.where(half < holder, 0, nr - 1)))

    blk = pl.BlockSpec(bshape, lambda hf, i, wh: idx(hf, i))
    o = jax.ShapeDtypeStruct((rows, cols), F32)
    return pl.pallas_call(
        body,
        grid_spec=pltpu.PrefetchScalarGridSpec(
            num_scalar_prefetch=1,
            grid=(2, nr),
            in_specs=[blk, pl.BlockSpec(bshape, lambda hf, i, wh: parked(hf, i, wh[0])),
                      pl.BlockSpec(bshape, lambda hf, i, wh: parked(hf, i, 1 - wh[0])), blk, blk],
            out_specs=[blk] * 4,
        ),
        out_shape=[o, o, o, o],
        compiler_params=_cparams(("arbitrary", "arbitrary")),
        name=name,
    )(where, w, g_mine, g_other, m, v)


def _adamw(w, g, m, v, *, name):
    rows, cols = w.shape
    tr = _row_tile(rows, cap=256, mult=8)

    def body(w_ref, g_ref, m_ref, v_ref, d_ref, nm_ref, nv_ref):
        d_ref[...], nm_ref[...], nv_ref[...] = _adamw_math(w_ref[...], g_ref[...], m_ref[...], v_ref[...])

    blk = pl.BlockSpec((tr, cols), lambda i: (i, 0))
    o = jax.ShapeDtypeStruct((rows, cols), F32)
    return pl.pallas_call(
        body,
        grid=(rows // tr,),
        in_specs=[blk] * 4,
        out_specs=[blk] * 3,
        out_shape=[o, o, o],
        compiler_params=_cparams(("parallel",)),
        name=name,
    )(w, g, m, v)


def _pack(arrs):
    flat = jnp.concatenate([a.reshape(-1) for a in arrs])
    pad = (-flat.shape[0]) % (8 * LANES)
    return jnp.pad(flat, (0, pad)).reshape(-1, LANES)


def _unpack(packed, shapes):
    flat = packed.reshape(-1)
    out, off = [], 0
    for s in shapes:
        n = 1
        for dim in s:
            n *= dim
        out.append(flat[off:off + n].reshape(s))
        off += n
    return out


def _pad_rows(a, rows):
    return jnp.pad(a, ((0, rows - a.shape[0]), (0, 0)))


def _pad_lanes(a, lanes=LANES):
    return jnp.pad(a, ((0, 0), (0, lanes - a.shape[1])))


def _local_grads(x2d, tgt2d, prm, get_w, on_grads, *, nb, seq, dep=None):
    g_pre, g_post, g_fpre, g_fpost = prm["norm_mix_pre"], prm["norm_mix_post"], prm["norm_ffn_pre"], prm["norm_ffn_post"]
    dt_bias, a_log = _pad_lanes(prm["ssd_dt_bias"]), _pad_lanes(prm["ssd_a_log"])
    d_lanes = jnp.repeat(prm["ssd_d"], HEAD_DIM, axis=1)

    h = _rms_fwd(x2d, g_pre, dep=dep, out_dtype=BF16, name="rms_mix_pre")
    t, d = x2d.shape
    w_in_t, w_dt_t, cw, sw = get_w("in", h)
    proj = _matmul([(h, w_in_t)], mode="nt", out_dtype=F32, tm=1024, tn=1024, tk=2048, name="mm_proj",
                   extent=(t, D_MAIN, d))
    dtr = _matmul([(h, w_dt_t)], mode="nt", out_dtype=F32, tm=1024, tn=128, tk=2048, name="mm_dt")
    u1, uy = _conv_branch_fwd(proj, cw, prm["conv_dw_b"], prm["conv_ln_g"], prm["conv_ln_b"], nb=nb, seq=seq)
    xs_all = _ssd_pre_fwd(proj, sw, prm["ssd_conv_b"], nb=nb, seq=seq)
    uy, ys, states = _ssd_fwd(uy, xs_all, proj, dtr, dt_bias, a_log, d_lanes, prm["ssd_norm_w"], nb=nb, seq=seq)
    w_out = get_w("out", uy)
    mix = _matmul([(uy, w_out)], mode="nn", out_dtype=F32, tm=1024, tn=1024, tk=2048, name="mm_mix")
    x1, h2, h2_t = _rms_post_pre(mix, x2d, g_post, g_fpre)
    w_gate, w_up = get_w("up", h2)
    gt, up, act, act_t = _ffn_up(h2, w_gate, w_up, tm=1024, tn=512)
    w_down = get_w("down", act)
    dx2, df, loss, dg_fpost = _down_loss(act, w_down, x1, tgt2d, g_fpost, tk=1408)

    dgt, dup = _ffn_bwd_act(df, w_down, gt, up, tm=1024, tn=512)
    dw_down = _matmul([(act_t, df)], mode="nn", out_dtype=F32, tm=1408, tn=1024, tk=2048, name="mm_dw_down",
                      also_bf16=True)
    dw_gate = _matmul([(h2_t, dgt)], mode="nn", out_dtype=F32, tm=1024, tn=1408, tk=2048, name="mm_dw_gate",
                      slot_out=True, also_bf16=True)
    dw_up = _matmul([(h2_t, dup)], mode="nn", out_dtype=F32, tm=1024, tn=1408, tk=2048, name="mm_dw_up",
                    slot_out=True, also_bf16=True)
    dep = on_grads("ffn", (dw_down, dw_gate, dw_up))
    dh2 = _matmul([(dgt, w_gate), (dup, w_up)], mode="nt", out_dtype=F32, tm=1024, tn=1024, tk=1408, name="mm_dh2",
                  dep=dep)
    dep = on_grads("ffn_sums", dh2)
    dx1, dmix, dg_fpre, dg_post = _rms_bwd_pre_post(dh2, x1, g_fpre, dx2, mix, g_post, dep=dep)
    dw_out = _matmul([(uy, dmix)], mode="tn", out_dtype=F32, tm=1024, tn=1024, tk=2048, name="mm_dw_out",
                     also_bf16=True)
    dep = on_grads("out", (dw_out,))
    duy = _matmul([(dmix, w_out)], mode="nt", out_dtype=F32, tm=1024, tn=1024, tk=2048, name="mm_duy", dep=dep)
    dproj, dcw, dcb, dlg, dlb = _conv_branch_bwd(duy, u1, proj, cw, prm["conv_ln_g"], prm["conv_ln_b"], nb=nb, seq=seq)
    dproj, dxs, ddtr, ssd_small = _ssd_bwd(dproj, duy, proj, ys, xs_all, dtr, states, dt_bias, a_log, d_lanes,
                                           prm["ssd_norm_w"], nb=nb, seq=seq)
    dproj, dsw, dsb = _ssd_pre_bwd(dproj, dxs, proj, sw, prm["ssd_conv_b"], nb=nb, seq=seq)
    ddtr_b = ddtr.astype(BF16)
    dw_in_t = _matmul([(dproj, h)], mode="tn", out_dtype=F32, tm=1024, tn=1024, tk=2048, name="mm_dw_main",
                      extent=(D_MAIN, d, t), out_rows=D_IN)
    dw_in_t = _dw_dt_rows(dw_in_t, ddtr_b, h)
    dep = on_grads("in", ((dw_in_t, dw_in_t),))
    dx, dg_pre = _dh_dx(dproj, w_in_t, ddtr_b, w_dt_t, x2d, dx1, g_pre, tk=1280, dep=g_pre if dep is None else dep)

    grads = {
        "norm_mix_pre": dg_pre,
        "w_in": dw_in_t,
        "conv_dw_w": dcw[:CONV_K], "conv_dw_b": dcb, "conv_ln_g": dlg, "conv_ln_b": dlb,
        "ssd_conv_w": dsw[:SSD_CONV_K], "ssd_conv_b": dsb,
        "ssd_dt_bias": ssd_small[2:3, :HEADS], "ssd_a_log": ssd_small[3:4, :HEADS], "ssd_d": ssd_small[1:2, :HEADS],
        "ssd_norm_w": ssd_small[0:1],
        "w_out": dw_out[0],
        "norm_mix_post": dg_post, "norm_ffn_pre": dg_fpre,
        "w_gate": dw_gate[0], "w_up": dw_up[0],
        "w_down": dw_down[0], "norm_ffn_post": dg_fpost,
    }
    return loss, dx, grads


BIG = ("w_in", "w_out", "w_gate", "w_up", "w_down")
HALF_AXIS = {"w_in": 1, "w_out": 0, "w_gate": 0, "w_up": 0, "w_down": 0}
GATHER_STAGES = {"in": ("w_in", "conv_dw_w", "ssd_conv_w"), "out": ("w_out",), "up": ("w_gate", "w_up"),
                 "down": ("w_down",)}
GATHER_ORDER = tuple(n for st in ("in", "out", "up", "down") for n in GATHER_STAGES[st])
SMALL = ("norm_mix_pre", "conv_dw_w", "conv_dw_b", "conv_ln_g", "conv_ln_b", "ssd_conv_w", "ssd_conv_b", "ssd_dt_bias",
         "ssd_a_log", "ssd_d", "ssd_norm_w", "norm_mix_post", "norm_ffn_pre", "norm_ffn_post")
WEIGHTS = ("norm_mix_pre", "w_in", "conv_dw_w", "conv_dw_b", "conv_ln_g", "conv_ln_b", "ssd_conv_w", "ssd_conv_b",
           "ssd_dt_bias", "ssd_a_log", "ssd_d", "ssd_norm_w", "w_out", "norm_mix_post", "norm_ffn_pre", "w_gate", "w_up",
           "w_down", "norm_ffn_post")


def _cols_from_slots(a):
    n, rows, w = a.shape
    return a.transpose(1, 0, 2).reshape(rows, n * w)


def kernel(x, norm_mix_pre, w_in, conv_dw_w, conv_dw_b, conv_ln_g, conv_ln_b, ssd_conv_w, ssd_conv_b, ssd_dt_bias, ssd_a_log, ssd_d, ssd_norm_w, w_out, norm_mix_post, norm_ffn_pre, w_gate, w_up, w_down, norm_ffn_post, loss_target, m_norm_mix_pre, m_w_in, m_conv_dw_w, m_conv_dw_b, m_conv_ln_g, m_conv_ln_b, m_ssd_conv_w, m_ssd_conv_b, m_ssd_dt_bias, m_ssd_a_log, m_ssd_d, m_ssd_norm_w, m_w_out, m_norm_mix_post, m_norm_ffn_pre, m_w_gate, m_w_up, m_w_down, m_norm_ffn_post, v_norm_mix_pre, v_w_in, v_conv_dw_w, v_conv_dw_b, v_conv_ln_g, v_conv_ln_b, v_ssd_conv_w, v_ssd_conv_b, v_ssd_dt_bias, v_ssd_a_log, v_ssd_d, v_ssd_norm_w, v_w_out, v_norm_mix_post, v_norm_ffn_pre, v_w_gate, v_w_up, v_w_down, v_norm_ffn_post):
    args = dict(locals())
    two_d = lambda n, a: jnp.swapaxes(a, 1, 2)[0] if n == "w_in" else a.reshape(a.shape[-2:])
    wts = {n: two_d(n, args[n]) for n in WEIGHTS}
    ms = {n: two_d(n, args["m_" + n]) for n in WEIGHTS}
    vs = {n: two_d(n, args["v_" + n]) for n in WEIGHTS}
    nb, seq, d = x.shape
    t = nb * seq
    xi, yi, ci = _coords()
    chip = 2 * xi + yi
    where = jnp.stack([ci, chip]).astype(jnp.int32)

    shards = {n: wts[n].astype(BF16) for n in BIG}
    shards.update(conv_dw_w=_pad_rows(wts["conv_dw_w"], 32), ssd_conv_w=_pad_rows(wts["ssd_conv_w"], 8))
    plans = [_gather_plan(HALF_AXIS.get(n)) for n in GATHER_ORDER]
    n_first = len(GATHER_STAGES["in"])

    def start(part, name, dep=None):
        src = [shards[n] for n in GATHER_ORDER[part]]
        return _split_start(src, [lax.empty((N_CHIPS,) + s.shape, s.dtype) for s in src], plans[part], name=name,
                            dep=dep)

    head = start(slice(0, n_first), "gather_start_in")
    rest = start(slice(n_first, None), "gather_start_rest", dep=head[4])
    ssems, rsems, srcs, lands = (list(h) + list(r) for h, r in zip(head[:4], rest[:4]))
    token = rest[4]

    forwarding = {}

    def arrived(stage, after):
        names = GATHER_STAGES[stage]
        pick = lambda seq_: [seq_[GATHER_ORDER.index(n)] for n in names]
        own, got = _split_wait(pick(ssems), pick(rsems), pick(srcs), pick(lands), pick(plans), after,
                               name="gather_wait_" + stage)
        return dict(zip(names, own)), dict(zip(names, got))

    def get_w(stage, after):
        names = GATHER_STAGES[stage]
        if stage in forwarding:
            own, sems = forwarding.pop(stage)
            fwd_own, fwd_got = _split_wait(*sems[:4], [_forward_plan(HALF_AXIS[n]) for n in names], after,
                                           name="gather_forward_wait_" + stage)
            got = dict(zip(names, fwd_got))
        else:
            own, got = arrived(stage, after)
            big = [n for n in names if n in BIG]
            got.update(zip(big, _forward_halves([got[n] for n in big], [HALF_AXIS[n] for n in big],
                                                name="gather_forward_" + stage)))
        if stage == "out":
            nxt = GATHER_STAGES["up"]
            up_own, up_got = arrived("up", after)
            forwarding["up"] = (up_own, _split_start([up_own[n] for n in nxt], [up_got[n] for n in nxt],
                                                     [_forward_plan(HALF_AXIS[n]) for n in nxt],
                                                     name="gather_forward_start_up"))
        full = {n: lax.dynamic_update_slice(got[n], own[n][None], (chip, 0, 0)) for n in got}
        if stage == "in":
            w_in_t = full["w_in"].reshape(D_IN, D_MODEL)
            return (w_in_t, _pad_rows(w_in_t[D_MAIN:], LANES), _cols_from_slots(full["conv_dw_w"]),
                    _cols_from_slots(full["ssd_conv_w"]))
        if stage == "out":
            return full["w_out"].reshape(D_MODEL, D_MODEL)
        if stage == "up":
            return _cols_from_slots(full["w_gate"]), _cols_from_slots(full["w_up"])
        return full["w_down"].reshape(D_FF, D_MODEL)

    reduce_groups = {"ffn": ("w_down", "w_gate", "w_up"), "out": ("w_out",), "in": ("w_in",)}
    in_flight = {}

    swapping = {}

    def send_to_owners(stage, f32s, kept, axes):
        sums = [_add_core_halves(f32, l, where, ax) for f32, l, ax in zip(f32s, kept, axes)]
        ps = [s[1] for s in sums]
        ssem, rsem, ps, recv, started = _split_start(
            ps, [lax.empty((COPIES,) + p.shape[1:], p.dtype) for p in ps], [_owners_plan] * len(ps),
            name="owners_start_" + stage)
        in_flight[stage] = (ssem, rsem, ps, recv, [s[0] for s in sums])
        return started

    def on_grads(stage, gs):
        if stage == "ffn_sums":
            ssem, rsem, b16s, kept, f32s, axes = swapping.pop("ffn")
            _, kept = _split_wait(ssem, rsem, b16s, kept, [_halves_plan(ax) for ax in axes], gs,
                                  name="swap_other_halves_wait_ffn")
            return send_to_owners("ffn", f32s, kept, axes)
        names = reduce_groups[stage]
        axes = [HALF_AXIS[n] for n in names]
        slot = lambda g: g if g.ndim == 3 else g.reshape((N_CHIPS, g.shape[0] // N_CHIPS, g.shape[1]))
        f32s, b16s = [slot(f32) for f32, _ in gs], [slot(b16) for _, b16 in gs]
        if stage == "ffn":
            kept = [lax.empty(_halved_shape(b.shape, ax), b.dtype) for b, ax in zip(b16s, axes)]
            ssem, rsem, b16s, kept, started = _split_start(b16s, kept, [_halves_plan(ax) for ax in axes],
                                                           name="swap_other_halves_start_ffn")
            swapping["ffn"] = (ssem, rsem, b16s, kept, f32s, axes)
            return started
        return send_to_owners(stage, f32s, _swap_other_halves(b16s, axes, name="swap_other_halves_" + stage), axes)

    prm = {n: wts[n] for n in SMALL}
    loss, dx, grads = _local_grads(x.reshape(t, d), loss_target.reshape(t, d), prm, get_w, on_grads,
                                   nb=nb, seq=seq, dep=token)
    loss = lax.psum(loss[0, 0], MESH_AXES)

    def reduced(stage, after):
        ssem, rsem, ps, recv, own_sums = in_flight[stage]
        _, recv = _split_wait(ssem, rsem, ps, recv, [_owners_plan] * len(ps), after, name="owners_wait_" + stage)
        return {n: _add_chip_sums(f32_sum, r, where, HALF_AXIS[n])
                for n, f32_sum, r in zip(reduce_groups[stage], own_sums, recv)}

    def swap_start(names, name, dep=None):
        mine = [halves[n] for n in names]
        return _split_start(mine, [lax.empty(h.shape, h.dtype) for h in mine], [_sibling_plan] * len(mine), name=name,
                            dep=dep)

    halves = reduced("ffn", dx)
    ffn_swap = swap_start(reduce_groups["ffn"], "swap_reduced_start_ffn")
    halves.update(reduced("out", ffn_swap[4]))
    halves.update(reduced("in", ffn_swap[4]))
    mix_names = reduce_groups["out"] + reduce_groups["in"]
    mix_swap = swap_start(mix_names, "swap_reduced_start_mix", dep=ffn_swap[4])

    small_shapes = [grads[n].shape for n in SMALL]
    small_sum = _unpack(_all_reduce_small(_pack([grads[n] for n in SMALL])), small_shapes)
    small_grads = dict(zip(SMALL, small_sum))
    cwid, swid = D_CONV // N_CHIPS, D_XBC // N_CHIPS
    small_grads["conv_dw_w"] = lax.dynamic_slice(small_grads["conv_dw_w"], (0, chip * cwid), (CONV_K, cwid))
    small_grads["ssd_conv_w"] = lax.dynamic_slice(small_grads["ssd_conv_w"], (0, chip * swid), (SSD_CONV_K, swid))

    out_g, out_d, out_m, out_v = {}, {}, {}, {}
    shard_shapes = [wts[n].shape for n in SMALL]
    pd, pm, pv = _adamw(_pack([wts[n] for n in SMALL]), _pack([small_grads[n] for n in SMALL]),
                        _pack([ms[n] for n in SMALL]), _pack([vs[n] for n in SMALL]), name="adamw_small")
    for n, dd, mm, vv in zip(SMALL, _unpack(pd, shard_shapes), _unpack(pm, shard_shapes), _unpack(pv, shard_shapes)):
        out_g[n], out_d[n], out_m[n], out_v[n] = small_grads[n], dd, mm, vv

    def big_adamw(names, swap, after, name):
        mine, other = _split_wait(*swap[:4], [_sibling_plan] * len(names), after, name=name)
        for n, gm, go in zip(names, mine, other):
            out_g[n], out_d[n], out_m[n], out_v[n] = _adamw_halves(wts[n], gm, go, ms[n], vs[n], where, HALF_AXIS[n],
                                                                     name="adamw_" + n)

    big_adamw(reduce_groups["ffn"], ffn_swap, pd, "swap_reduced_wait_ffn")
    big_adamw(mix_names, mix_swap, out_d[reduce_groups["ffn"][-1]], "swap_reduced_wait_mix")

    back = lambda n, a: jnp.swapaxes(a[None], 1, 2) if n == "w_in" else a.reshape(args[n].shape)
    outs = [back(n, o[n]) for o in (out_g, out_d, out_m, out_v) for n in WEIGHTS]
    return (loss, dx.reshape(nb, seq, d), *outs)
```
